```python
import jax, jax.numpy as jnp
from jax import lax
import numpy as np

D_MODEL = 1024
BATCH = 2
SEQ = 8192
DEPTH = 1
DEC_BATCH = 32
DEC_SEQ = 64
PAST_LEN = 4096

CHUNK = 64
D_CONV = 1024
CONV_W = 3
N_HEADS = 16
N_KV_HEADS = 4
HEAD_DIM = 64
GQA_GROUP = N_HEADS // N_KV_HEADS
WINDOW = 128
WINDOW_CHUNKS = WINDOW // CHUNK
ROPE_THETA = 10000.0
ATTN_SCALE = HEAD_DIM ** -0.5
N_EXPERTS = 64
TOP_K = 8
N_EXPERT_GROUPS = 8
TOPK_GROUPS = 4
D_EXPERT = 256
D_SHARED = 256
ROUTED_SCALE = 2.5
DISPATCH_BLOCK = 128
EPS = 1e-6
Q_DIM = N_HEADS * HEAD_DIM
KV_DIM = N_KV_HEADS * HEAD_DIM
SPLITS = (D_CONV, D_CONV, D_CONV, Q_DIM, KV_DIM, KV_DIM, D_MODEL, D_MODEL)
D_IN = sum(SPLITS)

kernel_name = 'hybrid_stream_conv_swa_moe_step'


def rms_norm(x):
    xf = x.astype(jnp.float32)
    y = xf * lax.rsqrt(jnp.mean(xf * xf, axis=-1, keepdims=True) + EPS)
    return y.astype(x.dtype)


def rope(x, pos):
    half = HEAD_DIM // 2
    inv_freq = ROPE_THETA ** (-jnp.arange(half, dtype=jnp.float32) / half)
    ang = pos.astype(jnp.float32)[:, None] * inv_freq[None, :]
    cos = jnp.cos(ang)[:, None, :]
    sin = jnp.sin(ang)[:, None, :]
    xf = x.astype(jnp.float32)
    x1, x2 = xf[..., :half], xf[..., half:]
    return jnp.concatenate([x1 * cos - x2 * sin, x2 * cos + x1 * sin], axis=-1).astype(x.dtype)


def sink_softmax(scores, sinks):
    s = sinks.astype(jnp.float32).reshape(N_KV_HEADS, GQA_GROUP, 1, 1)
    m = jnp.maximum(jnp.max(scores, axis=-1, keepdims=True), s)
    e = jnp.exp(scores - m)
    z = jnp.sum(e, axis=-1, keepdims=True) + jnp.exp(s - m)
    return e / z


def band_attention(q, k, v, sinks):
    b, s = q.shape[:2]
    nc = s // CHUNK
    qc = q.reshape(b, nc, CHUNK, N_KV_HEADS, GQA_GROUP, HEAD_DIM)

    def band(t):
        tc = t.reshape(b, nc, CHUNK, N_KV_HEADS, HEAD_DIM)
        tp = jnp.pad(tc, ((0, 0), (WINDOW_CHUNKS, 0), (0, 0), (0, 0), (0, 0)))
        return jnp.concatenate([tp[:, j:j + nc] for j in range(WINDOW_CHUNKS + 1)], axis=2)

    kb, vb = band(k), band(v)
    band_len = (WINDOW_CHUNKS + 1) * CHUNK
    key_chunk = jnp.arange(nc)[:, None] + jnp.arange(band_len)[None, :] // CHUNK - WINDOW_CHUNKS
    valid = (key_chunk >= 0)[None, :, None, None, None, :]
    scores = jnp.einsum('bnqhgd,bnkhd->bnhgqk', qc, kb, preferred_element_type=jnp.float32) * ATTN_SCALE
    p = sink_softmax(jnp.where(valid, scores, -jnp.inf), sinks)
    o = jnp.einsum('bnhgqk,bnkhd->bnqhgd', p.astype(v.dtype), vb)
    return o.reshape(b, s, Q_DIM)


def cached_attention(q, k_all, v_all, sinks):
    b, t = q.shape[:2]
    qg = q.reshape(b, t, N_KV_HEADS, GQA_GROUP, HEAD_DIM)
    scores = jnp.einsum('bqhgd,bkhd->bhgqk', qg, k_all, preferred_element_type=jnp.float32) * ATTN_SCALE
    p = sink_softmax(scores, sinks)
    o = jnp.einsum('bhgqk,bkhd->bqhgd', p.astype(v_all.dtype), v_all)
    return o.reshape(b, t, Q_DIM)


def causal_conv(u_ext, w):
    t = u_ext.shape[1] - (CONV_W - 1)
    out = w[0] * u_ext[:, 0:t]
    for i in range(1, CONV_W):
        out = out + w[i] * u_ext[:, i:i + t]
    return out


def token_mixer(h, pos, conv_hist, k_hist, v_hist, w_in, w_conv, w_conv_out, w_attn_o, attn_sinks, w_mix_out):
    b, t = h.shape[:2]
    offsets = [int(o) for o in np.cumsum(SPLITS)[:-1]]
    gate_b, gate_c, x_conv, q, k, v, g_conv, g_attn = jnp.split(h @ w_in, offsets, axis=-1)
    u = gate_c * x_conv
    if conv_hist is None:
        u_ext = jnp.pad(u, ((0, 0), (CONV_W - 1, 0), (0, 0)))
    else:
        u_ext = jnp.concatenate([conv_hist.astype(u.dtype), u], axis=1)
    y_conv = (gate_b * causal_conv(u_ext, w_conv)) @ w_conv_out
    new_conv = u_ext[:, -(CONV_W - 1):]
    q = rope(q.reshape(b, t, N_HEADS, HEAD_DIM), pos)
    k = rope(k.reshape(b, t, N_KV_HEADS, HEAD_DIM), pos)
    v = v.reshape(b, t, N_KV_HEADS, HEAD_DIM)
    if k_hist is None:
        o = band_attention(q, k, v, attn_sinks)
        k_all, v_all = k, v
    else:
        k_all = jnp.concatenate([k_hist.astype(k.dtype), k], axis=1)
        v_all = jnp.concatenate([v_hist.astype(v.dtype), v], axis=1)
        o = cached_attention(q, k_all, v_all, attn_sinks)
    y_attn = o @ w_attn_o
    merged = jax.nn.sigmoid(g_conv) * y_conv + jax.nn.sigmoid(g_attn) * y_attn
    return merged @ w_mix_out, new_conv, k_all[:, -WINDOW:], v_all[:, -WINDOW:]


def swiglu(x, w_gate, w_up, w_down):
    return (jax.nn.silu(x @ w_gate) * (x @ w_up)) @ w_down


def route(h, w_router, router_bias):
    n = h.shape[0]
    scores = jax.nn.sigmoid(jnp.matmul(h, w_router, preferred_element_type=jnp.float32))
    biased = scores + router_bias.astype(jnp.float32)
    grouped = biased.reshape(n, N_EXPERT_GROUPS, N_EXPERTS // N_EXPERT_GROUPS)
    group_score = jnp.sum(lax.top_k(grouped, 2)[0], axis=-1)
    _, top_groups = lax.top_k(group_score, TOPK_GROUPS)
    group_mask = jnp.sum(jax.nn.one_hot(top_groups, N_EXPERT_GROUPS, dtype=jnp.float32), axis=-2) > 0
    expert_mask = jnp.repeat(group_mask, N_EXPERTS // N_EXPERT_GROUPS, axis=-1)
    _, top_idx = lax.top_k(jnp.where(expert_mask, biased, -jnp.inf), TOP_K)
    top_s = jnp.take_along_axis(scores, top_idx, axis=-1)
    top_w = top_s / jnp.sum(top_s, axis=-1, keepdims=True) * ROUTED_SCALE
    return top_idx.astype(jnp.int32), top_w


def routed_experts(h, top_idx, top_w, w_gate, w_up, w_down):
    n, d = h.shape
    a = n * TOP_K
    n_blocks = -(-a // DISPATCH_BLOCK) + N_EXPERTS
    flat_e = top_idx.reshape(a)
    order = jnp.argsort(flat_e).astype(jnp.int32)
    e_sorted = flat_e[order]
    counts = jnp.zeros((N_EXPERTS,), jnp.int32).at[flat_e].add(1)
    padded = (counts + DISPATCH_BLOCK - 1) // DISPATCH_BLOCK * DISPATCH_BLOCK
    pad_end = jnp.cumsum(padded)
    pad_start = pad_end - padded
    start = jnp.cumsum(counts) - counts
    dest = pad_start[e_sorted] + jnp.arange(a, dtype=jnp.int32) - start[e_sorted]
    slot = jnp.full((n_blocks * DISPATCH_BLOCK,), a, jnp.int32).at[dest].set(order)
    tok = jnp.append(jnp.arange(a, dtype=jnp.int32) // TOP_K, n)[slot]
    wt = jnp.append(top_w.reshape(a), 0.0)[slot]
    block_e = jnp.minimum(jnp.searchsorted(pad_end, jnp.arange(n_blocks) * DISPATCH_BLOCK, side='right'), N_EXPERTS - 1)
    h_pad = jnp.concatenate([h, jnp.zeros((1, d), h.dtype)], axis=0)

    def expert_block(args):
        e, tok_b, wt_b = args
        xb = h_pad[tok_b]
        return swiglu(xb, w_gate[e], w_up[e], w_down[e]) * wt_b[:, None].astype(h.dtype)

    yb = lax.map(expert_block, (block_e, tok.reshape(n_blocks, DISPATCH_BLOCK), wt.reshape(n_blocks, DISPATCH_BLOCK)))
    return jnp.zeros((n + 1, d), h.dtype).at[tok].add(yb.reshape(-1, d))[:n]


def layer(x, c, pos, conv_hist, k_hist, v_hist, w_ada, b_ada, w_in, w_conv, w_conv_out, w_attn_o,
          attn_sinks, w_mix_out, w_router, router_bias, w_exp_gate, w_exp_up, w_exp_down,
          w_sh_gate, w_sh_up, w_sh_down):
    mod = jax.nn.silu(c) @ w_ada + b_ada
    sh1, sc1, g1, sh2, sc2, g2 = jnp.split(mod[:, None, :], 6, axis=-1)
    h = rms_norm(x) * (1 + sc1) + sh1
    mix, new_conv, new_k, new_v = token_mixer(h, pos, conv_hist, k_hist, v_hist, w_in, w_conv,
                                              w_conv_out, w_attn_o, attn_sinks, w_mix_out)
    x = x + g1 * mix
    h = rms_norm(x) * (1 + sc2) + sh2
    b, t, d = h.shape
    hf = h.reshape(b * t, d)
    top_idx, top_w = route(hf, w_router, router_bias)
    ffn = swiglu(hf, w_sh_gate, w_sh_up, w_sh_down) + routed_experts(hf, top_idx, top_w, w_exp_gate, w_exp_up, w_exp_down)
    x = x + g2 * ffn.reshape(b, t, d)
    return x, new_conv, new_k, new_v


def setup_inputs(seed: int = 0) -> dict:
    key = jax.random.key(seed)
    ks = jax.random.split(key, 24)

    def nrm(k, shape, s):
        return jax.random.normal(k, shape, jnp.float32) * s

    L = DEPTH
    return {
        'x_prompt': nrm(ks[0], (BATCH, SEQ, D_MODEL), 1.0),
        'x_sample': nrm(ks[1], (DEC_BATCH, DEC_SEQ, D_MODEL), 1.0),
        'cache_k': nrm(ks[2], (L, DEC_BATCH, WINDOW, N_KV_HEADS, HEAD_DIM), 1.0),
        'cache_v': nrm(ks[3], (L, DEC_BATCH, WINDOW, N_KV_HEADS, HEAD_DIM), 1.0),
        'state_conv': nrm(ks[4], (L, DEC_BATCH, CONV_W - 1, D_CONV), 1.0),
        'c_prompt': nrm(ks[5], (BATCH, D_MODEL), 1.0),
        'c_sample': nrm(ks[6], (DEC_BATCH, D_MODEL), 1.0),
        'w_ada': nrm(ks[7], (L, D_MODEL, 6 * D_MODEL), 0.5 * D_MODEL ** -0.5),
        'b_ada': nrm(ks[8], (L, 6 * D_MODEL), 0.02),
        'w_in': nrm(ks[9], (L, D_MODEL, D_IN), D_MODEL ** -0.5),
        'w_conv': nrm(ks[10], (L, CONV_W, D_CONV), CONV_W ** -0.5),
        'w_conv_out': nrm(ks[11], (L, D_CONV, D_MODEL), D_CONV ** -0.5),
        'w_attn_o': nrm(ks[12], (L, Q_DIM, D_MODEL), Q_DIM ** -0.5),
        'attn_sinks': nrm(ks[13], (L, N_HEADS), 0.5),
        'w_mix_out': nrm(ks[14], (L, D_MODEL, D_MODEL), D_MODEL ** -0.5),
        'w_router': nrm(ks[15], (L, D_MODEL, N_EXPERTS), D_MODEL ** -0.5),
        'router_bias': nrm(ks[16], (L, N_EXPERTS), 0.01),
        'w_exp_gate': nrm(ks[17], (L, N_EXPERTS, D_MODEL, D_EXPERT), D_MODEL ** -0.5),
        'w_exp_up': nrm(ks[18], (L, N_EXPERTS, D_MODEL, D_EXPERT), D_MODEL ** -0.5),
        'w_exp_down': nrm(ks[19], (L, N_EXPERTS, D_EXPERT, D_MODEL), D_EXPERT ** -0.5),
        'w_sh_gate': nrm(ks[20], (L, D_MODEL, D_SHARED), D_MODEL ** -0.5),
        'w_sh_up': nrm(ks[21], (L, D_MODEL, D_SHARED), D_MODEL ** -0.5),
        'w_sh_down': nrm(ks[22], (L, D_SHARED, D_MODEL), D_SHARED ** -0.5),
        'final_gain': 1.0 + nrm(ks[23], (D_MODEL,), 0.02),
    }


def reference(x_prompt, x_sample, cache_k, cache_v, state_conv, c_prompt, c_sample,
              w_ada, b_ada, w_in, w_conv, w_conv_out, w_attn_o, attn_sinks, w_mix_out,
              w_router, router_bias, w_exp_gate, w_exp_up, w_exp_down,
              w_sh_gate, w_sh_up, w_sh_down, final_gain):
    pos_p = jnp.arange(x_prompt.shape[1], dtype=jnp.int32)
    pos_s = PAST_LEN + jnp.arange(x_sample.shape[1], dtype=jnp.int32)
    yp, ys = x_prompt, x_sample
    p_conv, p_k, p_v, s_conv, s_k, s_v = [], [], [], [], [], []
    for l in range(DEPTH):
        weights = (w_ada[l], b_ada[l], w_in[l], w_conv[l], w_conv_out[l], w_attn_o[l], attn_sinks[l],
                   w_mix_out[l], w_router[l], router_bias[l], w_exp_gate[l], w_exp_up[l], w_exp_down[l],
                   w_sh_gate[l], w_sh_up[l], w_sh_down[l])
        yp, pc, pk, pv = layer(yp, c_prompt, pos_p, None, None, None, *weights)
        ys, sc, sk, sv = layer(ys, c_sample, pos_s, state_conv[l], cache_k[l], cache_v[l], *weights)
        p_conv.append(pc)
        p_k.append(pk)
        p_v.append(pv)
        s_conv.append(sc)
        s_k.append(sk)
        s_v.append(sv)
    y_prompt = rms_norm(yp) * final_gain
    y_sample = rms_norm(ys) * final_gain
    return (y_prompt, y_sample, jnp.stack(p_conv), jnp.stack(p_k), jnp.stack(p_v),
            jnp.stack(s_conv), jnp.stack(s_k), jnp.stack(s_v))
```

```python
import functools

import jax
import jax.numpy as jnp
from jax import lax
from jax.experimental import pallas as pl
from jax.experimental.pallas import tpu as pltpu
from jax.experimental.pallas import tpu_sc as plsc

F32 = jnp.float32
BF16 = jnp.bfloat16
I32 = jnp.int32

D_MODEL = 1024
CHUNK = 64
D_CONV = 1024
CONV_W = 3
N_HEADS = 16
N_KV_HEADS = 4
HEAD_DIM = 64
GQA_GROUP = N_HEADS // N_KV_HEADS
WINDOW = 128
ROPE_THETA = 10000.0
ATTN_SCALE = HEAD_DIM ** -0.5
N_EXPERTS = 64
TOP_K = 8
N_EXPERT_GROUPS = 8
GROUP_SIZE = N_EXPERTS // N_EXPERT_GROUPS
TOPK_GROUPS = 4
D_EXPERT = 256
D_SHARED = 256
ROUTED_SCALE = 2.5
EPS = 1e-6
PAST_LEN = 4096
Q_DIM = N_HEADS * HEAD_DIM
KV_DIM = N_KV_HEADS * HEAD_DIM
OFF_GB, OFF_GC, OFF_XC, OFF_Q, OFF_K, OFF_V, OFF_GCONV, OFF_GATTN, D_IN = (
    0, 1024, 2048, 3072, 4096, 4352, 4608, 5632, 6656)

LANES = 128
SUBLANES = 8
VMEM_LIMIT = 56 * 1024 * 1024

MIX_TILE = 256
ATT_Q = 128
SAMPLE_BB = 4
PRE_TILE = 256
RANK_TILE = 512
GMM_BM = 256
COMB_TILE = 256
SC_WORKERS = 32
SC_WINDOW = 64


def _const_spec(shape):
    nd = len(shape)
    return pl.BlockSpec(shape, lambda *_: (0,) * nd, pipeline_mode=pl.Buffered(1))


def _rms(x):
    return x * lax.rsqrt(jnp.mean(x * x, axis=-1, keepdims=True) + EPS)


def _sigmoid(x):
    return 1.0 / (1.0 + jnp.exp(-x))


def _silu(x):
    return x * _sigmoid(x)


def _dot(a, b):
    return jnp.dot(a, b, preferred_element_type=F32)


def _ada_kernel(c_ref, w_ref, b_ref, o_ref):
    s = _silu(c_ref[...]).astype(BF16)
    o_ref[...] = _dot(s, w_ref[...].astype(BF16)) + b_ref[...]


def _ada(c_all, w_ada, b_ada):
    rows = c_all.shape[0]
    n_out = w_ada.shape[1]
    bn = 768
    return pl.pallas_call(
        _ada_kernel,
        grid=(n_out // bn,),
        in_specs=[pl.BlockSpec((rows, D_MODEL), lambda i: (0, 0)),
                  pl.BlockSpec((D_MODEL, bn), lambda i: (0, i)),
                  pl.BlockSpec((1, bn), lambda i: (0, i))],
        out_specs=pl.BlockSpec((rows, bn), lambda i: (0, i)),
        out_shape=jax.ShapeDtypeStruct((rows, n_out), F32),
        name="ada_mod",
    )(c_all, w_ada, b_ada.reshape(1, n_out))


def _rope(x, cos, sin_signed):
    lane = lax.broadcasted_iota(I32, (x.shape[0], LANES), 1)
    first_half = (lane % HEAD_DIM) < (HEAD_DIM // 2)
    outs = []
    for g in range(x.shape[1] // LANES):
        xg = x[:, g * LANES:(g + 1) * LANES]
        up = pltpu.roll(xg, LANES - HEAD_DIM // 2, axis=1)
        down = pltpu.roll(xg, HEAD_DIM // 2, axis=1)
        partner = jnp.where(first_half, up, down)
        outs.append(xg * cos + partner * sin_signed)
    return jnp.concatenate(outs, axis=1)


def _attend(qg, kh, vh, sink_col, mask):
    s = lax.dot_general(qg, kh, (((1,), (1,)), ((), ())), preferred_element_type=F32)
    if mask is not None:
        s = jnp.where(mask, s, -jnp.inf)
    m = jnp.maximum(jnp.max(s, axis=1, keepdims=True), sink_col)
    e = jnp.exp(s - m)
    z = jnp.sum(e, axis=1, keepdims=True) + jnp.exp(sink_col - m)
    return _dot(e.astype(BF16), vh) / z


def _sink_col(sinks_ref, kv_head, rows_per_head):
    parts = [jnp.full((rows_per_head, 1), sinks_ref[kv_head * GQA_GROUP + i], F32) for i in range(GQA_GROUP)]
    return jnp.concatenate(parts, axis=0)


def _stack_heads(q, kv_head):
    h0 = kv_head * GQA_GROUP
    return jnp.concatenate([q[:, (h0 + i) * HEAD_DIM:(h0 + i + 1) * HEAD_DIM] for i in range(GQA_GROUP)], axis=0)


def _in_proj(hb, win_ref, lo, hi):
    return _dot(hb, win_ref[:, lo:hi])


def _mix_out(x, g1, proj_gb, conv, y_attn_in, hb, win_ref, wco_ref, wao_ref, wmo_ref):
    y_conv = _dot((proj_gb * conv).astype(BF16), wco_ref[...])
    y_attn = _dot(y_attn_in.astype(BF16), wao_ref[...])
    g_conv = _in_proj(hb, win_ref, OFF_GCONV, OFF_GATTN)
    g_attn = _in_proj(hb, win_ref, OFF_GATTN, D_IN)
    merged = _sigmoid(g_conv) * y_conv + _sigmoid(g_attn) * y_attn
    return x + g1 * _dot(merged.astype(BF16), wmo_ref[...])


def _mixer_prompt_kernel(x_ref, mod_ref, cos_ref, sin_ref, win_ref, wconv_ref, wco_ref, wao_ref, sinks_ref, wmo_ref,
                         x1_ref, conv_ref, k_ref, v_ref, ubuf, kbuf, vbuf, obuf):
    j = pl.program_id(1)
    t = x_ref.shape[1]

    @pl.when(j == 0)
    def _():
        ubuf[0:SUBLANES, :] = jnp.zeros((SUBLANES, D_CONV), F32)
        kbuf[0:WINDOW, :] = jnp.zeros((WINDOW, KV_DIM), BF16)
        vbuf[0:WINDOW, :] = jnp.zeros((WINDOW, KV_DIM), BF16)

    x = x_ref[0]
    mod = mod_ref[0]
    sh1, sc1, g1 = mod[:, 0:D_MODEL], mod[:, D_MODEL:2 * D_MODEL], mod[:, 2 * D_MODEL:3 * D_MODEL]
    hb = (_rms(x) * (1.0 + sc1) + sh1).astype(BF16)

    u = _in_proj(hb, win_ref, OFF_GC, OFF_XC) * _in_proj(hb, win_ref, OFF_XC, OFF_Q)
    ubuf[SUBLANES:SUBLANES + t, :] = u
    wc = wconv_ref[...]
    conv = wc[0:1] * ubuf[SUBLANES - 2:SUBLANES - 2 + t, :] + wc[1:2] * ubuf[SUBLANES - 1:SUBLANES - 1 + t, :] + wc[2:3] * u
    conv_ref[0] = u[t - (CONV_W - 1):t]
    ubuf[SUBLANES - 2:SUBLANES, :] = u[t - (CONV_W - 1):t]

    cos, sin = cos_ref[...], sin_ref[...]
    q = (_rope(_in_proj(hb, win_ref, OFF_Q, OFF_K), cos, sin) * ATTN_SCALE).astype(BF16)
    k = _rope(_in_proj(hb, win_ref, OFF_K, OFF_V), cos, sin)
    v = _in_proj(hb, win_ref, OFF_V, OFF_GCONV)
    kbuf[WINDOW:WINDOW + t, :] = k.astype(BF16)
    vbuf[WINDOW:WINDOW + t, :] = v.astype(BF16)
    k_ref[0] = k[t - WINDOW:t]
    v_ref[0] = v[t - WINDOW:t]

    nkeys = ATT_Q + WINDOW
    rows = GQA_GROUP * ATT_Q
    qi = lax.broadcasted_iota(I32, (rows, nkeys), 0) % ATT_Q
    kj = lax.broadcasted_iota(I32, (rows, nkeys), 1)
    band = kj // CHUNK - qi // CHUNK
    band_ok = (band >= 0) & (band <= WINDOW // CHUNK)
    for s in range(t // ATT_Q):
        mask = band_ok & (kj + (j * t + s * ATT_Q - WINDOW) >= 0)
        qs = q[s * ATT_Q:(s + 1) * ATT_Q]
        for g in range(N_KV_HEADS):
            kh = kbuf[s * ATT_Q:s * ATT_Q + nkeys, g * HEAD_DIM:(g + 1) * HEAD_DIM]
            vh = vbuf[s * ATT_Q:s * ATT_Q + nkeys, g * HEAD_DIM:(g + 1) * HEAD_DIM]
            o = _attend(_stack_heads(qs, g), kh, vh, _sink_col(sinks_ref, g, ATT_Q), mask)
            for i in range(GQA_GROUP):
                h = g * GQA_GROUP + i
                obuf[s * ATT_Q:(s + 1) * ATT_Q, h * HEAD_DIM:(h + 1) * HEAD_DIM] = o[i * ATT_Q:(i + 1) * ATT_Q]
    kbuf[0:WINDOW, :] = kbuf[t:t + WINDOW, :]
    vbuf[0:WINDOW, :] = vbuf[t:t + WINDOW, :]

    gate_b = _in_proj(hb, win_ref, OFF_GB, OFF_GC)
    x1_ref[0] = _mix_out(x, g1, gate_b, conv, obuf[...], hb, win_ref, wco_ref, wao_ref, wmo_ref)


def _mixer_prompt(x, mod, cos, sin, win, wconv, wco, wao, sinks, wmo):
    b, seq, d = x.shape
    t = MIX_TILE
    return pl.pallas_call(
        _mixer_prompt_kernel,
        grid=(b, seq // t),
        in_specs=[pl.BlockSpec((1, t, d), lambda i, j: (i, j, 0)),
                  pl.BlockSpec((1, 1, 6 * d), lambda i, j: (i, 0, 0)),
                  pl.BlockSpec((t, LANES), lambda i, j: (j, 0)),
                  pl.BlockSpec((t, LANES), lambda i, j: (j, 0)),
                  _const_spec(win.shape), _const_spec(wconv.shape), _const_spec(wco.shape), _const_spec(wao.shape),
                  pl.BlockSpec(memory_space=pltpu.SMEM),
                  _const_spec(wmo.shape)],
        out_specs=[pl.BlockSpec((1, t, d), lambda i, j: (i, j, 0)),
                   pl.BlockSpec((1, CONV_W - 1, D_CONV), lambda i, j: (i, 0, 0)),
                   pl.BlockSpec((1, WINDOW, KV_DIM), lambda i, j: (i, 0, 0)),
                   pl.BlockSpec((1, WINDOW, KV_DIM), lambda i, j: (i, 0, 0))],
        out_shape=[jax.ShapeDtypeStruct((b, seq, d), F32),
                   jax.ShapeDtypeStruct((b, CONV_W - 1, D_CONV), F32),
                   jax.ShapeDtypeStruct((b, WINDOW, KV_DIM), F32),
                   jax.ShapeDtypeStruct((b, WINDOW, KV_DIM), F32)],
        scratch_shapes=[pltpu.VMEM((SUBLANES + t, D_CONV), F32),
                        pltpu.VMEM((WINDOW + t, KV_DIM), BF16),
                        pltpu.VMEM((WINDOW + t, KV_DIM), BF16),
                        pltpu.VMEM((t, Q_DIM), F32)],
        compiler_params=pltpu.CompilerParams(dimension_semantics=("arbitrary", "arbitrary"),
                                             vmem_limit_bytes=VMEM_LIMIT),
        name="mixer_prompt",
    )(x, mod, cos, sin, win, wconv, wco, wao, sinks, wmo)


def _mixer_sample_kernel(x_ref, mod_ref, cos_ref, sin_ref, ck_ref, cv_ref, sconv_ref, win_ref, wconv_ref, wco_ref,
                         wao_ref, sinks_ref, wmo_ref, x1_ref, conv_ref, k_ref, v_ref, ubuf, obuf):
    bb, t, d = x_ref.shape
    x3 = x_ref[...]
    mod = mod_ref[...]
    sh1, sc1, g1 = mod[:, :, 0:d], mod[:, :, d:2 * d], mod[:, :, 2 * d:3 * d]
    x = x3.reshape(bb * t, d)
    hb = (_rms(x3) * (1.0 + sc1) + sh1).astype(BF16).reshape(bb * t, d)

    u = _in_proj(hb, win_ref, OFF_GC, OFF_XC) * _in_proj(hb, win_ref, OFF_XC, OFF_Q)
    u3 = u.reshape(bb, t, D_CONV)
    ubuf[:, SUBLANES - 2:SUBLANES, :] = sconv_ref[...]
    ubuf[:, SUBLANES:SUBLANES + t, :] = u3
    wc = wconv_ref[...]
    conv = (wc[0:1] * ubuf[:, SUBLANES - 2:SUBLANES - 2 + t, :] + wc[1:2] * ubuf[:, SUBLANES - 1:SUBLANES - 1 + t, :]
            + wc[2:3] * u3).reshape(bb * t, D_CONV)
    conv_ref[...] = u3[:, t - (CONV_W - 1):t, :]

    cos = jnp.concatenate([cos_ref[...]] * bb, axis=0)
    sin = jnp.concatenate([sin_ref[...]] * bb, axis=0)
    q = (_rope(_in_proj(hb, win_ref, OFF_Q, OFF_K), cos, sin) * ATTN_SCALE).astype(BF16)
    k = _rope(_in_proj(hb, win_ref, OFF_K, OFF_V), cos, sin)
    v = _in_proj(hb, win_ref, OFF_V, OFF_GCONV)
    for b in range(bb):
        kb, vb = k[b * t:(b + 1) * t], v[b * t:(b + 1) * t]
        ck, cv = ck_ref[b], cv_ref[b]
        k_ref[b] = jnp.concatenate([ck[t:WINDOW], kb], axis=0)
        v_ref[b] = jnp.concatenate([cv[t:WINDOW], vb], axis=0)
        k_all = jnp.concatenate([ck, kb], axis=0).astype(BF16)
        v_all = jnp.concatenate([cv, vb], axis=0).astype(BF16)
        qs = q[b * t:(b + 1) * t]
        for g in range(N_KV_HEADS):
            kh = k_all[:, g * HEAD_DIM:(g + 1) * HEAD_DIM]
            vh = v_all[:, g * HEAD_DIM:(g + 1) * HEAD_DIM]
            o = _attend(_stack_heads(qs, g), kh, vh, _sink_col(sinks_ref, g, t), None)
            for i in range(GQA_GROUP):
                h = g * GQA_GROUP + i
                obuf[b * t:(b + 1) * t, h * HEAD_DIM:(h + 1) * HEAD_DIM] = o[i * t:(i + 1) * t]

    gate_b = _in_proj(hb, win_ref, OFF_GB, OFF_GC)
    g1f = jnp.broadcast_to(g1, (bb, t, d)).reshape(bb * t, d)
    x1_ref[...] = _mix_out(x, g1f, gate_b, conv, obuf[...], hb, win_ref, wco_ref, wao_ref, wmo_ref).reshape(bb, t, d)


def _mixer_sample(x, mod, cos, sin, ck, cv, sconv, win, wconv, wco, wao, sinks, wmo):
    b, t, d = x.shape
    bb = SAMPLE_BB
    blk = lambda *s: pl.BlockSpec((bb,) + s, lambda i: (i, 0, 0))
    return pl.pallas_call(
        _mixer_sample_kernel,
        grid=(b // bb,),
        in_specs=[blk(t, d), blk(1, 6 * d),
                  pl.BlockSpec((t, LANES), lambda i: (0, 0)), pl.BlockSpec((t, LANES), lambda i: (0, 0)),
                  blk(WINDOW, KV_DIM), blk(WINDOW, KV_DIM), blk(CONV_W - 1, D_CONV),
                  _const_spec(win.shape), _const_spec(wconv.shape), _const_spec(wco.shape), _const_spec(wao.shape),
                  pl.BlockSpec(memory_space=pltpu.SMEM),
                  _const_spec(wmo.shape)],
        out_specs=[blk(t, d), blk(CONV_W - 1, D_CONV), blk(WINDOW, KV_DIM), blk(WINDOW, KV_DIM)],
        out_shape=[jax.ShapeDtypeStruct((b, t, d), F32),
                   jax.ShapeDtypeStruct((b, CONV_W - 1, D_CONV), F32),
                   jax.ShapeDtypeStruct((b, WINDOW, KV_DIM), F32),
                   jax.ShapeDtypeStruct((b, WINDOW, KV_DIM), F32)],
        scratch_shapes=[pltpu.VMEM((bb, SUBLANES + t, D_CONV), F32),
                        pltpu.VMEM((bb * t, Q_DIM), F32)],
        compiler_params=pltpu.CompilerParams(dimension_semantics=("arbitrary",), vmem_limit_bytes=VMEM_LIMIT),
        name="mixer_sample",
    )(x, mod, cos, sin, ck, cv, sconv, win, wconv, wco, wao, sinks, wmo)


def _pre_kernel(x1_ref, mod_ref, wsg_ref, wsu_ref, wsd_ref, wr_ref, rb_ref, h2_ref, base_ref, cw_ref):
    nc, c, d = x1_ref.shape
    t = nc * c
    x3 = x1_ref[...]
    mod = mod_ref[...]
    sh2, sc2, g2 = mod[:, :, 0:d], mod[:, :, d:2 * d], mod[:, :, 2 * d:3 * d]
    h3 = _rms(x3) * (1.0 + sc2) + sh2
    h2 = h3.reshape(t, d)
    h2_ref[...] = h3
    hb = h2.astype(BF16)
    shared = _dot((_silu(_dot(hb, wsg_ref[...])) * _dot(hb, wsu_ref[...])).astype(BF16), wsd_ref[...])
    base_ref[...] = x3 + g2 * shared.reshape(nc, c, d)

    logits = lax.dot_general(wr_ref[...], h2, (((1,), (1,)), ((), ())), preferred_element_type=F32,
                             precision=lax.Precision.HIGHEST)
    scores = _sigmoid(logits)
    biased = scores + rb_ref[...]
    g3 = biased.reshape(N_EXPERT_GROUPS, GROUP_SIZE, t)
    member = lax.broadcasted_iota(I32, g3.shape, 1)
    m1 = jnp.max(g3, axis=1, keepdims=True)
    first = jnp.min(jnp.where(g3 == m1, member, GROUP_SIZE), axis=1, keepdims=True)
    m2 = jnp.max(jnp.where(member == first, -jnp.inf, g3), axis=1, keepdims=True)
    gs = m1 + m2
    gidx = lax.broadcasted_iota(I32, gs.shape, 0)
    grank = jnp.zeros(gs.shape, I32)
    for o in range(N_EXPERT_GROUPS):
        other = gs[o:o + 1]
        grank += ((other > gs) | ((other == gs) & (o < gidx))).astype(I32)
    eligible = jnp.broadcast_to(grank < TOPK_GROUPS, g3.shape).reshape(N_EXPERTS, t)
    mb = jnp.where(eligible, biased, -jnp.inf)
    eidx = lax.broadcasted_iota(I32, mb.shape, 0)
    erank = jnp.zeros(mb.shape, I32)
    for o in range(N_EXPERTS):
        other = mb[o:o + 1]
        erank += ((other > mb) | ((other == mb) & (o < eidx))).astype(I32)
    sel = eligible & (erank < TOP_K)
    ssum = jnp.sum(jnp.where(sel, scores, 0.0), axis=0, keepdims=True)
    cw_ref[...] = jnp.where(sel, scores / ssum * ROUTED_SCALE, -1.0)


def _pre(x1c, modc, wsg, wsu, wsd, wr_t, rb):
    nchunks, c, d = x1c.shape
    nc = PRE_TILE // c
    n = nchunks * c
    blk3 = pl.BlockSpec((nc, c, d), lambda i: (i, 0, 0))
    return pl.pallas_call(
        _pre_kernel,
        grid=(nchunks // nc,),
        in_specs=[blk3, pl.BlockSpec((nc, 1, 3 * d), lambda i: (i, 0, 0)),
                  _const_spec(wsg.shape), _const_spec(wsu.shape), _const_spec(wsd.shape),
                  _const_spec(wr_t.shape), _const_spec(rb.shape)],
        out_specs=[blk3, blk3, pl.BlockSpec((N_EXPERTS, nc * c), lambda i: (0, i))],
        out_shape=[jax.ShapeDtypeStruct((nchunks, c, d), F32),
                   jax.ShapeDtypeStruct((nchunks, c, d), F32),
                   jax.ShapeDtypeStruct((N_EXPERTS, n), F32)],
        compiler_params=pltpu.CompilerParams(dimension_semantics=("arbitrary",), vmem_limit_bytes=VMEM_LIMIT),
        name="pre_ffn",
    )(x1c, modc, wsg, wsu, wsd, wr_t, rb)


def _rank_kernel(cw_ref, rank_ref, cnt_ref, carry):
    i = pl.program_id(0)
    t = cw_ref.shape[1]

    @pl.when(i == 0)
    def _():
        carry[...] = jnp.zeros(carry.shape, F32)

    sel = (cw_ref[...] >= 0.0).astype(BF16)
    r = lax.broadcasted_iota(I32, (t, t), 0)
    c = lax.broadcasted_iota(I32, (t, t), 1)
    before = (r < c).astype(BF16)
    rank = carry[...] + _dot(sel, before)
    rank_ref[...] = rank.astype(I32)
    carry[...] = carry[...] + jnp.sum(sel.astype(F32), axis=1, keepdims=True)
    cnt_ref[...] = carry[...].astype(I32)


def _rank(cw):
    e, n = cw.shape
    t = RANK_TILE
    return pl.pallas_call(
        _rank_kernel,
        grid=(n // t,),
        in_specs=[pl.BlockSpec((e, t), lambda i: (0, i))],
        out_specs=[pl.BlockSpec((e, t), lambda i: (0, i)), pl.BlockSpec((e, 1), lambda i: (0, 0))],
        out_shape=[jax.ShapeDtypeStruct((e, n), I32), jax.ShapeDtypeStruct((e, 1), I32)],
        scratch_shapes=[pltpu.VMEM((e, 1), F32)],
        compiler_params=pltpu.CompilerParams(dimension_semantics=("arbitrary",)),
        name="expert_rank",
    )(cw)


def _slot_kernel(cw_ref, rank_ref, start_ref, pos_ref, w_ref):
    cw = cw_ref[...]
    e, t = cw.shape
    sel = cw >= 0.0
    r = lax.broadcasted_iota(I32, (e, e), 0)
    c = lax.broadcasted_iota(I32, (e, e), 1)
    lower = (c < r).astype(BF16)
    kidx = _dot(lower, sel.astype(BF16))
    posf = start_ref[...].astype(F32) + rank_ref[...].astype(F32)
    pos_rows, w_rows = [], []
    for k in range(TOP_K):
        m = sel & (kidx == float(k))
        pos_rows.append(jnp.sum(jnp.where(m, posf, 0.0), axis=0, keepdims=True))
        w_rows.append(jnp.sum(jnp.where(m, cw, 0.0), axis=0, keepdims=True))
    pos_ref[...] = jnp.concatenate(pos_rows, axis=0).astype(I32)
    w_ref[...] = jnp.concatenate(w_rows, axis=0)


def _slots(cw, rank, seg_start):
    e, n = cw.shape
    t = RANK_TILE
    return pl.pallas_call(
        _slot_kernel,
        grid=(n // t,),
        in_specs=[pl.BlockSpec((e, t), lambda i: (0, i)), pl.BlockSpec((e, t), lambda i: (0, i)),
                  pl.BlockSpec((e, 1), lambda i: (0, 0))],
        out_specs=[pl.BlockSpec((TOP_K, t), lambda i: (0, i)), pl.BlockSpec((TOP_K, t), lambda i: (0, i))],
        out_shape=[jax.ShapeDtypeStruct((TOP_K, n), I32), jax.ShapeDtypeStruct((TOP_K, n), F32)],
        compiler_params=pltpu.CompilerParams(dimension_semantics=("arbitrary",)),
        name="expert_slots",
    )(cw, rank, seg_start)


def _sc_mesh():
    return plsc.VectorSubcoreMesh(core_axis_name="c", subcore_axis_name="s")


def _sc_dispatch(h2, pos_flat, n_rows):
    n, d = h2.shape
    per_w = n // SC_WORKERS
    w = SC_WINDOW
    num_cores = SC_WORKERS // 16

    @functools.partial(
        pl.kernel, mesh=_sc_mesh(),
        out_type=jax.ShapeDtypeStruct((n_rows, d), h2.dtype),
        scratch_types=[pltpu.VMEM((w,), I32), pltpu.VMEM((w, d), h2.dtype)],
        name="sc_dispatch")
    def k(h_hbm, pos_hbm, o_hbm, idx_v, rows_v):
        wid = lax.axis_index("s") * num_cores + lax.axis_index("c")
        base = wid * per_w

        @pl.loop(0, per_w // w)
        def _(i):
            off = pl.multiple_of(base + i * w, w)
            pltpu.sync_copy(h_hbm.at[pl.ds(off, w)], rows_v)
            for kk in range(TOP_K):
                pltpu.sync_copy(pos_hbm.at[pl.ds(kk * n + off, w)], idx_v)
                pltpu.sync_copy(rows_v, o_hbm.at[idx_v])

    return k(h2, pos_flat)


def _sc_collect(y, pos_flat, n):
    d = y.shape[1]
    total = pos_flat.shape[0]
    per_w = total // SC_WORKERS
    w = SC_WINDOW
    num_cores = SC_WORKERS // 16

    @functools.partial(
        pl.kernel, mesh=_sc_mesh(),
        out_type=jax.ShapeDtypeStruct((total, d), y.dtype),
        scratch_types=[pltpu.VMEM((w,), I32), pltpu.VMEM((w, d), y.dtype)],
        name="sc_collect")
    def k(y_hbm, pos_hbm, o_hbm, idx_v, rows_v):
        wid = lax.axis_index("s") * num_cores + lax.axis_index("c")
        base = wid * per_w

        @pl.loop(0, per_w // w)
        def _(i):
            off = pl.multiple_of(base + i * w, w)
            pltpu.sync_copy(pos_hbm.at[pl.ds(off, w)], idx_v)
            pltpu.sync_copy(y_hbm.at[idx_v], rows_v)
            pltpu.sync_copy(rows_v, o_hbm.at[pl.ds(off, w)])

    return k(y, pos_flat)


def _gmm_kernel(be_ref, nu_ref, x_ref, wg_ref, wu_ref, wd_ref, y_ref, wgb, wub, wdb):
    b = pl.program_id(0)
    prev = be_ref[jnp.maximum(b - 1, 0)]

    @pl.when((b == 0) | (be_ref[b] != prev))
    def _():
        wgb[...] = wg_ref[0].astype(BF16)
        wub[...] = wu_ref[0].astype(BF16)
        wdb[...] = wd_ref[0].astype(BF16)

    @pl.when(b < nu_ref[0])
    def _():
        xb = x_ref[...].astype(BF16)
        mid = (_silu(_dot(xb, wgb[...])) * _dot(xb, wub[...])).astype(BF16)
        y_ref[...] = _dot(mid, wdb[...])


def _gmm(x_sorted, block_e, n_used, wg, wu, wd):
    r, d = x_sorted.shape
    bm = GMM_BM
    row_blk = pl.BlockSpec((bm, d), lambda b, be, nu: (jnp.minimum(b, nu[0] - 1), 0))
    return pl.pallas_call(
        _gmm_kernel,
        grid_spec=pltpu.PrefetchScalarGridSpec(
            num_scalar_prefetch=2,
            grid=(r // bm,),
            in_specs=[row_blk,
                      pl.BlockSpec((1, d, D_EXPERT), lambda b, be, nu: (be[b], 0, 0)),
                      pl.BlockSpec((1, d, D_EXPERT), lambda b, be, nu: (be[b], 0, 0)),
                      pl.BlockSpec((1, D_EXPERT, d), lambda b, be, nu: (be[b], 0, 0))],
            out_specs=row_blk,
            scratch_shapes=[pltpu.VMEM((d, D_EXPERT), BF16), pltpu.VMEM((d, D_EXPERT), BF16),
                            pltpu.VMEM((D_EXPERT, d), BF16)]),
        out_shape=jax.ShapeDtypeStruct((r, d), F32),
        compiler_params=pltpu.CompilerParams(dimension_semantics=("arbitrary",), vmem_limit_bytes=VMEM_LIMIT),
        name="expert_gmm",
    )(block_e, n_used, x_sorted, wg, wu, wd)


def _combine_kernel(base_ref, mod_ref, g_ref, w_ref, gain_ref, y_ref):
    nc, c, d = base_ref.shape
    t = nc * c
    w = w_ref[...]
    acc = w[:, 0:1] * g_ref[0]
    for k in range(1, TOP_K):
        acc += w[:, k:k + 1] * g_ref[k]
    g2 = mod_ref[...][:, :, 2 * d:3 * d]
    out = base_ref[...] + g2 * acc.reshape(nc, c, d)
    y_ref[...] = _rms(out) * gain_ref[...]


def _combine(base, modc, gathered, w_tok, gain):
    nchunks, c, d = base.shape
    nc = COMB_TILE // c
    t = nc * c
    blk3 = pl.BlockSpec((nc, c, d), lambda i: (i, 0, 0))
    return pl.pallas_call(
        _combine_kernel,
        grid=(nchunks // nc,),
        in_specs=[blk3, pl.BlockSpec((nc, 1, 3 * d), lambda i: (i, 0, 0)),
                  pl.BlockSpec((TOP_K, t, d), lambda i: (0, i, 0)),
                  pl.BlockSpec((t, TOP_K), lambda i: (i, 0)),
                  pl.BlockSpec((1, 1, d), lambda i: (0, 0, 0))],
        out_specs=blk3,
        out_shape=jax.ShapeDtypeStruct((nchunks, c, d), F32),
        compiler_params=pltpu.CompilerParams(dimension_semantics=("arbitrary",), vmem_limit_bytes=VMEM_LIMIT),
        name="combine_norm",
    )(base, modc, gathered, w_tok, gain.reshape(1, 1, d))


def _rope_tables(pos):
    half = HEAD_DIM // 2
    inv_freq = ROPE_THETA ** (-jnp.arange(half, dtype=F32) / half)
    ang = pos.astype(F32)[:, None] * inv_freq[None, :]
    cos, sin = jnp.cos(ang), jnp.sin(ang)
    reps = LANES // HEAD_DIM
    return jnp.tile(jnp.concatenate([cos, cos], axis=1), (1, reps)), jnp.tile(jnp.concatenate([-sin, sin], axis=1), (1, reps))


def _routed_ffn(h2, cw, w_gate, w_up, w_down):
    n, d = h2.shape
    rank, counts = _rank(cw)
    bm = GMM_BM
    padded = (counts[:, 0] + bm - 1) // bm * bm
    seg_end = jnp.cumsum(padded)
    seg_start = (seg_end - padded).astype(I32)
    n_rows = n * TOP_K + N_EXPERTS * bm
    n_blocks = n_rows // bm
    block_e = jnp.minimum(jnp.searchsorted(seg_end, jnp.arange(n_blocks, dtype=I32) * bm, side='right'),
                          N_EXPERTS - 1).astype(I32)
    n_used = (seg_end[-1:] // bm).astype(I32)
    pos, w_k = _slots(cw, rank, seg_start[:, None])
    pos_flat = pos.reshape(TOP_K * n)
    x_sorted = _sc_dispatch(h2, pos_flat, n_rows)
    y_sorted = _gmm(x_sorted, block_e, n_used, w_gate, w_up, w_down)
    gathered = _sc_collect(y_sorted, pos_flat, n).reshape(TOP_K, n, d)
    return gathered, w_k


def kernel(x_prompt, x_sample, cache_k, cache_v, state_conv, c_prompt, c_sample, w_ada, b_ada, w_in, w_conv,
           w_conv_out, w_attn_o, attn_sinks, w_mix_out, w_router, router_bias, w_exp_gate, w_exp_up, w_exp_down,
           w_sh_gate, w_sh_up, w_sh_down, final_gain):
    assert w_ada.shape[0] == 1, "one layer"
    bp, seq, d = x_prompt.shape
    bs, ts, _ = x_sample.shape
    assert ts == CHUNK and seq % MIX_TILE == 0 and bs % SAMPLE_BB == 0

    c_all = jnp.concatenate([c_prompt, c_sample], axis=0)
    pad = (-c_all.shape[0]) % SUBLANES
    mod = _ada(jnp.pad(c_all, ((0, pad), (0, 0))), w_ada[0], b_ada[0])[:bp + bs]
    mod_p, mod_s = mod[:bp, None, :], mod[bp:, None, :]

    win, wco, wao, wmo = (w.astype(BF16) for w in (w_in[0], w_conv_out[0], w_attn_o[0], w_mix_out[0]))
    cos_p, sin_p = _rope_tables(jnp.arange(seq, dtype=I32))
    cos_s, sin_s = _rope_tables(PAST_LEN + jnp.arange(ts, dtype=I32))

    x1_p, conv_p, k_p, v_p = _mixer_prompt(x_prompt, mod_p, cos_p, sin_p, win, w_conv[0], wco, wao, attn_sinks[0], wmo)
    x1_s, conv_s, k_s, v_s = _mixer_sample(
        x_sample, mod_s, cos_s, sin_s, cache_k[0].reshape(bs, WINDOW, KV_DIM), cache_v[0].reshape(bs, WINDOW, KV_DIM),
        state_conv[0], win, w_conv[0], wco, wao, attn_sinks[0], wmo)

    n_p, n_s = bp * seq, bs * ts
    n = n_p + n_s
    assert n % (SC_WORKERS * SC_WINDOW) == 0 and n % RANK_TILE == 0 and n % PRE_TILE == 0
    x1c = jnp.concatenate([x1_p.reshape(n_p // CHUNK, CHUNK, d), x1_s], axis=0)
    mod2 = mod[:, None, 3 * d:]
    modc = jnp.concatenate([jnp.repeat(mod2[:bp], seq // CHUNK, axis=0), mod2[bp:]], axis=0)

    h2, base, cw = _pre(x1c, modc, w_sh_gate[0].astype(BF16), w_sh_up[0].astype(BF16), w_sh_down[0].astype(BF16),
                        w_router[0].T, router_bias[0][:, None])
    gathered, w_k = _routed_ffn(h2.reshape(n, d), cw, w_exp_gate[0], w_exp_up[0], w_exp_down[0])
    y = _combine(base, modc, gathered, w_k.T, final_gain).reshape(n, d)

    kv = lambda a: a.reshape(1, a.shape[0], WINDOW, N_KV_HEADS, HEAD_DIM)
    return (y[:n_p].reshape(bp, seq, d), y[n_p:].reshape(bs, ts, d), conv_p[None], kv(k_p), kv(v_p),
            conv_s[None], kv(k_s), kv(v_s))
```

```python
import functools

import numpy as np
import jax
import jax.numpy as jnp
from jax import lax
from jax.experimental import pallas as pl
from jax.experimental.pallas import tpu as pltpu
from jax.experimental.pallas import tpu_sc as plsc

F32 = jnp.float32
BF16 = jnp.bfloat16
I32 = jnp.int32

D_MODEL = 1024
CHUNK = 64
D_CONV = 1024
CONV_W = 3
N_HEADS = 16
N_KV_HEADS = 4
HEAD_DIM = 64
GQA_GROUP = N_HEADS // N_KV_HEADS
WINDOW = 128
ROPE_THETA = 10000.0
ATTN_SCALE = HEAD_DIM ** -0.5
N_EXPERTS = 64
TOP_K = 8
N_EXPERT_GROUPS = 8
GROUP_SIZE = N_EXPERTS // N_EXPERT_GROUPS
TOPK_GROUPS = 4
D_EXPERT = 256
D_SHARED = 256
ROUTED_SCALE = 2.5
EPS = 1e-6
PAST_LEN = 4096
Q_DIM = N_HEADS * HEAD_DIM
KV_DIM = N_KV_HEADS * HEAD_DIM
OFF_GB, OFF_GC, OFF_XC, OFF_Q, OFF_K, OFF_V, OFF_GCONV, OFF_GATTN, D_IN = (
    0, 1024, 2048, 3072, 4096, 4352, 4608, 5632, 6656)

LANES = 128
SUBLANES = 8
VMEM_LIMIT = 56 * 1024 * 1024

MIX_TILE = 256
ATT_Q = 128
SAMPLE_BB = 4
PRE_TILE = 256
RANK_TILE = 512
GMM_BM = 256
COMB_TILE = 256
SC_WORKERS = 32
SC_WINDOW = 64


def _const_spec(shape):
    nd = len(shape)
    return pl.BlockSpec(shape, lambda *_: (0,) * nd, pipeline_mode=pl.Buffered(1))


def _rms(x):
    return x * lax.rsqrt(jnp.mean(x * x, axis=-1, keepdims=True) + EPS)


def _sigmoid(x):
    return 1.0 / (1.0 + jnp.exp(-x))


def _silu(x):
    return x * _sigmoid(x)


def _dot(a, b):
    return jnp.dot(a, b, preferred_element_type=F32)


def _ada_kernel(c_ref, w_ref, b_ref, o_ref):
    s = _silu(c_ref[...]).astype(BF16)
    o_ref[...] = _dot(s, w_ref[...].astype(BF16)) + b_ref[...]


def _ada(c_all, w_ada, b_ada):
    rows = c_all.shape[0]
    n_out = w_ada.shape[1]
    bn = 768
    return pl.pallas_call(
        _ada_kernel,
        grid=(n_out // bn,),
        in_specs=[pl.BlockSpec((rows, D_MODEL), lambda i: (0, 0)),
                  pl.BlockSpec((D_MODEL, bn), lambda i: (0, i)),
                  pl.BlockSpec((1, bn), lambda i: (0, i))],
        out_specs=pl.BlockSpec((rows, bn), lambda i: (0, i)),
        out_shape=jax.ShapeDtypeStruct((rows, n_out), F32),
        name="ada_mod",
    )(c_all, w_ada, b_ada.reshape(1, n_out))


def _rope(x, cos, sin_signed):
    lane = lax.broadcasted_iota(I32, (x.shape[0], LANES), 1)
    first_half = (lane % HEAD_DIM) < (HEAD_DIM // 2)
    outs = []
    for g in range(x.shape[1] // LANES):
        xg = x[:, g * LANES:(g + 1) * LANES]
        up = pltpu.roll(xg, LANES - HEAD_DIM // 2, axis=1)
        down = pltpu.roll(xg, HEAD_DIM // 2, axis=1)
        partner = jnp.where(first_half, up, down)
        outs.append(xg * cos + partner * sin_signed)
    return jnp.concatenate(outs, axis=1)


def _head_perm():
    n = np.arange(Q_DIM)
    c, par, dd = n // LANES, (n % LANES) // HEAD_DIM, n % HEAD_DIM
    pair, i = c // GQA_GROUP, c % GQA_GROUP
    return HEAD_DIM * (GQA_GROUP * (2 * pair + par) + i) + dd


def _attend_pair(q_blk, k_pair, vt_pair, sinks_ref, pair, mask, obuf, row0):
    rq = GQA_GROUP * ATT_Q
    low = lax.broadcasted_iota(I32, (ATT_Q, LANES), 1) < HEAD_DIM
    head_of_lane = lax.broadcasted_iota(I32, (1, rq), 1) // ATT_Q
    outs = []
    for par in range(2):
        g = 2 * pair + par
        keep = low if par == 0 else jnp.logical_not(low)
        cols = [q_blk[:, (GQA_GROUP * pair + i) * LANES:(GQA_GROUP * pair + i + 1) * LANES] for i in range(GQA_GROUP)]
        qg = jnp.concatenate([jnp.where(keep, c, jnp.zeros_like(c)) for c in cols], axis=0)
        st = lax.dot_general(k_pair, qg, (((1,), (1,)), ((), ())), preferred_element_type=F32)
        if mask is not None:
            st = jnp.where(mask, st, -jnp.inf)
        sink = jnp.full((1, rq), sinks_ref[g * GQA_GROUP + GQA_GROUP - 1], F32)
        for i in range(GQA_GROUP - 2, -1, -1):
            sink = jnp.where(head_of_lane == i, sinks_ref[g * GQA_GROUP + i], sink)
        m = jnp.maximum(jnp.max(st, axis=0, keepdims=True), sink)
        e = jnp.exp(st - m)
        z = jnp.sum(e, axis=0, keepdims=True) + jnp.exp(sink - m)
        ot = _dot(vt_pair[par * HEAD_DIM:(par + 1) * HEAD_DIM, :], e.astype(BF16))
        outs.append(ot / z)
    for i in range(GQA_GROUP):
        blk = jnp.concatenate([o[:, i * ATT_Q:(i + 1) * ATT_Q] for o in outs], axis=0)
        c0 = (GQA_GROUP * pair + i) * LANES
        obuf[row0:row0 + ATT_Q, c0:c0 + LANES] = blk.T


def _in_proj(hb, win_ref, lo, hi):
    return _dot(hb, win_ref[:, lo:hi])


def _mix_out(x, g1, proj_gb, conv, y_attn_in, hb, win_ref, wco_ref, wao_ref, wmo_ref):
    y_conv = _dot((proj_gb * conv).astype(BF16), wco_ref[...])
    y_attn = _dot(y_attn_in.astype(BF16), wao_ref[...])
    g_conv = _in_proj(hb, win_ref, OFF_GCONV, OFF_GATTN)
    g_attn = _in_proj(hb, win_ref, OFF_GATTN, D_IN)
    merged = _sigmoid(g_conv) * y_conv + _sigmoid(g_attn) * y_attn
    return x + g1 * _dot(merged.astype(BF16), wmo_ref[...])


def _mixer_prompt_kernel(x_ref, mod_ref, cos_ref, sin_ref, win_ref, wconv_ref, wco_ref, wao_ref, sinks_ref, wmo_ref,
                         x1_ref, conv_ref, k_ref, v_ref, ubuf, kbuf, vtbuf, obuf):
    j = pl.program_id(1)
    t = x_ref.shape[1]

    @pl.when(j == 0)
    def _():
        ubuf[0:SUBLANES, :] = jnp.zeros((SUBLANES, D_CONV), F32)
        kbuf[0:WINDOW, :] = jnp.zeros((WINDOW, KV_DIM), BF16)
        vtbuf[:, 0:WINDOW] = jnp.zeros((KV_DIM, WINDOW), BF16)

    x = x_ref[0]
    mod = mod_ref[0]
    sh1, sc1, g1 = mod[:, 0:D_MODEL], mod[:, D_MODEL:2 * D_MODEL], mod[:, 2 * D_MODEL:3 * D_MODEL]
    hb = (_rms(x) * (1.0 + sc1) + sh1).astype(BF16)

    u = _in_proj(hb, win_ref, OFF_GC, OFF_XC) * _in_proj(hb, win_ref, OFF_XC, OFF_Q)
    ubuf[SUBLANES:SUBLANES + t, :] = u
    wc = wconv_ref[...]
    conv = wc[0:1] * ubuf[SUBLANES - 2:SUBLANES - 2 + t, :] + wc[1:2] * ubuf[SUBLANES - 1:SUBLANES - 1 + t, :] + wc[2:3] * u
    conv_ref[0] = u[t - (CONV_W - 1):t]
    ubuf[SUBLANES - 2:SUBLANES, :] = u[t - (CONV_W - 1):t]

    cos, sin = cos_ref[...], sin_ref[...]
    q = (_rope(_in_proj(hb, win_ref, OFF_Q, OFF_K), cos, sin) * ATTN_SCALE).astype(BF16)
    k = _rope(_in_proj(hb, win_ref, OFF_K, OFF_V), cos, sin)
    v = _in_proj(hb, win_ref, OFF_V, OFF_GCONV)
    kbuf[WINDOW:WINDOW + t, :] = k.astype(BF16)
    vtbuf[:, WINDOW:WINDOW + t] = v.T.astype(BF16)
    k_ref[0] = k[t - WINDOW:t]
    v_ref[0] = v[t - WINDOW:t]

    nkeys = ATT_Q + WINDOW
    rq = GQA_GROUP * ATT_Q
    ki = lax.broadcasted_iota(I32, (nkeys, rq), 0)
    qi = lax.broadcasted_iota(I32, (nkeys, rq), 1) % ATT_Q
    band = ki // CHUNK - qi // CHUNK
    band_ok = (band >= 0) & (band <= WINDOW // CHUNK)
    for s in range(t // ATT_Q):
        mask = band_ok & (ki + (j * t + s * ATT_Q - WINDOW) >= 0)
        qs = q[s * ATT_Q:(s + 1) * ATT_Q]
        for pair in range(N_KV_HEADS // 2):
            k_pair = kbuf[s * ATT_Q:s * ATT_Q + nkeys, pair * LANES:(pair + 1) * LANES]
            vt_pair = vtbuf[pair * LANES:(pair + 1) * LANES, s * ATT_Q:s * ATT_Q + nkeys]
            _attend_pair(qs, k_pair, vt_pair, sinks_ref, pair, mask, obuf, s * ATT_Q)
    kbuf[0:WINDOW, :] = kbuf[t:t + WINDOW, :]
    vtbuf[:, 0:WINDOW] = vtbuf[:, t:t + WINDOW]

    gate_b = _in_proj(hb, win_ref, OFF_GB, OFF_GC)
    x1_ref[0] = _mix_out(x, g1, gate_b, conv, obuf[...], hb, win_ref, wco_ref, wao_ref, wmo_ref)


def _mixer_prompt(x, mod, cos, sin, win, wconv, wco, wao, sinks, wmo):
    b, seq, d = x.shape
    t = MIX_TILE
    return pl.pallas_call(
        _mixer_prompt_kernel,
        grid=(b, seq // t),
        in_specs=[pl.BlockSpec((1, t, d), lambda i, j: (i, j, 0)),
                  pl.BlockSpec((1, 1, 6 * d), lambda i, j: (i, 0, 0)),
                  pl.BlockSpec((t, LANES), lambda i, j: (j, 0)),
                  pl.BlockSpec((t, LANES), lambda i, j: (j, 0)),
                  _const_spec(win.shape), _const_spec(wconv.shape), _const_spec(wco.shape), _const_spec(wao.shape),
                  pl.BlockSpec(memory_space=pltpu.SMEM),
                  _const_spec(wmo.shape)],
        out_specs=[pl.BlockSpec((1, t, d), lambda i, j: (i, j, 0)),
                   pl.BlockSpec((1, CONV_W - 1, D_CONV), lambda i, j: (i, 0, 0)),
                   pl.BlockSpec((1, WINDOW, KV_DIM), lambda i, j: (i, 0, 0)),
                   pl.BlockSpec((1, WINDOW, KV_DIM), lambda i, j: (i, 0, 0))],
        out_shape=[jax.ShapeDtypeStruct((b, seq, d), F32),
                   jax.ShapeDtypeStruct((b, CONV_W - 1, D_CONV), F32),
                   jax.ShapeDtypeStruct((b, WINDOW, KV_DIM), F32),
                   jax.ShapeDtypeStruct((b, WINDOW, KV_DIM), F32)],
        scratch_shapes=[pltpu.VMEM((SUBLANES + t, D_CONV), F32),
                        pltpu.VMEM((WINDOW + t, KV_DIM), BF16),
                        pltpu.VMEM((KV_DIM, WINDOW + t), BF16),
                        pltpu.VMEM((t, Q_DIM), F32)],
        compiler_params=pltpu.CompilerParams(dimension_semantics=("arbitrary", "arbitrary"),
                                             vmem_limit_bytes=VMEM_LIMIT),
        name="mixer_prompt",
    )(x, mod, cos, sin, win, wconv, wco, wao, sinks, wmo)


def _mixer_sample_kernel(x_ref, mod_ref, cos_ref, sin_ref, ck_ref, cv_ref, sconv_ref, win_ref, wconv_ref, wco_ref,
                         wao_ref, sinks_ref, wmo_ref, x1_ref, conv_ref, k_ref, v_ref, ubuf, obuf):
    bb, t, d = x_ref.shape
    x3 = x_ref[...]
    mod = mod_ref[...]
    sh1, sc1, g1 = mod[:, :, 0:d], mod[:, :, d:2 * d], mod[:, :, 2 * d:3 * d]
    x = x3.reshape(bb * t, d)
    hb = (_rms(x3) * (1.0 + sc1) + sh1).astype(BF16).reshape(bb * t, d)

    u = _in_proj(hb, win_ref, OFF_GC, OFF_XC) * _in_proj(hb, win_ref, OFF_XC, OFF_Q)
    u3 = u.reshape(bb, t, D_CONV)
    ubuf[:, SUBLANES - 2:SUBLANES, :] = sconv_ref[...]
    ubuf[:, SUBLANES:SUBLANES + t, :] = u3
    wc = wconv_ref[...]
    conv = (wc[0:1] * ubuf[:, SUBLANES - 2:SUBLANES - 2 + t, :] + wc[1:2] * ubuf[:, SUBLANES - 1:SUBLANES - 1 + t, :]
            + wc[2:3] * u3).reshape(bb * t, D_CONV)
    conv_ref[...] = u3[:, t - (CONV_W - 1):t, :]

    cos = jnp.concatenate([cos_ref[...]] * bb, axis=0)
    sin = jnp.concatenate([sin_ref[...]] * bb, axis=0)
    q = (_rope(_in_proj(hb, win_ref, OFF_Q, OFF_K), cos, sin) * ATTN_SCALE).astype(BF16)
    k = _rope(_in_proj(hb, win_ref, OFF_K, OFF_V), cos, sin)
    v = _in_proj(hb, win_ref, OFF_V, OFF_GCONV)
    per = ATT_Q // t
    nkeys = per * (WINDOW + t)
    rq = GQA_GROUP * ATT_Q
    key_stream = lax.broadcasted_iota(I32, (nkeys, rq), 0) // (WINDOW + t)
    query_stream = (lax.broadcasted_iota(I32, (nkeys, rq), 1) % ATT_Q) // t
    mask = key_stream == query_stream
    for blk in range(bb // per):
        k_parts, v_parts = [], []
        for b in range(blk * per, (blk + 1) * per):
            kb, vb = k[b * t:(b + 1) * t], v[b * t:(b + 1) * t]
            ck, cv = ck_ref[b], cv_ref[b]
            k_ref[b] = jnp.concatenate([ck[t:WINDOW], kb], axis=0)
            v_ref[b] = jnp.concatenate([cv[t:WINDOW], vb], axis=0)
            k_parts += [ck, kb]
            v_parts += [cv, vb]
        k_all = jnp.concatenate(k_parts, axis=0).astype(BF16)
        vt_all = jnp.concatenate(v_parts, axis=0).T.astype(BF16)
        qs = q[blk * ATT_Q:(blk + 1) * ATT_Q]
        for pair in range(N_KV_HEADS // 2):
            _attend_pair(qs, k_all[:, pair * LANES:(pair + 1) * LANES], vt_all[pair * LANES:(pair + 1) * LANES, :],
                         sinks_ref, pair, mask, obuf, blk * ATT_Q)

    gate_b = _in_proj(hb, win_ref, OFF_GB, OFF_GC)
    g1f = jnp.broadcast_to(g1, (bb, t, d)).reshape(bb * t, d)
    x1_ref[...] = _mix_out(x, g1f, gate_b, conv, obuf[...], hb, win_ref, wco_ref, wao_ref, wmo_ref).reshape(bb, t, d)


def _mixer_sample(x, mod, cos, sin, ck, cv, sconv, win, wconv, wco, wao, sinks, wmo):
    b, t, d = x.shape
    bb = SAMPLE_BB
    blk = lambda *s: pl.BlockSpec((bb,) + s, lambda i: (i, 0, 0))
    return pl.pallas_call(
        _mixer_sample_kernel,
        grid=(b // bb,),
        in_specs=[blk(t, d), blk(1, 6 * d),
                  pl.BlockSpec((t, LANES), lambda i: (0, 0)), pl.BlockSpec((t, LANES), lambda i: (0, 0)),
                  blk(WINDOW, KV_DIM), blk(WINDOW, KV_DIM), blk(CONV_W - 1, D_CONV),
                  _const_spec(win.shape), _const_spec(wconv.shape), _const_spec(wco.shape), _const_spec(wao.shape),
                  pl.BlockSpec(memory_space=pltpu.SMEM),
                  _const_spec(wmo.shape)],
        out_specs=[blk(t, d), blk(CONV_W - 1, D_CONV), blk(WINDOW, KV_DIM), blk(WINDOW, KV_DIM)],
        out_shape=[jax.ShapeDtypeStruct((b, t, d), F32),
                   jax.ShapeDtypeStruct((b, CONV_W - 1, D_CONV), F32),
                   jax.ShapeDtypeStruct((b, WINDOW, KV_DIM), F32),
                   jax.ShapeDtypeStruct((b, WINDOW, KV_DIM), F32)],
        scratch_shapes=[pltpu.VMEM((bb, SUBLANES + t, D_CONV), F32),
                        pltpu.VMEM((bb * t, Q_DIM), F32)],
        compiler_params=pltpu.CompilerParams(dimension_semantics=("arbitrary",), vmem_limit_bytes=VMEM_LIMIT),
        name="mixer_sample",
    )(x, mod, cos, sin, ck, cv, sconv, win, wconv, wco, wao, sinks, wmo)


def _pre_kernel(x1_ref, mod_ref, wsg_ref, wsu_ref, wsd_ref, wr_ref, rb_ref, h2_ref, base_ref, cw_ref):
    nc, c, d = x1_ref.shape
    t = nc * c
    x3 = x1_ref[...]
    mod = mod_ref[...]
    sh2, sc2, g2 = mod[:, :, 0:d], mod[:, :, d:2 * d], mod[:, :, 2 * d:3 * d]
    h3 = _rms(x3) * (1.0 + sc2) + sh2
    h2 = h3.reshape(t, d)
    h2_ref[...] = h3
    hb = h2.astype(BF16)
    shared = _dot((_silu(_dot(hb, wsg_ref[...])) * _dot(hb, wsu_ref[...])).astype(BF16), wsd_ref[...])
    base_ref[...] = x3 + g2 * shared.reshape(nc, c, d)

    logits = lax.dot_general(wr_ref[...], h2, (((1,), (1,)), ((), ())), preferred_element_type=F32,
                             precision=lax.Precision.HIGHEST)
    scores = _sigmoid(logits)
    biased = scores + rb_ref[...]
    g3 = biased.reshape(N_EXPERT_GROUPS, GROUP_SIZE, t)
    member = lax.broadcasted_iota(I32, g3.shape, 1)
    m1 = jnp.max(g3, axis=1, keepdims=True)
    first = jnp.min(jnp.where(g3 == m1, member, GROUP_SIZE), axis=1, keepdims=True)
    m2 = jnp.max(jnp.where(member == first, -jnp.inf, g3), axis=1, keepdims=True)
    gs = m1 + m2
    gidx = lax.broadcasted_iota(I32, gs.shape, 0)
    grank = jnp.zeros(gs.shape, I32)
    for o in range(N_EXPERT_GROUPS):
        other = gs[o:o + 1]
        grank += ((other > gs) | ((other == gs) & (o < gidx))).astype(I32)
    eligible = jnp.broadcast_to(grank < TOPK_GROUPS, g3.shape).reshape(N_EXPERTS, t)
    mb = jnp.where(eligible, biased, -jnp.inf)
    eidx = lax.broadcasted_iota(I32, mb.shape, 0)
    erank = jnp.zeros(mb.shape, I32)
    for o in range(N_EXPERTS):
        other = mb[o:o + 1]
        erank += ((other > mb) | ((other == mb) & (o < eidx))).astype(I32)
    sel = eligible & (erank < TOP_K)
    ssum = jnp.sum(jnp.where(sel, scores, 0.0), axis=0, keepdims=True)
    cw_ref[...] = jnp.where(sel, scores / ssum * ROUTED_SCALE, -1.0)


def _pre(x1c, modc, wsg, wsu, wsd, wr_t, rb):
    nchunks, c, d = x1c.shape
    nc = PRE_TILE // c
    n = nchunks * c
    blk3 = pl.BlockSpec((nc, c, d), lambda i: (i, 0, 0))
    return pl.pallas_call(
        _pre_kernel,
        grid=(nchunks // nc,),
        in_specs=[blk3, pl.BlockSpec((nc, 1, 3 * d), lambda i: (i, 0, 0)),
                  _const_spec(wsg.shape), _const_spec(wsu.shape), _const_spec(wsd.shape),
                  _const_spec(wr_t.shape), _const_spec(rb.shape)],
        out_specs=[blk3, blk3, pl.BlockSpec((N_EXPERTS, nc * c), lambda i: (0, i))],
        out_shape=[jax.ShapeDtypeStruct((nchunks, c, d), F32),
                   jax.ShapeDtypeStruct((nchunks, c, d), F32),
                   jax.ShapeDtypeStruct((N_EXPERTS, n), F32)],
        compiler_params=pltpu.CompilerParams(dimension_semantics=("arbitrary",), vmem_limit_bytes=VMEM_LIMIT),
        name="pre_ffn",
    )(x1c, modc, wsg, wsu, wsd, wr_t, rb)


def _rank_kernel(cw_ref, rank_ref, cnt_ref, carry):
    i = pl.program_id(0)
    t = cw_ref.shape[1]

    @pl.when(i == 0)
    def _():
        carry[...] = jnp.zeros(carry.shape, F32)

    sel = (cw_ref[...] >= 0.0).astype(BF16)
    r = lax.broadcasted_iota(I32, (t, t), 0)
    c = lax.broadcasted_iota(I32, (t, t), 1)
    before = (r < c).astype(BF16)
    rank = carry[...] + _dot(sel, before)
    rank_ref[...] = rank.astype(I32)
    carry[...] = carry[...] + jnp.sum(sel.astype(F32), axis=1, keepdims=True)
    cnt_ref[...] = carry[...].astype(I32)


def _rank(cw):
    e, n = cw.shape
    t = RANK_TILE
    return pl.pallas_call(
        _rank_kernel,
        grid=(n // t,),
        in_specs=[pl.BlockSpec((e, t), lambda i: (0, i))],
        out_specs=[pl.BlockSpec((e, t), lambda i: (0, i)), pl.BlockSpec((e, 1), lambda i: (0, 0))],
        out_shape=[jax.ShapeDtypeStruct((e, n), I32), jax.ShapeDtypeStruct((e, 1), I32)],
        scratch_shapes=[pltpu.VMEM((e, 1), F32)],
        compiler_params=pltpu.CompilerParams(dimension_semantics=("arbitrary",)),
        name="expert_rank",
    )(cw)


def _slot_kernel(cw_ref, rank_ref, start_ref, pos_ref, w_ref):
    cw = cw_ref[...]
    e, t = cw.shape
    sel = cw >= 0.0
    r = lax.broadcasted_iota(I32, (e, e), 0)
    c = lax.broadcasted_iota(I32, (e, e), 1)
    lower = (c < r).astype(BF16)
    kidx = _dot(lower, sel.astype(BF16))
    posf = start_ref[...].astype(F32) + rank_ref[...].astype(F32)
    pos_rows, w_rows = [], []
    for k in range(TOP_K):
        m = sel & (kidx == float(k))
        pos_rows.append(jnp.sum(jnp.where(m, posf, 0.0), axis=0, keepdims=True))
        w_rows.append(jnp.sum(jnp.where(m, cw, 0.0), axis=0, keepdims=True))
    pos_ref[...] = jnp.concatenate(pos_rows, axis=0).astype(I32)
    w_ref[...] = jnp.concatenate(w_rows, axis=0)


def _slots(cw, rank, seg_start):
    e, n = cw.shape
    t = RANK_TILE
    return pl.pallas_call(
        _slot_kernel,
        grid=(n // t,),
        in_specs=[pl.BlockSpec((e, t), lambda i: (0, i)), pl.BlockSpec((e, t), lambda i: (0, i)),
                  pl.BlockSpec((e, 1), lambda i: (0, 0))],
        out_specs=[pl.BlockSpec((TOP_K, t), lambda i: (0, i)), pl.BlockSpec((TOP_K, t), lambda i: (0, i))],
        out_shape=[jax.ShapeDtypeStruct((TOP_K, n), I32), jax.ShapeDtypeStruct((TOP_K, n), F32)],
        compiler_params=pltpu.CompilerParams(dimension_semantics=("arbitrary",)),
        name="expert_slots",
    )(cw, rank, seg_start)


def _sc_mesh():
    return plsc.VectorSubcoreMesh(core_axis_name="c", subcore_axis_name="s")


def _sc_dispatch(h2, pos_flat, n_rows):
    n, d = h2.shape
    per_w = n // SC_WORKERS
    w = SC_WINDOW
    num_cores = SC_WORKERS // 16

    @functools.partial(
        pl.kernel, mesh=_sc_mesh(),
        out_type=jax.ShapeDtypeStruct((n_rows, d), h2.dtype),
        scratch_types=[pltpu.VMEM((w,), I32), pltpu.VMEM((w, d), h2.dtype)],
        name="sc_dispatch")
    def k(h_hbm, pos_hbm, o_hbm, idx_v, rows_v):
        wid = lax.axis_index("s") * num_cores + lax.axis_index("c")
        base = wid * per_w

        @pl.loop(0, per_w // w)
        def _(i):
            off = pl.multiple_of(base + i * w, w)
            pltpu.sync_copy(h_hbm.at[pl.ds(off, w)], rows_v)
            for kk in range(TOP_K):
                pltpu.sync_copy(pos_hbm.at[pl.ds(kk * n + off, w)], idx_v)
                pltpu.sync_copy(rows_v, o_hbm.at[idx_v])

    return k(h2, pos_flat)


def _sc_collect(y, pos_flat, n):
    d = y.shape[1]
    total = pos_flat.shape[0]
    per_w = total // SC_WORKERS
    w = SC_WINDOW
    num_cores = SC_WORKERS // 16

    @functools.partial(
        pl.kernel, mesh=_sc_mesh(),
        out_type=jax.ShapeDtypeStruct((total, d), y.dtype),
        scratch_types=[pltpu.VMEM((w,), I32), pltpu.VMEM((w, d), y.dtype)],
        name="sc_collect")
    def k(y_hbm, pos_hbm, o_hbm, idx_v, rows_v):
        wid = lax.axis_index("s") * num_cores + lax.axis_index("c")
        base = wid * per_w

        @pl.loop(0, per_w // w)
        def _(i):
            off = pl.multiple_of(base + i * w, w)
            pltpu.sync_copy(pos_hbm.at[pl.ds(off, w)], idx_v)
            pltpu.sync_copy(y_hbm.at[idx_v], rows_v)
            pltpu.sync_copy(rows_v, o_hbm.at[pl.ds(off, w)])

    return k(y, pos_flat)


def _gmm_kernel(be_ref, nu_ref, x_ref, wg_ref, wu_ref, wd_ref, y_ref, wgb, wub, wdb):
    b = pl.program_id(0)
    prev = be_ref[jnp.maximum(b - 1, 0)]

    @pl.when((b == 0) | (be_ref[b] != prev))
    def _():
        wgb[...] = wg_ref[0].astype(BF16)
        wub[...] = wu_ref[0].astype(BF16)
        wdb[...] = wd_ref[0].astype(BF16)

    @pl.when(b < nu_ref[0])
    def _():
        xb = x_ref[...].astype(BF16)
        mid = (_silu(_dot(xb, wgb[...])) * _dot(xb, wub[...])).astype(BF16)
        y_ref[...] = _dot(mid, wdb[...])


def _gmm(x_sorted, block_e, n_used, wg, wu, wd):
    r, d = x_sorted.shape
    bm = GMM_BM
    row_blk = pl.BlockSpec((bm, d), lambda b, be, nu: (jnp.minimum(b, nu[0] - 1), 0))
    return pl.pallas_call(
        _gmm_kernel,
        grid_spec=pltpu.PrefetchScalarGridSpec(
            num_scalar_prefetch=2,
            grid=(r // bm,),
            in_specs=[row_blk,
                      pl.BlockSpec((1, d, D_EXPERT), lambda b, be, nu: (be[b], 0, 0)),
                      pl.BlockSpec((1, d, D_EXPERT), lambda b, be, nu: (be[b], 0, 0)),
                      pl.BlockSpec((1, D_EXPERT, d), lambda b, be, nu: (be[b], 0, 0))],
            out_specs=row_blk,
            scratch_shapes=[pltpu.VMEM((d, D_EXPERT), BF16), pltpu.VMEM((d, D_EXPERT), BF16),
                            pltpu.VMEM((D_EXPERT, d), BF16)]),
        out_shape=jax.ShapeDtypeStruct((r, d), F32),
        compiler_params=pltpu.CompilerParams(dimension_semantics=("arbitrary",), vmem_limit_bytes=VMEM_LIMIT),
        name="expert_gmm",
    )(block_e, n_used, x_sorted, wg, wu, wd)


def _combine_kernel(base_ref, mod_ref, g_ref, w_ref, gain_ref, y_ref):
    nc, c, d = base_ref.shape
    t = nc * c
    w = w_ref[...]
    acc = w[:, 0:1] * g_ref[0]
    for k in range(1, TOP_K):
        acc += w[:, k:k + 1] * g_ref[k]
    g2 = mod_ref[...][:, :, 2 * d:3 * d]
    out = base_ref[...] + g2 * acc.reshape(nc, c, d)
    y_ref[...] = _rms(out) * gain_ref[...]


def _combine(base, modc, gathered, w_tok, gain):
    nchunks, c, d = base.shape
    nc = COMB_TILE // c
    t = nc * c
    blk3 = pl.BlockSpec((nc, c, d), lambda i: (i, 0, 0))
    return pl.pallas_call(
        _combine_kernel,
        grid=(nchunks // nc,),
        in_specs=[blk3, pl.BlockSpec((nc, 1, 3 * d), lambda i: (i, 0, 0)),
                  pl.BlockSpec((TOP_K, t, d), lambda i: (0, i, 0)),
                  pl.BlockSpec((t, TOP_K), lambda i: (i, 0)),
                  pl.BlockSpec((1, 1, d), lambda i: (0, 0, 0))],
        out_specs=blk3,
        out_shape=jax.ShapeDtypeStruct((nchunks, c, d), F32),
        compiler_params=pltpu.CompilerParams(dimension_semantics=("arbitrary",), vmem_limit_bytes=VMEM_LIMIT),
        name="combine_norm",
    )(base, modc, gathered, w_tok, gain.reshape(1, 1, d))


def _rope_tables(pos):
    half = HEAD_DIM // 2
    inv_freq = ROPE_THETA ** (-jnp.arange(half, dtype=F32) / half)
    ang = pos.astype(F32)[:, None] * inv_freq[None, :]
    cos, sin = jnp.cos(ang), jnp.sin(ang)
    reps = LANES // HEAD_DIM
    return jnp.tile(jnp.concatenate([cos, cos], axis=1), (1, reps)), jnp.tile(jnp.concatenate([-sin, sin], axis=1), (1, reps))


def _routed_ffn(h2, cw, w_gate, w_up, w_down):
    n, d = h2.shape
    rank, counts = _rank(cw)
    bm = GMM_BM
    padded = (counts[:, 0] + bm - 1) // bm * bm
    seg_end = jnp.cumsum(padded)
    seg_start = (seg_end - padded).astype(I32)
    n_rows = n * TOP_K + N_EXPERTS * bm
    n_blocks = n_rows // bm
    block_start = jnp.arange(n_blocks, dtype=I32) * bm
    block_e = jnp.minimum(jnp.sum((seg_end[None, :] <= block_start[:, None]).astype(I32), axis=1), N_EXPERTS - 1)
    n_used = (seg_end[-1:] // bm).astype(I32)
    pos, w_k = _slots(cw, rank, seg_start[:, None])
    pos_flat = pos.reshape(TOP_K * n)
    x_sorted = _sc_dispatch(h2, pos_flat, n_rows)
    y_sorted = _gmm(x_sorted, block_e, n_used, w_gate, w_up, w_down)
    gathered = _sc_collect(y_sorted, pos_flat, n).reshape(TOP_K, n, d)
    return gathered, w_k


def kernel(x_prompt, x_sample, cache_k, cache_v, state_conv, c_prompt, c_sample, w_ada, b_ada, w_in, w_conv,
           w_conv_out, w_attn_o, attn_sinks, w_mix_out, w_router, router_bias, w_exp_gate, w_exp_up, w_exp_down,
           w_sh_gate, w_sh_up, w_sh_down, final_gain):
    assert w_ada.shape[0] == 1, "one layer"
    bp, seq, d = x_prompt.shape
    bs, ts, _ = x_sample.shape
    assert ts == CHUNK and seq % MIX_TILE == 0 and bs % SAMPLE_BB == 0

    c_all = jnp.concatenate([c_prompt, c_sample], axis=0)
    pad = (-c_all.shape[0]) % SUBLANES
    mod = _ada(jnp.pad(c_all, ((0, pad), (0, 0))), w_ada[0], b_ada[0])[:bp + bs]
    mod_p, mod_s = mod[:bp, None, :], mod[bp:, None, :]

    perm = _head_perm()
    w_in_l = w_in[0]
    w_in_p = jnp.concatenate([w_in_l[:, :OFF_Q], w_in_l[:, OFF_Q:OFF_K][:, perm], w_in_l[:, OFF_K:]], axis=1)
    win, wco, wao, wmo = (w.astype(BF16) for w in (w_in_p, w_conv_out[0], w_attn_o[0][perm], w_mix_out[0]))
    cos_p, sin_p = _rope_tables(jnp.arange(seq, dtype=I32))
    cos_s, sin_s = _rope_tables(PAST_LEN + jnp.arange(ts, dtype=I32))

    x1_p, conv_p, k_p, v_p = _mixer_prompt(x_prompt, mod_p, cos_p, sin_p, win, w_conv[0], wco, wao, attn_sinks[0], wmo)
    x1_s, conv_s, k_s, v_s = _mixer_sample(
        x_sample, mod_s, cos_s, sin_s, cache_k[0].reshape(bs, WINDOW, KV_DIM), cache_v[0].reshape(bs, WINDOW, KV_DIM),
        state_conv[0], win, w_conv[0], wco, wao, attn_sinks[0], wmo)

    n_p, n_s = bp * seq, bs * ts
    n = n_p + n_s
    assert n % (SC_WORKERS * SC_WINDOW) == 0 and n % RANK_TILE == 0 and n % PRE_TILE == 0
    x1c = jnp.concatenate([x1_p.reshape(n_p // CHUNK, CHUNK, d), x1_s], axis=0)
    mod2 = mod[:, None, 3 * d:]
    modc = jnp.concatenate([jnp.repeat(mod2[:bp], seq // CHUNK, axis=0), mod2[bp:]], axis=0)

    h2, base, cw = _pre(x1c, modc, w_sh_gate[0].astype(BF16), w_sh_up[0].astype(BF16), w_sh_down[0].astype(BF16),
                        w_router[0].T, router_bias[0][:, None])
    gathered, w_k = _routed_ffn(h2.reshape(n, d), cw, w_exp_gate[0], w_exp_up[0], w_exp_down[0])
    y = _combine(base, modc, gathered, w_k.T, final_gain).reshape(n, d)

    kv = lambda a: a.reshape(1, a.shape[0], WINDOW, N_KV_HEADS, HEAD_DIM)
    return (y[:n_p].reshape(bp, seq, d), y[n_p:].reshape(bs, ts, d), conv_p[None], kv(k_p), kv(v_p),
            conv_s[None], kv(k_s), kv(v_s))
```

```python
import functools

import numpy as np
import jax
import jax.numpy as jnp
from jax import lax
from jax.experimental import pallas as pl
from jax.experimental.pallas import tpu as pltpu
from jax.experimental.pallas import tpu_sc as plsc

F32 = jnp.float32
BF16 = jnp.bfloat16
I32 = jnp.int32

D_MODEL = 1024
CHUNK = 64
D_CONV = 1024
CONV_W = 3
N_HEADS = 16
N_KV_HEADS = 4
HEAD_DIM = 64
GQA_GROUP = N_HEADS // N_KV_HEADS
WINDOW = 128
ROPE_THETA = 10000.0
ATTN_SCALE = HEAD_DIM ** -0.5
N_EXPERTS = 64
TOP_K = 8
N_EXPERT_GROUPS = 8
GROUP_SIZE = N_EXPERTS // N_EXPERT_GROUPS
TOPK_GROUPS = 4
D_EXPERT = 256
D_SHARED = 256
ROUTED_SCALE = 2.5
EPS = 1e-6
PAST_LEN = 4096
Q_DIM = N_HEADS * HEAD_DIM
KV_DIM = N_KV_HEADS * HEAD_DIM
OFF_GB, OFF_GC, OFF_XC, OFF_Q, OFF_K, OFF_V, OFF_GCONV, OFF_GATTN, D_IN = (
    0, 1024, 2048, 3072, 4096, 4352, 4608, 5632, 6656)

LANES = 128
SUBLANES = 8
VMEM_LIMIT = 56 * 1024 * 1024

MIX_TILE = 256
ATT_Q = 128
SAMPLE_BB = 4
PRE_TILE = 256
RANK_TILE = 512
GMM_BM = 256
COMB_TILE = 256
SC_WORKERS = 32
SC_WINDOW = 96


def _const_spec(shape):
    nd = len(shape)
    return pl.BlockSpec(shape, lambda *_: (0,) * nd, pipeline_mode=pl.Buffered(1))


def _rms(x):
    return x * lax.rsqrt(jnp.mean(x * x, axis=-1, keepdims=True) + EPS)


def _sigmoid(x):
    return 1.0 / (1.0 + jnp.exp(-x))


def _silu(x):
    return x * _sigmoid(x)


def _dot(a, b):
    return jnp.dot(a, b, preferred_element_type=F32)


def _pack_bf16_pairs(x):
    half = x.shape[-1] // 2
    lo = lax.bitcast_convert_type(x[..., :half].astype(BF16).astype(F32), I32)
    hi = lax.bitcast_convert_type(x[..., half:].astype(BF16).astype(F32), I32)
    return lax.shift_right_logical(lo, 16) | hi


def _unpack_bf16_pairs(words):
    lo = lax.bitcast_convert_type(lax.shift_left(words, 16), F32)
    hi = lax.bitcast_convert_type(words & jnp.int32(-65536), F32)
    return lo, hi


def _ada_kernel(c_ref, w_ref, b_ref, o_ref):
    s = _silu(c_ref[...]).astype(BF16)
    o_ref[...] = _dot(s, w_ref[...].astype(BF16)) + b_ref[...]


def _ada(c_all, w_ada, b_ada):
    rows = c_all.shape[0]
    n_out = w_ada.shape[1]
    bn = 768
    return pl.pallas_call(
        _ada_kernel,
        grid=(n_out // bn,),
        in_specs=[pl.BlockSpec((rows, D_MODEL), lambda i: (0, 0)),
                  pl.BlockSpec((D_MODEL, bn), lambda i: (0, i)),
                  pl.BlockSpec((1, bn), lambda i: (0, i))],
        out_specs=pl.BlockSpec((rows, bn), lambda i: (0, i)),
        out_shape=jax.ShapeDtypeStruct((rows, n_out), F32),
        name="ada_mod",
    )(c_all, w_ada, b_ada.reshape(1, n_out))


def _rope(x, cos, sin_signed):
    lane = lax.broadcasted_iota(I32, (x.shape[0], LANES), 1)
    first_half = (lane % HEAD_DIM) < (HEAD_DIM // 2)
    outs = []
    for g in range(x.shape[1] // LANES):
        xg = x[:, g * LANES:(g + 1) * LANES]
        up = pltpu.roll(xg, LANES - HEAD_DIM // 2, axis=1)
        down = pltpu.roll(xg, HEAD_DIM // 2, axis=1)
        partner = jnp.where(first_half, up, down)
        outs.append(xg * cos + partner * sin_signed)
    return jnp.concatenate(outs, axis=1)


def _head_perm():
    n = np.arange(Q_DIM)
    c, par, dd = n // LANES, (n % LANES) // HEAD_DIM, n % HEAD_DIM
    pair, i = c // GQA_GROUP, c % GQA_GROUP
    return HEAD_DIM * (GQA_GROUP * (2 * pair + par) + i) + dd


def _attend_pair(q_blk, k_pair, vt_pair, sinks_ref, pair, mask, obuf, row0):
    rq = GQA_GROUP * ATT_Q
    low = lax.broadcasted_iota(I32, (ATT_Q, LANES), 1) < HEAD_DIM
    head_of_lane = lax.broadcasted_iota(I32, (1, rq), 1) // ATT_Q
    outs = []
    for par in range(2):
        g = 2 * pair + par
        keep = low if par == 0 else jnp.logical_not(low)
        cols = [q_blk[:, (GQA_GROUP * pair + i) * LANES:(GQA_GROUP * pair + i + 1) * LANES] for i in range(GQA_GROUP)]
        qg = jnp.concatenate([jnp.where(keep, c, jnp.zeros_like(c)) for c in cols], axis=0)
        st = lax.dot_general(k_pair, qg, (((1,), (1,)), ((), ())), preferred_element_type=F32)
        if mask is not None:
            st = jnp.where(mask, st, -jnp.inf)
        sink = jnp.full((1, rq), sinks_ref[g * GQA_GROUP + GQA_GROUP - 1], F32)
        for i in range(GQA_GROUP - 2, -1, -1):
            sink = jnp.where(head_of_lane == i, sinks_ref[g * GQA_GROUP + i], sink)
        m = jnp.maximum(jnp.max(st, axis=0, keepdims=True), sink)
        e = jnp.exp(st - m)
        z = jnp.sum(e, axis=0, keepdims=True) + jnp.exp(sink - m)
        ot = _dot(vt_pair[par * HEAD_DIM:(par + 1) * HEAD_DIM, :], e.astype(BF16))
        outs.append(ot / z)
    for i in range(GQA_GROUP):
        blk = jnp.concatenate([o[:, i * ATT_Q:(i + 1) * ATT_Q] for o in outs], axis=0)
        c0 = (GQA_GROUP * pair + i) * LANES
        obuf[row0:row0 + ATT_Q, c0:c0 + LANES] = blk.T


def _in_proj(hb, win_ref, lo, hi):
    return _dot(hb, win_ref[:, lo:hi])


def _mix_out(x, g1, proj_gb, conv, y_attn_in, hb, win_ref, wco_ref, wao_ref, wmo_ref):
    y_conv = _dot((proj_gb * conv).astype(BF16), wco_ref[...])
    y_attn = _dot(y_attn_in.astype(BF16), wao_ref[...])
    g_conv = _in_proj(hb, win_ref, OFF_GCONV, OFF_GATTN)
    g_attn = _in_proj(hb, win_ref, OFF_GATTN, D_IN)
    merged = _sigmoid(g_conv) * y_conv + _sigmoid(g_attn) * y_attn
    return x + g1 * _dot(merged.astype(BF16), wmo_ref[...])


def _mixer_prompt_kernel(x_ref, mod_ref, cos_ref, sin_ref, win_ref, wconv_ref, wco_ref, wao_ref, sinks_ref, wmo_ref,
                         x1_ref, conv_ref, k_ref, v_ref, ubuf, kbuf, vtbuf, obuf):
    j = pl.program_id(1)
    t = x_ref.shape[1]

    @pl.when(j == 0)
    def _():
        ubuf[0:SUBLANES, :] = jnp.zeros((SUBLANES, D_CONV), F32)
        kbuf[0:WINDOW, :] = jnp.zeros((WINDOW, KV_DIM), BF16)
        vtbuf[:, 0:WINDOW] = jnp.zeros((KV_DIM, WINDOW), BF16)

    x = x_ref[0]
    mod = mod_ref[0]
    sh1, sc1, g1 = mod[:, 0:D_MODEL], mod[:, D_MODEL:2 * D_MODEL], mod[:, 2 * D_MODEL:3 * D_MODEL]
    hb = (_rms(x) * (1.0 + sc1) + sh1).astype(BF16)

    u = _in_proj(hb, win_ref, OFF_GC, OFF_XC) * _in_proj(hb, win_ref, OFF_XC, OFF_Q)
    ubuf[SUBLANES:SUBLANES + t, :] = u
    wc = wconv_ref[...]
    conv = wc[0:1] * ubuf[SUBLANES - 2:SUBLANES - 2 + t, :] + wc[1:2] * ubuf[SUBLANES - 1:SUBLANES - 1 + t, :] + wc[2:3] * u
    conv_ref[0] = u[t - (CONV_W - 1):t]
    ubuf[SUBLANES - 2:SUBLANES, :] = u[t - (CONV_W - 1):t]

    cos, sin = cos_ref[...], sin_ref[...]
    q = (_rope(_in_proj(hb, win_ref, OFF_Q, OFF_K), cos, sin) * ATTN_SCALE).astype(BF16)
    k = _rope(_in_proj(hb, win_ref, OFF_K, OFF_V), cos, sin)
    v = _in_proj(hb, win_ref, OFF_V, OFF_GCONV)
    kbuf[WINDOW:WINDOW + t, :] = k.astype(BF16)
    vtbuf[:, WINDOW:WINDOW + t] = v.T.astype(BF16)
    k_ref[0] = k[t - WINDOW:t]
    v_ref[0] = v[t - WINDOW:t]

    nkeys = ATT_Q + WINDOW
    rq = GQA_GROUP * ATT_Q
    ki = lax.broadcasted_iota(I32, (nkeys, rq), 0)
    qi = lax.broadcasted_iota(I32, (nkeys, rq), 1) % ATT_Q
    band = ki // CHUNK - qi // CHUNK
    band_ok = (band >= 0) & (band <= WINDOW // CHUNK)
    for s in range(t // ATT_Q):
        mask = band_ok & (ki + (j * t + s * ATT_Q - WINDOW) >= 0)
        qs = q[s * ATT_Q:(s + 1) * ATT_Q]
        for pair in range(N_KV_HEADS // 2):
            k_pair = kbuf[s * ATT_Q:s * ATT_Q + nkeys, pair * LANES:(pair + 1) * LANES]
            vt_pair = vtbuf[pair * LANES:(pair + 1) * LANES, s * ATT_Q:s * ATT_Q + nkeys]
            _attend_pair(qs, k_pair, vt_pair, sinks_ref, pair, mask, obuf, s * ATT_Q)
    kbuf[0:WINDOW, :] = kbuf[t:t + WINDOW, :]
    vtbuf[:, 0:WINDOW] = vtbuf[:, t:t + WINDOW]

    gate_b = _in_proj(hb, win_ref, OFF_GB, OFF_GC)
    x1_ref[0] = _mix_out(x, g1, gate_b, conv, obuf[...], hb, win_ref, wco_ref, wao_ref, wmo_ref)


def _mixer_prompt(x, mod, cos, sin, win, wconv, wco, wao, sinks, wmo):
    b, seq, d = x.shape
    t = MIX_TILE
    return pl.pallas_call(
        _mixer_prompt_kernel,
        grid=(b, seq // t),
        in_specs=[pl.BlockSpec((1, t, d), lambda i, j: (i, j, 0)),
                  pl.BlockSpec((1, 1, 6 * d), lambda i, j: (i, 0, 0)),
                  pl.BlockSpec((t, LANES), lambda i, j: (j, 0)),
                  pl.BlockSpec((t, LANES), lambda i, j: (j, 0)),
                  _const_spec(win.shape), _const_spec(wconv.shape), _const_spec(wco.shape), _const_spec(wao.shape),
                  pl.BlockSpec(memory_space=pltpu.SMEM),
                  _const_spec(wmo.shape)],
        out_specs=[pl.BlockSpec((1, t, d), lambda i, j: (i, j, 0)),
                   pl.BlockSpec((1, CONV_W - 1, D_CONV), lambda i, j: (i, 0, 0)),
                   pl.BlockSpec((1, WINDOW, KV_DIM), lambda i, j: (i, 0, 0)),
                   pl.BlockSpec((1, WINDOW, KV_DIM), lambda i, j: (i, 0, 0))],
        out_shape=[jax.ShapeDtypeStruct((b, seq, d), F32),
                   jax.ShapeDtypeStruct((b, CONV_W - 1, D_CONV), F32),
                   jax.ShapeDtypeStruct((b, WINDOW, KV_DIM), F32),
                   jax.ShapeDtypeStruct((b, WINDOW, KV_DIM), F32)],
        scratch_shapes=[pltpu.VMEM((SUBLANES + t, D_CONV), F32),
                        pltpu.VMEM((WINDOW + t, KV_DIM), BF16),
                        pltpu.VMEM((KV_DIM, WINDOW + t), BF16),
                        pltpu.VMEM((t, Q_DIM), F32)],
        compiler_params=pltpu.CompilerParams(dimension_semantics=("arbitrary", "arbitrary"),
                                             vmem_limit_bytes=VMEM_LIMIT),
        name="mixer_prompt",
    )(x, mod, cos, sin, win, wconv, wco, wao, sinks, wmo)


def _mixer_sample_kernel(x_ref, mod_ref, cos_ref, sin_ref, ck_ref, cv_ref, sconv_ref, win_ref, wconv_ref, wco_ref,
                         wao_ref, sinks_ref, wmo_ref, x1_ref, conv_ref, k_ref, v_ref, ubuf, obuf):
    bb, t, d = x_ref.shape
    x3 = x_ref[...]
    mod = mod_ref[...]
    sh1, sc1, g1 = mod[:, :, 0:d], mod[:, :, d:2 * d], mod[:, :, 2 * d:3 * d]
    x = x3.reshape(bb * t, d)
    hb = (_rms(x3) * (1.0 + sc1) + sh1).astype(BF16).reshape(bb * t, d)

    u = _in_proj(hb, win_ref, OFF_GC, OFF_XC) * _in_proj(hb, win_ref, OFF_XC, OFF_Q)
    u3 = u.reshape(bb, t, D_CONV)
    ubuf[:, SUBLANES - 2:SUBLANES, :] = sconv_ref[...]
    ubuf[:, SUBLANES:SUBLANES + t, :] = u3
    wc = wconv_ref[...]
    conv = (wc[0:1] * ubuf[:, SUBLANES - 2:SUBLANES - 2 + t, :] + wc[1:2] * ubuf[:, SUBLANES - 1:SUBLANES - 1 + t, :]
            + wc[2:3] * u3).reshape(bb * t, D_CONV)
    conv_ref[...] = u3[:, t - (CONV_W - 1):t, :]

    cos = jnp.concatenate([cos_ref[...]] * bb, axis=0)
    sin = jnp.concatenate([sin_ref[...]] * bb, axis=0)
    q = (_rope(_in_proj(hb, win_ref, OFF_Q, OFF_K), cos, sin) * ATTN_SCALE).astype(BF16)
    k = _rope(_in_proj(hb, win_ref, OFF_K, OFF_V), cos, sin)
    v = _in_proj(hb, win_ref, OFF_V, OFF_GCONV)
    per = ATT_Q // t
    nkeys = per * (WINDOW + t)
    rq = GQA_GROUP * ATT_Q
    key_stream = lax.broadcasted_iota(I32, (nkeys, rq), 0) // (WINDOW + t)
    query_stream = (lax.broadcasted_iota(I32, (nkeys, rq), 1) % ATT_Q) // t
    mask = key_stream == query_stream
    for blk in range(bb // per):
        k_parts, v_parts = [], []
        for b in range(blk * per, (blk + 1) * per):
            kb, vb = k[b * t:(b + 1) * t], v[b * t:(b + 1) * t]
            ck, cv = ck_ref[b], cv_ref[b]
            k_ref[b] = jnp.concatenate([ck[t:WINDOW], kb], axis=0)
            v_ref[b] = jnp.concatenate([cv[t:WINDOW], vb], axis=0)
            k_parts += [ck, kb]
            v_parts += [cv, vb]
        k_all = jnp.concatenate(k_parts, axis=0).astype(BF16)
        vt_all = jnp.concatenate(v_parts, axis=0).T.astype(BF16)
        qs = q[blk * ATT_Q:(blk + 1) * ATT_Q]
        for pair in range(N_KV_HEADS // 2):
            _attend_pair(qs, k_all[:, pair * LANES:(pair + 1) * LANES], vt_all[pair * LANES:(pair + 1) * LANES, :],
                         sinks_ref, pair, mask, obuf, blk * ATT_Q)

    gate_b = _in_proj(hb, win_ref, OFF_GB, OFF_GC)
    g1f = jnp.broadcast_to(g1, (bb, t, d)).reshape(bb * t, d)
    x1_ref[...] = _mix_out(x, g1f, gate_b, conv, obuf[...], hb, win_ref, wco_ref, wao_ref, wmo_ref).reshape(bb, t, d)


def _mixer_sample(x, mod, cos, sin, ck, cv, sconv, win, wconv, wco, wao, sinks, wmo):
    b, t, d = x.shape
    bb = SAMPLE_BB
    blk = lambda *s: pl.BlockSpec((bb,) + s, lambda i: (i, 0, 0))
    return pl.pallas_call(
        _mixer_sample_kernel,
        grid=(b // bb,),
        in_specs=[blk(t, d), blk(1, 6 * d),
                  pl.BlockSpec((t, LANES), lambda i: (0, 0)), pl.BlockSpec((t, LANES), lambda i: (0, 0)),
                  blk(WINDOW, KV_DIM), blk(WINDOW, KV_DIM), blk(CONV_W - 1, D_CONV),
                  _const_spec(win.shape), _const_spec(wconv.shape), _const_spec(wco.shape), _const_spec(wao.shape),
                  pl.BlockSpec(memory_space=pltpu.SMEM),
                  _const_spec(wmo.shape)],
        out_specs=[blk(t, d), blk(CONV_W - 1, D_CONV), blk(WINDOW, KV_DIM), blk(WINDOW, KV_DIM)],
        out_shape=[jax.ShapeDtypeStruct((b, t, d), F32),
                   jax.ShapeDtypeStruct((b, CONV_W - 1, D_CONV), F32),
                   jax.ShapeDtypeStruct((b, WINDOW, KV_DIM), F32),
                   jax.ShapeDtypeStruct((b, WINDOW, KV_DIM), F32)],
        scratch_shapes=[pltpu.VMEM((bb, SUBLANES + t, D_CONV), F32),
                        pltpu.VMEM((bb * t, Q_DIM), F32)],
        compiler_params=pltpu.CompilerParams(dimension_semantics=("arbitrary",), vmem_limit_bytes=VMEM_LIMIT),
        name="mixer_sample",
    )(x, mod, cos, sin, ck, cv, sconv, win, wconv, wco, wao, sinks, wmo)


def _pre_kernel(xp_ref, xs_ref, mod_ref, wsg_ref, wsu_ref, wsd_ref, wr_ref, rb_ref, h2_ref, base_ref, cw_ref,
                *, prompt_tiles):
    nc, c, d = xp_ref.shape
    t = nc * c
    x3 = jnp.where(pl.program_id(0) < prompt_tiles, xp_ref[...], xs_ref[...])
    mod = mod_ref[...]
    sh2, sc2, g2 = mod[:, :, 0:d], mod[:, :, d:2 * d], mod[:, :, 2 * d:3 * d]
    h3 = _rms(x3) * (1.0 + sc2) + sh2
    h2 = h3.reshape(t, d)
    hb = h2.astype(BF16)
    h2_ref[...] = _pack_bf16_pairs(h2)
    shared = _dot((_silu(_dot(hb, wsg_ref[...])) * _dot(hb, wsu_ref[...])).astype(BF16), wsd_ref[...])
    base_ref[...] = x3 + g2 * shared.reshape(nc, c, d)

    logits = lax.dot_general(wr_ref[...], h2, (((1,), (1,)), ((), ())), preferred_element_type=F32,
                             precision=lax.Precision.HIGHEST)
    scores = _sigmoid(logits)
    biased = scores + rb_ref[...]
    g3 = biased.reshape(N_EXPERT_GROUPS, GROUP_SIZE, t)
    member = lax.broadcasted_iota(I32, g3.shape, 1)
    m1 = jnp.max(g3, axis=1, keepdims=True)
    first = jnp.min(jnp.where(g3 == m1, member, GROUP_SIZE), axis=1, keepdims=True)
    m2 = jnp.max(jnp.where(member == first, -jnp.inf, g3), axis=1, keepdims=True)
    gs = m1 + m2
    gidx = lax.broadcasted_iota(I32, gs.shape, 0)
    grank = jnp.zeros(gs.shape, I32)
    for o in range(N_EXPERT_GROUPS):
        other = gs[o:o + 1]
        grank += ((other > gs) | ((other == gs) & (o < gidx))).astype(I32)
    eligible = jnp.broadcast_to(grank < TOPK_GROUPS, g3.shape).reshape(N_EXPERTS, t)
    mb = jnp.where(eligible, biased, -jnp.inf)
    eidx = lax.broadcasted_iota(I32, mb.shape, 0)
    erank = jnp.zeros(mb.shape, I32)
    for o in range(N_EXPERTS):
        other = mb[o:o + 1]
        erank += ((other > mb) | ((other == mb) & (o < eidx))).astype(I32)
    sel = eligible & (erank < TOP_K)
    ssum = jnp.sum(jnp.where(sel, scores, 0.0), axis=0, keepdims=True)
    cw_ref[...] = jnp.where(sel, scores / ssum * ROUTED_SCALE, -1.0)


def _pre(x1_p, x1_s, modc, wsg, wsu, wsd, wr_t, rb):
    (ncp, c, d), ncs = x1_p.shape, x1_s.shape[0]
    nc = PRE_TILE // c
    nchunks = ncp + ncs
    n = nchunks * c
    pt = ncp // nc
    blk3 = pl.BlockSpec((nc, c, d), lambda i: (i, 0, 0))
    return pl.pallas_call(
        functools.partial(_pre_kernel, prompt_tiles=pt),
        grid=(nchunks // nc,),
        in_specs=[pl.BlockSpec((nc, c, d), lambda i: (jnp.minimum(i, pt - 1), 0, 0)),
                  pl.BlockSpec((nc, c, d), lambda i: (jnp.maximum(i - pt, 0), 0, 0)),
                  pl.BlockSpec((nc, 1, 3 * d), lambda i: (i, 0, 0)),
                  _const_spec(wsg.shape), _const_spec(wsu.shape), _const_spec(wsd.shape),
                  _const_spec(wr_t.shape), _const_spec(rb.shape)],
        out_specs=[pl.BlockSpec((nc * c, d // 2), lambda i: (i, 0)), blk3,
                   pl.BlockSpec((N_EXPERTS, nc * c), lambda i: (0, i))],
        out_shape=[jax.ShapeDtypeStruct((n, d // 2), I32),
                   jax.ShapeDtypeStruct((nchunks, c, d), F32),
                   jax.ShapeDtypeStruct((N_EXPERTS, n), F32)],
        compiler_params=pltpu.CompilerParams(dimension_semantics=("arbitrary",), vmem_limit_bytes=VMEM_LIMIT),
        name="pre_ffn",
    )(x1_p, x1_s, modc, wsg, wsu, wsd, wr_t, rb)


def _rank_kernel(cw_ref, rank_ref, cnt_ref, carry):
    i = pl.program_id(0)
    t = cw_ref.shape[1]

    @pl.when(i == 0)
    def _():
        carry[...] = jnp.zeros(carry.shape, F32)

    sel = (cw_ref[...] >= 0.0).astype(BF16)
    r = lax.broadcasted_iota(I32, (t, t), 0)
    c = lax.broadcasted_iota(I32, (t, t), 1)
    before = (r < c).astype(BF16)
    rank = carry[...] + _dot(sel, before)
    rank_ref[...] = rank.astype(I32)
    carry[...] = carry[...] + jnp.sum(sel.astype(F32), axis=1, keepdims=True)
    cnt_ref[...] = carry[...].astype(I32)


def _rank(cw):
    e, n = cw.shape
    t = RANK_TILE
    return pl.pallas_call(
        _rank_kernel,
        grid=(n // t,),
        in_specs=[pl.BlockSpec((e, t), lambda i: (0, i))],
        out_specs=[pl.BlockSpec((e, t), lambda i: (0, i)), pl.BlockSpec((e, 1), lambda i: (0, 0))],
        out_shape=[jax.ShapeDtypeStruct((e, n), I32), jax.ShapeDtypeStruct((e, 1), I32)],
        scratch_shapes=[pltpu.VMEM((e, 1), F32)],
        compiler_params=pltpu.CompilerParams(dimension_semantics=("arbitrary",)),
        name="expert_rank",
    )(cw)


def _slot_kernel(cw_ref, rank_ref, start_ref, pos_ref, w_ref):
    cw = cw_ref[...]
    e, t = cw.shape
    sel = cw >= 0.0
    r = lax.broadcasted_iota(I32, (e, e), 0)
    c = lax.broadcasted_iota(I32, (e, e), 1)
    lower = (c < r).astype(BF16)
    kidx = _dot(lower, sel.astype(BF16))
    posf = start_ref[...].astype(F32) + rank_ref[...].astype(F32)
    pos_rows, w_rows = [], []
    for k in range(TOP_K):
        m = sel & (kidx == float(k))
        pos_rows.append(jnp.sum(jnp.where(m, posf, 0.0), axis=0, keepdims=True))
        w_rows.append(jnp.sum(jnp.where(m, cw, 0.0), axis=0, keepdims=True))
    pos_ref[...] = jnp.concatenate(pos_rows, axis=0).astype(I32)
    w_ref[...] = jnp.concatenate(w_rows, axis=0)


def _slots(cw, rank, seg_start):
    e, n = cw.shape
    t = RANK_TILE
    return pl.pallas_call(
        _slot_kernel,
        grid=(n // t,),
        in_specs=[pl.BlockSpec((e, t), lambda i: (0, i)), pl.BlockSpec((e, t), lambda i: (0, i)),
                  pl.BlockSpec((e, 1), lambda i: (0, 0))],
        out_specs=[pl.BlockSpec((TOP_K, t), lambda i: (0, i)), pl.BlockSpec((TOP_K, t), lambda i: (0, i))],
        out_shape=[jax.ShapeDtypeStruct((TOP_K, n), I32), jax.ShapeDtypeStruct((TOP_K, n), F32)],
        compiler_params=pltpu.CompilerParams(dimension_semantics=("arbitrary",)),
        name="expert_slots",
    )(cw, rank, seg_start)


def _sc_mesh():
    return plsc.VectorSubcoreMesh(core_axis_name="c", subcore_axis_name="s")


def _sc_worker_id():
    return lax.axis_index("s") * (SC_WORKERS // 16) + lax.axis_index("c")


def _sc_dispatch(rows, pos, n_rows):
    n, d = rows.shape
    per_w = n // SC_WORKERS
    w = SC_WINDOW
    n_chunks = per_w // w

    @functools.partial(
        pl.kernel, mesh=_sc_mesh(),
        out_type=jax.ShapeDtypeStruct((n_rows, d), rows.dtype),
        scratch_types=[pltpu.VMEM((2, TOP_K, w), I32), pltpu.VMEM((2, w, d), rows.dtype),
                       pltpu.SemaphoreType.DMA((2,)), pltpu.SemaphoreType.DMA((2,)), pltpu.SemaphoreType.DMA((2,))],
        name="sc_dispatch")
    def k(rows_hbm, pos_hbm, o_hbm, idx_v, rows_v, row_sem, idx_sem, out_sem):
        wid = _sc_worker_id()
        base = wid * per_w

        def loads(c, slot):
            off = pl.multiple_of(base + c * w, SUBLANES)
            return (pltpu.make_async_copy(rows_hbm.at[pl.ds(off, w)], rows_v.at[slot], row_sem.at[slot]),
                    pltpu.make_async_copy(pos_hbm.at[wid * n_chunks + c], idx_v.at[slot], idx_sem.at[slot]))

        def scatters(slot):
            return [pltpu.make_async_copy(rows_v.at[slot], o_hbm.at[idx_v.at[slot, kk]], out_sem.at[slot])
                    for kk in range(TOP_K)]

        for cp in loads(0, 0):
            cp.start()
        for c in range(n_chunks):
            slot = c % 2
            for cp in loads(c, slot):
                cp.wait()
            for cp in scatters(slot):
                cp.start()
            if c >= 1:
                for cp in scatters(1 - slot):
                    cp.wait()
            if c + 1 < n_chunks:
                for cp in loads(c + 1, 1 - slot):
                    cp.start()
        for cp in scatters((n_chunks - 1) % 2):
            cp.wait()

    pos_chunks = pos.reshape(TOP_K, n // w, w).transpose(1, 0, 2)
    return k(rows, pos_chunks)


def _sc_collect(rows, pos_flat):
    d = rows.shape[1]
    total = pos_flat.shape[0]
    per_w = total // SC_WORKERS
    w = SC_WINDOW
    n_pairs = per_w // (2 * w)

    @functools.partial(
        pl.kernel, mesh=_sc_mesh(),
        out_type=jax.ShapeDtypeStruct((total, d), rows.dtype),
        scratch_types=[pltpu.VMEM((per_w,), I32), pltpu.VMEM((2, w, d), rows.dtype),
                       pltpu.SemaphoreType.DMA((2,)), pltpu.SemaphoreType.DMA((2,))],
        name="sc_collect")
    def k(rows_hbm, pos_hbm, o_hbm, idx_v, rows_v, in_sem, out_sem):
        base = pl.multiple_of(_sc_worker_id() * per_w, SUBLANES)
        pltpu.sync_copy(pos_hbm.at[pl.ds(base, per_w)], idx_v)

        def gather(c, slot):
            idx = idx_v.at[pl.ds(pl.multiple_of(c * w, SUBLANES), w)]
            return pltpu.make_async_copy(rows_hbm.at[idx], rows_v.at[slot], in_sem.at[slot])

        def write(c, slot):
            off = pl.multiple_of(base + c * w, SUBLANES)
            return pltpu.make_async_copy(rows_v.at[slot], o_hbm.at[pl.ds(off, w)], out_sem.at[slot])

        gather(0, 0).start()

        @pl.loop(0, n_pairs)
        def _(p):
            c0 = 2 * p
            gather(c0 + 1, 1).start()
            gather(c0, 0).wait()
            write(c0, 0).start()
            gather(c0 + 1, 1).wait()
            write(c0 + 1, 1).start()
            write(c0, 0).wait()

            @pl.when(p + 1 < n_pairs)
            def _():
                gather(c0 + 2, 0).start()

            write(c0 + 1, 1).wait()

    return k(rows, pos_flat)


def _gmm_kernel(be_ref, nu_ref, x_ref, wg_ref, wu_ref, wd_ref, y_ref, wgb, wub, wdb):
    b = pl.program_id(0)
    prev = be_ref[jnp.maximum(b - 1, 0)]

    @pl.when((b == 0) | (be_ref[b] != prev))
    def _():
        wgb[...] = wg_ref[0].astype(BF16)
        wub[...] = wu_ref[0].astype(BF16)
        wdb[...] = wd_ref[0].astype(BF16)

    @pl.when(b < nu_ref[0])
    def _():
        lo, hi = _unpack_bf16_pairs(x_ref[...])
        xb = jnp.concatenate([lo.astype(BF16), hi.astype(BF16)], axis=1)
        mid = (_silu(_dot(xb, wgb[...])) * _dot(xb, wub[...])).astype(BF16)
        y_ref[...] = _pack_bf16_pairs(_dot(mid, wdb[...]))


def _gmm(x_sorted, block_e, n_used, wg, wu, wd):
    r, half = x_sorted.shape
    d = 2 * half
    bm = GMM_BM
    row_blk = pl.BlockSpec((bm, half), lambda b, be, nu: (jnp.minimum(b, nu[0] - 1), 0))
    return pl.pallas_call(
        _gmm_kernel,
        grid_spec=pltpu.PrefetchScalarGridSpec(
            num_scalar_prefetch=2,
            grid=(r // bm,),
            in_specs=[row_blk,
                      pl.BlockSpec((1, d, D_EXPERT), lambda b, be, nu: (be[b], 0, 0)),
                      pl.BlockSpec((1, d, D_EXPERT), lambda b, be, nu: (be[b], 0, 0)),
                      pl.BlockSpec((1, D_EXPERT, d), lambda b, be, nu: (be[b], 0, 0))],
            out_specs=row_blk,
            scratch_shapes=[pltpu.VMEM((d, D_EXPERT), BF16), pltpu.VMEM((d, D_EXPERT), BF16),
                            pltpu.VMEM((D_EXPERT, d), BF16)]),
        out_shape=jax.ShapeDtypeStruct((r, half), I32),
        compiler_params=pltpu.CompilerParams(dimension_semantics=("arbitrary",), vmem_limit_bytes=VMEM_LIMIT),
        name="expert_gmm",
    )(block_e, n_used, x_sorted, wg, wu, wd)


def _combine_kernel(base_ref, mod_ref, g_ref, w_ref, gain_ref, y_ref):
    nc, c, d = base_ref.shape
    w = w_ref[...]
    acc_lo = acc_hi = None
    for k in range(TOP_K):
        lo, hi = _unpack_bf16_pairs(g_ref[k])
        wk = w[:, k:k + 1]
        acc_lo = wk * lo if k == 0 else acc_lo + wk * lo
        acc_hi = wk * hi if k == 0 else acc_hi + wk * hi
    acc = jnp.concatenate([acc_lo, acc_hi], axis=1)
    g2 = mod_ref[...][:, :, 2 * d:3 * d]
    out = base_ref[...] + g2 * acc.reshape(nc, c, d)
    y_ref[...] = _rms(out) * gain_ref[...]


def _combine(base, modc, gathered, w_tok, gain, first_chunk, n_chunks):
    _, c, d = base.shape
    nc = COMB_TILE // c
    t = nc * c
    t0 = first_chunk // nc
    blk3 = pl.BlockSpec((nc, c, d), lambda i: (t0 + i, 0, 0))
    return pl.pallas_call(
        _combine_kernel,
        grid=(n_chunks // nc,),
        in_specs=[blk3, pl.BlockSpec((nc, 1, 3 * d), lambda i: (t0 + i, 0, 0)),
                  pl.BlockSpec((TOP_K, t, d // 2), lambda i: (0, t0 + i, 0)),
                  pl.BlockSpec((t, TOP_K), lambda i: (t0 + i, 0)),
                  pl.BlockSpec((1, 1, d), lambda i: (0, 0, 0))],
        out_specs=pl.BlockSpec((nc, c, d), lambda i: (i, 0, 0)),
        out_shape=jax.ShapeDtypeStruct((n_chunks, c, d), F32),
        compiler_params=pltpu.CompilerParams(dimension_semantics=("arbitrary",), vmem_limit_bytes=VMEM_LIMIT),
        name="combine_norm",
    )(base, modc, gathered, w_tok, gain.reshape(1, 1, d))


def _rope_tables(pos):
    half = HEAD_DIM // 2
    inv_freq = ROPE_THETA ** (-jnp.arange(half, dtype=F32) / half)
    ang = pos.astype(F32)[:, None] * inv_freq[None, :]
    cos, sin = jnp.cos(ang), jnp.sin(ang)
    reps = LANES // HEAD_DIM
    return jnp.tile(jnp.concatenate([cos, cos], axis=1), (1, reps)), jnp.tile(jnp.concatenate([-sin, sin], axis=1), (1, reps))


def _routed_ffn(h2, cw, w_gate, w_up, w_down):
    n, half = h2.shape
    rank, counts = _rank(cw)
    bm = GMM_BM
    padded = (counts[:, 0] + bm - 1) // bm * bm
    seg_end = jnp.cumsum(padded)
    seg_start = (seg_end - padded).astype(I32)
    n_rows = n * TOP_K + N_EXPERTS * bm
    n_blocks = n_rows // bm
    block_start = jnp.arange(n_blocks, dtype=I32) * bm
    block_e = jnp.minimum(jnp.sum((seg_end[None, :] <= block_start[:, None]).astype(I32), axis=1), N_EXPERTS - 1)
    n_used = (seg_end[-1:] // bm).astype(I32)
    pos, w_k = _slots(cw, rank, seg_start[:, None])
    x_sorted = _sc_dispatch(h2, pos, n_rows)
    y_sorted = _gmm(x_sorted, block_e, n_used, w_gate, w_up, w_down)
    gathered = _sc_collect(y_sorted, pos.reshape(TOP_K * n)).reshape(TOP_K, n, half)
    return gathered, w_k


def kernel(x_prompt, x_sample, cache_k, cache_v, state_conv, c_prompt, c_sample, w_ada, b_ada, w_in, w_conv,
           w_conv_out, w_attn_o, attn_sinks, w_mix_out, w_router, router_bias, w_exp_gate, w_exp_up, w_exp_down,
           w_sh_gate, w_sh_up, w_sh_down, final_gain):
    assert w_ada.shape[0] == 1, "one layer"
    bp, seq, d = x_prompt.shape
    bs, ts, _ = x_sample.shape
    assert ts == CHUNK and seq % MIX_TILE == 0 and bs % SAMPLE_BB == 0

    c_all = jnp.concatenate([c_prompt, c_sample], axis=0)
    pad = (-c_all.shape[0]) % SUBLANES
    mod = _ada(jnp.pad(c_all, ((0, pad), (0, 0))), w_ada[0], b_ada[0])[:bp + bs]
    mod_p, mod_s = mod[:bp, None, :], mod[bp:, None, :]

    perm = _head_perm()
    w_in_l = w_in[0]
    w_in_p = jnp.concatenate([w_in_l[:, :OFF_Q], w_in_l[:, OFF_Q:OFF_K][:, perm], w_in_l[:, OFF_K:]], axis=1)
    win, wco, wao, wmo = (w.astype(BF16) for w in (w_in_p, w_conv_out[0], w_attn_o[0][perm], w_mix_out[0]))
    cos_p, sin_p = _rope_tables(jnp.arange(seq, dtype=I32))
    cos_s, sin_s = _rope_tables(PAST_LEN + jnp.arange(ts, dtype=I32))

    x1_p, conv_p, k_p, v_p = _mixer_prompt(x_prompt, mod_p, cos_p, sin_p, win, w_conv[0], wco, wao, attn_sinks[0], wmo)
    x1_s, conv_s, k_s, v_s = _mixer_sample(
        x_sample, mod_s, cos_s, sin_s, cache_k[0].reshape(bs, WINDOW, KV_DIM), cache_v[0].reshape(bs, WINDOW, KV_DIM),
        state_conv[0], win, w_conv[0], wco, wao, attn_sinks[0], wmo)

    n_p, n_s = bp * seq, bs * ts
    n = n_p + n_s
    assert n % (SC_WORKERS * SC_WINDOW) == 0 and n % RANK_TILE == 0
    assert n_p % PRE_TILE == 0 and n_s % PRE_TILE == 0 and n_p % COMB_TILE == 0 and n_s % COMB_TILE == 0
    mod2 = mod[:, None, 3 * d:]
    modc = jnp.concatenate([jnp.repeat(mod2[:bp], seq // CHUNK, axis=0), mod2[bp:]], axis=0)

    h2, base, cw = _pre(x1_p.reshape(n_p // CHUNK, CHUNK, d), x1_s, modc, w_sh_gate[0].astype(BF16),
                        w_sh_up[0].astype(BF16), w_sh_down[0].astype(BF16), w_router[0].T, router_bias[0][:, None])
    gathered, w_k = _routed_ffn(h2, cw, w_exp_gate[0], w_exp_up[0], w_exp_down[0])
    w_tok = w_k.T
    y_p = _combine(base, modc, gathered, w_tok, final_gain, 0, n_p // CHUNK)
    y_s = _combine(base, modc, gathered, w_tok, final_gain, n_p // CHUNK, n_s // CHUNK)

    kv = lambda a: a.reshape(1, a.shape[0], WINDOW, N_KV_HEADS, HEAD_DIM)
    return (y_p.reshape(bp, seq, d), y_s, conv_p[None], kv(k_p), kv(v_p), conv_s[None], kv(k_s), kv(v_s))
```

```python
import functools

import numpy as np
import jax
import jax.numpy as jnp
from jax import lax
from jax.experimental import pallas as pl
from jax.experimental.pallas import tpu as pltpu
from jax.experimental.pallas import tpu_sc as plsc

F32 = jnp.float32
BF16 = jnp.bfloat16
I32 = jnp.int32

D_MODEL = 1024
CHUNK = 64
D_CONV = 1024
CONV_W = 3
N_HEADS = 16
N_KV_HEADS = 4
HEAD_DIM = 64
GQA_GROUP = N_HEADS // N_KV_HEADS
WINDOW = 128
ROPE_THETA = 10000.0
ATTN_SCALE = HEAD_DIM ** -0.5
N_EXPERTS = 64
TOP_K = 8
N_EXPERT_GROUPS = 8
GROUP_SIZE = N_EXPERTS // N_EXPERT_GROUPS
TOPK_GROUPS = 4
D_EXPERT = 256
D_SHARED = 256
ROUTED_SCALE = 2.5
EPS = 1e-6
PAST_LEN = 4096
Q_DIM = N_HEADS * HEAD_DIM
KV_DIM = N_KV_HEADS * HEAD_DIM
OFF_GB, OFF_GC, OFF_XC, OFF_Q, OFF_K, OFF_V, OFF_GCONV, OFF_GATTN, D_IN = (
    0, 1024, 2048, 3072, 4096, 4352, 4608, 5632, 6656)

LANES = 128
SUBLANES = 8
VMEM_LIMIT = 56 * 1024 * 1024

MIX_TILE = 256
ATT_Q = 128
SAMPLE_BB = 4
PRE_TILE = 256
RANK_TILE = 512
GMM_BM = 1024
GMM_SUB = 256
COMB_TILE = 256
SC_WORKERS = 32
SC_WINDOW = 96


def _const_spec(shape):
    nd = len(shape)
    return pl.BlockSpec(shape, lambda *_: (0,) * nd, pipeline_mode=pl.Buffered(1))


def _rms(x):
    return x * lax.rsqrt(jnp.mean(x * x, axis=-1, keepdims=True) + EPS)


def _sigmoid(x):
    return 1.0 / (1.0 + jnp.exp(-x))


def _silu(x):
    return x * _sigmoid(x)


def _dot(a, b):
    return jnp.dot(a, b, preferred_element_type=F32)


def _pack_bf16_pairs(x):
    half = x.shape[-1] // 2
    lo = lax.bitcast_convert_type(x[..., :half].astype(BF16).astype(F32), I32)
    hi = lax.bitcast_convert_type(x[..., half:].astype(BF16).astype(F32), I32)
    return lax.shift_right_logical(lo, 16) | hi


def _unpack_bf16_pairs(words):
    lo = lax.bitcast_convert_type(lax.shift_left(words, 16), F32)
    hi = lax.bitcast_convert_type(words & jnp.int32(-65536), F32)
    return lo, hi


def _ada_kernel(c_ref, w_ref, b_ref, o_ref):
    s = _silu(c_ref[...]).astype(BF16)
    o_ref[...] = _dot(s, w_ref[...].astype(BF16)) + b_ref[...]


def _ada(c_all, w_ada, b_ada):
    rows = c_all.shape[0]
    n_out = w_ada.shape[1]
    bn = 768
    return pl.pallas_call(
        _ada_kernel,
        grid=(n_out // bn,),
        in_specs=[pl.BlockSpec((rows, D_MODEL), lambda i: (0, 0)),
                  pl.BlockSpec((D_MODEL, bn), lambda i: (0, i)),
                  pl.BlockSpec((1, bn), lambda i: (0, i))],
        out_specs=pl.BlockSpec((rows, bn), lambda i: (0, i)),
        out_shape=jax.ShapeDtypeStruct((rows, n_out), F32),
        name="ada_mod",
    )(c_all, w_ada, b_ada.reshape(1, n_out))


def _rope(x, cos, sin_signed):
    lane = lax.broadcasted_iota(I32, (x.shape[0], LANES), 1)
    first_half = (lane % HEAD_DIM) < (HEAD_DIM // 2)
    outs = []
    for g in range(x.shape[1] // LANES):
        xg = x[:, g * LANES:(g + 1) * LANES]
        up = pltpu.roll(xg, LANES - HEAD_DIM // 2, axis=1)
        down = pltpu.roll(xg, HEAD_DIM // 2, axis=1)
        partner = jnp.where(first_half, up, down)
        outs.append(xg * cos + partner * sin_signed)
    return jnp.concatenate(outs, axis=1)


def _head_perm():
    n = np.arange(Q_DIM)
    c, par, dd = n // LANES, (n % LANES) // HEAD_DIM, n % HEAD_DIM
    pair, i = c // GQA_GROUP, c % GQA_GROUP
    return HEAD_DIM * (GQA_GROUP * (2 * pair + par) + i) + dd


def _attend_pair(q_blk, k_pair, vt_pair, sinks_ref, pair, mask, obuf, row0):
    rq = GQA_GROUP * ATT_Q
    low = lax.broadcasted_iota(I32, (ATT_Q, LANES), 1) < HEAD_DIM
    head_of_lane = lax.broadcasted_iota(I32, (1, rq), 1) // ATT_Q
    outs = []
    for par in range(2):
        g = 2 * pair + par
        keep = low if par == 0 else jnp.logical_not(low)
        cols = [q_blk[:, (GQA_GROUP * pair + i) * LANES:(GQA_GROUP * pair + i + 1) * LANES] for i in range(GQA_GROUP)]
        qg = jnp.concatenate([jnp.where(keep, c, jnp.zeros_like(c)) for c in cols], axis=0)
        st = lax.dot_general(k_pair, qg, (((1,), (1,)), ((), ())), preferred_element_type=F32)
        if mask is not None:
            st = jnp.where(mask, st, -jnp.inf)
        sink = jnp.full((1, rq), sinks_ref[g * GQA_GROUP + GQA_GROUP - 1], F32)
        for i in range(GQA_GROUP - 2, -1, -1):
            sink = jnp.where(head_of_lane == i, sinks_ref[g * GQA_GROUP + i], sink)
        m = jnp.maximum(jnp.max(st, axis=0, keepdims=True), sink)
        e = jnp.exp(st - m)
        z = jnp.sum(e, axis=0, keepdims=True) + jnp.exp(sink - m)
        ot = _dot(vt_pair[par * HEAD_DIM:(par + 1) * HEAD_DIM, :], e.astype(BF16))
        outs.append(ot / z)
    for i in range(GQA_GROUP):
        blk = jnp.concatenate([o[:, i * ATT_Q:(i + 1) * ATT_Q] for o in outs], axis=0)
        c0 = (GQA_GROUP * pair + i) * LANES
        obuf[row0:row0 + ATT_Q, c0:c0 + LANES] = blk.T


def _in_proj(hb, win_ref, lo, hi):
    return _dot(hb, win_ref[:, lo:hi])


def _mix_out(x, g1, proj_gb, conv, y_attn_in, hb, win_ref, wco_ref, wao_ref, wmo_ref):
    y_conv = _dot((proj_gb * conv).astype(BF16), wco_ref[...])
    y_attn = _dot(y_attn_in.astype(BF16), wao_ref[...])
    g_conv = _in_proj(hb, win_ref, OFF_GCONV, OFF_GATTN)
    g_attn = _in_proj(hb, win_ref, OFF_GATTN, D_IN)
    merged = _sigmoid(g_conv) * y_conv + _sigmoid(g_attn) * y_attn
    return x + g1 * _dot(merged.astype(BF16), wmo_ref[...])


def _mixer_prompt_kernel(x_ref, mod_ref, cos_ref, sin_ref, win_ref, wconv_ref, wco_ref, wao_ref, sinks_ref, wmo_ref,
                         x1_ref, conv_ref, k_ref, v_ref, ubuf, kbuf, vtbuf, obuf):
    j = pl.program_id(1)
    t = x_ref.shape[1]

    @pl.when(j == 0)
    def _():
        ubuf[0:SUBLANES, :] = jnp.zeros((SUBLANES, D_CONV), F32)
        kbuf[0:WINDOW, :] = jnp.zeros((WINDOW, KV_DIM), BF16)
        vtbuf[:, 0:WINDOW] = jnp.zeros((KV_DIM, WINDOW), BF16)

    x = x_ref[0]
    mod = mod_ref[0]
    sh1, sc1, g1 = mod[:, 0:D_MODEL], mod[:, D_MODEL:2 * D_MODEL], mod[:, 2 * D_MODEL:3 * D_MODEL]
    hb = (_rms(x) * (1.0 + sc1) + sh1).astype(BF16)

    u = _in_proj(hb, win_ref, OFF_GC, OFF_XC) * _in_proj(hb, win_ref, OFF_XC, OFF_Q)
    ubuf[SUBLANES:SUBLANES + t, :] = u
    wc = wconv_ref[...]
    conv = wc[0:1] * ubuf[SUBLANES - 2:SUBLANES - 2 + t, :] + wc[1:2] * ubuf[SUBLANES - 1:SUBLANES - 1 + t, :] + wc[2:3] * u
    conv_ref[0] = u[t - (CONV_W - 1):t]
    ubuf[SUBLANES - 2:SUBLANES, :] = u[t - (CONV_W - 1):t]

    cos, sin = cos_ref[...], sin_ref[...]
    q = (_rope(_in_proj(hb, win_ref, OFF_Q, OFF_K), cos, sin) * ATTN_SCALE).astype(BF16)
    k = _rope(_in_proj(hb, win_ref, OFF_K, OFF_V), cos, sin)
    v = _in_proj(hb, win_ref, OFF_V, OFF_GCONV)
    kbuf[WINDOW:WINDOW + t, :] = k.astype(BF16)
    vtbuf[:, WINDOW:WINDOW + t] = v.T.astype(BF16)
    k_ref[0] = k[t - WINDOW:t]
    v_ref[0] = v[t - WINDOW:t]

    nkeys = ATT_Q + WINDOW
    rq = GQA_GROUP * ATT_Q
    ki = lax.broadcasted_iota(I32, (nkeys, rq), 0)
    qi = lax.broadcasted_iota(I32, (nkeys, rq), 1) % ATT_Q
    band = ki // CHUNK - qi // CHUNK
    band_ok = (band >= 0) & (band <= WINDOW // CHUNK)
    for s in range(t // ATT_Q):
        mask = band_ok & (ki + (j * t + s * ATT_Q - WINDOW) >= 0)
        qs = q[s * ATT_Q:(s + 1) * ATT_Q]
        for pair in range(N_KV_HEADS // 2):
            k_pair = kbuf[s * ATT_Q:s * ATT_Q + nkeys, pair * LANES:(pair + 1) * LANES]
            vt_pair = vtbuf[pair * LANES:(pair + 1) * LANES, s * ATT_Q:s * ATT_Q + nkeys]
            _attend_pair(qs, k_pair, vt_pair, sinks_ref, pair, mask, obuf, s * ATT_Q)
    kbuf[0:WINDOW, :] = kbuf[t:t + WINDOW, :]
    vtbuf[:, 0:WINDOW] = vtbuf[:, t:t + WINDOW]

    gate_b = _in_proj(hb, win_ref, OFF_GB, OFF_GC)
    x1_ref[0] = _mix_out(x, g1, gate_b, conv, obuf[...], hb, win_ref, wco_ref, wao_ref, wmo_ref)


def _mixer_prompt(x, mod, cos, sin, win, wconv, wco, wao, sinks, wmo):
    b, seq, d = x.shape
    t = MIX_TILE
    return pl.pallas_call(
        _mixer_prompt_kernel,
        grid=(b, seq // t),
        in_specs=[pl.BlockSpec((1, t, d), lambda i, j: (i, j, 0)),
                  pl.BlockSpec((1, 1, 6 * d), lambda i, j: (i, 0, 0)),
                  pl.BlockSpec((t, LANES), lambda i, j: (j, 0)),
                  pl.BlockSpec((t, LANES), lambda i, j: (j, 0)),
                  _const_spec(win.shape), _const_spec(wconv.shape), _const_spec(wco.shape), _const_spec(wao.shape),
                  pl.BlockSpec(memory_space=pltpu.SMEM),
                  _const_spec(wmo.shape)],
        out_specs=[pl.BlockSpec((1, t, d), lambda i, j: (i, j, 0)),
                   pl.BlockSpec((1, CONV_W - 1, D_CONV), lambda i, j: (i, 0, 0)),
                   pl.BlockSpec((1, WINDOW, KV_DIM), lambda i, j: (i, 0, 0)),
                   pl.BlockSpec((1, WINDOW, KV_DIM), lambda i, j: (i, 0, 0))],
        out_shape=[jax.ShapeDtypeStruct((b, seq, d), F32),
                   jax.ShapeDtypeStruct((b, CONV_W - 1, D_CONV), F32),
                   jax.ShapeDtypeStruct((b, WINDOW, KV_DIM), F32),
                   jax.ShapeDtypeStruct((b, WINDOW, KV_DIM), F32)],
        scratch_shapes=[pltpu.VMEM((SUBLANES + t, D_CONV), F32),
                        pltpu.VMEM((WINDOW + t, KV_DIM), BF16),
                        pltpu.VMEM((KV_DIM, WINDOW + t), BF16),
                        pltpu.VMEM((t, Q_DIM), F32)],
        compiler_params=pltpu.CompilerParams(dimension_semantics=("arbitrary", "arbitrary"),
                                             vmem_limit_bytes=VMEM_LIMIT),
        name="mixer_prompt",
    )(x, mod, cos, sin, win, wconv, wco, wao, sinks, wmo)


def _mixer_sample_kernel(x_ref, mod_ref, cos_ref, sin_ref, ck_ref, cv_ref, sconv_ref, win_ref, wconv_ref, wco_ref,
                         wao_ref, sinks_ref, wmo_ref, x1_ref, conv_ref, k_ref, v_ref, ubuf, obuf):
    bb, t, d = x_ref.shape
    x3 = x_ref[...]
    mod = mod_ref[...]
    sh1, sc1, g1 = mod[:, :, 0:d], mod[:, :, d:2 * d], mod[:, :, 2 * d:3 * d]
    x = x3.reshape(bb * t, d)
    hb = (_rms(x3) * (1.0 + sc1) + sh1).astype(BF16).reshape(bb * t, d)

    u = _in_proj(hb, win_ref, OFF_GC, OFF_XC) * _in_proj(hb, win_ref, OFF_XC, OFF_Q)
    u3 = u.reshape(bb, t, D_CONV)
    ubuf[:, SUBLANES - 2:SUBLANES, :] = sconv_ref[...]
    ubuf[:, SUBLANES:SUBLANES + t, :] = u3
    wc = wconv_ref[...]
    conv = (wc[0:1] * ubuf[:, SUBLANES - 2:SUBLANES - 2 + t, :] + wc[1:2] * ubuf[:, SUBLANES - 1:SUBLANES - 1 + t, :]
            + wc[2:3] * u3).reshape(bb * t, D_CONV)
    conv_ref[...] = u3[:, t - (CONV_W - 1):t, :]

    cos = jnp.concatenate([cos_ref[...]] * bb, axis=0)
    sin = jnp.concatenate([sin_ref[...]] * bb, axis=0)
    q = (_rope(_in_proj(hb, win_ref, OFF_Q, OFF_K), cos, sin) * ATTN_SCALE).astype(BF16)
    k = _rope(_in_proj(hb, win_ref, OFF_K, OFF_V), cos, sin)
    v = _in_proj(hb, win_ref, OFF_V, OFF_GCONV)
    per = ATT_Q // t
    nkeys = per * (WINDOW + t)
    rq = GQA_GROUP * ATT_Q
    key_stream = lax.broadcasted_iota(I32, (nkeys, rq), 0) // (WINDOW + t)
    query_stream = (lax.broadcasted_iota(I32, (nkeys, rq), 1) % ATT_Q) // t
    mask = key_stream == query_stream
    for blk in range(bb // per):
        k_parts, v_parts = [], []
        for b in range(blk * per, (blk + 1) * per):
            kb, vb = k[b * t:(b + 1) * t], v[b * t:(b + 1) * t]
            ck, cv = ck_ref[b], cv_ref[b]
            k_ref[b] = jnp.concatenate([ck[t:WINDOW], kb], axis=0)
            v_ref[b] = jnp.concatenate([cv[t:WINDOW], vb], axis=0)
            k_parts += [ck, kb]
            v_parts += [cv, vb]
        k_all = jnp.concatenate(k_parts, axis=0).astype(BF16)
        vt_all = jnp.concatenate(v_parts, axis=0).T.astype(BF16)
        qs = q[blk * ATT_Q:(blk + 1) * ATT_Q]
        for pair in range(N_KV_HEADS // 2):
            _attend_pair(qs, k_all[:, pair * LANES:(pair + 1) * LANES], vt_all[pair * LANES:(pair + 1) * LANES, :],
                         sinks_ref, pair, mask, obuf, blk * ATT_Q)

    gate_b = _in_proj(hb, win_ref, OFF_GB, OFF_GC)
    g1f = jnp.broadcast_to(g1, (bb, t, d)).reshape(bb * t, d)
    x1_ref[...] = _mix_out(x, g1f, gate_b, conv, obuf[...], hb, win_ref, wco_ref, wao_ref, wmo_ref).reshape(bb, t, d)


def _mixer_sample(x, mod, cos, sin, ck, cv, sconv, win, wconv, wco, wao, sinks, wmo):
    b, t, d = x.shape
    bb = SAMPLE_BB
    blk = lambda *s: pl.BlockSpec((bb,) + s, lambda i: (i, 0, 0))
    return pl.pallas_call(
        _mixer_sample_kernel,
        grid=(b // bb,),
        in_specs=[blk(t, d), blk(1, 6 * d),
                  pl.BlockSpec((t, LANES), lambda i: (0, 0)), pl.BlockSpec((t, LANES), lambda i: (0, 0)),
                  blk(WINDOW, KV_DIM), blk(WINDOW, KV_DIM), blk(CONV_W - 1, D_CONV),
                  _const_spec(win.shape), _const_spec(wconv.shape), _const_spec(wco.shape), _const_spec(wao.shape),
                  pl.BlockSpec(memory_space=pltpu.SMEM),
                  _const_spec(wmo.shape)],
        out_specs=[blk(t, d), blk(CONV_W - 1, D_CONV), blk(WINDOW, KV_DIM), blk(WINDOW, KV_DIM)],
        out_shape=[jax.ShapeDtypeStruct((b, t, d), F32),
                   jax.ShapeDtypeStruct((b, CONV_W - 1, D_CONV), F32),
                   jax.ShapeDtypeStruct((b, WINDOW, KV_DIM), F32),
                   jax.ShapeDtypeStruct((b, WINDOW, KV_DIM), F32)],
        scratch_shapes=[pltpu.VMEM((bb, SUBLANES + t, D_CONV), F32),
                        pltpu.VMEM((bb * t, Q_DIM), F32)],
        compiler_params=pltpu.CompilerParams(dimension_semantics=("arbitrary",), vmem_limit_bytes=VMEM_LIMIT),
        name="mixer_sample",
    )(x, mod, cos, sin, ck, cv, sconv, win, wconv, wco, wao, sinks, wmo)


def _pre_kernel(xp_ref, xs_ref, mod_ref, wsg_ref, wsu_ref, wsd_ref, wr_ref, rb_ref, h2_ref, base_ref, cw_ref,
                *, prompt_tiles):
    nc, c, d = xp_ref.shape
    t = nc * c
    x3 = jnp.where(pl.program_id(0) < prompt_tiles, xp_ref[...], xs_ref[...])
    mod = mod_ref[...]
    sh2, sc2, g2 = mod[:, :, 0:d], mod[:, :, d:2 * d], mod[:, :, 2 * d:3 * d]
    h3 = _rms(x3) * (1.0 + sc2) + sh2
    h2 = h3.reshape(t, d)
    hb = h2.astype(BF16)
    h2_ref[...] = _pack_bf16_pairs(h2)
    shared = _dot((_silu(_dot(hb, wsg_ref[...])) * _dot(hb, wsu_ref[...])).astype(BF16), wsd_ref[...])
    base_ref[...] = x3 + g2 * shared.reshape(nc, c, d)

    logits = lax.dot_general(wr_ref[...], h2, (((1,), (1,)), ((), ())), preferred_element_type=F32,
                             precision=lax.Precision.HIGHEST)
    scores = _sigmoid(logits)
    biased = scores + rb_ref[...]
    g3 = biased.reshape(N_EXPERT_GROUPS, GROUP_SIZE, t)
    member = lax.broadcasted_iota(I32, g3.shape, 1)
    m1 = jnp.max(g3, axis=1, keepdims=True)
    first = jnp.min(jnp.where(g3 == m1, member, GROUP_SIZE), axis=1, keepdims=True)
    m2 = jnp.max(jnp.where(member == first, -jnp.inf, g3), axis=1, keepdims=True)
    gs = m1 + m2
    gidx = lax.broadcasted_iota(I32, gs.shape, 0)
    grank = jnp.zeros(gs.shape, I32)
    for o in range(N_EXPERT_GROUPS):
        other = gs[o:o + 1]
        grank += ((other > gs) | ((other == gs) & (o < gidx))).astype(I32)
    eligible = jnp.broadcast_to(grank < TOPK_GROUPS, g3.shape).reshape(N_EXPERTS, t)
    mb = jnp.where(eligible, biased, -jnp.inf)
    eidx = lax.broadcasted_iota(I32, mb.shape, 0)
    erank = jnp.zeros(mb.shape, I32)
    for o in range(N_EXPERTS):
        other = mb[o:o + 1]
        erank += ((other > mb) | ((other == mb) & (o < eidx))).astype(I32)
    sel = eligible & (erank < TOP_K)
    ssum = jnp.sum(jnp.where(sel, scores, 0.0), axis=0, keepdims=True)
    cw_ref[...] = jnp.where(sel, scores / ssum * ROUTED_SCALE, -1.0)


def _pre(x1_p, x1_s, modc, wsg, wsu, wsd, wr_t, rb):
    (ncp, c, d), ncs = x1_p.shape, x1_s.shape[0]
    nc = PRE_TILE // c
    nchunks = ncp + ncs
    n = nchunks * c
    pt = ncp // nc
    blk3 = pl.BlockSpec((nc, c, d), lambda i: (i, 0, 0))
    return pl.pallas_call(
        functools.partial(_pre_kernel, prompt_tiles=pt),
        grid=(nchunks // nc,),
        in_specs=[pl.BlockSpec((nc, c, d), lambda i: (jnp.minimum(i, pt - 1), 0, 0)),
                  pl.BlockSpec((nc, c, d), lambda i: (jnp.maximum(i - pt, 0), 0, 0)),
                  pl.BlockSpec((nc, 1, 3 * d), lambda i: (i, 0, 0)),
                  _const_spec(wsg.shape), _const_spec(wsu.shape), _const_spec(wsd.shape),
                  _const_spec(wr_t.shape), _const_spec(rb.shape)],
        out_specs=[pl.BlockSpec((nc * c, d // 2), lambda i: (i, 0)), blk3,
                   pl.BlockSpec((N_EXPERTS, nc * c), lambda i: (0, i))],
        out_shape=[jax.ShapeDtypeStruct((n, d // 2), I32),
                   jax.ShapeDtypeStruct((nchunks, c, d), F32),
                   jax.ShapeDtypeStruct((N_EXPERTS, n), F32)],
        compiler_params=pltpu.CompilerParams(dimension_semantics=("arbitrary",), vmem_limit_bytes=VMEM_LIMIT),
        name="pre_ffn",
    )(x1_p, x1_s, modc, wsg, wsu, wsd, wr_t, rb)


def _rank_kernel(cw_ref, rank_ref, cnt_ref, carry):
    i = pl.program_id(0)
    t = cw_ref.shape[1]

    @pl.when(i == 0)
    def _():
        carry[...] = jnp.zeros(carry.shape, F32)

    sel = (cw_ref[...] >= 0.0).astype(BF16)
    r = lax.broadcasted_iota(I32, (t, t), 0)
    c = lax.broadcasted_iota(I32, (t, t), 1)
    before = (r < c).astype(BF16)
    rank = carry[...] + _dot(sel, before)
    rank_ref[...] = rank.astype(I32)
    carry[...] = carry[...] + jnp.sum(sel.astype(F32), axis=1, keepdims=True)
    cnt_ref[...] = carry[...].astype(I32)


def _rank(cw):
    e, n = cw.shape
    t = RANK_TILE
    return pl.pallas_call(
        _rank_kernel,
        grid=(n // t,),
        in_specs=[pl.BlockSpec((e, t), lambda i: (0, i))],
        out_specs=[pl.BlockSpec((e, t), lambda i: (0, i)), pl.BlockSpec((e, 1), lambda i: (0, 0))],
        out_shape=[jax.ShapeDtypeStruct((e, n), I32), jax.ShapeDtypeStruct((e, 1), I32)],
        scratch_shapes=[pltpu.VMEM((e, 1), F32)],
        compiler_params=pltpu.CompilerParams(dimension_semantics=("arbitrary",)),
        name="expert_rank",
    )(cw)


def _slot_kernel(cw_ref, rank_ref, start_ref, pos_ref, w_ref):
    cw = cw_ref[...]
    e, t = cw.shape
    sel = cw >= 0.0
    r = lax.broadcasted_iota(I32, (e, e), 0)
    c = lax.broadcasted_iota(I32, (e, e), 1)
    lower = (c < r).astype(BF16)
    kidx = _dot(lower, sel.astype(BF16))
    posf = start_ref[...].astype(F32) + rank_ref[...].astype(F32)
    pos_rows, w_rows = [], []
    for k in range(TOP_K):
        m = sel & (kidx == float(k))
        pos_rows.append(jnp.sum(jnp.where(m, posf, 0.0), axis=0, keepdims=True))
        w_rows.append(jnp.sum(jnp.where(m, cw, 0.0), axis=0, keepdims=True))
    pos_ref[...] = jnp.concatenate(pos_rows, axis=0).astype(I32)
    w_ref[...] = jnp.concatenate(w_rows, axis=0)


def _slots(cw, rank, seg_start):
    e, n = cw.shape
    t = RANK_TILE
    return pl.pallas_call(
        _slot_kernel,
        grid=(n // t,),
        in_specs=[pl.BlockSpec((e, t), lambda i: (0, i)), pl.BlockSpec((e, t), lambda i: (0, i)),
                  pl.BlockSpec((e, 1), lambda i: (0, 0))],
        out_specs=[pl.BlockSpec((TOP_K, t), lambda i: (0, i)), pl.BlockSpec((TOP_K, t), lambda i: (0, i))],
        out_shape=[jax.ShapeDtypeStruct((TOP_K, n), I32), jax.ShapeDtypeStruct((TOP_K, n), F32)],
        compiler_params=pltpu.CompilerParams(dimension_semantics=("arbitrary",)),
        name="expert_slots",
    )(cw, rank, seg_start)


def _sc_mesh():
    return plsc.VectorSubcoreMesh(core_axis_name="c", subcore_axis_name="s")


def _sc_worker_id():
    return lax.axis_index("s") * (SC_WORKERS // 16) + lax.axis_index("c")


def _sc_dispatch(rows, pos, n_rows):
    n, d = rows.shape
    per_w = n // SC_WORKERS
    w = SC_WINDOW
    n_chunks = per_w // w

    @functools.partial(
        pl.kernel, mesh=_sc_mesh(),
        out_type=jax.ShapeDtypeStruct((n_rows, d), rows.dtype),
        scratch_types=[pltpu.VMEM((2, TOP_K, w), I32), pltpu.VMEM((2, w, d), rows.dtype),
                       pltpu.SemaphoreType.DMA((2,)), pltpu.SemaphoreType.DMA((2,)), pltpu.SemaphoreType.DMA((2,))],
        name="sc_dispatch")
    def k(rows_hbm, pos_hbm, o_hbm, idx_v, rows_v, row_sem, idx_sem, out_sem):
        wid = _sc_worker_id()
        base = wid * per_w

        def loads(c, slot):
            off = pl.multiple_of(base + c * w, SUBLANES)
            return (pltpu.make_async_copy(rows_hbm.at[pl.ds(off, w)], rows_v.at[slot], row_sem.at[slot]),
                    pltpu.make_async_copy(pos_hbm.at[wid * n_chunks + c], idx_v.at[slot], idx_sem.at[slot]))

        def scatters(slot):
            return [pltpu.make_async_copy(rows_v.at[slot], o_hbm.at[idx_v.at[slot, kk]], out_sem.at[slot])
                    for kk in range(TOP_K)]

        for cp in loads(0, 0):
            cp.start()
        for c in range(n_chunks):
            slot = c % 2
            for cp in loads(c, slot):
                cp.wait()
            for cp in scatters(slot):
                cp.start()
            if c >= 1:
                for cp in scatters(1 - slot):
                    cp.wait()
            if c + 1 < n_chunks:
                for cp in loads(c + 1, 1 - slot):
                    cp.start()
        for cp in scatters((n_chunks - 1) % 2):
            cp.wait()

    pos_chunks = pos.reshape(TOP_K, n // w, w).transpose(1, 0, 2)
    return k(rows, pos_chunks)


def _sc_collect(rows, pos_flat):
    d = rows.shape[1]
    total = pos_flat.shape[0]
    per_w = total // SC_WORKERS
    w = SC_WINDOW
    n_pairs = per_w // (2 * w)

    @functools.partial(
        pl.kernel, mesh=_sc_mesh(),
        out_type=jax.ShapeDtypeStruct((total, d), rows.dtype),
        scratch_types=[pltpu.VMEM((per_w,), I32), pltpu.VMEM((2, w, d), rows.dtype),
                       pltpu.SemaphoreType.DMA((2,)), pltpu.SemaphoreType.DMA((2,))],
        name="sc_collect")
    def k(rows_hbm, pos_hbm, o_hbm, idx_v, rows_v, in_sem, out_sem):
        base = pl.multiple_of(_sc_worker_id() * per_w, SUBLANES)
        pltpu.sync_copy(pos_hbm.at[pl.ds(base, per_w)], idx_v)

        def gather(c, slot):
            idx = idx_v.at[pl.ds(pl.multiple_of(c * w, SUBLANES), w)]
            return pltpu.make_async_copy(rows_hbm.at[idx], rows_v.at[slot], in_sem.at[slot])

        def write(c, slot):
            off = pl.multiple_of(base + c * w, SUBLANES)
            return pltpu.make_async_copy(rows_v.at[slot], o_hbm.at[pl.ds(off, w)], out_sem.at[slot])

        gather(0, 0).start()

        @pl.loop(0, n_pairs)
        def _(p):
            c0 = 2 * p
            gather(c0 + 1, 1).start()
            gather(c0, 0).wait()
            write(c0, 0).start()
            gather(c0 + 1, 1).wait()
            write(c0 + 1, 1).start()
            write(c0, 0).wait()

            @pl.when(p + 1 < n_pairs)
            def _():
                gather(c0 + 2, 0).start()

            write(c0 + 1, 1).wait()

    return k(rows, pos_flat)


def _gmm_kernel(be_ref, br_ref, nu_ref, x_ref, wg_ref, wu_ref, wd_ref, y_ref, wgb, wub, wdb):
    b = pl.program_id(0)
    prev = be_ref[jnp.maximum(b - 1, 0)]

    @pl.when((b == 0) | (be_ref[b] != prev))
    def _():
        wgb[...] = wg_ref[0].astype(BF16)
        wub[...] = wu_ref[0].astype(BF16)
        wdb[...] = wd_ref[0].astype(BF16)

    @pl.loop(0, (br_ref[b] + GMM_SUB - 1) // GMM_SUB)
    def _(i):
        rows = pl.ds(pl.multiple_of(i * GMM_SUB, GMM_SUB), GMM_SUB)
        lo, hi = _unpack_bf16_pairs(x_ref[rows, :])
        xb = jnp.concatenate([lo.astype(BF16), hi.astype(BF16)], axis=1)
        mid = (_silu(_dot(xb, wgb[...])) * _dot(xb, wub[...])).astype(BF16)
        y_ref[rows, :] = _pack_bf16_pairs(_dot(mid, wdb[...]))


def _gmm(x_sorted, block_e, block_rows, n_used, wg, wu, wd):
    r, half = x_sorted.shape
    d = 2 * half
    bm = GMM_BM
    row_blk = pl.BlockSpec((bm, half), lambda b, be, br, nu: (jnp.minimum(b, nu[0] - 1), 0))
    w_spec = lambda shape: pl.BlockSpec((1,) + shape, lambda b, be, br, nu: (be[b], 0, 0))
    return pl.pallas_call(
        _gmm_kernel,
        grid_spec=pltpu.PrefetchScalarGridSpec(
            num_scalar_prefetch=3,
            grid=(r // bm,),
            in_specs=[row_blk, w_spec((d, D_EXPERT)), w_spec((d, D_EXPERT)), w_spec((D_EXPERT, d))],
            out_specs=row_blk,
            scratch_shapes=[pltpu.VMEM((d, D_EXPERT), BF16), pltpu.VMEM((d, D_EXPERT), BF16),
                            pltpu.VMEM((D_EXPERT, d), BF16)]),
        out_shape=jax.ShapeDtypeStruct((r, half), I32),
        compiler_params=pltpu.CompilerParams(dimension_semantics=("arbitrary",), vmem_limit_bytes=VMEM_LIMIT),
        name="expert_gmm",
    )(block_e, block_rows, n_used, x_sorted, wg, wu, wd)


def _combine_kernel(base_ref, mod_ref, g_ref, w_ref, gain_ref, y_ref):
    nc, c, d = base_ref.shape
    w = w_ref[...]
    acc_lo = acc_hi = None
    for k in range(TOP_K):
        lo, hi = _unpack_bf16_pairs(g_ref[k])
        wk = w[:, k:k + 1]
        acc_lo = wk * lo if k == 0 else acc_lo + wk * lo
        acc_hi = wk * hi if k == 0 else acc_hi + wk * hi
    acc = jnp.concatenate([acc_lo, acc_hi], axis=1)
    g2 = mod_ref[...][:, :, 2 * d:3 * d]
    out = base_ref[...] + g2 * acc.reshape(nc, c, d)
    y_ref[...] = _rms(out) * gain_ref[...]


def _combine(base, modc, gathered, w_tok, gain, first_chunk, n_chunks):
    _, c, d = base.shape
    nc = COMB_TILE // c
    t = nc * c
    t0 = first_chunk // nc
    blk3 = pl.BlockSpec((nc, c, d), lambda i: (t0 + i, 0, 0))
    return pl.pallas_call(
        _combine_kernel,
        grid=(n_chunks // nc,),
        in_specs=[blk3, pl.BlockSpec((nc, 1, 3 * d), lambda i: (t0 + i, 0, 0)),
                  pl.BlockSpec((TOP_K, t, d // 2), lambda i: (0, t0 + i, 0)),
                  pl.BlockSpec((t, TOP_K), lambda i: (t0 + i, 0)),
                  pl.BlockSpec((1, 1, d), lambda i: (0, 0, 0))],
        out_specs=pl.BlockSpec((nc, c, d), lambda i: (i, 0, 0)),
        out_shape=jax.ShapeDtypeStruct((n_chunks, c, d), F32),
        compiler_params=pltpu.CompilerParams(dimension_semantics=("arbitrary",), vmem_limit_bytes=VMEM_LIMIT),
        name="combine_norm",
    )(base, modc, gathered, w_tok, gain.reshape(1, 1, d))


def _rope_tables(pos):
    half = HEAD_DIM // 2
    inv_freq = ROPE_THETA ** (-jnp.arange(half, dtype=F32) / half)
    ang = pos.astype(F32)[:, None] * inv_freq[None, :]
    cos, sin = jnp.cos(ang), jnp.sin(ang)
    reps = LANES // HEAD_DIM
    return jnp.tile(jnp.concatenate([cos, cos], axis=1), (1, reps)), jnp.tile(jnp.concatenate([-sin, sin], axis=1), (1, reps))


def _routed_ffn(h2, cw, w_gate, w_up, w_down):
    n, half = h2.shape
    rank, counts = _rank(cw)
    bm = GMM_BM
    padded = (counts[:, 0] + bm - 1) // bm * bm
    seg_end = jnp.cumsum(padded)
    seg_start = (seg_end - padded).astype(I32)
    n_rows = n * TOP_K + N_EXPERTS * bm
    n_blocks = n_rows // bm
    block_start = jnp.arange(n_blocks, dtype=I32) * bm
    block_e = jnp.minimum(jnp.sum((seg_end[None, :] <= block_start[:, None]).astype(I32), axis=1), N_EXPERTS - 1)
    own = block_e[:, None] == jnp.arange(N_EXPERTS, dtype=I32)[None, :]
    real_end = jnp.sum(jnp.where(own, (seg_start + counts[:, 0])[None, :], 0), axis=1)
    block_rows = jnp.clip(real_end - block_start, 0, bm).astype(I32)
    n_used = (seg_end[-1:] // bm).astype(I32)
    pos, w_k = _slots(cw, rank, seg_start[:, None])
    x_sorted = _sc_dispatch(h2, pos, n_rows)
    y_sorted = _gmm(x_sorted, block_e, block_rows, n_used, w_gate, w_up, w_down)
    gathered = _sc_collect(y_sorted, pos.reshape(TOP_K * n)).reshape(TOP_K, n, half)
    return gathered, w_k


def kernel(x_prompt, x_sample, cache_k, cache_v, state_conv, c_prompt, c_sample, w_ada, b_ada, w_in, w_conv,
           w_conv_out, w_attn_o, attn_sinks, w_mix_out, w_router, router_bias, w_exp_gate, w_exp_up, w_exp_down,
           w_sh_gate, w_sh_up, w_sh_down, final_gain):
    assert w_ada.shape[0] == 1, "one layer"
    bp, seq, d = x_prompt.shape
    bs, ts, _ = x_sample.shape
    assert ts == CHUNK and seq % MIX_TILE == 0 and bs % SAMPLE_BB == 0

    c_all = jnp.concatenate([c_prompt, c_sample], axis=0)
    pad = (-c_all.shape[0]) % SUBLANES
    mod = _ada(jnp.pad(c_all, ((0, pad), (0, 0))), w_ada[0], b_ada[0])[:bp + bs]
    mod_p, mod_s = mod[:bp, None, :], mod[bp:, None, :]

    perm = _head_perm()
    w_in_l = w_in[0]
    w_in_p = jnp.concatenate([w_in_l[:, :OFF_Q], w_in_l[:, OFF_Q:OFF_K][:, perm], w_in_l[:, OFF_K:]], axis=1)
    win, wco, wao, wmo = (w.astype(BF16) for w in (w_in_p, w_conv_out[0], w_attn_o[0][perm], w_mix_out[0]))
    cos_p, sin_p = _rope_tables(jnp.arange(seq, dtype=I32))
    cos_s, sin_s = _rope_tables(PAST_LEN + jnp.arange(ts, dtype=I32))

    x1_p, conv_p, k_p, v_p = _mixer_prompt(x_prompt, mod_p, cos_p, sin_p, win, w_conv[0], wco, wao, attn_sinks[0], wmo)
    x1_s, conv_s, k_s, v_s = _mixer_sample(
        x_sample, mod_s, cos_s, sin_s, cache_k[0].reshape(bs, WINDOW, KV_DIM), cache_v[0].reshape(bs, WINDOW, KV_DIM),
        state_conv[0], win, w_conv[0], wco, wao, attn_sinks[0], wmo)

    n_p, n_s = bp * seq, bs * ts
    n = n_p + n_s
    assert n % (SC_WORKERS * SC_WINDOW) == 0 and n % RANK_TILE == 0
    assert n_p % PRE_TILE == 0 and n_s % PRE_TILE == 0 and n_p % COMB_TILE == 0 and n_s % COMB_TILE == 0
    mod2 = mod[:, None, 3 * d:]
    modc = jnp.concatenate([jnp.repeat(mod2[:bp], seq // CHUNK, axis=0), mod2[bp:]], axis=0)

    h2, base, cw = _pre(x1_p.reshape(n_p // CHUNK, CHUNK, d), x1_s, modc, w_sh_gate[0].astype(BF16),
                        w_sh_up[0].astype(BF16), w_sh_down[0].astype(BF16), w_router[0].T, router_bias[0][:, None])
    gathered, w_k = _routed_ffn(h2, cw, w_exp_gate[0], w_exp_up[0], w_exp_down[0])
    w_tok = w_k.T
    y_p = _combine(base, modc, gathered, w_tok, final_gain, 0, n_p // CHUNK)
    y_s = _combine(base, modc, gathered, w_tok, final_gain, n_p // CHUNK, n_s // CHUNK)

    kv = lambda a: a.reshape(1, a.shape[0], WINDOW, N_KV_HEADS, HEAD_DIM)
    return (y_p.reshape(bp, seq, d), y_s, conv_p[None], kv(k_p), kv(v_p), conv_s[None], kv(k_s), kv(v_s))
```

```python
import functools

import numpy as np
import jax
import jax.numpy as jnp
from jax import lax
from jax.experimental import pallas as pl
from jax.experimental.pallas import tpu as pltpu
from jax.experimental.pallas import tpu_sc as plsc

F32 = jnp.float32
BF16 = jnp.bfloat16
I32 = jnp.int32

D_MODEL = 1024
CHUNK = 64
D_CONV = 1024
CONV_W = 3
N_HEADS = 16
N_KV_HEADS = 4
HEAD_DIM = 64
GQA_GROUP = N_HEADS // N_KV_HEADS
WINDOW = 128
ROPE_THETA = 10000.0
ATTN_SCALE = HEAD_DIM ** -0.5
N_EXPERTS = 64
TOP_K = 8
N_EXPERT_GROUPS = 8
GROUP_SIZE = N_EXPERTS // N_EXPERT_GROUPS
TOPK_GROUPS = 4
D_EXPERT = 256
D_SHARED = 256
ROUTED_SCALE = 2.5
EPS = 1e-6
PAST_LEN = 4096
Q_DIM = N_HEADS * HEAD_DIM
KV_DIM = N_KV_HEADS * HEAD_DIM
OFF_GB, OFF_GC, OFF_XC, OFF_Q, OFF_K, OFF_V, OFF_GCONV, OFF_GATTN, D_IN = (
    0, 1024, 2048, 3072, 4096, 4352, 4608, 5632, 6656)

LANES = 128
SUBLANES = 8
VMEM_LIMIT = 56 * 1024 * 1024

MIX_TILE = 512
ATT_Q = 128
SAMPLE_BB = 4
PRE_TILE = 512
RANK_TILE = 512
GMM_BM = 1024
GMM_SUB = 512
GMM_TAIL = 128
COMB_TILE = 256
SC_WORKERS = 32
SC_WINDOW = 96


def _const_spec(shape):
    nd = len(shape)
    return pl.BlockSpec(shape, lambda *_: (0,) * nd, pipeline_mode=pl.Buffered(1))


def _rms(x):
    return x * lax.rsqrt(jnp.mean(x * x, axis=-1, keepdims=True) + EPS)


def _sigmoid(x):
    return 1.0 / (1.0 + jnp.exp(-x))


def _silu(x):
    return x * _sigmoid(x)


def _dot(a, b):
    return jnp.dot(a, b, preferred_element_type=F32)


def _pack_bf16_pairs(x):
    half = x.shape[-1] // 2
    lo = lax.bitcast_convert_type(x[..., :half].astype(BF16).astype(F32), I32)
    hi = lax.bitcast_convert_type(x[..., half:].astype(BF16).astype(F32), I32)
    return lax.shift_right_logical(lo, 16) | hi


def _unpack_bf16_pairs(words):
    lo = lax.bitcast_convert_type(lax.shift_left(words, 16), F32)
    hi = lax.bitcast_convert_type(words & jnp.int32(-65536), F32)
    return lo, hi


def _ada_kernel(c_ref, w_ref, b_ref, o_ref):
    s = _silu(c_ref[...]).astype(BF16)
    o_ref[...] = _dot(s, w_ref[...].astype(BF16)) + b_ref[...]


def _ada(c_all, w_ada, b_ada):
    rows = c_all.shape[0]
    n_out = w_ada.shape[1]
    bn = 768
    return pl.pallas_call(
        _ada_kernel,
        grid=(n_out // bn,),
        in_specs=[pl.BlockSpec((rows, D_MODEL), lambda i: (0, 0)),
                  pl.BlockSpec((D_MODEL, bn), lambda i: (0, i)),
                  pl.BlockSpec((1, bn), lambda i: (0, i))],
        out_specs=pl.BlockSpec((rows, bn), lambda i: (0, i)),
        out_shape=jax.ShapeDtypeStruct((rows, n_out), F32),
        name="ada_mod",
    )(c_all, w_ada, b_ada.reshape(1, n_out))


def _rope(x, cos, sin_signed):
    lane = lax.broadcasted_iota(I32, (x.shape[0], LANES), 1)
    first_half = (lane % HEAD_DIM) < (HEAD_DIM // 2)
    outs = []
    for g in range(x.shape[1] // LANES):
        xg = x[:, g * LANES:(g + 1) * LANES]
        up = pltpu.roll(xg, LANES - HEAD_DIM // 2, axis=1)
        down = pltpu.roll(xg, HEAD_DIM // 2, axis=1)
        partner = jnp.where(first_half, up, down)
        outs.append(xg * cos + partner * sin_signed)
    return jnp.concatenate(outs, axis=1)


def _head_perm():
    n = np.arange(Q_DIM)
    c, par, dd = n // LANES, (n % LANES) // HEAD_DIM, n % HEAD_DIM
    pair, i = c // GQA_GROUP, c % GQA_GROUP
    return HEAD_DIM * (GQA_GROUP * (2 * pair + par) + i) + dd


def _attend_pair(q_blk, k_pair, vt_pair, sinks_ref, pair, mask, obuf, row0):
    rq = GQA_GROUP * ATT_Q
    low = lax.broadcasted_iota(I32, (ATT_Q, LANES), 1) < HEAD_DIM
    head_of_lane = lax.broadcasted_iota(I32, (1, rq), 1) // ATT_Q
    outs = []
    for par in range(2):
        g = 2 * pair + par
        keep = low if par == 0 else jnp.logical_not(low)
        cols = [q_blk[:, (GQA_GROUP * pair + i) * LANES:(GQA_GROUP * pair + i + 1) * LANES] for i in range(GQA_GROUP)]
        qg = jnp.concatenate([jnp.where(keep, c, jnp.zeros_like(c)) for c in cols], axis=0)
        st = lax.dot_general(k_pair, qg, (((1,), (1,)), ((), ())), preferred_element_type=F32)
        if mask is not None:
            st = jnp.where(mask, st, -jnp.inf)
        sink = jnp.full((1, rq), sinks_ref[g * GQA_GROUP + GQA_GROUP - 1], F32)
        for i in range(GQA_GROUP - 2, -1, -1):
            sink = jnp.where(head_of_lane == i, sinks_ref[g * GQA_GROUP + i], sink)
        m = jnp.maximum(jnp.max(st, axis=0, keepdims=True), sink)
        e = jnp.exp(st - m)
        z = jnp.sum(e, axis=0, keepdims=True) + jnp.exp(sink - m)
        ot = _dot(vt_pair[par * HEAD_DIM:(par + 1) * HEAD_DIM, :], e.astype(BF16))
        outs.append(ot / z)
    for i in range(GQA_GROUP):
        blk = jnp.concatenate([o[:, i * ATT_Q:(i + 1) * ATT_Q] for o in outs], axis=0)
        c0 = (GQA_GROUP * pair + i) * LANES
        obuf[row0:row0 + ATT_Q, c0:c0 + LANES] = blk.T


def _in_proj(hb, win_ref, lo, hi):
    return _dot(hb, win_ref[:, lo:hi])


def _mix_out(x, g1, proj_gb, conv, y_attn_in, hb, win_ref, wco_ref, wao_ref, wmo_ref):
    y_conv = _dot((proj_gb * conv).astype(BF16), wco_ref[...])
    y_attn = _dot(y_attn_in.astype(BF16), wao_ref[...])
    g_conv = _in_proj(hb, win_ref, OFF_GCONV, OFF_GATTN)
    g_attn = _in_proj(hb, win_ref, OFF_GATTN, D_IN)
    merged = _sigmoid(g_conv) * y_conv + _sigmoid(g_attn) * y_attn
    return x + g1 * _dot(merged.astype(BF16), wmo_ref[...])


def _mixer_prompt_kernel(x_ref, mod_ref, cos_ref, sin_ref, win_ref, wconv_ref, wco_ref, wao_ref, sinks_ref, wmo_ref,
                         x1_ref, conv_ref, k_ref, v_ref, ubuf, kbuf, vtbuf, obuf):
    j = pl.program_id(1)
    t = x_ref.shape[1]

    @pl.when(j == 0)
    def _():
        ubuf[0:SUBLANES, :] = jnp.zeros((SUBLANES, D_CONV), F32)
        kbuf[0:WINDOW, :] = jnp.zeros((WINDOW, KV_DIM), BF16)
        vtbuf[:, 0:WINDOW] = jnp.zeros((KV_DIM, WINDOW), BF16)

    x = x_ref[0]
    mod = mod_ref[0]
    sh1, sc1, g1 = mod[:, 0:D_MODEL], mod[:, D_MODEL:2 * D_MODEL], mod[:, 2 * D_MODEL:3 * D_MODEL]
    hb = (_rms(x) * (1.0 + sc1) + sh1).astype(BF16)

    u = _in_proj(hb, win_ref, OFF_GC, OFF_XC) * _in_proj(hb, win_ref, OFF_XC, OFF_Q)
    ubuf[SUBLANES:SUBLANES + t, :] = u
    wc = wconv_ref[...]
    conv = wc[0:1] * ubuf[SUBLANES - 2:SUBLANES - 2 + t, :] + wc[1:2] * ubuf[SUBLANES - 1:SUBLANES - 1 + t, :] + wc[2:3] * u
    conv_ref[0] = u[t - (CONV_W - 1):t]
    ubuf[SUBLANES - 2:SUBLANES, :] = u[t - (CONV_W - 1):t]

    cos, sin = cos_ref[...], sin_ref[...]
    q = (_rope(_in_proj(hb, win_ref, OFF_Q, OFF_K), cos, sin) * ATTN_SCALE).astype(BF16)
    k = _rope(_in_proj(hb, win_ref, OFF_K, OFF_V), cos, sin)
    v = _in_proj(hb, win_ref, OFF_V, OFF_GCONV)
    kbuf[WINDOW:WINDOW + t, :] = k.astype(BF16)
    vtbuf[:, WINDOW:WINDOW + t] = v.T.astype(BF16)
    k_ref[0] = k[t - WINDOW:t]
    v_ref[0] = v[t - WINDOW:t]

    nkeys = ATT_Q + WINDOW
    rq = GQA_GROUP * ATT_Q
    ki = lax.broadcasted_iota(I32, (nkeys, rq), 0)
    qi = lax.broadcasted_iota(I32, (nkeys, rq), 1) % ATT_Q
    band = ki // CHUNK - qi // CHUNK
    band_ok = (band >= 0) & (band <= WINDOW // CHUNK)
    for s in range(t // ATT_Q):
        mask = band_ok & (ki + (j * t + s * ATT_Q - WINDOW) >= 0)
        qs = q[s * ATT_Q:(s + 1) * ATT_Q]
        for pair in range(N_KV_HEADS // 2):
            k_pair = kbuf[s * ATT_Q:s * ATT_Q + nkeys, pair * LANES:(pair + 1) * LANES]
            vt_pair = vtbuf[pair * LANES:(pair + 1) * LANES, s * ATT_Q:s * ATT_Q + nkeys]
            _attend_pair(qs, k_pair, vt_pair, sinks_ref, pair, mask, obuf, s * ATT_Q)
    kbuf[0:WINDOW, :] = kbuf[t:t + WINDOW, :]
    vtbuf[:, 0:WINDOW] = vtbuf[:, t:t + WINDOW]

    gate_b = _in_proj(hb, win_ref, OFF_GB, OFF_GC)
    x1_ref[0] = _mix_out(x, g1, gate_b, conv, obuf[...], hb, win_ref, wco_ref, wao_ref, wmo_ref)


def _mixer_prompt(x, mod, cos, sin, win, wconv, wco, wao, sinks, wmo):
    b, seq, d = x.shape
    t = MIX_TILE
    return pl.pallas_call(
        _mixer_prompt_kernel,
        grid=(b, seq // t),
        in_specs=[pl.BlockSpec((1, t, d), lambda i, j: (i, j, 0)),
                  pl.BlockSpec((1, 1, 6 * d), lambda i, j: (i, 0, 0)),
                  pl.BlockSpec((t, LANES), lambda i, j: (j, 0)),
                  pl.BlockSpec((t, LANES), lambda i, j: (j, 0)),
                  _const_spec(win.shape), _const_spec(wconv.shape), _const_spec(wco.shape), _const_spec(wao.shape),
                  pl.BlockSpec(memory_space=pltpu.SMEM),
                  _const_spec(wmo.shape)],
        out_specs=[pl.BlockSpec((1, t, d), lambda i, j: (i, j, 0)),
                   pl.BlockSpec((1, CONV_W - 1, D_CONV), lambda i, j: (i, 0, 0)),
                   pl.BlockSpec((1, WINDOW, KV_DIM), lambda i, j: (i, 0, 0)),
                   pl.BlockSpec((1, WINDOW, KV_DIM), lambda i, j: (i, 0, 0))],
        out_shape=[jax.ShapeDtypeStruct((b, seq, d), F32),
                   jax.ShapeDtypeStruct((b, CONV_W - 1, D_CONV), F32),
                   jax.ShapeDtypeStruct((b, WINDOW, KV_DIM), F32),
                   jax.ShapeDtypeStruct((b, WINDOW, KV_DIM), F32)],
        scratch_shapes=[pltpu.VMEM((SUBLANES + t, D_CONV), F32),
                        pltpu.VMEM((WINDOW + t, KV_DIM), BF16),
                        pltpu.VMEM((KV_DIM, WINDOW + t), BF16),
                        pltpu.VMEM((t, Q_DIM), F32)],
        compiler_params=pltpu.CompilerParams(dimension_semantics=("arbitrary", "arbitrary"),
                                             vmem_limit_bytes=VMEM_LIMIT),
        name="mixer_prompt",
    )(x, mod, cos, sin, win, wconv, wco, wao, sinks, wmo)


def _mixer_sample_kernel(x_ref, mod_ref, cos_ref, sin_ref, ck_ref, cv_ref, sconv_ref, win_ref, wconv_ref, wco_ref,
                         wao_ref, sinks_ref, wmo_ref, x1_ref, conv_ref, k_ref, v_ref, ubuf, obuf):
    bb, t, d = x_ref.shape
    x3 = x_ref[...]
    mod = mod_ref[...]
    sh1, sc1, g1 = mod[:, :, 0:d], mod[:, :, d:2 * d], mod[:, :, 2 * d:3 * d]
    x = x3.reshape(bb * t, d)
    hb = (_rms(x3) * (1.0 + sc1) + sh1).astype(BF16).reshape(bb * t, d)

    u = _in_proj(hb, win_ref, OFF_GC, OFF_XC) * _in_proj(hb, win_ref, OFF_XC, OFF_Q)
    u3 = u.reshape(bb, t, D_CONV)
    ubuf[:, SUBLANES - 2:SUBLANES, :] = sconv_ref[...]
    ubuf[:, SUBLANES:SUBLANES + t, :] = u3
    wc = wconv_ref[...]
    conv = (wc[0:1] * ubuf[:, SUBLANES - 2:SUBLANES - 2 + t, :] + wc[1:2] * ubuf[:, SUBLANES - 1:SUBLANES - 1 + t, :]
            + wc[2:3] * u3).reshape(bb * t, D_CONV)
    conv_ref[...] = u3[:, t - (CONV_W - 1):t, :]

    cos = jnp.concatenate([cos_ref[...]] * bb, axis=0)
    sin = jnp.concatenate([sin_ref[...]] * bb, axis=0)
    q = (_rope(_in_proj(hb, win_ref, OFF_Q, OFF_K), cos, sin) * ATTN_SCALE).astype(BF16)
    k = _rope(_in_proj(hb, win_ref, OFF_K, OFF_V), cos, sin)
    v = _in_proj(hb, win_ref, OFF_V, OFF_GCONV)
    per = ATT_Q // t
    nkeys = per * (WINDOW + t)
    rq = GQA_GROUP * ATT_Q
    key_stream = lax.broadcasted_iota(I32, (nkeys, rq), 0) // (WINDOW + t)
    query_stream = (lax.broadcasted_iota(I32, (nkeys, rq), 1) % ATT_Q) // t
    mask = key_stream == query_stream
    for blk in range(bb // per):
        k_parts, v_parts = [], []
        for b in range(blk * per, (blk + 1) * per):
            kb, vb = k[b * t:(b + 1) * t], v[b * t:(b + 1) * t]
            ck, cv = ck_ref[b], cv_ref[b]
            k_ref[b] = jnp.concatenate([ck[t:WINDOW], kb], axis=0)
            v_ref[b] = jnp.concatenate([cv[t:WINDOW], vb], axis=0)
            k_parts += [ck, kb]
            v_parts += [cv, vb]
        k_all = jnp.concatenate(k_parts, axis=0).astype(BF16)
        vt_all = jnp.concatenate(v_parts, axis=0).T.astype(BF16)
        qs = q[blk * ATT_Q:(blk + 1) * ATT_Q]
        for pair in range(N_KV_HEADS // 2):
            _attend_pair(qs, k_all[:, pair * LANES:(pair + 1) * LANES], vt_all[pair * LANES:(pair + 1) * LANES, :],
                         sinks_ref, pair, mask, obuf, blk * ATT_Q)

    gate_b = _in_proj(hb, win_ref, OFF_GB, OFF_GC)
    g1f = jnp.broadcast_to(g1, (bb, t, d)).reshape(bb * t, d)
    x1_ref[...] = _mix_out(x, g1f, gate_b, conv, obuf[...], hb, win_ref, wco_ref, wao_ref, wmo_ref).reshape(bb, t, d)


def _mixer_sample(x, mod, cos, sin, ck, cv, sconv, win, wconv, wco, wao, sinks, wmo):
    b, t, d = x.shape
    bb = SAMPLE_BB
    blk = lambda *s: pl.BlockSpec((bb,) + s, lambda i: (i, 0, 0))
    return pl.pallas_call(
        _mixer_sample_kernel,
        grid=(b // bb,),
        in_specs=[blk(t, d), blk(1, 6 * d),
                  pl.BlockSpec((t, LANES), lambda i: (0, 0)), pl.BlockSpec((t, LANES), lambda i: (0, 0)),
                  blk(WINDOW, KV_DIM), blk(WINDOW, KV_DIM), blk(CONV_W - 1, D_CONV),
                  _const_spec(win.shape), _const_spec(wconv.shape), _const_spec(wco.shape), _const_spec(wao.shape),
                  pl.BlockSpec(memory_space=pltpu.SMEM),
                  _const_spec(wmo.shape)],
        out_specs=[blk(t, d), blk(CONV_W - 1, D_CONV), blk(WINDOW, KV_DIM), blk(WINDOW, KV_DIM)],
        out_shape=[jax.ShapeDtypeStruct((b, t, d), F32),
                   jax.ShapeDtypeStruct((b, CONV_W - 1, D_CONV), F32),
                   jax.ShapeDtypeStruct((b, WINDOW, KV_DIM), F32),
                   jax.ShapeDtypeStruct((b, WINDOW, KV_DIM), F32)],
        scratch_shapes=[pltpu.VMEM((bb, SUBLANES + t, D_CONV), F32),
                        pltpu.VMEM((bb * t, Q_DIM), F32)],
        compiler_params=pltpu.CompilerParams(dimension_semantics=("arbitrary",), vmem_limit_bytes=VMEM_LIMIT),
        name="mixer_sample",
    )(x, mod, cos, sin, ck, cv, sconv, win, wconv, wco, wao, sinks, wmo)


def _pre_kernel(*refs, prompt_tiles, has_sample):
    if has_sample:
        xp_ref, xs_ref, mod_ref, wsg_ref, wsu_ref, wsd_ref, wr_ref, rb_ref, h2_ref, base_ref, cw_ref = refs
    else:
        xp_ref, mod_ref, wsg_ref, wsu_ref, wsd_ref, wr_ref, rb_ref, h2_ref, base_ref, cw_ref = refs
    nc, c, d = xp_ref.shape
    t = nc * c
    x3 = xp_ref[...]
    if has_sample:
        x3 = jnp.where(pl.program_id(0) < prompt_tiles, x3, xs_ref[...])
    mod = mod_ref[...]
    sh2, sc2, g2 = mod[:, :, 0:d], mod[:, :, d:2 * d], mod[:, :, 2 * d:3 * d]
    h3 = _rms(x3) * (1.0 + sc2) + sh2
    h2 = h3.reshape(t, d)
    hb = h2.astype(BF16)
    h2_ref[...] = _pack_bf16_pairs(h2)
    shared = _dot((_silu(_dot(hb, wsg_ref[...])) * _dot(hb, wsu_ref[...])).astype(BF16), wsd_ref[...])
    base_ref[...] = x3 + g2 * shared.reshape(nc, c, d)

    logits = lax.dot_general(wr_ref[...], h2, (((1,), (1,)), ((), ())), preferred_element_type=F32,
                             precision=lax.Precision.HIGHEST)
    scores = _sigmoid(logits)
    biased = scores + rb_ref[...]
    g3 = biased.reshape(N_EXPERT_GROUPS, GROUP_SIZE, t)
    member = lax.broadcasted_iota(I32, g3.shape, 1)
    m1 = jnp.max(g3, axis=1, keepdims=True)
    first = jnp.min(jnp.where(g3 == m1, member, GROUP_SIZE), axis=1, keepdims=True)
    m2 = jnp.max(jnp.where(member == first, -jnp.inf, g3), axis=1, keepdims=True)
    gs = m1 + m2
    gidx = lax.broadcasted_iota(I32, gs.shape, 0)
    grank = jnp.zeros(gs.shape, I32)
    for o in range(N_EXPERT_GROUPS):
        other = gs[o:o + 1]
        grank += ((other > gs) | ((other == gs) & (o < gidx))).astype(I32)
    eligible = jnp.broadcast_to(grank < TOPK_GROUPS, g3.shape).reshape(N_EXPERTS, t)
    mb = jnp.where(eligible, biased, -jnp.inf)
    eidx = lax.broadcasted_iota(I32, mb.shape, 0)
    erank = jnp.zeros(mb.shape, I32)
    for o in range(N_EXPERTS):
        other = mb[o:o + 1]
        erank += ((other > mb) | ((other == mb) & (o < eidx))).astype(I32)
    sel = eligible & (erank < TOP_K)
    ssum = jnp.sum(jnp.where(sel, scores, 0.0), axis=0, keepdims=True)
    cw_ref[...] = jnp.where(sel, scores / ssum * ROUTED_SCALE, -1.0)


def _pre(x1_p, p_chunk0, ncp, x1_s, ncs, modc, wsg, wsu, wsd, wr_t, rb):
    _, c, d = x1_p.shape
    nc = PRE_TILE // c
    nchunks = ncp + ncs
    n = nchunks * c
    pt, p0 = ncp // nc, p_chunk0 // nc
    blk3 = pl.BlockSpec((nc, c, d), lambda i: (i, 0, 0))
    xs_args, xs_specs = [], []
    if ncs:
        xs_args, xs_specs = [x1_s], [pl.BlockSpec((nc, c, d), lambda i: (jnp.maximum(i - pt, 0), 0, 0))]
    return pl.pallas_call(
        functools.partial(_pre_kernel, prompt_tiles=pt, has_sample=bool(ncs)),
        grid=(nchunks // nc,),
        in_specs=[pl.BlockSpec((nc, c, d), lambda i: (p0 + jnp.minimum(i, pt - 1), 0, 0))] + xs_specs + [
                  pl.BlockSpec((nc, 1, 3 * d), lambda i: (i, 0, 0)),
                  _const_spec(wsg.shape), _const_spec(wsu.shape), _const_spec(wsd.shape),
                  _const_spec(wr_t.shape), _const_spec(rb.shape)],
        out_specs=[pl.BlockSpec((nc * c, d // 2), lambda i: (i, 0)), blk3,
                   pl.BlockSpec((N_EXPERTS, nc * c), lambda i: (0, i))],
        out_shape=[jax.ShapeDtypeStruct((n, d // 2), I32),
                   jax.ShapeDtypeStruct((nchunks, c, d), F32),
                   jax.ShapeDtypeStruct((N_EXPERTS, n), F32)],
        compiler_params=pltpu.CompilerParams(dimension_semantics=("arbitrary",), vmem_limit_bytes=VMEM_LIMIT),
        name="pre_ffn",
    )(x1_p, *xs_args, modc, wsg, wsu, wsd, wr_t, rb)


def _rank_kernel(cw_ref, rank_ref, cnt_ref, carry):
    i = pl.program_id(0)
    t = cw_ref.shape[1]

    @pl.when(i == 0)
    def _():
        carry[...] = jnp.zeros(carry.shape, F32)

    sel = (cw_ref[...] >= 0.0).astype(BF16)
    r = lax.broadcasted_iota(I32, (t, t), 0)
    c = lax.broadcasted_iota(I32, (t, t), 1)
    before = (r < c).astype(BF16)
    rank = carry[...] + _dot(sel, before)
    rank_ref[...] = rank.astype(I32)
    carry[...] = carry[...] + jnp.sum(sel.astype(F32), axis=1, keepdims=True)
    cnt_ref[...] = carry[...].astype(I32)


def _rank(cw):
    e, n = cw.shape
    t = RANK_TILE
    return pl.pallas_call(
        _rank_kernel,
        grid=(n // t,),
        in_specs=[pl.BlockSpec((e, t), lambda i: (0, i))],
        out_specs=[pl.BlockSpec((e, t), lambda i: (0, i)), pl.BlockSpec((e, 1), lambda i: (0, 0))],
        out_shape=[jax.ShapeDtypeStruct((e, n), I32), jax.ShapeDtypeStruct((e, 1), I32)],
        scratch_shapes=[pltpu.VMEM((e, 1), F32)],
        compiler_params=pltpu.CompilerParams(dimension_semantics=("arbitrary",)),
        name="expert_rank",
    )(cw)


def _slot_kernel(cw_ref, rank_ref, start_ref, pos_ref, w_ref):
    cw = cw_ref[...]
    e, t = cw.shape
    sel = cw >= 0.0
    r = lax.broadcasted_iota(I32, (e, e), 0)
    c = lax.broadcasted_iota(I32, (e, e), 1)
    lower = (c < r).astype(BF16)
    kidx = _dot(lower, sel.astype(BF16))
    posf = start_ref[...].astype(F32) + rank_ref[...].astype(F32)
    pos_rows, w_rows = [], []
    for k in range(TOP_K):
        m = sel & (kidx == float(k))
        pos_rows.append(jnp.sum(jnp.where(m, posf, 0.0), axis=0, keepdims=True))
        w_rows.append(jnp.sum(jnp.where(m, cw, 0.0), axis=0, keepdims=True))
    pos_ref[...] = jnp.concatenate(pos_rows, axis=0).astype(I32)
    w_ref[...] = jnp.concatenate(w_rows, axis=0)


def _slots(cw, rank, seg_start):
    e, n = cw.shape
    t = RANK_TILE
    return pl.pallas_call(
        _slot_kernel,
        grid=(n // t,),
        in_specs=[pl.BlockSpec((e, t), lambda i: (0, i)), pl.BlockSpec((e, t), lambda i: (0, i)),
                  pl.BlockSpec((e, 1), lambda i: (0, 0))],
        out_specs=[pl.BlockSpec((TOP_K, t), lambda i: (0, i)), pl.BlockSpec((TOP_K, t), lambda i: (0, i))],
        out_shape=[jax.ShapeDtypeStruct((TOP_K, n), I32), jax.ShapeDtypeStruct((TOP_K, n), F32)],
        compiler_params=pltpu.CompilerParams(dimension_semantics=("arbitrary",)),
        name="expert_slots",
    )(cw, rank, seg_start)


def _sc_mesh():
    return plsc.VectorSubcoreMesh(core_axis_name="c", subcore_axis_name="s")


def _sc_worker_id():
    return lax.axis_index("s") * (SC_WORKERS // 16) + lax.axis_index("c")


def _sc_dispatch(rows, pos, n_rows):
    n, d = rows.shape
    per_w = n // SC_WORKERS
    w = SC_WINDOW
    n_chunks = per_w // w

    @functools.partial(
        pl.kernel, mesh=_sc_mesh(),
        out_type=jax.ShapeDtypeStruct((n_rows, d), rows.dtype),
        scratch_types=[pltpu.VMEM((2, TOP_K, w), I32), pltpu.VMEM((2, w, d), rows.dtype),
                       pltpu.SemaphoreType.DMA((2,)), pltpu.SemaphoreType.DMA((2,)), pltpu.SemaphoreType.DMA((2,))],
        name="sc_dispatch")
    def k(rows_hbm, pos_hbm, o_hbm, idx_v, rows_v, row_sem, idx_sem, out_sem):
        wid = _sc_worker_id()
        base = wid * per_w

        def loads(c, slot):
            off = pl.multiple_of(base + c * w, SUBLANES)
            return (pltpu.make_async_copy(rows_hbm.at[pl.ds(off, w)], rows_v.at[slot], row_sem.at[slot]),
                    pltpu.make_async_copy(pos_hbm.at[wid * n_chunks + c], idx_v.at[slot], idx_sem.at[slot]))

        def scatters(slot):
            return [pltpu.make_async_copy(rows_v.at[slot], o_hbm.at[idx_v.at[slot, kk]], out_sem.at[slot])
                    for kk in range(TOP_K)]

        for cp in loads(0, 0):
            cp.start()
        for c in range(n_chunks):
            slot = c % 2
            for cp in loads(c, slot):
                cp.wait()
            for cp in scatters(slot):
                cp.start()
            if c >= 1:
                for cp in scatters(1 - slot):
                    cp.wait()
            if c + 1 < n_chunks:
                for cp in loads(c + 1, 1 - slot):
                    cp.start()
        for cp in scatters((n_chunks - 1) % 2):
            cp.wait()

    pos_chunks = pos.reshape(TOP_K, n // w, w).transpose(1, 0, 2)
    return k(rows, pos_chunks)


def _sc_collect(rows, pos_flat):
    d = rows.shape[1]
    total = pos_flat.shape[0]
    per_w = total // SC_WORKERS
    w = SC_WINDOW
    n_pairs = per_w // (2 * w)

    @functools.partial(
        pl.kernel, mesh=_sc_mesh(),
        out_type=jax.ShapeDtypeStruct((total, d), rows.dtype),
        scratch_types=[pltpu.VMEM((per_w,), I32), pltpu.VMEM((2, w, d), rows.dtype),
                       pltpu.SemaphoreType.DMA((2,)), pltpu.SemaphoreType.DMA((2,))],
        name="sc_collect")
    def k(rows_hbm, pos_hbm, o_hbm, idx_v, rows_v, in_sem, out_sem):
        base = pl.multiple_of(_sc_worker_id() * per_w, SUBLANES)
        pltpu.sync_copy(pos_hbm.at[pl.ds(base, per_w)], idx_v)

        def gather(c, slot):
            idx = idx_v.at[pl.ds(pl.multiple_of(c * w, SUBLANES), w)]
            return pltpu.make_async_copy(rows_hbm.at[idx], rows_v.at[slot], in_sem.at[slot])

        def write(c, slot):
            off = pl.multiple_of(base + c * w, SUBLANES)
            return pltpu.make_async_copy(rows_v.at[slot], o_hbm.at[pl.ds(off, w)], out_sem.at[slot])

        gather(0, 0).start()

        @pl.loop(0, n_pairs)
        def _(p):
            c0 = 2 * p
            gather(c0 + 1, 1).start()
            gather(c0, 0).wait()
            write(c0, 0).start()
            gather(c0 + 1, 1).wait()
            write(c0 + 1, 1).start()
            write(c0, 0).wait()

            @pl.when(p + 1 < n_pairs)
            def _():
                gather(c0 + 2, 0).start()

            write(c0 + 1, 1).wait()

    return k(rows, pos_flat)


def _gmm_kernel(be_ref, br_ref, nu_ref, x_ref, wg_ref, wu_ref, wd_ref, y_ref, wgb, wub, wdb):
    b = pl.program_id(0)
    prev = be_ref[jnp.maximum(b - 1, 0)]

    @pl.when((b == 0) | (be_ref[b] != prev))
    def _():
        wgb[...] = wg_ref[0].astype(BF16)
        wub[...] = wu_ref[0].astype(BF16)
        wdb[...] = wd_ref[0].astype(BF16)

    def expert_rows(r0, n):
        rows = pl.ds(r0, n)
        lo, hi = _unpack_bf16_pairs(x_ref[rows, :])
        xb = jnp.concatenate([lo.astype(BF16), hi.astype(BF16)], axis=1)
        mid = (_silu(_dot(xb, wgb[...])) * _dot(xb, wub[...])).astype(BF16)
        y_ref[rows, :] = _pack_bf16_pairs(_dot(mid, wdb[...]))

    n_real = br_ref[b]
    n_main = n_real // GMM_SUB

    @pl.loop(0, n_main)
    def _(i):
        expert_rows(pl.multiple_of(i * GMM_SUB, GMM_SUB), GMM_SUB)

    @pl.loop(0, (n_real - n_main * GMM_SUB + GMM_TAIL - 1) // GMM_TAIL)
    def _(i):
        expert_rows(pl.multiple_of(n_main * GMM_SUB + i * GMM_TAIL, GMM_TAIL), GMM_TAIL)


def _gmm(x_sorted, block_e, block_rows, n_used, wg, wu, wd):
    r, half = x_sorted.shape
    d = 2 * half
    bm = GMM_BM
    row_blk = pl.BlockSpec((bm, half), lambda b, be, br, nu: (jnp.minimum(b, nu[0] - 1), 0))
    w_spec = lambda shape: pl.BlockSpec((1,) + shape, lambda b, be, br, nu: (be[b], 0, 0))
    return pl.pallas_call(
        _gmm_kernel,
        grid_spec=pltpu.PrefetchScalarGridSpec(
            num_scalar_prefetch=3,
            grid=(r // bm,),
            in_specs=[row_blk, w_spec((d, D_EXPERT)), w_spec((d, D_EXPERT)), w_spec((D_EXPERT, d))],
            out_specs=row_blk,
            scratch_shapes=[pltpu.VMEM((d, D_EXPERT), BF16), pltpu.VMEM((d, D_EXPERT), BF16),
                            pltpu.VMEM((D_EXPERT, d), BF16)]),
        out_shape=jax.ShapeDtypeStruct((r, half), I32),
        compiler_params=pltpu.CompilerParams(dimension_semantics=("arbitrary",), vmem_limit_bytes=VMEM_LIMIT),
        name="expert_gmm",
    )(block_e, block_rows, n_used, x_sorted, wg, wu, wd)


def _combine_kernel(base_ref, mod_ref, g_ref, w_ref, gain_ref, *rest):
    y_ref = rest[-1]
    nc, c, d = base_ref.shape
    w = w_ref[...]
    acc_lo = acc_hi = None
    for k in range(TOP_K):
        lo, hi = _unpack_bf16_pairs(g_ref[k])
        wk = w[:, k:k + 1]
        acc_lo = wk * lo if k == 0 else acc_lo + wk * lo
        acc_hi = wk * hi if k == 0 else acc_hi + wk * hi
    acc = jnp.concatenate([acc_lo, acc_hi], axis=1)
    g2 = mod_ref[...][:, :, 2 * d:3 * d]
    out = base_ref[...] + g2 * acc.reshape(nc, c, d)
    y_ref[...] = _rms(out) * gain_ref[...]


def _combine(base, modc, gathered, w_tok, gain, first_chunk, n_chunks, out_chunks, out_first_chunk, out_buf=None):
    _, c, d = base.shape
    nc = COMB_TILE // c
    t = nc * c
    t0, o0 = first_chunk // nc, out_first_chunk // nc
    blk3 = pl.BlockSpec((nc, c, d), lambda i: (t0 + i, 0, 0))
    in_specs = [blk3, pl.BlockSpec((nc, 1, 3 * d), lambda i: (t0 + i, 0, 0)),
                pl.BlockSpec((TOP_K, t, d // 2), lambda i: (0, t0 + i, 0)),
                pl.BlockSpec((t, TOP_K), lambda i: (t0 + i, 0)),
                pl.BlockSpec((1, 1, d), lambda i: (0, 0, 0))]
    args = [base, modc, gathered, w_tok, gain.reshape(1, 1, d)]
    aliases = {}
    if out_buf is not None:
        in_specs.append(pl.BlockSpec(memory_space=pl.ANY))
        args.append(out_buf)
        aliases = {len(args) - 1: 0}
    return pl.pallas_call(
        _combine_kernel,
        grid=(n_chunks // nc,),
        in_specs=in_specs,
        out_specs=pl.BlockSpec((nc, c, d), lambda i: (o0 + i, 0, 0)),
        out_shape=jax.ShapeDtypeStruct((out_chunks, c, d), F32),
        input_output_aliases=aliases,
        compiler_params=pltpu.CompilerParams(dimension_semantics=("arbitrary",), vmem_limit_bytes=VMEM_LIMIT),
        name="combine_norm",
    )(*args)


def _rope_tables(pos):
    half = HEAD_DIM // 2
    inv_freq = ROPE_THETA ** (-jnp.arange(half, dtype=F32) / half)
    ang = pos.astype(F32)[:, None] * inv_freq[None, :]
    cos, sin = jnp.cos(ang), jnp.sin(ang)
    reps = LANES // HEAD_DIM
    return jnp.tile(jnp.concatenate([cos, cos], axis=1), (1, reps)), jnp.tile(jnp.concatenate([-sin, sin], axis=1), (1, reps))


def _routed_ffn(h2, cw, w_gate, w_up, w_down):
    n, half = h2.shape
    rank, counts = _rank(cw)
    bm = GMM_BM
    padded = (counts[:, 0] + bm - 1) // bm * bm
    seg_end = jnp.cumsum(padded)
    seg_start = (seg_end - padded).astype(I32)
    n_rows = n * TOP_K + N_EXPERTS * bm
    n_blocks = n_rows // bm
    block_start = jnp.arange(n_blocks, dtype=I32) * bm
    block_e = jnp.minimum(jnp.sum((seg_end[None, :] <= block_start[:, None]).astype(I32), axis=1), N_EXPERTS - 1)
    own = block_e[:, None] == jnp.arange(N_EXPERTS, dtype=I32)[None, :]
    real_end = jnp.sum(jnp.where(own, (seg_start + counts[:, 0])[None, :], 0), axis=1)
    block_rows = jnp.clip(real_end - block_start, 0, bm).astype(I32)
    n_used = (seg_end[-1:] // bm).astype(I32)
    pos, w_k = _slots(cw, rank, seg_start[:, None])
    x_sorted = _sc_dispatch(h2, pos, n_rows)
    y_sorted = _gmm(x_sorted, block_e, block_rows, n_used, w_gate, w_up, w_down)
    gathered = _sc_collect(y_sorted, pos.reshape(TOP_K * n)).reshape(TOP_K, n, half)
    return gathered, w_k


def kernel(x_prompt, x_sample, cache_k, cache_v, state_conv, c_prompt, c_sample, w_ada, b_ada, w_in, w_conv,
           w_conv_out, w_attn_o, attn_sinks, w_mix_out, w_router, router_bias, w_exp_gate, w_exp_up, w_exp_down,
           w_sh_gate, w_sh_up, w_sh_down, final_gain):
    assert w_ada.shape[0] == 1, "one layer"
    bp, seq, d = x_prompt.shape
    bs, ts, _ = x_sample.shape
    assert ts == CHUNK and seq % MIX_TILE == 0 and bs % SAMPLE_BB == 0

    c_all = jnp.concatenate([c_prompt, c_sample], axis=0)
    pad = (-c_all.shape[0]) % SUBLANES
    mod = _ada(jnp.pad(c_all, ((0, pad), (0, 0))), w_ada[0], b_ada[0])[:bp + bs]
    mod_p, mod_s = mod[:bp, None, :], mod[bp:, None, :]

    perm = _head_perm()
    w_in_l = w_in[0]
    w_in_p = jnp.concatenate([w_in_l[:, :OFF_Q], w_in_l[:, OFF_Q:OFF_K][:, perm], w_in_l[:, OFF_K:]], axis=1)
    win, wco, wao, wmo = (w.astype(BF16) for w in (w_in_p, w_conv_out[0], w_attn_o[0][perm], w_mix_out[0]))
    cos_p, sin_p = _rope_tables(jnp.arange(seq, dtype=I32))
    cos_s, sin_s = _rope_tables(PAST_LEN + jnp.arange(ts, dtype=I32))

    x1_p, conv_p, k_p, v_p = _mixer_prompt(x_prompt, mod_p, cos_p, sin_p, win, w_conv[0], wco, wao, attn_sinks[0], wmo)
    x1_s, conv_s, k_s, v_s = _mixer_sample(
        x_sample, mod_s, cos_s, sin_s, cache_k[0].reshape(bs, WINDOW, KV_DIM), cache_v[0].reshape(bs, WINDOW, KV_DIM),
        state_conv[0], win, w_conv[0], wco, wao, attn_sinks[0], wmo)

    n_p, n_s = bp * seq, bs * ts
    n = n_p + n_s
    mod2 = mod[:, None, 3 * d:]
    modc_p, modc_s = jnp.repeat(mod2[:bp], seq // CHUNK, axis=0), mod2[bp:]
    x1_pc = x1_p.reshape(n_p // CHUNK, CHUNK, d)
    wsg, wsu, wsd = (w[0].astype(BF16) for w in (w_sh_gate, w_sh_up, w_sh_down))
    wr_t, rb = w_router[0].T, router_bias[0][:, None]

    ncp, ncs = n_p // CHUNK, n_s // CHUNK
    half = (ncp + ncs) // 2
    tile_chunks = max(PRE_TILE, COMB_TILE) // CHUNK
    assert half <= ncp and half % tile_chunks == 0 and (ncp - half) % tile_chunks == 0 and ncs % tile_chunks == 0
    assert (half * CHUNK) % (SC_WORKERS * SC_WINDOW) == 0 and (half * CHUNK) % RANK_TILE == 0
    y_p = None
    for p0, np_c, ns_c in ((0, half, 0), (half, ncp - half, ncs)):
        modc = jnp.concatenate([modc_p[p0:p0 + np_c], modc_s[:ns_c]], axis=0)
        h2, base, cw = _pre(x1_pc, p0, np_c, x1_s, ns_c, modc, wsg, wsu, wsd, wr_t, rb)
        gathered, w_k = _routed_ffn(h2, cw, w_exp_gate[0], w_exp_up[0], w_exp_down[0])
        w_tok = w_k.T
        y_p = _combine(base, modc, gathered, w_tok, final_gain, 0, np_c, ncp, p0, out_buf=y_p)
        if ns_c:
            y_s = _combine(base, modc, gathered, w_tok, final_gain, np_c, ns_c, ncs, 0)

    kv = lambda a: a.reshape(1, a.shape[0], WINDOW, N_KV_HEADS, HEAD_DIM)
    return (y_p.reshape(bp, seq, d), y_s, conv_p[None], kv(k_p), kv(v_p), conv_s[None], kv(k_s), kv(v_s))
```

```python
import functools

import numpy as np
import jax
import jax.numpy as jnp
from jax import lax
from jax.experimental import pallas as pl
from jax.experimental.pallas import tpu as pltpu
from jax.experimental.pallas import tpu_sc as plsc

F32 = jnp.float32
BF16 = jnp.bfloat16
I32 = jnp.int32

D_MODEL = 1024
CHUNK = 64
D_CONV = 1024
CONV_W = 3
N_HEADS = 16
N_KV_HEADS = 4
HEAD_DIM = 64
GQA_GROUP = N_HEADS // N_KV_HEADS
WINDOW = 128
ROPE_THETA = 10000.0
ATTN_SCALE = HEAD_DIM ** -0.5
N_EXPERTS = 64
TOP_K = 8
N_EXPERT_GROUPS = 8
GROUP_SIZE = N_EXPERTS // N_EXPERT_GROUPS
TOPK_GROUPS = 4
D_EXPERT = 256
D_SHARED = 256
ROUTED_SCALE = 2.5
EPS = 1e-6
PAST_LEN = 4096
Q_DIM = N_HEADS * HEAD_DIM
KV_DIM = N_KV_HEADS * HEAD_DIM
OFF_GB, OFF_GC, OFF_XC, OFF_Q, OFF_K, OFF_V, OFF_GCONV, OFF_GATTN, D_IN = (
    0, 1024, 2048, 3072, 4096, 4352, 4608, 5632, 6656)

LANES = 128
SUBLANES = 8
VMEM_LIMIT = 56 * 1024 * 1024

MIX_TILE = 512
ATT_Q = 128
SAMPLE_BB = 4
PRE_TILE = 512
RANK_TILE = 512
GMM_BM = 512
GMM_SUB = 512
GMM_TAIL = 128
COMB_TILE = 256
SC_WORKERS = 32
SC_WINDOW = 96


def _const_spec(shape):
    nd = len(shape)
    return pl.BlockSpec(shape, lambda *_: (0,) * nd, pipeline_mode=pl.Buffered(1))


def _rms(x):
    return x * lax.rsqrt(jnp.mean(x * x, axis=-1, keepdims=True) + EPS)


def _sigmoid(x):
    return 1.0 / (1.0 + jnp.exp(-x))


def _silu(x):
    return x * _sigmoid(x)


def _dot(a, b):
    return jnp.dot(a, b, preferred_element_type=F32)


def _pack_bf16_pairs(x):
    half = x.shape[-1] // 2
    lo = lax.bitcast_convert_type(x[..., :half].astype(BF16).astype(F32), I32)
    hi = lax.bitcast_convert_type(x[..., half:].astype(BF16).astype(F32), I32)
    return lax.shift_right_logical(lo, 16) | hi


def _unpack_bf16_pairs(words):
    lo = lax.bitcast_convert_type(lax.shift_left(words, 16), F32)
    hi = lax.bitcast_convert_type(words & jnp.int32(-65536), F32)
    return lo, hi


def _ada_kernel(c_ref, w_ref, b_ref, o_ref):
    s = _silu(c_ref[...]).astype(BF16)
    o_ref[...] = _dot(s, w_ref[...].astype(BF16)) + b_ref[...]


def _ada(c_all, w_ada, b_ada):
    rows = c_all.shape[0]
    n_out = w_ada.shape[1]
    bn = 768
    return pl.pallas_call(
        _ada_kernel,
        grid=(n_out // bn,),
        in_specs=[pl.BlockSpec((rows, D_MODEL), lambda i: (0, 0)),
                  pl.BlockSpec((D_MODEL, bn), lambda i: (0, i)),
                  pl.BlockSpec((1, bn), lambda i: (0, i))],
        out_specs=pl.BlockSpec((rows, bn), lambda i: (0, i)),
        out_shape=jax.ShapeDtypeStruct((rows, n_out), F32),
        name="ada_mod",
    )(c_all, w_ada, b_ada.reshape(1, n_out))


def _rope(x, cos, sin_signed):
    lane = lax.broadcasted_iota(I32, (x.shape[0], LANES), 1)
    first_half = (lane % HEAD_DIM) < (HEAD_DIM // 2)
    outs = []
    for g in range(x.shape[1] // LANES):
        xg = x[:, g * LANES:(g + 1) * LANES]
        up = pltpu.roll(xg, LANES - HEAD_DIM // 2, axis=1)
        down = pltpu.roll(xg, HEAD_DIM // 2, axis=1)
        partner = jnp.where(first_half, up, down)
        outs.append(xg * cos + partner * sin_signed)
    return jnp.concatenate(outs, axis=1)


def _head_perm():
    n = np.arange(Q_DIM)
    c, par, dd = n // LANES, (n % LANES) // HEAD_DIM, n % HEAD_DIM
    pair, i = c // GQA_GROUP, c % GQA_GROUP
    return HEAD_DIM * (GQA_GROUP * (2 * pair + par) + i) + dd


def _attend_pair(q_blk, k_pair, vt_pair, sinks_ref, pair, mask, obuf, row0):
    rq = GQA_GROUP * ATT_Q
    low = lax.broadcasted_iota(I32, (ATT_Q, LANES), 1) < HEAD_DIM
    head_of_lane = lax.broadcasted_iota(I32, (1, rq), 1) // ATT_Q
    outs = []
    for par in range(2):
        g = 2 * pair + par
        keep = low if par == 0 else jnp.logical_not(low)
        cols = [q_blk[:, (GQA_GROUP * pair + i) * LANES:(GQA_GROUP * pair + i + 1) * LANES] for i in range(GQA_GROUP)]
        qg = jnp.concatenate([jnp.where(keep, c, jnp.zeros_like(c)) for c in cols], axis=0)
        st = lax.dot_general(k_pair, qg, (((1,), (1,)), ((), ())), preferred_element_type=F32)
        if mask is not None:
            st = jnp.where(mask, st, -jnp.inf)
        sink = jnp.full((1, rq), sinks_ref[g * GQA_GROUP + GQA_GROUP - 1], F32)
        for i in range(GQA_GROUP - 2, -1, -1):
            sink = jnp.where(head_of_lane == i, sinks_ref[g * GQA_GROUP + i], sink)
        m = jnp.maximum(jnp.max(st, axis=0, keepdims=True), sink)
        e = jnp.exp(st - m)
        z = jnp.sum(e, axis=0, keepdims=True) + jnp.exp(sink - m)
        ot = _dot(vt_pair[par * HEAD_DIM:(par + 1) * HEAD_DIM, :], e.astype(BF16))
        outs.append(ot / z)
    for i in range(GQA_GROUP):
        blk = jnp.concatenate([o[:, i * ATT_Q:(i + 1) * ATT_Q] for o in outs], axis=0)
        c0 = (GQA_GROUP * pair + i) * LANES
        obuf[row0:row0 + ATT_Q, c0:c0 + LANES] = blk.T


def _in_proj(hb, win_ref, lo, hi):
    return _dot(hb, win_ref[:, lo:hi])


def _mix_out(x, g1, proj_gb, conv, y_attn_in, hb, win_ref, wco_ref, wao_ref, wmo_ref):
    y_conv = _dot((proj_gb * conv).astype(BF16), wco_ref[...])
    y_attn = _dot(y_attn_in.astype(BF16), wao_ref[...])
    g_conv = _in_proj(hb, win_ref, OFF_GCONV, OFF_GATTN)
    g_attn = _in_proj(hb, win_ref, OFF_GATTN, D_IN)
    merged = _sigmoid(g_conv) * y_conv + _sigmoid(g_attn) * y_attn
    return x + g1 * _dot(merged.astype(BF16), wmo_ref[...])


def _mixer_prompt_kernel(x_ref, mod_ref, cos_ref, sin_ref, win_ref, wconv_ref, wco_ref, wao_ref, sinks_ref, wmo_ref,
                         x1_ref, conv_ref, k_ref, v_ref, ubuf, kbuf, vtbuf, obuf):
    j = pl.program_id(1)
    t = x_ref.shape[1]

    @pl.when(j == 0)
    def _():
        ubuf[0:SUBLANES, :] = jnp.zeros((SUBLANES, D_CONV), F32)
        kbuf[0:WINDOW, :] = jnp.zeros((WINDOW, KV_DIM), BF16)
        vtbuf[:, 0:WINDOW] = jnp.zeros((KV_DIM, WINDOW), BF16)

    x = x_ref[0]
    mod = mod_ref[0]
    sh1, sc1, g1 = mod[:, 0:D_MODEL], mod[:, D_MODEL:2 * D_MODEL], mod[:, 2 * D_MODEL:3 * D_MODEL]
    hb = (_rms(x) * (1.0 + sc1) + sh1).astype(BF16)

    u = _in_proj(hb, win_ref, OFF_GC, OFF_XC) * _in_proj(hb, win_ref, OFF_XC, OFF_Q)
    ubuf[SUBLANES:SUBLANES + t, :] = u
    wc = wconv_ref[...]
    conv = wc[0:1] * ubuf[SUBLANES - 2:SUBLANES - 2 + t, :] + wc[1:2] * ubuf[SUBLANES - 1:SUBLANES - 1 + t, :] + wc[2:3] * u
    conv_ref[0] = u[t - (CONV_W - 1):t]
    ubuf[SUBLANES - 2:SUBLANES, :] = u[t - (CONV_W - 1):t]

    cos, sin = cos_ref[...], sin_ref[...]
    q = (_rope(_in_proj(hb, win_ref, OFF_Q, OFF_K), cos, sin) * ATTN_SCALE).astype(BF16)
    k = _rope(_in_proj(hb, win_ref, OFF_K, OFF_V), cos, sin)
    v = _in_proj(hb, win_ref, OFF_V, OFF_GCONV)
    kbuf[WINDOW:WINDOW + t, :] = k.astype(BF16)
    vtbuf[:, WINDOW:WINDOW + t] = v.T.astype(BF16)
    k_ref[0] = k[t - WINDOW:t]
    v_ref[0] = v[t - WINDOW:t]

    nkeys = ATT_Q + WINDOW
    rq = GQA_GROUP * ATT_Q
    ki = lax.broadcasted_iota(I32, (nkeys, rq), 0)
    qi = lax.broadcasted_iota(I32, (nkeys, rq), 1) % ATT_Q
    band = ki // CHUNK - qi // CHUNK
    band_ok = (band >= 0) & (band <= WINDOW // CHUNK)
    for s in range(t // ATT_Q):
        mask = band_ok & (ki + (j * t + s * ATT_Q - WINDOW) >= 0)
        qs = q[s * ATT_Q:(s + 1) * ATT_Q]
        for pair in range(N_KV_HEADS // 2):
            k_pair = kbuf[s * ATT_Q:s * ATT_Q + nkeys, pair * LANES:(pair + 1) * LANES]
            vt_pair = vtbuf[pair * LANES:(pair + 1) * LANES, s * ATT_Q:s * ATT_Q + nkeys]
            _attend_pair(qs, k_pair, vt_pair, sinks_ref, pair, mask, obuf, s * ATT_Q)
    kbuf[0:WINDOW, :] = kbuf[t:t + WINDOW, :]
    vtbuf[:, 0:WINDOW] = vtbuf[:, t:t + WINDOW]

    gate_b = _in_proj(hb, win_ref, OFF_GB, OFF_GC)
    x1_ref[0] = _mix_out(x, g1, gate_b, conv, obuf[...], hb, win_ref, wco_ref, wao_ref, wmo_ref)


def _mixer_prompt(x, mod, cos, sin, win, wconv, wco, wao, sinks, wmo):
    b, seq, d = x.shape
    t = MIX_TILE
    return pl.pallas_call(
        _mixer_prompt_kernel,
        grid=(b, seq // t),
        in_specs=[pl.BlockSpec((1, t, d), lambda i, j: (i, j, 0)),
                  pl.BlockSpec((1, 1, 6 * d), lambda i, j: (i, 0, 0)),
                  pl.BlockSpec((t, LANES), lambda i, j: (j, 0)),
                  pl.BlockSpec((t, LANES), lambda i, j: (j, 0)),
                  _const_spec(win.shape), _const_spec(wconv.shape), _const_spec(wco.shape), _const_spec(wao.shape),
                  pl.BlockSpec(memory_space=pltpu.SMEM),
                  _const_spec(wmo.shape)],
        out_specs=[pl.BlockSpec((1, t, d), lambda i, j: (i, j, 0)),
                   pl.BlockSpec((1, CONV_W - 1, D_CONV), lambda i, j: (i, 0, 0)),
                   pl.BlockSpec((1, WINDOW, KV_DIM), lambda i, j: (i, 0, 0)),
                   pl.BlockSpec((1, WINDOW, KV_DIM), lambda i, j: (i, 0, 0))],
        out_shape=[jax.ShapeDtypeStruct((b, seq, d), F32),
                   jax.ShapeDtypeStruct((b, CONV_W - 1, D_CONV), F32),
                   jax.ShapeDtypeStruct((b, WINDOW, KV_DIM), F32),
                   jax.ShapeDtypeStruct((b, WINDOW, KV_DIM), F32)],
        scratch_shapes=[pltpu.VMEM((SUBLANES + t, D_CONV), F32),
                        pltpu.VMEM((WINDOW + t, KV_DIM), BF16),
                        pltpu.VMEM((KV_DIM, WINDOW + t), BF16),
                        pltpu.VMEM((t, Q_DIM), F32)],
        compiler_params=pltpu.CompilerParams(dimension_semantics=("arbitrary", "arbitrary"),
                                             vmem_limit_bytes=VMEM_LIMIT),
        name="mixer_prompt",
    )(x, mod, cos, sin, win, wconv, wco, wao, sinks, wmo)


def _mixer_sample_kernel(x_ref, mod_ref, cos_ref, sin_ref, ck_ref, cv_ref, sconv_ref, win_ref, wconv_ref, wco_ref,
                         wao_ref, sinks_ref, wmo_ref, x1_ref, conv_ref, k_ref, v_ref, ubuf, obuf):
    bb, t, d = x_ref.shape
    x3 = x_ref[...]
    mod = mod_ref[...]
    sh1, sc1, g1 = mod[:, :, 0:d], mod[:, :, d:2 * d], mod[:, :, 2 * d:3 * d]
    x = x3.reshape(bb * t, d)
    hb = (_rms(x3) * (1.0 + sc1) + sh1).astype(BF16).reshape(bb * t, d)

    u = _in_proj(hb, win_ref, OFF_GC, OFF_XC) * _in_proj(hb, win_ref, OFF_XC, OFF_Q)
    u3 = u.reshape(bb, t, D_CONV)
    ubuf[:, SUBLANES - 2:SUBLANES, :] = sconv_ref[...]
    ubuf[:, SUBLANES:SUBLANES + t, :] = u3
    wc = wconv_ref[...]
    conv = (wc[0:1] * ubuf[:, SUBLANES - 2:SUBLANES - 2 + t, :] + wc[1:2] * ubuf[:, SUBLANES - 1:SUBLANES - 1 + t, :]
            + wc[2:3] * u3).reshape(bb * t, D_CONV)
    conv_ref[...] = u3[:, t - (CONV_W - 1):t, :]

    cos = jnp.concatenate([cos_ref[...]] * bb, axis=0)
    sin = jnp.concatenate([sin_ref[...]] * bb, axis=0)
    q = (_rope(_in_proj(hb, win_ref, OFF_Q, OFF_K), cos, sin) * ATTN_SCALE).astype(BF16)
    k = _rope(_in_proj(hb, win_ref, OFF_K, OFF_V), cos, sin)
    v = _in_proj(hb, win_ref, OFF_V, OFF_GCONV)
    per = ATT_Q // t
    nkeys = per * (WINDOW + t)
    rq = GQA_GROUP * ATT_Q
    key_stream = lax.broadcasted_iota(I32, (nkeys, rq), 0) // (WINDOW + t)
    query_stream = (lax.broadcasted_iota(I32, (nkeys, rq), 1) % ATT_Q) // t
    mask = key_stream == query_stream
    for blk in range(bb // per):
        k_parts, v_parts = [], []
        for b in range(blk * per, (blk + 1) * per):
            kb, vb = k[b * t:(b + 1) * t], v[b * t:(b + 1) * t]
            ck, cv = ck_ref[b], cv_ref[b]
            k_ref[b] = jnp.concatenate([ck[t:WINDOW], kb], axis=0)
            v_ref[b] = jnp.concatenate([cv[t:WINDOW], vb], axis=0)
            k_parts += [ck, kb]
            v_parts += [cv, vb]
        k_all = jnp.concatenate(k_parts, axis=0).astype(BF16)
        vt_all = jnp.concatenate(v_parts, axis=0).T.astype(BF16)
        qs = q[blk * ATT_Q:(blk + 1) * ATT_Q]
        for pair in range(N_KV_HEADS // 2):
            _attend_pair(qs, k_all[:, pair * LANES:(pair + 1) * LANES], vt_all[pair * LANES:(pair + 1) * LANES, :],
                         sinks_ref, pair, mask, obuf, blk * ATT_Q)

    gate_b = _in_proj(hb, win_ref, OFF_GB, OFF_GC)
    g1f = jnp.broadcast_to(g1, (bb, t, d)).reshape(bb * t, d)
    x1_ref[...] = _mix_out(x, g1f, gate_b, conv, obuf[...], hb, win_ref, wco_ref, wao_ref, wmo_ref).reshape(bb, t, d)


def _mixer_sample(x, mod, cos, sin, ck, cv, sconv, win, wconv, wco, wao, sinks, wmo):
    b, t, d = x.shape
    bb = SAMPLE_BB
    blk = lambda *s: pl.BlockSpec((bb,) + s, lambda i: (i, 0, 0))
    return pl.pallas_call(
        _mixer_sample_kernel,
        grid=(b // bb,),
        in_specs=[blk(t, d), blk(1, 6 * d),
                  pl.BlockSpec((t, LANES), lambda i: (0, 0)), pl.BlockSpec((t, LANES), lambda i: (0, 0)),
                  blk(WINDOW, KV_DIM), blk(WINDOW, KV_DIM), blk(CONV_W - 1, D_CONV),
                  _const_spec(win.shape), _const_spec(wconv.shape), _const_spec(wco.shape), _const_spec(wao.shape),
                  pl.BlockSpec(memory_space=pltpu.SMEM),
                  _const_spec(wmo.shape)],
        out_specs=[blk(t, d), blk(CONV_W - 1, D_CONV), blk(WINDOW, KV_DIM), blk(WINDOW, KV_DIM)],
        out_shape=[jax.ShapeDtypeStruct((b, t, d), F32),
                   jax.ShapeDtypeStruct((b, CONV_W - 1, D_CONV), F32),
                   jax.ShapeDtypeStruct((b, WINDOW, KV_DIM), F32),
                   jax.ShapeDtypeStruct((b, WINDOW, KV_DIM), F32)],
        scratch_shapes=[pltpu.VMEM((bb, SUBLANES + t, D_CONV), F32),
                        pltpu.VMEM((bb * t, Q_DIM), F32)],
        compiler_params=pltpu.CompilerParams(dimension_semantics=("arbitrary",), vmem_limit_bytes=VMEM_LIMIT),
        name="mixer_sample",
    )(x, mod, cos, sin, ck, cv, sconv, win, wconv, wco, wao, sinks, wmo)


def _pre_kernel(*refs, prompt_tiles, has_sample):
    if has_sample:
        xp_ref, xs_ref, mod_ref, wsg_ref, wsu_ref, wsd_ref, wr_ref, rb_ref, h2_ref, base_ref, cw_ref = refs
    else:
        xp_ref, mod_ref, wsg_ref, wsu_ref, wsd_ref, wr_ref, rb_ref, h2_ref, base_ref, cw_ref = refs
    nc, c, d = xp_ref.shape
    t = nc * c
    x3 = xp_ref[...]
    if has_sample:
        x3 = jnp.where(pl.program_id(0) < prompt_tiles, x3, xs_ref[...])
    mod = mod_ref[...]
    sh2, sc2, g2 = mod[:, :, 0:d], mod[:, :, d:2 * d], mod[:, :, 2 * d:3 * d]
    h3 = _rms(x3) * (1.0 + sc2) + sh2
    h2 = h3.reshape(t, d)
    hb = h2.astype(BF16)
    h2_ref[...] = _pack_bf16_pairs(h2)
    shared = _dot((_silu(_dot(hb, wsg_ref[...])) * _dot(hb, wsu_ref[...])).astype(BF16), wsd_ref[...])
    base_ref[...] = x3 + g2 * shared.reshape(nc, c, d)

    logits = lax.dot_general(wr_ref[...], h2, (((1,), (1,)), ((), ())), preferred_element_type=F32,
                             precision=lax.Precision.HIGHEST)
    scores = _sigmoid(logits)
    biased = scores + rb_ref[...]
    g3 = biased.reshape(N_EXPERT_GROUPS, GROUP_SIZE, t)
    member = lax.broadcasted_iota(I32, g3.shape, 1)
    m1 = jnp.max(g3, axis=1, keepdims=True)
    first = jnp.min(jnp.where(g3 == m1, member, GROUP_SIZE), axis=1, keepdims=True)
    m2 = jnp.max(jnp.where(member == first, -jnp.inf, g3), axis=1, keepdims=True)
    gs = m1 + m2
    gidx = lax.broadcasted_iota(I32, gs.shape, 0)
    grank = jnp.zeros(gs.shape, I32)
    for o in range(N_EXPERT_GROUPS):
        other = gs[o:o + 1]
        grank += ((other > gs) | ((other == gs) & (o < gidx))).astype(I32)
    eligible = jnp.broadcast_to(grank < TOPK_GROUPS, g3.shape).reshape(N_EXPERTS, t)
    mb = jnp.where(eligible, biased, -jnp.inf)
    eidx = lax.broadcasted_iota(I32, mb.shape, 0)
    erank = jnp.zeros(mb.shape, I32)
    for o in range(N_EXPERTS):
        other = mb[o:o + 1]
        erank += ((other > mb) | ((other == mb) & (o < eidx))).astype(I32)
    sel = eligible & (erank < TOP_K)
    ssum = jnp.sum(jnp.where(sel, scores, 0.0), axis=0, keepdims=True)
    cw_ref[...] = jnp.where(sel, scores / ssum * ROUTED_SCALE, -1.0)


def _pre(x1_p, p_chunk0, ncp, x1_s, ncs, modc, wsg, wsu, wsd, wr_t, rb):
    _, c, d = x1_p.shape
    nc = PRE_TILE // c
    nchunks = ncp + ncs
    n = nchunks * c
    pt, p0 = ncp // nc, p_chunk0 // nc
    blk3 = pl.BlockSpec((nc, c, d), lambda i: (i, 0, 0))
    xs_args, xs_specs = [], []
    if ncs:
        xs_args, xs_specs = [x1_s], [pl.BlockSpec((nc, c, d), lambda i: (jnp.maximum(i - pt, 0), 0, 0))]
    return pl.pallas_call(
        functools.partial(_pre_kernel, prompt_tiles=pt, has_sample=bool(ncs)),
        grid=(nchunks // nc,),
        in_specs=[pl.BlockSpec((nc, c, d), lambda i: (p0 + jnp.minimum(i, pt - 1), 0, 0))] + xs_specs + [
                  pl.BlockSpec((nc, 1, 3 * d), lambda i: (i, 0, 0)),
                  _const_spec(wsg.shape), _const_spec(wsu.shape), _const_spec(wsd.shape),
                  _const_spec(wr_t.shape), _const_spec(rb.shape)],
        out_specs=[pl.BlockSpec((nc * c, d // 2), lambda i: (i, 0)), blk3,
                   pl.BlockSpec((N_EXPERTS, nc * c), lambda i: (0, i))],
        out_shape=[jax.ShapeDtypeStruct((n, d // 2), I32),
                   jax.ShapeDtypeStruct((nchunks, c, d), F32),
                   jax.ShapeDtypeStruct((N_EXPERTS, n), F32)],
        compiler_params=pltpu.CompilerParams(dimension_semantics=("arbitrary",), vmem_limit_bytes=VMEM_LIMIT),
        name="pre_ffn",
    )(x1_p, *xs_args, modc, wsg, wsu, wsd, wr_t, rb)


def _rank_kernel(cw_ref, rank_ref, cnt_ref, carry):
    i = pl.program_id(0)
    t = cw_ref.shape[1]

    @pl.when(i == 0)
    def _():
        carry[...] = jnp.zeros(carry.shape, F32)

    sel = (cw_ref[...] >= 0.0).astype(BF16)
    r = lax.broadcasted_iota(I32, (t, t), 0)
    c = lax.broadcasted_iota(I32, (t, t), 1)
    before = (r < c).astype(BF16)
    rank = carry[...] + _dot(sel, before)
    rank_ref[...] = rank.astype(I32)
    carry[...] = carry[...] + jnp.sum(sel.astype(F32), axis=1, keepdims=True)
    cnt_ref[...] = carry[...].astype(I32)


def _rank(cw):
    e, n = cw.shape
    t = RANK_TILE
    return pl.pallas_call(
        _rank_kernel,
        grid=(n // t,),
        in_specs=[pl.BlockSpec((e, t), lambda i: (0, i))],
        out_specs=[pl.BlockSpec((e, t), lambda i: (0, i)), pl.BlockSpec((e, 1), lambda i: (0, 0))],
        out_shape=[jax.ShapeDtypeStruct((e, n), I32), jax.ShapeDtypeStruct((e, 1), I32)],
        scratch_shapes=[pltpu.VMEM((e, 1), F32)],
        compiler_params=pltpu.CompilerParams(dimension_semantics=("arbitrary",)),
        name="expert_rank",
    )(cw)


def _slot_kernel(cw_ref, rank_ref, start_ref, pos_ref, w_ref):
    cw = cw_ref[...]
    e, t = cw.shape
    sel = cw >= 0.0
    r = lax.broadcasted_iota(I32, (e, e), 0)
    c = lax.broadcasted_iota(I32, (e, e), 1)
    lower = (c < r).astype(BF16)
    kidx = _dot(lower, sel.astype(BF16))
    posf = start_ref[...].astype(F32) + rank_ref[...].astype(F32)
    pos_rows, w_rows = [], []
    for k in range(TOP_K):
        m = sel & (kidx == float(k))
        pos_rows.append(jnp.sum(jnp.where(m, posf, 0.0), axis=0, keepdims=True))
        w_rows.append(jnp.sum(jnp.where(m, cw, 0.0), axis=0, keepdims=True))
    pos_ref[...] = jnp.concatenate(pos_rows, axis=0).astype(I32)
    w_ref[...] = jnp.concatenate(w_rows, axis=0)


def _slots(cw, rank, seg_start):
    e, n = cw.shape
    t = RANK_TILE
    return pl.pallas_call(
        _slot_kernel,
        grid=(n // t,),
        in_specs=[pl.BlockSpec((e, t), lambda i: (0, i)), pl.BlockSpec((e, t), lambda i: (0, i)),
                  pl.BlockSpec((e, 1), lambda i: (0, 0))],
        out_specs=[pl.BlockSpec((TOP_K, t), lambda i: (0, i)), pl.BlockSpec((TOP_K, t), lambda i: (0, i))],
        out_shape=[jax.ShapeDtypeStruct((TOP_K, n), I32), jax.ShapeDtypeStruct((TOP_K, n), F32)],
        compiler_params=pltpu.CompilerParams(dimension_semantics=("arbitrary",)),
        name="expert_slots",
    )(cw, rank, seg_start)


def _sc_mesh():
    return plsc.VectorSubcoreMesh(core_axis_name="c", subcore_axis_name="s")


def _sc_worker_id():
    return lax.axis_index("s") * (SC_WORKERS // 16) + lax.axis_index("c")


def _sc_dispatch(rows, pos, n_rows):
    n, d = rows.shape
    per_w = n // SC_WORKERS
    w = SC_WINDOW
    n_chunks = per_w // w

    @functools.partial(
        pl.kernel, mesh=_sc_mesh(),
        out_type=jax.ShapeDtypeStruct((n_rows, d), rows.dtype),
        scratch_types=[pltpu.VMEM((2, TOP_K, w), I32), pltpu.VMEM((2, w, d), rows.dtype),
                       pltpu.SemaphoreType.DMA((2,)), pltpu.SemaphoreType.DMA((2,)), pltpu.SemaphoreType.DMA((2,))],
        name="sc_dispatch")
    def k(rows_hbm, pos_hbm, o_hbm, idx_v, rows_v, row_sem, idx_sem, out_sem):
        wid = _sc_worker_id()
        base = wid * per_w

        def loads(c, slot):
            off = pl.multiple_of(base + c * w, SUBLANES)
            return (pltpu.make_async_copy(rows_hbm.at[pl.ds(off, w)], rows_v.at[slot], row_sem.at[slot]),
                    pltpu.make_async_copy(pos_hbm.at[wid * n_chunks + c], idx_v.at[slot], idx_sem.at[slot]))

        def scatters(slot):
            return [pltpu.make_async_copy(rows_v.at[slot], o_hbm.at[idx_v.at[slot, kk]], out_sem.at[slot])
                    for kk in range(TOP_K)]

        for cp in loads(0, 0):
            cp.start()
        for c in range(n_chunks):
            slot = c % 2
            for cp in loads(c, slot):
                cp.wait()
            for cp in scatters(slot):
                cp.start()
            if c >= 1:
                for cp in scatters(1 - slot):
                    cp.wait()
            if c + 1 < n_chunks:
                for cp in loads(c + 1, 1 - slot):
                    cp.start()
        for cp in scatters((n_chunks - 1) % 2):
            cp.wait()

    pos_chunks = pos.reshape(TOP_K, n // w, w).transpose(1, 0, 2)
    return k(rows, pos_chunks)


def _sc_collect(rows, pos_flat):
    d = rows.shape[1]
    total = pos_flat.shape[0]
    per_w = total // SC_WORKERS
    w = SC_WINDOW
    n_pairs = per_w // (2 * w)

    @functools.partial(
        pl.kernel, mesh=_sc_mesh(),
        out_type=jax.ShapeDtypeStruct((total, d), rows.dtype),
        scratch_types=[pltpu.VMEM((per_w,), I32), pltpu.VMEM((2, w, d), rows.dtype),
                       pltpu.SemaphoreType.DMA((2,)), pltpu.SemaphoreType.DMA((2,))],
        name="sc_collect")
    def k(rows_hbm, pos_hbm, o_hbm, idx_v, rows_v, in_sem, out_sem):
        base = pl.multiple_of(_sc_worker_id() * per_w, SUBLANES)
        pltpu.sync_copy(pos_hbm.at[pl.ds(base, per_w)], idx_v)

        def gather(c, slot):
            idx = idx_v.at[pl.ds(pl.multiple_of(c * w, SUBLANES), w)]
            return pltpu.make_async_copy(rows_hbm.at[idx], rows_v.at[slot], in_sem.at[slot])

        def write(c, slot):
            off = pl.multiple_of(base + c * w, SUBLANES)
            return pltpu.make_async_copy(rows_v.at[slot], o_hbm.at[pl.ds(off, w)], out_sem.at[slot])

        gather(0, 0).start()

        @pl.loop(0, n_pairs)
        def _(p):
            c0 = 2 * p
            gather(c0 + 1, 1).start()
            gather(c0, 0).wait()
            write(c0, 0).start()
            gather(c0 + 1, 1).wait()
            write(c0 + 1, 1).start()
            write(c0, 0).wait()

            @pl.when(p + 1 < n_pairs)
            def _():
                gather(c0 + 2, 0).start()

            write(c0 + 1, 1).wait()

    return k(rows, pos_flat)


def _gmm_kernel(be_ref, br_ref, nu_ref, x_ref, wg_ref, wu_ref, wd_ref, y_ref, wgb, wub, wdb):
    b = pl.program_id(0)
    prev = be_ref[jnp.maximum(b - 1, 0)]

    @pl.when((b == 0) | (be_ref[b] != prev))
    def _():
        wgb[...] = wg_ref[0].astype(BF16)
        wub[...] = wu_ref[0].astype(BF16)
        wdb[...] = wd_ref[0].astype(BF16)

    def expert_rows(r0, n):
        rows = pl.ds(r0, n)
        lo, hi = _unpack_bf16_pairs(x_ref[rows, :])
        xb = jnp.concatenate([lo.astype(BF16), hi.astype(BF16)], axis=1)
        mid = (_silu(_dot(xb, wgb[...])) * _dot(xb, wub[...])).astype(BF16)
        y_ref[rows, :] = _pack_bf16_pairs(_dot(mid, wdb[...]))

    n_real = br_ref[b]
    n_main = n_real // GMM_SUB

    @pl.loop(0, n_main)
    def _(i):
        expert_rows(pl.multiple_of(i * GMM_SUB, GMM_SUB), GMM_SUB)

    @pl.loop(0, (n_real - n_main * GMM_SUB + GMM_TAIL - 1) // GMM_TAIL)
    def _(i):
        expert_rows(pl.multiple_of(n_main * GMM_SUB + i * GMM_TAIL, GMM_TAIL), GMM_TAIL)


def _gmm(x_sorted, block_e, block_rows, n_used, wg, wu, wd):
    r, half = x_sorted.shape
    d = 2 * half
    bm = GMM_BM
    row_blk = pl.BlockSpec((bm, half), lambda b, be, br, nu: (jnp.minimum(b, nu[0] - 1), 0))
    w_spec = lambda shape: pl.BlockSpec((1,) + shape, lambda b, be, br, nu: (be[b], 0, 0))
    return pl.pallas_call(
        _gmm_kernel,
        grid_spec=pltpu.PrefetchScalarGridSpec(
            num_scalar_prefetch=3,
            grid=(r // bm,),
            in_specs=[row_blk, w_spec((d, D_EXPERT)), w_spec((d, D_EXPERT)), w_spec((D_EXPERT, d))],
            out_specs=row_blk,
            scratch_shapes=[pltpu.VMEM((d, D_EXPERT), BF16), pltpu.VMEM((d, D_EXPERT), BF16),
                            pltpu.VMEM((D_EXPERT, d), BF16)]),
        out_shape=jax.ShapeDtypeStruct((r, half), I32),
        compiler_params=pltpu.CompilerParams(dimension_semantics=("arbitrary",), vmem_limit_bytes=VMEM_LIMIT),
        name="expert_gmm",
    )(block_e, block_rows, n_used, x_sorted, wg, wu, wd)


def _combine_kernel(base_ref, mod_ref, g_ref, w_ref, gain_ref, *rest):
    y_ref = rest[-1]
    nc, c, d = base_ref.shape
    w = w_ref[...]
    acc_lo = acc_hi = None
    for k in range(TOP_K):
        lo, hi = _unpack_bf16_pairs(g_ref[k])
        wk = w[:, k:k + 1]
        acc_lo = wk * lo if k == 0 else acc_lo + wk * lo
        acc_hi = wk * hi if k == 0 else acc_hi + wk * hi
    acc = jnp.concatenate([acc_lo, acc_hi], axis=1)
    g2 = mod_ref[...][:, :, 2 * d:3 * d]
    out = base_ref[...] + g2 * acc.reshape(nc, c, d)
    y_ref[...] = _rms(out) * gain_ref[...]


def _combine(base, modc, gathered, w_tok, gain, first_chunk, n_chunks, out_chunks, out_first_chunk, out_buf=None):
    _, c, d = base.shape
    nc = COMB_TILE // c
    t = nc * c
    t0, o0 = first_chunk // nc, out_first_chunk // nc
    blk3 = pl.BlockSpec((nc, c, d), lambda i: (t0 + i, 0, 0))
    in_specs = [blk3, pl.BlockSpec((nc, 1, 3 * d), lambda i: (t0 + i, 0, 0)),
                pl.BlockSpec((TOP_K, t, d // 2), lambda i: (0, t0 + i, 0)),
                pl.BlockSpec((t, TOP_K), lambda i: (t0 + i, 0)),
                pl.BlockSpec((1, 1, d), lambda i: (0, 0, 0))]
    args = [base, modc, gathered, w_tok, gain.reshape(1, 1, d)]
    aliases = {}
    if out_buf is not None:
        in_specs.append(pl.BlockSpec(memory_space=pl.ANY))
        args.append(out_buf)
        aliases = {len(args) - 1: 0}
    return pl.pallas_call(
        _combine_kernel,
        grid=(n_chunks // nc,),
        in_specs=in_specs,
        out_specs=pl.BlockSpec((nc, c, d), lambda i: (o0 + i, 0, 0)),
        out_shape=jax.ShapeDtypeStruct((out_chunks, c, d), F32),
        input_output_aliases=aliases,
        compiler_params=pltpu.CompilerParams(dimension_semantics=("arbitrary",), vmem_limit_bytes=VMEM_LIMIT),
        name="combine_norm",
    )(*args)


def _rope_tables(pos):
    half = HEAD_DIM // 2
    inv_freq = ROPE_THETA ** (-jnp.arange(half, dtype=F32) / half)
    ang = pos.astype(F32)[:, None] * inv_freq[None, :]
    cos, sin = jnp.cos(ang), jnp.sin(ang)
    reps = LANES // HEAD_DIM
    return jnp.tile(jnp.concatenate([cos, cos], axis=1), (1, reps)), jnp.tile(jnp.concatenate([-sin, sin], axis=1), (1, reps))


def _routed_ffn(h2, cw, w_gate, w_up, w_down):
    n, half = h2.shape
    rank, counts = _rank(cw)
    bm = GMM_BM
    padded = (counts[:, 0] + bm - 1) // bm * bm
    seg_end = jnp.cumsum(padded)
    seg_start = (seg_end - padded).astype(I32)
    n_rows = n * TOP_K + N_EXPERTS * bm
    n_blocks = n_rows // bm
    block_start = jnp.arange(n_blocks, dtype=I32) * bm
    block_e = jnp.minimum(jnp.sum((seg_end[None, :] <= block_start[:, None]).astype(I32), axis=1), N_EXPERTS - 1)
    own = block_e[:, None] == jnp.arange(N_EXPERTS, dtype=I32)[None, :]
    real_end = jnp.sum(jnp.where(own, (seg_start + counts[:, 0])[None, :], 0), axis=1)
    block_rows = jnp.clip(real_end - block_start, 0, bm).astype(I32)
    n_used = (seg_end[-1:] // bm).astype(I32)
    pos, w_k = _slots(cw, rank, seg_start[:, None])
    x_sorted = _sc_dispatch(h2, pos, n_rows)
    y_sorted = _gmm(x_sorted, block_e, block_rows, n_used, w_gate, w_up, w_down)
    gathered = _sc_collect(y_sorted, pos.reshape(TOP_K * n)).reshape(TOP_K, n, half)
    return gathered, w_k


def kernel(x_prompt, x_sample, cache_k, cache_v, state_conv, c_prompt, c_sample, w_ada, b_ada, w_in, w_conv,
           w_conv_out, w_attn_o, attn_sinks, w_mix_out, w_router, router_bias, w_exp_gate, w_exp_up, w_exp_down,
           w_sh_gate, w_sh_up, w_sh_down, final_gain):
    assert w_ada.shape[0] == 1, "one layer"
    bp, seq, d = x_prompt.shape
    bs, ts, _ = x_sample.shape
    assert ts == CHUNK and seq % MIX_TILE == 0 and bs % SAMPLE_BB == 0

    c_all = jnp.concatenate([c_prompt, c_sample], axis=0)
    pad = (-c_all.shape[0]) % SUBLANES
    mod = _ada(jnp.pad(c_all, ((0, pad), (0, 0))), w_ada[0], b_ada[0])[:bp + bs]
    mod_p, mod_s = mod[:bp, None, :], mod[bp:, None, :]

    perm = _head_perm()
    w_in_l = w_in[0]
    w_in_p = jnp.concatenate([w_in_l[:, :OFF_Q], w_in_l[:, OFF_Q:OFF_K][:, perm], w_in_l[:, OFF_K:]], axis=1)
    win, wco, wao, wmo = (w.astype(BF16) for w in (w_in_p, w_conv_out[0], w_attn_o[0][perm], w_mix_out[0]))
    cos_p, sin_p = _rope_tables(jnp.arange(seq, dtype=I32))
    cos_s, sin_s = _rope_tables(PAST_LEN + jnp.arange(ts, dtype=I32))

    x1_p, conv_p, k_p, v_p = _mixer_prompt(x_prompt, mod_p, cos_p, sin_p, win, w_conv[0], wco, wao, attn_sinks[0], wmo)
    x1_s, conv_s, k_s, v_s = _mixer_sample(
        x_sample, mod_s, cos_s, sin_s, cache_k[0].reshape(bs, WINDOW, KV_DIM), cache_v[0].reshape(bs, WINDOW, KV_DIM),
        state_conv[0], win, w_conv[0], wco, wao, attn_sinks[0], wmo)

    n_p, n_s = bp * seq, bs * ts
    n = n_p + n_s
    mod2 = mod[:, None, 3 * d:]
    modc_p, modc_s = jnp.repeat(mod2[:bp], seq // CHUNK, axis=0), mod2[bp:]
    x1_pc = x1_p.reshape(n_p // CHUNK, CHUNK, d)
    wsg, wsu, wsd = (w[0].astype(BF16) for w in (w_sh_gate, w_sh_up, w_sh_down))
    wr_t, rb = w_router[0].T, router_bias[0][:, None]

    ncp, ncs = n_p // CHUNK, n_s // CHUNK
    half = (ncp + ncs) // 2
    tile_chunks = max(PRE_TILE, COMB_TILE) // CHUNK
    assert half <= ncp and half % tile_chunks == 0 and (ncp - half) % tile_chunks == 0 and ncs % tile_chunks == 0
    assert (half * CHUNK) % (SC_WORKERS * SC_WINDOW) == 0 and (half * CHUNK) % RANK_TILE == 0
    y_p = None
    for p0, np_c, ns_c in ((0, half, 0), (half, ncp - half, ncs)):
        modc = jnp.concatenate([modc_p[p0:p0 + np_c], modc_s[:ns_c]], axis=0)
        h2, base, cw = _pre(x1_pc, p0, np_c, x1_s, ns_c, modc, wsg, wsu, wsd, wr_t, rb)
        gathered, w_k = _routed_ffn(h2, cw, w_exp_gate[0], w_exp_up[0], w_exp_down[0])
        w_tok = w_k.T
        y_p = _combine(base, modc, gathered, w_tok, final_gain, 0, np_c, ncp, p0, out_buf=y_p)
        if ns_c:
            y_s = _combine(base, modc, gathered, w_tok, final_gain, np_c, ns_c, ncs, 0)

    kv = lambda a: a.reshape(1, a.shape[0], WINDOW, N_KV_HEADS, HEAD_DIM)
    return (y_p.reshape(bp, seq, d), y_s, conv_p[None], kv(k_p), kv(v_p), conv_s[None], kv(k_s), kv(v_s))
```

```python
import functools

import numpy as np
import jax
import jax.numpy as jnp
from jax import lax
from jax.experimental import pallas as pl
from jax.experimental.pallas import tpu as pltpu
from jax.experimental.pallas import tpu_sc as plsc

F32 = jnp.float32
BF16 = jnp.bfloat16
I32 = jnp.int32

D_MODEL = 1024
CHUNK = 64
D_CONV = 1024
CONV_W = 3
N_HEADS = 16
N_KV_HEADS = 4
HEAD_DIM = 64
GQA_GROUP = N_HEADS // N_KV_HEADS
WINDOW = 128
ROPE_THETA = 10000.0
ATTN_SCALE = HEAD_DIM ** -0.5
N_EXPERTS = 64
TOP_K = 8
N_EXPERT_GROUPS = 8
GROUP_SIZE = N_EXPERTS // N_EXPERT_GROUPS
TOPK_GROUPS = 4
D_EXPERT = 256
D_SHARED = 256
ROUTED_SCALE = 2.5
EPS = 1e-6
PAST_LEN = 4096
Q_DIM = N_HEADS * HEAD_DIM
KV_DIM = N_KV_HEADS * HEAD_DIM
OFF_GB, OFF_GC, OFF_XC, OFF_Q, OFF_K, OFF_V, OFF_GCONV, OFF_GATTN, D_IN = (
    0, 1024, 2048, 3072, 4096, 4352, 4608, 5632, 6656)

LANES = 128
SUBLANES = 8
VMEM_LIMIT = 56 * 1024 * 1024

MIX_TILE = 512
ATT_Q = 128
SAMPLE_BB = 4
PRE_TILE = 512
RANK_TILE = 512
GMM_BM = 1024
GMM_X_SLOTS = 3
GMM_SUB = 512
GMM_TAIL = 128
COMB_TILE = 256
SC_WORKERS = 32
SC_WINDOW = 96


def _const_spec(shape):
    nd = len(shape)
    return pl.BlockSpec(shape, lambda *_: (0,) * nd, pipeline_mode=pl.Buffered(1))


def _rms(x):
    return x * lax.rsqrt(jnp.mean(x * x, axis=-1, keepdims=True) + EPS)


def _sigmoid(x):
    return 1.0 / (1.0 + jnp.exp(-x))


def _silu(x):
    return x * _sigmoid(x)


def _dot(a, b):
    return jnp.dot(a, b, preferred_element_type=F32)


def _pack_bf16_pairs(x):
    half = x.shape[-1] // 2
    lo = lax.bitcast_convert_type(x[..., :half].astype(BF16).astype(F32), I32)
    hi = lax.bitcast_convert_type(x[..., half:].astype(BF16).astype(F32), I32)
    return lax.shift_right_logical(lo, 16) | hi


def _unpack_bf16_pairs(words):
    lo = lax.bitcast_convert_type(lax.shift_left(words, 16), F32)
    hi = lax.bitcast_convert_type(words & jnp.int32(-65536), F32)
    return lo, hi


def _ada_kernel(c_ref, w_ref, b_ref, o_ref):
    s = _silu(c_ref[...]).astype(BF16)
    o_ref[...] = _dot(s, w_ref[...].astype(BF16)) + b_ref[...]


def _ada(c_all, w_ada, b_ada):
    rows = c_all.shape[0]
    n_out = w_ada.shape[1]
    bn = 768
    return pl.pallas_call(
        _ada_kernel,
        grid=(n_out // bn,),
        in_specs=[pl.BlockSpec((rows, D_MODEL), lambda i: (0, 0)),
                  pl.BlockSpec((D_MODEL, bn), lambda i: (0, i)),
                  pl.BlockSpec((1, bn), lambda i: (0, i))],
        out_specs=pl.BlockSpec((rows, bn), lambda i: (0, i)),
        out_shape=jax.ShapeDtypeStruct((rows, n_out), F32),
        name="ada_mod",
    )(c_all, w_ada, b_ada.reshape(1, n_out))


def _rope(x, cos, sin_signed):
    lane = lax.broadcasted_iota(I32, (x.shape[0], LANES), 1)
    first_half = (lane % HEAD_DIM) < (HEAD_DIM // 2)
    outs = []
    for g in range(x.shape[1] // LANES):
        xg = x[:, g * LANES:(g + 1) * LANES]
        up = pltpu.roll(xg, LANES - HEAD_DIM // 2, axis=1)
        down = pltpu.roll(xg, HEAD_DIM // 2, axis=1)
        partner = jnp.where(first_half, up, down)
        outs.append(xg * cos + partner * sin_signed)
    return jnp.concatenate(outs, axis=1)


def _head_perm():
    n = np.arange(Q_DIM)
    c, par, dd = n // LANES, (n % LANES) // HEAD_DIM, n % HEAD_DIM
    pair, i = c // GQA_GROUP, c % GQA_GROUP
    return HEAD_DIM * (GQA_GROUP * (2 * pair + par) + i) + dd


def _attend_pair(q_blk, k_pair, vt_pair, sinks_ref, pair, mask, obuf, row0):
    rq = GQA_GROUP * ATT_Q
    low = lax.broadcasted_iota(I32, (ATT_Q, LANES), 1) < HEAD_DIM
    head_of_lane = lax.broadcasted_iota(I32, (1, rq), 1) // ATT_Q
    outs = []
    for par in range(2):
        g = 2 * pair + par
        keep = low if par == 0 else jnp.logical_not(low)
        cols = [q_blk[:, (GQA_GROUP * pair + i) * LANES:(GQA_GROUP * pair + i + 1) * LANES] for i in range(GQA_GROUP)]
        qg = jnp.concatenate([jnp.where(keep, c, jnp.zeros_like(c)) for c in cols], axis=0)
        st = lax.dot_general(k_pair, qg, (((1,), (1,)), ((), ())), preferred_element_type=F32)
        if mask is not None:
            st = jnp.where(mask, st, -jnp.inf)
        sink = jnp.full((1, rq), sinks_ref[g * GQA_GROUP + GQA_GROUP - 1], F32)
        for i in range(GQA_GROUP - 2, -1, -1):
            sink = jnp.where(head_of_lane == i, sinks_ref[g * GQA_GROUP + i], sink)
        m = jnp.maximum(jnp.max(st, axis=0, keepdims=True), sink)
        e = jnp.exp(st - m)
        z = jnp.sum(e, axis=0, keepdims=True) + jnp.exp(sink - m)
        ot = _dot(vt_pair[par * HEAD_DIM:(par + 1) * HEAD_DIM, :], e.astype(BF16))
        outs.append(ot / z)
    for i in range(GQA_GROUP):
        blk = jnp.concatenate([o[:, i * ATT_Q:(i + 1) * ATT_Q] for o in outs], axis=0)
        c0 = (GQA_GROUP * pair + i) * LANES
        obuf[row0:row0 + ATT_Q, c0:c0 + LANES] = blk.T


def _in_proj(hb, win_ref, lo, hi):
    return _dot(hb, win_ref[:, lo:hi])


def _mix_out(x, g1, proj_gb, conv, y_attn_in, hb, win_ref, wco_ref, wao_ref, wmo_ref):
    y_conv = _dot((proj_gb * conv).astype(BF16), wco_ref[...])
    y_attn = _dot(y_attn_in.astype(BF16), wao_ref[...])
    g_conv = _in_proj(hb, win_ref, OFF_GCONV, OFF_GATTN)
    g_attn = _in_proj(hb, win_ref, OFF_GATTN, D_IN)
    merged = _sigmoid(g_conv) * y_conv + _sigmoid(g_attn) * y_attn
    return x + g1 * _dot(merged.astype(BF16), wmo_ref[...])


def _mixer_prompt_kernel(x_ref, mod_ref, cos_ref, sin_ref, win_ref, wconv_ref, wco_ref, wao_ref, sinks_ref, wmo_ref,
                         x1_ref, conv_ref, k_ref, v_ref, ubuf, kbuf, vtbuf, obuf):
    j = pl.program_id(1)
    t = x_ref.shape[1]

    @pl.when(j == 0)
    def _():
        ubuf[0:SUBLANES, :] = jnp.zeros((SUBLANES, D_CONV), F32)
        kbuf[0:WINDOW, :] = jnp.zeros((WINDOW, KV_DIM), BF16)
        vtbuf[:, 0:WINDOW] = jnp.zeros((KV_DIM, WINDOW), BF16)

    x = x_ref[0]
    mod = mod_ref[0]
    sh1, sc1, g1 = mod[:, 0:D_MODEL], mod[:, D_MODEL:2 * D_MODEL], mod[:, 2 * D_MODEL:3 * D_MODEL]
    hb = (_rms(x) * (1.0 + sc1) + sh1).astype(BF16)

    u = _in_proj(hb, win_ref, OFF_GC, OFF_XC) * _in_proj(hb, win_ref, OFF_XC, OFF_Q)
    ubuf[SUBLANES:SUBLANES + t, :] = u
    wc = wconv_ref[...]
    conv = wc[0:1] * ubuf[SUBLANES - 2:SUBLANES - 2 + t, :] + wc[1:2] * ubuf[SUBLANES - 1:SUBLANES - 1 + t, :] + wc[2:3] * u
    conv_ref[0] = u[t - (CONV_W - 1):t]
    ubuf[SUBLANES - 2:SUBLANES, :] = u[t - (CONV_W - 1):t]

    cos, sin = cos_ref[...], sin_ref[...]
    q = (_rope(_in_proj(hb, win_ref, OFF_Q, OFF_K), cos, sin) * ATTN_SCALE).astype(BF16)
    k = _rope(_in_proj(hb, win_ref, OFF_K, OFF_V), cos, sin)
    v = _in_proj(hb, win_ref, OFF_V, OFF_GCONV)
    kbuf[WINDOW:WINDOW + t, :] = k.astype(BF16)
    vtbuf[:, WINDOW:WINDOW + t] = v.T.astype(BF16)
    k_ref[0] = k[t - WINDOW:t]
    v_ref[0] = v[t - WINDOW:t]

    nkeys = ATT_Q + WINDOW
    rq = GQA_GROUP * ATT_Q
    ki = lax.broadcasted_iota(I32, (nkeys, rq), 0)
    qi = lax.broadcasted_iota(I32, (nkeys, rq), 1) % ATT_Q
    band = ki // CHUNK - qi // CHUNK
    band_ok = (band >= 0) & (band <= WINDOW // CHUNK)
    for s in range(t // ATT_Q):
        mask = band_ok & (ki + (j * t + s * ATT_Q - WINDOW) >= 0)
        qs = q[s * ATT_Q:(s + 1) * ATT_Q]
        for pair in range(N_KV_HEADS // 2):
            k_pair = kbuf[s * ATT_Q:s * ATT_Q + nkeys, pair * LANES:(pair + 1) * LANES]
            vt_pair = vtbuf[pair * LANES:(pair + 1) * LANES, s * ATT_Q:s * ATT_Q + nkeys]
            _attend_pair(qs, k_pair, vt_pair, sinks_ref, pair, mask, obuf, s * ATT_Q)
    kbuf[0:WINDOW, :] = kbuf[t:t + WINDOW, :]
    vtbuf[:, 0:WINDOW] = vtbuf[:, t:t + WINDOW]

    gate_b = _in_proj(hb, win_ref, OFF_GB, OFF_GC)
    x1_ref[0] = _mix_out(x, g1, gate_b, conv, obuf[...], hb, win_ref, wco_ref, wao_ref, wmo_ref)


def _mixer_prompt(x, mod, cos, sin, win, wconv, wco, wao, sinks, wmo):
    b, seq, d = x.shape
    t = MIX_TILE
    return pl.pallas_call(
        _mixer_prompt_kernel,
        grid=(b, seq // t),
        in_specs=[pl.BlockSpec((1, t, d), lambda i, j: (i, j, 0)),
                  pl.BlockSpec((1, 1, 6 * d), lambda i, j: (i, 0, 0)),
                  pl.BlockSpec((t, LANES), lambda i, j: (j, 0)),
                  pl.BlockSpec((t, LANES), lambda i, j: (j, 0)),
                  _const_spec(win.shape), _const_spec(wconv.shape), _const_spec(wco.shape), _const_spec(wao.shape),
                  pl.BlockSpec(memory_space=pltpu.SMEM),
                  _const_spec(wmo.shape)],
        out_specs=[pl.BlockSpec((1, t, d), lambda i, j: (i, j, 0)),
                   pl.BlockSpec((1, CONV_W - 1, D_CONV), lambda i, j: (i, 0, 0)),
                   pl.BlockSpec((1, WINDOW, KV_DIM), lambda i, j: (i, 0, 0)),
                   pl.BlockSpec((1, WINDOW, KV_DIM), lambda i, j: (i, 0, 0))],
        out_shape=[jax.ShapeDtypeStruct((b, seq, d), F32),
                   jax.ShapeDtypeStruct((b, CONV_W - 1, D_CONV), F32),
                   jax.ShapeDtypeStruct((b, WINDOW, KV_DIM), F32),
                   jax.ShapeDtypeStruct((b, WINDOW, KV_DIM), F32)],
        scratch_shapes=[pltpu.VMEM((SUBLANES + t, D_CONV), F32),
                        pltpu.VMEM((WINDOW + t, KV_DIM), BF16),
                        pltpu.VMEM((KV_DIM, WINDOW + t), BF16),
                        pltpu.VMEM((t, Q_DIM), F32)],
        compiler_params=pltpu.CompilerParams(dimension_semantics=("arbitrary", "arbitrary"),
                                             vmem_limit_bytes=VMEM_LIMIT),
        name="mixer_prompt",
    )(x, mod, cos, sin, win, wconv, wco, wao, sinks, wmo)


def _mixer_sample_kernel(x_ref, mod_ref, cos_ref, sin_ref, ck_ref, cv_ref, sconv_ref, win_ref, wconv_ref, wco_ref,
                         wao_ref, sinks_ref, wmo_ref, x1_ref, conv_ref, k_ref, v_ref, ubuf, obuf):
    bb, t, d = x_ref.shape
    x3 = x_ref[...]
    mod = mod_ref[...]
    sh1, sc1, g1 = mod[:, :, 0:d], mod[:, :, d:2 * d], mod[:, :, 2 * d:3 * d]
    x = x3.reshape(bb * t, d)
    hb = (_rms(x3) * (1.0 + sc1) + sh1).astype(BF16).reshape(bb * t, d)

    u = _in_proj(hb, win_ref, OFF_GC, OFF_XC) * _in_proj(hb, win_ref, OFF_XC, OFF_Q)
    u3 = u.reshape(bb, t, D_CONV)
    ubuf[:, SUBLANES - 2:SUBLANES, :] = sconv_ref[...]
    ubuf[:, SUBLANES:SUBLANES + t, :] = u3
    wc = wconv_ref[...]
    conv = (wc[0:1] * ubuf[:, SUBLANES - 2:SUBLANES - 2 + t, :] + wc[1:2] * ubuf[:, SUBLANES - 1:SUBLANES - 1 + t, :]
            + wc[2:3] * u3).reshape(bb * t, D_CONV)
    conv_ref[...] = u3[:, t - (CONV_W - 1):t, :]

    cos = jnp.concatenate([cos_ref[...]] * bb, axis=0)
    sin = jnp.concatenate([sin_ref[...]] * bb, axis=0)
    q = (_rope(_in_proj(hb, win_ref, OFF_Q, OFF_K), cos, sin) * ATTN_SCALE).astype(BF16)
    k = _rope(_in_proj(hb, win_ref, OFF_K, OFF_V), cos, sin)
    v = _in_proj(hb, win_ref, OFF_V, OFF_GCONV)
    per = ATT_Q // t
    nkeys = per * (WINDOW + t)
    rq = GQA_GROUP * ATT_Q
    key_stream = lax.broadcasted_iota(I32, (nkeys, rq), 0) // (WINDOW + t)
    query_stream = (lax.broadcasted_iota(I32, (nkeys, rq), 1) % ATT_Q) // t
    mask = key_stream == query_stream
    for blk in range(bb // per):
        k_parts, v_parts = [], []
        for b in range(blk * per, (blk + 1) * per):
            kb, vb = k[b * t:(b + 1) * t], v[b * t:(b + 1) * t]
            ck, cv = ck_ref[b], cv_ref[b]
            k_ref[b] = jnp.concatenate([ck[t:WINDOW], kb], axis=0)
            v_ref[b] = jnp.concatenate([cv[t:WINDOW], vb], axis=0)
            k_parts += [ck, kb]
            v_parts += [cv, vb]
        k_all = jnp.concatenate(k_parts, axis=0).astype(BF16)
        vt_all = jnp.concatenate(v_parts, axis=0).T.astype(BF16)
        qs = q[blk * ATT_Q:(blk + 1) * ATT_Q]
        for pair in range(N_KV_HEADS // 2):
            _attend_pair(qs, k_all[:, pair * LANES:(pair + 1) * LANES], vt_all[pair * LANES:(pair + 1) * LANES, :],
                         sinks_ref, pair, mask, obuf, blk * ATT_Q)

    gate_b = _in_proj(hb, win_ref, OFF_GB, OFF_GC)
    g1f = jnp.broadcast_to(g1, (bb, t, d)).reshape(bb * t, d)
    x1_ref[...] = _mix_out(x, g1f, gate_b, conv, obuf[...], hb, win_ref, wco_ref, wao_ref, wmo_ref).reshape(bb, t, d)


def _mixer_sample(x, mod, cos, sin, ck, cv, sconv, win, wconv, wco, wao, sinks, wmo):
    b, t, d = x.shape
    bb = SAMPLE_BB
    blk = lambda *s: pl.BlockSpec((bb,) + s, lambda i: (i, 0, 0))
    return pl.pallas_call(
        _mixer_sample_kernel,
        grid=(b // bb,),
        in_specs=[blk(t, d), blk(1, 6 * d),
                  pl.BlockSpec((t, LANES), lambda i: (0, 0)), pl.BlockSpec((t, LANES), lambda i: (0, 0)),
                  blk(WINDOW, KV_DIM), blk(WINDOW, KV_DIM), blk(CONV_W - 1, D_CONV),
                  _const_spec(win.shape), _const_spec(wconv.shape), _const_spec(wco.shape), _const_spec(wao.shape),
                  pl.BlockSpec(memory_space=pltpu.SMEM),
                  _const_spec(wmo.shape)],
        out_specs=[blk(t, d), blk(CONV_W - 1, D_CONV), blk(WINDOW, KV_DIM), blk(WINDOW, KV_DIM)],
        out_shape=[jax.ShapeDtypeStruct((b, t, d), F32),
                   jax.ShapeDtypeStruct((b, CONV_W - 1, D_CONV), F32),
                   jax.ShapeDtypeStruct((b, WINDOW, KV_DIM), F32),
                   jax.ShapeDtypeStruct((b, WINDOW, KV_DIM), F32)],
        scratch_shapes=[pltpu.VMEM((bb, SUBLANES + t, D_CONV), F32),
                        pltpu.VMEM((bb * t, Q_DIM), F32)],
        compiler_params=pltpu.CompilerParams(dimension_semantics=("arbitrary",), vmem_limit_bytes=VMEM_LIMIT),
        name="mixer_sample",
    )(x, mod, cos, sin, ck, cv, sconv, win, wconv, wco, wao, sinks, wmo)


def _pre_kernel(*refs, prompt_tiles, has_sample):
    if has_sample:
        xp_ref, xs_ref, mod_ref, wsg_ref, wsu_ref, wsd_ref, wr_ref, rb_ref, h2_ref, base_ref, cw_ref = refs
    else:
        xp_ref, mod_ref, wsg_ref, wsu_ref, wsd_ref, wr_ref, rb_ref, h2_ref, base_ref, cw_ref = refs
    nc, c, d = xp_ref.shape
    t = nc * c
    x3 = xp_ref[...]
    if has_sample:
        x3 = jnp.where(pl.program_id(0) < prompt_tiles, x3, xs_ref[...])
    mod = mod_ref[...]
    sh2, sc2, g2 = mod[:, :, 0:d], mod[:, :, d:2 * d], mod[:, :, 2 * d:3 * d]
    h3 = _rms(x3) * (1.0 + sc2) + sh2
    h2 = h3.reshape(t, d)
    hb = h2.astype(BF16)
    h2_ref[...] = _pack_bf16_pairs(h2)
    shared = _dot((_silu(_dot(hb, wsg_ref[...])) * _dot(hb, wsu_ref[...])).astype(BF16), wsd_ref[...])
    base_ref[...] = x3 + g2 * shared.reshape(nc, c, d)

    logits = lax.dot_general(wr_ref[...], h2, (((1,), (1,)), ((), ())), preferred_element_type=F32,
                             precision=lax.Precision.HIGHEST)
    scores = _sigmoid(logits)
    biased = scores + rb_ref[...]
    g3 = biased.reshape(N_EXPERT_GROUPS, GROUP_SIZE, t)
    member = lax.broadcasted_iota(I32, g3.shape, 1)
    m1 = jnp.max(g3, axis=1, keepdims=True)
    first = jnp.min(jnp.where(g3 == m1, member, GROUP_SIZE), axis=1, keepdims=True)
    m2 = jnp.max(jnp.where(member == first, -jnp.inf, g3), axis=1, keepdims=True)
    gs = m1 + m2
    gidx = lax.broadcasted_iota(I32, gs.shape, 0)
    grank = jnp.zeros(gs.shape, I32)
    for o in range(N_EXPERT_GROUPS):
        other = gs[o:o + 1]
        grank += ((other > gs) | ((other == gs) & (o < gidx))).astype(I32)
    eligible = jnp.broadcast_to(grank < TOPK_GROUPS, g3.shape).reshape(N_EXPERTS, t)
    mb = jnp.where(eligible, biased, -jnp.inf)
    eidx = lax.broadcasted_iota(I32, mb.shape, 0)
    erank = jnp.zeros(mb.shape, I32)
    for o in range(N_EXPERTS):
        other = mb[o:o + 1]
        erank += ((other > mb) | ((other == mb) & (o < eidx))).astype(I32)
    sel = eligible & (erank < TOP_K)
    ssum = jnp.sum(jnp.where(sel, scores, 0.0), axis=0, keepdims=True)
    cw_ref[...] = jnp.where(sel, scores / ssum * ROUTED_SCALE, -1.0)


def _pre(x1_p, p_chunk0, ncp, x1_s, ncs, modc, wsg, wsu, wsd, wr_t, rb):
    _, c, d = x1_p.shape
    nc = PRE_TILE // c
    nchunks = ncp + ncs
    n = nchunks * c
    pt, p0 = ncp // nc, p_chunk0 // nc
    blk3 = pl.BlockSpec((nc, c, d), lambda i: (i, 0, 0))
    xs_args, xs_specs = [], []
    if ncs:
        xs_args, xs_specs = [x1_s], [pl.BlockSpec((nc, c, d), lambda i: (jnp.maximum(i - pt, 0), 0, 0))]
    return pl.pallas_call(
        functools.partial(_pre_kernel, prompt_tiles=pt, has_sample=bool(ncs)),
        grid=(nchunks // nc,),
        in_specs=[pl.BlockSpec((nc, c, d), lambda i: (p0 + jnp.minimum(i, pt - 1), 0, 0))] + xs_specs + [
                  pl.BlockSpec((nc, 1, 3 * d), lambda i: (i, 0, 0)),
                  _const_spec(wsg.shape), _const_spec(wsu.shape), _const_spec(wsd.shape),
                  _const_spec(wr_t.shape), _const_spec(rb.shape)],
        out_specs=[pl.BlockSpec((nc * c, d // 2), lambda i: (i, 0)), blk3,
                   pl.BlockSpec((N_EXPERTS, nc * c), lambda i: (0, i))],
        out_shape=[jax.ShapeDtypeStruct((n, d // 2), I32),
                   jax.ShapeDtypeStruct((nchunks, c, d), F32),
                   jax.ShapeDtypeStruct((N_EXPERTS, n), F32)],
        compiler_params=pltpu.CompilerParams(dimension_semantics=("arbitrary",), vmem_limit_bytes=VMEM_LIMIT),
        name="pre_ffn",
    )(x1_p, *xs_args, modc, wsg, wsu, wsd, wr_t, rb)


def _rank_kernel(cw_ref, rank_ref, cnt_ref, carry):
    i = pl.program_id(0)
    t = cw_ref.shape[1]

    @pl.when(i == 0)
    def _():
        carry[...] = jnp.zeros(carry.shape, F32)

    sel = (cw_ref[...] >= 0.0).astype(BF16)
    r = lax.broadcasted_iota(I32, (t, t), 0)
    c = lax.broadcasted_iota(I32, (t, t), 1)
    before = (r < c).astype(BF16)
    rank = carry[...] + _dot(sel, before)
    rank_ref[...] = rank.astype(I32)
    carry[...] = carry[...] + jnp.sum(sel.astype(F32), axis=1, keepdims=True)
    cnt_ref[...] = carry[...].astype(I32)


def _rank(cw):
    e, n = cw.shape
    t = RANK_TILE
    return pl.pallas_call(
        _rank_kernel,
        grid=(n // t,),
        in_specs=[pl.BlockSpec((e, t), lambda i: (0, i))],
        out_specs=[pl.BlockSpec((e, t), lambda i: (0, i)), pl.BlockSpec((e, 1), lambda i: (0, 0))],
        out_shape=[jax.ShapeDtypeStruct((e, n), I32), jax.ShapeDtypeStruct((e, 1), I32)],
        scratch_shapes=[pltpu.VMEM((e, 1), F32)],
        compiler_params=pltpu.CompilerParams(dimension_semantics=("arbitrary",)),
        name="expert_rank",
    )(cw)


def _slot_kernel(cw_ref, rank_ref, start_ref, pos_ref, w_ref):
    cw = cw_ref[...]
    e, t = cw.shape
    sel = cw >= 0.0
    r = lax.broadcasted_iota(I32, (e, e), 0)
    c = lax.broadcasted_iota(I32, (e, e), 1)
    lower = (c < r).astype(BF16)
    kidx = _dot(lower, sel.astype(BF16))
    posf = start_ref[...].astype(F32) + rank_ref[...].astype(F32)
    pos_rows, w_rows = [], []
    for k in range(TOP_K):
        m = sel & (kidx == float(k))
        pos_rows.append(jnp.sum(jnp.where(m, posf, 0.0), axis=0, keepdims=True))
        w_rows.append(jnp.sum(jnp.where(m, cw, 0.0), axis=0, keepdims=True))
    pos_ref[...] = jnp.concatenate(pos_rows, axis=0).astype(I32)
    w_ref[...] = jnp.concatenate(w_rows, axis=0)


def _slots(cw, rank, seg_start):
    e, n = cw.shape
    t = RANK_TILE
    return pl.pallas_call(
        _slot_kernel,
        grid=(n // t,),
        in_specs=[pl.BlockSpec((e, t), lambda i: (0, i)), pl.BlockSpec((e, t), lambda i: (0, i)),
                  pl.BlockSpec((e, 1), lambda i: (0, 0))],
        out_specs=[pl.BlockSpec((TOP_K, t), lambda i: (0, i)), pl.BlockSpec((TOP_K, t), lambda i: (0, i))],
        out_shape=[jax.ShapeDtypeStruct((TOP_K, n), I32), jax.ShapeDtypeStruct((TOP_K, n), F32)],
        compiler_params=pltpu.CompilerParams(dimension_semantics=("arbitrary",)),
        name="expert_slots",
    )(cw, rank, seg_start)


def _sc_mesh():
    return plsc.VectorSubcoreMesh(core_axis_name="c", subcore_axis_name="s")


def _sc_worker_id():
    return lax.axis_index("s") * (SC_WORKERS // 16) + lax.axis_index("c")


def _sc_dispatch(rows, pos, n_rows):
    n, d = rows.shape
    per_w = n // SC_WORKERS
    w = SC_WINDOW
    n_chunks = per_w // w

    @functools.partial(
        pl.kernel, mesh=_sc_mesh(),
        out_type=jax.ShapeDtypeStruct((n_rows, d), rows.dtype),
        scratch_types=[pltpu.VMEM((2, TOP_K, w), I32), pltpu.VMEM((2, w, d), rows.dtype),
                       pltpu.SemaphoreType.DMA((2,)), pltpu.SemaphoreType.DMA((2,)), pltpu.SemaphoreType.DMA((2,))],
        name="sc_dispatch")
    def k(rows_hbm, pos_hbm, o_hbm, idx_v, rows_v, row_sem, idx_sem, out_sem):
        wid = _sc_worker_id()
        base = wid * per_w

        def loads(c, slot):
            off = pl.multiple_of(base + c * w, SUBLANES)
            return (pltpu.make_async_copy(rows_hbm.at[pl.ds(off, w)], rows_v.at[slot], row_sem.at[slot]),
                    pltpu.make_async_copy(pos_hbm.at[wid * n_chunks + c], idx_v.at[slot], idx_sem.at[slot]))

        def scatters(slot):
            return [pltpu.make_async_copy(rows_v.at[slot], o_hbm.at[idx_v.at[slot, kk]], out_sem.at[slot])
                    for kk in range(TOP_K)]

        for cp in loads(0, 0):
            cp.start()
        for c in range(n_chunks):
            slot = c % 2
            for cp in loads(c, slot):
                cp.wait()
            for cp in scatters(slot):
                cp.start()
            if c >= 1:
                for cp in scatters(1 - slot):
                    cp.wait()
            if c + 1 < n_chunks:
                for cp in loads(c + 1, 1 - slot):
                    cp.start()
        for cp in scatters((n_chunks - 1) % 2):
            cp.wait()

    pos_chunks = pos.reshape(TOP_K, n // w, w).transpose(1, 0, 2)
    return k(rows, pos_chunks)


def _sc_collect(rows, pos_flat):
    d = rows.shape[1]
    total = pos_flat.shape[0]
    per_w = total // SC_WORKERS
    w = SC_WINDOW
    n_pairs = per_w // (2 * w)

    @functools.partial(
        pl.kernel, mesh=_sc_mesh(),
        out_type=jax.ShapeDtypeStruct((total, d), rows.dtype),
        scratch_types=[pltpu.VMEM((per_w,), I32), pltpu.VMEM((2, w, d), rows.dtype),
                       pltpu.SemaphoreType.DMA((2,)), pltpu.SemaphoreType.DMA((2,))],
        name="sc_collect")
    def k(rows_hbm, pos_hbm, o_hbm, idx_v, rows_v, in_sem, out_sem):
        base = pl.multiple_of(_sc_worker_id() * per_w, SUBLANES)
        pltpu.sync_copy(pos_hbm.at[pl.ds(base, per_w)], idx_v)

        def gather(c, slot):
            idx = idx_v.at[pl.ds(pl.multiple_of(c * w, SUBLANES), w)]
            return pltpu.make_async_copy(rows_hbm.at[idx], rows_v.at[slot], in_sem.at[slot])

        def write(c, slot):
            off = pl.multiple_of(base + c * w, SUBLANES)
            return pltpu.make_async_copy(rows_v.at[slot], o_hbm.at[pl.ds(off, w)], out_sem.at[slot])

        gather(0, 0).start()

        @pl.loop(0, n_pairs)
        def _(p):
            c0 = 2 * p
            gather(c0 + 1, 1).start()
            gather(c0, 0).wait()
            write(c0, 0).start()
            gather(c0 + 1, 1).wait()
            write(c0 + 1, 1).start()
            write(c0, 0).wait()

            @pl.when(p + 1 < n_pairs)
            def _():
                gather(c0 + 2, 0).start()

            write(c0 + 1, 1).wait()

    return k(rows, pos_flat)


def _gmm_kernel(be_ref, br_ref, nu_ref, x_hbm, wg_ref, wu_ref, wd_ref, y_ref, wgb, wub, wdb, xbuf, xsem):
    b = pl.program_id(0)
    n_used = nu_ref[0]
    bm = xbuf.shape[1]

    def x_copy(blk):
        slot = blk % GMM_X_SLOTS
        return pltpu.make_async_copy(x_hbm.at[pl.ds(pl.multiple_of(blk * bm, bm), bm)], xbuf.at[slot], xsem.at[slot])

    @pl.when(b == 0)
    def _():
        for ahead in range(GMM_X_SLOTS - 1):
            @pl.when(ahead < n_used)
            def _():
                x_copy(ahead).start()

    @pl.when(b + GMM_X_SLOTS - 1 < n_used)
    def _():
        x_copy(b + GMM_X_SLOTS - 1).start()

    prev = be_ref[jnp.maximum(b - 1, 0)]

    @pl.when((b == 0) | (be_ref[b] != prev))
    def _():
        wgb[...] = wg_ref[0].astype(BF16)
        wub[...] = wu_ref[0].astype(BF16)
        wdb[...] = wd_ref[0].astype(BF16)

    @pl.when(b < n_used)
    def _():
        x_copy(b).wait()

    x_ref = xbuf.at[b % GMM_X_SLOTS]

    def expert_rows(r0, n):
        rows = pl.ds(r0, n)
        lo, hi = _unpack_bf16_pairs(x_ref[rows, :])
        xb = jnp.concatenate([lo.astype(BF16), hi.astype(BF16)], axis=1)
        mid = (_silu(_dot(xb, wgb[...])) * _dot(xb, wub[...])).astype(BF16)
        y_ref[rows, :] = _pack_bf16_pairs(_dot(mid, wdb[...]))

    n_real = br_ref[b]
    n_main = n_real // GMM_SUB

    @pl.loop(0, n_main)
    def _(i):
        expert_rows(pl.multiple_of(i * GMM_SUB, GMM_SUB), GMM_SUB)

    @pl.loop(0, (n_real - n_main * GMM_SUB + GMM_TAIL - 1) // GMM_TAIL)
    def _(i):
        expert_rows(pl.multiple_of(n_main * GMM_SUB + i * GMM_TAIL, GMM_TAIL), GMM_TAIL)


def _gmm(x_sorted, block_e, block_rows, n_used, wg, wu, wd):
    r, half = x_sorted.shape
    d = 2 * half
    bm = GMM_BM
    row_blk = pl.BlockSpec((bm, half), lambda b, be, br, nu: (jnp.minimum(b, nu[0] - 1), 0))
    w_spec = lambda shape: pl.BlockSpec((1,) + shape, lambda b, be, br, nu: (be[b], 0, 0))
    return pl.pallas_call(
        _gmm_kernel,
        grid_spec=pltpu.PrefetchScalarGridSpec(
            num_scalar_prefetch=3,
            grid=(r // bm,),
            in_specs=[pl.BlockSpec(memory_space=pl.ANY),
                      w_spec((d, D_EXPERT)), w_spec((d, D_EXPERT)), w_spec((D_EXPERT, d))],
            out_specs=row_blk,
            scratch_shapes=[pltpu.VMEM((d, D_EXPERT), BF16), pltpu.VMEM((d, D_EXPERT), BF16),
                            pltpu.VMEM((D_EXPERT, d), BF16),
                            pltpu.VMEM((GMM_X_SLOTS, bm, half), I32), pltpu.SemaphoreType.DMA((GMM_X_SLOTS,))]),
        out_shape=jax.ShapeDtypeStruct((r, half), I32),
        compiler_params=pltpu.CompilerParams(dimension_semantics=("arbitrary",), vmem_limit_bytes=VMEM_LIMIT),
        name="expert_gmm",
    )(block_e, block_rows, n_used, x_sorted, wg, wu, wd)


def _combine_kernel(base_ref, mod_ref, g_ref, w_ref, gain_ref, *rest):
    y_ref = rest[-1]
    nc, c, d = base_ref.shape
    w = w_ref[...]
    acc_lo = acc_hi = None
    for k in range(TOP_K):
        lo, hi = _unpack_bf16_pairs(g_ref[k])
        wk = w[:, k:k + 1]
        acc_lo = wk * lo if k == 0 else acc_lo + wk * lo
        acc_hi = wk * hi if k == 0 else acc_hi + wk * hi
    acc = jnp.concatenate([acc_lo, acc_hi], axis=1)
    g2 = mod_ref[...][:, :, 2 * d:3 * d]
    out = base_ref[...] + g2 * acc.reshape(nc, c, d)
    y_ref[...] = _rms(out) * gain_ref[...]


def _combine(base, modc, gathered, w_tok, gain, first_chunk, n_chunks, out_chunks, out_first_chunk, out_buf=None):
    _, c, d = base.shape
    nc = COMB_TILE // c
    t = nc * c
    t0, o0 = first_chunk // nc, out_first_chunk // nc
    blk3 = pl.BlockSpec((nc, c, d), lambda i: (t0 + i, 0, 0))
    in_specs = [blk3, pl.BlockSpec((nc, 1, 3 * d), lambda i: (t0 + i, 0, 0)),
                pl.BlockSpec((TOP_K, t, d // 2), lambda i: (0, t0 + i, 0)),
                pl.BlockSpec((t, TOP_K), lambda i: (t0 + i, 0)),
                pl.BlockSpec((1, 1, d), lambda i: (0, 0, 0))]
    args = [base, modc, gathered, w_tok, gain.reshape(1, 1, d)]
    aliases = {}
    if out_buf is not None:
        in_specs.append(pl.BlockSpec(memory_space=pl.ANY))
        args.append(out_buf)
        aliases = {len(args) - 1: 0}
    return pl.pallas_call(
        _combine_kernel,
        grid=(n_chunks // nc,),
        in_specs=in_specs,
        out_specs=pl.BlockSpec((nc, c, d), lambda i: (o0 + i, 0, 0)),
        out_shape=jax.ShapeDtypeStruct((out_chunks, c, d), F32),
        input_output_aliases=aliases,
        compiler_params=pltpu.CompilerParams(dimension_semantics=("arbitrary",), vmem_limit_bytes=VMEM_LIMIT),
        name="combine_norm",
    )(*args)


def _rope_tables(pos):
    half = HEAD_DIM // 2
    inv_freq = ROPE_THETA ** (-jnp.arange(half, dtype=F32) / half)
    ang = pos.astype(F32)[:, None] * inv_freq[None, :]
    cos, sin = jnp.cos(ang), jnp.sin(ang)
    reps = LANES // HEAD_DIM
    return jnp.tile(jnp.concatenate([cos, cos], axis=1), (1, reps)), jnp.tile(jnp.concatenate([-sin, sin], axis=1), (1, reps))


def _routed_ffn(h2, cw, w_gate, w_up, w_down):
    n, half = h2.shape
    rank, counts = _rank(cw)
    bm = GMM_BM
    padded = (counts[:, 0] + bm - 1) // bm * bm
    seg_end = jnp.cumsum(padded)
    seg_start = (seg_end - padded).astype(I32)
    n_rows = n * TOP_K + N_EXPERTS * bm
    n_blocks = n_rows // bm
    block_start = jnp.arange(n_blocks, dtype=I32) * bm
    block_e = jnp.minimum(jnp.sum((seg_end[None, :] <= block_start[:, None]).astype(I32), axis=1), N_EXPERTS - 1)
    own = block_e[:, None] == jnp.arange(N_EXPERTS, dtype=I32)[None, :]
    real_end = jnp.sum(jnp.where(own, (seg_start + counts[:, 0])[None, :], 0), axis=1)
    block_rows = jnp.clip(real_end - block_start, 0, bm).astype(I32)
    n_used = (seg_end[-1:] // bm).astype(I32)
    pos, w_k = _slots(cw, rank, seg_start[:, None])
    x_sorted = _sc_dispatch(h2, pos, n_rows)
    y_sorted = _gmm(x_sorted, block_e, block_rows, n_used, w_gate, w_up, w_down)
    gathered = _sc_collect(y_sorted, pos.reshape(TOP_K * n)).reshape(TOP_K, n, half)
    return gathered, w_k


def kernel(x_prompt, x_sample, cache_k, cache_v, state_conv, c_prompt, c_sample, w_ada, b_ada, w_in, w_conv,
           w_conv_out, w_attn_o, attn_sinks, w_mix_out, w_router, router_bias, w_exp_gate, w_exp_up, w_exp_down,
           w_sh_gate, w_sh_up, w_sh_down, final_gain):
    assert w_ada.shape[0] == 1, "one layer"
    bp, seq, d = x_prompt.shape
    bs, ts, _ = x_sample.shape
    assert ts == CHUNK and seq % MIX_TILE == 0 and bs % SAMPLE_BB == 0

    c_all = jnp.concatenate([c_prompt, c_sample], axis=0)
    pad = (-c_all.shape[0]) % SUBLANES
    mod = _ada(jnp.pad(c_all, ((0, pad), (0, 0))), w_ada[0], b_ada[0])[:bp + bs]
    mod_p, mod_s = mod[:bp, None, :], mod[bp:, None, :]

    perm = _head_perm()
    w_in_l = w_in[0]
    w_in_p = jnp.concatenate([w_in_l[:, :OFF_Q], w_in_l[:, OFF_Q:OFF_K][:, perm], w_in_l[:, OFF_K:]], axis=1)
    win, wco, wao, wmo = (w.astype(BF16) for w in (w_in_p, w_conv_out[0], w_attn_o[0][perm], w_mix_out[0]))
    cos_p, sin_p = _rope_tables(jnp.arange(seq, dtype=I32))
    cos_s, sin_s = _rope_tables(PAST_LEN + jnp.arange(ts, dtype=I32))

    x1_p, conv_p, k_p, v_p = _mixer_prompt(x_prompt, mod_p, cos_p, sin_p, win, w_conv[0], wco, wao, attn_sinks[0], wmo)
    x1_s, conv_s, k_s, v_s = _mixer_sample(
        x_sample, mod_s, cos_s, sin_s, cache_k[0].reshape(bs, WINDOW, KV_DIM), cache_v[0].reshape(bs, WINDOW, KV_DIM),
        state_conv[0], win, w_conv[0], wco, wao, attn_sinks[0], wmo)

    n_p, n_s = bp * seq, bs * ts
    n = n_p + n_s
    mod2 = mod[:, None, 3 * d:]
    modc_p, modc_s = jnp.repeat(mod2[:bp], seq // CHUNK, axis=0), mod2[bp:]
    x1_pc = x1_p.reshape(n_p // CHUNK, CHUNK, d)
    wsg, wsu, wsd = (w[0].astype(BF16) for w in (w_sh_gate, w_sh_up, w_sh_down))
    wr_t, rb = w_router[0].T, router_bias[0][:, None]

    ncp, ncs = n_p // CHUNK, n_s // CHUNK
    half = (ncp + ncs) // 2
    tile_chunks = max(PRE_TILE, COMB_TILE) // CHUNK
    assert half <= ncp and half % tile_chunks == 0 and (ncp - half) % tile_chunks == 0 and ncs % tile_chunks == 0
    assert (half * CHUNK) % (SC_WORKERS * SC_WINDOW) == 0 and (half * CHUNK) % RANK_TILE == 0
    y_p = None
    for p0, np_c, ns_c in ((0, half, 0), (half, ncp - half, ncs)):
        modc = jnp.concatenate([modc_p[p0:p0 + np_c], modc_s[:ns_c]], axis=0)
        h2, base, cw = _pre(x1_pc, p0, np_c, x1_s, ns_c, modc, wsg, wsu, wsd, wr_t, rb)
        gathered, w_k = _routed_ffn(h2, cw, w_exp_gate[0], w_exp_up[0], w_exp_down[0])
        w_tok = w_k.T
        y_p = _combine(base, modc, gathered, w_tok, final_gain, 0, np_c, ncp, p0, out_buf=y_p)
        if ns_c:
            y_s = _combine(base, modc, gathered, w_tok, final_gain, np_c, ns_c, ncs, 0)

    kv = lambda a: a.reshape(1, a.shape[0], WINDOW, N_KV_HEADS, HEAD_DIM)
    return (y_p.reshape(bp, seq, d), y_s, conv_p[None], kv(k_p), kv(v_p), conv_s[None], kv(k_s), kv(v_s))
```

```python
import functools

import numpy as np
import jax
import jax.numpy as jnp
from jax import lax
from jax.experimental import pallas as pl
from jax.experimental.pallas import tpu as pltpu
from jax.experimental.pallas import tpu_sc as plsc

F32 = jnp.float32
BF16 = jnp.bfloat16
I32 = jnp.int32

D_MODEL = 1024
CHUNK = 64
D_CONV = 1024
CONV_W = 3
N_HEADS = 16
N_KV_HEADS = 4
HEAD_DIM = 64
GQA_GROUP = N_HEADS // N_KV_HEADS
WINDOW = 128
ROPE_THETA = 10000.0
ATTN_SCALE = HEAD_DIM ** -0.5
N_EXPERTS = 64
TOP_K = 8
N_EXPERT_GROUPS = 8
GROUP_SIZE = N_EXPERTS // N_EXPERT_GROUPS
TOPK_GROUPS = 4
D_EXPERT = 256
D_SHARED = 256
ROUTED_SCALE = 2.5
EPS = 1e-6
PAST_LEN = 4096
Q_DIM = N_HEADS * HEAD_DIM
KV_DIM = N_KV_HEADS * HEAD_DIM
OFF_GB, OFF_GC, OFF_XC, OFF_Q, OFF_K, OFF_V, OFF_GCONV, OFF_GATTN, D_IN = (
    0, 1024, 2048, 3072, 4096, 4352, 4608, 5632, 6656)

LANES = 128
SUBLANES = 8
VMEM_LIMIT = 56 * 1024 * 1024

MIX_TILE = 512
ATT_Q = 128
SAMPLE_BB = 4
PRE_TILE = 512
RANK_TILE = 512
GMM_BM = 1024
GMM_X_SLOTS = 3
GMM_SUB = 512
GMM_TAIL = 128
COMB_TILE = 256
SC_WORKERS = 32
SC_WINDOW = 96


def _const_spec(shape):
    nd = len(shape)
    return pl.BlockSpec(shape, lambda *_: (0,) * nd, pipeline_mode=pl.Buffered(1))


def _rms(x):
    return x * lax.rsqrt(jnp.mean(x * x, axis=-1, keepdims=True) + EPS)


def _sigmoid(x):
    return 1.0 / (1.0 + jnp.exp(-x))


def _silu(x):
    return x * _sigmoid(x)


def _dot(a, b):
    return jnp.dot(a, b, preferred_element_type=F32)


def _pack_bf16_pairs(x):
    half = x.shape[-1] // 2
    lo = lax.bitcast_convert_type(x[..., :half].astype(BF16).astype(F32), I32)
    hi = lax.bitcast_convert_type(x[..., half:].astype(BF16).astype(F32), I32)
    return lax.shift_right_logical(lo, 16) | hi


def _unpack_bf16_pairs(words):
    lo = lax.bitcast_convert_type(lax.shift_left(words, 16), F32)
    hi = lax.bitcast_convert_type(words & jnp.int32(-65536), F32)
    return lo, hi


def _ada_kernel(c_ref, w_ref, b_ref, o_ref):
    s = _silu(c_ref[...]).astype(BF16)
    o_ref[...] = _dot(s, w_ref[...].astype(BF16)) + b_ref[...]


def _ada(c_all, w_ada, b_ada):
    rows = c_all.shape[0]
    n_out = w_ada.shape[1]
    bn = 768
    return pl.pallas_call(
        _ada_kernel,
        grid=(n_out // bn,),
        in_specs=[pl.BlockSpec((rows, D_MODEL), lambda i: (0, 0)),
                  pl.BlockSpec((D_MODEL, bn), lambda i: (0, i)),
                  pl.BlockSpec((1, bn), lambda i: (0, i))],
        out_specs=pl.BlockSpec((rows, bn), lambda i: (0, i)),
        out_shape=jax.ShapeDtypeStruct((rows, n_out), F32),
        name="ada_mod",
    )(c_all, w_ada, b_ada.reshape(1, n_out))


def _rope(x, cos, sin_signed):
    lane = lax.broadcasted_iota(I32, (x.shape[0], LANES), 1)
    first_half = (lane % HEAD_DIM) < (HEAD_DIM // 2)
    outs = []
    for g in range(x.shape[1] // LANES):
        xg = x[:, g * LANES:(g + 1) * LANES]
        up = pltpu.roll(xg, LANES - HEAD_DIM // 2, axis=1)
        down = pltpu.roll(xg, HEAD_DIM // 2, axis=1)
        partner = jnp.where(first_half, up, down)
        outs.append(xg * cos + partner * sin_signed)
    return jnp.concatenate(outs, axis=1)


def _head_perm():
    n = np.arange(Q_DIM)
    c, par, dd = n // LANES, (n % LANES) // HEAD_DIM, n % HEAD_DIM
    pair, i = c // GQA_GROUP, c % GQA_GROUP
    return HEAD_DIM * (GQA_GROUP * (2 * pair + par) + i) + dd


def _attend_pair(q_blk, k_pair, vt_pair, sinks_ref, pair, mask, obuf, row0):
    rq = GQA_GROUP * ATT_Q
    low = lax.broadcasted_iota(I32, (ATT_Q, LANES), 1) < HEAD_DIM
    head_of_lane = lax.broadcasted_iota(I32, (1, rq), 1) // ATT_Q
    outs = []
    for par in range(2):
        g = 2 * pair + par
        keep = low if par == 0 else jnp.logical_not(low)
        cols = [q_blk[:, (GQA_GROUP * pair + i) * LANES:(GQA_GROUP * pair + i + 1) * LANES] for i in range(GQA_GROUP)]
        qg = jnp.concatenate([jnp.where(keep, c, jnp.zeros_like(c)) for c in cols], axis=0)
        st = lax.dot_general(k_pair, qg, (((1,), (1,)), ((), ())), preferred_element_type=F32)
        if mask is not None:
            st = jnp.where(mask, st, -jnp.inf)
        sink = jnp.full((1, rq), sinks_ref[g * GQA_GROUP + GQA_GROUP - 1], F32)
        for i in range(GQA_GROUP - 2, -1, -1):
            sink = jnp.where(head_of_lane == i, sinks_ref[g * GQA_GROUP + i], sink)
        m = jnp.maximum(jnp.max(st, axis=0, keepdims=True), sink)
        e = jnp.exp(st - m)
        z = jnp.sum(e, axis=0, keepdims=True) + jnp.exp(sink - m)
        ot = _dot(vt_pair[par * HEAD_DIM:(par + 1) * HEAD_DIM, :], e.astype(BF16))
        outs.append(ot / z)
    for i in range(GQA_GROUP):
        blk = jnp.concatenate([o[:, i * ATT_Q:(i + 1) * ATT_Q] for o in outs], axis=0)
        c0 = (GQA_GROUP * pair + i) * LANES
        obuf[row0:row0 + ATT_Q, c0:c0 + LANES] = blk.T


def _in_proj(hb, win_ref, lo, hi):
    return _dot(hb, win_ref[:, lo:hi])


def _mix_out(x, g1, proj_gb, conv, y_attn_in, hb, win_ref, wco_ref, wao_ref, wmo_ref):
    y_conv = _dot((proj_gb * conv).astype(BF16), wco_ref[...])
    y_attn = _dot(y_attn_in.astype(BF16), wao_ref[...])
    g_conv = _in_proj(hb, win_ref, OFF_GCONV, OFF_GATTN)
    g_attn = _in_proj(hb, win_ref, OFF_GATTN, D_IN)
    merged = _sigmoid(g_conv) * y_conv + _sigmoid(g_attn) * y_attn
    return x + g1 * _dot(merged.astype(BF16), wmo_ref[...])


def _mixer_prompt_kernel(x_ref, mod_ref, cos_ref, sin_ref, win_ref, wconv_ref, wco_ref, wao_ref, sinks_ref, wmo_ref,
                         x1_ref, conv_ref, k_ref, v_ref, ubuf, kbuf, vtbuf, obuf):
    j = pl.program_id(1)
    t = x_ref.shape[1]

    @pl.when(j == 0)
    def _():
        ubuf[0:SUBLANES, :] = jnp.zeros((SUBLANES, D_CONV), F32)
        kbuf[0:WINDOW, :] = jnp.zeros((WINDOW, KV_DIM), BF16)
        vtbuf[:, 0:WINDOW] = jnp.zeros((KV_DIM, WINDOW), BF16)

    x = x_ref[0]
    mod = mod_ref[0]
    sh1, sc1, g1 = mod[:, 0:D_MODEL], mod[:, D_MODEL:2 * D_MODEL], mod[:, 2 * D_MODEL:3 * D_MODEL]
    hb = (_rms(x) * (1.0 + sc1) + sh1).astype(BF16)

    u = _in_proj(hb, win_ref, OFF_GC, OFF_XC) * _in_proj(hb, win_ref, OFF_XC, OFF_Q)
    ubuf[SUBLANES:SUBLANES + t, :] = u
    wc = wconv_ref[...]
    conv = wc[0:1] * ubuf[SUBLANES - 2:SUBLANES - 2 + t, :] + wc[1:2] * ubuf[SUBLANES - 1:SUBLANES - 1 + t, :] + wc[2:3] * u
    conv_ref[0] = u[t - (CONV_W - 1):t]
    ubuf[SUBLANES - 2:SUBLANES, :] = u[t - (CONV_W - 1):t]

    cos, sin = cos_ref[...], sin_ref[...]
    q = (_rope(_in_proj(hb, win_ref, OFF_Q, OFF_K), cos, sin) * ATTN_SCALE).astype(BF16)
    k = _rope(_in_proj(hb, win_ref, OFF_K, OFF_V), cos, sin)
    v = _in_proj(hb, win_ref, OFF_V, OFF_GCONV)
    kbuf[WINDOW:WINDOW + t, :] = k.astype(BF16)
    vtbuf[:, WINDOW:WINDOW + t] = v.T.astype(BF16)
    k_ref[0] = k[t - WINDOW:t]
    v_ref[0] = v[t - WINDOW:t]

    nkeys = ATT_Q + WINDOW
    rq = GQA_GROUP * ATT_Q
    ki = lax.broadcasted_iota(I32, (nkeys, rq), 0)
    qi = lax.broadcasted_iota(I32, (nkeys, rq), 1) % ATT_Q
    band = ki // CHUNK - qi // CHUNK
    band_ok = (band >= 0) & (band <= WINDOW // CHUNK)
    for s in range(t // ATT_Q):
        mask = band_ok & (ki + (j * t + s * ATT_Q - WINDOW) >= 0)
        qs = q[s * ATT_Q:(s + 1) * ATT_Q]
        for pair in range(N_KV_HEADS // 2):
            k_pair = kbuf[s * ATT_Q:s * ATT_Q + nkeys, pair * LANES:(pair + 1) * LANES]
            vt_pair = vtbuf[pair * LANES:(pair + 1) * LANES, s * ATT_Q:s * ATT_Q + nkeys]
            _attend_pair(qs, k_pair, vt_pair, sinks_ref, pair, mask, obuf, s * ATT_Q)
    kbuf[0:WINDOW, :] = kbuf[t:t + WINDOW, :]
    vtbuf[:, 0:WINDOW] = vtbuf[:, t:t + WINDOW]

    gate_b = _in_proj(hb, win_ref, OFF_GB, OFF_GC)
    x1_ref[0] = _mix_out(x, g1, gate_b, conv, obuf[...], hb, win_ref, wco_ref, wao_ref, wmo_ref)


def _mixer_prompt(x, mod, cos, sin, win, wconv, wco, wao, sinks, wmo):
    b, seq, d = x.shape
    t = MIX_TILE
    return pl.pallas_call(
        _mixer_prompt_kernel,
        grid=(b, seq // t),
        in_specs=[pl.BlockSpec((1, t, d), lambda i, j: (i, j, 0)),
                  pl.BlockSpec((1, 1, 6 * d), lambda i, j: (i, 0, 0)),
                  pl.BlockSpec((t, LANES), lambda i, j: (j, 0)),
                  pl.BlockSpec((t, LANES), lambda i, j: (j, 0)),
                  _const_spec(win.shape), _const_spec(wconv.shape), _const_spec(wco.shape), _const_spec(wao.shape),
                  pl.BlockSpec(memory_space=pltpu.SMEM),
                  _const_spec(wmo.shape)],
        out_specs=[pl.BlockSpec((1, t, d), lambda i, j: (i, j, 0)),
                   pl.BlockSpec((1, CONV_W - 1, D_CONV), lambda i, j: (i, 0, 0)),
                   pl.BlockSpec((1, WINDOW, KV_DIM), lambda i, j: (i, 0, 0)),
                   pl.BlockSpec((1, WINDOW, KV_DIM), lambda i, j: (i, 0, 0))],
        out_shape=[jax.ShapeDtypeStruct((b, seq, d), F32),
                   jax.ShapeDtypeStruct((b, CONV_W - 1, D_CONV), F32),
                   jax.ShapeDtypeStruct((b, WINDOW, KV_DIM), F32),
                   jax.ShapeDtypeStruct((b, WINDOW, KV_DIM), F32)],
        scratch_shapes=[pltpu.VMEM((SUBLANES + t, D_CONV), F32),
                        pltpu.VMEM((WINDOW + t, KV_DIM), BF16),
                        pltpu.VMEM((KV_DIM, WINDOW + t), BF16),
                        pltpu.VMEM((t, Q_DIM), F32)],
        compiler_params=pltpu.CompilerParams(dimension_semantics=("arbitrary", "arbitrary"),
                                             vmem_limit_bytes=VMEM_LIMIT),
        name="mixer_prompt",
    )(x, mod, cos, sin, win, wconv, wco, wao, sinks, wmo)


def _mixer_sample_kernel(x_ref, mod_ref, cos_ref, sin_ref, ck_ref, cv_ref, sconv_ref, win_ref, wconv_ref, wco_ref,
                         wao_ref, sinks_ref, wmo_ref, x1_ref, conv_ref, k_ref, v_ref, ubuf, obuf):
    bb, t, d = x_ref.shape
    x3 = x_ref[...]
    mod = mod_ref[...]
    sh1, sc1, g1 = mod[:, :, 0:d], mod[:, :, d:2 * d], mod[:, :, 2 * d:3 * d]
    x = x3.reshape(bb * t, d)
    hb = (_rms(x3) * (1.0 + sc1) + sh1).astype(BF16).reshape(bb * t, d)

    u = _in_proj(hb, win_ref, OFF_GC, OFF_XC) * _in_proj(hb, win_ref, OFF_XC, OFF_Q)
    u3 = u.reshape(bb, t, D_CONV)
    ubuf[:, SUBLANES - 2:SUBLANES, :] = sconv_ref[...]
    ubuf[:, SUBLANES:SUBLANES + t, :] = u3
    wc = wconv_ref[...]
    conv = (wc[0:1] * ubuf[:, SUBLANES - 2:SUBLANES - 2 + t, :] + wc[1:2] * ubuf[:, SUBLANES - 1:SUBLANES - 1 + t, :]
            + wc[2:3] * u3).reshape(bb * t, D_CONV)
    conv_ref[...] = u3[:, t - (CONV_W - 1):t, :]

    cos = jnp.concatenate([cos_ref[...]] * bb, axis=0)
    sin = jnp.concatenate([sin_ref[...]] * bb, axis=0)
    q = (_rope(_in_proj(hb, win_ref, OFF_Q, OFF_K), cos, sin) * ATTN_SCALE).astype(BF16)
    k = _rope(_in_proj(hb, win_ref, OFF_K, OFF_V), cos, sin)
    v = _in_proj(hb, win_ref, OFF_V, OFF_GCONV)
    per = ATT_Q // t
    nkeys = per * (WINDOW + t)
    rq = GQA_GROUP * ATT_Q
    key_stream = lax.broadcasted_iota(I32, (nkeys, rq), 0) // (WINDOW + t)
    query_stream = (lax.broadcasted_iota(I32, (nkeys, rq), 1) % ATT_Q) // t
    mask = key_stream == query_stream
    for blk in range(bb // per):
        k_parts, v_parts = [], []
        for b in range(blk * per, (blk + 1) * per):
            kb, vb = k[b * t:(b + 1) * t], v[b * t:(b + 1) * t]
            ck, cv = ck_ref[b], cv_ref[b]
            k_ref[b] = jnp.concatenate([ck[t:WINDOW], kb], axis=0)
            v_ref[b] = jnp.concatenate([cv[t:WINDOW], vb], axis=0)
            k_parts += [ck, kb]
            v_parts += [cv, vb]
        k_all = jnp.concatenate(k_parts, axis=0).astype(BF16)
        vt_all = jnp.concatenate(v_parts, axis=0).T.astype(BF16)
        qs = q[blk * ATT_Q:(blk + 1) * ATT_Q]
        for pair in range(N_KV_HEADS // 2):
            _attend_pair(qs, k_all[:, pair * LANES:(pair + 1) * LANES], vt_all[pair * LANES:(pair + 1) * LANES, :],
                         sinks_ref, pair, mask, obuf, blk * ATT_Q)

    gate_b = _in_proj(hb, win_ref, OFF_GB, OFF_GC)
    g1f = jnp.broadcast_to(g1, (bb, t, d)).reshape(bb * t, d)
    x1_ref[...] = _mix_out(x, g1f, gate_b, conv, obuf[...], hb, win_ref, wco_ref, wao_ref, wmo_ref).reshape(bb, t, d)


def _mixer_sample(x, mod, cos, sin, ck, cv, sconv, win, wconv, wco, wao, sinks, wmo):
    b, t, d = x.shape
    bb = SAMPLE_BB
    blk = lambda *s: pl.BlockSpec((bb,) + s, lambda i: (i, 0, 0))
    return pl.pallas_call(
        _mixer_sample_kernel,
        grid=(b // bb,),
        in_specs=[blk(t, d), blk(1, 6 * d),
                  pl.BlockSpec((t, LANES), lambda i: (0, 0)), pl.BlockSpec((t, LANES), lambda i: (0, 0)),
                  blk(WINDOW, KV_DIM), blk(WINDOW, KV_DIM), blk(CONV_W - 1, D_CONV),
                  _const_spec(win.shape), _const_spec(wconv.shape), _const_spec(wco.shape), _const_spec(wao.shape),
                  pl.BlockSpec(memory_space=pltpu.SMEM),
                  _const_spec(wmo.shape)],
        out_specs=[blk(t, d), blk(CONV_W - 1, D_CONV), blk(WINDOW, KV_DIM), blk(WINDOW, KV_DIM)],
        out_shape=[jax.ShapeDtypeStruct((b, t, d), F32),
                   jax.ShapeDtypeStruct((b, CONV_W - 1, D_CONV), F32),
                   jax.ShapeDtypeStruct((b, WINDOW, KV_DIM), F32),
                   jax.ShapeDtypeStruct((b, WINDOW, KV_DIM), F32)],
        scratch_shapes=[pltpu.VMEM((bb, SUBLANES + t, D_CONV), F32),
                        pltpu.VMEM((bb * t, Q_DIM), F32)],
        compiler_params=pltpu.CompilerParams(dimension_semantics=("arbitrary",), vmem_limit_bytes=VMEM_LIMIT),
        name="mixer_sample",
    )(x, mod, cos, sin, ck, cv, sconv, win, wconv, wco, wao, sinks, wmo)


def _pre_kernel(*refs, prompt_tiles, has_sample):
    if has_sample:
        xp_ref, xs_ref, mod_ref, wsg_ref, wsu_ref, wsd_ref, wr_ref, rb_ref, h2_ref, base_ref, cw_ref = refs
    else:
        xp_ref, mod_ref, wsg_ref, wsu_ref, wsd_ref, wr_ref, rb_ref, h2_ref, base_ref, cw_ref = refs
    nc, c, d = xp_ref.shape
    t = nc * c
    x3 = xp_ref[...]
    if has_sample:
        x3 = jnp.where(pl.program_id(0) < prompt_tiles, x3, xs_ref[...])
    mod = mod_ref[...]
    sh2, sc2, g2 = mod[:, :, 0:d], mod[:, :, d:2 * d], mod[:, :, 2 * d:3 * d]
    h3 = _rms(x3) * (1.0 + sc2) + sh2
    h2 = h3.reshape(t, d)
    hb = h2.astype(BF16)
    h2_ref[...] = _pack_bf16_pairs(h2)
    shared = _dot((_silu(_dot(hb, wsg_ref[...])) * _dot(hb, wsu_ref[...])).astype(BF16), wsd_ref[...])
    base_ref[...] = x3 + g2 * shared.reshape(nc, c, d)

    logits = lax.dot_general(wr_ref[...], h2, (((1,), (1,)), ((), ())), preferred_element_type=F32,
                             precision=lax.Precision.HIGHEST)
    scores = _sigmoid(logits)
    biased = scores + rb_ref[...]
    g3 = biased.reshape(N_EXPERT_GROUPS, GROUP_SIZE, t)
    member = lax.broadcasted_iota(I32, g3.shape, 1)
    m1 = jnp.max(g3, axis=1, keepdims=True)
    first = jnp.min(jnp.where(g3 == m1, member, GROUP_SIZE), axis=1, keepdims=True)
    m2 = jnp.max(jnp.where(member == first, -jnp.inf, g3), axis=1, keepdims=True)
    gs = m1 + m2
    gidx = lax.broadcasted_iota(I32, gs.shape, 0)
    grank = jnp.zeros(gs.shape, I32)
    for o in range(N_EXPERT_GROUPS):
        other = gs[o:o + 1]
        grank += ((other > gs) | ((other == gs) & (o < gidx))).astype(I32)
    eligible = jnp.broadcast_to(grank < TOPK_GROUPS, g3.shape).reshape(N_EXPERTS, t)
    mb = jnp.where(eligible, biased, -jnp.inf)
    eidx = lax.broadcasted_iota(I32, mb.shape, 0)
    erank = jnp.zeros(mb.shape, I32)
    for o in range(N_EXPERTS):
        other = mb[o:o + 1]
        erank += ((other > mb) | ((other == mb) & (o < eidx))).astype(I32)
    sel = eligible & (erank < TOP_K)
    ssum = jnp.sum(jnp.where(sel, scores, 0.0), axis=0, keepdims=True)
    cw_ref[...] = jnp.where(sel, scores / ssum * ROUTED_SCALE, -1.0)


def _pre(x1_p, p_chunk0, ncp, x1_s, ncs, modc, wsg, wsu, wsd, wr_t, rb):
    _, c, d = x1_p.shape
    nc = PRE_TILE // c
    nchunks = ncp + ncs
    n = nchunks * c
    pt, p0 = ncp // nc, p_chunk0 // nc
    blk3 = pl.BlockSpec((nc, c, d), lambda i: (i, 0, 0))
    xs_args, xs_specs = [], []
    if ncs:
        xs_args, xs_specs = [x1_s], [pl.BlockSpec((nc, c, d), lambda i: (jnp.maximum(i - pt, 0), 0, 0))]
    return pl.pallas_call(
        functools.partial(_pre_kernel, prompt_tiles=pt, has_sample=bool(ncs)),
        grid=(nchunks // nc,),
        in_specs=[pl.BlockSpec((nc, c, d), lambda i: (p0 + jnp.minimum(i, pt - 1), 0, 0))] + xs_specs + [
                  pl.BlockSpec((nc, 1, 3 * d), lambda i: (i, 0, 0)),
                  _const_spec(wsg.shape), _const_spec(wsu.shape), _const_spec(wsd.shape),
                  _const_spec(wr_t.shape), _const_spec(rb.shape)],
        out_specs=[pl.BlockSpec((nc * c, d // 2), lambda i: (i, 0)), blk3,
                   pl.BlockSpec((N_EXPERTS, nc * c), lambda i: (0, i))],
        out_shape=[jax.ShapeDtypeStruct((n, d // 2), I32),
                   jax.ShapeDtypeStruct((nchunks, c, d), F32),
                   jax.ShapeDtypeStruct((N_EXPERTS, n), F32)],
        compiler_params=pltpu.CompilerParams(dimension_semantics=("arbitrary",), vmem_limit_bytes=VMEM_LIMIT),
        name="pre_ffn",
    )(x1_p, *xs_args, modc, wsg, wsu, wsd, wr_t, rb)


def _rank_kernel(cw_ref, rank_ref, cnt_ref, carry):
    i = pl.program_id(0)
    t = cw_ref.shape[1]

    @pl.when(i == 0)
    def _():
        carry[...] = jnp.zeros(carry.shape, F32)

    sel = (cw_ref[...] >= 0.0).astype(BF16)
    r = lax.broadcasted_iota(I32, (t, t), 0)
    c = lax.broadcasted_iota(I32, (t, t), 1)
    before = (r < c).astype(BF16)
    rank = carry[...] + _dot(sel, before)
    rank_ref[...] = rank.astype(I32)
    carry[...] = carry[...] + jnp.sum(sel.astype(F32), axis=1, keepdims=True)
    cnt_ref[...] = carry[...].astype(I32)


def _rank(cw):
    e, n = cw.shape
    t = RANK_TILE
    return pl.pallas_call(
        _rank_kernel,
        grid=(n // t,),
        in_specs=[pl.BlockSpec((e, t), lambda i: (0, i))],
        out_specs=[pl.BlockSpec((e, t), lambda i: (0, i)), pl.BlockSpec((e, 1), lambda i: (0, 0))],
        out_shape=[jax.ShapeDtypeStruct((e, n), I32), jax.ShapeDtypeStruct((e, 1), I32)],
        scratch_shapes=[pltpu.VMEM((e, 1), F32)],
        compiler_params=pltpu.CompilerParams(dimension_semantics=("arbitrary",)),
        name="expert_rank",
    )(cw)


def _slot_kernel(cw_ref, rank_ref, start_ref, pos_ref, w_ref):
    cw = cw_ref[...]
    e, t = cw.shape
    sel = cw >= 0.0
    r = lax.broadcasted_iota(I32, (e, e), 0)
    c = lax.broadcasted_iota(I32, (e, e), 1)
    lower = (c < r).astype(BF16)
    kidx = _dot(lower, sel.astype(BF16))
    posf = start_ref[...].astype(F32) + rank_ref[...].astype(F32)
    pos_rows, w_rows = [], []
    for k in range(TOP_K):
        m = sel & (kidx == float(k))
        pos_rows.append(jnp.sum(jnp.where(m, posf, 0.0), axis=0, keepdims=True))
        w_rows.append(jnp.sum(jnp.where(m, cw, 0.0), axis=0, keepdims=True))
    pos_ref[...] = jnp.concatenate(pos_rows, axis=0).astype(I32)
    w_ref[...] = jnp.concatenate(w_rows, axis=0)


def _slots(cw, rank, seg_start):
    e, n = cw.shape
    t = RANK_TILE
    return pl.pallas_call(
        _slot_kernel,
        grid=(n // t,),
        in_specs=[pl.BlockSpec((e, t), lambda i: (0, i)), pl.BlockSpec((e, t), lambda i: (0, i)),
                  pl.BlockSpec((e, 1), lambda i: (0, 0))],
        out_specs=[pl.BlockSpec((TOP_K, t), lambda i: (0, i)), pl.BlockSpec((TOP_K, t), lambda i: (0, i))],
        out_shape=[jax.ShapeDtypeStruct((TOP_K, n), I32), jax.ShapeDtypeStruct((TOP_K, n), F32)],
        compiler_params=pltpu.CompilerParams(dimension_semantics=("arbitrary",)),
        name="expert_slots",
    )(cw, rank, seg_start)


def _sc_mesh():
    return plsc.VectorSubcoreMesh(core_axis_name="c", subcore_axis_name="s")


def _sc_worker_id():
    return lax.axis_index("s") * (SC_WORKERS // 16) + lax.axis_index("c")


def _sc_dispatch(rows, pos, n_rows):
    n, d = rows.shape
    per_w = n // SC_WORKERS
    w = SC_WINDOW
    n_chunks = per_w // w

    @functools.partial(
        pl.kernel, mesh=_sc_mesh(),
        out_type=jax.ShapeDtypeStruct((n_rows, d), rows.dtype),
        scratch_types=[pltpu.VMEM((2, TOP_K, w), I32), pltpu.VMEM((2, w, d), rows.dtype),
                       pltpu.SemaphoreType.DMA((2,)), pltpu.SemaphoreType.DMA((2,)), pltpu.SemaphoreType.DMA((2,))],
        name="sc_dispatch")
    def k(rows_hbm, pos_hbm, o_hbm, idx_v, rows_v, row_sem, idx_sem, out_sem):
        wid = _sc_worker_id()
        base = wid * per_w

        def loads(c, slot):
            off = pl.multiple_of(base + c * w, SUBLANES)
            return (pltpu.make_async_copy(rows_hbm.at[pl.ds(off, w)], rows_v.at[slot], row_sem.at[slot]),
                    pltpu.make_async_copy(pos_hbm.at[wid * n_chunks + c], idx_v.at[slot], idx_sem.at[slot]))

        def scatters(slot):
            return [pltpu.make_async_copy(rows_v.at[slot], o_hbm.at[idx_v.at[slot, kk]], out_sem.at[slot])
                    for kk in range(TOP_K)]

        for cp in loads(0, 0):
            cp.start()
        for c in range(n_chunks):
            slot = c % 2
            for cp in loads(c, slot):
                cp.wait()
            for cp in scatters(slot):
                cp.start()
            if c >= 1:
                for cp in scatters(1 - slot):
                    cp.wait()
            if c + 1 < n_chunks:
                for cp in loads(c + 1, 1 - slot):
                    cp.start()
        for cp in scatters((n_chunks - 1) % 2):
            cp.wait()

    pos_chunks = pos.reshape(TOP_K, n // w, w).transpose(1, 0, 2)
    return k(rows, pos_chunks)


def _sc_collect(rows, pos_flat):
    d = rows.shape[1]
    total = pos_flat.shape[0]
    per_w = total // SC_WORKERS
    w = SC_WINDOW
    n_pairs = per_w // (2 * w)

    @functools.partial(
        pl.kernel, mesh=_sc_mesh(),
        out_type=jax.ShapeDtypeStruct((total, d), rows.dtype),
        scratch_types=[pltpu.VMEM((per_w,), I32), pltpu.VMEM((2, w, d), rows.dtype),
                       pltpu.SemaphoreType.DMA((2,)), pltpu.SemaphoreType.DMA((2,))],
        name="sc_collect")
    def k(rows_hbm, pos_hbm, o_hbm, idx_v, rows_v, in_sem, out_sem):
        base = pl.multiple_of(_sc_worker_id() * per_w, SUBLANES)
        pltpu.sync_copy(pos_hbm.at[pl.ds(base, per_w)], idx_v)

        def gather(c, slot):
            idx = idx_v.at[pl.ds(pl.multiple_of(c * w, SUBLANES), w)]
            return pltpu.make_async_copy(rows_hbm.at[idx], rows_v.at[slot], in_sem.at[slot])

        def write(c, slot):
            off = pl.multiple_of(base + c * w, SUBLANES)
            return pltpu.make_async_copy(rows_v.at[slot], o_hbm.at[pl.ds(off, w)], out_sem.at[slot])

        gather(0, 0).start()

        @pl.loop(0, n_pairs)
        def _(p):
            c0 = 2 * p
            gather(c0 + 1, 1).start()
            gather(c0, 0).wait()
            write(c0, 0).start()
            gather(c0 + 1, 1).wait()
            write(c0 + 1, 1).start()
            write(c0, 0).wait()

            @pl.when(p + 1 < n_pairs)
            def _():
                gather(c0 + 2, 0).start()

            write(c0 + 1, 1).wait()

    return k(rows, pos_flat)


def _gmm_kernel(be_ref, br_ref, nu_ref, ord_ref, nxt_ref, x_hbm, wg_hbm, wu_hbm, wd_hbm, y_ref,
                wgb, wub, wdb, xbuf, xsem, wgf, wuf, wdf, wsem):
    b = pl.program_id(0)
    n_used = nu_ref[0]
    bm = xbuf.shape[1]
    e = be_ref[b]

    def w_copies(ex, slot):
        return [pltpu.make_async_copy(src.at[ex], dst.at[slot], wsem.at[slot, i])
                for i, (src, dst) in enumerate(((wg_hbm, wgf), (wu_hbm, wuf), (wd_hbm, wdf)))]

    def x_copy(blk):
        slot = blk % GMM_X_SLOTS
        return pltpu.make_async_copy(x_hbm.at[pl.ds(pl.multiple_of(blk * bm, bm), bm)], xbuf.at[slot], xsem.at[slot])

    @pl.when(b == 0)
    def _():
        for ahead in range(GMM_X_SLOTS - 1):
            @pl.when(ahead < n_used)
            def _():
                x_copy(ahead).start()

    @pl.when(b + GMM_X_SLOTS - 1 < n_used)
    def _():
        x_copy(b + GMM_X_SLOTS - 1).start()

    prev = be_ref[jnp.maximum(b - 1, 0)]

    @pl.when((b < n_used) & ((b == 0) | (e != prev)))
    def _():
        slot = ord_ref[e] % 2

        @pl.when(b == 0)
        def _():
            for cp in w_copies(e, slot):
                cp.start()

        for cp in w_copies(e, slot):
            cp.wait()
        wgb[...] = wgf[slot].astype(BF16)
        wub[...] = wuf[slot].astype(BF16)
        wdb[...] = wdf[slot].astype(BF16)
        nxt = nxt_ref[e]

        @pl.when(nxt >= 0)
        def _():
            for cp in w_copies(nxt, 1 - slot):
                cp.start()

    @pl.when(b < n_used)
    def _():
        x_copy(b).wait()

    x_ref = xbuf.at[b % GMM_X_SLOTS]

    def expert_rows(r0, n):
        rows = pl.ds(r0, n)
        lo, hi = _unpack_bf16_pairs(x_ref[rows, :])
        xb = jnp.concatenate([lo.astype(BF16), hi.astype(BF16)], axis=1)
        mid = (_silu(_dot(xb, wgb[...])) * _dot(xb, wub[...])).astype(BF16)
        y_ref[rows, :] = _pack_bf16_pairs(_dot(mid, wdb[...]))

    n_real = br_ref[b]
    n_main = n_real // GMM_SUB

    @pl.loop(0, n_main)
    def _(i):
        expert_rows(pl.multiple_of(i * GMM_SUB, GMM_SUB), GMM_SUB)

    @pl.loop(0, (n_real - n_main * GMM_SUB + GMM_TAIL - 1) // GMM_TAIL)
    def _(i):
        expert_rows(pl.multiple_of(n_main * GMM_SUB + i * GMM_TAIL, GMM_TAIL), GMM_TAIL)


def _gmm(x_sorted, block_e, block_rows, n_used, e_ord, e_next, wg, wu, wd):
    r, half = x_sorted.shape
    d = 2 * half
    bm = GMM_BM
    any_spec = pl.BlockSpec(memory_space=pl.ANY)
    return pl.pallas_call(
        _gmm_kernel,
        grid_spec=pltpu.PrefetchScalarGridSpec(
            num_scalar_prefetch=5,
            grid=(r // bm,),
            in_specs=[any_spec, any_spec, any_spec, any_spec],
            out_specs=pl.BlockSpec((bm, half), lambda b, be, br, nu, eo, en: (jnp.minimum(b, nu[0] - 1), 0)),
            scratch_shapes=[pltpu.VMEM((d, D_EXPERT), BF16), pltpu.VMEM((d, D_EXPERT), BF16),
                            pltpu.VMEM((D_EXPERT, d), BF16),
                            pltpu.VMEM((GMM_X_SLOTS, bm, half), I32), pltpu.SemaphoreType.DMA((GMM_X_SLOTS,)),
                            pltpu.VMEM((2, d, D_EXPERT), F32), pltpu.VMEM((2, d, D_EXPERT), F32),
                            pltpu.VMEM((2, D_EXPERT, d), F32), pltpu.SemaphoreType.DMA((2, 3))]),
        out_shape=jax.ShapeDtypeStruct((r, half), I32),
        compiler_params=pltpu.CompilerParams(dimension_semantics=("arbitrary",), vmem_limit_bytes=VMEM_LIMIT),
        name="expert_gmm",
    )(block_e, block_rows, n_used, e_ord, e_next, x_sorted, wg, wu, wd)


def _combine_kernel(base_ref, mod_ref, g_ref, w_ref, gain_ref, *rest):
    y_ref = rest[-1]
    nc, c, d = base_ref.shape
    w = w_ref[...]
    acc_lo = acc_hi = None
    for k in range(TOP_K):
        lo, hi = _unpack_bf16_pairs(g_ref[k])
        wk = w[:, k:k + 1]
        acc_lo = wk * lo if k == 0 else acc_lo + wk * lo
        acc_hi = wk * hi if k == 0 else acc_hi + wk * hi
    acc = jnp.concatenate([acc_lo, acc_hi], axis=1)
    g2 = mod_ref[...][:, :, 2 * d:3 * d]
    out = base_ref[...] + g2 * acc.reshape(nc, c, d)
    y_ref[...] = _rms(out) * gain_ref[...]


def _combine(base, modc, gathered, w_tok, gain, first_chunk, n_chunks, out_chunks, out_first_chunk, out_buf=None):
    _, c, d = base.shape
    nc = COMB_TILE // c
    t = nc * c
    t0, o0 = first_chunk // nc, out_first_chunk // nc
    blk3 = pl.BlockSpec((nc, c, d), lambda i: (t0 + i, 0, 0))
    in_specs = [blk3, pl.BlockSpec((nc, 1, 3 * d), lambda i: (t0 + i, 0, 0)),
                pl.BlockSpec((TOP_K, t, d // 2), lambda i: (0, t0 + i, 0)),
                pl.BlockSpec((t, TOP_K), lambda i: (t0 + i, 0)),
                pl.BlockSpec((1, 1, d), lambda i: (0, 0, 0))]
    args = [base, modc, gathered, w_tok, gain.reshape(1, 1, d)]
    aliases = {}
    if out_buf is not None:
        in_specs.append(pl.BlockSpec(memory_space=pl.ANY))
        args.append(out_buf)
        aliases = {len(args) - 1: 0}
    return pl.pallas_call(
        _combine_kernel,
        grid=(n_chunks // nc,),
        in_specs=in_specs,
        out_specs=pl.BlockSpec((nc, c, d), lambda i: (o0 + i, 0, 0)),
        out_shape=jax.ShapeDtypeStruct((out_chunks, c, d), F32),
        input_output_aliases=aliases,
        compiler_params=pltpu.CompilerParams(dimension_semantics=("arbitrary",), vmem_limit_bytes=VMEM_LIMIT),
        name="combine_norm",
    )(*args)


def _rope_tables(pos):
    half = HEAD_DIM // 2
    inv_freq = ROPE_THETA ** (-jnp.arange(half, dtype=F32) / half)
    ang = pos.astype(F32)[:, None] * inv_freq[None, :]
    cos, sin = jnp.cos(ang), jnp.sin(ang)
    reps = LANES // HEAD_DIM
    return jnp.tile(jnp.concatenate([cos, cos], axis=1), (1, reps)), jnp.tile(jnp.concatenate([-sin, sin], axis=1), (1, reps))


def _routed_ffn(h2, cw, w_gate, w_up, w_down):
    n, half = h2.shape
    rank, counts = _rank(cw)
    bm = GMM_BM
    padded = (counts[:, 0] + bm - 1) // bm * bm
    seg_end = jnp.cumsum(padded)
    seg_start = (seg_end - padded).astype(I32)
    n_rows = n * TOP_K + N_EXPERTS * bm
    n_blocks = n_rows // bm
    block_start = jnp.arange(n_blocks, dtype=I32) * bm
    block_e = jnp.minimum(jnp.sum((seg_end[None, :] <= block_start[:, None]).astype(I32), axis=1), N_EXPERTS - 1)
    own = block_e[:, None] == jnp.arange(N_EXPERTS, dtype=I32)[None, :]
    real_end = jnp.sum(jnp.where(own, (seg_start + counts[:, 0])[None, :], 0), axis=1)
    block_rows = jnp.clip(real_end - block_start, 0, bm).astype(I32)
    n_used = (seg_end[-1:] // bm).astype(I32)
    has_rows = counts[:, 0] > 0
    eids = jnp.arange(N_EXPERTS, dtype=I32)
    e_ord = (jnp.cumsum(has_rows.astype(I32)) - has_rows.astype(I32)).astype(I32)
    later = has_rows[None, :] & (eids[None, :] > eids[:, None])
    e_next = jnp.min(jnp.where(later, eids[None, :], N_EXPERTS), axis=1)
    e_next = jnp.where(e_next == N_EXPERTS, -1, e_next).astype(I32)
    pos, w_k = _slots(cw, rank, seg_start[:, None])
    x_sorted = _sc_dispatch(h2, pos, n_rows)
    y_sorted = _gmm(x_sorted, block_e, block_rows, n_used, e_ord, e_next, w_gate, w_up, w_down)
    gathered = _sc_collect(y_sorted, pos.reshape(TOP_K * n)).reshape(TOP_K, n, half)
    return gathered, w_k


def kernel(x_prompt, x_sample, cache_k, cache_v, state_conv, c_prompt, c_sample, w_ada, b_ada, w_in, w_conv,
           w_conv_out, w_attn_o, attn_sinks, w_mix_out, w_router, router_bias, w_exp_gate, w_exp_up, w_exp_down,
           w_sh_gate, w_sh_up, w_sh_down, final_gain):
    assert w_ada.shape[0] == 1, "one layer"
    bp, seq, d = x_prompt.shape
    bs, ts, _ = x_sample.shape
    assert ts == CHUNK and seq % MIX_TILE == 0 and bs % SAMPLE_BB == 0

    c_all = jnp.concatenate([c_prompt, c_sample], axis=0)
    pad = (-c_all.shape[0]) % SUBLANES
    mod = _ada(jnp.pad(c_all, ((0, pad), (0, 0))), w_ada[0], b_ada[0])[:bp + bs]
    mod_p, mod_s = mod[:bp, None, :], mod[bp:, None, :]

    perm = _head_perm()
    w_in_l = w_in[0]
    w_in_p = jnp.concatenate([w_in_l[:, :OFF_Q], w_in_l[:, OFF_Q:OFF_K][:, perm], w_in_l[:, OFF_K:]], axis=1)
    win, wco, wao, wmo = (w.astype(BF16) for w in (w_in_p, w_conv_out[0], w_attn_o[0][perm], w_mix_out[0]))
    cos_p, sin_p = _rope_tables(jnp.arange(seq, dtype=I32))
    cos_s, sin_s = _rope_tables(PAST_LEN + jnp.arange(ts, dtype=I32))

    x1_p, conv_p, k_p, v_p = _mixer_prompt(x_prompt, mod_p, cos_p, sin_p, win, w_conv[0], wco, wao, attn_sinks[0], wmo)
    x1_s, conv_s, k_s, v_s = _mixer_sample(
        x_sample, mod_s, cos_s, sin_s, cache_k[0].reshape(bs, WINDOW, KV_DIM), cache_v[0].reshape(bs, WINDOW, KV_DIM),
        state_conv[0], win, w_conv[0], wco, wao, attn_sinks[0], wmo)

    n_p, n_s = bp * seq, bs * ts
    n = n_p + n_s
    mod2 = mod[:, None, 3 * d:]
    modc_p, modc_s = jnp.repeat(mod2[:bp], seq // CHUNK, axis=0), mod2[bp:]
    x1_pc = x1_p.reshape(n_p // CHUNK, CHUNK, d)
    wsg, wsu, wsd = (w[0].astype(BF16) for w in (w_sh_gate, w_sh_up, w_sh_down))
    wr_t, rb = w_router[0].T, router_bias[0][:, None]

    ncp, ncs = n_p // CHUNK, n_s // CHUNK
    half = (ncp + ncs) // 2
    tile_chunks = max(PRE_TILE, COMB_TILE) // CHUNK
    assert half <= ncp and half % tile_chunks == 0 and (ncp - half) % tile_chunks == 0 and ncs % tile_chunks == 0
    assert (half * CHUNK) % (SC_WORKERS * SC_WINDOW) == 0 and (half * CHUNK) % RANK_TILE == 0
    y_p = None
    for p0, np_c, ns_c in ((0, half, 0), (half, ncp - half, ncs)):
        modc = jnp.concatenate([modc_p[p0:p0 + np_c], modc_s[:ns_c]], axis=0)
        h2, base, cw = _pre(x1_pc, p0, np_c, x1_s, ns_c, modc, wsg, wsu, wsd, wr_t, rb)
        gathered, w_k = _routed_ffn(h2, cw, w_exp_gate[0], w_exp_up[0], w_exp_down[0])
        w_tok = w_k.T
        y_p = _combine(base, modc, gathered, w_tok, final_gain, 0, np_c, ncp, p0, out_buf=y_p)
        if ns_c:
            y_s = _combine(base, modc, gathered, w_tok, final_gain, np_c, ns_c, ncs, 0)

    kv = lambda a: a.reshape(1, a.shape[0], WINDOW, N_KV_HEADS, HEAD_DIM)
    return (y_p.reshape(bp, seq, d), y_s, conv_p[None], kv(k_p), kv(v_p), conv_s[None], kv(k_s), kv(v_s))
```

```python
import functools

import jax
import jax.numpy as jnp
from jax import lax
from jax.experimental import pallas as pl
from jax.experimental.pallas import tpu as pltpu
from jax.experimental.pallas import tpu_sc as plsc

F32 = jnp.float32
BF16 = jnp.bfloat16
I32 = jnp.int32

D_MODEL = 1024
CHUNK = 64
D_CONV = 1024
CONV_W = 3
N_HEADS = 16
N_KV_HEADS = 4
HEAD_DIM = 64
GQA_GROUP = N_HEADS // N_KV_HEADS
WINDOW = 128
ROPE_THETA = 10000.0
ATTN_SCALE = HEAD_DIM ** -0.5
N_EXPERTS = 64
TOP_K = 8
N_EXPERT_GROUPS = 8
GROUP_SIZE = N_EXPERTS // N_EXPERT_GROUPS
TOPK_GROUPS = 4
D_EXPERT = 256
D_SHARED = 256
ROUTED_SCALE = 2.5
EPS = 1e-6
PAST_LEN = 4096
Q_DIM = N_HEADS * HEAD_DIM
KV_DIM = N_KV_HEADS * HEAD_DIM
OFF_GB, OFF_GC, OFF_XC, OFF_Q, OFF_K, OFF_V, OFF_GCONV, OFF_GATTN, D_IN = (
    0, 1024, 2048, 3072, 4096, 4352, 4608, 5632, 6656)

LANES = 128
SUBLANES = 8
VMEM_LIMIT = 56 * 1024 * 1024

MIX_TILE = 512
ATT_Q = 128
SAMPLE_BB = 8
PRE_TILE = 512
RANK_TILE = 512
GMM_BM = 1024
GMM_X_SLOTS = 3
GMM_SUB = 512
GMM_TAIL = 128
COMB_TILE = 256
SC_WORKERS = 32
SC_WINDOW = 96


def _const_spec(shape):
    nd = len(shape)
    return pl.BlockSpec(shape, lambda *_: (0,) * nd, pipeline_mode=pl.Buffered(1))


def _rms(x):
    return x * lax.rsqrt(jnp.mean(x * x, axis=-1, keepdims=True) + EPS)


def _sigmoid(x):
    return 1.0 / (1.0 + jnp.exp(-x))


def _silu(x):
    return x * _sigmoid(x)


def _dot(a, b):
    return jnp.dot(a, b, preferred_element_type=F32)


def _pack_bf16_pairs(x):
    half = x.shape[-1] // 2
    lo = lax.bitcast_convert_type(x[..., :half].astype(BF16).astype(F32), I32)
    hi = lax.bitcast_convert_type(x[..., half:].astype(BF16).astype(F32), I32)
    return lax.shift_right_logical(lo, 16) | hi


def _unpack_bf16_pairs(words):
    lo = lax.bitcast_convert_type(lax.shift_left(words, 16), F32)
    hi = lax.bitcast_convert_type(words & jnp.int32(-65536), F32)
    return lo, hi


def _ada_kernel(c_ref, w_ref, b_ref, o_ref):
    s = _silu(c_ref[...]).astype(BF16)
    o_ref[...] = _dot(s, w_ref[...].astype(BF16)) + b_ref[...]


def _ada(c_all, w_ada, b_ada):
    rows = c_all.shape[0]
    n_out = w_ada.shape[1]
    bn = 768
    return pl.pallas_call(
        _ada_kernel,
        grid=(n_out // bn,),
        in_specs=[pl.BlockSpec((rows, D_MODEL), lambda i: (0, 0)),
                  pl.BlockSpec((D_MODEL, bn), lambda i: (0, i)),
                  pl.BlockSpec((1, bn), lambda i: (0, i))],
        out_specs=pl.BlockSpec((rows, bn), lambda i: (0, i)),
        out_shape=jax.ShapeDtypeStruct((rows, n_out), F32),
        name="ada_mod",
    )(c_all, w_ada, b_ada.reshape(1, n_out))


def _rope(x, cos, sin_signed):
    lane = lax.broadcasted_iota(I32, (x.shape[0], LANES), 1)
    first_half = (lane % HEAD_DIM) < (HEAD_DIM // 2)
    outs = []
    for g in range(x.shape[1] // LANES):
        xg = x[:, g * LANES:(g + 1) * LANES]
        up = pltpu.roll(xg, LANES - HEAD_DIM // 2, axis=1)
        down = pltpu.roll(xg, HEAD_DIM // 2, axis=1)
        partner = jnp.where(first_half, up, down)
        outs.append(xg * cos + partner * sin_signed)
    return jnp.concatenate(outs, axis=1)


def _attend_pair(q_blk, k_pair, vt_pair, sinks_ref, pair, mask, obuf, row0):
    rq = GQA_GROUP * ATT_Q
    low = lax.broadcasted_iota(I32, (ATT_Q, LANES), 1) < HEAD_DIM
    head_of_lane = lax.broadcasted_iota(I32, (1, rq), 1) // ATT_Q
    outs = []
    for par in range(2):
        g = 2 * pair + par
        keep = low if par == 0 else jnp.logical_not(low)
        cols = [q_blk[:, (GQA_GROUP * pair + i) * LANES:(GQA_GROUP * pair + i + 1) * LANES] for i in range(GQA_GROUP)]
        qg = jnp.concatenate([jnp.where(keep, c, jnp.zeros_like(c)) for c in cols], axis=0)
        st = lax.dot_general(k_pair, qg, (((1,), (1,)), ((), ())), preferred_element_type=F32)
        if mask is not None:
            st = jnp.where(mask, st, -jnp.inf)
        sink = jnp.full((1, rq), sinks_ref[g * GQA_GROUP + GQA_GROUP - 1], F32)
        for i in range(GQA_GROUP - 2, -1, -1):
            sink = jnp.where(head_of_lane == i, sinks_ref[g * GQA_GROUP + i], sink)
        m = jnp.maximum(jnp.max(st, axis=0, keepdims=True), sink)
        e = jnp.exp(st - m)
        z = jnp.sum(e, axis=0, keepdims=True) + jnp.exp(sink - m)
        ot = _dot(vt_pair[par * HEAD_DIM:(par + 1) * HEAD_DIM, :], e.astype(BF16))
        outs.append(ot / z)
    for i in range(GQA_GROUP):
        blk = jnp.concatenate([o[:, i * ATT_Q:(i + 1) * ATT_Q] for o in outs], axis=0)
        c0 = (GQA_GROUP * pair + i) * LANES
        obuf[row0:row0 + ATT_Q, c0:c0 + LANES] = blk.T


def _in_proj(hb, win_refs, lo, hi):
    w_pre, w_q, w_post = win_refs
    if hi <= OFF_Q:
        return _dot(hb, w_pre[:, lo:hi])
    if lo >= OFF_K:
        return _dot(hb, w_post[:, lo - OFF_K:hi - OFF_K])
    assert (lo, hi) == (OFF_Q, OFF_K)
    return _dot(hb, w_q[...])


def _mix_out(x, g1, proj_gb, conv, y_attn_in, hb, win_ref, wco_ref, wao_ref, wmo_ref):
    y_conv = _dot((proj_gb * conv).astype(BF16), wco_ref[...])
    y_attn = _dot(y_attn_in.astype(BF16), wao_ref[...])
    g_conv = _in_proj(hb, win_ref, OFF_GCONV, OFF_GATTN)
    g_attn = _in_proj(hb, win_ref, OFF_GATTN, D_IN)
    merged = _sigmoid(g_conv) * y_conv + _sigmoid(g_attn) * y_attn
    return x + g1 * _dot(merged.astype(BF16), wmo_ref[...])


def _mixer_prompt_kernel(x_ref, mod_ref, cos_ref, sin_ref, wpre_ref, wq_ref, wpost_ref, wconv_ref, wco_ref, wao_ref,
                         sinks_ref, wmo_ref, x1_ref, conv_ref, k_ref, v_ref, ubuf, kbuf, vtbuf, obuf):
    win_ref = (wpre_ref, wq_ref, wpost_ref)
    j = pl.program_id(1)
    t = x_ref.shape[1]

    @pl.when(j == 0)
    def _():
        ubuf[0:SUBLANES, :] = jnp.zeros((SUBLANES, D_CONV), F32)
        kbuf[0:WINDOW, :] = jnp.zeros((WINDOW, KV_DIM), BF16)
        vtbuf[:, 0:WINDOW] = jnp.zeros((KV_DIM, WINDOW), BF16)

    x = x_ref[0]
    mod = mod_ref[0]
    sh1, sc1, g1 = mod[:, 0:D_MODEL], mod[:, D_MODEL:2 * D_MODEL], mod[:, 2 * D_MODEL:3 * D_MODEL]
    hb = (_rms(x) * (1.0 + sc1) + sh1).astype(BF16)

    u = _in_proj(hb, win_ref, OFF_GC, OFF_XC) * _in_proj(hb, win_ref, OFF_XC, OFF_Q)
    ubuf[SUBLANES:SUBLANES + t, :] = u
    wc = wconv_ref[...]
    conv = wc[0:1] * ubuf[SUBLANES - 2:SUBLANES - 2 + t, :] + wc[1:2] * ubuf[SUBLANES - 1:SUBLANES - 1 + t, :] + wc[2:3] * u
    conv_ref[0] = u[t - (CONV_W - 1):t]
    ubuf[SUBLANES - 2:SUBLANES, :] = u[t - (CONV_W - 1):t]

    cos, sin = cos_ref[...], sin_ref[...]
    q = (_rope(_in_proj(hb, win_ref, OFF_Q, OFF_K), cos, sin) * ATTN_SCALE).astype(BF16)
    k = _rope(_in_proj(hb, win_ref, OFF_K, OFF_V), cos, sin)
    v = _in_proj(hb, win_ref, OFF_V, OFF_GCONV)
    kbuf[WINDOW:WINDOW + t, :] = k.astype(BF16)
    vtbuf[:, WINDOW:WINDOW + t] = v.T.astype(BF16)
    k_ref[0] = k[t - WINDOW:t]
    v_ref[0] = v[t - WINDOW:t]

    nkeys = ATT_Q + WINDOW
    rq = GQA_GROUP * ATT_Q
    ki = lax.broadcasted_iota(I32, (nkeys, rq), 0)
    qi = lax.broadcasted_iota(I32, (nkeys, rq), 1) % ATT_Q
    band = ki // CHUNK - qi // CHUNK
    band_ok = (band >= 0) & (band <= WINDOW // CHUNK)
    for s in range(t // ATT_Q):
        mask = band_ok & (ki + (j * t + s * ATT_Q - WINDOW) >= 0)
        qs = q[s * ATT_Q:(s + 1) * ATT_Q]
        for pair in range(N_KV_HEADS // 2):
            k_pair = kbuf[s * ATT_Q:s * ATT_Q + nkeys, pair * LANES:(pair + 1) * LANES]
            vt_pair = vtbuf[pair * LANES:(pair + 1) * LANES, s * ATT_Q:s * ATT_Q + nkeys]
            _attend_pair(qs, k_pair, vt_pair, sinks_ref, pair, mask, obuf, s * ATT_Q)
    kbuf[0:WINDOW, :] = kbuf[t:t + WINDOW, :]
    vtbuf[:, 0:WINDOW] = vtbuf[:, t:t + WINDOW]

    gate_b = _in_proj(hb, win_ref, OFF_GB, OFF_GC)
    x1_ref[0] = _mix_out(x, g1, gate_b, conv, obuf[...], hb, win_ref, wco_ref, wao_ref, wmo_ref)


def _mixer_prompt(x, mod, cos, sin, win, wconv, wco, wao, sinks, wmo):
    b, seq, d = x.shape
    t = MIX_TILE
    return pl.pallas_call(
        _mixer_prompt_kernel,
        grid=(b, seq // t),
        in_specs=[pl.BlockSpec((1, t, d), lambda i, j: (i, j, 0)),
                  pl.BlockSpec((1, 1, 6 * d), lambda i, j: (i, 0, 0)),
                  pl.BlockSpec((t, LANES), lambda i, j: (j, 0)),
                  pl.BlockSpec((t, LANES), lambda i, j: (j, 0)),
                  *[_const_spec(w.shape) for w in win],
                  _const_spec(wconv.shape), _const_spec(wco.shape), _const_spec(wao.shape),
                  pl.BlockSpec(memory_space=pltpu.SMEM),
                  _const_spec(wmo.shape)],
        out_specs=[pl.BlockSpec((1, t, d), lambda i, j: (i, j, 0)),
                   pl.BlockSpec((1, CONV_W - 1, D_CONV), lambda i, j: (i, 0, 0)),
                   pl.BlockSpec((1, WINDOW, KV_DIM), lambda i, j: (i, 0, 0)),
                   pl.BlockSpec((1, WINDOW, KV_DIM), lambda i, j: (i, 0, 0))],
        out_shape=[jax.ShapeDtypeStruct((b, seq, d), F32),
                   jax.ShapeDtypeStruct((b, CONV_W - 1, D_CONV), F32),
                   jax.ShapeDtypeStruct((b, WINDOW, KV_DIM), F32),
                   jax.ShapeDtypeStruct((b, WINDOW, KV_DIM), F32)],
        scratch_shapes=[pltpu.VMEM((SUBLANES + t, D_CONV), F32),
                        pltpu.VMEM((WINDOW + t, KV_DIM), BF16),
                        pltpu.VMEM((KV_DIM, WINDOW + t), BF16),
                        pltpu.VMEM((t, Q_DIM), F32)],
        compiler_params=pltpu.CompilerParams(dimension_semantics=("arbitrary", "arbitrary"),
                                             vmem_limit_bytes=VMEM_LIMIT),
        name="mixer_prompt",
    )(x, mod, cos, sin, *win, wconv, wco, wao, sinks, wmo)


def _mixer_sample_kernel(x_ref, mod_ref, cos_ref, sin_ref, ck_ref, cv_ref, sconv_ref, wpre_ref, wq_ref, wpost_ref,
                         wconv_ref, wco_ref, wao_ref, sinks_ref, wmo_ref, x1_ref, conv_ref, k_ref, v_ref, ubuf, obuf):
    win_ref = (wpre_ref, wq_ref, wpost_ref)
    bb, t, d = x_ref.shape
    x3 = x_ref[...]
    mod = mod_ref[...]
    sh1, sc1, g1 = mod[:, :, 0:d], mod[:, :, d:2 * d], mod[:, :, 2 * d:3 * d]
    x = x3.reshape(bb * t, d)
    hb = (_rms(x3) * (1.0 + sc1) + sh1).astype(BF16).reshape(bb * t, d)

    u = _in_proj(hb, win_ref, OFF_GC, OFF_XC) * _in_proj(hb, win_ref, OFF_XC, OFF_Q)
    u3 = u.reshape(bb, t, D_CONV)
    ubuf[:, SUBLANES - 2:SUBLANES, :] = sconv_ref[...]
    ubuf[:, SUBLANES:SUBLANES + t, :] = u3
    wc = wconv_ref[...]
    conv = (wc[0:1] * ubuf[:, SUBLANES - 2:SUBLANES - 2 + t, :] + wc[1:2] * ubuf[:, SUBLANES - 1:SUBLANES - 1 + t, :]
            + wc[2:3] * u3).reshape(bb * t, D_CONV)
    conv_ref[...] = u3[:, t - (CONV_W - 1):t, :]

    cos = jnp.concatenate([cos_ref[...]] * bb, axis=0)
    sin = jnp.concatenate([sin_ref[...]] * bb, axis=0)
    q = (_rope(_in_proj(hb, win_ref, OFF_Q, OFF_K), cos, sin) * ATTN_SCALE).astype(BF16)
    k = _rope(_in_proj(hb, win_ref, OFF_K, OFF_V), cos, sin)
    v = _in_proj(hb, win_ref, OFF_V, OFF_GCONV)
    per = ATT_Q // t
    nkeys = per * (WINDOW + t)
    rq = GQA_GROUP * ATT_Q
    key_stream = lax.broadcasted_iota(I32, (nkeys, rq), 0) // (WINDOW + t)
    query_stream = (lax.broadcasted_iota(I32, (nkeys, rq), 1) % ATT_Q) // t
    mask = key_stream == query_stream
    for blk in range(bb // per):
        k_parts, v_parts = [], []
        for b in range(blk * per, (blk + 1) * per):
            kb, vb = k[b * t:(b + 1) * t], v[b * t:(b + 1) * t]
            ck, cv = ck_ref[b], cv_ref[b]
            k_ref[b] = jnp.concatenate([ck[t:WINDOW], kb], axis=0)
            v_ref[b] = jnp.concatenate([cv[t:WINDOW], vb], axis=0)
            k_parts += [ck, kb]
            v_parts += [cv, vb]
        k_all = jnp.concatenate(k_parts, axis=0).astype(BF16)
        vt_all = jnp.concatenate(v_parts, axis=0).T.astype(BF16)
        qs = q[blk * ATT_Q:(blk + 1) * ATT_Q]
        for pair in range(N_KV_HEADS // 2):
            _attend_pair(qs, k_all[:, pair * LANES:(pair + 1) * LANES], vt_all[pair * LANES:(pair + 1) * LANES, :],
                         sinks_ref, pair, mask, obuf, blk * ATT_Q)

    gate_b = _in_proj(hb, win_ref, OFF_GB, OFF_GC)
    g1f = jnp.broadcast_to(g1, (bb, t, d)).reshape(bb * t, d)
    x1_ref[...] = _mix_out(x, g1f, gate_b, conv, obuf[...], hb, win_ref, wco_ref, wao_ref, wmo_ref).reshape(bb, t, d)


def _mixer_sample(x, mod, cos, sin, ck, cv, sconv, win, wconv, wco, wao, sinks, wmo):
    b, t, d = x.shape
    bb = SAMPLE_BB
    blk = lambda *s: pl.BlockSpec((bb,) + s, lambda i: (i, 0, 0))
    return pl.pallas_call(
        _mixer_sample_kernel,
        grid=(b // bb,),
        in_specs=[blk(t, d), blk(1, 6 * d),
                  pl.BlockSpec((t, LANES), lambda i: (0, 0)), pl.BlockSpec((t, LANES), lambda i: (0, 0)),
                  blk(WINDOW, KV_DIM), blk(WINDOW, KV_DIM), blk(CONV_W - 1, D_CONV),
                  *[_const_spec(w.shape) for w in win],
                  _const_spec(wconv.shape), _const_spec(wco.shape), _const_spec(wao.shape),
                  pl.BlockSpec(memory_space=pltpu.SMEM),
                  _const_spec(wmo.shape)],
        out_specs=[blk(t, d), blk(CONV_W - 1, D_CONV), blk(WINDOW, KV_DIM), blk(WINDOW, KV_DIM)],
        out_shape=[jax.ShapeDtypeStruct((b, t, d), F32),
                   jax.ShapeDtypeStruct((b, CONV_W - 1, D_CONV), F32),
                   jax.ShapeDtypeStruct((b, WINDOW, KV_DIM), F32),
                   jax.ShapeDtypeStruct((b, WINDOW, KV_DIM), F32)],
        scratch_shapes=[pltpu.VMEM((bb, SUBLANES + t, D_CONV), F32),
                        pltpu.VMEM((bb * t, Q_DIM), F32)],
        compiler_params=pltpu.CompilerParams(dimension_semantics=("arbitrary",), vmem_limit_bytes=VMEM_LIMIT),
        name="mixer_sample",
    )(x, mod, cos, sin, ck, cv, sconv, *win, wconv, wco, wao, sinks, wmo)


def _pre_kernel(*refs, prompt_tiles, has_sample):
    if has_sample:
        xp_ref, mp_ref, xs_ref, ms_ref, wsg_ref, wsu_ref, wsd_ref, wr_ref, rb_ref, h2_ref, base_ref, cw_ref = refs
    else:
        xp_ref, mp_ref, wsg_ref, wsu_ref, wsd_ref, wr_ref, rb_ref, h2_ref, base_ref, cw_ref = refs
    nc, c, d = xp_ref.shape
    t = nc * c
    x3, mod = xp_ref[...], mp_ref[...]
    if has_sample:
        is_prompt = pl.program_id(0) < prompt_tiles
        x3 = jnp.where(is_prompt, x3, xs_ref[...])
        mod = jnp.where(is_prompt, mod, ms_ref[...])
    sh2, sc2, g2 = mod[:, :, 0:d], mod[:, :, d:2 * d], mod[:, :, 2 * d:3 * d]
    h3 = _rms(x3) * (1.0 + sc2) + sh2
    h2 = h3.reshape(t, d)
    hb = h2.astype(BF16)
    h2_ref[...] = _pack_bf16_pairs(h2)
    shared = _dot((_silu(_dot(hb, wsg_ref[...])) * _dot(hb, wsu_ref[...])).astype(BF16), wsd_ref[...])
    base_ref[...] = x3 + g2 * shared.reshape(nc, c, d)

    logits = lax.dot_general(wr_ref[...], h2, (((1,), (1,)), ((), ())), preferred_element_type=F32,
                             precision=lax.Precision.HIGHEST)
    scores = _sigmoid(logits)
    biased = scores + rb_ref[...]
    g3 = biased.reshape(N_EXPERT_GROUPS, GROUP_SIZE, t)
    member = lax.broadcasted_iota(I32, g3.shape, 1)
    m1 = jnp.max(g3, axis=1, keepdims=True)
    first = jnp.min(jnp.where(g3 == m1, member, GROUP_SIZE), axis=1, keepdims=True)
    m2 = jnp.max(jnp.where(member == first, -jnp.inf, g3), axis=1, keepdims=True)
    gs = m1 + m2
    gidx = lax.broadcasted_iota(I32, gs.shape, 0)
    grank = jnp.zeros(gs.shape, I32)
    for o in range(N_EXPERT_GROUPS):
        other = gs[o:o + 1]
        grank += ((other > gs) | ((other == gs) & (o < gidx))).astype(I32)
    eligible = jnp.broadcast_to(grank < TOPK_GROUPS, g3.shape).reshape(N_EXPERTS, t)
    mb = jnp.where(eligible, biased, -jnp.inf)
    eidx = lax.broadcasted_iota(I32, mb.shape, 0)
    erank = jnp.zeros(mb.shape, I32)
    for o in range(N_EXPERTS):
        other = mb[o:o + 1]
        erank += ((other > mb) | ((other == mb) & (o < eidx))).astype(I32)
    sel = eligible & (erank < TOP_K)
    ssum = jnp.sum(jnp.where(sel, scores, 0.0), axis=0, keepdims=True)
    cw_ref[...] = jnp.where(sel, scores / ssum * ROUTED_SCALE, -1.0)


def _pre(x1_p, p_chunk0, ncp, mod_p, x1_s, ncs, mod_s, wsg, wsu, wsd, wr_t, rb):
    ncp_all, c, d = x1_p.shape
    nc = PRE_TILE // c
    nchunks = ncp + ncs
    n = nchunks * c
    pt, p0 = ncp // nc, p_chunk0 // nc
    tiles_per_stream = ncp_all // mod_p.shape[0] // nc
    blk3 = pl.BlockSpec((nc, c, d), lambda i: (i, 0, 0))
    p_tile = lambda i: p0 + jnp.minimum(i, pt - 1)
    s_tile = lambda i: jnp.maximum(i - pt, 0)
    s_args, s_specs = [], []
    if ncs:
        s_args = [x1_s, mod_s]
        s_specs = [pl.BlockSpec((nc, c, d), lambda i: (s_tile(i), 0, 0)),
                   pl.BlockSpec((nc, 1, 3 * d), lambda i: (s_tile(i), 0, 0))]
    return pl.pallas_call(
        functools.partial(_pre_kernel, prompt_tiles=pt, has_sample=bool(ncs)),
        grid=(nchunks // nc,),
        in_specs=[pl.BlockSpec((nc, c, d), lambda i: (p_tile(i), 0, 0)),
                  pl.BlockSpec((1, 1, 3 * d), lambda i: (p_tile(i) // tiles_per_stream, 0, 0))] + s_specs + [
                  _const_spec(wsg.shape), _const_spec(wsu.shape), _const_spec(wsd.shape),
                  _const_spec(wr_t.shape), _const_spec(rb.shape)],
        out_specs=[pl.BlockSpec((nc * c, d // 2), lambda i: (i, 0)), blk3,
                   pl.BlockSpec((N_EXPERTS, nc * c), lambda i: (0, i))],
        out_shape=[jax.ShapeDtypeStruct((n, d // 2), I32),
                   jax.ShapeDtypeStruct((nchunks, c, d), F32),
                   jax.ShapeDtypeStruct((N_EXPERTS, n), F32)],
        compiler_params=pltpu.CompilerParams(dimension_semantics=("arbitrary",), vmem_limit_bytes=VMEM_LIMIT),
        name="pre_ffn",
    )(x1_p, mod_p, *s_args, wsg, wsu, wsd, wr_t, rb)


def _rank_kernel(cw_ref, rank_ref, cnt_ref, carry):
    i = pl.program_id(0)
    t = cw_ref.shape[1]

    @pl.when(i == 0)
    def _():
        carry[...] = jnp.zeros(carry.shape, F32)

    sel = (cw_ref[...] >= 0.0).astype(BF16)
    r = lax.broadcasted_iota(I32, (t, t), 0)
    c = lax.broadcasted_iota(I32, (t, t), 1)
    before = (r < c).astype(BF16)
    rank = carry[...] + _dot(sel, before)
    rank_ref[...] = rank.astype(I32)
    carry[...] = carry[...] + jnp.sum(sel.astype(F32), axis=1, keepdims=True)
    cnt_ref[...] = carry[...].astype(I32)


def _rank(cw):
    e, n = cw.shape
    t = RANK_TILE
    return pl.pallas_call(
        _rank_kernel,
        grid=(n // t,),
        in_specs=[pl.BlockSpec((e, t), lambda i: (0, i))],
        out_specs=[pl.BlockSpec((e, t), lambda i: (0, i)), pl.BlockSpec((e, 1), lambda i: (0, 0))],
        out_shape=[jax.ShapeDtypeStruct((e, n), I32), jax.ShapeDtypeStruct((e, 1), I32)],
        scratch_shapes=[pltpu.VMEM((e, 1), F32)],
        compiler_params=pltpu.CompilerParams(dimension_semantics=("arbitrary",)),
        name="expert_rank",
    )(cw)


def _slot_kernel(cw_ref, rank_ref, start_ref, pos_ref, w_ref):
    cw = cw_ref[...]
    e, t = cw.shape
    sel = cw >= 0.0
    r = lax.broadcasted_iota(I32, (e, e), 0)
    c = lax.broadcasted_iota(I32, (e, e), 1)
    lower = (c < r).astype(BF16)
    kidx = _dot(lower, sel.astype(BF16))
    posf = start_ref[...].astype(F32) + rank_ref[...].astype(F32)
    pos_rows, w_rows = [], []
    for k in range(TOP_K):
        m = sel & (kidx == float(k))
        pos_rows.append(jnp.sum(jnp.where(m, posf, 0.0), axis=0, keepdims=True))
        w_rows.append(jnp.sum(jnp.where(m, cw, 0.0), axis=0, keepdims=True))
    pos_ref[...] = jnp.concatenate(pos_rows, axis=0).astype(I32)
    w_pad = jnp.concatenate(w_rows + [jnp.zeros((LANES - TOP_K, t), F32)], axis=0)
    w_ref[...] = w_pad.T[:, :TOP_K]


def _slots(cw, rank, seg_start):
    e, n = cw.shape
    t = RANK_TILE
    return pl.pallas_call(
        _slot_kernel,
        grid=(n // t,),
        in_specs=[pl.BlockSpec((e, t), lambda i: (0, i)), pl.BlockSpec((e, t), lambda i: (0, i)),
                  pl.BlockSpec((e, 1), lambda i: (0, 0))],
        out_specs=[pl.BlockSpec((TOP_K, t), lambda i: (0, i)), pl.BlockSpec((t, TOP_K), lambda i: (i, 0))],
        out_shape=[jax.ShapeDtypeStruct((TOP_K, n), I32), jax.ShapeDtypeStruct((n, TOP_K), F32)],
        compiler_params=pltpu.CompilerParams(dimension_semantics=("arbitrary",)),
        name="expert_slots",
    )(cw, rank, seg_start)


def _sc_mesh():
    return plsc.VectorSubcoreMesh(core_axis_name="c", subcore_axis_name="s")


def _sc_worker_id():
    return lax.axis_index("s") * (SC_WORKERS // 16) + lax.axis_index("c")


def _sc_dispatch(rows, pos, n_rows):
    n, d = rows.shape
    per_w = n // SC_WORKERS
    w = SC_WINDOW
    n_chunks = per_w // w

    @functools.partial(
        pl.kernel, mesh=_sc_mesh(),
        out_type=jax.ShapeDtypeStruct((n_rows, d), rows.dtype),
        scratch_types=[pltpu.VMEM((2, TOP_K, w), I32), pltpu.VMEM((2, w, d), rows.dtype),
                       pltpu.SemaphoreType.DMA((2,)), pltpu.SemaphoreType.DMA((2,)), pltpu.SemaphoreType.DMA((2,))],
        name="sc_dispatch")
    def k(rows_hbm, pos_hbm, o_hbm, idx_v, rows_v, row_sem, idx_sem, out_sem):
        wid = _sc_worker_id()
        base = wid * per_w

        def loads(c, slot):
            off = pl.multiple_of(base + c * w, SUBLANES)
            return (pltpu.make_async_copy(rows_hbm.at[pl.ds(off, w)], rows_v.at[slot], row_sem.at[slot]),
                    pltpu.make_async_copy(pos_hbm.at[wid * n_chunks + c], idx_v.at[slot], idx_sem.at[slot]))

        def scatters(slot):
            return [pltpu.make_async_copy(rows_v.at[slot], o_hbm.at[idx_v.at[slot, kk]], out_sem.at[slot])
                    for kk in range(TOP_K)]

        for cp in loads(0, 0):
            cp.start()
        for c in range(n_chunks):
            slot = c % 2
            for cp in loads(c, slot):
                cp.wait()
            for cp in scatters(slot):
                cp.start()
            if c >= 1:
                for cp in scatters(1 - slot):
                    cp.wait()
            if c + 1 < n_chunks:
                for cp in loads(c + 1, 1 - slot):
                    cp.start()
        for cp in scatters((n_chunks - 1) % 2):
            cp.wait()

    pos_chunks = pos.reshape(TOP_K, n // w, w).transpose(1, 0, 2)
    return k(rows, pos_chunks)


def _sc_collect(rows, pos_flat):
    d = rows.shape[1]
    total = pos_flat.shape[0]
    per_w = total // SC_WORKERS
    w = SC_WINDOW
    n_pairs = per_w // (2 * w)

    @functools.partial(
        pl.kernel, mesh=_sc_mesh(),
        out_type=jax.ShapeDtypeStruct((total, d), rows.dtype),
        scratch_types=[pltpu.VMEM((per_w,), I32), pltpu.VMEM((2, w, d), rows.dtype),
                       pltpu.SemaphoreType.DMA((2,)), pltpu.SemaphoreType.DMA((2,))],
        name="sc_collect")
    def k(rows_hbm, pos_hbm, o_hbm, idx_v, rows_v, in_sem, out_sem):
        base = pl.multiple_of(_sc_worker_id() * per_w, SUBLANES)
        pltpu.sync_copy(pos_hbm.at[pl.ds(base, per_w)], idx_v)

        def gather(c, slot):
            idx = idx_v.at[pl.ds(pl.multiple_of(c * w, SUBLANES), w)]
            return pltpu.make_async_copy(rows_hbm.at[idx], rows_v.at[slot], in_sem.at[slot])

        def write(c, slot):
            off = pl.multiple_of(base + c * w, SUBLANES)
            return pltpu.make_async_copy(rows_v.at[slot], o_hbm.at[pl.ds(off, w)], out_sem.at[slot])

        gather(0, 0).start()

        @pl.loop(0, n_pairs)
        def _(p):
            c0 = 2 * p
            gather(c0 + 1, 1).start()
            gather(c0, 0).wait()
            write(c0, 0).start()
            gather(c0 + 1, 1).wait()
            write(c0 + 1, 1).start()
            write(c0, 0).wait()

            @pl.when(p + 1 < n_pairs)
            def _():
                gather(c0 + 2, 0).start()

            write(c0 + 1, 1).wait()

    return k(rows, pos_flat)


def _gmm_kernel(be_ref, br_ref, nu_ref, ord_ref, nxt_ref, x_hbm, wg_hbm, wu_hbm, wd_hbm, y_ref,
                wgb, wub, wdb, xbuf, xsem, wgf, wuf, wdf, wsem):
    b = pl.program_id(0)
    n_used = nu_ref[0]
    bm = xbuf.shape[1]
    e = be_ref[b]

    def w_copies(ex, slot):
        return [pltpu.make_async_copy(src.at[ex], dst.at[slot], wsem.at[slot, i])
                for i, (src, dst) in enumerate(((wg_hbm, wgf), (wu_hbm, wuf), (wd_hbm, wdf)))]

    def x_copy(blk):
        slot = blk % GMM_X_SLOTS
        return pltpu.make_async_copy(x_hbm.at[pl.ds(pl.multiple_of(blk * bm, bm), bm)], xbuf.at[slot], xsem.at[slot])

    @pl.when(b == 0)
    def _():
        for ahead in range(GMM_X_SLOTS - 1):
            @pl.when(ahead < n_used)
            def _():
                x_copy(ahead).start()

    @pl.when(b + GMM_X_SLOTS - 1 < n_used)
    def _():
        x_copy(b + GMM_X_SLOTS - 1).start()

    prev = be_ref[jnp.maximum(b - 1, 0)]

    @pl.when((b < n_used) & ((b == 0) | (e != prev)))
    def _():
        slot = ord_ref[e] % 2

        @pl.when(b == 0)
        def _():
            for cp in w_copies(e, slot):
                cp.start()

        for cp in w_copies(e, slot):
            cp.wait()
        wgb[...] = wgf[slot].astype(BF16)
        wub[...] = wuf[slot].astype(BF16)
        wdb[...] = wdf[slot].astype(BF16)
        nxt = nxt_ref[e]

        @pl.when(nxt >= 0)
        def _():
            for cp in w_copies(nxt, 1 - slot):
                cp.start()

    @pl.when(b < n_used)
    def _():
        x_copy(b).wait()

    x_ref = xbuf.at[b % GMM_X_SLOTS]

    def expert_rows(r0, n):
        rows = pl.ds(r0, n)
        lo, hi = _unpack_bf16_pairs(x_ref[rows, :])
        xb = jnp.concatenate([lo.astype(BF16), hi.astype(BF16)], axis=1)
        mid = (_silu(_dot(xb, wgb[...])) * _dot(xb, wub[...])).astype(BF16)
        y_ref[rows, :] = _pack_bf16_pairs(_dot(mid, wdb[...]))

    n_real = br_ref[b]
    n_main = n_real // GMM_SUB

    @pl.loop(0, n_main)
    def _(i):
        expert_rows(pl.multiple_of(i * GMM_SUB, GMM_SUB), GMM_SUB)

    @pl.loop(0, (n_real - n_main * GMM_SUB + GMM_TAIL - 1) // GMM_TAIL)
    def _(i):
        expert_rows(pl.multiple_of(n_main * GMM_SUB + i * GMM_TAIL, GMM_TAIL), GMM_TAIL)


def _gmm(x_sorted, block_e, block_rows, n_used, e_ord, e_next, wg, wu, wd):
    r, half = x_sorted.shape
    d = 2 * half
    bm = GMM_BM
    any_spec = pl.BlockSpec(memory_space=pl.ANY)
    return pl.pallas_call(
        _gmm_kernel,
        grid_spec=pltpu.PrefetchScalarGridSpec(
            num_scalar_prefetch=5,
            grid=(r // bm,),
            in_specs=[any_spec, any_spec, any_spec, any_spec],
            out_specs=pl.BlockSpec((bm, half), lambda b, be, br, nu, eo, en: (jnp.minimum(b, nu[0] - 1), 0)),
            scratch_shapes=[pltpu.VMEM((d, D_EXPERT), BF16), pltpu.VMEM((d, D_EXPERT), BF16),
                            pltpu.VMEM((D_EXPERT, d), BF16),
                            pltpu.VMEM((GMM_X_SLOTS, bm, half), I32), pltpu.SemaphoreType.DMA((GMM_X_SLOTS,)),
                            pltpu.VMEM((2, d, D_EXPERT), F32), pltpu.VMEM((2, d, D_EXPERT), F32),
                            pltpu.VMEM((2, D_EXPERT, d), F32), pltpu.SemaphoreType.DMA((2, 3))]),
        out_shape=jax.ShapeDtypeStruct((r, half), I32),
        compiler_params=pltpu.CompilerParams(dimension_semantics=("arbitrary",), vmem_limit_bytes=VMEM_LIMIT),
        name="expert_gmm",
    )(block_e, block_rows, n_used, e_ord, e_next, x_sorted, wg, wu, wd)


def _combine_kernel(base_ref, mod_ref, g_ref, w_ref, gain_ref, *rest):
    y_ref = rest[-1]
    nc, c, d = base_ref.shape
    w = w_ref[...]
    acc_lo = acc_hi = None
    for k in range(TOP_K):
        lo, hi = _unpack_bf16_pairs(g_ref[k])
        wk = w[:, k:k + 1]
        acc_lo = wk * lo if k == 0 else acc_lo + wk * lo
        acc_hi = wk * hi if k == 0 else acc_hi + wk * hi
    acc = jnp.concatenate([acc_lo, acc_hi], axis=1)
    g2 = mod_ref[...][:, :, 2 * d:3 * d]
    out = base_ref[...] + g2 * acc.reshape(nc, c, d)
    y_ref[...] = _rms(out) * gain_ref[...]


def _combine(base, mod, gathered, w_tok, gain, first_chunk, n_chunks, out_chunks, out_first_chunk, out_buf=None):
    _, c, d = base.shape
    nc = COMB_TILE // c
    t = nc * c
    t0, o0 = first_chunk // nc, out_first_chunk // nc
    chunks_per_stream = out_chunks // mod.shape[0]
    if chunks_per_stream == 1:
        mod_spec = pl.BlockSpec((nc, 1, 3 * d), lambda i: (o0 + i, 0, 0))
    else:
        assert chunks_per_stream % nc == 0
        mod_spec = pl.BlockSpec((1, 1, 3 * d), lambda i: ((o0 + i) * nc // chunks_per_stream, 0, 0))
    blk3 = pl.BlockSpec((nc, c, d), lambda i: (t0 + i, 0, 0))
    in_specs = [blk3, mod_spec,
                pl.BlockSpec((TOP_K, t, d // 2), lambda i: (0, t0 + i, 0)),
                pl.BlockSpec((t, TOP_K), lambda i: (t0 + i, 0)),
                pl.BlockSpec((1, 1, d), lambda i: (0, 0, 0))]
    args = [base, mod, gathered, w_tok, gain.reshape(1, 1, d)]
    aliases = {}
    if out_buf is not None:
        in_specs.append(pl.BlockSpec(memory_space=pl.ANY))
        args.append(out_buf)
        aliases = {len(args) - 1: 0}
    return pl.pallas_call(
        _combine_kernel,
        grid=(n_chunks // nc,),
        in_specs=in_specs,
        out_specs=pl.BlockSpec((nc, c, d), lambda i: (o0 + i, 0, 0)),
        out_shape=jax.ShapeDtypeStruct((out_chunks, c, d), F32),
        input_output_aliases=aliases,
        compiler_params=pltpu.CompilerParams(dimension_semantics=("arbitrary",), vmem_limit_bytes=VMEM_LIMIT),
        name="combine_norm",
    )(*args)


def _rope_tables(pos):
    half = HEAD_DIM // 2
    inv_freq = ROPE_THETA ** (-jnp.arange(half, dtype=F32) / half)
    ang = pos.astype(F32)[:, None] * inv_freq[None, :]
    cos, sin = jnp.cos(ang), jnp.sin(ang)
    reps = LANES // HEAD_DIM
    return jnp.tile(jnp.concatenate([cos, cos], axis=1), (1, reps)), jnp.tile(jnp.concatenate([-sin, sin], axis=1), (1, reps))


def _routed_ffn(h2, cw, w_gate, w_up, w_down):
    n, half = h2.shape
    rank, counts = _rank(cw)
    bm = GMM_BM
    padded = (counts[:, 0] + bm - 1) // bm * bm
    seg_end = jnp.cumsum(padded)
    seg_start = (seg_end - padded).astype(I32)
    n_rows = n * TOP_K + N_EXPERTS * bm
    n_blocks = n_rows // bm
    block_start = jnp.arange(n_blocks, dtype=I32) * bm
    block_e = jnp.minimum(jnp.sum((seg_end[None, :] <= block_start[:, None]).astype(I32), axis=1), N_EXPERTS - 1)
    own = block_e[:, None] == jnp.arange(N_EXPERTS, dtype=I32)[None, :]
    real_end = jnp.sum(jnp.where(own, (seg_start + counts[:, 0])[None, :], 0), axis=1)
    block_rows = jnp.clip(real_end - block_start, 0, bm).astype(I32)
    n_used = (seg_end[-1:] // bm).astype(I32)
    has_rows = counts[:, 0] > 0
    eids = jnp.arange(N_EXPERTS, dtype=I32)
    e_ord = (jnp.cumsum(has_rows.astype(I32)) - has_rows.astype(I32)).astype(I32)
    later = has_rows[None, :] & (eids[None, :] > eids[:, None])
    e_next = jnp.min(jnp.where(later, eids[None, :], N_EXPERTS), axis=1)
    e_next = jnp.where(e_next == N_EXPERTS, -1, e_next).astype(I32)
    pos, w_k = _slots(cw, rank, seg_start[:, None])
    x_sorted = _sc_dispatch(h2, pos, n_rows)
    y_sorted = _gmm(x_sorted, block_e, block_rows, n_used, e_ord, e_next, w_gate, w_up, w_down)
    gathered = _sc_collect(y_sorted, pos.reshape(TOP_K * n)).reshape(TOP_K, n, half)
    return gathered, w_k


def kernel(x_prompt, x_sample, cache_k, cache_v, state_conv, c_prompt, c_sample, w_ada, b_ada, w_in, w_conv,
           w_conv_out, w_attn_o, attn_sinks, w_mix_out, w_router, router_bias, w_exp_gate, w_exp_up, w_exp_down,
           w_sh_gate, w_sh_up, w_sh_down, final_gain):
    assert w_ada.shape[0] == 1, "one layer"
    bp, seq, d = x_prompt.shape
    bs, ts, _ = x_sample.shape
    assert ts == CHUNK and seq % MIX_TILE == 0 and bs % SAMPLE_BB == 0

    c_all = jnp.concatenate([c_prompt, c_sample], axis=0)
    pad = (-c_all.shape[0]) % SUBLANES
    mod = _ada(jnp.pad(c_all, ((0, pad), (0, 0))), w_ada[0], b_ada[0])[:bp + bs]
    mod_p, mod_s = mod[:bp, None, :], mod[bp:, None, :]

    head_axes = (N_KV_HEADS // 2, 2, GQA_GROUP, HEAD_DIM)
    w_in_l = w_in[0]
    w_q = w_in_l[:, OFF_Q:OFF_K].reshape((d,) + head_axes).transpose(0, 1, 3, 2, 4).reshape(d, Q_DIM)
    w_o = w_attn_o[0].reshape(head_axes + (d,)).transpose(0, 2, 1, 3, 4).reshape(Q_DIM, d)
    win = tuple(w.astype(BF16) for w in (w_in_l[:, :OFF_Q], w_q, w_in_l[:, OFF_K:]))
    wco, wao, wmo = (w.astype(BF16) for w in (w_conv_out[0], w_o, w_mix_out[0]))
    cos_p, sin_p = _rope_tables(jnp.arange(seq, dtype=I32))
    cos_s, sin_s = _rope_tables(PAST_LEN + jnp.arange(ts, dtype=I32))

    x1_p, conv_p, k_p, v_p = _mixer_prompt(x_prompt, mod_p, cos_p, sin_p, win, w_conv[0], wco, wao, attn_sinks[0], wmo)
    x1_s, conv_s, k_s, v_s = _mixer_sample(
        x_sample, mod_s, cos_s, sin_s, cache_k[0].reshape(bs, WINDOW, KV_DIM), cache_v[0].reshape(bs, WINDOW, KV_DIM),
        state_conv[0], win, w_conv[0], wco, wao, attn_sinks[0], wmo)

    n_p, n_s = bp * seq, bs * ts
    n = n_p + n_s
    mod2_p, mod2_s = mod[:bp, None, 3 * d:], mod[bp:, None, 3 * d:]
    x1_pc = x1_p.reshape(n_p // CHUNK, CHUNK, d)
    wsg, wsu, wsd = (w[0].astype(BF16) for w in (w_sh_gate, w_sh_up, w_sh_down))
    wr_t, rb = w_router[0].T, router_bias[0][:, None]

    ncp, ncs = n_p // CHUNK, n_s // CHUNK
    half = (ncp + ncs) // 2
    tile_chunks = max(PRE_TILE, COMB_TILE) // CHUNK
    assert half <= ncp and half % tile_chunks == 0 and (ncp - half) % tile_chunks == 0 and ncs % tile_chunks == 0
    assert (half * CHUNK) % (SC_WORKERS * SC_WINDOW) == 0 and (half * CHUNK) % RANK_TILE == 0
    y_p = None
    for p0, np_c, ns_c in ((0, half, 0), (half, ncp - half, ncs)):
        h2, base, cw = _pre(x1_pc, p0, np_c, mod2_p, x1_s, ns_c, mod2_s, wsg, wsu, wsd, wr_t, rb)
        gathered, w_tok = _routed_ffn(h2, cw, w_exp_gate[0], w_exp_up[0], w_exp_down[0])
        y_p = _combine(base, mod2_p, gathered, w_tok, final_gain, 0, np_c, ncp, p0, out_buf=y_p)
        if ns_c:
            y_s = _combine(base, mod2_s, gathered, w_tok, final_gain, np_c, ns_c, ncs, 0)

    kv = lambda a: a.reshape(1, a.shape[0], WINDOW, N_KV_HEADS, HEAD_DIM)
    return (y_p.reshape(bp, seq, d), y_s, conv_p[None], kv(k_p), kv(v_p), conv_s[None], kv(k_s), kv(v_s))
```

```python
import functools

import jax
import jax.numpy as jnp
from jax import lax
from jax.experimental import pallas as pl
from jax.experimental.pallas import tpu as pltpu
from jax.experimental.pallas import tpu_sc as plsc

F32 = jnp.float32
BF16 = jnp.bfloat16
I32 = jnp.int32

D_MODEL = 1024
CHUNK = 64
D_CONV = 1024
CONV_W = 3
N_HEADS = 16
N_KV_HEADS = 4
HEAD_DIM = 64
GQA_GROUP = N_HEADS // N_KV_HEADS
WINDOW = 128
ROPE_THETA = 10000.0
ATTN_SCALE = HEAD_DIM ** -0.5
N_EXPERTS = 64
TOP_K = 8
N_EXPERT_GROUPS = 8
GROUP_SIZE = N_EXPERTS // N_EXPERT_GROUPS
TOPK_GROUPS = 4
D_EXPERT = 256
D_SHARED = 256
ROUTED_SCALE = 2.5
EPS = 1e-6
PAST_LEN = 4096
Q_DIM = N_HEADS * HEAD_DIM
KV_DIM = N_KV_HEADS * HEAD_DIM
OFF_GB, OFF_GC, OFF_XC, OFF_Q, OFF_K, OFF_V, OFF_GCONV, OFF_GATTN, D_IN = (
    0, 1024, 2048, 3072, 4096, 4352, 4608, 5632, 6656)

LANES = 128
SUBLANES = 8
VMEM_LIMIT = 56 * 1024 * 1024

MIX_TILE = 512
ATT_Q = 128
MIX_SIDE_COLS = 256
SAMPLE_BB = 8
PRE_TILE = 512
RANK_TILE = 512
GMM_BM = 1024
GMM_X_SLOTS = 3
GMM_SUB = 512
GMM_TAIL = 128
COMB_TILE = 256
SC_WORKERS = 32
SC_WINDOW = 96


def _const_spec(shape):
    nd = len(shape)
    return pl.BlockSpec(shape, lambda *_: (0,) * nd, pipeline_mode=pl.Buffered(1))


def _rms(x):
    return x * lax.rsqrt(jnp.mean(x * x, axis=-1, keepdims=True) + EPS)


def _sigmoid(x):
    return 1.0 / (1.0 + jnp.exp(-x))


def _silu(x):
    return x * _sigmoid(x)


def _dot(a, b):
    return jnp.dot(a, b, preferred_element_type=F32)


def _pack_bf16_pairs(x):
    half = x.shape[-1] // 2
    lo = lax.bitcast_convert_type(x[..., :half].astype(BF16).astype(F32), I32)
    hi = lax.bitcast_convert_type(x[..., half:].astype(BF16).astype(F32), I32)
    return lax.shift_right_logical(lo, 16) | hi


def _unpack_bf16_pairs(words):
    lo = lax.bitcast_convert_type(lax.shift_left(words, 16), F32)
    hi = lax.bitcast_convert_type(words & jnp.int32(-65536), F32)
    return lo, hi


def _ada_kernel(c_ref, w_ref, b_ref, o_ref):
    s = _silu(c_ref[...]).astype(BF16)
    o_ref[...] = _dot(s, w_ref[...].astype(BF16)) + b_ref[...]


def _ada(c_all, w_ada, b_ada):
    rows = c_all.shape[0]
    n_out = w_ada.shape[1]
    bn = 768
    return pl.pallas_call(
        _ada_kernel,
        grid=(n_out // bn,),
        in_specs=[pl.BlockSpec((rows, D_MODEL), lambda i: (0, 0)),
                  pl.BlockSpec((D_MODEL, bn), lambda i: (0, i)),
                  pl.BlockSpec((1, bn), lambda i: (0, i))],
        out_specs=pl.BlockSpec((rows, bn), lambda i: (0, i)),
        out_shape=jax.ShapeDtypeStruct((rows, n_out), F32),
        name="ada_mod",
    )(c_all, w_ada, b_ada.reshape(1, n_out))


def _rope(x, cos, sin_signed):
    lane = lax.broadcasted_iota(I32, (x.shape[0], LANES), 1)
    first_half = (lane % HEAD_DIM) < (HEAD_DIM // 2)
    outs = []
    for g in range(x.shape[1] // LANES):
        xg = x[:, g * LANES:(g + 1) * LANES]
        up = pltpu.roll(xg, LANES - HEAD_DIM // 2, axis=1)
        down = pltpu.roll(xg, HEAD_DIM // 2, axis=1)
        partner = jnp.where(first_half, up, down)
        outs.append(xg * cos + partner * sin_signed)
    return jnp.concatenate(outs, axis=1)


def _attention(blocks, sinks_ref, obuf, between):
    rq = GQA_GROUP * ATT_Q
    low = lax.broadcasted_iota(I32, (ATT_Q, LANES), 1) < HEAD_DIM
    head_of_lane = lax.broadcasted_iota(I32, (1, rq), 1) // ATT_Q
    units = [(b, pair, par) for b in range(len(blocks)) for pair in range(N_KV_HEADS // 2) for par in range(2)]
    loaded = {}

    def scores(u):
        b, pair, par = units[u]
        if b not in loaded:
            loaded.clear()
            loaded[b] = blocks[b]()
        q_blk, k_of_pair, _, mask, _ = loaded[b]
        keep = low if par == 0 else jnp.logical_not(low)
        cols = [q_blk[:, (GQA_GROUP * pair + i) * LANES:(GQA_GROUP * pair + i + 1) * LANES] for i in range(GQA_GROUP)]
        qg = jnp.concatenate([jnp.where(keep, c, jnp.zeros_like(c)) for c in cols], axis=0)
        st = lax.dot_general(k_of_pair(pair), qg, (((1,), (1,)), ((), ())), preferred_element_type=F32)
        vt = loaded[b][2](pair)[par * HEAD_DIM:(par + 1) * HEAD_DIM, :]
        return jnp.where(mask, st, -jnp.inf), vt, loaded[b][4]

    outs = []
    nxt = scores(0)
    for u, (b, pair, par) in enumerate(units):
        st, vt, row0 = nxt
        if u + 1 < len(units):
            nxt = scores(u + 1)
        if between:
            between.pop(0)()
        g = 2 * pair + par
        sink = jnp.full((1, rq), sinks_ref[g * GQA_GROUP + GQA_GROUP - 1], F32)
        for i in range(GQA_GROUP - 2, -1, -1):
            sink = jnp.where(head_of_lane == i, sinks_ref[g * GQA_GROUP + i], sink)
        m = jnp.maximum(jnp.max(st, axis=0, keepdims=True), sink)
        e = jnp.exp(st - m)
        z = jnp.sum(e, axis=0, keepdims=True) + jnp.exp(sink - m)
        outs.append(_dot(vt, e.astype(BF16)) / z)
        if par == 1:
            for i in range(GQA_GROUP):
                blk = jnp.concatenate([o[:, i * ATT_Q:(i + 1) * ATT_Q] for o in outs], axis=0)
                c0 = (GQA_GROUP * pair + i) * LANES
                obuf[row0:row0 + ATT_Q, c0:c0 + LANES] = blk.T
            outs = []
    for step in between:
        step()


def _in_proj(hb, win_refs, lo, hi):
    w_pre, w_q, w_post = win_refs
    if hi <= OFF_Q:
        return _dot(hb, w_pre[:, lo:hi])
    if lo >= OFF_K:
        return _dot(hb, w_post[:, lo - OFF_K:hi - OFF_K])
    assert (lo, hi) == (OFF_Q, OFF_K)
    return _dot(hb, w_q[...])


def _attention_free_steps(hb, conv, win_ref, wco_ref):
    out, parts = {}, {}
    n_parts = D_MODEL // MIX_SIDE_COLS

    def step(name, piece, compute):
        def run():
            parts.setdefault(name, []).append(compute(piece * MIX_SIDE_COLS, (piece + 1) * MIX_SIDE_COLS))
            if piece == n_parts - 1:
                out[name] = jnp.concatenate(parts.pop(name), axis=1)
        return run

    def conv_in():
        if "conv_in" not in out:
            out["conv_in"] = (out.pop("gate_b") * conv).astype(BF16)
        return out["conv_in"]

    computes = [("gate_b", lambda lo, hi: _in_proj(hb, win_ref, OFF_GB + lo, OFF_GB + hi)),
                ("g_conv", lambda lo, hi: _in_proj(hb, win_ref, OFF_GCONV + lo, OFF_GCONV + hi)),
                ("g_attn", lambda lo, hi: _in_proj(hb, win_ref, OFF_GATTN + lo, OFF_GATTN + hi)),
                ("y_conv", lambda lo, hi: _dot(conv_in(), wco_ref[:, lo:hi]))]
    return [step(name, p, fn) for name, fn in computes for p in range(n_parts)], out


def _mix_out(x, g1, side, y_attn_in, wao_ref, wmo_ref):
    y_attn = _dot(y_attn_in.astype(BF16), wao_ref[...])
    merged = _sigmoid(side["g_conv"]) * side["y_conv"] + _sigmoid(side["g_attn"]) * y_attn
    return x + g1 * _dot(merged.astype(BF16), wmo_ref[...])


def _mixer_prompt_kernel(x_ref, mod_ref, cos_ref, sin_ref, wpre_ref, wq_ref, wpost_ref, wconv_ref, wco_ref, wao_ref,
                         sinks_ref, wmo_ref, x1_ref, conv_ref, k_ref, v_ref, ubuf, kbuf, vtbuf, obuf):
    win_ref = (wpre_ref, wq_ref, wpost_ref)
    j = pl.program_id(1)
    t = x_ref.shape[1]

    @pl.when(j == 0)
    def _():
        ubuf[0:SUBLANES, :] = jnp.zeros((SUBLANES, D_CONV), F32)
        kbuf[0:WINDOW, :] = jnp.zeros((WINDOW, KV_DIM), BF16)
        vtbuf[:, 0:WINDOW] = jnp.zeros((KV_DIM, WINDOW), BF16)

    x = x_ref[0]
    mod = mod_ref[0]
    sh1, sc1, g1 = mod[:, 0:D_MODEL], mod[:, D_MODEL:2 * D_MODEL], mod[:, 2 * D_MODEL:3 * D_MODEL]
    hb = (_rms(x) * (1.0 + sc1) + sh1).astype(BF16)

    u = _in_proj(hb, win_ref, OFF_GC, OFF_XC) * _in_proj(hb, win_ref, OFF_XC, OFF_Q)
    ubuf[SUBLANES:SUBLANES + t, :] = u
    wc = wconv_ref[...]
    conv = wc[0:1] * ubuf[SUBLANES - 2:SUBLANES - 2 + t, :] + wc[1:2] * ubuf[SUBLANES - 1:SUBLANES - 1 + t, :] + wc[2:3] * u
    conv_ref[0] = u[t - (CONV_W - 1):t]
    ubuf[SUBLANES - 2:SUBLANES, :] = u[t - (CONV_W - 1):t]

    cos, sin = cos_ref[...], sin_ref[...]
    q = (_rope(_in_proj(hb, win_ref, OFF_Q, OFF_K), cos, sin) * ATTN_SCALE).astype(BF16)
    k = _rope(_in_proj(hb, win_ref, OFF_K, OFF_V), cos, sin)
    v = _in_proj(hb, win_ref, OFF_V, OFF_GCONV)
    kbuf[WINDOW:WINDOW + t, :] = k.astype(BF16)
    vtbuf[:, WINDOW:WINDOW + t] = v.T.astype(BF16)
    k_ref[0] = k[t - WINDOW:t]
    v_ref[0] = v[t - WINDOW:t]

    nkeys = ATT_Q + WINDOW
    rq = GQA_GROUP * ATT_Q
    ki = lax.broadcasted_iota(I32, (nkeys, rq), 0)
    qi = lax.broadcasted_iota(I32, (nkeys, rq), 1) % ATT_Q
    band = ki // CHUNK - qi // CHUNK
    band_ok = (band >= 0) & (band <= WINDOW // CHUNK)
    def block(s):
        def load():
            mask = band_ok & (ki + (j * t + s * ATT_Q - WINDOW) >= 0)
            k_of_pair = lambda pair: kbuf[s * ATT_Q:s * ATT_Q + nkeys, pair * LANES:(pair + 1) * LANES]
            vt_of_pair = lambda pair: vtbuf[pair * LANES:(pair + 1) * LANES, s * ATT_Q:s * ATT_Q + nkeys]
            return q[s * ATT_Q:(s + 1) * ATT_Q], k_of_pair, vt_of_pair, mask, s * ATT_Q
        return load

    steps, side = _attention_free_steps(hb, conv, win_ref, wco_ref)
    _attention([block(s) for s in range(t // ATT_Q)], sinks_ref, obuf, steps)
    kbuf[0:WINDOW, :] = kbuf[t:t + WINDOW, :]
    vtbuf[:, 0:WINDOW] = vtbuf[:, t:t + WINDOW]

    x1_ref[0] = _mix_out(x, g1, side, obuf[...], wao_ref, wmo_ref)


def _mixer_prompt(x, mod, cos, sin, win, wconv, wco, wao, sinks, wmo):
    b, seq, d = x.shape
    t = MIX_TILE
    return pl.pallas_call(
        _mixer_prompt_kernel,
        grid=(b, seq // t),
        in_specs=[pl.BlockSpec((1, t, d), lambda i, j: (i, j, 0)),
                  pl.BlockSpec((1, 1, 6 * d), lambda i, j: (i, 0, 0)),
                  pl.BlockSpec((t, LANES), lambda i, j: (j, 0)),
                  pl.BlockSpec((t, LANES), lambda i, j: (j, 0)),
                  *[_const_spec(w.shape) for w in win],
                  _const_spec(wconv.shape), _const_spec(wco.shape), _const_spec(wao.shape),
                  pl.BlockSpec(memory_space=pltpu.SMEM),
                  _const_spec(wmo.shape)],
        out_specs=[pl.BlockSpec((1, t, d), lambda i, j: (i, j, 0)),
                   pl.BlockSpec((1, CONV_W - 1, D_CONV), lambda i, j: (i, 0, 0)),
                   pl.BlockSpec((1, WINDOW, KV_DIM), lambda i, j: (i, 0, 0)),
                   pl.BlockSpec((1, WINDOW, KV_DIM), lambda i, j: (i, 0, 0))],
        out_shape=[jax.ShapeDtypeStruct((b, seq, d), F32),
                   jax.ShapeDtypeStruct((b, CONV_W - 1, D_CONV), F32),
                   jax.ShapeDtypeStruct((b, WINDOW, KV_DIM), F32),
                   jax.ShapeDtypeStruct((b, WINDOW, KV_DIM), F32)],
        scratch_shapes=[pltpu.VMEM((SUBLANES + t, D_CONV), F32),
                        pltpu.VMEM((WINDOW + t, KV_DIM), BF16),
                        pltpu.VMEM((KV_DIM, WINDOW + t), BF16),
                        pltpu.VMEM((t, Q_DIM), F32)],
        compiler_params=pltpu.CompilerParams(dimension_semantics=("arbitrary", "arbitrary"),
                                             vmem_limit_bytes=VMEM_LIMIT),
        name="mixer_prompt",
    )(x, mod, cos, sin, *win, wconv, wco, wao, sinks, wmo)


def _mixer_sample_kernel(x_ref, mod_ref, cos_ref, sin_ref, ck_ref, cv_ref, sconv_ref, wpre_ref, wq_ref, wpost_ref,
                         wconv_ref, wco_ref, wao_ref, sinks_ref, wmo_ref, x1_ref, conv_ref, k_ref, v_ref, ubuf, obuf):
    win_ref = (wpre_ref, wq_ref, wpost_ref)
    bb, t, d = x_ref.shape
    x3 = x_ref[...]
    mod = mod_ref[...]
    sh1, sc1, g1 = mod[:, :, 0:d], mod[:, :, d:2 * d], mod[:, :, 2 * d:3 * d]
    x = x3.reshape(bb * t, d)
    hb = (_rms(x3) * (1.0 + sc1) + sh1).astype(BF16).reshape(bb * t, d)

    u = _in_proj(hb, win_ref, OFF_GC, OFF_XC) * _in_proj(hb, win_ref, OFF_XC, OFF_Q)
    u3 = u.reshape(bb, t, D_CONV)
    ubuf[:, SUBLANES - 2:SUBLANES, :] = sconv_ref[...]
    ubuf[:, SUBLANES:SUBLANES + t, :] = u3
    wc = wconv_ref[...]
    conv = (wc[0:1] * ubuf[:, SUBLANES - 2:SUBLANES - 2 + t, :] + wc[1:2] * ubuf[:, SUBLANES - 1:SUBLANES - 1 + t, :]
            + wc[2:3] * u3).reshape(bb * t, D_CONV)
    conv_ref[...] = u3[:, t - (CONV_W - 1):t, :]

    cos = jnp.concatenate([cos_ref[...]] * bb, axis=0)
    sin = jnp.concatenate([sin_ref[...]] * bb, axis=0)
    q = (_rope(_in_proj(hb, win_ref, OFF_Q, OFF_K), cos, sin) * ATTN_SCALE).astype(BF16)
    k = _rope(_in_proj(hb, win_ref, OFF_K, OFF_V), cos, sin)
    v = _in_proj(hb, win_ref, OFF_V, OFF_GCONV)
    per = ATT_Q // t
    nkeys = per * (WINDOW + t)
    rq = GQA_GROUP * ATT_Q
    key_stream = lax.broadcasted_iota(I32, (nkeys, rq), 0) // (WINDOW + t)
    query_stream = (lax.broadcasted_iota(I32, (nkeys, rq), 1) % ATT_Q) // t
    mask = key_stream == query_stream
    def block(blk):
        def load():
            k_parts, v_parts = [], []
            for b in range(blk * per, (blk + 1) * per):
                kb, vb = k[b * t:(b + 1) * t], v[b * t:(b + 1) * t]
                ck, cv = ck_ref[b], cv_ref[b]
                k_ref[b] = jnp.concatenate([ck[t:WINDOW], kb], axis=0)
                v_ref[b] = jnp.concatenate([cv[t:WINDOW], vb], axis=0)
                k_parts += [ck, kb]
                v_parts += [cv, vb]
            k_all = jnp.concatenate(k_parts, axis=0).astype(BF16)
            vt_all = jnp.concatenate(v_parts, axis=0).T.astype(BF16)
            k_of_pair = lambda pair: k_all[:, pair * LANES:(pair + 1) * LANES]
            vt_of_pair = lambda pair: vt_all[pair * LANES:(pair + 1) * LANES, :]
            return q[blk * ATT_Q:(blk + 1) * ATT_Q], k_of_pair, vt_of_pair, mask, blk * ATT_Q
        return load

    steps, side = _attention_free_steps(hb, conv, win_ref, wco_ref)
    _attention([block(blk) for blk in range(bb // per)], sinks_ref, obuf, steps)

    g1f = jnp.broadcast_to(g1, (bb, t, d)).reshape(bb * t, d)
    x1_ref[...] = _mix_out(x, g1f, side, obuf[...], wao_ref, wmo_ref).reshape(bb, t, d)


def _mixer_sample(x, mod, cos, sin, ck, cv, sconv, win, wconv, wco, wao, sinks, wmo):
    b, t, d = x.shape
    bb = SAMPLE_BB
    blk = lambda *s: pl.BlockSpec((bb,) + s, lambda i: (i, 0, 0))
    return pl.pallas_call(
        _mixer_sample_kernel,
        grid=(b // bb,),
        in_specs=[blk(t, d), blk(1, 6 * d),
                  pl.BlockSpec((t, LANES), lambda i: (0, 0)), pl.BlockSpec((t, LANES), lambda i: (0, 0)),
                  blk(WINDOW, KV_DIM), blk(WINDOW, KV_DIM), blk(CONV_W - 1, D_CONV),
                  *[_const_spec(w.shape) for w in win],
                  _const_spec(wconv.shape), _const_spec(wco.shape), _const_spec(wao.shape),
                  pl.BlockSpec(memory_space=pltpu.SMEM),
                  _const_spec(wmo.shape)],
        out_specs=[blk(t, d), blk(CONV_W - 1, D_CONV), blk(WINDOW, KV_DIM), blk(WINDOW, KV_DIM)],
        out_shape=[jax.ShapeDtypeStruct((b, t, d), F32),
                   jax.ShapeDtypeStruct((b, CONV_W - 1, D_CONV), F32),
                   jax.ShapeDtypeStruct((b, WINDOW, KV_DIM), F32),
                   jax.ShapeDtypeStruct((b, WINDOW, KV_DIM), F32)],
        scratch_shapes=[pltpu.VMEM((bb, SUBLANES + t, D_CONV), F32),
                        pltpu.VMEM((bb * t, Q_DIM), F32)],
        compiler_params=pltpu.CompilerParams(dimension_semantics=("arbitrary",), vmem_limit_bytes=VMEM_LIMIT),
        name="mixer_sample",
    )(x, mod, cos, sin, ck, cv, sconv, *win, wconv, wco, wao, sinks, wmo)


def _pre_kernel(*refs, prompt_tiles, has_sample):
    if has_sample:
        xp_ref, mp_ref, xs_ref, ms_ref, wsg_ref, wsu_ref, wsd_ref, wr_ref, rb_ref, h2_ref, base_ref, cw_ref = refs
    else:
        xp_ref, mp_ref, wsg_ref, wsu_ref, wsd_ref, wr_ref, rb_ref, h2_ref, base_ref, cw_ref = refs
    nc, c, d = xp_ref.shape
    t = nc * c
    x3, mod = xp_ref[...], mp_ref[...]
    if has_sample:
        is_prompt = pl.program_id(0) < prompt_tiles
        x3 = jnp.where(is_prompt, x3, xs_ref[...])
        mod = jnp.where(is_prompt, mod, ms_ref[...])
    sh2, sc2, g2 = mod[:, :, 0:d], mod[:, :, d:2 * d], mod[:, :, 2 * d:3 * d]
    h3 = _rms(x3) * (1.0 + sc2) + sh2
    h2 = h3.reshape(t, d)
    hb = h2.astype(BF16)
    h2_ref[...] = _pack_bf16_pairs(h2)
    shared = _dot((_silu(_dot(hb, wsg_ref[...])) * _dot(hb, wsu_ref[...])).astype(BF16), wsd_ref[...])
    base_ref[...] = x3 + g2 * shared.reshape(nc, c, d)

    logits = lax.dot_general(wr_ref[...], h2, (((1,), (1,)), ((), ())), preferred_element_type=F32,
                             precision=lax.Precision.HIGHEST)
    scores = _sigmoid(logits)
    biased = scores + rb_ref[...]
    g3 = biased.reshape(N_EXPERT_GROUPS, GROUP_SIZE, t)
    member = lax.broadcasted_iota(I32, g3.shape, 1)
    m1 = jnp.max(g3, axis=1, keepdims=True)
    first = jnp.min(jnp.where(g3 == m1, member, GROUP_SIZE), axis=1, keepdims=True)
    m2 = jnp.max(jnp.where(member == first, -jnp.inf, g3), axis=1, keepdims=True)
    gs = m1 + m2
    gidx = lax.broadcasted_iota(I32, gs.shape, 0)
    grank = jnp.zeros(gs.shape, I32)
    for o in range(N_EXPERT_GROUPS):
        other = gs[o:o + 1]
        grank += ((other > gs) | ((other == gs) & (o < gidx))).astype(I32)
    eligible = jnp.broadcast_to(grank < TOPK_GROUPS, g3.shape).reshape(N_EXPERTS, t)
    mb = jnp.where(eligible, biased, -jnp.inf)
    eidx = lax.broadcasted_iota(I32, mb.shape, 0)
    erank = jnp.zeros(mb.shape, I32)
    for o in range(N_EXPERTS):
        other = mb[o:o + 1]
        erank += ((other > mb) | ((other == mb) & (o < eidx))).astype(I32)
    sel = eligible & (erank < TOP_K)
    ssum = jnp.sum(jnp.where(sel, scores, 0.0), axis=0, keepdims=True)
    cw_ref[...] = jnp.where(sel, scores / ssum * ROUTED_SCALE, -1.0)


def _pre(x1_p, p_chunk0, ncp, mod_p, x1_s, ncs, mod_s, wsg, wsu, wsd, wr_t, rb):
    ncp_all, c, d = x1_p.shape
    nc = PRE_TILE // c
    nchunks = ncp + ncs
    n = nchunks * c
    pt, p0 = ncp // nc, p_chunk0 // nc
    tiles_per_stream = ncp_all // mod_p.shape[0] // nc
    blk3 = pl.BlockSpec((nc, c, d), lambda i: (i, 0, 0))
    p_tile = lambda i: p0 + jnp.minimum(i, pt - 1)
    s_tile = lambda i: jnp.maximum(i - pt, 0)
    s_args, s_specs = [], []
    if ncs:
        s_args = [x1_s, mod_s]
        s_specs = [pl.BlockSpec((nc, c, d), lambda i: (s_tile(i), 0, 0)),
                   pl.BlockSpec((nc, 1, 3 * d), lambda i: (s_tile(i), 0, 0))]
    return pl.pallas_call(
        functools.partial(_pre_kernel, prompt_tiles=pt, has_sample=bool(ncs)),
        grid=(nchunks // nc,),
        in_specs=[pl.BlockSpec((nc, c, d), lambda i: (p_tile(i), 0, 0)),
                  pl.BlockSpec((1, 1, 3 * d), lambda i: (p_tile(i) // tiles_per_stream, 0, 0))] + s_specs + [
                  _const_spec(wsg.shape), _const_spec(wsu.shape), _const_spec(wsd.shape),
                  _const_spec(wr_t.shape), _const_spec(rb.shape)],
        out_specs=[pl.BlockSpec((nc * c, d // 2), lambda i: (i, 0)), blk3,
                   pl.BlockSpec((N_EXPERTS, nc * c), lambda i: (0, i))],
        out_shape=[jax.ShapeDtypeStruct((n, d // 2), I32),
                   jax.ShapeDtypeStruct((nchunks, c, d), F32),
                   jax.ShapeDtypeStruct((N_EXPERTS, n), F32)],
        compiler_params=pltpu.CompilerParams(dimension_semantics=("arbitrary",), vmem_limit_bytes=VMEM_LIMIT),
        name="pre_ffn",
    )(x1_p, mod_p, *s_args, wsg, wsu, wsd, wr_t, rb)


def _rank_kernel(cw_ref, rank_ref, cnt_ref, carry):
    i = pl.program_id(0)
    t = cw_ref.shape[1]

    @pl.when(i == 0)
    def _():
        carry[...] = jnp.zeros(carry.shape, F32)

    sel = (cw_ref[...] >= 0.0).astype(BF16)
    r = lax.broadcasted_iota(I32, (t, t), 0)
    c = lax.broadcasted_iota(I32, (t, t), 1)
    before = (r < c).astype(BF16)
    rank = carry[...] + _dot(sel, before)
    rank_ref[...] = rank.astype(I32)
    carry[...] = carry[...] + jnp.sum(sel.astype(F32), axis=1, keepdims=True)
    cnt_ref[...] = carry[...].astype(I32)


def _rank(cw):
    e, n = cw.shape
    t = RANK_TILE
    return pl.pallas_call(
        _rank_kernel,
        grid=(n // t,),
        in_specs=[pl.BlockSpec((e, t), lambda i: (0, i))],
        out_specs=[pl.BlockSpec((e, t), lambda i: (0, i)), pl.BlockSpec((e, 1), lambda i: (0, 0))],
        out_shape=[jax.ShapeDtypeStruct((e, n), I32), jax.ShapeDtypeStruct((e, 1), I32)],
        scratch_shapes=[pltpu.VMEM((e, 1), F32)],
        compiler_params=pltpu.CompilerParams(dimension_semantics=("arbitrary",)),
        name="expert_rank",
    )(cw)


def _slot_kernel(cw_ref, rank_ref, start_ref, pos_ref, w_ref):
    cw = cw_ref[...]
    e, t = cw.shape
    sel = cw >= 0.0
    r = lax.broadcasted_iota(I32, (e, e), 0)
    c = lax.broadcasted_iota(I32, (e, e), 1)
    lower = (c < r).astype(BF16)
    kidx = _dot(lower, sel.astype(BF16))
    posf = start_ref[...].astype(F32) + rank_ref[...].astype(F32)
    pos_rows, w_rows = [], []
    for k in range(TOP_K):
        m = sel & (kidx == float(k))
        pos_rows.append(jnp.sum(jnp.where(m, posf, 0.0), axis=0, keepdims=True))
        w_rows.append(jnp.sum(jnp.where(m, cw, 0.0), axis=0, keepdims=True))
    pos_ref[...] = jnp.concatenate(pos_rows, axis=0).astype(I32)
    w_pad = jnp.concatenate(w_rows + [jnp.zeros((LANES - TOP_K, t), F32)], axis=0)
    w_ref[...] = w_pad.T[:, :TOP_K]


def _slots(cw, rank, seg_start):
    e, n = cw.shape
    t = RANK_TILE
    return pl.pallas_call(
        _slot_kernel,
        grid=(n // t,),
        in_specs=[pl.BlockSpec((e, t), lambda i: (0, i)), pl.BlockSpec((e, t), lambda i: (0, i)),
                  pl.BlockSpec((e, 1), lambda i: (0, 0))],
        out_specs=[pl.BlockSpec((TOP_K, t), lambda i: (0, i)), pl.BlockSpec((t, TOP_K), lambda i: (i, 0))],
        out_shape=[jax.ShapeDtypeStruct((TOP_K, n), I32), jax.ShapeDtypeStruct((n, TOP_K), F32)],
        compiler_params=pltpu.CompilerParams(dimension_semantics=("arbitrary",)),
        name="expert_slots",
    )(cw, rank, seg_start)


def _sc_mesh():
    return plsc.VectorSubcoreMesh(core_axis_name="c", subcore_axis_name="s")


def _sc_worker_id():
    return lax.axis_index("s") * (SC_WORKERS // 16) + lax.axis_index("c")


def _sc_dispatch(rows, pos, n_rows):
    n, d = rows.shape
    per_w = n // SC_WORKERS
    w = SC_WINDOW
    n_chunks = per_w // w

    @functools.partial(
        pl.kernel, mesh=_sc_mesh(),
        out_type=jax.ShapeDtypeStruct((n_rows, d), rows.dtype),
        scratch_types=[pltpu.VMEM((2, TOP_K, w), I32), pltpu.VMEM((2, w, d), rows.dtype),
                       pltpu.SemaphoreType.DMA((2,)), pltpu.SemaphoreType.DMA((2,)), pltpu.SemaphoreType.DMA((2,))],
        name="sc_dispatch")
    def k(rows_hbm, pos_hbm, o_hbm, idx_v, rows_v, row_sem, idx_sem, out_sem):
        wid = _sc_worker_id()
        base = wid * per_w

        def loads(c, slot):
            off = pl.multiple_of(base + c * w, SUBLANES)
            return (pltpu.make_async_copy(rows_hbm.at[pl.ds(off, w)], rows_v.at[slot], row_sem.at[slot]),
                    pltpu.make_async_copy(pos_hbm.at[wid * n_chunks + c], idx_v.at[slot], idx_sem.at[slot]))

        def scatters(slot):
            return [pltpu.make_async_copy(rows_v.at[slot], o_hbm.at[idx_v.at[slot, kk]], out_sem.at[slot])
                    for kk in range(TOP_K)]

        for cp in loads(0, 0):
            cp.start()
        for c in range(n_chunks):
            slot = c % 2
            for cp in loads(c, slot):
                cp.wait()
            for cp in scatters(slot):
                cp.start()
            if c >= 1:
                for cp in scatters(1 - slot):
                    cp.wait()
            if c + 1 < n_chunks:
                for cp in loads(c + 1, 1 - slot):
                    cp.start()
        for cp in scatters((n_chunks - 1) % 2):
            cp.wait()

    pos_chunks = pos.reshape(TOP_K, n // w, w).transpose(1, 0, 2)
    return k(rows, pos_chunks)


def _sc_collect(rows, pos_flat):
    d = rows.shape[1]
    total = pos_flat.shape[0]
    per_w = total // SC_WORKERS
    w = SC_WINDOW
    n_pairs = per_w // (2 * w)

    @functools.partial(
        pl.kernel, mesh=_sc_mesh(),
        out_type=jax.ShapeDtypeStruct((total, d), rows.dtype),
        scratch_types=[pltpu.VMEM((per_w,), I32), pltpu.VMEM((2, w, d), rows.dtype),
                       pltpu.SemaphoreType.DMA((2,)), pltpu.SemaphoreType.DMA((2,))],
        name="sc_collect")
    def k(rows_hbm, pos_hbm, o_hbm, idx_v, rows_v, in_sem, out_sem):
        base = pl.multiple_of(_sc_worker_id() * per_w, SUBLANES)
        pltpu.sync_copy(pos_hbm.at[pl.ds(base, per_w)], idx_v)

        def gather(c, slot):
            idx = idx_v.at[pl.ds(pl.multiple_of(c * w, SUBLANES), w)]
            return pltpu.make_async_copy(rows_hbm.at[idx], rows_v.at[slot], in_sem.at[slot])

        def write(c, slot):
            off = pl.multiple_of(base + c * w, SUBLANES)
            return pltpu.make_async_copy(rows_v.at[slot], o_hbm.at[pl.ds(off, w)], out_sem.at[slot])

        gather(0, 0).start()

        @pl.loop(0, n_pairs)
        def _(p):
            c0 = 2 * p
            gather(c0 + 1, 1).start()
            gather(c0, 0).wait()
            write(c0, 0).start()
            gather(c0 + 1, 1).wait()
            write(c0 + 1, 1).start()
            write(c0, 0).wait()

            @pl.when(p + 1 < n_pairs)
            def _():
                gather(c0 + 2, 0).start()

            write(c0 + 1, 1).wait()

    return k(rows, pos_flat)


def _gmm_kernel(be_ref, br_ref, nu_ref, ord_ref, nxt_ref, x_hbm, wg_hbm, wu_hbm, wd_hbm, y_ref,
                wgb, wub, wdb, xbuf, xsem, wgf, wuf, wdf, wsem):
    b = pl.program_id(0)
    n_used = nu_ref[0]
    bm = xbuf.shape[1]
    e = be_ref[b]

    def w_copies(ex, slot):
        return [pltpu.make_async_copy(src.at[ex], dst.at[slot], wsem.at[slot, i])
                for i, (src, dst) in enumerate(((wg_hbm, wgf), (wu_hbm, wuf), (wd_hbm, wdf)))]

    def x_copy(blk):
        slot = blk % GMM_X_SLOTS
        return pltpu.make_async_copy(x_hbm.at[pl.ds(pl.multiple_of(blk * bm, bm), bm)], xbuf.at[slot], xsem.at[slot])

    @pl.when(b == 0)
    def _():
        for ahead in range(GMM_X_SLOTS - 1):
            @pl.when(ahead < n_used)
            def _():
                x_copy(ahead).start()

    @pl.when(b + GMM_X_SLOTS - 1 < n_used)
    def _():
        x_copy(b + GMM_X_SLOTS - 1).start()

    prev = be_ref[jnp.maximum(b - 1, 0)]

    @pl.when((b < n_used) & ((b == 0) | (e != prev)))
    def _():
        slot = ord_ref[e] % 2

        @pl.when(b == 0)
        def _():
            for cp in w_copies(e, slot):
                cp.start()

        for cp in w_copies(e, slot):
            cp.wait()
        wgb[...] = wgf[slot].astype(BF16)
        wub[...] = wuf[slot].astype(BF16)
        wdb[...] = wdf[slot].astype(BF16)
        nxt = nxt_ref[e]

        @pl.when(nxt >= 0)
        def _():
            for cp in w_copies(nxt, 1 - slot):
                cp.start()

    @pl.when(b < n_used)
    def _():
        x_copy(b).wait()

    x_ref = xbuf.at[b % GMM_X_SLOTS]

    def expert_rows(r0, n):
        rows = pl.ds(r0, n)
        lo, hi = _unpack_bf16_pairs(x_ref[rows, :])
        xb = jnp.concatenate([lo.astype(BF16), hi.astype(BF16)], axis=1)
        mid = (_silu(_dot(xb, wgb[...])) * _dot(xb, wub[...])).astype(BF16)
        y_ref[rows, :] = _pack_bf16_pairs(_dot(mid, wdb[...]))

    n_real = br_ref[b]
    n_main = n_real // GMM_SUB

    @pl.loop(0, n_main)
    def _(i):
        expert_rows(pl.multiple_of(i * GMM_SUB, GMM_SUB), GMM_SUB)

    @pl.loop(0, (n_real - n_main * GMM_SUB + GMM_TAIL - 1) // GMM_TAIL)
    def _(i):
        expert_rows(pl.multiple_of(n_main * GMM_SUB + i * GMM_TAIL, GMM_TAIL), GMM_TAIL)


def _gmm(x_sorted, block_e, block_rows, n_used, e_ord, e_next, wg, wu, wd):
    r, half = x_sorted.shape
    d = 2 * half
    bm = GMM_BM
    any_spec = pl.BlockSpec(memory_space=pl.ANY)
    return pl.pallas_call(
        _gmm_kernel,
        grid_spec=pltpu.PrefetchScalarGridSpec(
            num_scalar_prefetch=5,
            grid=(r // bm,),
            in_specs=[any_spec, any_spec, any_spec, any_spec],
            out_specs=pl.BlockSpec((bm, half), lambda b, be, br, nu, eo, en: (jnp.minimum(b, nu[0] - 1), 0)),
            scratch_shapes=[pltpu.VMEM((d, D_EXPERT), BF16), pltpu.VMEM((d, D_EXPERT), BF16),
                            pltpu.VMEM((D_EXPERT, d), BF16),
                            pltpu.VMEM((GMM_X_SLOTS, bm, half), I32), pltpu.SemaphoreType.DMA((GMM_X_SLOTS,)),
                            pltpu.VMEM((2, d, D_EXPERT), F32), pltpu.VMEM((2, d, D_EXPERT), F32),
                            pltpu.VMEM((2, D_EXPERT, d), F32), pltpu.SemaphoreType.DMA((2, 3))]),
        out_shape=jax.ShapeDtypeStruct((r, half), I32),
        compiler_params=pltpu.CompilerParams(dimension_semantics=("arbitrary",), vmem_limit_bytes=VMEM_LIMIT),
        name="expert_gmm",
    )(block_e, block_rows, n_used, e_ord, e_next, x_sorted, wg, wu, wd)


def _combine_kernel(base_ref, mod_ref, g_ref, w_ref, gain_ref, *rest):
    y_ref = rest[-1]
    nc, c, d = base_ref.shape
    w = w_ref[...]
    acc_lo = acc_hi = None
    for k in range(TOP_K):
        lo, hi = _unpack_bf16_pairs(g_ref[k])
        wk = w[:, k:k + 1]
        acc_lo = wk * lo if k == 0 else acc_lo + wk * lo
        acc_hi = wk * hi if k == 0 else acc_hi + wk * hi
    acc = jnp.concatenate([acc_lo, acc_hi], axis=1)
    g2 = mod_ref[...][:, :, 2 * d:3 * d]
    out = base_ref[...] + g2 * acc.reshape(nc, c, d)
    y_ref[...] = _rms(out) * gain_ref[...]


def _combine(base, mod, gathered, w_tok, gain, first_chunk, n_chunks, out_chunks, out_first_chunk, out_buf=None):
    _, c, d = base.shape
    nc = COMB_TILE // c
    t = nc * c
    t0, o0 = first_chunk // nc, out_first_chunk // nc
    chunks_per_stream = out_chunks // mod.shape[0]
    if chunks_per_stream == 1:
        mod_spec = pl.BlockSpec((nc, 1, 3 * d), lambda i: (o0 + i, 0, 0))
    else:
        assert chunks_per_stream % nc == 0
        mod_spec = pl.BlockSpec((1, 1, 3 * d), lambda i: ((o0 + i) * nc // chunks_per_stream, 0, 0))
    blk3 = pl.BlockSpec((nc, c, d), lambda i: (t0 + i, 0, 0))
    in_specs = [blk3, mod_spec,
                pl.BlockSpec((TOP_K, t, d // 2), lambda i: (0, t0 + i, 0)),
                pl.BlockSpec((t, TOP_K), lambda i: (t0 + i, 0)),
                pl.BlockSpec((1, 1, d), lambda i: (0, 0, 0))]
    args = [base, mod, gathered, w_tok, gain.reshape(1, 1, d)]
    aliases = {}
    if out_buf is not None:
        in_specs.append(pl.BlockSpec(memory_space=pl.ANY))
        args.append(out_buf)
        aliases = {len(args) - 1: 0}
    return pl.pallas_call(
        _combine_kernel,
        grid=(n_chunks // nc,),
        in_specs=in_specs,
        out_specs=pl.BlockSpec((nc, c, d), lambda i: (o0 + i, 0, 0)),
        out_shape=jax.ShapeDtypeStruct((out_chunks, c, d), F32),
        input_output_aliases=aliases,
        compiler_params=pltpu.CompilerParams(dimension_semantics=("arbitrary",), vmem_limit_bytes=VMEM_LIMIT),
        name="combine_norm",
    )(*args)


def _rope_tables(pos):
    half = HEAD_DIM // 2
    inv_freq = ROPE_THETA ** (-jnp.arange(half, dtype=F32) / half)
    ang = pos.astype(F32)[:, None] * inv_freq[None, :]
    cos, sin = jnp.cos(ang), jnp.sin(ang)
    reps = LANES // HEAD_DIM
    return jnp.tile(jnp.concatenate([cos, cos], axis=1), (1, reps)), jnp.tile(jnp.concatenate([-sin, sin], axis=1), (1, reps))


def _routed_ffn(h2, cw, w_gate, w_up, w_down):
    n, half = h2.shape
    rank, counts = _rank(cw)
    bm = GMM_BM
    padded = (counts[:, 0] + bm - 1) // bm * bm
    seg_end = jnp.cumsum(padded)
    seg_start = (seg_end - padded).astype(I32)
    n_rows = n * TOP_K + N_EXPERTS * bm
    n_blocks = n_rows // bm
    block_start = jnp.arange(n_blocks, dtype=I32) * bm
    block_e = jnp.minimum(jnp.sum((seg_end[None, :] <= block_start[:, None]).astype(I32), axis=1), N_EXPERTS - 1)
    own = block_e[:, None] == jnp.arange(N_EXPERTS, dtype=I32)[None, :]
    real_end = jnp.sum(jnp.where(own, (seg_start + counts[:, 0])[None, :], 0), axis=1)
    block_rows = jnp.clip(real_end - block_start, 0, bm).astype(I32)
    n_used = (seg_end[-1:] // bm).astype(I32)
    has_rows = counts[:, 0] > 0
    eids = jnp.arange(N_EXPERTS, dtype=I32)
    e_ord = (jnp.cumsum(has_rows.astype(I32)) - has_rows.astype(I32)).astype(I32)
    later = has_rows[None, :] & (eids[None, :] > eids[:, None])
    e_next = jnp.min(jnp.where(later, eids[None, :], N_EXPERTS), axis=1)
    e_next = jnp.where(e_next == N_EXPERTS, -1, e_next).astype(I32)
    pos, w_k = _slots(cw, rank, seg_start[:, None])
    x_sorted = _sc_dispatch(h2, pos, n_rows)
    y_sorted = _gmm(x_sorted, block_e, block_rows, n_used, e_ord, e_next, w_gate, w_up, w_down)
    gathered = _sc_collect(y_sorted, pos.reshape(TOP_K * n)).reshape(TOP_K, n, half)
    return gathered, w_k


def kernel(x_prompt, x_sample, cache_k, cache_v, state_conv, c_prompt, c_sample, w_ada, b_ada, w_in, w_conv,
           w_conv_out, w_attn_o, attn_sinks, w_mix_out, w_router, router_bias, w_exp_gate, w_exp_up, w_exp_down,
           w_sh_gate, w_sh_up, w_sh_down, final_gain):
    assert w_ada.shape[0] == 1, "one layer"
    bp, seq, d = x_prompt.shape
    bs, ts, _ = x_sample.shape
    assert ts == CHUNK and seq % MIX_TILE == 0 and bs % SAMPLE_BB == 0

    c_all = jnp.concatenate([c_prompt, c_sample], axis=0)
    pad = (-c_all.shape[0]) % SUBLANES
    mod = _ada(jnp.pad(c_all, ((0, pad), (0, 0))), w_ada[0], b_ada[0])[:bp + bs]
    mod_p, mod_s = mod[:bp, None, :], mod[bp:, None, :]

    head_axes = (N_KV_HEADS // 2, 2, GQA_GROUP, HEAD_DIM)
    w_in_l = w_in[0]
    w_q = w_in_l[:, OFF_Q:OFF_K].reshape((d,) + head_axes).transpose(0, 1, 3, 2, 4).reshape(d, Q_DIM)
    w_o = w_attn_o[0].reshape(head_axes + (d,)).transpose(0, 2, 1, 3, 4).reshape(Q_DIM, d)
    win = tuple(w.astype(BF16) for w in (w_in_l[:, :OFF_Q], w_q, w_in_l[:, OFF_K:]))
    wco, wao, wmo = (w.astype(BF16) for w in (w_conv_out[0], w_o, w_mix_out[0]))
    cos_p, sin_p = _rope_tables(jnp.arange(seq, dtype=I32))
    cos_s, sin_s = _rope_tables(PAST_LEN + jnp.arange(ts, dtype=I32))

    x1_p, conv_p, k_p, v_p = _mixer_prompt(x_prompt, mod_p, cos_p, sin_p, win, w_conv[0], wco, wao, attn_sinks[0], wmo)
    x1_s, conv_s, k_s, v_s = _mixer_sample(
        x_sample, mod_s, cos_s, sin_s, cache_k[0].reshape(bs, WINDOW, KV_DIM), cache_v[0].reshape(bs, WINDOW, KV_DIM),
        state_conv[0], win, w_conv[0], wco, wao, attn_sinks[0], wmo)

    n_p, n_s = bp * seq, bs * ts
    n = n_p + n_s
    mod2_p, mod2_s = mod[:bp, None, 3 * d:], mod[bp:, None, 3 * d:]
    x1_pc = x1_p.reshape(n_p // CHUNK, CHUNK, d)
    wsg, wsu, wsd = (w[0].astype(BF16) for w in (w_sh_gate, w_sh_up, w_sh_down))
    wr_t, rb = w_router[0].T, router_bias[0][:, None]

    ncp, ncs = n_p // CHUNK, n_s // CHUNK
    half = (ncp + ncs) // 2
    tile_chunks = max(PRE_TILE, COMB_TILE) // CHUNK
    assert half <= ncp and half % tile_chunks == 0 and (ncp - half) % tile_chunks == 0 and ncs % tile_chunks == 0
    assert (half * CHUNK) % (SC_WORKERS * SC_WINDOW) == 0 and (half * CHUNK) % RANK_TILE == 0
    y_p = None
    for p0, np_c, ns_c in ((0, half, 0), (half, ncp - half, ncs)):
        h2, base, cw = _pre(x1_pc, p0, np_c, mod2_p, x1_s, ns_c, mod2_s, wsg, wsu, wsd, wr_t, rb)
        gathered, w_tok = _routed_ffn(h2, cw, w_exp_gate[0], w_exp_up[0], w_exp_down[0])
        y_p = _combine(base, mod2_p, gathered, w_tok, final_gain, 0, np_c, ncp, p0, out_buf=y_p)
        if ns_c:
            y_s = _combine(base, mod2_s, gathered, w_tok, final_gain, np_c, ns_c, ncs, 0)

    kv = lambda a: a.reshape(1, a.shape[0], WINDOW, N_KV_HEADS, HEAD_DIM)
    return (y_p.reshape(bp, seq, d), y_s, conv_p[None], kv(k_p), kv(v_p), conv_s[None], kv(k_s), kv(v_s))
```

```python
import functools

import jax
import jax.numpy as jnp
from jax import lax
from jax.experimental import pallas as pl
from jax.experimental.pallas import tpu as pltpu
from jax.experimental.pallas import tpu_sc as plsc

F32 = jnp.float32
BF16 = jnp.bfloat16
I32 = jnp.int32

D_MODEL = 1024
CHUNK = 64
D_CONV = 1024
CONV_W = 3
N_HEADS = 16
N_KV_HEADS = 4
HEAD_DIM = 64
GQA_GROUP = N_HEADS // N_KV_HEADS
WINDOW = 128
ROPE_THETA = 10000.0
ATTN_SCALE = HEAD_DIM ** -0.5
N_EXPERTS = 64
TOP_K = 8
N_EXPERT_GROUPS = 8
GROUP_SIZE = N_EXPERTS // N_EXPERT_GROUPS
TOPK_GROUPS = 4
D_EXPERT = 256
D_SHARED = 256
ROUTED_SCALE = 2.5
EPS = 1e-6
PAST_LEN = 4096
Q_DIM = N_HEADS * HEAD_DIM
KV_DIM = N_KV_HEADS * HEAD_DIM
OFF_GB, OFF_GC, OFF_XC, OFF_Q, OFF_K, OFF_V, OFF_GCONV, OFF_GATTN, D_IN = (
    0, 1024, 2048, 3072, 4096, 4352, 4608, 5632, 6656)

LANES = 128
SUBLANES = 8
VMEM_LIMIT = 56 * 1024 * 1024

MIX_TILE = 512
ATT_Q = 128
MIX_SIDE_COLS = 256
SAMPLE_BB = 8
PRE_TILE = 512
RANK_TILE = 512
GMM_BM = 1280
GMM_X_SLOTS = 3
GMM_SUB = 512
GMM_TAIL = 128
COMB_TILE = 256
SC_WORKERS = 32
SC_WINDOW = 96


def _const_spec(shape):
    nd = len(shape)
    return pl.BlockSpec(shape, lambda *_: (0,) * nd, pipeline_mode=pl.Buffered(1))


def _rms(x):
    return x * lax.rsqrt(jnp.mean(x * x, axis=-1, keepdims=True) + EPS)


def _sigmoid(x):
    return 1.0 / (1.0 + jnp.exp(-x))


def _silu(x):
    return x * _sigmoid(x)


def _dot(a, b):
    return jnp.dot(a, b, preferred_element_type=F32)


def _pack_bf16_pairs(x):
    half = x.shape[-1] // 2
    lo = lax.bitcast_convert_type(x[..., :half].astype(BF16).astype(F32), I32)
    hi = lax.bitcast_convert_type(x[..., half:].astype(BF16).astype(F32), I32)
    return lax.shift_right_logical(lo, 16) | hi


def _unpack_bf16_pairs(words):
    lo = lax.bitcast_convert_type(lax.shift_left(words, 16), F32)
    hi = lax.bitcast_convert_type(words & jnp.int32(-65536), F32)
    return lo, hi


def _ada_kernel(c_ref, w_ref, b_ref, o_ref):
    s = _silu(c_ref[...]).astype(BF16)
    o_ref[...] = _dot(s, w_ref[...].astype(BF16)) + b_ref[...]


def _ada(c_all, w_ada, b_ada):
    rows = c_all.shape[0]
    n_out = w_ada.shape[1]
    bn = 768
    return pl.pallas_call(
        _ada_kernel,
        grid=(n_out // bn,),
        in_specs=[pl.BlockSpec((rows, D_MODEL), lambda i: (0, 0)),
                  pl.BlockSpec((D_MODEL, bn), lambda i: (0, i)),
                  pl.BlockSpec((1, bn), lambda i: (0, i))],
        out_specs=pl.BlockSpec((rows, bn), lambda i: (0, i)),
        out_shape=jax.ShapeDtypeStruct((rows, n_out), F32),
        name="ada_mod",
    )(c_all, w_ada, b_ada.reshape(1, n_out))


def _rope(x, cos, sin_signed):
    lane = lax.broadcasted_iota(I32, (x.shape[0], LANES), 1)
    first_half = (lane % HEAD_DIM) < (HEAD_DIM // 2)
    outs = []
    for g in range(x.shape[1] // LANES):
        xg = x[:, g * LANES:(g + 1) * LANES]
        up = pltpu.roll(xg, LANES - HEAD_DIM // 2, axis=1)
        down = pltpu.roll(xg, HEAD_DIM // 2, axis=1)
        partner = jnp.where(first_half, up, down)
        outs.append(xg * cos + partner * sin_signed)
    return jnp.concatenate(outs, axis=1)


def _attention(blocks, sinks_ref, obuf, between):
    rq = GQA_GROUP * ATT_Q
    low = lax.broadcasted_iota(I32, (ATT_Q, LANES), 1) < HEAD_DIM
    head_of_lane = lax.broadcasted_iota(I32, (1, rq), 1) // ATT_Q
    units = [(b, pair, par) for b in range(len(blocks)) for pair in range(N_KV_HEADS // 2) for par in range(2)]
    loaded = {}

    def scores(u):
        b, pair, par = units[u]
        if b not in loaded:
            loaded.clear()
            loaded[b] = blocks[b]()
        q_blk, k_of_pair, _, mask, _ = loaded[b]
        keep = low if par == 0 else jnp.logical_not(low)
        cols = [q_blk[:, (GQA_GROUP * pair + i) * LANES:(GQA_GROUP * pair + i + 1) * LANES] for i in range(GQA_GROUP)]
        qg = jnp.concatenate([jnp.where(keep, c, jnp.zeros_like(c)) for c in cols], axis=0)
        st = lax.dot_general(k_of_pair(pair), qg, (((1,), (1,)), ((), ())), preferred_element_type=F32)
        vt = loaded[b][2](pair)[par * HEAD_DIM:(par + 1) * HEAD_DIM, :]
        return jnp.where(mask, st, -jnp.inf), vt, loaded[b][4]

    outs = []
    nxt = scores(0)
    for u, (b, pair, par) in enumerate(units):
        st, vt, row0 = nxt
        if u + 1 < len(units):
            nxt = scores(u + 1)
        if between:
            between.pop(0)()
        g = 2 * pair + par
        sink = jnp.full((1, rq), sinks_ref[g * GQA_GROUP + GQA_GROUP - 1], F32)
        for i in range(GQA_GROUP - 2, -1, -1):
            sink = jnp.where(head_of_lane == i, sinks_ref[g * GQA_GROUP + i], sink)
        m = jnp.maximum(jnp.max(st, axis=0, keepdims=True), sink)
        e = jnp.exp(st - m)
        z = jnp.sum(e, axis=0, keepdims=True) + jnp.exp(sink - m)
        outs.append(_dot(vt, e.astype(BF16)) / z)
        if par == 1:
            for i in range(GQA_GROUP):
                blk = jnp.concatenate([o[:, i * ATT_Q:(i + 1) * ATT_Q] for o in outs], axis=0)
                c0 = (GQA_GROUP * pair + i) * LANES
                obuf[row0:row0 + ATT_Q, c0:c0 + LANES] = blk.T
            outs = []
    for step in between:
        step()


def _in_proj(hb, win_refs, lo, hi):
    w_pre, w_q, w_post = win_refs
    if hi <= OFF_Q:
        return _dot(hb, w_pre[:, lo:hi])
    if lo >= OFF_K:
        return _dot(hb, w_post[:, lo - OFF_K:hi - OFF_K])
    assert (lo, hi) == (OFF_Q, OFF_K)
    return _dot(hb, w_q[...])


def _attention_free_steps(hb, conv, win_ref, wco_ref):
    out, parts = {}, {}
    n_parts = D_MODEL // MIX_SIDE_COLS

    def step(name, piece, compute):
        def run():
            parts.setdefault(name, []).append(compute(piece * MIX_SIDE_COLS, (piece + 1) * MIX_SIDE_COLS))
            if piece == n_parts - 1:
                out[name] = jnp.concatenate(parts.pop(name), axis=1)
        return run

    def conv_in():
        if "conv_in" not in out:
            out["conv_in"] = (out.pop("gate_b") * conv).astype(BF16)
        return out["conv_in"]

    computes = [("gate_b", lambda lo, hi: _in_proj(hb, win_ref, OFF_GB + lo, OFF_GB + hi)),
                ("g_conv", lambda lo, hi: _in_proj(hb, win_ref, OFF_GCONV + lo, OFF_GCONV + hi)),
                ("g_attn", lambda lo, hi: _in_proj(hb, win_ref, OFF_GATTN + lo, OFF_GATTN + hi)),
                ("y_conv", lambda lo, hi: _dot(conv_in(), wco_ref[:, lo:hi]))]
    return [step(name, p, fn) for name, fn in computes for p in range(n_parts)], out


def _mix_out(x, g1, side, y_attn_in, wao_ref, wmo_ref):
    y_attn = _dot(y_attn_in.astype(BF16), wao_ref[...])
    merged = _sigmoid(side["g_conv"]) * side["y_conv"] + _sigmoid(side["g_attn"]) * y_attn
    return x + g1 * _dot(merged.astype(BF16), wmo_ref[...])


def _mixer_prompt_kernel(x_ref, mod_ref, cos_ref, sin_ref, wpre_ref, wq_ref, wpost_ref, wconv_ref, wco_ref, wao_ref,
                         sinks_ref, wmo_ref, x1_ref, conv_ref, k_ref, v_ref, ubuf, kbuf, vtbuf, obuf):
    win_ref = (wpre_ref, wq_ref, wpost_ref)
    j = pl.program_id(1)
    t = x_ref.shape[1]

    @pl.when(j == 0)
    def _():
        ubuf[0:SUBLANES, :] = jnp.zeros((SUBLANES, D_CONV), F32)
        kbuf[0:WINDOW, :] = jnp.zeros((WINDOW, KV_DIM), BF16)
        vtbuf[:, 0:WINDOW] = jnp.zeros((KV_DIM, WINDOW), BF16)

    x = x_ref[0]
    mod = mod_ref[0]
    sh1, sc1, g1 = mod[:, 0:D_MODEL], mod[:, D_MODEL:2 * D_MODEL], mod[:, 2 * D_MODEL:3 * D_MODEL]
    hb = (_rms(x) * (1.0 + sc1) + sh1).astype(BF16)

    u = _in_proj(hb, win_ref, OFF_GC, OFF_XC) * _in_proj(hb, win_ref, OFF_XC, OFF_Q)
    ubuf[SUBLANES:SUBLANES + t, :] = u
    wc = wconv_ref[...]
    conv = wc[0:1] * ubuf[SUBLANES - 2:SUBLANES - 2 + t, :] + wc[1:2] * ubuf[SUBLANES - 1:SUBLANES - 1 + t, :] + wc[2:3] * u
    conv_ref[0] = u[t - (CONV_W - 1):t]
    ubuf[SUBLANES - 2:SUBLANES, :] = u[t - (CONV_W - 1):t]

    cos, sin = cos_ref[...], sin_ref[...]
    q = (_rope(_in_proj(hb, win_ref, OFF_Q, OFF_K), cos, sin) * ATTN_SCALE).astype(BF16)
    k = _rope(_in_proj(hb, win_ref, OFF_K, OFF_V), cos, sin)
    v = _in_proj(hb, win_ref, OFF_V, OFF_GCONV)
    kbuf[WINDOW:WINDOW + t, :] = k.astype(BF16)
    vtbuf[:, WINDOW:WINDOW + t] = v.T.astype(BF16)
    k_ref[0] = k[t - WINDOW:t]
    v_ref[0] = v[t - WINDOW:t]

    nkeys = ATT_Q + WINDOW
    rq = GQA_GROUP * ATT_Q
    ki = lax.broadcasted_iota(I32, (nkeys, rq), 0)
    qi = lax.broadcasted_iota(I32, (nkeys, rq), 1) % ATT_Q
    band = ki // CHUNK - qi // CHUNK
    band_ok = (band >= 0) & (band <= WINDOW // CHUNK)
    def block(s):
        def load():
            mask = band_ok & (ki + (j * t + s * ATT_Q - WINDOW) >= 0)
            k_of_pair = lambda pair: kbuf[s * ATT_Q:s * ATT_Q + nkeys, pair * LANES:(pair + 1) * LANES]
            vt_of_pair = lambda pair: vtbuf[pair * LANES:(pair + 1) * LANES, s * ATT_Q:s * ATT_Q + nkeys]
            return q[s * ATT_Q:(s + 1) * ATT_Q], k_of_pair, vt_of_pair, mask, s * ATT_Q
        return load

    steps, side = _attention_free_steps(hb, conv, win_ref, wco_ref)
    _attention([block(s) for s in range(t // ATT_Q)], sinks_ref, obuf, steps)
    kbuf[0:WINDOW, :] = kbuf[t:t + WINDOW, :]
    vtbuf[:, 0:WINDOW] = vtbuf[:, t:t + WINDOW]

    x1_ref[0] = _mix_out(x, g1, side, obuf[...], wao_ref, wmo_ref)


def _mixer_prompt(x, mod, cos, sin, win, wconv, wco, wao, sinks, wmo):
    b, seq, d = x.shape
    t = MIX_TILE
    return pl.pallas_call(
        _mixer_prompt_kernel,
        grid=(b, seq // t),
        in_specs=[pl.BlockSpec((1, t, d), lambda i, j: (i, j, 0)),
                  pl.BlockSpec((1, 1, 6 * d), lambda i, j: (i, 0, 0)),
                  pl.BlockSpec((t, LANES), lambda i, j: (j, 0)),
                  pl.BlockSpec((t, LANES), lambda i, j: (j, 0)),
                  *[_const_spec(w.shape) for w in win],
                  _const_spec(wconv.shape), _const_spec(wco.shape), _const_spec(wao.shape),
                  pl.BlockSpec(memory_space=pltpu.SMEM),
                  _const_spec(wmo.shape)],
        out_specs=[pl.BlockSpec((1, t, d), lambda i, j: (i, j, 0)),
                   pl.BlockSpec((1, CONV_W - 1, D_CONV), lambda i, j: (i, 0, 0)),
                   pl.BlockSpec((1, WINDOW, KV_DIM), lambda i, j: (i, 0, 0)),
                   pl.BlockSpec((1, WINDOW, KV_DIM), lambda i, j: (i, 0, 0))],
        out_shape=[jax.ShapeDtypeStruct((b, seq, d), F32),
                   jax.ShapeDtypeStruct((b, CONV_W - 1, D_CONV), F32),
                   jax.ShapeDtypeStruct((b, WINDOW, KV_DIM), F32),
                   jax.ShapeDtypeStruct((b, WINDOW, KV_DIM), F32)],
        scratch_shapes=[pltpu.VMEM((SUBLANES + t, D_CONV), F32),
                        pltpu.VMEM((WINDOW + t, KV_DIM), BF16),
                        pltpu.VMEM((KV_DIM, WINDOW + t), BF16),
                        pltpu.VMEM((t, Q_DIM), F32)],
        compiler_params=pltpu.CompilerParams(dimension_semantics=("arbitrary", "arbitrary"),
                                             vmem_limit_bytes=VMEM_LIMIT),
        name="mixer_prompt",
    )(x, mod, cos, sin, *win, wconv, wco, wao, sinks, wmo)


def _mixer_sample_kernel(x_ref, mod_ref, cos_ref, sin_ref, ck_ref, cv_ref, sconv_ref, wpre_ref, wq_ref, wpost_ref,
                         wconv_ref, wco_ref, wao_ref, sinks_ref, wmo_ref, x1_ref, conv_ref, k_ref, v_ref, ubuf, obuf):
    win_ref = (wpre_ref, wq_ref, wpost_ref)
    bb, t, d = x_ref.shape
    x3 = x_ref[...]
    mod = mod_ref[...]
    sh1, sc1, g1 = mod[:, :, 0:d], mod[:, :, d:2 * d], mod[:, :, 2 * d:3 * d]
    x = x3.reshape(bb * t, d)
    hb = (_rms(x3) * (1.0 + sc1) + sh1).astype(BF16).reshape(bb * t, d)

    u = _in_proj(hb, win_ref, OFF_GC, OFF_XC) * _in_proj(hb, win_ref, OFF_XC, OFF_Q)
    u3 = u.reshape(bb, t, D_CONV)
    ubuf[:, SUBLANES - 2:SUBLANES, :] = sconv_ref[...]
    ubuf[:, SUBLANES:SUBLANES + t, :] = u3
    wc = wconv_ref[...]
    conv = (wc[0:1] * ubuf[:, SUBLANES - 2:SUBLANES - 2 + t, :] + wc[1:2] * ubuf[:, SUBLANES - 1:SUBLANES - 1 + t, :]
            + wc[2:3] * u3).reshape(bb * t, D_CONV)
    conv_ref[...] = u3[:, t - (CONV_W - 1):t, :]

    cos = jnp.concatenate([cos_ref[...]] * bb, axis=0)
    sin = jnp.concatenate([sin_ref[...]] * bb, axis=0)
    q = (_rope(_in_proj(hb, win_ref, OFF_Q, OFF_K), cos, sin) * ATTN_SCALE).astype(BF16)
    k = _rope(_in_proj(hb, win_ref, OFF_K, OFF_V), cos, sin)
    v = _in_proj(hb, win_ref, OFF_V, OFF_GCONV)
    per = ATT_Q // t
    nkeys = per * (WINDOW + t)
    rq = GQA_GROUP * ATT_Q
    key_stream = lax.broadcasted_iota(I32, (nkeys, rq), 0) // (WINDOW + t)
    query_stream = (lax.broadcasted_iota(I32, (nkeys, rq), 1) % ATT_Q) // t
    mask = key_stream == query_stream
    def block(blk):
        def load():
            k_parts, v_parts = [], []
            for b in range(blk * per, (blk + 1) * per):
                kb, vb = k[b * t:(b + 1) * t], v[b * t:(b + 1) * t]
                ck, cv = ck_ref[b], cv_ref[b]
                k_ref[b] = jnp.concatenate([ck[t:WINDOW], kb], axis=0)
                v_ref[b] = jnp.concatenate([cv[t:WINDOW], vb], axis=0)
                k_parts += [ck, kb]
                v_parts += [cv, vb]
            k_all = jnp.concatenate(k_parts, axis=0).astype(BF16)
            vt_all = jnp.concatenate(v_parts, axis=0).T.astype(BF16)
            k_of_pair = lambda pair: k_all[:, pair * LANES:(pair + 1) * LANES]
            vt_of_pair = lambda pair: vt_all[pair * LANES:(pair + 1) * LANES, :]
            return q[blk * ATT_Q:(blk + 1) * ATT_Q], k_of_pair, vt_of_pair, mask, blk * ATT_Q
        return load

    steps, side = _attention_free_steps(hb, conv, win_ref, wco_ref)
    _attention([block(blk) for blk in range(bb // per)], sinks_ref, obuf, steps)

    g1f = jnp.broadcast_to(g1, (bb, t, d)).reshape(bb * t, d)
    x1_ref[...] = _mix_out(x, g1f, side, obuf[...], wao_ref, wmo_ref).reshape(bb, t, d)


def _mixer_sample(x, mod, cos, sin, ck, cv, sconv, win, wconv, wco, wao, sinks, wmo):
    b, t, d = x.shape
    bb = SAMPLE_BB
    blk = lambda *s: pl.BlockSpec((bb,) + s, lambda i: (i, 0, 0))
    return pl.pallas_call(
        _mixer_sample_kernel,
        grid=(b // bb,),
        in_specs=[blk(t, d), blk(1, 6 * d),
                  pl.BlockSpec((t, LANES), lambda i: (0, 0)), pl.BlockSpec((t, LANES), lambda i: (0, 0)),
                  blk(WINDOW, KV_DIM), blk(WINDOW, KV_DIM), blk(CONV_W - 1, D_CONV),
                  *[_const_spec(w.shape) for w in win],
                  _const_spec(wconv.shape), _const_spec(wco.shape), _const_spec(wao.shape),
                  pl.BlockSpec(memory_space=pltpu.SMEM),
                  _const_spec(wmo.shape)],
        out_specs=[blk(t, d), blk(CONV_W - 1, D_CONV), blk(WINDOW, KV_DIM), blk(WINDOW, KV_DIM)],
        out_shape=[jax.ShapeDtypeStruct((b, t, d), F32),
                   jax.ShapeDtypeStruct((b, CONV_W - 1, D_CONV), F32),
                   jax.ShapeDtypeStruct((b, WINDOW, KV_DIM), F32),
                   jax.ShapeDtypeStruct((b, WINDOW, KV_DIM), F32)],
        scratch_shapes=[pltpu.VMEM((bb, SUBLANES + t, D_CONV), F32),
                        pltpu.VMEM((bb * t, Q_DIM), F32)],
        compiler_params=pltpu.CompilerParams(dimension_semantics=("arbitrary",), vmem_limit_bytes=VMEM_LIMIT),
        name="mixer_sample",
    )(x, mod, cos, sin, ck, cv, sconv, *win, wconv, wco, wao, sinks, wmo)


def _pre_kernel(*refs, prompt_tiles, has_sample):
    if has_sample:
        xp_ref, mp_ref, xs_ref, ms_ref, wsg_ref, wsu_ref, wsd_ref, wr_ref, rb_ref, h2_ref, base_ref, cw_ref = refs
    else:
        xp_ref, mp_ref, wsg_ref, wsu_ref, wsd_ref, wr_ref, rb_ref, h2_ref, base_ref, cw_ref = refs
    nc, c, d = xp_ref.shape
    t = nc * c
    x3, mod = xp_ref[...], mp_ref[...]
    if has_sample:
        is_prompt = pl.program_id(0) < prompt_tiles
        x3 = jnp.where(is_prompt, x3, xs_ref[...])
        mod = jnp.where(is_prompt, mod, ms_ref[...])
    sh2, sc2, g2 = mod[:, :, 0:d], mod[:, :, d:2 * d], mod[:, :, 2 * d:3 * d]
    h3 = _rms(x3) * (1.0 + sc2) + sh2
    h2 = h3.reshape(t, d)
    hb = h2.astype(BF16)
    h2_ref[...] = _pack_bf16_pairs(h2)
    shared = _dot((_silu(_dot(hb, wsg_ref[...])) * _dot(hb, wsu_ref[...])).astype(BF16), wsd_ref[...])
    base_ref[...] = x3 + g2 * shared.reshape(nc, c, d)

    logits = lax.dot_general(wr_ref[...], h2, (((1,), (1,)), ((), ())), preferred_element_type=F32,
                             precision=lax.Precision.HIGHEST)
    scores = _sigmoid(logits)
    biased = scores + rb_ref[...]
    g3 = biased.reshape(N_EXPERT_GROUPS, GROUP_SIZE, t)
    member = lax.broadcasted_iota(I32, g3.shape, 1)
    m1 = jnp.max(g3, axis=1, keepdims=True)
    first = jnp.min(jnp.where(g3 == m1, member, GROUP_SIZE), axis=1, keepdims=True)
    m2 = jnp.max(jnp.where(member == first, -jnp.inf, g3), axis=1, keepdims=True)
    gs = m1 + m2
    gidx = lax.broadcasted_iota(I32, gs.shape, 0)
    grank = jnp.zeros(gs.shape, I32)
    for o in range(N_EXPERT_GROUPS):
        other = gs[o:o + 1]
        grank += ((other > gs) | ((other == gs) & (o < gidx))).astype(I32)
    eligible = jnp.broadcast_to(grank < TOPK_GROUPS, g3.shape).reshape(N_EXPERTS, t)
    mb = jnp.where(eligible, biased, -jnp.inf)
    eidx = lax.broadcasted_iota(I32, mb.shape, 0)
    erank = jnp.zeros(mb.shape, I32)
    for o in range(N_EXPERTS):
        other = mb[o:o + 1]
        erank += ((other > mb) | ((other == mb) & (o < eidx))).astype(I32)
    sel = eligible & (erank < TOP_K)
    ssum = jnp.sum(jnp.where(sel, scores, 0.0), axis=0, keepdims=True)
    cw_ref[...] = jnp.where(sel, scores / ssum * ROUTED_SCALE, -1.0)


def _pre(x1_p, p_chunk0, ncp, mod_p, x1_s, ncs, mod_s, wsg, wsu, wsd, wr_t, rb):
    ncp_all, c, d = x1_p.shape
    nc = PRE_TILE // c
    nchunks = ncp + ncs
    n = nchunks * c
    pt, p0 = ncp // nc, p_chunk0 // nc
    tiles_per_stream = ncp_all // mod_p.shape[0] // nc
    blk3 = pl.BlockSpec((nc, c, d), lambda i: (i, 0, 0))
    p_tile = lambda i: p0 + jnp.minimum(i, pt - 1)
    s_tile = lambda i: jnp.maximum(i - pt, 0)
    s_args, s_specs = [], []
    if ncs:
        s_args = [x1_s, mod_s]
        s_specs = [pl.BlockSpec((nc, c, d), lambda i: (s_tile(i), 0, 0)),
                   pl.BlockSpec((nc, 1, 3 * d), lambda i: (s_tile(i), 0, 0))]
    return pl.pallas_call(
        functools.partial(_pre_kernel, prompt_tiles=pt, has_sample=bool(ncs)),
        grid=(nchunks // nc,),
        in_specs=[pl.BlockSpec((nc, c, d), lambda i: (p_tile(i), 0, 0)),
                  pl.BlockSpec((1, 1, 3 * d), lambda i: (p_tile(i) // tiles_per_stream, 0, 0))] + s_specs + [
                  _const_spec(wsg.shape), _const_spec(wsu.shape), _const_spec(wsd.shape),
                  _const_spec(wr_t.shape), _const_spec(rb.shape)],
        out_specs=[pl.BlockSpec((nc * c, d // 2), lambda i: (i, 0)), blk3,
                   pl.BlockSpec((N_EXPERTS, nc * c), lambda i: (0, i))],
        out_shape=[jax.ShapeDtypeStruct((n, d // 2), I32),
                   jax.ShapeDtypeStruct((nchunks, c, d), F32),
                   jax.ShapeDtypeStruct((N_EXPERTS, n), F32)],
        compiler_params=pltpu.CompilerParams(dimension_semantics=("arbitrary",), vmem_limit_bytes=VMEM_LIMIT),
        name="pre_ffn",
    )(x1_p, mod_p, *s_args, wsg, wsu, wsd, wr_t, rb)


def _rank_kernel(cw_ref, rank_ref, cnt_ref, carry):
    i = pl.program_id(0)
    t = cw_ref.shape[1]

    @pl.when(i == 0)
    def _():
        carry[...] = jnp.zeros(carry.shape, F32)

    sel = (cw_ref[...] >= 0.0).astype(BF16)
    r = lax.broadcasted_iota(I32, (t, t), 0)
    c = lax.broadcasted_iota(I32, (t, t), 1)
    before = (r < c).astype(BF16)
    rank = carry[...] + _dot(sel, before)
    rank_ref[...] = rank.astype(I32)
    carry[...] = carry[...] + jnp.sum(sel.astype(F32), axis=1, keepdims=True)
    cnt_ref[...] = carry[...].astype(I32)


def _rank(cw):
    e, n = cw.shape
    t = RANK_TILE
    return pl.pallas_call(
        _rank_kernel,
        grid=(n // t,),
        in_specs=[pl.BlockSpec((e, t), lambda i: (0, i))],
        out_specs=[pl.BlockSpec((e, t), lambda i: (0, i)), pl.BlockSpec((e, 1), lambda i: (0, 0))],
        out_shape=[jax.ShapeDtypeStruct((e, n), I32), jax.ShapeDtypeStruct((e, 1), I32)],
        scratch_shapes=[pltpu.VMEM((e, 1), F32)],
        compiler_params=pltpu.CompilerParams(dimension_semantics=("arbitrary",)),
        name="expert_rank",
    )(cw)


def _slot_kernel(cw_ref, rank_ref, start_ref, pos_ref, w_ref):
    cw = cw_ref[...]
    e, t = cw.shape
    sel = cw >= 0.0
    r = lax.broadcasted_iota(I32, (e, e), 0)
    c = lax.broadcasted_iota(I32, (e, e), 1)
    lower = (c < r).astype(BF16)
    kidx = _dot(lower, sel.astype(BF16))
    posf = start_ref[...].astype(F32) + rank_ref[...].astype(F32)
    pos_rows, w_rows = [], []
    for k in range(TOP_K):
        m = sel & (kidx == float(k))
        pos_rows.append(jnp.sum(jnp.where(m, posf, 0.0), axis=0, keepdims=True))
        w_rows.append(jnp.sum(jnp.where(m, cw, 0.0), axis=0, keepdims=True))
    pos_ref[...] = jnp.concatenate(pos_rows, axis=0).astype(I32)
    w_pad = jnp.concatenate(w_rows + [jnp.zeros((LANES - TOP_K, t), F32)], axis=0)
    w_ref[...] = w_pad.T[:, :TOP_K]


def _slots(cw, rank, seg_start):
    e, n = cw.shape
    t = RANK_TILE
    return pl.pallas_call(
        _slot_kernel,
        grid=(n // t,),
        in_specs=[pl.BlockSpec((e, t), lambda i: (0, i)), pl.BlockSpec((e, t), lambda i: (0, i)),
                  pl.BlockSpec((e, 1), lambda i: (0, 0))],
        out_specs=[pl.BlockSpec((TOP_K, t), lambda i: (0, i)), pl.BlockSpec((t, TOP_K), lambda i: (i, 0))],
        out_shape=[jax.ShapeDtypeStruct((TOP_K, n), I32), jax.ShapeDtypeStruct((n, TOP_K), F32)],
        compiler_params=pltpu.CompilerParams(dimension_semantics=("arbitrary",)),
        name="expert_slots",
    )(cw, rank, seg_start)


def _sc_mesh():
    return plsc.VectorSubcoreMesh(core_axis_name="c", subcore_axis_name="s")


def _sc_worker_id():
    return lax.axis_index("s") * (SC_WORKERS // 16) + lax.axis_index("c")


def _sc_dispatch(rows, pos, n_rows):
    n, d = rows.shape
    per_w = n // SC_WORKERS
    w = SC_WINDOW
    n_chunks = per_w // w

    @functools.partial(
        pl.kernel, mesh=_sc_mesh(),
        out_type=jax.ShapeDtypeStruct((n_rows, d), rows.dtype),
        scratch_types=[pltpu.VMEM((2, TOP_K, w), I32), pltpu.VMEM((2, w, d), rows.dtype),
                       pltpu.SemaphoreType.DMA((2,)), pltpu.SemaphoreType.DMA((2,)), pltpu.SemaphoreType.DMA((2,))],
        name="sc_dispatch")
    def k(rows_hbm, pos_hbm, o_hbm, idx_v, rows_v, row_sem, idx_sem, out_sem):
        wid = _sc_worker_id()
        base = wid * per_w

        def loads(c, slot):
            off = pl.multiple_of(base + c * w, SUBLANES)
            return (pltpu.make_async_copy(rows_hbm.at[pl.ds(off, w)], rows_v.at[slot], row_sem.at[slot]),
                    pltpu.make_async_copy(pos_hbm.at[wid * n_chunks + c], idx_v.at[slot], idx_sem.at[slot]))

        def scatters(slot):
            return [pltpu.make_async_copy(rows_v.at[slot], o_hbm.at[idx_v.at[slot, kk]], out_sem.at[slot])
                    for kk in range(TOP_K)]

        for cp in loads(0, 0):
            cp.start()
        for c in range(n_chunks):
            slot = c % 2
            for cp in loads(c, slot):
                cp.wait()
            for cp in scatters(slot):
                cp.start()
            if c >= 1:
                for cp in scatters(1 - slot):
                    cp.wait()
            if c + 1 < n_chunks:
                for cp in loads(c + 1, 1 - slot):
                    cp.start()
        for cp in scatters((n_chunks - 1) % 2):
            cp.wait()

    pos_chunks = pos.reshape(TOP_K, n // w, w).transpose(1, 0, 2)
    return k(rows, pos_chunks)


def _sc_collect(rows, pos_flat):
    d = rows.shape[1]
    total = pos_flat.shape[0]
    per_w = total // SC_WORKERS
    w = SC_WINDOW
    n_pairs = per_w // (2 * w)

    @functools.partial(
        pl.kernel, mesh=_sc_mesh(),
        out_type=jax.ShapeDtypeStruct((total, d), rows.dtype),
        scratch_types=[pltpu.VMEM((per_w,), I32), pltpu.VMEM((2, w, d), rows.dtype),
                       pltpu.SemaphoreType.DMA((2,)), pltpu.SemaphoreType.DMA((2,))],
        name="sc_collect")
    def k(rows_hbm, pos_hbm, o_hbm, idx_v, rows_v, in_sem, out_sem):
        base = pl.multiple_of(_sc_worker_id() * per_w, SUBLANES)
        pltpu.sync_copy(pos_hbm.at[pl.ds(base, per_w)], idx_v)

        def gather(c, slot):
            idx = idx_v.at[pl.ds(pl.multiple_of(c * w, SUBLANES), w)]
            return pltpu.make_async_copy(rows_hbm.at[idx], rows_v.at[slot], in_sem.at[slot])

        def write(c, slot):
            off = pl.multiple_of(base + c * w, SUBLANES)
            return pltpu.make_async_copy(rows_v.at[slot], o_hbm.at[pl.ds(off, w)], out_sem.at[slot])

        gather(0, 0).start()

        @pl.loop(0, n_pairs)
        def _(p):
            c0 = 2 * p
            gather(c0 + 1, 1).start()
            gather(c0, 0).wait()
            write(c0, 0).start()
            gather(c0 + 1, 1).wait()
            write(c0 + 1, 1).start()
            write(c0, 0).wait()

            @pl.when(p + 1 < n_pairs)
            def _():
                gather(c0 + 2, 0).start()

            write(c0 + 1, 1).wait()

    return k(rows, pos_flat)


def _gmm_kernel(be_ref, br_ref, nu_ref, ord_ref, nxt_ref, x_hbm, wg_hbm, wu_hbm, wd_hbm, y_ref,
                wgb, wub, wdb, xbuf, xsem, wgf, wuf, wdf, wsem):
    b = pl.program_id(0)
    n_used = nu_ref[0]
    bm = xbuf.shape[1]
    e = be_ref[b]

    def w_copies(ex, slot):
        return [pltpu.make_async_copy(src.at[ex], dst.at[slot], wsem.at[slot, i])
                for i, (src, dst) in enumerate(((wg_hbm, wgf), (wu_hbm, wuf), (wd_hbm, wdf)))]

    def x_copy(blk):
        slot = blk % GMM_X_SLOTS
        return pltpu.make_async_copy(x_hbm.at[pl.ds(pl.multiple_of(blk * bm, bm), bm)], xbuf.at[slot], xsem.at[slot])

    @pl.when(b == 0)
    def _():
        for ahead in range(GMM_X_SLOTS - 1):
            @pl.when(ahead < n_used)
            def _():
                x_copy(ahead).start()

    @pl.when(b + GMM_X_SLOTS - 1 < n_used)
    def _():
        x_copy(b + GMM_X_SLOTS - 1).start()

    prev = be_ref[jnp.maximum(b - 1, 0)]

    @pl.when((b < n_used) & ((b == 0) | (e != prev)))
    def _():
        slot = ord_ref[e] % 2

        @pl.when(b == 0)
        def _():
            for cp in w_copies(e, slot):
                cp.start()

        for cp in w_copies(e, slot):
            cp.wait()
        wgb[...] = wgf[slot].astype(BF16)
        wub[...] = wuf[slot].astype(BF16)
        wdb[...] = wdf[slot].astype(BF16)
        nxt = nxt_ref[e]

        @pl.when(nxt >= 0)
        def _():
            for cp in w_copies(nxt, 1 - slot):
                cp.start()

    @pl.when(b < n_used)
    def _():
        x_copy(b).wait()

    x_ref = xbuf.at[b % GMM_X_SLOTS]

    def expert_rows(r0, n):
        rows = pl.ds(r0, n)
        lo, hi = _unpack_bf16_pairs(x_ref[rows, :])
        xb = jnp.concatenate([lo.astype(BF16), hi.astype(BF16)], axis=1)
        mid = (_silu(_dot(xb, wgb[...])) * _dot(xb, wub[...])).astype(BF16)
        y_ref[rows, :] = _pack_bf16_pairs(_dot(mid, wdb[...]))

    n_real = br_ref[b]
    n_main = n_real // GMM_SUB

    @pl.loop(0, n_main)
    def _(i):
        expert_rows(pl.multiple_of(i * GMM_SUB, GMM_SUB), GMM_SUB)

    @pl.loop(0, (n_real - n_main * GMM_SUB + GMM_TAIL - 1) // GMM_TAIL)
    def _(i):
        expert_rows(pl.multiple_of(n_main * GMM_SUB + i * GMM_TAIL, GMM_TAIL), GMM_TAIL)


def _gmm(x_sorted, block_e, block_rows, n_used, e_ord, e_next, wg, wu, wd):
    r, half = x_sorted.shape
    d = 2 * half
    bm = GMM_BM
    any_spec = pl.BlockSpec(memory_space=pl.ANY)
    return pl.pallas_call(
        _gmm_kernel,
        grid_spec=pltpu.PrefetchScalarGridSpec(
            num_scalar_prefetch=5,
            grid=(r // bm,),
            in_specs=[any_spec, any_spec, any_spec, any_spec],
            out_specs=pl.BlockSpec((bm, half), lambda b, be, br, nu, eo, en: (jnp.minimum(b, nu[0] - 1), 0)),
            scratch_shapes=[pltpu.VMEM((d, D_EXPERT), BF16), pltpu.VMEM((d, D_EXPERT), BF16),
                            pltpu.VMEM((D_EXPERT, d), BF16),
                            pltpu.VMEM((GMM_X_SLOTS, bm, half), I32), pltpu.SemaphoreType.DMA((GMM_X_SLOTS,)),
                            pltpu.VMEM((2, d, D_EXPERT), F32), pltpu.VMEM((2, d, D_EXPERT), F32),
                            pltpu.VMEM((2, D_EXPERT, d), F32), pltpu.SemaphoreType.DMA((2, 3))]),
        out_shape=jax.ShapeDtypeStruct((r, half), I32),
        compiler_params=pltpu.CompilerParams(dimension_semantics=("arbitrary",), vmem_limit_bytes=VMEM_LIMIT),
        name="expert_gmm",
    )(block_e, block_rows, n_used, e_ord, e_next, x_sorted, wg, wu, wd)


def _combine_kernel(base_ref, mod_ref, g_ref, w_ref, gain_ref, *rest):
    y_ref = rest[-1]
    nc, c, d = base_ref.shape
    w = w_ref[...]
    acc_lo = acc_hi = None
    for k in range(TOP_K):
        lo, hi = _unpack_bf16_pairs(g_ref[k])
        wk = w[:, k:k + 1]
        acc_lo = wk * lo if k == 0 else acc_lo + wk * lo
        acc_hi = wk * hi if k == 0 else acc_hi + wk * hi
    acc = jnp.concatenate([acc_lo, acc_hi], axis=1)
    g2 = mod_ref[...][:, :, 2 * d:3 * d]
    out = base_ref[...] + g2 * acc.reshape(nc, c, d)
    y_ref[...] = _rms(out) * gain_ref[...]


def _combine(base, mod, gathered, w_tok, gain, first_chunk, n_chunks, out_chunks, out_first_chunk, out_buf=None):
    _, c, d = base.shape
    nc = COMB_TILE // c
    t = nc * c
    t0, o0 = first_chunk // nc, out_first_chunk // nc
    chunks_per_stream = out_chunks // mod.shape[0]
    if chunks_per_stream == 1:
        mod_spec = pl.BlockSpec((nc, 1, 3 * d), lambda i: (o0 + i, 0, 0))
    else:
        assert chunks_per_stream % nc == 0
        mod_spec = pl.BlockSpec((1, 1, 3 * d), lambda i: ((o0 + i) * nc // chunks_per_stream, 0, 0))
    blk3 = pl.BlockSpec((nc, c, d), lambda i: (t0 + i, 0, 0))
    in_specs = [blk3, mod_spec,
                pl.BlockSpec((TOP_K, t, d // 2), lambda i: (0, t0 + i, 0)),
                pl.BlockSpec((t, TOP_K), lambda i: (t0 + i, 0)),
                pl.BlockSpec((1, 1, d), lambda i: (0, 0, 0))]
    args = [base, mod, gathered, w_tok, gain.reshape(1, 1, d)]
    aliases = {}
    if out_buf is not None:
        in_specs.append(pl.BlockSpec(memory_space=pl.ANY))
        args.append(out_buf)
        aliases = {len(args) - 1: 0}
    return pl.pallas_call(
        _combine_kernel,
        grid=(n_chunks // nc,),
        in_specs=in_specs,
        out_specs=pl.BlockSpec((nc, c, d), lambda i: (o0 + i, 0, 0)),
        out_shape=jax.ShapeDtypeStruct((out_chunks, c, d), F32),
        input_output_aliases=aliases,
        compiler_params=pltpu.CompilerParams(dimension_semantics=("arbitrary",), vmem_limit_bytes=VMEM_LIMIT),
        name="combine_norm",
    )(*args)


def _rope_tables(pos):
    half = HEAD_DIM // 2
    inv_freq = ROPE_THETA ** (-jnp.arange(half, dtype=F32) / half)
    ang = pos.astype(F32)[:, None] * inv_freq[None, :]
    cos, sin = jnp.cos(ang), jnp.sin(ang)
    reps = LANES // HEAD_DIM
    return jnp.tile(jnp.concatenate([cos, cos], axis=1), (1, reps)), jnp.tile(jnp.concatenate([-sin, sin], axis=1), (1, reps))


def _routed_ffn(h2, cw, w_gate, w_up, w_down):
    n, half = h2.shape
    rank, counts = _rank(cw)
    bm = GMM_BM
    padded = (counts[:, 0] + bm - 1) // bm * bm
    seg_end = jnp.cumsum(padded)
    seg_start = (seg_end - padded).astype(I32)
    n_blocks = -(-n * TOP_K // bm) + N_EXPERTS
    n_rows = n_blocks * bm
    block_start = jnp.arange(n_blocks, dtype=I32) * bm
    block_e = jnp.minimum(jnp.sum((seg_end[None, :] <= block_start[:, None]).astype(I32), axis=1), N_EXPERTS - 1)
    own = block_e[:, None] == jnp.arange(N_EXPERTS, dtype=I32)[None, :]
    real_end = jnp.sum(jnp.where(own, (seg_start + counts[:, 0])[None, :], 0), axis=1)
    block_rows = jnp.clip(real_end - block_start, 0, bm).astype(I32)
    n_used = (seg_end[-1:] // bm).astype(I32)
    has_rows = counts[:, 0] > 0
    eids = jnp.arange(N_EXPERTS, dtype=I32)
    e_ord = (jnp.cumsum(has_rows.astype(I32)) - has_rows.astype(I32)).astype(I32)
    later = has_rows[None, :] & (eids[None, :] > eids[:, None])
    e_next = jnp.min(jnp.where(later, eids[None, :], N_EXPERTS), axis=1)
    e_next = jnp.where(e_next == N_EXPERTS, -1, e_next).astype(I32)
    pos, w_k = _slots(cw, rank, seg_start[:, None])
    x_sorted = _sc_dispatch(h2, pos, n_rows)
    y_sorted = _gmm(x_sorted, block_e, block_rows, n_used, e_ord, e_next, w_gate, w_up, w_down)
    gathered = _sc_collect(y_sorted, pos.reshape(TOP_K * n)).reshape(TOP_K, n, half)
    return gathered, w_k


def kernel(x_prompt, x_sample, cache_k, cache_v, state_conv, c_prompt, c_sample, w_ada, b_ada, w_in, w_conv,
           w_conv_out, w_attn_o, attn_sinks, w_mix_out, w_router, router_bias, w_exp_gate, w_exp_up, w_exp_down,
           w_sh_gate, w_sh_up, w_sh_down, final_gain):
    assert w_ada.shape[0] == 1, "one layer"
    bp, seq, d = x_prompt.shape
    bs, ts, _ = x_sample.shape
    assert ts == CHUNK and seq % MIX_TILE == 0 and bs % SAMPLE_BB == 0

    c_all = jnp.concatenate([c_prompt, c_sample], axis=0)
    pad = (-c_all.shape[0]) % SUBLANES
    mod = _ada(jnp.pad(c_all, ((0, pad), (0, 0))), w_ada[0], b_ada[0])[:bp + bs]
    mod_p, mod_s = mod[:bp, None, :], mod[bp:, None, :]

    head_axes = (N_KV_HEADS // 2, 2, GQA_GROUP, HEAD_DIM)
    w_in_l = w_in[0]
    w_q = w_in_l[:, OFF_Q:OFF_K].reshape((d,) + head_axes).transpose(0, 1, 3, 2, 4).reshape(d, Q_DIM)
    w_o = w_attn_o[0].reshape(head_axes + (d,)).transpose(0, 2, 1, 3, 4).reshape(Q_DIM, d)
    win = tuple(w.astype(BF16) for w in (w_in_l[:, :OFF_Q], w_q, w_in_l[:, OFF_K:]))
    wco, wao, wmo = (w.astype(BF16) for w in (w_conv_out[0], w_o, w_mix_out[0]))
    cos_p, sin_p = _rope_tables(jnp.arange(seq, dtype=I32))
    cos_s, sin_s = _rope_tables(PAST_LEN + jnp.arange(ts, dtype=I32))

    x1_p, conv_p, k_p, v_p = _mixer_prompt(x_prompt, mod_p, cos_p, sin_p, win, w_conv[0], wco, wao, attn_sinks[0], wmo)
    x1_s, conv_s, k_s, v_s = _mixer_sample(
        x_sample, mod_s, cos_s, sin_s, cache_k[0].reshape(bs, WINDOW, KV_DIM), cache_v[0].reshape(bs, WINDOW, KV_DIM),
        state_conv[0], win, w_conv[0], wco, wao, attn_sinks[0], wmo)

    n_p, n_s = bp * seq, bs * ts
    n = n_p + n_s
    mod2_p, mod2_s = mod[:bp, None, 3 * d:], mod[bp:, None, 3 * d:]
    x1_pc = x1_p.reshape(n_p // CHUNK, CHUNK, d)
    wsg, wsu, wsd = (w[0].astype(BF16) for w in (w_sh_gate, w_sh_up, w_sh_down))
    wr_t, rb = w_router[0].T, router_bias[0][:, None]

    ncp, ncs = n_p // CHUNK, n_s // CHUNK
    half = (ncp + ncs) // 2
    tile_chunks = max(PRE_TILE, COMB_TILE) // CHUNK
    assert half <= ncp and half % tile_chunks == 0 and (ncp - half) % tile_chunks == 0 and ncs % tile_chunks == 0
    assert (half * CHUNK) % (SC_WORKERS * SC_WINDOW) == 0 and (half * CHUNK) % RANK_TILE == 0
    y_p = None
    for p0, np_c, ns_c in ((0, half, 0), (half, ncp - half, ncs)):
        h2, base, cw = _pre(x1_pc, p0, np_c, mod2_p, x1_s, ns_c, mod2_s, wsg, wsu, wsd, wr_t, rb)
        gathered, w_tok = _routed_ffn(h2, cw, w_exp_gate[0], w_exp_up[0], w_exp_down[0])
        y_p = _combine(base, mod2_p, gathered, w_tok, final_gain, 0, np_c, ncp, p0, out_buf=y_p)
        if ns_c:
            y_s = _combine(base, mod2_s, gathered, w_tok, final_gain, np_c, ns_c, ncs, 0)

    kv = lambda a: a.reshape(1, a.shape[0], WINDOW, N_KV_HEADS, HEAD_DIM)
    return (y_p.reshape(bp, seq, d), y_s, conv_p[None], kv(k_p), kv(v_p), conv_s[None], kv(k_s), kv(v_s))
```

```python
import functools

import jax
import jax.numpy as jnp
from jax import lax
from jax.experimental import pallas as pl
from jax.experimental.pallas import tpu as pltpu
from jax.experimental.pallas import tpu_sc as plsc

F32 = jnp.float32
BF16 = jnp.bfloat16
I32 = jnp.int32

D_MODEL = 1024
CHUNK = 64
D_CONV = 1024
CONV_W = 3
N_HEADS = 16
N_KV_HEADS = 4
HEAD_DIM = 64
GQA_GROUP = N_HEADS // N_KV_HEADS
WINDOW = 128
ROPE_THETA = 10000.0
ATTN_SCALE = HEAD_DIM ** -0.5
N_EXPERTS = 64
TOP_K = 8
N_EXPERT_GROUPS = 8
GROUP_SIZE = N_EXPERTS // N_EXPERT_GROUPS
TOPK_GROUPS = 4
D_EXPERT = 256
D_SHARED = 256
ROUTED_SCALE = 2.5
EPS = 1e-6
PAST_LEN = 4096
Q_DIM = N_HEADS * HEAD_DIM
KV_DIM = N_KV_HEADS * HEAD_DIM
OFF_GB, OFF_GC, OFF_XC, OFF_Q, OFF_K, OFF_V, OFF_GCONV, OFF_GATTN, D_IN = (
    0, 1024, 2048, 3072, 4096, 4352, 4608, 5632, 6656)

LANES = 128
SUBLANES = 8
VMEM_LIMIT = 56 * 1024 * 1024

MIX_TILE = 512
ATT_Q = 128
MIX_SIDE_COLS = 256
SAMPLE_BB = 8
PRE_TILE = 512
RANK_TILE = 512
GMM_SUB = 512
GMM_TAIL = 128
GMM_X_SLOTS = 3
COMB_TILE = 256
SC_WORKERS = 32
SC_WINDOW = 96


def _const_spec(shape):
    nd = len(shape)
    return pl.BlockSpec(shape, lambda *_: (0,) * nd, pipeline_mode=pl.Buffered(1))


def _rms(x):
    return x * lax.rsqrt(jnp.mean(x * x, axis=-1, keepdims=True) + EPS)


def _sigmoid(x):
    return 1.0 / (1.0 + jnp.exp(-x))


def _silu(x):
    return x * _sigmoid(x)


def _dot(a, b):
    return jnp.dot(a, b, preferred_element_type=F32)


def _pack_bf16_pairs(x):
    half = x.shape[-1] // 2
    lo = lax.bitcast_convert_type(x[..., :half].astype(BF16).astype(F32), I32)
    hi = lax.bitcast_convert_type(x[..., half:].astype(BF16).astype(F32), I32)
    return lax.shift_right_logical(lo, 16) | hi


def _unpack_bf16_pairs(words):
    lo = lax.bitcast_convert_type(lax.shift_left(words, 16), F32)
    hi = lax.bitcast_convert_type(words & jnp.int32(-65536), F32)
    return lo, hi


def _ada_kernel(c_ref, w_ref, b_ref, o_ref):
    s = _silu(c_ref[...]).astype(BF16)
    o_ref[...] = _dot(s, w_ref[...].astype(BF16)) + b_ref[...]


def _ada(c_all, w_ada, b_ada):
    rows = c_all.shape[0]
    n_out = w_ada.shape[1]
    bn = 768
    return pl.pallas_call(
        _ada_kernel,
        grid=(n_out // bn,),
        in_specs=[pl.BlockSpec((rows, D_MODEL), lambda i: (0, 0)),
                  pl.BlockSpec((D_MODEL, bn), lambda i: (0, i)),
                  pl.BlockSpec((1, bn), lambda i: (0, i))],
        out_specs=pl.BlockSpec((rows, bn), lambda i: (0, i)),
        out_shape=jax.ShapeDtypeStruct((rows, n_out), F32),
        name="ada_mod",
    )(c_all, w_ada, b_ada.reshape(1, n_out))


def _rope(x, cos, sin_signed):
    lane = lax.broadcasted_iota(I32, (x.shape[0], LANES), 1)
    first_half = (lane % HEAD_DIM) < (HEAD_DIM // 2)
    outs = []
    for g in range(x.shape[1] // LANES):
        xg = x[:, g * LANES:(g + 1) * LANES]
        up = pltpu.roll(xg, LANES - HEAD_DIM // 2, axis=1)
        down = pltpu.roll(xg, HEAD_DIM // 2, axis=1)
        partner = jnp.where(first_half, up, down)
        outs.append(xg * cos + partner * sin_signed)
    return jnp.concatenate(outs, axis=1)


def _attention(blocks, sinks_ref, obuf, between):
    rq = GQA_GROUP * ATT_Q
    low = lax.broadcasted_iota(I32, (ATT_Q, LANES), 1) < HEAD_DIM
    head_of_lane = lax.broadcasted_iota(I32, (1, rq), 1) // ATT_Q
    units = [(b, pair, par) for b in range(len(blocks)) for pair in range(N_KV_HEADS // 2) for par in range(2)]
    loaded = {}

    def scores(u):
        b, pair, par = units[u]
        if b not in loaded:
            loaded.clear()
            loaded[b] = blocks[b]()
        q_blk, k_of_pair, _, mask, _ = loaded[b]
        keep = low if par == 0 else jnp.logical_not(low)
        cols = [q_blk[:, (GQA_GROUP * pair + i) * LANES:(GQA_GROUP * pair + i + 1) * LANES] for i in range(GQA_GROUP)]
        qg = jnp.concatenate([jnp.where(keep, c, jnp.zeros_like(c)) for c in cols], axis=0)
        st = lax.dot_general(k_of_pair(pair), qg, (((1,), (1,)), ((), ())), preferred_element_type=F32)
        vt = loaded[b][2](pair)[par * HEAD_DIM:(par + 1) * HEAD_DIM, :]
        return jnp.where(mask, st, -jnp.inf), vt, loaded[b][4]

    outs = []
    nxt = scores(0)
    for u, (b, pair, par) in enumerate(units):
        st, vt, row0 = nxt
        if u + 1 < len(units):
            nxt = scores(u + 1)
        if between:
            between.pop(0)()
        g = 2 * pair + par
        sink = jnp.full((1, rq), sinks_ref[g * GQA_GROUP + GQA_GROUP - 1], F32)
        for i in range(GQA_GROUP - 2, -1, -1):
            sink = jnp.where(head_of_lane == i, sinks_ref[g * GQA_GROUP + i], sink)
        m = jnp.maximum(jnp.max(st, axis=0, keepdims=True), sink)
        e = jnp.exp(st - m)
        z = jnp.sum(e, axis=0, keepdims=True) + jnp.exp(sink - m)
        outs.append(_dot(vt, e.astype(BF16)) / z)
        if par == 1:
            for i in range(GQA_GROUP):
                blk = jnp.concatenate([o[:, i * ATT_Q:(i + 1) * ATT_Q] for o in outs], axis=0)
                c0 = (GQA_GROUP * pair + i) * LANES
                obuf[row0:row0 + ATT_Q, c0:c0 + LANES] = blk.T
            outs = []
    for step in between:
        step()


def _in_proj(hb, win_refs, lo, hi):
    w_pre, w_q, w_post = win_refs
    if hi <= OFF_Q:
        return _dot(hb, w_pre[:, lo:hi])
    if lo >= OFF_K:
        return _dot(hb, w_post[:, lo - OFF_K:hi - OFF_K])
    assert (lo, hi) == (OFF_Q, OFF_K)
    return _dot(hb, w_q[...])


def _attention_free_steps(hb, conv, win_ref, wco_ref):
    out, parts = {}, {}
    n_parts = D_MODEL // MIX_SIDE_COLS

    def step(name, piece, compute):
        def run():
            parts.setdefault(name, []).append(compute(piece * MIX_SIDE_COLS, (piece + 1) * MIX_SIDE_COLS))
            if piece == n_parts - 1:
                out[name] = jnp.concatenate(parts.pop(name), axis=1)
        return run

    def conv_in():
        if "conv_in" not in out:
            out["conv_in"] = (out.pop("gate_b") * conv).astype(BF16)
        return out["conv_in"]

    computes = [("gate_b", lambda lo, hi: _in_proj(hb, win_ref, OFF_GB + lo, OFF_GB + hi)),
                ("g_conv", lambda lo, hi: _in_proj(hb, win_ref, OFF_GCONV + lo, OFF_GCONV + hi)),
                ("g_attn", lambda lo, hi: _in_proj(hb, win_ref, OFF_GATTN + lo, OFF_GATTN + hi)),
                ("y_conv", lambda lo, hi: _dot(conv_in(), wco_ref[:, lo:hi]))]
    return [step(name, p, fn) for name, fn in computes for p in range(n_parts)], out


def _mix_out(x, g1, side, y_attn_in, wao_ref, wmo_ref):
    y_attn = _dot(y_attn_in.astype(BF16), wao_ref[...])
    merged = _sigmoid(side["g_conv"]) * side["y_conv"] + _sigmoid(side["g_attn"]) * y_attn
    return x + g1 * _dot(merged.astype(BF16), wmo_ref[...])


def _mixer_prompt_kernel(x_ref, mod_ref, cos_ref, sin_ref, wpre_ref, wq_ref, wpost_ref, wconv_ref, wco_ref, wao_ref,
                         sinks_ref, wmo_ref, x1_ref, conv_ref, k_ref, v_ref, ubuf, kbuf, vtbuf, obuf):
    win_ref = (wpre_ref, wq_ref, wpost_ref)
    j = pl.program_id(1)
    t = x_ref.shape[1]

    @pl.when(j == 0)
    def _():
        ubuf[0:SUBLANES, :] = jnp.zeros((SUBLANES, D_CONV), F32)
        kbuf[0:WINDOW, :] = jnp.zeros((WINDOW, KV_DIM), BF16)
        vtbuf[:, 0:WINDOW] = jnp.zeros((KV_DIM, WINDOW), BF16)

    x = x_ref[0]
    mod = mod_ref[0]
    sh1, sc1, g1 = mod[:, 0:D_MODEL], mod[:, D_MODEL:2 * D_MODEL], mod[:, 2 * D_MODEL:3 * D_MODEL]
    hb = (_rms(x) * (1.0 + sc1) + sh1).astype(BF16)

    u = _in_proj(hb, win_ref, OFF_GC, OFF_XC) * _in_proj(hb, win_ref, OFF_XC, OFF_Q)
    ubuf[SUBLANES:SUBLANES + t, :] = u
    wc = wconv_ref[...]
    conv = wc[0:1] * ubuf[SUBLANES - 2:SUBLANES - 2 + t, :] + wc[1:2] * ubuf[SUBLANES - 1:SUBLANES - 1 + t, :] + wc[2:3] * u
    conv_ref[0] = u[t - (CONV_W - 1):t]
    ubuf[SUBLANES - 2:SUBLANES, :] = u[t - (CONV_W - 1):t]

    cos, sin = cos_ref[...], sin_ref[...]
    q = (_rope(_in_proj(hb, win_ref, OFF_Q, OFF_K), cos, sin) * ATTN_SCALE).astype(BF16)
    k = _rope(_in_proj(hb, win_ref, OFF_K, OFF_V), cos, sin)
    v = _in_proj(hb, win_ref, OFF_V, OFF_GCONV)
    kbuf[WINDOW:WINDOW + t, :] = k.astype(BF16)
    vtbuf[:, WINDOW:WINDOW + t] = v.T.astype(BF16)
    k_ref[0] = k[t - WINDOW:t]
    v_ref[0] = v[t - WINDOW:t]

    nkeys = ATT_Q + WINDOW
    rq = GQA_GROUP * ATT_Q
    ki = lax.broadcasted_iota(I32, (nkeys, rq), 0)
    qi = lax.broadcasted_iota(I32, (nkeys, rq), 1) % ATT_Q
    band = ki // CHUNK - qi // CHUNK
    band_ok = (band >= 0) & (band <= WINDOW // CHUNK)
    def block(s):
        def load():
            mask = band_ok & (ki + (j * t + s * ATT_Q - WINDOW) >= 0)
            k_of_pair = lambda pair: kbuf[s * ATT_Q:s * ATT_Q + nkeys, pair * LANES:(pair + 1) * LANES]
            vt_of_pair = lambda pair: vtbuf[pair * LANES:(pair + 1) * LANES, s * ATT_Q:s * ATT_Q + nkeys]
            return q[s * ATT_Q:(s + 1) * ATT_Q], k_of_pair, vt_of_pair, mask, s * ATT_Q
        return load

    steps, side = _attention_free_steps(hb, conv, win_ref, wco_ref)
    _attention([block(s) for s in range(t // ATT_Q)], sinks_ref, obuf, steps)
    kbuf[0:WINDOW, :] = kbuf[t:t + WINDOW, :]
    vtbuf[:, 0:WINDOW] = vtbuf[:, t:t + WINDOW]

    x1_ref[0] = _mix_out(x, g1, side, obuf[...], wao_ref, wmo_ref)


def _mixer_prompt(x, mod, cos, sin, win, wconv, wco, wao, sinks, wmo):
    b, seq, d = x.shape
    t = MIX_TILE
    return pl.pallas_call(
        _mixer_prompt_kernel,
        grid=(b, seq // t),
        in_specs=[pl.BlockSpec((1, t, d), lambda i, j: (i, j, 0)),
                  pl.BlockSpec((1, 1, 6 * d), lambda i, j: (i, 0, 0)),
                  pl.BlockSpec((t, LANES), lambda i, j: (j, 0)),
                  pl.BlockSpec((t, LANES), lambda i, j: (j, 0)),
                  *[_const_spec(w.shape) for w in win],
                  _const_spec(wconv.shape), _const_spec(wco.shape), _const_spec(wao.shape),
                  pl.BlockSpec(memory_space=pltpu.SMEM),
                  _const_spec(wmo.shape)],
        out_specs=[pl.BlockSpec((1, t, d), lambda i, j: (i, j, 0)),
                   pl.BlockSpec((1, CONV_W - 1, D_CONV), lambda i, j: (i, 0, 0)),
                   pl.BlockSpec((1, WINDOW, KV_DIM), lambda i, j: (i, 0, 0)),
                   pl.BlockSpec((1, WINDOW, KV_DIM), lambda i, j: (i, 0, 0))],
        out_shape=[jax.ShapeDtypeStruct((b, seq, d), F32),
                   jax.ShapeDtypeStruct((b, CONV_W - 1, D_CONV), F32),
                   jax.ShapeDtypeStruct((b, WINDOW, KV_DIM), F32),
                   jax.ShapeDtypeStruct((b, WINDOW, KV_DIM), F32)],
        scratch_shapes=[pltpu.VMEM((SUBLANES + t, D_CONV), F32),
                        pltpu.VMEM((WINDOW + t, KV_DIM), BF16),
                        pltpu.VMEM((KV_DIM, WINDOW + t), BF16),
                        pltpu.VMEM((t, Q_DIM), F32)],
        compiler_params=pltpu.CompilerParams(dimension_semantics=("arbitrary", "arbitrary"),
                                             vmem_limit_bytes=VMEM_LIMIT),
        name="mixer_prompt",
    )(x, mod, cos, sin, *win, wconv, wco, wao, sinks, wmo)


def _mixer_sample_kernel(x_ref, mod_ref, cos_ref, sin_ref, ck_ref, cv_ref, sconv_ref, wpre_ref, wq_ref, wpost_ref,
                         wconv_ref, wco_ref, wao_ref, sinks_ref, wmo_ref, x1_ref, conv_ref, k_ref, v_ref, ubuf, obuf):
    win_ref = (wpre_ref, wq_ref, wpost_ref)
    bb, t, d = x_ref.shape
    x3 = x_ref[...]
    mod = mod_ref[...]
    sh1, sc1, g1 = mod[:, :, 0:d], mod[:, :, d:2 * d], mod[:, :, 2 * d:3 * d]
    x = x3.reshape(bb * t, d)
    hb = (_rms(x3) * (1.0 + sc1) + sh1).astype(BF16).reshape(bb * t, d)

    u = _in_proj(hb, win_ref, OFF_GC, OFF_XC) * _in_proj(hb, win_ref, OFF_XC, OFF_Q)
    u3 = u.reshape(bb, t, D_CONV)
    ubuf[:, SUBLANES - 2:SUBLANES, :] = sconv_ref[...]
    ubuf[:, SUBLANES:SUBLANES + t, :] = u3
    wc = wconv_ref[...]
    conv = (wc[0:1] * ubuf[:, SUBLANES - 2:SUBLANES - 2 + t, :] + wc[1:2] * ubuf[:, SUBLANES - 1:SUBLANES - 1 + t, :]
            + wc[2:3] * u3).reshape(bb * t, D_CONV)
    conv_ref[...] = u3[:, t - (CONV_W - 1):t, :]

    cos = jnp.concatenate([cos_ref[...]] * bb, axis=0)
    sin = jnp.concatenate([sin_ref[...]] * bb, axis=0)
    q = (_rope(_in_proj(hb, win_ref, OFF_Q, OFF_K), cos, sin) * ATTN_SCALE).astype(BF16)
    k = _rope(_in_proj(hb, win_ref, OFF_K, OFF_V), cos, sin)
    v = _in_proj(hb, win_ref, OFF_V, OFF_GCONV)
    per = ATT_Q // t
    nkeys = per * (WINDOW + t)
    rq = GQA_GROUP * ATT_Q
    key_stream = lax.broadcasted_iota(I32, (nkeys, rq), 0) // (WINDOW + t)
    query_stream = (lax.broadcasted_iota(I32, (nkeys, rq), 1) % ATT_Q) // t
    mask = key_stream == query_stream
    def block(blk):
        def load():
            k_parts, v_parts = [], []
            for b in range(blk * per, (blk + 1) * per):
                kb, vb = k[b * t:(b + 1) * t], v[b * t:(b + 1) * t]
                ck, cv = ck_ref[b], cv_ref[b]
                k_ref[b] = jnp.concatenate([ck[t:WINDOW], kb], axis=0)
                v_ref[b] = jnp.concatenate([cv[t:WINDOW], vb], axis=0)
                k_parts += [ck, kb]
                v_parts += [cv, vb]
            k_all = jnp.concatenate(k_parts, axis=0).astype(BF16)
            vt_all = jnp.concatenate(v_parts, axis=0).T.astype(BF16)
            k_of_pair = lambda pair: k_all[:, pair * LANES:(pair + 1) * LANES]
            vt_of_pair = lambda pair: vt_all[pair * LANES:(pair + 1) * LANES, :]
            return q[blk * ATT_Q:(blk + 1) * ATT_Q], k_of_pair, vt_of_pair, mask, blk * ATT_Q
        return load

    steps, side = _attention_free_steps(hb, conv, win_ref, wco_ref)
    _attention([block(blk) for blk in range(bb // per)], sinks_ref, obuf, steps)

    g1f = jnp.broadcast_to(g1, (bb, t, d)).reshape(bb * t, d)
    x1_ref[...] = _mix_out(x, g1f, side, obuf[...], wao_ref, wmo_ref).reshape(bb, t, d)


def _mixer_sample(x, mod, cos, sin, ck, cv, sconv, win, wconv, wco, wao, sinks, wmo):
    b, t, d = x.shape
    bb = SAMPLE_BB
    blk = lambda *s: pl.BlockSpec((bb,) + s, lambda i: (i, 0, 0))
    return pl.pallas_call(
        _mixer_sample_kernel,
        grid=(b // bb,),
        in_specs=[blk(t, d), blk(1, 6 * d),
                  pl.BlockSpec((t, LANES), lambda i: (0, 0)), pl.BlockSpec((t, LANES), lambda i: (0, 0)),
                  blk(WINDOW, KV_DIM), blk(WINDOW, KV_DIM), blk(CONV_W - 1, D_CONV),
                  *[_const_spec(w.shape) for w in win],
                  _const_spec(wconv.shape), _const_spec(wco.shape), _const_spec(wao.shape),
                  pl.BlockSpec(memory_space=pltpu.SMEM),
                  _const_spec(wmo.shape)],
        out_specs=[blk(t, d), blk(CONV_W - 1, D_CONV), blk(WINDOW, KV_DIM), blk(WINDOW, KV_DIM)],
        out_shape=[jax.ShapeDtypeStruct((b, t, d), F32),
                   jax.ShapeDtypeStruct((b, CONV_W - 1, D_CONV), F32),
                   jax.ShapeDtypeStruct((b, WINDOW, KV_DIM), F32),
                   jax.ShapeDtypeStruct((b, WINDOW, KV_DIM), F32)],
        scratch_shapes=[pltpu.VMEM((bb, SUBLANES + t, D_CONV), F32),
                        pltpu.VMEM((bb * t, Q_DIM), F32)],
        compiler_params=pltpu.CompilerParams(dimension_semantics=("arbitrary",), vmem_limit_bytes=VMEM_LIMIT),
        name="mixer_sample",
    )(x, mod, cos, sin, ck, cv, sconv, *win, wconv, wco, wao, sinks, wmo)


def _pre_kernel(*refs, prompt_tiles, has_sample):
    if has_sample:
        xp_ref, mp_ref, xs_ref, ms_ref, wsg_ref, wsu_ref, wsd_ref, wr_ref, rb_ref, h2_ref, base_ref, cw_ref = refs
    else:
        xp_ref, mp_ref, wsg_ref, wsu_ref, wsd_ref, wr_ref, rb_ref, h2_ref, base_ref, cw_ref = refs
    nc, c, d = xp_ref.shape
    t = nc * c
    x3, mod = xp_ref[...], mp_ref[...]
    if has_sample:
        is_prompt = pl.program_id(0) < prompt_tiles
        x3 = jnp.where(is_prompt, x3, xs_ref[...])
        mod = jnp.where(is_prompt, mod, ms_ref[...])
    sh2, sc2, g2 = mod[:, :, 0:d], mod[:, :, d:2 * d], mod[:, :, 2 * d:3 * d]
    h3 = _rms(x3) * (1.0 + sc2) + sh2
    h2 = h3.reshape(t, d)
    hb = h2.astype(BF16)
    h2_ref[...] = _pack_bf16_pairs(h2)
    shared = _dot((_silu(_dot(hb, wsg_ref[...])) * _dot(hb, wsu_ref[...])).astype(BF16), wsd_ref[...])
    base_ref[...] = x3 + g2 * shared.reshape(nc, c, d)

    logits = lax.dot_general(wr_ref[...], h2, (((1,), (1,)), ((), ())), preferred_element_type=F32,
                             precision=lax.Precision.HIGHEST)
    scores = _sigmoid(logits)
    biased = scores + rb_ref[...]
    g3 = biased.reshape(N_EXPERT_GROUPS, GROUP_SIZE, t)
    member = lax.broadcasted_iota(I32, g3.shape, 1)
    m1 = jnp.max(g3, axis=1, keepdims=True)
    first = jnp.min(jnp.where(g3 == m1, member, GROUP_SIZE), axis=1, keepdims=True)
    m2 = jnp.max(jnp.where(member == first, -jnp.inf, g3), axis=1, keepdims=True)
    gs = m1 + m2
    gidx = lax.broadcasted_iota(I32, gs.shape, 0)
    grank = jnp.zeros(gs.shape, I32)
    for o in range(N_EXPERT_GROUPS):
        other = gs[o:o + 1]
        grank += ((other > gs) | ((other == gs) & (o < gidx))).astype(I32)
    eligible = jnp.broadcast_to(grank < TOPK_GROUPS, g3.shape).reshape(N_EXPERTS, t)
    mb = jnp.where(eligible, biased, -jnp.inf)
    eidx = lax.broadcasted_iota(I32, mb.shape, 0)
    erank = jnp.zeros(mb.shape, I32)
    for o in range(N_EXPERTS):
        other = mb[o:o + 1]
        erank += ((other > mb) | ((other == mb) & (o < eidx))).astype(I32)
    sel = eligible & (erank < TOP_K)
    ssum = jnp.sum(jnp.where(sel, scores, 0.0), axis=0, keepdims=True)
    cw_ref[...] = jnp.where(sel, scores / ssum * ROUTED_SCALE, -1.0)


def _pre(x1_p, p_chunk0, ncp, mod_p, x1_s, ncs, mod_s, wsg, wsu, wsd, wr_t, rb):
    ncp_all, c, d = x1_p.shape
    nc = PRE_TILE // c
    nchunks = ncp + ncs
    n = nchunks * c
    pt, p0 = ncp // nc, p_chunk0 // nc
    tiles_per_stream = ncp_all // mod_p.shape[0] // nc
    blk3 = pl.BlockSpec((nc, c, d), lambda i: (i, 0, 0))
    p_tile = lambda i: p0 + jnp.minimum(i, pt - 1)
    s_tile = lambda i: jnp.maximum(i - pt, 0)
    s_args, s_specs = [], []
    if ncs:
        s_args = [x1_s, mod_s]
        s_specs = [pl.BlockSpec((nc, c, d), lambda i: (s_tile(i), 0, 0)),
                   pl.BlockSpec((nc, 1, 3 * d), lambda i: (s_tile(i), 0, 0))]
    return pl.pallas_call(
        functools.partial(_pre_kernel, prompt_tiles=pt, has_sample=bool(ncs)),
        grid=(nchunks // nc,),
        in_specs=[pl.BlockSpec((nc, c, d), lambda i: (p_tile(i), 0, 0)),
                  pl.BlockSpec((1, 1, 3 * d), lambda i: (p_tile(i) // tiles_per_stream, 0, 0))] + s_specs + [
                  _const_spec(wsg.shape), _const_spec(wsu.shape), _const_spec(wsd.shape),
                  _const_spec(wr_t.shape), _const_spec(rb.shape)],
        out_specs=[pl.BlockSpec((nc * c, d // 2), lambda i: (i, 0)), blk3,
                   pl.BlockSpec((N_EXPERTS, nc * c), lambda i: (0, i))],
        out_shape=[jax.ShapeDtypeStruct((n, d // 2), I32),
                   jax.ShapeDtypeStruct((nchunks, c, d), F32),
                   jax.ShapeDtypeStruct((N_EXPERTS, n), F32)],
        compiler_params=pltpu.CompilerParams(dimension_semantics=("arbitrary",), vmem_limit_bytes=VMEM_LIMIT),
        name="pre_ffn",
    )(x1_p, mod_p, *s_args, wsg, wsu, wsd, wr_t, rb)


def _rank_kernel(cw_ref, rank_ref, cnt_ref, carry):
    i = pl.program_id(0)
    t = cw_ref.shape[1]

    @pl.when(i == 0)
    def _():
        carry[...] = jnp.zeros(carry.shape, F32)

    sel = (cw_ref[...] >= 0.0).astype(BF16)
    r = lax.broadcasted_iota(I32, (t, t), 0)
    c = lax.broadcasted_iota(I32, (t, t), 1)
    before = (r < c).astype(BF16)
    rank = carry[...] + _dot(sel, before)
    rank_ref[...] = rank.astype(I32)
    carry[...] = carry[...] + jnp.sum(sel.astype(F32), axis=1, keepdims=True)
    cnt_ref[...] = carry[...].astype(I32)


def _rank(cw):
    e, n = cw.shape
    t = RANK_TILE
    return pl.pallas_call(
        _rank_kernel,
        grid=(n // t,),
        in_specs=[pl.BlockSpec((e, t), lambda i: (0, i))],
        out_specs=[pl.BlockSpec((e, t), lambda i: (0, i)), pl.BlockSpec((e, 1), lambda i: (0, 0))],
        out_shape=[jax.ShapeDtypeStruct((e, n), I32), jax.ShapeDtypeStruct((e, 1), I32)],
        scratch_shapes=[pltpu.VMEM((e, 1), F32)],
        compiler_params=pltpu.CompilerParams(dimension_semantics=("arbitrary",)),
        name="expert_rank",
    )(cw)


def _slot_kernel(cw_ref, rank_ref, start_ref, pos_ref, w_ref):
    cw = cw_ref[...]
    e, t = cw.shape
    sel = cw >= 0.0
    r = lax.broadcasted_iota(I32, (e, e), 0)
    c = lax.broadcasted_iota(I32, (e, e), 1)
    lower = (c < r).astype(BF16)
    kidx = _dot(lower, sel.astype(BF16))
    posf = start_ref[...].astype(F32) + rank_ref[...].astype(F32)
    pos_rows, w_rows = [], []
    for k in range(TOP_K):
        m = sel & (kidx == float(k))
        pos_rows.append(jnp.sum(jnp.where(m, posf, 0.0), axis=0, keepdims=True))
        w_rows.append(jnp.sum(jnp.where(m, cw, 0.0), axis=0, keepdims=True))
    pos_ref[...] = jnp.concatenate(pos_rows, axis=0).astype(I32)
    w_pad = jnp.concatenate(w_rows + [jnp.zeros((LANES - TOP_K, t), F32)], axis=0)
    w_ref[...] = w_pad.T[:, :TOP_K]


def _slots(cw, rank, seg_start):
    e, n = cw.shape
    t = RANK_TILE
    return pl.pallas_call(
        _slot_kernel,
        grid=(n // t,),
        in_specs=[pl.BlockSpec((e, t), lambda i: (0, i)), pl.BlockSpec((e, t), lambda i: (0, i)),
                  pl.BlockSpec((e, 1), lambda i: (0, 0))],
        out_specs=[pl.BlockSpec((TOP_K, t), lambda i: (0, i)), pl.BlockSpec((t, TOP_K), lambda i: (i, 0))],
        out_shape=[jax.ShapeDtypeStruct((TOP_K, n), I32), jax.ShapeDtypeStruct((n, TOP_K), F32)],
        compiler_params=pltpu.CompilerParams(dimension_semantics=("arbitrary",)),
        name="expert_slots",
    )(cw, rank, seg_start)


def _sc_mesh():
    return plsc.VectorSubcoreMesh(core_axis_name="c", subcore_axis_name="s")


def _sc_worker_id():
    return lax.axis_index("s") * (SC_WORKERS // 16) + lax.axis_index("c")


def _sc_dispatch(rows, pos, n_rows):
    n, d = rows.shape
    per_w = n // SC_WORKERS
    w = SC_WINDOW
    n_chunks = per_w // w

    @functools.partial(
        pl.kernel, mesh=_sc_mesh(),
        out_type=jax.ShapeDtypeStruct((n_rows, d), rows.dtype),
        scratch_types=[pltpu.VMEM((2, TOP_K, w), I32), pltpu.VMEM((2, w, d), rows.dtype),
                       pltpu.SemaphoreType.DMA((2,)), pltpu.SemaphoreType.DMA((2,)), pltpu.SemaphoreType.DMA((2,))],
        name="sc_dispatch")
    def k(rows_hbm, pos_hbm, o_hbm, idx_v, rows_v, row_sem, idx_sem, out_sem):
        wid = _sc_worker_id()
        base = wid * per_w

        def loads(c, slot):
            off = pl.multiple_of(base + c * w, SUBLANES)
            return (pltpu.make_async_copy(rows_hbm.at[pl.ds(off, w)], rows_v.at[slot], row_sem.at[slot]),
                    pltpu.make_async_copy(pos_hbm.at[wid * n_chunks + c], idx_v.at[slot], idx_sem.at[slot]))

        def scatters(slot):
            return [pltpu.make_async_copy(rows_v.at[slot], o_hbm.at[idx_v.at[slot, kk]], out_sem.at[slot])
                    for kk in range(TOP_K)]

        for cp in loads(0, 0):
            cp.start()
        for c in range(n_chunks):
            slot = c % 2
            for cp in loads(c, slot):
                cp.wait()
            for cp in scatters(slot):
                cp.start()
            if c >= 1:
                for cp in scatters(1 - slot):
                    cp.wait()
            if c + 1 < n_chunks:
                for cp in loads(c + 1, 1 - slot):
                    cp.start()
        for cp in scatters((n_chunks - 1) % 2):
            cp.wait()

    pos_chunks = pos.reshape(TOP_K, n // w, w).transpose(1, 0, 2)
    return k(rows, pos_chunks)


def _sc_collect(rows, pos_flat):
    d = rows.shape[1]
    total = pos_flat.shape[0]
    per_w = total // SC_WORKERS
    w = SC_WINDOW
    n_pairs = per_w // (2 * w)

    @functools.partial(
        pl.kernel, mesh=_sc_mesh(),
        out_type=jax.ShapeDtypeStruct((total, d), rows.dtype),
        scratch_types=[pltpu.VMEM((per_w,), I32), pltpu.VMEM((2, w, d), rows.dtype),
                       pltpu.SemaphoreType.DMA((2,)), pltpu.SemaphoreType.DMA((2,))],
        name="sc_collect")
    def k(rows_hbm, pos_hbm, o_hbm, idx_v, rows_v, in_sem, out_sem):
        base = pl.multiple_of(_sc_worker_id() * per_w, SUBLANES)
        pltpu.sync_copy(pos_hbm.at[pl.ds(base, per_w)], idx_v)

        def gather(c, slot):
            idx = idx_v.at[pl.ds(pl.multiple_of(c * w, SUBLANES), w)]
            return pltpu.make_async_copy(rows_hbm.at[idx], rows_v.at[slot], in_sem.at[slot])

        def write(c, slot):
            off = pl.multiple_of(base + c * w, SUBLANES)
            return pltpu.make_async_copy(rows_v.at[slot], o_hbm.at[pl.ds(off, w)], out_sem.at[slot])

        gather(0, 0).start()

        @pl.loop(0, n_pairs)
        def _(p):
            c0 = 2 * p
            gather(c0 + 1, 1).start()
            gather(c0, 0).wait()
            write(c0, 0).start()
            gather(c0 + 1, 1).wait()
            write(c0 + 1, 1).start()
            write(c0, 0).wait()

            @pl.when(p + 1 < n_pairs)
            def _():
                gather(c0 + 2, 0).start()

            write(c0 + 1, 1).wait()

    return k(rows, pos_flat)


def _gmm_kernel(tot_ref, ce_ref, row_ref, val_ref, ord_ref, nxt_ref, x_hbm, wg_hbm, wu_hbm, wd_hbm, y_hbm,
                wgb, wub, wdb, xbuf, xsem, ybuf, ysem, wgf, wuf, wdf, wsem):
    total = tot_ref[0]
    pieces = GMM_SUB // GMM_TAIL

    def w_copies(ex, slot):
        return [pltpu.make_async_copy(src.at[ex], dst.at[slot], wsem.at[slot, i])
                for i, (src, dst) in enumerate(((wg_hbm, wgf), (wu_hbm, wuf), (wd_hbm, wdf)))]

    def x_copy(g):
        slot = g % GMM_X_SLOTS
        rows = pl.ds(pl.multiple_of(row_ref[g], GMM_TAIL), GMM_SUB)
        return pltpu.make_async_copy(x_hbm.at[rows], xbuf.at[slot], xsem.at[slot])

    def y_piece(g, p):
        slot = g % 2
        rows = pl.ds(pl.multiple_of(row_ref[g] + p * GMM_TAIL, GMM_TAIL), GMM_TAIL)
        return pltpu.make_async_copy(ybuf.at[slot, pl.ds(p * GMM_TAIL, GMM_TAIL)], y_hbm.at[rows], ysem.at[slot])

    def for_y_pieces(g, action):
        for p in range(pieces):
            @pl.when(p * GMM_TAIL < val_ref[g])
            def _():
                action(y_piece(g, p))

    for ahead in range(GMM_X_SLOTS - 1):
        @pl.when(ahead < total)
        def _():
            x_copy(ahead).start()

    def chunk(g, carry):
        e = ce_ref[g]
        prev = ce_ref[jnp.maximum(g - 1, 0)]

        @pl.when((g == 0) | (e != prev))
        def _():
            slot = ord_ref[e] % 2

            @pl.when(g == 0)
            def _():
                for cp in w_copies(e, slot):
                    cp.start()

            for cp in w_copies(e, slot):
                cp.wait()
            wgb[...] = wgf[slot].astype(BF16)
            wub[...] = wuf[slot].astype(BF16)
            wdb[...] = wdf[slot].astype(BF16)
            nxt = nxt_ref[e]

            @pl.when(nxt >= 0)
            def _():
                for cp in w_copies(nxt, 1 - slot):
                    cp.start()

        x_copy(g).wait()

        @pl.when(g + GMM_X_SLOTS - 1 < total)
        def _():
            x_copy(g + GMM_X_SLOTS - 1).start()

        @pl.when(g >= 2)
        def _():
            for_y_pieces(g - 2, lambda cp: cp.wait())

        x_ref = xbuf.at[g % GMM_X_SLOTS]
        y_ref = ybuf.at[g % 2]

        def expert_rows(r0, n):
            rows = pl.ds(r0, n)
            lo, hi = _unpack_bf16_pairs(x_ref[rows, :])
            xb = jnp.concatenate([lo.astype(BF16), hi.astype(BF16)], axis=1)
            mid = (_silu(_dot(xb, wgb[...])) * _dot(xb, wub[...])).astype(BF16)
            y_ref[rows, :] = _pack_bf16_pairs(_dot(mid, wdb[...]))

        n_real = val_ref[g]

        @pl.when(n_real == GMM_SUB)
        def _():
            expert_rows(0, GMM_SUB)

        @pl.when(n_real < GMM_SUB)
        def _():
            @pl.loop(0, (n_real + GMM_TAIL - 1) // GMM_TAIL)
            def _(i):
                expert_rows(pl.multiple_of(i * GMM_TAIL, GMM_TAIL), GMM_TAIL)

        for_y_pieces(g, lambda cp: cp.start())
        return carry

    lax.fori_loop(0, total, chunk, 0)
    for back in (2, 1):
        @pl.when(total >= back)
        def _():
            for_y_pieces(total - back, lambda cp: cp.wait())


def _gmm(x_sorted, n_chunks, chunk_e, chunk_row, chunk_valid, e_ord, e_next, wg, wu, wd):
    r, half = x_sorted.shape
    d = 2 * half
    any_spec = pl.BlockSpec(memory_space=pl.ANY)
    return pl.pallas_call(
        _gmm_kernel,
        grid_spec=pltpu.PrefetchScalarGridSpec(
            num_scalar_prefetch=6,
            grid=(1,),
            in_specs=[any_spec, any_spec, any_spec, any_spec],
            out_specs=any_spec,
            scratch_shapes=[pltpu.VMEM((d, D_EXPERT), BF16), pltpu.VMEM((d, D_EXPERT), BF16),
                            pltpu.VMEM((D_EXPERT, d), BF16),
                            pltpu.VMEM((GMM_X_SLOTS, GMM_SUB, half), I32), pltpu.SemaphoreType.DMA((GMM_X_SLOTS,)),
                            pltpu.VMEM((2, GMM_SUB, half), I32), pltpu.SemaphoreType.DMA((2,)),
                            pltpu.VMEM((2, d, D_EXPERT), F32), pltpu.VMEM((2, d, D_EXPERT), F32),
                            pltpu.VMEM((2, D_EXPERT, d), F32), pltpu.SemaphoreType.DMA((2, 3))]),
        out_shape=jax.ShapeDtypeStruct((r, half), I32),
        compiler_params=pltpu.CompilerParams(dimension_semantics=("arbitrary",), vmem_limit_bytes=VMEM_LIMIT),
        name="expert_gmm",
    )(n_chunks, chunk_e, chunk_row, chunk_valid, e_ord, e_next, x_sorted, wg, wu, wd)


def _combine_kernel(base_ref, mod_ref, g_ref, w_ref, gain_ref, *rest):
    y_ref = rest[-1]
    nc, c, d = base_ref.shape
    w = w_ref[...]
    acc_lo = acc_hi = None
    for k in range(TOP_K):
        lo, hi = _unpack_bf16_pairs(g_ref[k])
        wk = w[:, k:k + 1]
        acc_lo = wk * lo if k == 0 else acc_lo + wk * lo
        acc_hi = wk * hi if k == 0 else acc_hi + wk * hi
    acc = jnp.concatenate([acc_lo, acc_hi], axis=1)
    g2 = mod_ref[...][:, :, 2 * d:3 * d]
    out = base_ref[...] + g2 * acc.reshape(nc, c, d)
    y_ref[...] = _rms(out) * gain_ref[...]


def _combine(base, mod, gathered, w_tok, gain, first_chunk, n_chunks, out_chunks, out_first_chunk, out_buf=None):
    _, c, d = base.shape
    nc = COMB_TILE // c
    t = nc * c
    t0, o0 = first_chunk // nc, out_first_chunk // nc
    chunks_per_stream = out_chunks // mod.shape[0]
    if chunks_per_stream == 1:
        mod_spec = pl.BlockSpec((nc, 1, 3 * d), lambda i: (o0 + i, 0, 0))
    else:
        assert chunks_per_stream % nc == 0
        mod_spec = pl.BlockSpec((1, 1, 3 * d), lambda i: ((o0 + i) * nc // chunks_per_stream, 0, 0))
    blk3 = pl.BlockSpec((nc, c, d), lambda i: (t0 + i, 0, 0))
    in_specs = [blk3, mod_spec,
                pl.BlockSpec((TOP_K, t, d // 2), lambda i: (0, t0 + i, 0)),
                pl.BlockSpec((t, TOP_K), lambda i: (t0 + i, 0)),
                pl.BlockSpec((1, 1, d), lambda i: (0, 0, 0))]
    args = [base, mod, gathered, w_tok, gain.reshape(1, 1, d)]
    aliases = {}
    if out_buf is not None:
        in_specs.append(pl.BlockSpec(memory_space=pl.ANY))
        args.append(out_buf)
        aliases = {len(args) - 1: 0}
    return pl.pallas_call(
        _combine_kernel,
        grid=(n_chunks // nc,),
        in_specs=in_specs,
        out_specs=pl.BlockSpec((nc, c, d), lambda i: (o0 + i, 0, 0)),
        out_shape=jax.ShapeDtypeStruct((out_chunks, c, d), F32),
        input_output_aliases=aliases,
        compiler_params=pltpu.CompilerParams(dimension_semantics=("arbitrary",), vmem_limit_bytes=VMEM_LIMIT),
        name="combine_norm",
    )(*args)


def _rope_tables(pos):
    half = HEAD_DIM // 2
    inv_freq = ROPE_THETA ** (-jnp.arange(half, dtype=F32) / half)
    ang = pos.astype(F32)[:, None] * inv_freq[None, :]
    cos, sin = jnp.cos(ang), jnp.sin(ang)
    reps = LANES // HEAD_DIM
    return jnp.tile(jnp.concatenate([cos, cos], axis=1), (1, reps)), jnp.tile(jnp.concatenate([-sin, sin], axis=1), (1, reps))


def _routed_ffn(h2, cw, w_gate, w_up, w_down):
    n, half = h2.shape
    rank, counts = _rank(cw)
    counts = counts[:, 0]
    padded = (counts + GMM_TAIL - 1) // GMM_TAIL * GMM_TAIL
    seg_start = (jnp.cumsum(padded) - padded).astype(I32)
    n_rows = n * TOP_K + N_EXPERTS * GMM_TAIL + GMM_SUB
    e_chunks = (counts + GMM_SUB - 1) // GMM_SUB
    chunk_end = jnp.cumsum(e_chunks)
    max_chunks = n * TOP_K // GMM_SUB + N_EXPERTS
    g = jnp.arange(max_chunks, dtype=I32)
    chunk_e = jnp.minimum(jnp.sum((chunk_end[None, :] <= g[:, None]).astype(I32), axis=1), N_EXPERTS - 1)
    eids = jnp.arange(N_EXPERTS, dtype=I32)
    own = chunk_e[:, None] == eids[None, :]
    pick = lambda table: jnp.sum(jnp.where(own, table[None, :], 0), axis=1)
    in_expert = (g - pick(chunk_end - e_chunks)) * GMM_SUB
    chunk_row = (pick(seg_start) + in_expert).astype(I32)
    chunk_valid = jnp.clip(pick(counts) - in_expert, 0, GMM_SUB).astype(I32)
    n_chunks = chunk_end[-1:].astype(I32)
    has_rows = counts > 0
    e_ord = (jnp.cumsum(has_rows.astype(I32)) - has_rows.astype(I32)).astype(I32)
    later = has_rows[None, :] & (eids[None, :] > eids[:, None])
    e_next = jnp.min(jnp.where(later, eids[None, :], N_EXPERTS), axis=1)
    e_next = jnp.where(e_next == N_EXPERTS, -1, e_next).astype(I32)
    pos, w_k = _slots(cw, rank, seg_start[:, None])
    x_sorted = _sc_dispatch(h2, pos, n_rows)
    y_sorted = _gmm(x_sorted, n_chunks, chunk_e, chunk_row, chunk_valid, e_ord, e_next, w_gate, w_up, w_down)
    gathered = _sc_collect(y_sorted, pos.reshape(TOP_K * n)).reshape(TOP_K, n, half)
    return gathered, w_k


def kernel(x_prompt, x_sample, cache_k, cache_v, state_conv, c_prompt, c_sample, w_ada, b_ada, w_in, w_conv,
           w_conv_out, w_attn_o, attn_sinks, w_mix_out, w_router, router_bias, w_exp_gate, w_exp_up, w_exp_down,
           w_sh_gate, w_sh_up, w_sh_down, final_gain):
    assert w_ada.shape[0] == 1, "one layer"
    bp, seq, d = x_prompt.shape
    bs, ts, _ = x_sample.shape
    assert ts == CHUNK and seq % MIX_TILE == 0 and bs % SAMPLE_BB == 0

    c_all = jnp.concatenate([c_prompt, c_sample], axis=0)
    pad = (-c_all.shape[0]) % SUBLANES
    mod = _ada(jnp.pad(c_all, ((0, pad), (0, 0))), w_ada[0], b_ada[0])[:bp + bs]
    mod_p, mod_s = mod[:bp, None, :], mod[bp:, None, :]

    head_axes = (N_KV_HEADS // 2, 2, GQA_GROUP, HEAD_DIM)
    w_in_l = w_in[0]
    w_q = w_in_l[:, OFF_Q:OFF_K].reshape((d,) + head_axes).transpose(0, 1, 3, 2, 4).reshape(d, Q_DIM)
    w_o = w_attn_o[0].reshape(head_axes + (d,)).transpose(0, 2, 1, 3, 4).reshape(Q_DIM, d)
    win = tuple(w.astype(BF16) for w in (w_in_l[:, :OFF_Q], w_q, w_in_l[:, OFF_K:]))
    wco, wao, wmo = (w.astype(BF16) for w in (w_conv_out[0], w_o, w_mix_out[0]))
    cos_p, sin_p = _rope_tables(jnp.arange(seq, dtype=I32))
    cos_s, sin_s = _rope_tables(PAST_LEN + jnp.arange(ts, dtype=I32))

    x1_p, conv_p, k_p, v_p = _mixer_prompt(x_prompt, mod_p, cos_p, sin_p, win, w_conv[0], wco, wao, attn_sinks[0], wmo)
    x1_s, conv_s, k_s, v_s = _mixer_sample(
        x_sample, mod_s, cos_s, sin_s, cache_k[0].reshape(bs, WINDOW, KV_DIM), cache_v[0].reshape(bs, WINDOW, KV_DIM),
        state_conv[0], win, w_conv[0], wco, wao, attn_sinks[0], wmo)

    n_p, n_s = bp * seq, bs * ts
    n = n_p + n_s
    mod2_p, mod2_s = mod[:bp, None, 3 * d:], mod[bp:, None, 3 * d:]
    x1_pc = x1_p.reshape(n_p // CHUNK, CHUNK, d)
    wsg, wsu, wsd = (w[0].astype(BF16) for w in (w_sh_gate, w_sh_up, w_sh_down))
    wr_t, rb = w_router[0].T, router_bias[0][:, None]

    ncp, ncs = n_p // CHUNK, n_s // CHUNK
    half = (ncp + ncs) // 2
    tile_chunks = max(PRE_TILE, COMB_TILE) // CHUNK
    assert half <= ncp and half % tile_chunks == 0 and (ncp - half) % tile_chunks == 0 and ncs % tile_chunks == 0
    assert (half * CHUNK) % (SC_WORKERS * SC_WINDOW) == 0 and (half * CHUNK) % RANK_TILE == 0
    y_p = None
    for p0, np_c, ns_c in ((0, half, 0), (half, ncp - half, ncs)):
        h2, base, cw = _pre(x1_pc, p0, np_c, mod2_p, x1_s, ns_c, mod2_s, wsg, wsu, wsd, wr_t, rb)
        gathered, w_tok = _routed_ffn(h2, cw, w_exp_gate[0], w_exp_up[0], w_exp_down[0])
        y_p = _combine(base, mod2_p, gathered, w_tok, final_gain, 0, np_c, ncp, p0, out_buf=y_p)
        if ns_c:
            y_s = _combine(base, mod2_s, gathered, w_tok, final_gain, np_c, ns_c, ncs, 0)

    kv = lambda a: a.reshape(1, a.shape[0], WINDOW, N_KV_HEADS, HEAD_DIM)
    return (y_p.reshape(bp, seq, d), y_s, conv_p[None], kv(k_p), kv(v_p), conv_s[None], kv(k_s), kv(v_s))
```

```python
import functools

import jax
import jax.numpy as jnp
from jax import lax
from jax.experimental import pallas as pl
from jax.experimental.pallas import tpu as pltpu
from jax.experimental.pallas import tpu_sc as plsc

F32 = jnp.float32
BF16 = jnp.bfloat16
I32 = jnp.int32

D_MODEL = 1024
CHUNK = 64
D_CONV = 1024
CONV_W = 3
N_HEADS = 16
N_KV_HEADS = 4
HEAD_DIM = 64
GQA_GROUP = N_HEADS // N_KV_HEADS
WINDOW = 128
ROPE_THETA = 10000.0
ATTN_SCALE = HEAD_DIM ** -0.5
N_EXPERTS = 64
TOP_K = 8
N_EXPERT_GROUPS = 8
GROUP_SIZE = N_EXPERTS // N_EXPERT_GROUPS
TOPK_GROUPS = 4
D_EXPERT = 256
D_SHARED = 256
ROUTED_SCALE = 2.5
EPS = 1e-6
PAST_LEN = 4096
Q_DIM = N_HEADS * HEAD_DIM
KV_DIM = N_KV_HEADS * HEAD_DIM
OFF_GB, OFF_GC, OFF_XC, OFF_Q, OFF_K, OFF_V, OFF_GCONV, OFF_GATTN, D_IN = (
    0, 1024, 2048, 3072, 4096, 4352, 4608, 5632, 6656)

LANES = 128
SUBLANES = 8
VMEM_LIMIT = 56 * 1024 * 1024

MIX_TILE = 512
ATT_Q = 128
MIX_SIDE_COLS = 256
SAMPLE_BB = 8
PRE_TILE = 512
RANK_TILE = 512
GMM_SUB = 512
GMM_TAIL = 128
GMM_X_SLOTS = 3
COMB_TILE = 256
SC_WORKERS = 32
SC_WINDOW = 96
SC_LANES = 16
SC_SUM_TOKENS = 8
SC_SUM_VREGS = 16


def _const_spec(shape):
    nd = len(shape)
    return pl.BlockSpec(shape, lambda *_: (0,) * nd, pipeline_mode=pl.Buffered(1))


def _rms(x):
    return x * lax.rsqrt(jnp.mean(x * x, axis=-1, keepdims=True) + EPS)


def _sigmoid(x):
    return 1.0 / (1.0 + jnp.exp(-x))


def _silu(x):
    return x * _sigmoid(x)


def _dot(a, b):
    return jnp.dot(a, b, preferred_element_type=F32)


def _pack_bf16_pairs(x):
    half = x.shape[-1] // 2
    lo = lax.bitcast_convert_type(x[..., :half].astype(BF16).astype(F32), I32)
    hi = lax.bitcast_convert_type(x[..., half:].astype(BF16).astype(F32), I32)
    return lax.shift_right_logical(lo, 16) | hi


def _unpack_bf16_pairs(words):
    lo = lax.bitcast_convert_type(lax.shift_left(words, 16), F32)
    hi = lax.bitcast_convert_type(words & jnp.int32(-65536), F32)
    return lo, hi


def _ada_kernel(c_ref, w_ref, b_ref, o_ref):
    s = _silu(c_ref[...]).astype(BF16)
    o_ref[...] = _dot(s, w_ref[...].astype(BF16)) + b_ref[...]


def _ada(c_all, w_ada, b_ada):
    rows = c_all.shape[0]
    n_out = w_ada.shape[1]
    bn = 768
    return pl.pallas_call(
        _ada_kernel,
        grid=(n_out // bn,),
        in_specs=[pl.BlockSpec((rows, D_MODEL), lambda i: (0, 0)),
                  pl.BlockSpec((D_MODEL, bn), lambda i: (0, i)),
                  pl.BlockSpec((1, bn), lambda i: (0, i))],
        out_specs=pl.BlockSpec((rows, bn), lambda i: (0, i)),
        out_shape=jax.ShapeDtypeStruct((rows, n_out), F32),
        name="ada_mod",
    )(c_all, w_ada, b_ada.reshape(1, n_out))


def _rope(x, cos, sin_signed):
    lane = lax.broadcasted_iota(I32, (x.shape[0], LANES), 1)
    first_half = (lane % HEAD_DIM) < (HEAD_DIM // 2)
    outs = []
    for g in range(x.shape[1] // LANES):
        xg = x[:, g * LANES:(g + 1) * LANES]
        up = pltpu.roll(xg, LANES - HEAD_DIM // 2, axis=1)
        down = pltpu.roll(xg, HEAD_DIM // 2, axis=1)
        partner = jnp.where(first_half, up, down)
        outs.append(xg * cos + partner * sin_signed)
    return jnp.concatenate(outs, axis=1)


def _attention(blocks, sinks_ref, obuf, between):
    rq = GQA_GROUP * ATT_Q
    low = lax.broadcasted_iota(I32, (ATT_Q, LANES), 1) < HEAD_DIM
    head_of_lane = lax.broadcasted_iota(I32, (1, rq), 1) // ATT_Q
    units = [(b, pair, par) for b in range(len(blocks)) for pair in range(N_KV_HEADS // 2) for par in range(2)]
    loaded = {}

    def scores(u):
        b, pair, par = units[u]
        if b not in loaded:
            loaded.clear()
            loaded[b] = blocks[b]()
        q_blk, k_of_pair, _, mask, _ = loaded[b]
        keep = low if par == 0 else jnp.logical_not(low)
        cols = [q_blk[:, (GQA_GROUP * pair + i) * LANES:(GQA_GROUP * pair + i + 1) * LANES] for i in range(GQA_GROUP)]
        qg = jnp.concatenate([jnp.where(keep, c, jnp.zeros_like(c)) for c in cols], axis=0)
        st = lax.dot_general(k_of_pair(pair), qg, (((1,), (1,)), ((), ())), preferred_element_type=F32)
        vt = loaded[b][2](pair)[par * HEAD_DIM:(par + 1) * HEAD_DIM, :]
        return jnp.where(mask, st, -jnp.inf), vt, loaded[b][4]

    outs = []
    nxt = scores(0)
    for u, (b, pair, par) in enumerate(units):
        st, vt, row0 = nxt
        if u + 1 < len(units):
            nxt = scores(u + 1)
        if between:
            between.pop(0)()
        g = 2 * pair + par
        sink = jnp.full((1, rq), sinks_ref[g * GQA_GROUP + GQA_GROUP - 1], F32)
        for i in range(GQA_GROUP - 2, -1, -1):
            sink = jnp.where(head_of_lane == i, sinks_ref[g * GQA_GROUP + i], sink)
        m = jnp.maximum(jnp.max(st, axis=0, keepdims=True), sink)
        e = jnp.exp(st - m)
        z = jnp.sum(e, axis=0, keepdims=True) + jnp.exp(sink - m)
        outs.append(_dot(vt, e.astype(BF16)) / z)
        if par == 1:
            for i in range(GQA_GROUP):
                blk = jnp.concatenate([o[:, i * ATT_Q:(i + 1) * ATT_Q] for o in outs], axis=0)
                c0 = (GQA_GROUP * pair + i) * LANES
                obuf[row0:row0 + ATT_Q, c0:c0 + LANES] = blk.T
            outs = []
    for step in between:
        step()


def _in_proj(hb, win_refs, lo, hi):
    w_pre, w_q, w_post = win_refs
    if hi <= OFF_Q:
        return _dot(hb, w_pre[:, lo:hi])
    if lo >= OFF_K:
        return _dot(hb, w_post[:, lo - OFF_K:hi - OFF_K])
    assert (lo, hi) == (OFF_Q, OFF_K)
    return _dot(hb, w_q[...])


def _attention_free_steps(hb, conv, win_ref, wco_ref):
    out, parts = {}, {}
    n_parts = D_MODEL // MIX_SIDE_COLS

    def step(name, piece, compute):
        def run():
            parts.setdefault(name, []).append(compute(piece * MIX_SIDE_COLS, (piece + 1) * MIX_SIDE_COLS))
            if piece == n_parts - 1:
                out[name] = jnp.concatenate(parts.pop(name), axis=1)
        return run

    def conv_in():
        if "conv_in" not in out:
            out["conv_in"] = (out.pop("gate_b") * conv).astype(BF16)
        return out["conv_in"]

    computes = [("gate_b", lambda lo, hi: _in_proj(hb, win_ref, OFF_GB + lo, OFF_GB + hi)),
                ("g_conv", lambda lo, hi: _in_proj(hb, win_ref, OFF_GCONV + lo, OFF_GCONV + hi)),
                ("g_attn", lambda lo, hi: _in_proj(hb, win_ref, OFF_GATTN + lo, OFF_GATTN + hi)),
                ("y_conv", lambda lo, hi: _dot(conv_in(), wco_ref[:, lo:hi]))]
    return [step(name, p, fn) for name, fn in computes for p in range(n_parts)], out


def _mix_out(x, g1, side, y_attn_in, wao_ref, wmo_ref):
    y_attn = _dot(y_attn_in.astype(BF16), wao_ref[...])
    merged = _sigmoid(side["g_conv"]) * side["y_conv"] + _sigmoid(side["g_attn"]) * y_attn
    return x + g1 * _dot(merged.astype(BF16), wmo_ref[...])


def _mixer_prompt_kernel(x_ref, mod_ref, cos_ref, sin_ref, wpre_ref, wq_ref, wpost_ref, wconv_ref, wco_ref, wao_ref,
                         sinks_ref, wmo_ref, x1_ref, conv_ref, k_ref, v_ref, ubuf, kbuf, vtbuf, obuf):
    win_ref = (wpre_ref, wq_ref, wpost_ref)
    j = pl.program_id(1)
    t = x_ref.shape[1]

    @pl.when(j == 0)
    def _():
        ubuf[0:SUBLANES, :] = jnp.zeros((SUBLANES, D_CONV), F32)
        kbuf[0:WINDOW, :] = jnp.zeros((WINDOW, KV_DIM), BF16)
        vtbuf[:, 0:WINDOW] = jnp.zeros((KV_DIM, WINDOW), BF16)

    x = x_ref[0]
    mod = mod_ref[0]
    sh1, sc1, g1 = mod[:, 0:D_MODEL], mod[:, D_MODEL:2 * D_MODEL], mod[:, 2 * D_MODEL:3 * D_MODEL]
    hb = (_rms(x) * (1.0 + sc1) + sh1).astype(BF16)

    u = _in_proj(hb, win_ref, OFF_GC, OFF_XC) * _in_proj(hb, win_ref, OFF_XC, OFF_Q)
    ubuf[SUBLANES:SUBLANES + t, :] = u
    wc = wconv_ref[...]
    conv = wc[0:1] * ubuf[SUBLANES - 2:SUBLANES - 2 + t, :] + wc[1:2] * ubuf[SUBLANES - 1:SUBLANES - 1 + t, :] + wc[2:3] * u
    conv_ref[0] = u[t - (CONV_W - 1):t]
    ubuf[SUBLANES - 2:SUBLANES, :] = u[t - (CONV_W - 1):t]

    cos, sin = cos_ref[...], sin_ref[...]
    q = (_rope(_in_proj(hb, win_ref, OFF_Q, OFF_K), cos, sin) * ATTN_SCALE).astype(BF16)
    k = _rope(_in_proj(hb, win_ref, OFF_K, OFF_V), cos, sin)
    v = _in_proj(hb, win_ref, OFF_V, OFF_GCONV)
    kbuf[WINDOW:WINDOW + t, :] = k.astype(BF16)
    vtbuf[:, WINDOW:WINDOW + t] = v.T.astype(BF16)
    k_ref[0] = k[t - WINDOW:t]
    v_ref[0] = v[t - WINDOW:t]

    nkeys = ATT_Q + WINDOW
    rq = GQA_GROUP * ATT_Q
    ki = lax.broadcasted_iota(I32, (nkeys, rq), 0)
    qi = lax.broadcasted_iota(I32, (nkeys, rq), 1) % ATT_Q
    band = ki // CHUNK - qi // CHUNK
    band_ok = (band >= 0) & (band <= WINDOW // CHUNK)
    def block(s):
        def load():
            mask = band_ok & (ki + (j * t + s * ATT_Q - WINDOW) >= 0)
            k_of_pair = lambda pair: kbuf[s * ATT_Q:s * ATT_Q + nkeys, pair * LANES:(pair + 1) * LANES]
            vt_of_pair = lambda pair: vtbuf[pair * LANES:(pair + 1) * LANES, s * ATT_Q:s * ATT_Q + nkeys]
            return q[s * ATT_Q:(s + 1) * ATT_Q], k_of_pair, vt_of_pair, mask, s * ATT_Q
        return load

    steps, side = _attention_free_steps(hb, conv, win_ref, wco_ref)
    _attention([block(s) for s in range(t // ATT_Q)], sinks_ref, obuf, steps)
    kbuf[0:WINDOW, :] = kbuf[t:t + WINDOW, :]
    vtbuf[:, 0:WINDOW] = vtbuf[:, t:t + WINDOW]

    x1_ref[0] = _mix_out(x, g1, side, obuf[...], wao_ref, wmo_ref)


def _mixer_prompt(x, mod, cos, sin, win, wconv, wco, wao, sinks, wmo):
    b, seq, d = x.shape
    t = MIX_TILE
    return pl.pallas_call(
        _mixer_prompt_kernel,
        grid=(b, seq // t),
        in_specs=[pl.BlockSpec((1, t, d), lambda i, j: (i, j, 0)),
                  pl.BlockSpec((1, 1, 6 * d), lambda i, j: (i, 0, 0)),
                  pl.BlockSpec((t, LANES), lambda i, j: (j, 0)),
                  pl.BlockSpec((t, LANES), lambda i, j: (j, 0)),
                  *[_const_spec(w.shape) for w in win],
                  _const_spec(wconv.shape), _const_spec(wco.shape), _const_spec(wao.shape),
                  pl.BlockSpec(memory_space=pltpu.SMEM),
                  _const_spec(wmo.shape)],
        out_specs=[pl.BlockSpec((1, t, d), lambda i, j: (i, j, 0)),
                   pl.BlockSpec((1, CONV_W - 1, D_CONV), lambda i, j: (i, 0, 0)),
                   pl.BlockSpec((1, WINDOW, KV_DIM), lambda i, j: (i, 0, 0)),
                   pl.BlockSpec((1, WINDOW, KV_DIM), lambda i, j: (i, 0, 0))],
        out_shape=[jax.ShapeDtypeStruct((b, seq, d), F32),
                   jax.ShapeDtypeStruct((b, CONV_W - 1, D_CONV), F32),
                   jax.ShapeDtypeStruct((b, WINDOW, KV_DIM), F32),
                   jax.ShapeDtypeStruct((b, WINDOW, KV_DIM), F32)],
        scratch_shapes=[pltpu.VMEM((SUBLANES + t, D_CONV), F32),
                        pltpu.VMEM((WINDOW + t, KV_DIM), BF16),
                        pltpu.VMEM((KV_DIM, WINDOW + t), BF16),
                        pltpu.VMEM((t, Q_DIM), F32)],
        compiler_params=pltpu.CompilerParams(dimension_semantics=("arbitrary", "arbitrary"),
                                             vmem_limit_bytes=VMEM_LIMIT),
        name="mixer_prompt",
    )(x, mod, cos, sin, *win, wconv, wco, wao, sinks, wmo)


def _mixer_sample_kernel(x_ref, mod_ref, cos_ref, sin_ref, ck_ref, cv_ref, sconv_ref, wpre_ref, wq_ref, wpost_ref,
                         wconv_ref, wco_ref, wao_ref, sinks_ref, wmo_ref, x1_ref, conv_ref, k_ref, v_ref, ubuf, obuf):
    win_ref = (wpre_ref, wq_ref, wpost_ref)
    bb, t, d = x_ref.shape
    x3 = x_ref[...]
    mod = mod_ref[...]
    sh1, sc1, g1 = mod[:, :, 0:d], mod[:, :, d:2 * d], mod[:, :, 2 * d:3 * d]
    x = x3.reshape(bb * t, d)
    hb = (_rms(x3) * (1.0 + sc1) + sh1).astype(BF16).reshape(bb * t, d)

    u = _in_proj(hb, win_ref, OFF_GC, OFF_XC) * _in_proj(hb, win_ref, OFF_XC, OFF_Q)
    u3 = u.reshape(bb, t, D_CONV)
    ubuf[:, SUBLANES - 2:SUBLANES, :] = sconv_ref[...]
    ubuf[:, SUBLANES:SUBLANES + t, :] = u3
    wc = wconv_ref[...]
    conv = (wc[0:1] * ubuf[:, SUBLANES - 2:SUBLANES - 2 + t, :] + wc[1:2] * ubuf[:, SUBLANES - 1:SUBLANES - 1 + t, :]
            + wc[2:3] * u3).reshape(bb * t, D_CONV)
    conv_ref[...] = u3[:, t - (CONV_W - 1):t, :]

    cos = jnp.concatenate([cos_ref[...]] * bb, axis=0)
    sin = jnp.concatenate([sin_ref[...]] * bb, axis=0)
    q = (_rope(_in_proj(hb, win_ref, OFF_Q, OFF_K), cos, sin) * ATTN_SCALE).astype(BF16)
    k = _rope(_in_proj(hb, win_ref, OFF_K, OFF_V), cos, sin)
    v = _in_proj(hb, win_ref, OFF_V, OFF_GCONV)
    per = ATT_Q // t
    nkeys = per * (WINDOW + t)
    rq = GQA_GROUP * ATT_Q
    key_stream = lax.broadcasted_iota(I32, (nkeys, rq), 0) // (WINDOW + t)
    query_stream = (lax.broadcasted_iota(I32, (nkeys, rq), 1) % ATT_Q) // t
    mask = key_stream == query_stream
    def block(blk):
        def load():
            k_parts, v_parts = [], []
            for b in range(blk * per, (blk + 1) * per):
                kb, vb = k[b * t:(b + 1) * t], v[b * t:(b + 1) * t]
                ck, cv = ck_ref[b], cv_ref[b]
                k_ref[b] = jnp.concatenate([ck[t:WINDOW], kb], axis=0)
                v_ref[b] = jnp.concatenate([cv[t:WINDOW], vb], axis=0)
                k_parts += [ck, kb]
                v_parts += [cv, vb]
            k_all = jnp.concatenate(k_parts, axis=0).astype(BF16)
            vt_all = jnp.concatenate(v_parts, axis=0).T.astype(BF16)
            k_of_pair = lambda pair: k_all[:, pair * LANES:(pair + 1) * LANES]
            vt_of_pair = lambda pair: vt_all[pair * LANES:(pair + 1) * LANES, :]
            return q[blk * ATT_Q:(blk + 1) * ATT_Q], k_of_pair, vt_of_pair, mask, blk * ATT_Q
        return load

    steps, side = _attention_free_steps(hb, conv, win_ref, wco_ref)
    _attention([block(blk) for blk in range(bb // per)], sinks_ref, obuf, steps)

    g1f = jnp.broadcast_to(g1, (bb, t, d)).reshape(bb * t, d)
    x1_ref[...] = _mix_out(x, g1f, side, obuf[...], wao_ref, wmo_ref).reshape(bb, t, d)


def _mixer_sample(x, mod, cos, sin, ck, cv, sconv, win, wconv, wco, wao, sinks, wmo):
    b, t, d = x.shape
    bb = SAMPLE_BB
    blk = lambda *s: pl.BlockSpec((bb,) + s, lambda i: (i, 0, 0))
    return pl.pallas_call(
        _mixer_sample_kernel,
        grid=(b // bb,),
        in_specs=[blk(t, d), blk(1, 6 * d),
                  pl.BlockSpec((t, LANES), lambda i: (0, 0)), pl.BlockSpec((t, LANES), lambda i: (0, 0)),
                  blk(WINDOW, KV_DIM), blk(WINDOW, KV_DIM), blk(CONV_W - 1, D_CONV),
                  *[_const_spec(w.shape) for w in win],
                  _const_spec(wconv.shape), _const_spec(wco.shape), _const_spec(wao.shape),
                  pl.BlockSpec(memory_space=pltpu.SMEM),
                  _const_spec(wmo.shape)],
        out_specs=[blk(t, d), blk(CONV_W - 1, D_CONV), blk(WINDOW, KV_DIM), blk(WINDOW, KV_DIM)],
        out_shape=[jax.ShapeDtypeStruct((b, t, d), F32),
                   jax.ShapeDtypeStruct((b, CONV_W - 1, D_CONV), F32),
                   jax.ShapeDtypeStruct((b, WINDOW, KV_DIM), F32),
                   jax.ShapeDtypeStruct((b, WINDOW, KV_DIM), F32)],
        scratch_shapes=[pltpu.VMEM((bb, SUBLANES + t, D_CONV), F32),
                        pltpu.VMEM((bb * t, Q_DIM), F32)],
        compiler_params=pltpu.CompilerParams(dimension_semantics=("arbitrary",), vmem_limit_bytes=VMEM_LIMIT),
        name="mixer_sample",
    )(x, mod, cos, sin, ck, cv, sconv, *win, wconv, wco, wao, sinks, wmo)


def _pre_kernel(*refs, prompt_tiles, has_sample):
    if has_sample:
        xp_ref, mp_ref, xs_ref, ms_ref, wsg_ref, wsu_ref, wsd_ref, wr_ref, rb_ref, h2_ref, base_ref, cw_ref = refs
    else:
        xp_ref, mp_ref, wsg_ref, wsu_ref, wsd_ref, wr_ref, rb_ref, h2_ref, base_ref, cw_ref = refs
    nc, c, d = xp_ref.shape
    t = nc * c
    x3, mod = xp_ref[...], mp_ref[...]
    if has_sample:
        is_prompt = pl.program_id(0) < prompt_tiles
        x3 = jnp.where(is_prompt, x3, xs_ref[...])
        mod = jnp.where(is_prompt, mod, ms_ref[...])
    sh2, sc2, g2 = mod[:, :, 0:d], mod[:, :, d:2 * d], mod[:, :, 2 * d:3 * d]
    h3 = _rms(x3) * (1.0 + sc2) + sh2
    h2 = h3.reshape(t, d)
    hb = h2.astype(BF16)
    h2_ref[...] = _pack_bf16_pairs(h2)
    shared = _dot((_silu(_dot(hb, wsg_ref[...])) * _dot(hb, wsu_ref[...])).astype(BF16), wsd_ref[...])
    base_ref[...] = x3 + g2 * shared.reshape(nc, c, d)

    logits = lax.dot_general(wr_ref[...], h2, (((1,), (1,)), ((), ())), preferred_element_type=F32,
                             precision=lax.Precision.HIGHEST)
    scores = _sigmoid(logits)
    biased = scores + rb_ref[...]
    g3 = biased.reshape(N_EXPERT_GROUPS, GROUP_SIZE, t)
    member = lax.broadcasted_iota(I32, g3.shape, 1)
    m1 = jnp.max(g3, axis=1, keepdims=True)
    first = jnp.min(jnp.where(g3 == m1, member, GROUP_SIZE), axis=1, keepdims=True)
    m2 = jnp.max(jnp.where(member == first, -jnp.inf, g3), axis=1, keepdims=True)
    gs = m1 + m2
    gidx = lax.broadcasted_iota(I32, gs.shape, 0)
    grank = jnp.zeros(gs.shape, I32)
    for o in range(N_EXPERT_GROUPS):
        other = gs[o:o + 1]
        grank += ((other > gs) | ((other == gs) & (o < gidx))).astype(I32)
    eligible = jnp.broadcast_to(grank < TOPK_GROUPS, g3.shape).reshape(N_EXPERTS, t)
    mb = jnp.where(eligible, biased, -jnp.inf)
    eidx = lax.broadcasted_iota(I32, mb.shape, 0)
    erank = jnp.zeros(mb.shape, I32)
    for o in range(N_EXPERTS):
        other = mb[o:o + 1]
        erank += ((other > mb) | ((other == mb) & (o < eidx))).astype(I32)
    sel = eligible & (erank < TOP_K)
    ssum = jnp.sum(jnp.where(sel, scores, 0.0), axis=0, keepdims=True)
    cw_ref[...] = jnp.where(sel, scores / ssum * ROUTED_SCALE, -1.0)


def _pre(x1_p, p_chunk0, ncp, mod_p, x1_s, ncs, mod_s, wsg, wsu, wsd, wr_t, rb):
    ncp_all, c, d = x1_p.shape
    nc = PRE_TILE // c
    nchunks = ncp + ncs
    n = nchunks * c
    pt, p0 = ncp // nc, p_chunk0 // nc
    tiles_per_stream = ncp_all // mod_p.shape[0] // nc
    blk3 = pl.BlockSpec((nc, c, d), lambda i: (i, 0, 0))
    p_tile = lambda i: p0 + jnp.minimum(i, pt - 1)
    s_tile = lambda i: jnp.maximum(i - pt, 0)
    s_args, s_specs = [], []
    if ncs:
        s_args = [x1_s, mod_s]
        s_specs = [pl.BlockSpec((nc, c, d), lambda i: (s_tile(i), 0, 0)),
                   pl.BlockSpec((nc, 1, 3 * d), lambda i: (s_tile(i), 0, 0))]
    return pl.pallas_call(
        functools.partial(_pre_kernel, prompt_tiles=pt, has_sample=bool(ncs)),
        grid=(nchunks // nc,),
        in_specs=[pl.BlockSpec((nc, c, d), lambda i: (p_tile(i), 0, 0)),
                  pl.BlockSpec((1, 1, 3 * d), lambda i: (p_tile(i) // tiles_per_stream, 0, 0))] + s_specs + [
                  _const_spec(wsg.shape), _const_spec(wsu.shape), _const_spec(wsd.shape),
                  _const_spec(wr_t.shape), _const_spec(rb.shape)],
        out_specs=[pl.BlockSpec((nc * c, d // 2), lambda i: (i, 0)), blk3,
                   pl.BlockSpec((N_EXPERTS, nc * c), lambda i: (0, i))],
        out_shape=[jax.ShapeDtypeStruct((n, d // 2), I32),
                   jax.ShapeDtypeStruct((nchunks, c, d), F32),
                   jax.ShapeDtypeStruct((N_EXPERTS, n), F32)],
        compiler_params=pltpu.CompilerParams(dimension_semantics=("arbitrary",), vmem_limit_bytes=VMEM_LIMIT),
        name="pre_ffn",
    )(x1_p, mod_p, *s_args, wsg, wsu, wsd, wr_t, rb)


def _rank_kernel(cw_ref, rank_ref, cnt_ref, carry):
    i = pl.program_id(0)
    t = cw_ref.shape[1]

    @pl.when(i == 0)
    def _():
        carry[...] = jnp.zeros(carry.shape, F32)

    sel = (cw_ref[...] >= 0.0).astype(BF16)
    r = lax.broadcasted_iota(I32, (t, t), 0)
    c = lax.broadcasted_iota(I32, (t, t), 1)
    before = (r < c).astype(BF16)
    rank = carry[...] + _dot(sel, before)
    rank_ref[...] = rank.astype(I32)
    carry[...] = carry[...] + jnp.sum(sel.astype(F32), axis=1, keepdims=True)
    cnt_ref[...] = carry[...].astype(I32)


def _rank(cw):
    e, n = cw.shape
    t = RANK_TILE
    return pl.pallas_call(
        _rank_kernel,
        grid=(n // t,),
        in_specs=[pl.BlockSpec((e, t), lambda i: (0, i))],
        out_specs=[pl.BlockSpec((e, t), lambda i: (0, i)), pl.BlockSpec((e, 1), lambda i: (0, 0))],
        out_shape=[jax.ShapeDtypeStruct((e, n), I32), jax.ShapeDtypeStruct((e, 1), I32)],
        scratch_shapes=[pltpu.VMEM((e, 1), F32)],
        compiler_params=pltpu.CompilerParams(dimension_semantics=("arbitrary",)),
        name="expert_rank",
    )(cw)


def _slot_kernel(cw_ref, rank_ref, start_ref, pos_ref, w_ref, pos_tok_ref):
    cw = cw_ref[...]
    e, t = cw.shape
    sel = cw >= 0.0
    r = lax.broadcasted_iota(I32, (e, e), 0)
    c = lax.broadcasted_iota(I32, (e, e), 1)
    lower = (c < r).astype(BF16)
    kidx = _dot(lower, sel.astype(BF16))
    posf = start_ref[...].astype(F32) + rank_ref[...].astype(F32)
    pos_rows, w_rows = [], []
    for k in range(TOP_K):
        m = sel & (kidx == float(k))
        pos_rows.append(jnp.sum(jnp.where(m, posf, 0.0), axis=0, keepdims=True))
        w_rows.append(jnp.sum(jnp.where(m, cw, 0.0), axis=0, keepdims=True))
    pos_ref[...] = jnp.concatenate(pos_rows, axis=0).astype(I32)
    stack = jnp.concatenate(w_rows + pos_rows + [jnp.zeros((LANES - 2 * TOP_K, t), F32)], axis=0).T
    w_ref[...] = stack[:, :TOP_K]
    pos_tok_ref[...] = stack[:, TOP_K:2 * TOP_K].astype(I32)


def _slots(cw, rank, seg_start):
    e, n = cw.shape
    t = RANK_TILE
    return pl.pallas_call(
        _slot_kernel,
        grid=(n // t,),
        in_specs=[pl.BlockSpec((e, t), lambda i: (0, i)), pl.BlockSpec((e, t), lambda i: (0, i)),
                  pl.BlockSpec((e, 1), lambda i: (0, 0))],
        out_specs=[pl.BlockSpec((TOP_K, t), lambda i: (0, i)), pl.BlockSpec((t, TOP_K), lambda i: (i, 0)),
                   pl.BlockSpec((t, TOP_K), lambda i: (i, 0))],
        out_shape=[jax.ShapeDtypeStruct((TOP_K, n), I32), jax.ShapeDtypeStruct((n, TOP_K), F32),
                   jax.ShapeDtypeStruct((n, TOP_K), I32)],
        compiler_params=pltpu.CompilerParams(dimension_semantics=("arbitrary",)),
        name="expert_slots",
    )(cw, rank, seg_start)


def _sc_mesh():
    return plsc.VectorSubcoreMesh(core_axis_name="c", subcore_axis_name="s")


def _sc_worker_id():
    return lax.axis_index("s") * (SC_WORKERS // 16) + lax.axis_index("c")


def _sc_dispatch(rows, pos, n_rows):
    n, d = rows.shape
    per_w = n // SC_WORKERS
    w = SC_WINDOW
    n_chunks = per_w // w

    @functools.partial(
        pl.kernel, mesh=_sc_mesh(),
        out_type=jax.ShapeDtypeStruct((n_rows, d), rows.dtype),
        scratch_types=[pltpu.VMEM((2, TOP_K, w), I32), pltpu.VMEM((2, w, d), rows.dtype),
                       pltpu.SemaphoreType.DMA((2,)), pltpu.SemaphoreType.DMA((2,)), pltpu.SemaphoreType.DMA((2,))],
        name="sc_dispatch")
    def k(rows_hbm, pos_hbm, o_hbm, idx_v, rows_v, row_sem, idx_sem, out_sem):
        wid = _sc_worker_id()
        base = wid * per_w

        def loads(c, slot):
            off = pl.multiple_of(base + c * w, SUBLANES)
            return (pltpu.make_async_copy(rows_hbm.at[pl.ds(off, w)], rows_v.at[slot], row_sem.at[slot]),
                    pltpu.make_async_copy(pos_hbm.at[wid * n_chunks + c], idx_v.at[slot], idx_sem.at[slot]))

        def scatters(slot):
            return [pltpu.make_async_copy(rows_v.at[slot], o_hbm.at[idx_v.at[slot, kk]], out_sem.at[slot])
                    for kk in range(TOP_K)]

        for cp in loads(0, 0):
            cp.start()
        for c in range(n_chunks):
            slot = c % 2
            for cp in loads(c, slot):
                cp.wait()
            for cp in scatters(slot):
                cp.start()
            if c >= 1:
                for cp in scatters(1 - slot):
                    cp.wait()
            if c + 1 < n_chunks:
                for cp in loads(c + 1, 1 - slot):
                    cp.start()
        for cp in scatters((n_chunks - 1) % 2):
            cp.wait()

    pos_chunks = pos.reshape(TOP_K, n // w, w).transpose(1, 0, 2)
    return k(rows, pos_chunks)


def _sc_collect_sum(rows, pos_tok, w_lanes):
    words = rows.shape[1]
    n = w_lanes.shape[0]
    lanes = SC_LANES
    per_w = n // SC_WORKERS
    tw = SC_SUM_TOKENS
    n_pairs = per_w // (2 * tw)
    col_blocks = words // lanes // SC_SUM_VREGS

    @functools.partial(
        pl.kernel, mesh=_sc_mesh(),
        out_type=jax.ShapeDtypeStruct((n, 2 * words), F32),
        scratch_types=[pltpu.VMEM((per_w * TOP_K,), I32), pltpu.VMEM((2, tw * TOP_K, words), I32),
                       pltpu.VMEM((2, tw, TOP_K * lanes), F32), pltpu.VMEM((2, tw, 2 * words), F32),
                       pltpu.SemaphoreType.DMA((2,)), pltpu.SemaphoreType.DMA((2,)), pltpu.SemaphoreType.DMA((2,))],
        compiler_params=pltpu.CompilerParams(needs_layout_passes=False),
        name="sc_collect_sum")
    def k(rows_hbm, pos_hbm, w_hbm, o_hbm, idx_v, rows_v, w_v, out_v, in_sem, w_sem, out_sem):
        base = pl.multiple_of(_sc_worker_id() * per_w, SUBLANES)
        pltpu.sync_copy(pos_hbm.at[pl.ds(pl.multiple_of(base * TOP_K, SUBLANES), per_w * TOP_K)], idx_v)

        def loads(c, slot):
            idx = idx_v.at[pl.ds(pl.multiple_of(c * tw * TOP_K, SUBLANES), tw * TOP_K)]
            tok0 = pl.multiple_of(base + c * tw, SUBLANES)
            return (pltpu.make_async_copy(rows_hbm.at[idx], rows_v.at[slot], in_sem.at[slot]),
                    pltpu.make_async_copy(w_hbm.at[pl.ds(tok0, tw)], w_v.at[slot], w_sem.at[slot]))

        def write(c, slot):
            tok0 = pl.multiple_of(base + c * tw, SUBLANES)
            return pltpu.make_async_copy(out_v.at[slot], o_hbm.at[pl.ds(tok0, tw)], out_sem.at[slot])

        high_half = jnp.full((lanes,), -65536, I32)
        sixteen = jnp.full((lanes,), 16, I32)

        def reduce_window(slot):
            rv, wv, ov = rows_v.at[slot], w_v.at[slot], out_v.at[slot]

            @pl.loop(0, tw)
            def _(t):
                for cb in range(col_blocks):
                    acc_lo, acc_hi = [None] * SC_SUM_VREGS, [None] * SC_SUM_VREGS
                    for kk in range(TOP_K):
                        wk = wv[t, pl.ds(kk * lanes, lanes)]
                        for c in range(SC_SUM_VREGS):
                            wd = rv[t * TOP_K + kk, pl.ds((cb * SC_SUM_VREGS + c) * lanes, lanes)]
                            lo = wk * plsc.bitcast(lax.shift_left(wd, sixteen), F32)
                            hi = wk * plsc.bitcast(wd & high_half, F32)
                            acc_lo[c] = lo if kk == 0 else acc_lo[c] + lo
                            acc_hi[c] = hi if kk == 0 else acc_hi[c] + hi
                    for c in range(SC_SUM_VREGS):
                        col = (cb * SC_SUM_VREGS + c) * lanes
                        ov[t, pl.ds(col, lanes)] = acc_lo[c]
                        ov[t, pl.ds(words + col, lanes)] = acc_hi[c]

        for cp in loads(0, 0):
            cp.start()

        @pl.loop(0, n_pairs)
        def _(p):
            c0 = 2 * p
            for cp in loads(c0 + 1, 1):
                cp.start()
            for cp in loads(c0, 0):
                cp.wait()
            reduce_window(0)
            write(c0, 0).start()
            for cp in loads(c0 + 1, 1):
                cp.wait()
            reduce_window(1)
            write(c0 + 1, 1).start()
            write(c0, 0).wait()

            @pl.when(p + 1 < n_pairs)
            def _():
                for cp in loads(c0 + 2, 0):
                    cp.start()

            write(c0 + 1, 1).wait()

    return k(rows, pos_tok, w_lanes)


def _gmm_kernel(tot_ref, ce_ref, row_ref, val_ref, ord_ref, nxt_ref, x_hbm, wg_hbm, wu_hbm, wd_hbm, y_hbm,
                wgb, wub, wdb, xbuf, xsem, ybuf, ysem, wgf, wuf, wdf, wsem):
    total = tot_ref[0]
    pieces = GMM_SUB // GMM_TAIL

    def w_copies(ex, slot):
        return [pltpu.make_async_copy(src.at[ex], dst.at[slot], wsem.at[slot, i])
                for i, (src, dst) in enumerate(((wg_hbm, wgf), (wu_hbm, wuf), (wd_hbm, wdf)))]

    def x_copy(g):
        slot = g % GMM_X_SLOTS
        rows = pl.ds(pl.multiple_of(row_ref[g], GMM_TAIL), GMM_SUB)
        return pltpu.make_async_copy(x_hbm.at[rows], xbuf.at[slot], xsem.at[slot])

    def y_piece(g, p):
        slot = g % 2
        rows = pl.ds(pl.multiple_of(row_ref[g] + p * GMM_TAIL, GMM_TAIL), GMM_TAIL)
        return pltpu.make_async_copy(ybuf.at[slot, pl.ds(p * GMM_TAIL, GMM_TAIL)], y_hbm.at[rows], ysem.at[slot])

    def for_y_pieces(g, action):
        for p in range(pieces):
            @pl.when(p * GMM_TAIL < val_ref[g])
            def _():
                action(y_piece(g, p))

    for ahead in range(GMM_X_SLOTS - 1):
        @pl.when(ahead < total)
        def _():
            x_copy(ahead).start()

    def chunk(g, carry):
        e = ce_ref[g]
        prev = ce_ref[jnp.maximum(g - 1, 0)]

        @pl.when((g == 0) | (e != prev))
        def _():
            slot = ord_ref[e] % 2

            @pl.when(g == 0)
            def _():
                for cp in w_copies(e, slot):
                    cp.start()

            for cp in w_copies(e, slot):
                cp.wait()
            wgb[...] = wgf[slot].astype(BF16)
            wub[...] = wuf[slot].astype(BF16)
            wdb[...] = wdf[slot].astype(BF16)
            nxt = nxt_ref[e]

            @pl.when(nxt >= 0)
            def _():
                for cp in w_copies(nxt, 1 - slot):
                    cp.start()

        x_copy(g).wait()

        @pl.when(g + GMM_X_SLOTS - 1 < total)
        def _():
            x_copy(g + GMM_X_SLOTS - 1).start()

        @pl.when(g >= 2)
        def _():
            for_y_pieces(g - 2, lambda cp: cp.wait())

        x_ref = xbuf.at[g % GMM_X_SLOTS]
        y_ref = ybuf.at[g % 2]

        def expert_rows(r0, n):
            rows = pl.ds(r0, n)
            lo, hi = _unpack_bf16_pairs(x_ref[rows, :])
            xb = jnp.concatenate([lo.astype(BF16), hi.astype(BF16)], axis=1)
            mid = (_silu(_dot(xb, wgb[...])) * _dot(xb, wub[...])).astype(BF16)
            y_ref[rows, :] = _pack_bf16_pairs(_dot(mid, wdb[...]))

        n_real = val_ref[g]

        @pl.when(n_real == GMM_SUB)
        def _():
            expert_rows(0, GMM_SUB)

        @pl.when(n_real < GMM_SUB)
        def _():
            @pl.loop(0, (n_real + GMM_TAIL - 1) // GMM_TAIL)
            def _(i):
                expert_rows(pl.multiple_of(i * GMM_TAIL, GMM_TAIL), GMM_TAIL)

        for_y_pieces(g, lambda cp: cp.start())
        return carry

    lax.fori_loop(0, total, chunk, 0)
    for back in (2, 1):
        @pl.when(total >= back)
        def _():
            for_y_pieces(total - back, lambda cp: cp.wait())


def _gmm(x_sorted, n_chunks, chunk_e, chunk_row, chunk_valid, e_ord, e_next, wg, wu, wd):
    r, half = x_sorted.shape
    d = 2 * half
    any_spec = pl.BlockSpec(memory_space=pl.ANY)
    return pl.pallas_call(
        _gmm_kernel,
        grid_spec=pltpu.PrefetchScalarGridSpec(
            num_scalar_prefetch=6,
            grid=(1,),
            in_specs=[any_spec, any_spec, any_spec, any_spec],
            out_specs=any_spec,
            scratch_shapes=[pltpu.VMEM((d, D_EXPERT), BF16), pltpu.VMEM((d, D_EXPERT), BF16),
                            pltpu.VMEM((D_EXPERT, d), BF16),
                            pltpu.VMEM((GMM_X_SLOTS, GMM_SUB, half), I32), pltpu.SemaphoreType.DMA((GMM_X_SLOTS,)),
                            pltpu.VMEM((2, GMM_SUB, half), I32), pltpu.SemaphoreType.DMA((2,)),
                            pltpu.VMEM((2, d, D_EXPERT), F32), pltpu.VMEM((2, d, D_EXPERT), F32),
                            pltpu.VMEM((2, D_EXPERT, d), F32), pltpu.SemaphoreType.DMA((2, 3))]),
        out_shape=jax.ShapeDtypeStruct((r, half), I32),
        compiler_params=pltpu.CompilerParams(dimension_semantics=("arbitrary",), vmem_limit_bytes=VMEM_LIMIT),
        name="expert_gmm",
    )(n_chunks, chunk_e, chunk_row, chunk_valid, e_ord, e_next, x_sorted, wg, wu, wd)


def _combine_kernel(base_ref, mod_ref, routed_ref, gain_ref, *rest):
    y_ref = rest[-1]
    d = base_ref.shape[-1]
    g2 = mod_ref[...][:, :, 2 * d:3 * d]
    out = base_ref[...] + g2 * routed_ref[...]
    y_ref[...] = _rms(out) * gain_ref[...]


def _combine(base, mod, routed, gain, first_chunk, n_chunks, out_chunks, out_first_chunk, out_buf=None):
    _, c, d = base.shape
    nc = COMB_TILE // c
    t0, o0 = first_chunk // nc, out_first_chunk // nc
    chunks_per_stream = out_chunks // mod.shape[0]
    if chunks_per_stream == 1:
        mod_spec = pl.BlockSpec((nc, 1, 3 * d), lambda i: (o0 + i, 0, 0))
    else:
        assert chunks_per_stream % nc == 0
        mod_spec = pl.BlockSpec((1, 1, 3 * d), lambda i: ((o0 + i) * nc // chunks_per_stream, 0, 0))
    blk3 = pl.BlockSpec((nc, c, d), lambda i: (t0 + i, 0, 0))
    in_specs = [blk3, mod_spec, blk3, pl.BlockSpec((1, 1, d), lambda i: (0, 0, 0))]
    args = [base, mod, routed, gain.reshape(1, 1, d)]
    aliases = {}
    if out_buf is not None:
        in_specs.append(pl.BlockSpec(memory_space=pl.ANY))
        args.append(out_buf)
        aliases = {len(args) - 1: 0}
    return pl.pallas_call(
        _combine_kernel,
        grid=(n_chunks // nc,),
        in_specs=in_specs,
        out_specs=pl.BlockSpec((nc, c, d), lambda i: (o0 + i, 0, 0)),
        out_shape=jax.ShapeDtypeStruct((out_chunks, c, d), F32),
        input_output_aliases=aliases,
        compiler_params=pltpu.CompilerParams(dimension_semantics=("arbitrary",), vmem_limit_bytes=VMEM_LIMIT),
        name="combine_norm",
    )(*args)


def _rope_tables(pos):
    half = HEAD_DIM // 2
    inv_freq = ROPE_THETA ** (-jnp.arange(half, dtype=F32) / half)
    ang = pos.astype(F32)[:, None] * inv_freq[None, :]
    cos, sin = jnp.cos(ang), jnp.sin(ang)
    reps = LANES // HEAD_DIM
    return jnp.tile(jnp.concatenate([cos, cos], axis=1), (1, reps)), jnp.tile(jnp.concatenate([-sin, sin], axis=1), (1, reps))


def _routed_ffn(h2, cw, w_gate, w_up, w_down):
    n, half = h2.shape
    rank, counts = _rank(cw)
    counts = counts[:, 0]
    padded = (counts + GMM_TAIL - 1) // GMM_TAIL * GMM_TAIL
    seg_start = (jnp.cumsum(padded) - padded).astype(I32)
    n_rows = n * TOP_K + N_EXPERTS * GMM_TAIL + GMM_SUB
    e_chunks = (counts + GMM_SUB - 1) // GMM_SUB
    chunk_end = jnp.cumsum(e_chunks)
    max_chunks = n * TOP_K // GMM_SUB + N_EXPERTS
    g = jnp.arange(max_chunks, dtype=I32)
    chunk_e = jnp.minimum(jnp.sum((chunk_end[None, :] <= g[:, None]).astype(I32), axis=1), N_EXPERTS - 1)
    eids = jnp.arange(N_EXPERTS, dtype=I32)
    own = chunk_e[:, None] == eids[None, :]
    pick = lambda table: jnp.sum(jnp.where(own, table[None, :], 0), axis=1)
    in_expert = (g - pick(chunk_end - e_chunks)) * GMM_SUB
    chunk_row = (pick(seg_start) + in_expert).astype(I32)
    chunk_valid = jnp.clip(pick(counts) - in_expert, 0, GMM_SUB).astype(I32)
    n_chunks = chunk_end[-1:].astype(I32)
    has_rows = counts > 0
    e_ord = (jnp.cumsum(has_rows.astype(I32)) - has_rows.astype(I32)).astype(I32)
    later = has_rows[None, :] & (eids[None, :] > eids[:, None])
    e_next = jnp.min(jnp.where(later, eids[None, :], N_EXPERTS), axis=1)
    e_next = jnp.where(e_next == N_EXPERTS, -1, e_next).astype(I32)
    pos, w_tok, pos_tok = _slots(cw, rank, seg_start[:, None])
    x_sorted = _sc_dispatch(h2, pos, n_rows)
    y_sorted = _gmm(x_sorted, n_chunks, chunk_e, chunk_row, chunk_valid, e_ord, e_next, w_gate, w_up, w_down)
    w_lanes = jnp.repeat(w_tok, SC_LANES, axis=1)
    return _sc_collect_sum(y_sorted, pos_tok.reshape(n * TOP_K), w_lanes)


def kernel(x_prompt, x_sample, cache_k, cache_v, state_conv, c_prompt, c_sample, w_ada, b_ada, w_in, w_conv,
           w_conv_out, w_attn_o, attn_sinks, w_mix_out, w_router, router_bias, w_exp_gate, w_exp_up, w_exp_down,
           w_sh_gate, w_sh_up, w_sh_down, final_gain):
    assert w_ada.shape[0] == 1, "one layer"
    bp, seq, d = x_prompt.shape
    bs, ts, _ = x_sample.shape
    assert ts == CHUNK and seq % MIX_TILE == 0 and bs % SAMPLE_BB == 0

    c_all = jnp.concatenate([c_prompt, c_sample], axis=0)
    pad = (-c_all.shape[0]) % SUBLANES
    mod = _ada(jnp.pad(c_all, ((0, pad), (0, 0))), w_ada[0], b_ada[0])[:bp + bs]
    mod_p, mod_s = mod[:bp, None, :], mod[bp:, None, :]

    head_axes = (N_KV_HEADS // 2, 2, GQA_GROUP, HEAD_DIM)
    w_in_l = w_in[0]
    w_q = w_in_l[:, OFF_Q:OFF_K].reshape((d,) + head_axes).transpose(0, 1, 3, 2, 4).reshape(d, Q_DIM)
    w_o = w_attn_o[0].reshape(head_axes + (d,)).transpose(0, 2, 1, 3, 4).reshape(Q_DIM, d)
    win = tuple(w.astype(BF16) for w in (w_in_l[:, :OFF_Q], w_q, w_in_l[:, OFF_K:]))
    wco, wao, wmo = (w.astype(BF16) for w in (w_conv_out[0], w_o, w_mix_out[0]))
    cos_p, sin_p = _rope_tables(jnp.arange(seq, dtype=I32))
    cos_s, sin_s = _rope_tables(PAST_LEN + jnp.arange(ts, dtype=I32))

    x1_p, conv_p, k_p, v_p = _mixer_prompt(x_prompt, mod_p, cos_p, sin_p, win, w_conv[0], wco, wao, attn_sinks[0], wmo)
    x1_s, conv_s, k_s, v_s = _mixer_sample(
        x_sample, mod_s, cos_s, sin_s, cache_k[0].reshape(bs, WINDOW, KV_DIM), cache_v[0].reshape(bs, WINDOW, KV_DIM),
        state_conv[0], win, w_conv[0], wco, wao, attn_sinks[0], wmo)

    n_p, n_s = bp * seq, bs * ts
    n = n_p + n_s
    mod2_p, mod2_s = mod[:bp, None, 3 * d:], mod[bp:, None, 3 * d:]
    x1_pc = x1_p.reshape(n_p // CHUNK, CHUNK, d)
    wsg, wsu, wsd = (w[0].astype(BF16) for w in (w_sh_gate, w_sh_up, w_sh_down))
    wr_t, rb = w_router[0].T, router_bias[0][:, None]

    ncp, ncs = n_p // CHUNK, n_s // CHUNK
    half = (ncp + ncs) // 2
    tile_chunks = max(PRE_TILE, COMB_TILE) // CHUNK
    assert half <= ncp and half % tile_chunks == 0 and (ncp - half) % tile_chunks == 0 and ncs % tile_chunks == 0
    assert (half * CHUNK) % (SC_WORKERS * SC_WINDOW) == 0 and (half * CHUNK) % RANK_TILE == 0
    assert (half * CHUNK) % (SC_WORKERS * 2 * SC_SUM_TOKENS) == 0
    y_p = None
    for p0, np_c, ns_c in ((0, half, 0), (half, ncp - half, ncs)):
        h2, base, cw = _pre(x1_pc, p0, np_c, mod2_p, x1_s, ns_c, mod2_s, wsg, wsu, wsd, wr_t, rb)
        routed = _routed_ffn(h2, cw, w_exp_gate[0], w_exp_up[0], w_exp_down[0]).reshape(base.shape)
        y_p = _combine(base, mod2_p, routed, final_gain, 0, np_c, ncp, p0, out_buf=y_p)
        if ns_c:
            y_s = _combine(base, mod2_s, routed, final_gain, np_c, ns_c, ncs, 0)

    kv = lambda a: a.reshape(1, a.shape[0], WINDOW, N_KV_HEADS, HEAD_DIM)
    return (y_p.reshape(bp, seq, d), y_s, conv_p[None], kv(k_p), kv(v_p), conv_s[None], kv(k_s), kv(v_s))
```

```python
import functools

import jax
import jax.numpy as jnp
from jax import lax
from jax.experimental import pallas as pl
from jax.experimental.pallas import tpu as pltpu
from jax.experimental.pallas import tpu_sc as plsc

F32 = jnp.float32
BF16 = jnp.bfloat16
I32 = jnp.int32

D_MODEL = 1024
CHUNK = 64
D_CONV = 1024
CONV_W = 3
N_HEADS = 16
N_KV_HEADS = 4
HEAD_DIM = 64
GQA_GROUP = N_HEADS // N_KV_HEADS
WINDOW = 128
ROPE_THETA = 10000.0
ATTN_SCALE = HEAD_DIM ** -0.5
N_EXPERTS = 64
TOP_K = 8
N_EXPERT_GROUPS = 8
GROUP_SIZE = N_EXPERTS // N_EXPERT_GROUPS
TOPK_GROUPS = 4
D_EXPERT = 256
D_SHARED = 256
ROUTED_SCALE = 2.5
EPS = 1e-6
PAST_LEN = 4096
Q_DIM = N_HEADS * HEAD_DIM
KV_DIM = N_KV_HEADS * HEAD_DIM
OFF_GB, OFF_GC, OFF_XC, OFF_Q, OFF_K, OFF_V, OFF_GCONV, OFF_GATTN, D_IN = (
    0, 1024, 2048, 3072, 4096, 4352, 4608, 5632, 6656)

LANES = 128
SUBLANES = 8
VMEM_LIMIT = 56 * 1024 * 1024

MIX_TILE = 512
ATT_Q = 128
MIX_SIDE_COLS = 256
SAMPLE_BB = 8
PRE_TILE = 512
RANK_TILE = 512
GMM_SUB = 512
GMM_TAIL = 128
GMM_X_SLOTS = 3
COMB_TILE = 256
SC_WORKERS = 32
SC_WINDOW = 96
SC_LANES = 16
SC_SUM_TOKENS = 8
SC_SUM_VREGS = 16


def _const_spec(shape):
    nd = len(shape)
    return pl.BlockSpec(shape, lambda *_: (0,) * nd, pipeline_mode=pl.Buffered(1))


def _rms(x):
    return x * lax.rsqrt(jnp.mean(x * x, axis=-1, keepdims=True) + EPS)


def _sigmoid(x):
    return 1.0 / (1.0 + jnp.exp(-x))


def _silu(x):
    return x * _sigmoid(x)


def _dot(a, b):
    return jnp.dot(a, b, preferred_element_type=F32)


def _pack_bf16_pairs(x):
    half = x.shape[-1] // 2
    lo = lax.bitcast_convert_type(x[..., :half].astype(BF16).astype(F32), I32)
    hi = lax.bitcast_convert_type(x[..., half:].astype(BF16).astype(F32), I32)
    return lax.shift_right_logical(lo, 16) | hi


def _unpack_bf16_pairs(words):
    lo = lax.bitcast_convert_type(lax.shift_left(words, 16), F32)
    hi = lax.bitcast_convert_type(words & jnp.int32(-65536), F32)
    return lo, hi


def _ada_kernel(c_ref, w_ref, b_ref, o_ref):
    s = _silu(c_ref[...]).astype(BF16)
    o_ref[...] = _dot(s, w_ref[...].astype(BF16)) + b_ref[...]


def _ada(c_all, w_ada, b_ada):
    rows = c_all.shape[0]
    n_out = w_ada.shape[1]
    bn = 768
    return pl.pallas_call(
        _ada_kernel,
        grid=(n_out // bn,),
        in_specs=[pl.BlockSpec((rows, D_MODEL), lambda i: (0, 0)),
                  pl.BlockSpec((D_MODEL, bn), lambda i: (0, i)),
                  pl.BlockSpec((1, bn), lambda i: (0, i))],
        out_specs=pl.BlockSpec((rows, bn), lambda i: (0, i)),
        out_shape=jax.ShapeDtypeStruct((rows, n_out), F32),
        name="ada_mod",
    )(c_all, w_ada, b_ada.reshape(1, n_out))


def _rope(x, cos, sin_signed):
    lane = lax.broadcasted_iota(I32, (x.shape[0], LANES), 1)
    first_half = (lane % HEAD_DIM) < (HEAD_DIM // 2)
    outs = []
    for g in range(x.shape[1] // LANES):
        xg = x[:, g * LANES:(g + 1) * LANES]
        up = pltpu.roll(xg, LANES - HEAD_DIM // 2, axis=1)
        down = pltpu.roll(xg, HEAD_DIM // 2, axis=1)
        partner = jnp.where(first_half, up, down)
        outs.append(xg * cos + partner * sin_signed)
    return jnp.concatenate(outs, axis=1)


def _attention(blocks, sinks_ref, obuf, between):
    rq = GQA_GROUP * ATT_Q
    low = lax.broadcasted_iota(I32, (ATT_Q, LANES), 1) < HEAD_DIM
    head_of_lane = lax.broadcasted_iota(I32, (1, rq), 1) // ATT_Q
    units = [(b, pair, par) for b in range(len(blocks)) for pair in range(N_KV_HEADS // 2) for par in range(2)]
    loaded = {}

    def scores(u):
        b, pair, par = units[u]
        if b not in loaded:
            loaded.clear()
            loaded[b] = blocks[b]()
        q_blk, k_of_pair, _, mask, _ = loaded[b]
        keep = low if par == 0 else jnp.logical_not(low)
        cols = [q_blk[:, (GQA_GROUP * pair + i) * LANES:(GQA_GROUP * pair + i + 1) * LANES] for i in range(GQA_GROUP)]
        qg = jnp.concatenate([jnp.where(keep, c, jnp.zeros_like(c)) for c in cols], axis=0)
        st = lax.dot_general(k_of_pair(pair), qg, (((1,), (1,)), ((), ())), preferred_element_type=F32)
        vt = loaded[b][2](pair)[par * HEAD_DIM:(par + 1) * HEAD_DIM, :]
        return jnp.where(mask, st, -jnp.inf), vt, loaded[b][4]

    outs = []
    nxt = scores(0)
    for u, (b, pair, par) in enumerate(units):
        st, vt, row0 = nxt
        if u + 1 < len(units):
            nxt = scores(u + 1)
        if between:
            between.pop(0)()
        g = 2 * pair + par
        sink = jnp.full((1, rq), sinks_ref[g * GQA_GROUP + GQA_GROUP - 1], F32)
        for i in range(GQA_GROUP - 2, -1, -1):
            sink = jnp.where(head_of_lane == i, sinks_ref[g * GQA_GROUP + i], sink)
        m = jnp.maximum(jnp.max(st, axis=0, keepdims=True), sink)
        e = jnp.exp(st - m)
        z = jnp.sum(e, axis=0, keepdims=True) + jnp.exp(sink - m)
        outs.append(_dot(vt, e.astype(BF16)) / z)
        if par == 1:
            for i in range(GQA_GROUP):
                blk = jnp.concatenate([o[:, i * ATT_Q:(i + 1) * ATT_Q] for o in outs], axis=0)
                c0 = (GQA_GROUP * pair + i) * LANES
                obuf[row0:row0 + ATT_Q, c0:c0 + LANES] = blk.T
            outs = []
    for step in between:
        step()


def _in_proj(hb, win_refs, lo, hi):
    w_pre, w_q, w_post = win_refs
    if hi <= OFF_Q:
        return _dot(hb, w_pre[:, lo:hi])
    if lo >= OFF_K:
        return _dot(hb, w_post[:, lo - OFF_K:hi - OFF_K])
    assert (lo, hi) == (OFF_Q, OFF_K)
    return _dot(hb, w_q[...])


def _attention_free_steps(hb, conv, win_ref, wco_ref):
    out, parts = {}, {}
    n_parts = D_MODEL // MIX_SIDE_COLS

    def step(name, piece, compute):
        def run():
            parts.setdefault(name, []).append(compute(piece * MIX_SIDE_COLS, (piece + 1) * MIX_SIDE_COLS))
            if piece == n_parts - 1:
                out[name] = jnp.concatenate(parts.pop(name), axis=1)
        return run

    def conv_in():
        if "conv_in" not in out:
            out["conv_in"] = (out.pop("gate_b") * conv).astype(BF16)
        return out["conv_in"]

    computes = [("gate_b", lambda lo, hi: _in_proj(hb, win_ref, OFF_GB + lo, OFF_GB + hi)),
                ("g_conv", lambda lo, hi: _in_proj(hb, win_ref, OFF_GCONV + lo, OFF_GCONV + hi)),
                ("g_attn", lambda lo, hi: _in_proj(hb, win_ref, OFF_GATTN + lo, OFF_GATTN + hi)),
                ("y_conv", lambda lo, hi: _dot(conv_in(), wco_ref[:, lo:hi]))]
    return [step(name, p, fn) for name, fn in computes for p in range(n_parts)], out


def _mix_out(x, g1, side, y_attn_in, wao_ref, wmo_ref):
    y_attn = _dot(y_attn_in.astype(BF16), wao_ref[...])
    merged = _sigmoid(side["g_conv"]) * side["y_conv"] + _sigmoid(side["g_attn"]) * y_attn
    return x + g1 * _dot(merged.astype(BF16), wmo_ref[...])


def _mixer_prompt_kernel(x_ref, mod_ref, cos_ref, sin_ref, wpre_ref, wq_ref, wpost_ref, wconv_ref, wco_ref, wao_ref,
                         sinks_ref, wmo_ref, x1_ref, conv_ref, k_ref, v_ref, ubuf, kbuf, vtbuf, obuf):
    win_ref = (wpre_ref, wq_ref, wpost_ref)
    j = pl.program_id(1)
    t = x_ref.shape[1]

    @pl.when(j == 0)
    def _():
        ubuf[0:SUBLANES, :] = jnp.zeros((SUBLANES, D_CONV), F32)
        kbuf[0:WINDOW, :] = jnp.zeros((WINDOW, KV_DIM), BF16)
        vtbuf[:, 0:WINDOW] = jnp.zeros((KV_DIM, WINDOW), BF16)

    x = x_ref[0]
    mod = mod_ref[0]
    sh1, sc1, g1 = mod[:, 0:D_MODEL], mod[:, D_MODEL:2 * D_MODEL], mod[:, 2 * D_MODEL:3 * D_MODEL]
    hb = (_rms(x) * (1.0 + sc1) + sh1).astype(BF16)

    u = _in_proj(hb, win_ref, OFF_GC, OFF_XC) * _in_proj(hb, win_ref, OFF_XC, OFF_Q)
    ubuf[SUBLANES:SUBLANES + t, :] = u
    wc = wconv_ref[...]
    conv = wc[0:1] * ubuf[SUBLANES - 2:SUBLANES - 2 + t, :] + wc[1:2] * ubuf[SUBLANES - 1:SUBLANES - 1 + t, :] + wc[2:3] * u
    conv_ref[0] = u[t - (CONV_W - 1):t]
    ubuf[SUBLANES - 2:SUBLANES, :] = u[t - (CONV_W - 1):t]

    cos, sin = cos_ref[...], sin_ref[...]
    q = (_rope(_in_proj(hb, win_ref, OFF_Q, OFF_K), cos, sin) * ATTN_SCALE).astype(BF16)
    k = _rope(_in_proj(hb, win_ref, OFF_K, OFF_V), cos, sin)
    v = _in_proj(hb, win_ref, OFF_V, OFF_GCONV)
    kbuf[WINDOW:WINDOW + t, :] = k.astype(BF16)
    vtbuf[:, WINDOW:WINDOW + t] = v.T.astype(BF16)
    k_ref[0] = k[t - WINDOW:t]
    v_ref[0] = v[t - WINDOW:t]

    nkeys = ATT_Q + WINDOW
    rq = GQA_GROUP * ATT_Q
    ki = lax.broadcasted_iota(I32, (nkeys, rq), 0)
    qi = lax.broadcasted_iota(I32, (nkeys, rq), 1) % ATT_Q
    band = ki // CHUNK - qi // CHUNK
    band_ok = (band >= 0) & (band <= WINDOW // CHUNK)
    def block(s):
        def load():
            mask = band_ok & (ki + (j * t + s * ATT_Q - WINDOW) >= 0)
            k_of_pair = lambda pair: kbuf[s * ATT_Q:s * ATT_Q + nkeys, pair * LANES:(pair + 1) * LANES]
            vt_of_pair = lambda pair: vtbuf[pair * LANES:(pair + 1) * LANES, s * ATT_Q:s * ATT_Q + nkeys]
            return q[s * ATT_Q:(s + 1) * ATT_Q], k_of_pair, vt_of_pair, mask, s * ATT_Q
        return load

    steps, side = _attention_free_steps(hb, conv, win_ref, wco_ref)
    _attention([block(s) for s in range(t // ATT_Q)], sinks_ref, obuf, steps)
    kbuf[0:WINDOW, :] = kbuf[t:t + WINDOW, :]
    vtbuf[:, 0:WINDOW] = vtbuf[:, t:t + WINDOW]

    x1_ref[0] = _mix_out(x, g1, side, obuf[...], wao_ref, wmo_ref)


def _mixer_prompt(x, mod, cos, sin, win, wconv, wco, wao, sinks, wmo):
    b, seq, d = x.shape
    t = MIX_TILE
    return pl.pallas_call(
        _mixer_prompt_kernel,
        grid=(b, seq // t),
        in_specs=[pl.BlockSpec((1, t, d), lambda i, j: (i, j, 0)),
                  pl.BlockSpec((1, 1, 6 * d), lambda i, j: (i, 0, 0)),
                  pl.BlockSpec((t, LANES), lambda i, j: (j, 0)),
                  pl.BlockSpec((t, LANES), lambda i, j: (j, 0)),
                  *[_const_spec(w.shape) for w in win],
                  _const_spec(wconv.shape), _const_spec(wco.shape), _const_spec(wao.shape),
                  pl.BlockSpec(memory_space=pltpu.SMEM),
                  _const_spec(wmo.shape)],
        out_specs=[pl.BlockSpec((1, t, d), lambda i, j: (i, j, 0)),
                   pl.BlockSpec((1, CONV_W - 1, D_CONV), lambda i, j: (i, 0, 0)),
                   pl.BlockSpec((1, WINDOW, KV_DIM), lambda i, j: (i, 0, 0)),
                   pl.BlockSpec((1, WINDOW, KV_DIM), lambda i, j: (i, 0, 0))],
        out_shape=[jax.ShapeDtypeStruct((b, seq, d), F32),
                   jax.ShapeDtypeStruct((b, CONV_W - 1, D_CONV), F32),
                   jax.ShapeDtypeStruct((b, WINDOW, KV_DIM), F32),
                   jax.ShapeDtypeStruct((b, WINDOW, KV_DIM), F32)],
        scratch_shapes=[pltpu.VMEM((SUBLANES + t, D_CONV), F32),
                        pltpu.VMEM((WINDOW + t, KV_DIM), BF16),
                        pltpu.VMEM((KV_DIM, WINDOW + t), BF16),
                        pltpu.VMEM((t, Q_DIM), F32)],
        compiler_params=pltpu.CompilerParams(dimension_semantics=("arbitrary", "arbitrary"),
                                             vmem_limit_bytes=VMEM_LIMIT),
        name="mixer_prompt",
    )(x, mod, cos, sin, *win, wconv, wco, wao, sinks, wmo)


def _mixer_sample_kernel(x_ref, mod_ref, cos_ref, sin_ref, ck_ref, cv_ref, sconv_ref, wpre_ref, wq_ref, wpost_ref,
                         wconv_ref, wco_ref, wao_ref, sinks_ref, wmo_ref, x1_ref, conv_ref, k_ref, v_ref, ubuf, obuf):
    win_ref = (wpre_ref, wq_ref, wpost_ref)
    bb, t, d = x_ref.shape
    x3 = x_ref[...]
    mod = mod_ref[...]
    sh1, sc1, g1 = mod[:, :, 0:d], mod[:, :, d:2 * d], mod[:, :, 2 * d:3 * d]
    x = x3.reshape(bb * t, d)
    hb = (_rms(x3) * (1.0 + sc1) + sh1).astype(BF16).reshape(bb * t, d)

    u = _in_proj(hb, win_ref, OFF_GC, OFF_XC) * _in_proj(hb, win_ref, OFF_XC, OFF_Q)
    u3 = u.reshape(bb, t, D_CONV)
    ubuf[:, SUBLANES - 2:SUBLANES, :] = sconv_ref[...]
    ubuf[:, SUBLANES:SUBLANES + t, :] = u3
    wc = wconv_ref[...]
    conv = (wc[0:1] * ubuf[:, SUBLANES - 2:SUBLANES - 2 + t, :] + wc[1:2] * ubuf[:, SUBLANES - 1:SUBLANES - 1 + t, :]
            + wc[2:3] * u3).reshape(bb * t, D_CONV)
    conv_ref[...] = u3[:, t - (CONV_W - 1):t, :]

    cos = jnp.concatenate([cos_ref[...]] * bb, axis=0)
    sin = jnp.concatenate([sin_ref[...]] * bb, axis=0)
    q = (_rope(_in_proj(hb, win_ref, OFF_Q, OFF_K), cos, sin) * ATTN_SCALE).astype(BF16)
    k = _rope(_in_proj(hb, win_ref, OFF_K, OFF_V), cos, sin)
    v = _in_proj(hb, win_ref, OFF_V, OFF_GCONV)
    per = ATT_Q // t
    nkeys = per * (WINDOW + t)
    rq = GQA_GROUP * ATT_Q
    key_stream = lax.broadcasted_iota(I32, (nkeys, rq), 0) // (WINDOW + t)
    query_stream = (lax.broadcasted_iota(I32, (nkeys, rq), 1) % ATT_Q) // t
    mask = key_stream == query_stream
    def block(blk):
        def load():
            k_parts, v_parts = [], []
            for b in range(blk * per, (blk + 1) * per):
                kb, vb = k[b * t:(b + 1) * t], v[b * t:(b + 1) * t]
                ck, cv = ck_ref[b], cv_ref[b]
                k_ref[b] = jnp.concatenate([ck[t:WINDOW], kb], axis=0)
                v_ref[b] = jnp.concatenate([cv[t:WINDOW], vb], axis=0)
                k_parts += [ck, kb]
                v_parts += [cv, vb]
            k_all = jnp.concatenate(k_parts, axis=0).astype(BF16)
            vt_all = jnp.concatenate(v_parts, axis=0).T.astype(BF16)
            k_of_pair = lambda pair: k_all[:, pair * LANES:(pair + 1) * LANES]
            vt_of_pair = lambda pair: vt_all[pair * LANES:(pair + 1) * LANES, :]
            return q[blk * ATT_Q:(blk + 1) * ATT_Q], k_of_pair, vt_of_pair, mask, blk * ATT_Q
        return load

    steps, side = _attention_free_steps(hb, conv, win_ref, wco_ref)
    _attention([block(blk) for blk in range(bb // per)], sinks_ref, obuf, steps)

    g1f = jnp.broadcast_to(g1, (bb, t, d)).reshape(bb * t, d)
    x1_ref[...] = _mix_out(x, g1f, side, obuf[...], wao_ref, wmo_ref).reshape(bb, t, d)


def _mixer_sample(x, mod, cos, sin, ck, cv, sconv, win, wconv, wco, wao, sinks, wmo):
    b, t, d = x.shape
    bb = SAMPLE_BB
    blk = lambda *s: pl.BlockSpec((bb,) + s, lambda i: (i, 0, 0))
    return pl.pallas_call(
        _mixer_sample_kernel,
        grid=(b // bb,),
        in_specs=[blk(t, d), blk(1, 6 * d),
                  pl.BlockSpec((t, LANES), lambda i: (0, 0)), pl.BlockSpec((t, LANES), lambda i: (0, 0)),
                  blk(WINDOW, KV_DIM), blk(WINDOW, KV_DIM), blk(CONV_W - 1, D_CONV),
                  *[_const_spec(w.shape) for w in win],
                  _const_spec(wconv.shape), _const_spec(wco.shape), _const_spec(wao.shape),
                  pl.BlockSpec(memory_space=pltpu.SMEM),
                  _const_spec(wmo.shape)],
        out_specs=[blk(t, d), blk(CONV_W - 1, D_CONV), blk(WINDOW, KV_DIM), blk(WINDOW, KV_DIM)],
        out_shape=[jax.ShapeDtypeStruct((b, t, d), F32),
                   jax.ShapeDtypeStruct((b, CONV_W - 1, D_CONV), F32),
                   jax.ShapeDtypeStruct((b, WINDOW, KV_DIM), F32),
                   jax.ShapeDtypeStruct((b, WINDOW, KV_DIM), F32)],
        scratch_shapes=[pltpu.VMEM((bb, SUBLANES + t, D_CONV), F32),
                        pltpu.VMEM((bb * t, Q_DIM), F32)],
        compiler_params=pltpu.CompilerParams(dimension_semantics=("arbitrary",), vmem_limit_bytes=VMEM_LIMIT),
        name="mixer_sample",
    )(x, mod, cos, sin, ck, cv, sconv, *win, wconv, wco, wao, sinks, wmo)


def _pre_kernel(*refs, prompt_tiles, has_sample):
    if has_sample:
        xp_ref, mp_ref, xs_ref, ms_ref, wsg_ref, wsu_ref, wsd_ref, wrh_ref, wrl_ref, rb_ref, h2_ref, base_ref, cw_ref = refs
    else:
        xp_ref, mp_ref, wsg_ref, wsu_ref, wsd_ref, wrh_ref, wrl_ref, rb_ref, h2_ref, base_ref, cw_ref = refs
    nc, c, d = xp_ref.shape
    t = nc * c
    x3, mod = xp_ref[...], mp_ref[...]
    if has_sample:
        is_prompt = pl.program_id(0) < prompt_tiles
        x3 = jnp.where(is_prompt, x3, xs_ref[...])
        mod = jnp.where(is_prompt, mod, ms_ref[...])
    sh2, sc2, g2 = mod[:, :, 0:d], mod[:, :, d:2 * d], mod[:, :, 2 * d:3 * d]
    h3 = _rms(x3) * (1.0 + sc2) + sh2
    h2 = h3.reshape(t, d)
    hb = h2.astype(BF16)
    h2_ref[...] = _pack_bf16_pairs(h2)
    shared = _dot((_silu(_dot(hb, wsg_ref[...])) * _dot(hb, wsu_ref[...])).astype(BF16), wsd_ref[...])
    base_ref[...] = x3 + g2 * shared.reshape(nc, c, d)

    h_lo = (h2 - hb.astype(F32)).astype(BF16)
    nt = lambda a, b: lax.dot_general(a, b, (((1,), (1,)), ((), ())), preferred_element_type=F32)
    logits = nt(wrh_ref[...], hb) + (nt(wrh_ref[...], h_lo) + nt(wrl_ref[...], hb))
    scores = _sigmoid(logits)
    biased = scores + rb_ref[...]
    g3 = biased.reshape(N_EXPERT_GROUPS, GROUP_SIZE, t)
    member = lax.broadcasted_iota(I32, g3.shape, 1)
    m1 = jnp.max(g3, axis=1, keepdims=True)
    first = jnp.min(jnp.where(g3 == m1, member, GROUP_SIZE), axis=1, keepdims=True)
    m2 = jnp.max(jnp.where(member == first, -jnp.inf, g3), axis=1, keepdims=True)
    gs = m1 + m2
    gidx = lax.broadcasted_iota(I32, gs.shape, 0)
    grank = jnp.zeros(gs.shape, I32)
    for o in range(N_EXPERT_GROUPS):
        other = gs[o:o + 1]
        grank += ((other > gs) | ((other == gs) & (o < gidx))).astype(I32)
    group_ok = grank < TOPK_GROUPS
    slot = jnp.zeros((1, 1, t), I32)
    takes = []
    for gi in range(N_EXPERT_GROUPS):
        ok = group_ok[gi:gi + 1]
        takes.append([ok & (slot == s) for s in range(TOPK_GROUPS)])
        slot = slot + ok.astype(I32)
    packed = []
    for s in range(TOPK_GROUPS):
        vals = jnp.zeros((GROUP_SIZE, t), F32)
        for gi in range(N_EXPERT_GROUPS):
            vals = jnp.where(takes[gi][s][0], g3[gi], vals)
        packed.append(vals)
    cand = jnp.concatenate(packed, axis=0)
    cidx = lax.broadcasted_iota(I32, cand.shape, 0)
    crank = jnp.zeros(cand.shape, I32)
    for o in range(TOPK_GROUPS * GROUP_SIZE):
        other = cand[o:o + 1]
        crank += ((other > cand) | ((other == cand) & (o < cidx))).astype(I32)
    chosen = crank < TOP_K
    sel_groups = []
    for gi in range(N_EXPERT_GROUPS):
        hit = jnp.zeros((GROUP_SIZE, t), jnp.bool_)
        for s in range(TOPK_GROUPS):
            hit = hit | (takes[gi][s][0] & chosen[s * GROUP_SIZE:(s + 1) * GROUP_SIZE])
        sel_groups.append(hit)
    sel = jnp.concatenate(sel_groups, axis=0)
    ssum = jnp.sum(jnp.where(sel, scores, 0.0), axis=0, keepdims=True)
    cw_ref[...] = jnp.where(sel, scores / ssum * ROUTED_SCALE, -1.0)


def _pre(x1_p, p_chunk0, ncp, mod_p, x1_s, ncs, mod_s, wsg, wsu, wsd, wr_hi, wr_lo, rb):
    ncp_all, c, d = x1_p.shape
    nc = PRE_TILE // c
    nchunks = ncp + ncs
    n = nchunks * c
    pt, p0 = ncp // nc, p_chunk0 // nc
    tiles_per_stream = ncp_all // mod_p.shape[0] // nc
    blk3 = pl.BlockSpec((nc, c, d), lambda i: (i, 0, 0))
    p_tile = lambda i: p0 + jnp.minimum(i, pt - 1)
    s_tile = lambda i: jnp.maximum(i - pt, 0)
    s_args, s_specs = [], []
    if ncs:
        s_args = [x1_s, mod_s]
        s_specs = [pl.BlockSpec((nc, c, d), lambda i: (s_tile(i), 0, 0)),
                   pl.BlockSpec((nc, 1, 3 * d), lambda i: (s_tile(i), 0, 0))]
    return pl.pallas_call(
        functools.partial(_pre_kernel, prompt_tiles=pt, has_sample=bool(ncs)),
        grid=(nchunks // nc,),
        in_specs=[pl.BlockSpec((nc, c, d), lambda i: (p_tile(i), 0, 0)),
                  pl.BlockSpec((1, 1, 3 * d), lambda i: (p_tile(i) // tiles_per_stream, 0, 0))] + s_specs + [
                  _const_spec(wsg.shape), _const_spec(wsu.shape), _const_spec(wsd.shape),
                  _const_spec(wr_hi.shape), _const_spec(wr_lo.shape), _const_spec(rb.shape)],
        out_specs=[pl.BlockSpec((nc * c, d // 2), lambda i: (i, 0)), blk3,
                   pl.BlockSpec((N_EXPERTS, nc * c), lambda i: (0, i))],
        out_shape=[jax.ShapeDtypeStruct((n, d // 2), I32),
                   jax.ShapeDtypeStruct((nchunks, c, d), F32),
                   jax.ShapeDtypeStruct((N_EXPERTS, n), F32)],
        compiler_params=pltpu.CompilerParams(dimension_semantics=("arbitrary",), vmem_limit_bytes=VMEM_LIMIT),
        name="pre_ffn",
    )(x1_p, mod_p, *s_args, wsg, wsu, wsd, wr_hi, wr_lo, rb)


def _rank_kernel(cw_ref, rank_ref, cnt_ref, carry):
    i = pl.program_id(0)
    t = cw_ref.shape[1]

    @pl.when(i == 0)
    def _():
        carry[...] = jnp.zeros(carry.shape, F32)

    sel = (cw_ref[...] >= 0.0).astype(BF16)
    r = lax.broadcasted_iota(I32, (t, t), 0)
    c = lax.broadcasted_iota(I32, (t, t), 1)
    before = (r < c).astype(BF16)
    rank = carry[...] + _dot(sel, before)
    rank_ref[...] = rank.astype(I32)
    carry[...] = carry[...] + jnp.sum(sel.astype(F32), axis=1, keepdims=True)
    cnt_ref[...] = carry[...].astype(I32)


def _rank(cw):
    e, n = cw.shape
    t = RANK_TILE
    return pl.pallas_call(
        _rank_kernel,
        grid=(n // t,),
        in_specs=[pl.BlockSpec((e, t), lambda i: (0, i))],
        out_specs=[pl.BlockSpec((e, t), lambda i: (0, i)), pl.BlockSpec((e, 1), lambda i: (0, 0))],
        out_shape=[jax.ShapeDtypeStruct((e, n), I32), jax.ShapeDtypeStruct((e, 1), I32)],
        scratch_shapes=[pltpu.VMEM((e, 1), F32)],
        compiler_params=pltpu.CompilerParams(dimension_semantics=("arbitrary",)),
        name="expert_rank",
    )(cw)


def _slot_kernel(cw_ref, rank_ref, start_ref, pos_ref, w_ref, pos_tok_ref):
    cw = cw_ref[...]
    e, t = cw.shape
    sel = cw >= 0.0
    r = lax.broadcasted_iota(I32, (e, e), 0)
    c = lax.broadcasted_iota(I32, (e, e), 1)
    lower = (c < r).astype(BF16)
    kidx = _dot(lower, sel.astype(BF16))
    posf = start_ref[...].astype(F32) + rank_ref[...].astype(F32)
    pos_rows, w_rows = [], []
    for k in range(TOP_K):
        m = sel & (kidx == float(k))
        pos_rows.append(jnp.sum(jnp.where(m, posf, 0.0), axis=0, keepdims=True))
        w_rows.append(jnp.sum(jnp.where(m, cw, 0.0), axis=0, keepdims=True))
    pos_ref[...] = jnp.concatenate(pos_rows, axis=0).astype(I32)
    stack = jnp.concatenate(w_rows + pos_rows + [jnp.zeros((LANES - 2 * TOP_K, t), F32)], axis=0).T
    w_ref[...] = stack[:, :TOP_K]
    pos_tok_ref[...] = stack[:, TOP_K:2 * TOP_K].astype(I32)


def _slots(cw, rank, seg_start):
    e, n = cw.shape
    t = RANK_TILE
    return pl.pallas_call(
        _slot_kernel,
        grid=(n // t,),
        in_specs=[pl.BlockSpec((e, t), lambda i: (0, i)), pl.BlockSpec((e, t), lambda i: (0, i)),
                  pl.BlockSpec((e, 1), lambda i: (0, 0))],
        out_specs=[pl.BlockSpec((TOP_K, t), lambda i: (0, i)), pl.BlockSpec((t, TOP_K), lambda i: (i, 0)),
                   pl.BlockSpec((t, TOP_K), lambda i: (i, 0))],
        out_shape=[jax.ShapeDtypeStruct((TOP_K, n), I32), jax.ShapeDtypeStruct((n, TOP_K), F32),
                   jax.ShapeDtypeStruct((n, TOP_K), I32)],
        compiler_params=pltpu.CompilerParams(dimension_semantics=("arbitrary",)),
        name="expert_slots",
    )(cw, rank, seg_start)


def _sc_mesh():
    return plsc.VectorSubcoreMesh(core_axis_name="c", subcore_axis_name="s")


def _sc_worker_id():
    return lax.axis_index("s") * (SC_WORKERS // 16) + lax.axis_index("c")


def _sc_dispatch(rows, pos, n_rows):
    n, d = rows.shape
    per_w = n // SC_WORKERS
    w = SC_WINDOW
    n_chunks = per_w // w

    @functools.partial(
        pl.kernel, mesh=_sc_mesh(),
        out_type=jax.ShapeDtypeStruct((n_rows, d), rows.dtype),
        scratch_types=[pltpu.VMEM((2, TOP_K, w), I32), pltpu.VMEM((2, w, d), rows.dtype),
                       pltpu.SemaphoreType.DMA((2,)), pltpu.SemaphoreType.DMA((2,)), pltpu.SemaphoreType.DMA((2,))],
        name="sc_dispatch")
    def k(rows_hbm, pos_hbm, o_hbm, idx_v, rows_v, row_sem, idx_sem, out_sem):
        wid = _sc_worker_id()
        base = wid * per_w

        def loads(c, slot):
            off = pl.multiple_of(base + c * w, SUBLANES)
            return (pltpu.make_async_copy(rows_hbm.at[pl.ds(off, w)], rows_v.at[slot], row_sem.at[slot]),
                    pltpu.make_async_copy(pos_hbm.at[wid * n_chunks + c], idx_v.at[slot], idx_sem.at[slot]))

        def scatters(slot):
            return [pltpu.make_async_copy(rows_v.at[slot], o_hbm.at[idx_v.at[slot, kk]], out_sem.at[slot])
                    for kk in range(TOP_K)]

        for cp in loads(0, 0):
            cp.start()
        for c in range(n_chunks):
            slot = c % 2
            for cp in loads(c, slot):
                cp.wait()
            for cp in scatters(slot):
                cp.start()
            if c >= 1:
                for cp in scatters(1 - slot):
                    cp.wait()
            if c + 1 < n_chunks:
                for cp in loads(c + 1, 1 - slot):
                    cp.start()
        for cp in scatters((n_chunks - 1) % 2):
            cp.wait()

    pos_chunks = pos.reshape(TOP_K, n // w, w).transpose(1, 0, 2)
    return k(rows, pos_chunks)


def _sc_collect_sum(rows, pos_tok, w_lanes):
    words = rows.shape[1]
    n = w_lanes.shape[0]
    lanes = SC_LANES
    per_w = n // SC_WORKERS
    tw = SC_SUM_TOKENS
    n_pairs = per_w // (2 * tw)
    col_blocks = words // lanes // SC_SUM_VREGS

    @functools.partial(
        pl.kernel, mesh=_sc_mesh(),
        out_type=jax.ShapeDtypeStruct((n, 2 * words), F32),
        scratch_types=[pltpu.VMEM((per_w * TOP_K,), I32), pltpu.VMEM((2, tw * TOP_K, words), I32),
                       pltpu.VMEM((2, tw, TOP_K * lanes), F32), pltpu.VMEM((2, tw, 2 * words), F32),
                       pltpu.SemaphoreType.DMA((2,)), pltpu.SemaphoreType.DMA((2,)), pltpu.SemaphoreType.DMA((2,))],
        compiler_params=pltpu.CompilerParams(needs_layout_passes=False),
        name="sc_collect_sum")
    def k(rows_hbm, pos_hbm, w_hbm, o_hbm, idx_v, rows_v, w_v, out_v, in_sem, w_sem, out_sem):
        base = pl.multiple_of(_sc_worker_id() * per_w, SUBLANES)
        pltpu.sync_copy(pos_hbm.at[pl.ds(pl.multiple_of(base * TOP_K, SUBLANES), per_w * TOP_K)], idx_v)

        def loads(c, slot):
            idx = idx_v.at[pl.ds(pl.multiple_of(c * tw * TOP_K, SUBLANES), tw * TOP_K)]
            tok0 = pl.multiple_of(base + c * tw, SUBLANES)
            return (pltpu.make_async_copy(rows_hbm.at[idx], rows_v.at[slot], in_sem.at[slot]),
                    pltpu.make_async_copy(w_hbm.at[pl.ds(tok0, tw)], w_v.at[slot], w_sem.at[slot]))

        def write(c, slot):
            tok0 = pl.multiple_of(base + c * tw, SUBLANES)
            return pltpu.make_async_copy(out_v.at[slot], o_hbm.at[pl.ds(tok0, tw)], out_sem.at[slot])

        high_half = jnp.full((lanes,), -65536, I32)
        sixteen = jnp.full((lanes,), 16, I32)

        def reduce_window(slot):
            rv, wv, ov = rows_v.at[slot], w_v.at[slot], out_v.at[slot]

            @pl.loop(0, tw)
            def _(t):
                for cb in range(col_blocks):
                    acc_lo, acc_hi = [None] * SC_SUM_VREGS, [None] * SC_SUM_VREGS
                    for kk in range(TOP_K):
                        wk = wv[t, pl.ds(kk * lanes, lanes)]
                        for c in range(SC_SUM_VREGS):
                            wd = rv[t * TOP_K + kk, pl.ds((cb * SC_SUM_VREGS + c) * lanes, lanes)]
                            lo = wk * plsc.bitcast(lax.shift_left(wd, sixteen), F32)
                            hi = wk * plsc.bitcast(wd & high_half, F32)
                            acc_lo[c] = lo if kk == 0 else acc_lo[c] + lo
                            acc_hi[c] = hi if kk == 0 else acc_hi[c] + hi
                    for c in range(SC_SUM_VREGS):
                        col = (cb * SC_SUM_VREGS + c) * lanes
                        ov[t, pl.ds(col, lanes)] = acc_lo[c]
                        ov[t, pl.ds(words + col, lanes)] = acc_hi[c]

        for cp in loads(0, 0):
            cp.start()

        @pl.loop(0, n_pairs)
        def _(p):
            c0 = 2 * p
            for cp in loads(c0 + 1, 1):
                cp.start()
            for cp in loads(c0, 0):
                cp.wait()
            reduce_window(0)
            write(c0, 0).start()
            for cp in loads(c0 + 1, 1):
                cp.wait()
            reduce_window(1)
            write(c0 + 1, 1).start()
            write(c0, 0).wait()

            @pl.when(p + 1 < n_pairs)
            def _():
                for cp in loads(c0 + 2, 0):
                    cp.start()

            write(c0 + 1, 1).wait()

    return k(rows, pos_tok, w_lanes)


def _gmm_kernel(tot_ref, ce_ref, row_ref, val_ref, ord_ref, nxt_ref, x_hbm, wg_hbm, wu_hbm, wd_hbm, y_hbm,
                wgb, wub, wdb, xbuf, xsem, ybuf, ysem, wgf, wuf, wdf, wsem):
    total = tot_ref[0]
    pieces = GMM_SUB // GMM_TAIL

    def w_copies(ex, slot):
        return [pltpu.make_async_copy(src.at[ex], dst.at[slot], wsem.at[slot, i])
                for i, (src, dst) in enumerate(((wg_hbm, wgf), (wu_hbm, wuf), (wd_hbm, wdf)))]

    def x_copy(g):
        slot = g % GMM_X_SLOTS
        rows = pl.ds(pl.multiple_of(row_ref[g], GMM_TAIL), GMM_SUB)
        return pltpu.make_async_copy(x_hbm.at[rows], xbuf.at[slot], xsem.at[slot])

    def y_piece(g, p):
        slot = g % 2
        rows = pl.ds(pl.multiple_of(row_ref[g] + p * GMM_TAIL, GMM_TAIL), GMM_TAIL)
        return pltpu.make_async_copy(ybuf.at[slot, pl.ds(p * GMM_TAIL, GMM_TAIL)], y_hbm.at[rows], ysem.at[slot])

    def for_y_pieces(g, action):
        for p in range(pieces):
            @pl.when(p * GMM_TAIL < val_ref[g])
            def _():
                action(y_piece(g, p))

    for ahead in range(GMM_X_SLOTS - 1):
        @pl.when(ahead < total)
        def _():
            x_copy(ahead).start()

    def chunk(g, carry):
        e = ce_ref[g]
        prev = ce_ref[jnp.maximum(g - 1, 0)]

        @pl.when((g == 0) | (e != prev))
        def _():
            slot = ord_ref[e] % 2

            @pl.when(g == 0)
            def _():
                for cp in w_copies(e, slot):
                    cp.start()

            for cp in w_copies(e, slot):
                cp.wait()
            wgb[...] = wgf[slot].astype(BF16)
            wub[...] = wuf[slot].astype(BF16)
            wdb[...] = wdf[slot].astype(BF16)
            nxt = nxt_ref[e]

            @pl.when(nxt >= 0)
            def _():
                for cp in w_copies(nxt, 1 - slot):
                    cp.start()

        x_copy(g).wait()

        @pl.when(g + GMM_X_SLOTS - 1 < total)
        def _():
            x_copy(g + GMM_X_SLOTS - 1).start()

        @pl.when(g >= 2)
        def _():
            for_y_pieces(g - 2, lambda cp: cp.wait())

        x_ref = xbuf.at[g % GMM_X_SLOTS]
        y_ref = ybuf.at[g % 2]

        def expert_rows(r0, n):
            rows = pl.ds(r0, n)
            lo, hi = _unpack_bf16_pairs(x_ref[rows, :])
            xb = jnp.concatenate([lo.astype(BF16), hi.astype(BF16)], axis=1)
            mid = (_silu(_dot(xb, wgb[...])) * _dot(xb, wub[...])).astype(BF16)
            y_ref[rows, :] = _pack_bf16_pairs(_dot(mid, wdb[...]))

        n_real = val_ref[g]

        @pl.when(n_real == GMM_SUB)
        def _():
            expert_rows(0, GMM_SUB)

        @pl.when(n_real < GMM_SUB)
        def _():
            @pl.loop(0, (n_real + GMM_TAIL - 1) // GMM_TAIL)
            def _(i):
                expert_rows(pl.multiple_of(i * GMM_TAIL, GMM_TAIL), GMM_TAIL)

        for_y_pieces(g, lambda cp: cp.start())
        return carry

    lax.fori_loop(0, total, chunk, 0)
    for back in (2, 1):
        @pl.when(total >= back)
        def _():
            for_y_pieces(total - back, lambda cp: cp.wait())


def _gmm(x_sorted, n_chunks, chunk_e, chunk_row, chunk_valid, e_ord, e_next, wg, wu, wd):
    r, half = x_sorted.shape
    d = 2 * half
    any_spec = pl.BlockSpec(memory_space=pl.ANY)
    return pl.pallas_call(
        _gmm_kernel,
        grid_spec=pltpu.PrefetchScalarGridSpec(
            num_scalar_prefetch=6,
            grid=(1,),
            in_specs=[any_spec, any_spec, any_spec, any_spec],
            out_specs=any_spec,
            scratch_shapes=[pltpu.VMEM((d, D_EXPERT), BF16), pltpu.VMEM((d, D_EXPERT), BF16),
                            pltpu.VMEM((D_EXPERT, d), BF16),
                            pltpu.VMEM((GMM_X_SLOTS, GMM_SUB, half), I32), pltpu.SemaphoreType.DMA((GMM_X_SLOTS,)),
                            pltpu.VMEM((2, GMM_SUB, half), I32), pltpu.SemaphoreType.DMA((2,)),
                            pltpu.VMEM((2, d, D_EXPERT), F32), pltpu.VMEM((2, d, D_EXPERT), F32),
                            pltpu.VMEM((2, D_EXPERT, d), F32), pltpu.SemaphoreType.DMA((2, 3))]),
        out_shape=jax.ShapeDtypeStruct((r, half), I32),
        compiler_params=pltpu.CompilerParams(dimension_semantics=("arbitrary",), vmem_limit_bytes=VMEM_LIMIT),
        name="expert_gmm",
    )(n_chunks, chunk_e, chunk_row, chunk_valid, e_ord, e_next, x_sorted, wg, wu, wd)


def _combine_kernel(base_ref, mod_ref, routed_ref, gain_ref, *rest):
    y_ref = rest[-1]
    d = base_ref.shape[-1]
    g2 = mod_ref[...][:, :, 2 * d:3 * d]
    out = base_ref[...] + g2 * routed_ref[...]
    y_ref[...] = _rms(out) * gain_ref[...]


def _combine(base, mod, routed, gain, first_chunk, n_chunks, out_chunks, out_first_chunk, out_buf=None):
    _, c, d = base.shape
    nc = COMB_TILE // c
    t0, o0 = first_chunk // nc, out_first_chunk // nc
    chunks_per_stream = out_chunks // mod.shape[0]
    if chunks_per_stream == 1:
        mod_spec = pl.BlockSpec((nc, 1, 3 * d), lambda i: (o0 + i, 0, 0))
    else:
        assert chunks_per_stream % nc == 0
        mod_spec = pl.BlockSpec((1, 1, 3 * d), lambda i: ((o0 + i) * nc // chunks_per_stream, 0, 0))
    blk3 = pl.BlockSpec((nc, c, d), lambda i: (t0 + i, 0, 0))
    in_specs = [blk3, mod_spec, blk3, pl.BlockSpec((1, 1, d), lambda i: (0, 0, 0))]
    args = [base, mod, routed, gain.reshape(1, 1, d)]
    aliases = {}
    if out_buf is not None:
        in_specs.append(pl.BlockSpec(memory_space=pl.ANY))
        args.append(out_buf)
        aliases = {len(args) - 1: 0}
    return pl.pallas_call(
        _combine_kernel,
        grid=(n_chunks // nc,),
        in_specs=in_specs,
        out_specs=pl.BlockSpec((nc, c, d), lambda i: (o0 + i, 0, 0)),
        out_shape=jax.ShapeDtypeStruct((out_chunks, c, d), F32),
        input_output_aliases=aliases,
        compiler_params=pltpu.CompilerParams(dimension_semantics=("arbitrary",), vmem_limit_bytes=VMEM_LIMIT),
        name="combine_norm",
    )(*args)


def _rope_tables(pos):
    half = HEAD_DIM // 2
    inv_freq = ROPE_THETA ** (-jnp.arange(half, dtype=F32) / half)
    ang = pos.astype(F32)[:, None] * inv_freq[None, :]
    cos, sin = jnp.cos(ang), jnp.sin(ang)
    reps = LANES // HEAD_DIM
    return jnp.tile(jnp.concatenate([cos, cos], axis=1), (1, reps)), jnp.tile(jnp.concatenate([-sin, sin], axis=1), (1, reps))


def _routed_ffn(h2, cw, w_gate, w_up, w_down):
    n, half = h2.shape
    rank, counts = _rank(cw)
    counts = counts[:, 0]
    padded = (counts + GMM_TAIL - 1) // GMM_TAIL * GMM_TAIL
    seg_start = (jnp.cumsum(padded) - padded).astype(I32)
    n_rows = n * TOP_K + N_EXPERTS * GMM_TAIL + GMM_SUB
    e_chunks = (counts + GMM_SUB - 1) // GMM_SUB
    chunk_end = jnp.cumsum(e_chunks)
    max_chunks = n * TOP_K // GMM_SUB + N_EXPERTS
    g = jnp.arange(max_chunks, dtype=I32)
    chunk_e = jnp.minimum(jnp.sum((chunk_end[None, :] <= g[:, None]).astype(I32), axis=1), N_EXPERTS - 1)
    eids = jnp.arange(N_EXPERTS, dtype=I32)
    own = chunk_e[:, None] == eids[None, :]
    pick = lambda table: jnp.sum(jnp.where(own, table[None, :], 0), axis=1)
    in_expert = (g - pick(chunk_end - e_chunks)) * GMM_SUB
    chunk_row = (pick(seg_start) + in_expert).astype(I32)
    chunk_valid = jnp.clip(pick(counts) - in_expert, 0, GMM_SUB).astype(I32)
    n_chunks = chunk_end[-1:].astype(I32)
    has_rows = counts > 0
    e_ord = (jnp.cumsum(has_rows.astype(I32)) - has_rows.astype(I32)).astype(I32)
    later = has_rows[None, :] & (eids[None, :] > eids[:, None])
    e_next = jnp.min(jnp.where(later, eids[None, :], N_EXPERTS), axis=1)
    e_next = jnp.where(e_next == N_EXPERTS, -1, e_next).astype(I32)
    pos, w_tok, pos_tok = _slots(cw, rank, seg_start[:, None])
    x_sorted = _sc_dispatch(h2, pos, n_rows)
    y_sorted = _gmm(x_sorted, n_chunks, chunk_e, chunk_row, chunk_valid, e_ord, e_next, w_gate, w_up, w_down)
    w_lanes = jnp.repeat(w_tok, SC_LANES, axis=1)
    return _sc_collect_sum(y_sorted, pos_tok.reshape(n * TOP_K), w_lanes)


def kernel(x_prompt, x_sample, cache_k, cache_v, state_conv, c_prompt, c_sample, w_ada, b_ada, w_in, w_conv,
           w_conv_out, w_attn_o, attn_sinks, w_mix_out, w_router, router_bias, w_exp_gate, w_exp_up, w_exp_down,
           w_sh_gate, w_sh_up, w_sh_down, final_gain):
    assert w_ada.shape[0] == 1, "one layer"
    bp, seq, d = x_prompt.shape
    bs, ts, _ = x_sample.shape
    assert ts == CHUNK and seq % MIX_TILE == 0 and bs % SAMPLE_BB == 0

    c_all = jnp.concatenate([c_prompt, c_sample], axis=0)
    pad = (-c_all.shape[0]) % SUBLANES
    mod = _ada(jnp.pad(c_all, ((0, pad), (0, 0))), w_ada[0], b_ada[0])[:bp + bs]
    mod_p, mod_s = mod[:bp, None, :], mod[bp:, None, :]

    head_axes = (N_KV_HEADS // 2, 2, GQA_GROUP, HEAD_DIM)
    w_in_l = w_in[0]
    w_q = w_in_l[:, OFF_Q:OFF_K].reshape((d,) + head_axes).transpose(0, 1, 3, 2, 4).reshape(d, Q_DIM)
    w_o = w_attn_o[0].reshape(head_axes + (d,)).transpose(0, 2, 1, 3, 4).reshape(Q_DIM, d)
    win = tuple(w.astype(BF16) for w in (w_in_l[:, :OFF_Q], w_q, w_in_l[:, OFF_K:]))
    wco, wao, wmo = (w.astype(BF16) for w in (w_conv_out[0], w_o, w_mix_out[0]))
    cos_p, sin_p = _rope_tables(jnp.arange(seq, dtype=I32))
    cos_s, sin_s = _rope_tables(PAST_LEN + jnp.arange(ts, dtype=I32))

    x1_p, conv_p, k_p, v_p = _mixer_prompt(x_prompt, mod_p, cos_p, sin_p, win, w_conv[0], wco, wao, attn_sinks[0], wmo)
    x1_s, conv_s, k_s, v_s = _mixer_sample(
        x_sample, mod_s, cos_s, sin_s, cache_k[0].reshape(bs, WINDOW, KV_DIM), cache_v[0].reshape(bs, WINDOW, KV_DIM),
        state_conv[0], win, w_conv[0], wco, wao, attn_sinks[0], wmo)

    n_p, n_s = bp * seq, bs * ts
    n = n_p + n_s
    mod2_p, mod2_s = mod[:bp, None, 3 * d:], mod[bp:, None, 3 * d:]
    x1_pc = x1_p.reshape(n_p // CHUNK, CHUNK, d)
    wsg, wsu, wsd = (w[0].astype(BF16) for w in (w_sh_gate, w_sh_up, w_sh_down))
    wr_t, rb = w_router[0].T, router_bias[0][:, None]
    wr_hi = wr_t.astype(BF16)
    wr_lo = (wr_t - wr_hi.astype(F32)).astype(BF16)

    ncp, ncs = n_p // CHUNK, n_s // CHUNK
    half = (ncp + ncs) // 2
    tile_chunks = max(PRE_TILE, COMB_TILE) // CHUNK
    assert half <= ncp and half % tile_chunks == 0 and (ncp - half) % tile_chunks == 0 and ncs % tile_chunks == 0
    assert (half * CHUNK) % (SC_WORKERS * SC_WINDOW) == 0 and (half * CHUNK) % RANK_TILE == 0
    assert (half * CHUNK) % (SC_WORKERS * 2 * SC_SUM_TOKENS) == 0
    y_p = None
    for p0, np_c, ns_c in ((0, half, 0), (half, ncp - half, ncs)):
        h2, base, cw = _pre(x1_pc, p0, np_c, mod2_p, x1_s, ns_c, mod2_s, wsg, wsu, wsd, wr_hi, wr_lo, rb)
        routed = _routed_ffn(h2, cw, w_exp_gate[0], w_exp_up[0], w_exp_down[0]).reshape(base.shape)
        y_p = _combine(base, mod2_p, routed, final_gain, 0, np_c, ncp, p0, out_buf=y_p)
        if ns_c:
            y_s = _combine(base, mod2_s, routed, final_gain, np_c, ns_c, ncs, 0)

    kv = lambda a: a.reshape(1, a.shape[0], WINDOW, N_KV_HEADS, HEAD_DIM)
    return (y_p.reshape(bp, seq, d), y_s, conv_p[None], kv(k_p), kv(v_p), conv_s[None], kv(k_s), kv(v_s))
```

```python
import functools

import jax
import jax.numpy as jnp
from jax import lax
from jax.experimental import pallas as pl
from jax.experimental.pallas import tpu as pltpu
from jax.experimental.pallas import tpu_sc as plsc

F32 = jnp.float32
BF16 = jnp.bfloat16
I32 = jnp.int32

D_MODEL = 1024
CHUNK = 64
D_CONV = 1024
CONV_W = 3
N_HEADS = 16
N_KV_HEADS = 4
HEAD_DIM = 64
GQA_GROUP = N_HEADS // N_KV_HEADS
WINDOW = 128
ROPE_THETA = 10000.0
ATTN_SCALE = HEAD_DIM ** -0.5
N_EXPERTS = 64
TOP_K = 8
N_EXPERT_GROUPS = 8
GROUP_SIZE = N_EXPERTS // N_EXPERT_GROUPS
TOPK_GROUPS = 4
D_EXPERT = 256
D_SHARED = 256
ROUTED_SCALE = 2.5
EPS = 1e-6
PAST_LEN = 4096
Q_DIM = N_HEADS * HEAD_DIM
KV_DIM = N_KV_HEADS * HEAD_DIM
OFF_GB, OFF_GC, OFF_XC, OFF_Q, OFF_K, OFF_V, OFF_GCONV, OFF_GATTN, D_IN = (
    0, 1024, 2048, 3072, 4096, 4352, 4608, 5632, 6656)

LANES = 128
SUBLANES = 8
VMEM_LIMIT = 56 * 1024 * 1024

MIX_TILE = 512
ATT_Q = 128
MIX_SIDE_COLS = 256
SAMPLE_BB = 8
PRE_TILE = 512
RANK_TILE = 512
GMM_SUB = 512
GMM_TAIL = 128
GMM_X_SLOTS = 3
COMB_TILE = 256
FFN_SET_A_SHARE = (2, 3)
SC_WORKERS = 32
SC_WINDOW = 96
SC_LANES = 16
SC_SUM_TOKENS = 8
SC_SUM_VREGS = 16


def _const_spec(shape):
    nd = len(shape)
    return pl.BlockSpec(shape, lambda *_: (0,) * nd, pipeline_mode=pl.Buffered(1))


def _rms(x):
    return x * lax.rsqrt(jnp.mean(x * x, axis=-1, keepdims=True) + EPS)


def _sigmoid(x):
    return 1.0 / (1.0 + jnp.exp(-x))


def _silu(x):
    return x * _sigmoid(x)


def _dot(a, b):
    return jnp.dot(a, b, preferred_element_type=F32)


def _pack_bf16_pairs(x):
    half = x.shape[-1] // 2
    lo = lax.bitcast_convert_type(x[..., :half].astype(BF16).astype(F32), I32)
    hi = lax.bitcast_convert_type(x[..., half:].astype(BF16).astype(F32), I32)
    return lax.shift_right_logical(lo, 16) | hi


def _unpack_bf16_pairs(words):
    lo = lax.bitcast_convert_type(lax.shift_left(words, 16), F32)
    hi = lax.bitcast_convert_type(words & jnp.int32(-65536), F32)
    return lo, hi


def _ada_kernel(c_ref, w_ref, b_ref, o_ref):
    s = _silu(c_ref[...]).astype(BF16)
    o_ref[...] = _dot(s, w_ref[...].astype(BF16)) + b_ref[...]


def _ada(c_all, w_ada, b_ada):
    rows = c_all.shape[0]
    n_out = w_ada.shape[1]
    bn = 768
    return pl.pallas_call(
        _ada_kernel,
        grid=(n_out // bn,),
        in_specs=[pl.BlockSpec((rows, D_MODEL), lambda i: (0, 0)),
                  pl.BlockSpec((D_MODEL, bn), lambda i: (0, i)),
                  pl.BlockSpec((1, bn), lambda i: (0, i))],
        out_specs=pl.BlockSpec((rows, bn), lambda i: (0, i)),
        out_shape=jax.ShapeDtypeStruct((rows, n_out), F32),
        name="ada_mod",
    )(c_all, w_ada, b_ada.reshape(1, n_out))


def _rope(x, cos, sin_signed):
    lane = lax.broadcasted_iota(I32, (x.shape[0], LANES), 1)
    first_half = (lane % HEAD_DIM) < (HEAD_DIM // 2)
    outs = []
    for g in range(x.shape[1] // LANES):
        xg = x[:, g * LANES:(g + 1) * LANES]
        up = pltpu.roll(xg, LANES - HEAD_DIM // 2, axis=1)
        down = pltpu.roll(xg, HEAD_DIM // 2, axis=1)
        partner = jnp.where(first_half, up, down)
        outs.append(xg * cos + partner * sin_signed)
    return jnp.concatenate(outs, axis=1)


def _attention(blocks, sinks_ref, obuf, between):
    rq = GQA_GROUP * ATT_Q
    low = lax.broadcasted_iota(I32, (ATT_Q, LANES), 1) < HEAD_DIM
    head_of_lane = lax.broadcasted_iota(I32, (1, rq), 1) // ATT_Q
    units = [(b, pair, par) for b in range(len(blocks)) for pair in range(N_KV_HEADS // 2) for par in range(2)]
    loaded = {}

    def scores(u):
        b, pair, par = units[u]
        if b not in loaded:
            loaded.clear()
            loaded[b] = blocks[b]()
        q_blk, k_of_pair, _, mask, _ = loaded[b]
        keep = low if par == 0 else jnp.logical_not(low)
        cols = [q_blk[:, (GQA_GROUP * pair + i) * LANES:(GQA_GROUP * pair + i + 1) * LANES] for i in range(GQA_GROUP)]
        qg = jnp.concatenate([jnp.where(keep, c, jnp.zeros_like(c)) for c in cols], axis=0)
        st = lax.dot_general(k_of_pair(pair), qg, (((1,), (1,)), ((), ())), preferred_element_type=F32)
        vt = loaded[b][2](pair)[par * HEAD_DIM:(par + 1) * HEAD_DIM, :]
        return jnp.where(mask, st, -jnp.inf), vt, loaded[b][4]

    outs = []
    nxt = scores(0)
    for u, (b, pair, par) in enumerate(units):
        st, vt, row0 = nxt
        if u + 1 < len(units):
            nxt = scores(u + 1)
        if between:
            between.pop(0)()
        g = 2 * pair + par
        sink = jnp.full((1, rq), sinks_ref[g * GQA_GROUP + GQA_GROUP - 1], F32)
        for i in range(GQA_GROUP - 2, -1, -1):
            sink = jnp.where(head_of_lane == i, sinks_ref[g * GQA_GROUP + i], sink)
        m = jnp.maximum(jnp.max(st, axis=0, keepdims=True), sink)
        e = jnp.exp(st - m)
        z = jnp.sum(e, axis=0, keepdims=True) + jnp.exp(sink - m)
        outs.append(_dot(vt, e.astype(BF16)) / z)
        if par == 1:
            for i in range(GQA_GROUP):
                blk = jnp.concatenate([o[:, i * ATT_Q:(i + 1) * ATT_Q] for o in outs], axis=0)
                c0 = (GQA_GROUP * pair + i) * LANES
                obuf[row0:row0 + ATT_Q, c0:c0 + LANES] = blk.T
            outs = []
    for step in between:
        step()


def _in_proj(hb, win_refs, lo, hi):
    w_pre, w_q, w_post = win_refs
    if hi <= OFF_Q:
        return _dot(hb, w_pre[:, lo:hi])
    if lo >= OFF_K:
        return _dot(hb, w_post[:, lo - OFF_K:hi - OFF_K])
    assert (lo, hi) == (OFF_Q, OFF_K)
    return _dot(hb, w_q[...])


def _attention_free_steps(hb, conv, win_ref, wco_ref):
    out, parts = {}, {}
    n_parts = D_MODEL // MIX_SIDE_COLS

    def step(name, piece, compute):
        def run():
            parts.setdefault(name, []).append(compute(piece * MIX_SIDE_COLS, (piece + 1) * MIX_SIDE_COLS))
            if piece == n_parts - 1:
                out[name] = jnp.concatenate(parts.pop(name), axis=1)
        return run

    def conv_in():
        if "conv_in" not in out:
            out["conv_in"] = (out.pop("gate_b") * conv).astype(BF16)
        return out["conv_in"]

    computes = [("gate_b", lambda lo, hi: _in_proj(hb, win_ref, OFF_GB + lo, OFF_GB + hi)),
                ("g_conv", lambda lo, hi: _in_proj(hb, win_ref, OFF_GCONV + lo, OFF_GCONV + hi)),
                ("g_attn", lambda lo, hi: _in_proj(hb, win_ref, OFF_GATTN + lo, OFF_GATTN + hi)),
                ("y_conv", lambda lo, hi: _dot(conv_in(), wco_ref[:, lo:hi]))]
    return [step(name, p, fn) for name, fn in computes for p in range(n_parts)], out


def _mix_out(x, g1, side, y_attn_in, wao_ref, wmo_ref):
    y_attn = _dot(y_attn_in.astype(BF16), wao_ref[...])
    merged = _sigmoid(side["g_conv"]) * side["y_conv"] + _sigmoid(side["g_attn"]) * y_attn
    return x + g1 * _dot(merged.astype(BF16), wmo_ref[...])


def _mixer_prompt_kernel(x_ref, mod_ref, cos_ref, sin_ref, wpre_ref, wq_ref, wpost_ref, wconv_ref, wco_ref, wao_ref,
                         sinks_ref, wmo_ref, x1_ref, conv_ref, k_ref, v_ref, ubuf, kbuf, vtbuf, obuf):
    win_ref = (wpre_ref, wq_ref, wpost_ref)
    j = pl.program_id(1)
    t = x_ref.shape[1]

    @pl.when(j == 0)
    def _():
        ubuf[0:SUBLANES, :] = jnp.zeros((SUBLANES, D_CONV), F32)
        kbuf[0:WINDOW, :] = jnp.zeros((WINDOW, KV_DIM), BF16)
        vtbuf[:, 0:WINDOW] = jnp.zeros((KV_DIM, WINDOW), BF16)

    x = x_ref[0]
    mod = mod_ref[0]
    sh1, sc1, g1 = mod[:, 0:D_MODEL], mod[:, D_MODEL:2 * D_MODEL], mod[:, 2 * D_MODEL:3 * D_MODEL]
    hb = (_rms(x) * (1.0 + sc1) + sh1).astype(BF16)

    u = _in_proj(hb, win_ref, OFF_GC, OFF_XC) * _in_proj(hb, win_ref, OFF_XC, OFF_Q)
    ubuf[SUBLANES:SUBLANES + t, :] = u
    wc = wconv_ref[...]
    conv = wc[0:1] * ubuf[SUBLANES - 2:SUBLANES - 2 + t, :] + wc[1:2] * ubuf[SUBLANES - 1:SUBLANES - 1 + t, :] + wc[2:3] * u
    conv_ref[0] = u[t - (CONV_W - 1):t]
    ubuf[SUBLANES - 2:SUBLANES, :] = u[t - (CONV_W - 1):t]

    cos, sin = cos_ref[...], sin_ref[...]
    q = (_rope(_in_proj(hb, win_ref, OFF_Q, OFF_K), cos, sin) * ATTN_SCALE).astype(BF16)
    k = _rope(_in_proj(hb, win_ref, OFF_K, OFF_V), cos, sin)
    v = _in_proj(hb, win_ref, OFF_V, OFF_GCONV)
    kbuf[WINDOW:WINDOW + t, :] = k.astype(BF16)
    vtbuf[:, WINDOW:WINDOW + t] = v.T.astype(BF16)
    k_ref[0] = k[t - WINDOW:t]
    v_ref[0] = v[t - WINDOW:t]

    nkeys = ATT_Q + WINDOW
    rq = GQA_GROUP * ATT_Q
    ki = lax.broadcasted_iota(I32, (nkeys, rq), 0)
    qi = lax.broadcasted_iota(I32, (nkeys, rq), 1) % ATT_Q
    band = ki // CHUNK - qi // CHUNK
    band_ok = (band >= 0) & (band <= WINDOW // CHUNK)
    def block(s):
        def load():
            mask = band_ok & (ki + (j * t + s * ATT_Q - WINDOW) >= 0)
            k_of_pair = lambda pair: kbuf[s * ATT_Q:s * ATT_Q + nkeys, pair * LANES:(pair + 1) * LANES]
            vt_of_pair = lambda pair: vtbuf[pair * LANES:(pair + 1) * LANES, s * ATT_Q:s * ATT_Q + nkeys]
            return q[s * ATT_Q:(s + 1) * ATT_Q], k_of_pair, vt_of_pair, mask, s * ATT_Q
        return load

    steps, side = _attention_free_steps(hb, conv, win_ref, wco_ref)
    _attention([block(s) for s in range(t // ATT_Q)], sinks_ref, obuf, steps)
    kbuf[0:WINDOW, :] = kbuf[t:t + WINDOW, :]
    vtbuf[:, 0:WINDOW] = vtbuf[:, t:t + WINDOW]

    x1_ref[0] = _mix_out(x, g1, side, obuf[...], wao_ref, wmo_ref)


def _mixer_prompt(x, mod, cos, sin, win, wconv, wco, wao, sinks, wmo):
    b, seq, d = x.shape
    t = MIX_TILE
    return pl.pallas_call(
        _mixer_prompt_kernel,
        grid=(b, seq // t),
        in_specs=[pl.BlockSpec((1, t, d), lambda i, j: (i, j, 0)),
                  pl.BlockSpec((1, 1, 6 * d), lambda i, j: (i, 0, 0)),
                  pl.BlockSpec((t, LANES), lambda i, j: (j, 0)),
                  pl.BlockSpec((t, LANES), lambda i, j: (j, 0)),
                  *[_const_spec(w.shape) for w in win],
                  _const_spec(wconv.shape), _const_spec(wco.shape), _const_spec(wao.shape),
                  pl.BlockSpec(memory_space=pltpu.SMEM),
                  _const_spec(wmo.shape)],
        out_specs=[pl.BlockSpec((1, t, d), lambda i, j: (i, j, 0)),
                   pl.BlockSpec((1, CONV_W - 1, D_CONV), lambda i, j: (i, 0, 0)),
                   pl.BlockSpec((1, WINDOW, KV_DIM), lambda i, j: (i, 0, 0)),
                   pl.BlockSpec((1, WINDOW, KV_DIM), lambda i, j: (i, 0, 0))],
        out_shape=[jax.ShapeDtypeStruct((b, seq, d), F32),
                   jax.ShapeDtypeStruct((b, CONV_W - 1, D_CONV), F32),
                   jax.ShapeDtypeStruct((b, WINDOW, KV_DIM), F32),
                   jax.ShapeDtypeStruct((b, WINDOW, KV_DIM), F32)],
        scratch_shapes=[pltpu.VMEM((SUBLANES + t, D_CONV), F32),
                        pltpu.VMEM((WINDOW + t, KV_DIM), BF16),
                        pltpu.VMEM((KV_DIM, WINDOW + t), BF16),
                        pltpu.VMEM((t, Q_DIM), F32)],
        compiler_params=pltpu.CompilerParams(dimension_semantics=("arbitrary", "arbitrary"),
                                             vmem_limit_bytes=VMEM_LIMIT),
        name="mixer_prompt",
    )(x, mod, cos, sin, *win, wconv, wco, wao, sinks, wmo)


def _mixer_sample_kernel(x_ref, mod_ref, cos_ref, sin_ref, ck_ref, cv_ref, sconv_ref, wpre_ref, wq_ref, wpost_ref,
                         wconv_ref, wco_ref, wao_ref, sinks_ref, wmo_ref, x1_ref, conv_ref, k_ref, v_ref, ubuf, obuf):
    win_ref = (wpre_ref, wq_ref, wpost_ref)
    bb, t, d = x_ref.shape
    x3 = x_ref[...]
    mod = mod_ref[...]
    sh1, sc1, g1 = mod[:, :, 0:d], mod[:, :, d:2 * d], mod[:, :, 2 * d:3 * d]
    x = x3.reshape(bb * t, d)
    hb = (_rms(x3) * (1.0 + sc1) + sh1).astype(BF16).reshape(bb * t, d)

    u = _in_proj(hb, win_ref, OFF_GC, OFF_XC) * _in_proj(hb, win_ref, OFF_XC, OFF_Q)
    u3 = u.reshape(bb, t, D_CONV)
    ubuf[:, SUBLANES - 2:SUBLANES, :] = sconv_ref[...]
    ubuf[:, SUBLANES:SUBLANES + t, :] = u3
    wc = wconv_ref[...]
    conv = (wc[0:1] * ubuf[:, SUBLANES - 2:SUBLANES - 2 + t, :] + wc[1:2] * ubuf[:, SUBLANES - 1:SUBLANES - 1 + t, :]
            + wc[2:3] * u3).reshape(bb * t, D_CONV)
    conv_ref[...] = u3[:, t - (CONV_W - 1):t, :]

    cos = jnp.concatenate([cos_ref[...]] * bb, axis=0)
    sin = jnp.concatenate([sin_ref[...]] * bb, axis=0)
    q = (_rope(_in_proj(hb, win_ref, OFF_Q, OFF_K), cos, sin) * ATTN_SCALE).astype(BF16)
    k = _rope(_in_proj(hb, win_ref, OFF_K, OFF_V), cos, sin)
    v = _in_proj(hb, win_ref, OFF_V, OFF_GCONV)
    per = ATT_Q // t
    nkeys = per * (WINDOW + t)
    rq = GQA_GROUP * ATT_Q
    key_stream = lax.broadcasted_iota(I32, (nkeys, rq), 0) // (WINDOW + t)
    query_stream = (lax.broadcasted_iota(I32, (nkeys, rq), 1) % ATT_Q) // t
    mask = key_stream == query_stream
    def block(blk):
        def load():
            k_parts, v_parts = [], []
            for b in range(blk * per, (blk + 1) * per):
                kb, vb = k[b * t:(b + 1) * t], v[b * t:(b + 1) * t]
                ck, cv = ck_ref[b], cv_ref[b]
                k_ref[b] = jnp.concatenate([ck[t:WINDOW], kb], axis=0)
                v_ref[b] = jnp.concatenate([cv[t:WINDOW], vb], axis=0)
                k_parts += [ck, kb]
                v_parts += [cv, vb]
            k_all = jnp.concatenate(k_parts, axis=0).astype(BF16)
            vt_all = jnp.concatenate(v_parts, axis=0).T.astype(BF16)
            k_of_pair = lambda pair: k_all[:, pair * LANES:(pair + 1) * LANES]
            vt_of_pair = lambda pair: vt_all[pair * LANES:(pair + 1) * LANES, :]
            return q[blk * ATT_Q:(blk + 1) * ATT_Q], k_of_pair, vt_of_pair, mask, blk * ATT_Q
        return load

    steps, side = _attention_free_steps(hb, conv, win_ref, wco_ref)
    _attention([block(blk) for blk in range(bb // per)], sinks_ref, obuf, steps)

    g1f = jnp.broadcast_to(g1, (bb, t, d)).reshape(bb * t, d)
    x1_ref[...] = _mix_out(x, g1f, side, obuf[...], wao_ref, wmo_ref).reshape(bb, t, d)


def _mixer_sample(x, mod, cos, sin, ck, cv, sconv, win, wconv, wco, wao, sinks, wmo):
    b, t, d = x.shape
    bb = SAMPLE_BB
    blk = lambda *s: pl.BlockSpec((bb,) + s, lambda i: (i, 0, 0))
    return pl.pallas_call(
        _mixer_sample_kernel,
        grid=(b // bb,),
        in_specs=[blk(t, d), blk(1, 6 * d),
                  pl.BlockSpec((t, LANES), lambda i: (0, 0)), pl.BlockSpec((t, LANES), lambda i: (0, 0)),
                  blk(WINDOW, KV_DIM), blk(WINDOW, KV_DIM), blk(CONV_W - 1, D_CONV),
                  *[_const_spec(w.shape) for w in win],
                  _const_spec(wconv.shape), _const_spec(wco.shape), _const_spec(wao.shape),
                  pl.BlockSpec(memory_space=pltpu.SMEM),
                  _const_spec(wmo.shape)],
        out_specs=[blk(t, d), blk(CONV_W - 1, D_CONV), blk(WINDOW, KV_DIM), blk(WINDOW, KV_DIM)],
        out_shape=[jax.ShapeDtypeStruct((b, t, d), F32),
                   jax.ShapeDtypeStruct((b, CONV_W - 1, D_CONV), F32),
                   jax.ShapeDtypeStruct((b, WINDOW, KV_DIM), F32),
                   jax.ShapeDtypeStruct((b, WINDOW, KV_DIM), F32)],
        scratch_shapes=[pltpu.VMEM((bb, SUBLANES + t, D_CONV), F32),
                        pltpu.VMEM((bb * t, Q_DIM), F32)],
        compiler_params=pltpu.CompilerParams(dimension_semantics=("arbitrary",), vmem_limit_bytes=VMEM_LIMIT),
        name="mixer_sample",
    )(x, mod, cos, sin, ck, cv, sconv, *win, wconv, wco, wao, sinks, wmo)


def _pre_kernel(*refs, prompt_tiles, has_sample):
    if has_sample:
        xp_ref, mp_ref, xs_ref, ms_ref, wsg_ref, wsu_ref, wsd_ref, wrh_ref, wrl_ref, rb_ref, h2_ref, base_ref, cw_ref = refs
    else:
        xp_ref, mp_ref, wsg_ref, wsu_ref, wsd_ref, wrh_ref, wrl_ref, rb_ref, h2_ref, base_ref, cw_ref = refs
    nc, c, d = xp_ref.shape
    t = nc * c
    x3, mod = xp_ref[...], mp_ref[...]
    if has_sample:
        is_prompt = pl.program_id(0) < prompt_tiles
        x3 = jnp.where(is_prompt, x3, xs_ref[...])
        mod = jnp.where(is_prompt, mod, ms_ref[...])
    sh2, sc2, g2 = mod[:, :, 0:d], mod[:, :, d:2 * d], mod[:, :, 2 * d:3 * d]
    h3 = _rms(x3) * (1.0 + sc2) + sh2
    h2 = h3.reshape(t, d)
    hb = h2.astype(BF16)
    h2_ref[...] = _pack_bf16_pairs(h2)
    shared = _dot((_silu(_dot(hb, wsg_ref[...])) * _dot(hb, wsu_ref[...])).astype(BF16), wsd_ref[...])
    base_ref[...] = x3 + g2 * shared.reshape(nc, c, d)

    h_lo = (h2 - hb.astype(F32)).astype(BF16)
    nt = lambda a, b: lax.dot_general(a, b, (((1,), (1,)), ((), ())), preferred_element_type=F32)
    logits = nt(wrh_ref[...], hb) + (nt(wrh_ref[...], h_lo) + nt(wrl_ref[...], hb))
    scores = _sigmoid(logits)
    biased = scores + rb_ref[...]
    g3 = biased.reshape(N_EXPERT_GROUPS, GROUP_SIZE, t)
    member = lax.broadcasted_iota(I32, g3.shape, 1)
    m1 = jnp.max(g3, axis=1, keepdims=True)
    first = jnp.min(jnp.where(g3 == m1, member, GROUP_SIZE), axis=1, keepdims=True)
    m2 = jnp.max(jnp.where(member == first, -jnp.inf, g3), axis=1, keepdims=True)
    gs = m1 + m2
    gidx = lax.broadcasted_iota(I32, gs.shape, 0)
    grank = jnp.zeros(gs.shape, I32)
    for o in range(N_EXPERT_GROUPS):
        other = gs[o:o + 1]
        grank += ((other > gs) | ((other == gs) & (o < gidx))).astype(I32)
    group_ok = grank < TOPK_GROUPS
    slot = jnp.zeros((1, 1, t), I32)
    takes = []
    for gi in range(N_EXPERT_GROUPS):
        ok = group_ok[gi:gi + 1]
        takes.append([ok & (slot == s) for s in range(TOPK_GROUPS)])
        slot = slot + ok.astype(I32)
    packed = []
    for s in range(TOPK_GROUPS):
        vals = jnp.zeros((GROUP_SIZE, t), F32)
        for gi in range(N_EXPERT_GROUPS):
            vals = jnp.where(takes[gi][s][0], g3[gi], vals)
        packed.append(vals)
    cand = jnp.concatenate(packed, axis=0)
    cidx = lax.broadcasted_iota(I32, cand.shape, 0)
    crank = jnp.zeros(cand.shape, I32)
    for o in range(TOPK_GROUPS * GROUP_SIZE):
        other = cand[o:o + 1]
        crank += ((other > cand) | ((other == cand) & (o < cidx))).astype(I32)
    chosen = crank < TOP_K
    sel_groups = []
    for gi in range(N_EXPERT_GROUPS):
        hit = jnp.zeros((GROUP_SIZE, t), jnp.bool_)
        for s in range(TOPK_GROUPS):
            hit = hit | (takes[gi][s][0] & chosen[s * GROUP_SIZE:(s + 1) * GROUP_SIZE])
        sel_groups.append(hit)
    sel = jnp.concatenate(sel_groups, axis=0)
    ssum = jnp.sum(jnp.where(sel, scores, 0.0), axis=0, keepdims=True)
    cw_ref[...] = jnp.where(sel, scores / ssum * ROUTED_SCALE, -1.0)


def _pre(x1_p, p_chunk0, ncp, mod_p, x1_s, ncs, mod_s, wsg, wsu, wsd, wr_hi, wr_lo, rb):
    ncp_all, c, d = x1_p.shape
    nc = PRE_TILE // c
    nchunks = ncp + ncs
    n = nchunks * c
    pt, p0 = ncp // nc, p_chunk0 // nc
    tiles_per_stream = ncp_all // mod_p.shape[0] // nc
    blk3 = pl.BlockSpec((nc, c, d), lambda i: (i, 0, 0))
    p_tile = lambda i: p0 + jnp.minimum(i, pt - 1)
    s_tile = lambda i: jnp.maximum(i - pt, 0)
    s_args, s_specs = [], []
    if ncs:
        s_args = [x1_s, mod_s]
        s_specs = [pl.BlockSpec((nc, c, d), lambda i: (s_tile(i), 0, 0)),
                   pl.BlockSpec((nc, 1, 3 * d), lambda i: (s_tile(i), 0, 0))]
    return pl.pallas_call(
        functools.partial(_pre_kernel, prompt_tiles=pt, has_sample=bool(ncs)),
        grid=(nchunks // nc,),
        in_specs=[pl.BlockSpec((nc, c, d), lambda i: (p_tile(i), 0, 0)),
                  pl.BlockSpec((1, 1, 3 * d), lambda i: (p_tile(i) // tiles_per_stream, 0, 0))] + s_specs + [
                  _const_spec(wsg.shape), _const_spec(wsu.shape), _const_spec(wsd.shape),
                  _const_spec(wr_hi.shape), _const_spec(wr_lo.shape), _const_spec(rb.shape)],
        out_specs=[pl.BlockSpec((nc * c, d // 2), lambda i: (i, 0)), blk3,
                   pl.BlockSpec((N_EXPERTS, nc * c), lambda i: (0, i))],
        out_shape=[jax.ShapeDtypeStruct((n, d // 2), I32),
                   jax.ShapeDtypeStruct((nchunks, c, d), F32),
                   jax.ShapeDtypeStruct((N_EXPERTS, n), F32)],
        compiler_params=pltpu.CompilerParams(dimension_semantics=("arbitrary",), vmem_limit_bytes=VMEM_LIMIT),
        name="pre_ffn",
    )(x1_p, mod_p, *s_args, wsg, wsu, wsd, wr_hi, wr_lo, rb)


def _rank_kernel(cw_ref, rank_ref, cnt_ref, carry):
    i = pl.program_id(0)
    t = cw_ref.shape[1]

    @pl.when(i == 0)
    def _():
        carry[...] = jnp.zeros(carry.shape, F32)

    sel = (cw_ref[...] >= 0.0).astype(BF16)
    r = lax.broadcasted_iota(I32, (t, t), 0)
    c = lax.broadcasted_iota(I32, (t, t), 1)
    before = (r < c).astype(BF16)
    rank = carry[...] + _dot(sel, before)
    rank_ref[...] = rank.astype(I32)
    carry[...] = carry[...] + jnp.sum(sel.astype(F32), axis=1, keepdims=True)
    cnt_ref[...] = carry[...].astype(I32)


def _rank(cw):
    e, n = cw.shape
    t = RANK_TILE
    return pl.pallas_call(
        _rank_kernel,
        grid=(n // t,),
        in_specs=[pl.BlockSpec((e, t), lambda i: (0, i))],
        out_specs=[pl.BlockSpec((e, t), lambda i: (0, i)), pl.BlockSpec((e, 1), lambda i: (0, 0))],
        out_shape=[jax.ShapeDtypeStruct((e, n), I32), jax.ShapeDtypeStruct((e, 1), I32)],
        scratch_shapes=[pltpu.VMEM((e, 1), F32)],
        compiler_params=pltpu.CompilerParams(dimension_semantics=("arbitrary",)),
        name="expert_rank",
    )(cw)


def _slot_kernel(cw_ref, rank_ref, start_ref, pos_ref, w_ref, pos_tok_ref):
    cw = cw_ref[...]
    e, t = cw.shape
    sel = cw >= 0.0
    r = lax.broadcasted_iota(I32, (e, e), 0)
    c = lax.broadcasted_iota(I32, (e, e), 1)
    lower = (c < r).astype(BF16)
    kidx = _dot(lower, sel.astype(BF16))
    posf = start_ref[...].astype(F32) + rank_ref[...].astype(F32)
    pos_rows, w_rows = [], []
    for k in range(TOP_K):
        m = sel & (kidx == float(k))
        pos_rows.append(jnp.sum(jnp.where(m, posf, 0.0), axis=0, keepdims=True))
        w_rows.append(jnp.sum(jnp.where(m, cw, 0.0), axis=0, keepdims=True))
    pos_ref[...] = jnp.concatenate(pos_rows, axis=0).astype(I32)
    stack = jnp.concatenate(w_rows + pos_rows + [jnp.zeros((LANES - 2 * TOP_K, t), F32)], axis=0).T
    w_ref[...] = stack[:, :TOP_K]
    pos_tok_ref[...] = stack[:, TOP_K:2 * TOP_K].astype(I32)


def _slots(cw, rank, seg_start):
    e, n = cw.shape
    t = RANK_TILE
    return pl.pallas_call(
        _slot_kernel,
        grid=(n // t,),
        in_specs=[pl.BlockSpec((e, t), lambda i: (0, i)), pl.BlockSpec((e, t), lambda i: (0, i)),
                  pl.BlockSpec((e, 1), lambda i: (0, 0))],
        out_specs=[pl.BlockSpec((TOP_K, t), lambda i: (0, i)), pl.BlockSpec((t, TOP_K), lambda i: (i, 0)),
                   pl.BlockSpec((t, TOP_K), lambda i: (i, 0))],
        out_shape=[jax.ShapeDtypeStruct((TOP_K, n), I32), jax.ShapeDtypeStruct((n, TOP_K), F32),
                   jax.ShapeDtypeStruct((n, TOP_K), I32)],
        compiler_params=pltpu.CompilerParams(dimension_semantics=("arbitrary",)),
        name="expert_slots",
    )(cw, rank, seg_start)


def _sc_mesh():
    return plsc.VectorSubcoreMesh(core_axis_name="c", subcore_axis_name="s")


def _sc_worker_id():
    return lax.axis_index("s") * (SC_WORKERS // 16) + lax.axis_index("c")


def _sc_dispatch(rows, pos, n_rows):
    n, d = rows.shape
    per_w = n // SC_WORKERS
    w = SC_WINDOW
    n_chunks = per_w // w

    @functools.partial(
        pl.kernel, mesh=_sc_mesh(),
        out_type=jax.ShapeDtypeStruct((n_rows, d), rows.dtype),
        scratch_types=[pltpu.VMEM((2, TOP_K, w), I32), pltpu.VMEM((2, w, d), rows.dtype),
                       pltpu.SemaphoreType.DMA((2,)), pltpu.SemaphoreType.DMA((2,)), pltpu.SemaphoreType.DMA((2,))],
        name="sc_dispatch")
    def k(rows_hbm, pos_hbm, o_hbm, idx_v, rows_v, row_sem, idx_sem, out_sem):
        wid = _sc_worker_id()
        base = wid * per_w

        def loads(c, slot):
            off = pl.multiple_of(base + c * w, SUBLANES)
            return (pltpu.make_async_copy(rows_hbm.at[pl.ds(off, w)], rows_v.at[slot], row_sem.at[slot]),
                    pltpu.make_async_copy(pos_hbm.at[wid * n_chunks + c], idx_v.at[slot], idx_sem.at[slot]))

        def scatters(slot):
            return [pltpu.make_async_copy(rows_v.at[slot], o_hbm.at[idx_v.at[slot, kk]], out_sem.at[slot])
                    for kk in range(TOP_K)]

        for cp in loads(0, 0):
            cp.start()
        for c in range(n_chunks):
            slot = c % 2
            for cp in loads(c, slot):
                cp.wait()
            for cp in scatters(slot):
                cp.start()
            if c >= 1:
                for cp in scatters(1 - slot):
                    cp.wait()
            if c + 1 < n_chunks:
                for cp in loads(c + 1, 1 - slot):
                    cp.start()
        for cp in scatters((n_chunks - 1) % 2):
            cp.wait()

    pos_chunks = pos.reshape(TOP_K, n // w, w).transpose(1, 0, 2)
    return k(rows, pos_chunks)


def _sc_collect_sum(rows, pos_tok, w_lanes):
    words = rows.shape[1]
    n = w_lanes.shape[0]
    lanes = SC_LANES
    per_w = n // SC_WORKERS
    tw = SC_SUM_TOKENS
    n_pairs = per_w // (2 * tw)
    col_blocks = words // lanes // SC_SUM_VREGS

    @functools.partial(
        pl.kernel, mesh=_sc_mesh(),
        out_type=jax.ShapeDtypeStruct((n, 2 * words), F32),
        scratch_types=[pltpu.VMEM((per_w * TOP_K,), I32), pltpu.VMEM((2, tw * TOP_K, words), I32),
                       pltpu.VMEM((2, tw, TOP_K * lanes), F32), pltpu.VMEM((2, tw, 2 * words), F32),
                       pltpu.SemaphoreType.DMA((2,)), pltpu.SemaphoreType.DMA((2,)), pltpu.SemaphoreType.DMA((2,))],
        compiler_params=pltpu.CompilerParams(needs_layout_passes=False),
        name="sc_collect_sum")
    def k(rows_hbm, pos_hbm, w_hbm, o_hbm, idx_v, rows_v, w_v, out_v, in_sem, w_sem, out_sem):
        base = pl.multiple_of(_sc_worker_id() * per_w, SUBLANES)
        pltpu.sync_copy(pos_hbm.at[pl.ds(pl.multiple_of(base * TOP_K, SUBLANES), per_w * TOP_K)], idx_v)

        def loads(c, slot):
            idx = idx_v.at[pl.ds(pl.multiple_of(c * tw * TOP_K, SUBLANES), tw * TOP_K)]
            tok0 = pl.multiple_of(base + c * tw, SUBLANES)
            return (pltpu.make_async_copy(rows_hbm.at[idx], rows_v.at[slot], in_sem.at[slot]),
                    pltpu.make_async_copy(w_hbm.at[pl.ds(tok0, tw)], w_v.at[slot], w_sem.at[slot]))

        def write(c, slot):
            tok0 = pl.multiple_of(base + c * tw, SUBLANES)
            return pltpu.make_async_copy(out_v.at[slot], o_hbm.at[pl.ds(tok0, tw)], out_sem.at[slot])

        high_half = jnp.full((lanes,), -65536, I32)
        sixteen = jnp.full((lanes,), 16, I32)

        def reduce_window(slot):
            rv, wv, ov = rows_v.at[slot], w_v.at[slot], out_v.at[slot]

            @pl.loop(0, tw)
            def _(t):
                for cb in range(col_blocks):
                    acc_lo, acc_hi = [None] * SC_SUM_VREGS, [None] * SC_SUM_VREGS
                    for kk in range(TOP_K):
                        wk = wv[t, pl.ds(kk * lanes, lanes)]
                        for c in range(SC_SUM_VREGS):
                            wd = rv[t * TOP_K + kk, pl.ds((cb * SC_SUM_VREGS + c) * lanes, lanes)]
                            lo = wk * plsc.bitcast(lax.shift_left(wd, sixteen), F32)
                            hi = wk * plsc.bitcast(wd & high_half, F32)
                            acc_lo[c] = lo if kk == 0 else acc_lo[c] + lo
                            acc_hi[c] = hi if kk == 0 else acc_hi[c] + hi
                    for c in range(SC_SUM_VREGS):
                        col = (cb * SC_SUM_VREGS + c) * lanes
                        ov[t, pl.ds(col, lanes)] = acc_lo[c]
                        ov[t, pl.ds(words + col, lanes)] = acc_hi[c]

        for cp in loads(0, 0):
            cp.start()

        @pl.loop(0, n_pairs)
        def _(p):
            c0 = 2 * p
            for cp in loads(c0 + 1, 1):
                cp.start()
            for cp in loads(c0, 0):
                cp.wait()
            reduce_window(0)
            write(c0, 0).start()
            for cp in loads(c0 + 1, 1):
                cp.wait()
            reduce_window(1)
            write(c0 + 1, 1).start()
            write(c0, 0).wait()

            @pl.when(p + 1 < n_pairs)
            def _():
                for cp in loads(c0 + 2, 0):
                    cp.start()

            write(c0 + 1, 1).wait()

    return k(rows, pos_tok, w_lanes)


def _gmm_kernel(tot_ref, ce_ref, row_ref, val_ref, ord_ref, nxt_ref, x_hbm, wg_hbm, wu_hbm, wd_hbm, y_hbm,
                wgb, wub, wdb, xbuf, xsem, ybuf, ysem, wgf, wuf, wdf, wsem):
    total = tot_ref[0]
    pieces = GMM_SUB // GMM_TAIL

    def w_copies(ex, slot):
        return [pltpu.make_async_copy(src.at[ex], dst.at[slot], wsem.at[slot, i])
                for i, (src, dst) in enumerate(((wg_hbm, wgf), (wu_hbm, wuf), (wd_hbm, wdf)))]

    def x_copy(g):
        slot = g % GMM_X_SLOTS
        rows = pl.ds(pl.multiple_of(row_ref[g], GMM_TAIL), GMM_SUB)
        return pltpu.make_async_copy(x_hbm.at[rows], xbuf.at[slot], xsem.at[slot])

    def y_piece(g, p):
        slot = g % 2
        rows = pl.ds(pl.multiple_of(row_ref[g] + p * GMM_TAIL, GMM_TAIL), GMM_TAIL)
        return pltpu.make_async_copy(ybuf.at[slot, pl.ds(p * GMM_TAIL, GMM_TAIL)], y_hbm.at[rows], ysem.at[slot])

    def for_y_pieces(g, action):
        for p in range(pieces):
            @pl.when(p * GMM_TAIL < val_ref[g])
            def _():
                action(y_piece(g, p))

    for ahead in range(GMM_X_SLOTS - 1):
        @pl.when(ahead < total)
        def _():
            x_copy(ahead).start()

    def chunk(g, carry):
        e = ce_ref[g]
        prev = ce_ref[jnp.maximum(g - 1, 0)]

        @pl.when((g == 0) | (e != prev))
        def _():
            slot = ord_ref[e] % 2

            @pl.when(g == 0)
            def _():
                for cp in w_copies(e, slot):
                    cp.start()

            for cp in w_copies(e, slot):
                cp.wait()
            wgb[...] = wgf[slot].astype(BF16)
            wub[...] = wuf[slot].astype(BF16)
            wdb[...] = wdf[slot].astype(BF16)
            nxt = nxt_ref[e]

            @pl.when(nxt >= 0)
            def _():
                for cp in w_copies(nxt, 1 - slot):
                    cp.start()

        x_copy(g).wait()

        @pl.when(g + GMM_X_SLOTS - 1 < total)
        def _():
            x_copy(g + GMM_X_SLOTS - 1).start()

        @pl.when(g >= 2)
        def _():
            for_y_pieces(g - 2, lambda cp: cp.wait())

        x_ref = xbuf.at[g % GMM_X_SLOTS]
        y_ref = ybuf.at[g % 2]

        def expert_rows(r0, n):
            rows = pl.ds(r0, n)
            lo, hi = _unpack_bf16_pairs(x_ref[rows, :])
            xb = jnp.concatenate([lo.astype(BF16), hi.astype(BF16)], axis=1)
            mid = (_silu(_dot(xb, wgb[...])) * _dot(xb, wub[...])).astype(BF16)
            y_ref[rows, :] = _pack_bf16_pairs(_dot(mid, wdb[...]))

        n_real = val_ref[g]

        @pl.when(n_real == GMM_SUB)
        def _():
            expert_rows(0, GMM_SUB)

        @pl.when(n_real < GMM_SUB)
        def _():
            @pl.loop(0, (n_real + GMM_TAIL - 1) // GMM_TAIL)
            def _(i):
                expert_rows(pl.multiple_of(i * GMM_TAIL, GMM_TAIL), GMM_TAIL)

        for_y_pieces(g, lambda cp: cp.start())
        return carry

    lax.fori_loop(0, total, chunk, 0)
    for back in (2, 1):
        @pl.when(total >= back)
        def _():
            for_y_pieces(total - back, lambda cp: cp.wait())


def _gmm(x_sorted, n_chunks, chunk_e, chunk_row, chunk_valid, e_ord, e_next, wg, wu, wd):
    r, half = x_sorted.shape
    d = 2 * half
    any_spec = pl.BlockSpec(memory_space=pl.ANY)
    return pl.pallas_call(
        _gmm_kernel,
        grid_spec=pltpu.PrefetchScalarGridSpec(
            num_scalar_prefetch=6,
            grid=(1,),
            in_specs=[any_spec, any_spec, any_spec, any_spec],
            out_specs=any_spec,
            scratch_shapes=[pltpu.VMEM((d, D_EXPERT), BF16), pltpu.VMEM((d, D_EXPERT), BF16),
                            pltpu.VMEM((D_EXPERT, d), BF16),
                            pltpu.VMEM((GMM_X_SLOTS, GMM_SUB, half), I32), pltpu.SemaphoreType.DMA((GMM_X_SLOTS,)),
                            pltpu.VMEM((2, GMM_SUB, half), I32), pltpu.SemaphoreType.DMA((2,)),
                            pltpu.VMEM((2, d, D_EXPERT), F32), pltpu.VMEM((2, d, D_EXPERT), F32),
                            pltpu.VMEM((2, D_EXPERT, d), F32), pltpu.SemaphoreType.DMA((2, 3))]),
        out_shape=jax.ShapeDtypeStruct((r, half), I32),
        compiler_params=pltpu.CompilerParams(dimension_semantics=("arbitrary",), vmem_limit_bytes=VMEM_LIMIT),
        name="expert_gmm",
    )(n_chunks, chunk_e, chunk_row, chunk_valid, e_ord, e_next, x_sorted, wg, wu, wd)


def _combine_kernel(base_ref, mod_ref, routed_ref, gain_ref, *rest):
    y_ref = rest[-1]
    d = base_ref.shape[-1]
    g2 = mod_ref[...][:, :, 2 * d:3 * d]
    out = base_ref[...] + g2 * routed_ref[...]
    y_ref[...] = _rms(out) * gain_ref[...]


def _combine(base, mod, routed, gain, first_chunk, n_chunks, out_chunks, out_first_chunk, out_buf=None):
    _, c, d = base.shape
    nc = COMB_TILE // c
    t0, o0 = first_chunk // nc, out_first_chunk // nc
    chunks_per_stream = out_chunks // mod.shape[0]
    if chunks_per_stream == 1:
        mod_spec = pl.BlockSpec((nc, 1, 3 * d), lambda i: (o0 + i, 0, 0))
    else:
        assert chunks_per_stream % nc == 0
        mod_spec = pl.BlockSpec((1, 1, 3 * d), lambda i: ((o0 + i) * nc // chunks_per_stream, 0, 0))
    blk3 = pl.BlockSpec((nc, c, d), lambda i: (t0 + i, 0, 0))
    in_specs = [blk3, mod_spec, blk3, pl.BlockSpec((1, 1, d), lambda i: (0, 0, 0))]
    args = [base, mod, routed, gain.reshape(1, 1, d)]
    aliases = {}
    if out_buf is not None:
        in_specs.append(pl.BlockSpec(memory_space=pl.ANY))
        args.append(out_buf)
        aliases = {len(args) - 1: 0}
    return pl.pallas_call(
        _combine_kernel,
        grid=(n_chunks // nc,),
        in_specs=in_specs,
        out_specs=pl.BlockSpec((nc, c, d), lambda i: (o0 + i, 0, 0)),
        out_shape=jax.ShapeDtypeStruct((out_chunks, c, d), F32),
        input_output_aliases=aliases,
        compiler_params=pltpu.CompilerParams(dimension_semantics=("arbitrary",), vmem_limit_bytes=VMEM_LIMIT),
        name="combine_norm",
    )(*args)


def _rope_tables(pos):
    half = HEAD_DIM // 2
    inv_freq = ROPE_THETA ** (-jnp.arange(half, dtype=F32) / half)
    ang = pos.astype(F32)[:, None] * inv_freq[None, :]
    cos, sin = jnp.cos(ang), jnp.sin(ang)
    reps = LANES // HEAD_DIM
    return jnp.tile(jnp.concatenate([cos, cos], axis=1), (1, reps)), jnp.tile(jnp.concatenate([-sin, sin], axis=1), (1, reps))


def _routed_ffn(h2, cw, w_gate, w_up, w_down):
    n, half = h2.shape
    rank, counts = _rank(cw)
    counts = counts[:, 0]
    padded = (counts + GMM_TAIL - 1) // GMM_TAIL * GMM_TAIL
    seg_start = (jnp.cumsum(padded) - padded).astype(I32)
    n_rows = n * TOP_K + N_EXPERTS * GMM_TAIL + GMM_SUB
    e_chunks = (counts + GMM_SUB - 1) // GMM_SUB
    chunk_end = jnp.cumsum(e_chunks)
    max_chunks = n * TOP_K // GMM_SUB + N_EXPERTS
    g = jnp.arange(max_chunks, dtype=I32)
    chunk_e = jnp.minimum(jnp.sum((chunk_end[None, :] <= g[:, None]).astype(I32), axis=1), N_EXPERTS - 1)
    eids = jnp.arange(N_EXPERTS, dtype=I32)
    own = chunk_e[:, None] == eids[None, :]
    pick = lambda table: jnp.sum(jnp.where(own, table[None, :], 0), axis=1)
    in_expert = (g - pick(chunk_end - e_chunks)) * GMM_SUB
    chunk_row = (pick(seg_start) + in_expert).astype(I32)
    chunk_valid = jnp.clip(pick(counts) - in_expert, 0, GMM_SUB).astype(I32)
    n_chunks = chunk_end[-1:].astype(I32)
    has_rows = counts > 0
    e_ord = (jnp.cumsum(has_rows.astype(I32)) - has_rows.astype(I32)).astype(I32)
    later = has_rows[None, :] & (eids[None, :] > eids[:, None])
    e_next = jnp.min(jnp.where(later, eids[None, :], N_EXPERTS), axis=1)
    e_next = jnp.where(e_next == N_EXPERTS, -1, e_next).astype(I32)
    pos, w_tok, pos_tok = _slots(cw, rank, seg_start[:, None])
    x_sorted = _sc_dispatch(h2, pos, n_rows)
    y_sorted = _gmm(x_sorted, n_chunks, chunk_e, chunk_row, chunk_valid, e_ord, e_next, w_gate, w_up, w_down)
    w_lanes = jnp.repeat(w_tok, SC_LANES, axis=1)
    return _sc_collect_sum(y_sorted, pos_tok.reshape(n * TOP_K), w_lanes)


def kernel(x_prompt, x_sample, cache_k, cache_v, state_conv, c_prompt, c_sample, w_ada, b_ada, w_in, w_conv,
           w_conv_out, w_attn_o, attn_sinks, w_mix_out, w_router, router_bias, w_exp_gate, w_exp_up, w_exp_down,
           w_sh_gate, w_sh_up, w_sh_down, final_gain):
    assert w_ada.shape[0] == 1, "one layer"
    bp, seq, d = x_prompt.shape
    bs, ts, _ = x_sample.shape
    assert ts == CHUNK and seq % MIX_TILE == 0 and bs % SAMPLE_BB == 0

    c_all = jnp.concatenate([c_prompt, c_sample], axis=0)
    pad = (-c_all.shape[0]) % SUBLANES
    mod = _ada(jnp.pad(c_all, ((0, pad), (0, 0))), w_ada[0], b_ada[0])[:bp + bs]
    mod_p, mod_s = mod[:bp, None, :], mod[bp:, None, :]

    head_axes = (N_KV_HEADS // 2, 2, GQA_GROUP, HEAD_DIM)
    w_in_l = w_in[0]
    w_q = w_in_l[:, OFF_Q:OFF_K].reshape((d,) + head_axes).transpose(0, 1, 3, 2, 4).reshape(d, Q_DIM)
    w_o = w_attn_o[0].reshape(head_axes + (d,)).transpose(0, 2, 1, 3, 4).reshape(Q_DIM, d)
    win = tuple(w.astype(BF16) for w in (w_in_l[:, :OFF_Q], w_q, w_in_l[:, OFF_K:]))
    wco, wao, wmo = (w.astype(BF16) for w in (w_conv_out[0], w_o, w_mix_out[0]))
    cos_p, sin_p = _rope_tables(jnp.arange(seq, dtype=I32))
    cos_s, sin_s = _rope_tables(PAST_LEN + jnp.arange(ts, dtype=I32))

    x1_p, conv_p, k_p, v_p = _mixer_prompt(x_prompt, mod_p, cos_p, sin_p, win, w_conv[0], wco, wao, attn_sinks[0], wmo)
    x1_s, conv_s, k_s, v_s = _mixer_sample(
        x_sample, mod_s, cos_s, sin_s, cache_k[0].reshape(bs, WINDOW, KV_DIM), cache_v[0].reshape(bs, WINDOW, KV_DIM),
        state_conv[0], win, w_conv[0], wco, wao, attn_sinks[0], wmo)

    n_p, n_s = bp * seq, bs * ts
    n = n_p + n_s
    mod2_p, mod2_s = mod[:bp, None, 3 * d:], mod[bp:, None, 3 * d:]
    x1_pc = x1_p.reshape(n_p // CHUNK, CHUNK, d)
    wsg, wsu, wsd = (w[0].astype(BF16) for w in (w_sh_gate, w_sh_up, w_sh_down))
    wr_t, rb = w_router[0].T, router_bias[0][:, None]
    wr_hi = wr_t.astype(BF16)
    wr_lo = (wr_t - wr_hi.astype(F32)).astype(BF16)

    ncp, ncs = n_p // CHUNK, n_s // CHUNK
    half = (ncp + ncs) * FFN_SET_A_SHARE[0] // FFN_SET_A_SHARE[1]
    tile_chunks = max(PRE_TILE, COMB_TILE) // CHUNK
    assert half <= ncp and half % tile_chunks == 0 and (ncp - half) % tile_chunks == 0 and ncs % tile_chunks == 0
    for set_tokens in (half * CHUNK, n - half * CHUNK):
        assert set_tokens % (SC_WORKERS * SC_WINDOW) == 0 and set_tokens % RANK_TILE == 0
        assert set_tokens % (SC_WORKERS * 2 * SC_SUM_TOKENS) == 0
    y_p = None
    for p0, np_c, ns_c in ((0, half, 0), (half, ncp - half, ncs)):
        h2, base, cw = _pre(x1_pc, p0, np_c, mod2_p, x1_s, ns_c, mod2_s, wsg, wsu, wsd, wr_hi, wr_lo, rb)
        routed = _routed_ffn(h2, cw, w_exp_gate[0], w_exp_up[0], w_exp_down[0]).reshape(base.shape)
        y_p = _combine(base, mod2_p, routed, final_gain, 0, np_c, ncp, p0, out_buf=y_p)
        if ns_c:
            y_s = _combine(base, mod2_s, routed, final_gain, np_c, ns_c, ncs, 0)

    kv = lambda a: a.reshape(1, a.shape[0], WINDOW, N_KV_HEADS, HEAD_DIM)
    return (y_p.reshape(bp, seq, d), y_s, conv_p[None], kv(k_p), kv(v_p), conv_s[None], kv(k_s), kv(v_s))
```

```python
import functools

import jax
import jax.numpy as jnp
from jax import lax
from jax.experimental import pallas as pl
from jax.experimental.pallas import tpu as pltpu
from jax.experimental.pallas import tpu_sc as plsc

F32 = jnp.float32
BF16 = jnp.bfloat16
I32 = jnp.int32

D_MODEL = 1024
CHUNK = 64
D_CONV = 1024
CONV_W = 3
N_HEADS = 16
N_KV_HEADS = 4
HEAD_DIM = 64
GQA_GROUP = N_HEADS // N_KV_HEADS
WINDOW = 128
ROPE_THETA = 10000.0
ATTN_SCALE = HEAD_DIM ** -0.5
N_EXPERTS = 64
TOP_K = 8
N_EXPERT_GROUPS = 8
GROUP_SIZE = N_EXPERTS // N_EXPERT_GROUPS
TOPK_GROUPS = 4
D_EXPERT = 256
D_SHARED = 256
ROUTED_SCALE = 2.5
EPS = 1e-6
PAST_LEN = 4096
Q_DIM = N_HEADS * HEAD_DIM
KV_DIM = N_KV_HEADS * HEAD_DIM
OFF_GB, OFF_GC, OFF_XC, OFF_Q, OFF_K, OFF_V, OFF_GCONV, OFF_GATTN, D_IN = (
    0, 1024, 2048, 3072, 4096, 4352, 4608, 5632, 6656)

LANES = 128
SUBLANES = 8
VMEM_LIMIT = 56 * 1024 * 1024

MIX_TILE = 512
ATT_Q = 128
MIX_SIDE_COLS = 256
SAMPLE_BB = 8
PRE_TILE = 512
RANK_TILE = 512
GMM_SUB = 512
GMM_TAIL = 128
GMM_X_SLOTS = 3
COMB_TILE = 256
FFN_SET_A_SHARE = (2, 3)
SC_WORKERS = 32
SC_WINDOW = 96
SC_LANES = 16
SC_SUM_TOKENS = 8
SC_SUM_VREGS = 16


def _const_spec(shape):
    nd = len(shape)
    return pl.BlockSpec(shape, lambda *_: (0,) * nd, pipeline_mode=pl.Buffered(1))


def _rms(x):
    return x * lax.rsqrt(jnp.mean(x * x, axis=-1, keepdims=True) + EPS)


def _sigmoid(x):
    return 1.0 / (1.0 + jnp.exp(-x))


def _silu(x):
    return x * _sigmoid(x)


def _dot(a, b):
    return jnp.dot(a, b, preferred_element_type=F32)


def _pack_bf16_pairs(x):
    half = x.shape[-1] // 2
    lo = lax.bitcast_convert_type(x[..., :half].astype(BF16).astype(F32), I32)
    hi = lax.bitcast_convert_type(x[..., half:].astype(BF16).astype(F32), I32)
    return lax.shift_right_logical(lo, 16) | hi


def _unpack_bf16_pairs(words):
    lo = lax.bitcast_convert_type(lax.shift_left(words, 16), F32)
    hi = lax.bitcast_convert_type(words & jnp.int32(-65536), F32)
    return lo, hi


def _ada_kernel(c_ref, w_ref, b_ref, o_ref):
    s = _silu(c_ref[...]).astype(BF16)
    o_ref[...] = _dot(s, w_ref[...].astype(BF16)) + b_ref[...]


def _ada(c_all, w_ada, b_ada):
    rows = c_all.shape[0]
    n_out = w_ada.shape[1]
    bn = 768
    return pl.pallas_call(
        _ada_kernel,
        grid=(n_out // bn,),
        in_specs=[pl.BlockSpec((rows, D_MODEL), lambda i: (0, 0)),
                  pl.BlockSpec((D_MODEL, bn), lambda i: (0, i)),
                  pl.BlockSpec((1, bn), lambda i: (0, i))],
        out_specs=pl.BlockSpec((rows, bn), lambda i: (0, i)),
        out_shape=jax.ShapeDtypeStruct((rows, n_out), F32),
        name="ada_mod",
    )(c_all, w_ada, b_ada.reshape(1, n_out))


def _rope(x, cos, sin_signed):
    lane = lax.broadcasted_iota(I32, (x.shape[0], LANES), 1)
    first_half = (lane % HEAD_DIM) < (HEAD_DIM // 2)
    outs = []
    for g in range(x.shape[1] // LANES):
        xg = x[:, g * LANES:(g + 1) * LANES]
        up = pltpu.roll(xg, LANES - HEAD_DIM // 2, axis=1)
        down = pltpu.roll(xg, HEAD_DIM // 2, axis=1)
        partner = jnp.where(first_half, up, down)
        outs.append(xg * cos + partner * sin_signed)
    return jnp.concatenate(outs, axis=1)


def _attention(blocks, sinks_ref, obuf, between):
    rq = GQA_GROUP * ATT_Q
    low = lax.broadcasted_iota(I32, (ATT_Q, LANES), 1) < HEAD_DIM
    head_of_lane = lax.broadcasted_iota(I32, (1, rq), 1) // ATT_Q
    units = [(b, pair, par) for b in range(len(blocks)) for pair in range(N_KV_HEADS // 2) for par in range(2)]
    loaded = {}

    def scores(u):
        b, pair, par = units[u]
        if b not in loaded:
            loaded.clear()
            loaded[b] = blocks[b]()
        q_blk, k_of_pair, _, mask, _ = loaded[b]
        keep = low if par == 0 else jnp.logical_not(low)
        cols = [q_blk[:, (GQA_GROUP * pair + i) * LANES:(GQA_GROUP * pair + i + 1) * LANES] for i in range(GQA_GROUP)]
        qg = jnp.concatenate([jnp.where(keep, c, jnp.zeros_like(c)) for c in cols], axis=0)
        st = lax.dot_general(k_of_pair(pair), qg, (((1,), (1,)), ((), ())), preferred_element_type=F32)
        vt = loaded[b][2](pair)[par * HEAD_DIM:(par + 1) * HEAD_DIM, :]
        return jnp.where(mask, st, -jnp.inf), vt, loaded[b][4]

    outs = []
    nxt = scores(0)
    for u, (b, pair, par) in enumerate(units):
        st, vt, row0 = nxt
        if u + 1 < len(units):
            nxt = scores(u + 1)
        if between:
            between.pop(0)()
        g = 2 * pair + par
        sink = jnp.full((1, rq), sinks_ref[g * GQA_GROUP + GQA_GROUP - 1], F32)
        for i in range(GQA_GROUP - 2, -1, -1):
            sink = jnp.where(head_of_lane == i, sinks_ref[g * GQA_GROUP + i], sink)
        m = jnp.maximum(jnp.max(st, axis=0, keepdims=True), sink)
        e = jnp.exp(st - m)
        z = jnp.sum(e, axis=0, keepdims=True) + jnp.exp(sink - m)
        outs.append(_dot(vt, e.astype(BF16)) / z)
        if par == 1:
            for i in range(GQA_GROUP):
                blk = jnp.concatenate([o[:, i * ATT_Q:(i + 1) * ATT_Q] for o in outs], axis=0)
                c0 = (GQA_GROUP * pair + i) * LANES
                obuf[row0:row0 + ATT_Q, c0:c0 + LANES] = blk.T
            outs = []
    for step in between:
        step()


def _in_proj(hb, win_refs, lo, hi):
    w_all, w_q = win_refs
    if (lo, hi) == (OFF_Q, OFF_K):
        return _dot(hb, w_q[...])
    assert hi <= OFF_Q or lo >= OFF_K
    return _dot(hb, w_all[:, lo:hi])


def _attention_free_steps(hb, conv, win_ref, wco_ref):
    out, parts = {}, {}
    n_parts = D_MODEL // MIX_SIDE_COLS

    def step(name, piece, compute):
        def run():
            parts.setdefault(name, []).append(compute(piece * MIX_SIDE_COLS, (piece + 1) * MIX_SIDE_COLS))
            if piece == n_parts - 1:
                out[name] = jnp.concatenate(parts.pop(name), axis=1)
        return run

    def conv_in():
        if "conv_in" not in out:
            out["conv_in"] = (out.pop("gate_b") * conv).astype(BF16)
        return out["conv_in"]

    computes = [("gate_b", lambda lo, hi: _in_proj(hb, win_ref, OFF_GB + lo, OFF_GB + hi)),
                ("g_conv", lambda lo, hi: _in_proj(hb, win_ref, OFF_GCONV + lo, OFF_GCONV + hi)),
                ("g_attn", lambda lo, hi: _in_proj(hb, win_ref, OFF_GATTN + lo, OFF_GATTN + hi)),
                ("y_conv", lambda lo, hi: _dot(conv_in(), wco_ref[:, lo:hi]))]
    return [step(name, p, fn) for name, fn in computes for p in range(n_parts)], out


def _mix_out(x, g1, side, y_attn_in, wao_ref, wmo_ref):
    y_attn = _dot(y_attn_in.astype(BF16), wao_ref[...])
    merged = _sigmoid(side["g_conv"]) * side["y_conv"] + _sigmoid(side["g_attn"]) * y_attn
    return x + g1 * _dot(merged.astype(BF16), wmo_ref[...])


def _mixer_prompt_kernel(x_ref, mod_ref, cos_ref, sin_ref, wall_ref, wq_ref, wconv_ref, wco_ref, wao_ref,
                         sinks_ref, wmo_ref, x1_ref, conv_ref, k_ref, v_ref, ubuf, kbuf, vtbuf, obuf):
    win_ref = (wall_ref, wq_ref)
    j = pl.program_id(1)
    t = x_ref.shape[1]

    @pl.when(j == 0)
    def _():
        ubuf[0:SUBLANES, :] = jnp.zeros((SUBLANES, D_CONV), F32)
        kbuf[0:WINDOW, :] = jnp.zeros((WINDOW, KV_DIM), BF16)
        vtbuf[:, 0:WINDOW] = jnp.zeros((KV_DIM, WINDOW), BF16)

    x = x_ref[0]
    mod = mod_ref[0]
    sh1, sc1, g1 = mod[:, 0:D_MODEL], mod[:, D_MODEL:2 * D_MODEL], mod[:, 2 * D_MODEL:3 * D_MODEL]
    hb = (_rms(x) * (1.0 + sc1) + sh1).astype(BF16)

    u = _in_proj(hb, win_ref, OFF_GC, OFF_XC) * _in_proj(hb, win_ref, OFF_XC, OFF_Q)
    ubuf[SUBLANES:SUBLANES + t, :] = u
    wc = wconv_ref[...]
    conv = wc[0:1] * ubuf[SUBLANES - 2:SUBLANES - 2 + t, :] + wc[1:2] * ubuf[SUBLANES - 1:SUBLANES - 1 + t, :] + wc[2:3] * u
    conv_ref[0] = u[t - (CONV_W - 1):t]
    ubuf[SUBLANES - 2:SUBLANES, :] = u[t - (CONV_W - 1):t]

    cos, sin = cos_ref[...], sin_ref[...]
    q = (_rope(_in_proj(hb, win_ref, OFF_Q, OFF_K), cos, sin) * ATTN_SCALE).astype(BF16)
    k = _rope(_in_proj(hb, win_ref, OFF_K, OFF_V), cos, sin)
    v = _in_proj(hb, win_ref, OFF_V, OFF_GCONV)
    kbuf[WINDOW:WINDOW + t, :] = k.astype(BF16)
    vtbuf[:, WINDOW:WINDOW + t] = v.T.astype(BF16)
    k_ref[0] = k[t - WINDOW:t]
    v_ref[0] = v[t - WINDOW:t]

    nkeys = ATT_Q + WINDOW
    rq = GQA_GROUP * ATT_Q
    ki = lax.broadcasted_iota(I32, (nkeys, rq), 0)
    qi = lax.broadcasted_iota(I32, (nkeys, rq), 1) % ATT_Q
    band = ki // CHUNK - qi // CHUNK
    band_ok = (band >= 0) & (band <= WINDOW // CHUNK)
    def block(s):
        def load():
            mask = band_ok & (ki + (j * t + s * ATT_Q - WINDOW) >= 0)
            k_of_pair = lambda pair: kbuf[s * ATT_Q:s * ATT_Q + nkeys, pair * LANES:(pair + 1) * LANES]
            vt_of_pair = lambda pair: vtbuf[pair * LANES:(pair + 1) * LANES, s * ATT_Q:s * ATT_Q + nkeys]
            return q[s * ATT_Q:(s + 1) * ATT_Q], k_of_pair, vt_of_pair, mask, s * ATT_Q
        return load

    steps, side = _attention_free_steps(hb, conv, win_ref, wco_ref)
    _attention([block(s) for s in range(t // ATT_Q)], sinks_ref, obuf, steps)
    kbuf[0:WINDOW, :] = kbuf[t:t + WINDOW, :]
    vtbuf[:, 0:WINDOW] = vtbuf[:, t:t + WINDOW]

    x1_ref[0] = _mix_out(x, g1, side, obuf[...], wao_ref, wmo_ref)


def _mixer_prompt(x, mod, cos, sin, win, wconv, wco, wao, sinks, wmo):
    b, seq, d = x.shape
    t = MIX_TILE
    return pl.pallas_call(
        _mixer_prompt_kernel,
        grid=(b, seq // t),
        in_specs=[pl.BlockSpec((1, t, d), lambda i, j: (i, j, 0)),
                  pl.BlockSpec((1, 1, 6 * d), lambda i, j: (i, 0, 0)),
                  pl.BlockSpec((t, LANES), lambda i, j: (j, 0)),
                  pl.BlockSpec((t, LANES), lambda i, j: (j, 0)),
                  *[_const_spec(w.shape) for w in win],
                  _const_spec(wconv.shape), _const_spec(wco.shape), _const_spec(wao.shape),
                  pl.BlockSpec(memory_space=pltpu.SMEM),
                  _const_spec(wmo.shape)],
        out_specs=[pl.BlockSpec((1, t, d), lambda i, j: (i, j, 0)),
                   pl.BlockSpec((1, CONV_W - 1, D_CONV), lambda i, j: (i, 0, 0)),
                   pl.BlockSpec((1, WINDOW, KV_DIM), lambda i, j: (i, 0, 0)),
                   pl.BlockSpec((1, WINDOW, KV_DIM), lambda i, j: (i, 0, 0))],
        out_shape=[jax.ShapeDtypeStruct((b, seq, d), F32),
                   jax.ShapeDtypeStruct((b, CONV_W - 1, D_CONV), F32),
                   jax.ShapeDtypeStruct((b, WINDOW, KV_DIM), F32),
                   jax.ShapeDtypeStruct((b, WINDOW, KV_DIM), F32)],
        scratch_shapes=[pltpu.VMEM((SUBLANES + t, D_CONV), F32),
                        pltpu.VMEM((WINDOW + t, KV_DIM), BF16),
                        pltpu.VMEM((KV_DIM, WINDOW + t), BF16),
                        pltpu.VMEM((t, Q_DIM), F32)],
        compiler_params=pltpu.CompilerParams(dimension_semantics=("arbitrary", "arbitrary"),
                                             vmem_limit_bytes=VMEM_LIMIT),
        name="mixer_prompt",
    )(x, mod, cos, sin, *win, wconv, wco, wao, sinks, wmo)


def _mixer_sample_kernel(x_ref, mod_ref, cos_ref, sin_ref, ck_ref, cv_ref, sconv_ref, wall_ref, wq_ref,
                         wconv_ref, wco_ref, wao_ref, sinks_ref, wmo_ref, x1_ref, conv_ref, k_ref, v_ref, ubuf, obuf):
    win_ref = (wall_ref, wq_ref)
    bb, t, d = x_ref.shape
    x3 = x_ref[...]
    mod = mod_ref[...]
    sh1, sc1, g1 = mod[:, :, 0:d], mod[:, :, d:2 * d], mod[:, :, 2 * d:3 * d]
    x = x3.reshape(bb * t, d)
    hb = (_rms(x3) * (1.0 + sc1) + sh1).astype(BF16).reshape(bb * t, d)

    u = _in_proj(hb, win_ref, OFF_GC, OFF_XC) * _in_proj(hb, win_ref, OFF_XC, OFF_Q)
    u3 = u.reshape(bb, t, D_CONV)
    ubuf[:, SUBLANES - 2:SUBLANES, :] = sconv_ref[...]
    ubuf[:, SUBLANES:SUBLANES + t, :] = u3
    wc = wconv_ref[...]
    conv = (wc[0:1] * ubuf[:, SUBLANES - 2:SUBLANES - 2 + t, :] + wc[1:2] * ubuf[:, SUBLANES - 1:SUBLANES - 1 + t, :]
            + wc[2:3] * u3).reshape(bb * t, D_CONV)
    conv_ref[...] = u3[:, t - (CONV_W - 1):t, :]

    cos = jnp.concatenate([cos_ref[...]] * bb, axis=0)
    sin = jnp.concatenate([sin_ref[...]] * bb, axis=0)
    q = (_rope(_in_proj(hb, win_ref, OFF_Q, OFF_K), cos, sin) * ATTN_SCALE).astype(BF16)
    k = _rope(_in_proj(hb, win_ref, OFF_K, OFF_V), cos, sin)
    v = _in_proj(hb, win_ref, OFF_V, OFF_GCONV)
    per = ATT_Q // t
    nkeys = per * (WINDOW + t)
    rq = GQA_GROUP * ATT_Q
    key_stream = lax.broadcasted_iota(I32, (nkeys, rq), 0) // (WINDOW + t)
    query_stream = (lax.broadcasted_iota(I32, (nkeys, rq), 1) % ATT_Q) // t
    mask = key_stream == query_stream
    def block(blk):
        def load():
            k_parts, v_parts = [], []
            for b in range(blk * per, (blk + 1) * per):
                kb, vb = k[b * t:(b + 1) * t], v[b * t:(b + 1) * t]
                ck, cv = ck_ref[b], cv_ref[b]
                k_ref[b] = jnp.concatenate([ck[t:WINDOW], kb], axis=0)
                v_ref[b] = jnp.concatenate([cv[t:WINDOW], vb], axis=0)
                k_parts += [ck, kb]
                v_parts += [cv, vb]
            k_all = jnp.concatenate(k_parts, axis=0).astype(BF16)
            vt_all = jnp.concatenate(v_parts, axis=0).T.astype(BF16)
            k_of_pair = lambda pair: k_all[:, pair * LANES:(pair + 1) * LANES]
            vt_of_pair = lambda pair: vt_all[pair * LANES:(pair + 1) * LANES, :]
            return q[blk * ATT_Q:(blk + 1) * ATT_Q], k_of_pair, vt_of_pair, mask, blk * ATT_Q
        return load

    steps, side = _attention_free_steps(hb, conv, win_ref, wco_ref)
    _attention([block(blk) for blk in range(bb // per)], sinks_ref, obuf, steps)

    g1f = jnp.broadcast_to(g1, (bb, t, d)).reshape(bb * t, d)
    x1_ref[...] = _mix_out(x, g1f, side, obuf[...], wao_ref, wmo_ref).reshape(bb, t, d)


def _mixer_sample(x, mod, cos, sin, ck, cv, sconv, win, wconv, wco, wao, sinks, wmo):
    b, t, d = x.shape
    bb = SAMPLE_BB
    blk = lambda *s: pl.BlockSpec((bb,) + s, lambda i: (i, 0, 0))
    return pl.pallas_call(
        _mixer_sample_kernel,
        grid=(b // bb,),
        in_specs=[blk(t, d), blk(1, 6 * d),
                  pl.BlockSpec((t, LANES), lambda i: (0, 0)), pl.BlockSpec((t, LANES), lambda i: (0, 0)),
                  blk(WINDOW, KV_DIM), blk(WINDOW, KV_DIM), blk(CONV_W - 1, D_CONV),
                  *[_const_spec(w.shape) for w in win],
                  _const_spec(wconv.shape), _const_spec(wco.shape), _const_spec(wao.shape),
                  pl.BlockSpec(memory_space=pltpu.SMEM),
                  _const_spec(wmo.shape)],
        out_specs=[blk(t, d), blk(CONV_W - 1, D_CONV), blk(WINDOW, KV_DIM), blk(WINDOW, KV_DIM)],
        out_shape=[jax.ShapeDtypeStruct((b, t, d), F32),
                   jax.ShapeDtypeStruct((b, CONV_W - 1, D_CONV), F32),
                   jax.ShapeDtypeStruct((b, WINDOW, KV_DIM), F32),
                   jax.ShapeDtypeStruct((b, WINDOW, KV_DIM), F32)],
        scratch_shapes=[pltpu.VMEM((bb, SUBLANES + t, D_CONV), F32),
                        pltpu.VMEM((bb * t, Q_DIM), F32)],
        compiler_params=pltpu.CompilerParams(dimension_semantics=("arbitrary",), vmem_limit_bytes=VMEM_LIMIT),
        name="mixer_sample",
    )(x, mod, cos, sin, ck, cv, sconv, *win, wconv, wco, wao, sinks, wmo)


def _pre_kernel(*refs, prompt_tiles, has_sample):
    if has_sample:
        xp_ref, mp_ref, xs_ref, ms_ref, wsg_ref, wsu_ref, wsd_ref, wrh_ref, wrl_ref, rb_ref, h2_ref, base_ref, cw_ref = refs
    else:
        xp_ref, mp_ref, wsg_ref, wsu_ref, wsd_ref, wrh_ref, wrl_ref, rb_ref, h2_ref, base_ref, cw_ref = refs
    nc, c, d = xp_ref.shape
    t = nc * c
    x3, mod = xp_ref[...], mp_ref[...]
    if has_sample:
        is_prompt = pl.program_id(0) < prompt_tiles
        x3 = jnp.where(is_prompt, x3, xs_ref[...])
        mod = jnp.where(is_prompt, mod, ms_ref[...])
    sh2, sc2, g2 = mod[:, :, 0:d], mod[:, :, d:2 * d], mod[:, :, 2 * d:3 * d]
    h3 = _rms(x3) * (1.0 + sc2) + sh2
    h2 = h3.reshape(t, d)
    hb = h2.astype(BF16)
    h2_ref[...] = _pack_bf16_pairs(h2)
    shared = _dot((_silu(_dot(hb, wsg_ref[...])) * _dot(hb, wsu_ref[...])).astype(BF16), wsd_ref[...])
    base_ref[...] = x3 + g2 * shared.reshape(nc, c, d)

    h_lo = (h2 - hb.astype(F32)).astype(BF16)
    nt = lambda a, b: lax.dot_general(a, b, (((1,), (1,)), ((), ())), preferred_element_type=F32)
    logits = nt(wrh_ref[...], hb) + (nt(wrh_ref[...], h_lo) + nt(wrl_ref[...], hb))
    scores = _sigmoid(logits)
    biased = scores + rb_ref[...]
    g3 = biased.reshape(N_EXPERT_GROUPS, GROUP_SIZE, t)
    member = lax.broadcasted_iota(I32, g3.shape, 1)
    m1 = jnp.max(g3, axis=1, keepdims=True)
    first = jnp.min(jnp.where(g3 == m1, member, GROUP_SIZE), axis=1, keepdims=True)
    m2 = jnp.max(jnp.where(member == first, -jnp.inf, g3), axis=1, keepdims=True)
    gs = m1 + m2
    gidx = lax.broadcasted_iota(I32, gs.shape, 0)
    grank = jnp.zeros(gs.shape, I32)
    for o in range(N_EXPERT_GROUPS):
        other = gs[o:o + 1]
        grank += ((other > gs) | ((other == gs) & (o < gidx))).astype(I32)
    group_ok = grank < TOPK_GROUPS
    slot = jnp.zeros((1, 1, t), I32)
    takes = []
    for gi in range(N_EXPERT_GROUPS):
        ok = group_ok[gi:gi + 1]
        takes.append([ok & (slot == s) for s in range(TOPK_GROUPS)])
        slot = slot + ok.astype(I32)
    packed = []
    for s in range(TOPK_GROUPS):
        vals = jnp.zeros((GROUP_SIZE, t), F32)
        for gi in range(N_EXPERT_GROUPS):
            vals = jnp.where(takes[gi][s][0], g3[gi], vals)
        packed.append(vals)
    cand = jnp.concatenate(packed, axis=0)
    cidx = lax.broadcasted_iota(I32, cand.shape, 0)
    crank = jnp.zeros(cand.shape, I32)
    for o in range(TOPK_GROUPS * GROUP_SIZE):
        other = cand[o:o + 1]
        crank += ((other > cand) | ((other == cand) & (o < cidx))).astype(I32)
    chosen = crank < TOP_K
    sel_groups = []
    for gi in range(N_EXPERT_GROUPS):
        hit = jnp.zeros((GROUP_SIZE, t), jnp.bool_)
        for s in range(TOPK_GROUPS):
            hit = hit | (takes[gi][s][0] & chosen[s * GROUP_SIZE:(s + 1) * GROUP_SIZE])
        sel_groups.append(hit)
    sel = jnp.concatenate(sel_groups, axis=0)
    ssum = jnp.sum(jnp.where(sel, scores, 0.0), axis=0, keepdims=True)
    cw_ref[...] = jnp.where(sel, scores / ssum * ROUTED_SCALE, -1.0)


def _pre(x1_p, p_chunk0, ncp, mod_p, x1_s, ncs, mod_s, wsg, wsu, wsd, wr_hi, wr_lo, rb):
    ncp_all, c, d = x1_p.shape
    nc = PRE_TILE // c
    nchunks = ncp + ncs
    n = nchunks * c
    pt, p0 = ncp // nc, p_chunk0 // nc
    tiles_per_stream = ncp_all // mod_p.shape[0] // nc
    blk3 = pl.BlockSpec((nc, c, d), lambda i: (i, 0, 0))
    p_tile = lambda i: p0 + jnp.minimum(i, pt - 1)
    s_tile = lambda i: jnp.maximum(i - pt, 0)
    s_args, s_specs = [], []
    if ncs:
        s_args = [x1_s, mod_s]
        s_specs = [pl.BlockSpec((nc, c, d), lambda i: (s_tile(i), 0, 0)),
                   pl.BlockSpec((nc, 1, 3 * d), lambda i: (s_tile(i), 0, 0))]
    return pl.pallas_call(
        functools.partial(_pre_kernel, prompt_tiles=pt, has_sample=bool(ncs)),
        grid=(nchunks // nc,),
        in_specs=[pl.BlockSpec((nc, c, d), lambda i: (p_tile(i), 0, 0)),
                  pl.BlockSpec((1, 1, 3 * d), lambda i: (p_tile(i) // tiles_per_stream, 0, 0))] + s_specs + [
                  _const_spec(wsg.shape), _const_spec(wsu.shape), _const_spec(wsd.shape),
                  _const_spec(wr_hi.shape), _const_spec(wr_lo.shape), _const_spec(rb.shape)],
        out_specs=[pl.BlockSpec((nc * c, d // 2), lambda i: (i, 0)), blk3,
                   pl.BlockSpec((N_EXPERTS, nc * c), lambda i: (0, i))],
        out_shape=[jax.ShapeDtypeStruct((n, d // 2), I32),
                   jax.ShapeDtypeStruct((nchunks, c, d), F32),
                   jax.ShapeDtypeStruct((N_EXPERTS, n), F32)],
        compiler_params=pltpu.CompilerParams(dimension_semantics=("arbitrary",), vmem_limit_bytes=VMEM_LIMIT),
        name="pre_ffn",
    )(x1_p, mod_p, *s_args, wsg, wsu, wsd, wr_hi, wr_lo, rb)


def _rank_kernel(cw_ref, rank_ref, cnt_ref, carry):
    i = pl.program_id(0)
    t = cw_ref.shape[1]

    @pl.when(i == 0)
    def _():
        carry[...] = jnp.zeros(carry.shape, F32)

    sel = (cw_ref[...] >= 0.0).astype(BF16)
    r = lax.broadcasted_iota(I32, (t, t), 0)
    c = lax.broadcasted_iota(I32, (t, t), 1)
    before = (r < c).astype(BF16)
    rank = carry[...] + _dot(sel, before)
    rank_ref[...] = rank.astype(I32)
    carry[...] = carry[...] + jnp.sum(sel.astype(F32), axis=1, keepdims=True)
    cnt_ref[...] = carry[...].astype(I32)


def _rank(cw):
    e, n = cw.shape
    t = RANK_TILE
    return pl.pallas_call(
        _rank_kernel,
        grid=(n // t,),
        in_specs=[pl.BlockSpec((e, t), lambda i: (0, i))],
        out_specs=[pl.BlockSpec((e, t), lambda i: (0, i)), pl.BlockSpec((e, 1), lambda i: (0, 0))],
        out_shape=[jax.ShapeDtypeStruct((e, n), I32), jax.ShapeDtypeStruct((e, 1), I32)],
        scratch_shapes=[pltpu.VMEM((e, 1), F32)],
        compiler_params=pltpu.CompilerParams(dimension_semantics=("arbitrary",)),
        name="expert_rank",
    )(cw)


def _slot_kernel(cw_ref, rank_ref, start_ref, pos_ref, w_ref, pos_tok_ref):
    cw = cw_ref[...]
    e, t = cw.shape
    sel = cw >= 0.0
    r = lax.broadcasted_iota(I32, (e, e), 0)
    c = lax.broadcasted_iota(I32, (e, e), 1)
    lower = (c < r).astype(BF16)
    kidx = _dot(lower, sel.astype(BF16))
    posf = start_ref[...].astype(F32) + rank_ref[...].astype(F32)
    pos_rows, w_rows = [], []
    for k in range(TOP_K):
        m = sel & (kidx == float(k))
        pos_rows.append(jnp.sum(jnp.where(m, posf, 0.0), axis=0, keepdims=True))
        w_rows.append(jnp.sum(jnp.where(m, cw, 0.0), axis=0, keepdims=True))
    pos_ref[...] = jnp.concatenate(pos_rows, axis=0).astype(I32)
    w_ref[...] = jnp.concatenate([jnp.broadcast_to(w, (SC_LANES, t)) for w in w_rows], axis=0).T
    pos_pad = jnp.concatenate(pos_rows + [jnp.zeros((LANES - TOP_K, t), F32)], axis=0)
    pos_tok_ref[...] = pos_pad.T[:, :TOP_K].astype(I32)


def _slots(cw, rank, seg_start):
    e, n = cw.shape
    t = RANK_TILE
    return pl.pallas_call(
        _slot_kernel,
        grid=(n // t,),
        in_specs=[pl.BlockSpec((e, t), lambda i: (0, i)), pl.BlockSpec((e, t), lambda i: (0, i)),
                  pl.BlockSpec((e, 1), lambda i: (0, 0))],
        out_specs=[pl.BlockSpec((TOP_K, t), lambda i: (0, i)), pl.BlockSpec((t, TOP_K * SC_LANES), lambda i: (i, 0)),
                   pl.BlockSpec((t, TOP_K), lambda i: (i, 0))],
        out_shape=[jax.ShapeDtypeStruct((TOP_K, n), I32), jax.ShapeDtypeStruct((n, TOP_K * SC_LANES), F32),
                   jax.ShapeDtypeStruct((n, TOP_K), I32)],
        compiler_params=pltpu.CompilerParams(dimension_semantics=("arbitrary",)),
        name="expert_slots",
    )(cw, rank, seg_start)


def _sc_mesh():
    return plsc.VectorSubcoreMesh(core_axis_name="c", subcore_axis_name="s")


def _sc_worker_id():
    return lax.axis_index("s") * (SC_WORKERS // 16) + lax.axis_index("c")


def _sc_dispatch(rows, pos, n_rows):
    n, d = rows.shape
    per_w = n // SC_WORKERS
    w = SC_WINDOW
    n_chunks = per_w // w

    @functools.partial(
        pl.kernel, mesh=_sc_mesh(),
        out_type=jax.ShapeDtypeStruct((n_rows, d), rows.dtype),
        scratch_types=[pltpu.VMEM((2, TOP_K, w), I32), pltpu.VMEM((2, w, d), rows.dtype),
                       pltpu.SemaphoreType.DMA((2,)), pltpu.SemaphoreType.DMA((2,)), pltpu.SemaphoreType.DMA((2,))],
        name="sc_dispatch")
    def k(rows_hbm, pos_hbm, o_hbm, idx_v, rows_v, row_sem, idx_sem, out_sem):
        wid = _sc_worker_id()
        base = wid * per_w

        def loads(c, slot):
            off = pl.multiple_of(base + c * w, SUBLANES)
            return (pltpu.make_async_copy(rows_hbm.at[pl.ds(off, w)], rows_v.at[slot], row_sem.at[slot]),
                    pltpu.make_async_copy(pos_hbm.at[wid * n_chunks + c], idx_v.at[slot], idx_sem.at[slot]))

        def scatters(slot):
            return [pltpu.make_async_copy(rows_v.at[slot], o_hbm.at[idx_v.at[slot, kk]], out_sem.at[slot])
                    for kk in range(TOP_K)]

        for cp in loads(0, 0):
            cp.start()
        for c in range(n_chunks):
            slot = c % 2
            for cp in loads(c, slot):
                cp.wait()
            for cp in scatters(slot):
                cp.start()
            if c >= 1:
                for cp in scatters(1 - slot):
                    cp.wait()
            if c + 1 < n_chunks:
                for cp in loads(c + 1, 1 - slot):
                    cp.start()
        for cp in scatters((n_chunks - 1) % 2):
            cp.wait()

    pos_chunks = pos.reshape(TOP_K, n // w, w).transpose(1, 0, 2)
    return k(rows, pos_chunks)


def _sc_collect_sum(rows, pos_tok, w_lanes):
    words = rows.shape[1]
    n = w_lanes.shape[0]
    lanes = SC_LANES
    per_w = n // SC_WORKERS
    tw = SC_SUM_TOKENS
    n_pairs = per_w // (2 * tw)
    col_blocks = words // lanes // SC_SUM_VREGS

    @functools.partial(
        pl.kernel, mesh=_sc_mesh(),
        out_type=jax.ShapeDtypeStruct((n, 2 * words), F32),
        scratch_types=[pltpu.VMEM((per_w * TOP_K,), I32), pltpu.VMEM((2, tw * TOP_K, words), I32),
                       pltpu.VMEM((2, tw, TOP_K * lanes), F32), pltpu.VMEM((2, tw, 2 * words), F32),
                       pltpu.SemaphoreType.DMA((2,)), pltpu.SemaphoreType.DMA((2,)), pltpu.SemaphoreType.DMA((2,))],
        compiler_params=pltpu.CompilerParams(needs_layout_passes=False),
        name="sc_collect_sum")
    def k(rows_hbm, pos_hbm, w_hbm, o_hbm, idx_v, rows_v, w_v, out_v, in_sem, w_sem, out_sem):
        base = pl.multiple_of(_sc_worker_id() * per_w, SUBLANES)
        pltpu.sync_copy(pos_hbm.at[pl.ds(pl.multiple_of(base * TOP_K, SUBLANES), per_w * TOP_K)], idx_v)

        def loads(c, slot):
            idx = idx_v.at[pl.ds(pl.multiple_of(c * tw * TOP_K, SUBLANES), tw * TOP_K)]
            tok0 = pl.multiple_of(base + c * tw, SUBLANES)
            return (pltpu.make_async_copy(rows_hbm.at[idx], rows_v.at[slot], in_sem.at[slot]),
                    pltpu.make_async_copy(w_hbm.at[pl.ds(tok0, tw)], w_v.at[slot], w_sem.at[slot]))

        def write(c, slot):
            tok0 = pl.multiple_of(base + c * tw, SUBLANES)
            return pltpu.make_async_copy(out_v.at[slot], o_hbm.at[pl.ds(tok0, tw)], out_sem.at[slot])

        high_half = jnp.full((lanes,), -65536, I32)
        sixteen = jnp.full((lanes,), 16, I32)

        def reduce_window(slot):
            rv, wv, ov = rows_v.at[slot], w_v.at[slot], out_v.at[slot]

            @pl.loop(0, tw)
            def _(t):
                for cb in range(col_blocks):
                    acc_lo, acc_hi = [None] * SC_SUM_VREGS, [None] * SC_SUM_VREGS
                    for kk in range(TOP_K):
                        wk = wv[t, pl.ds(kk * lanes, lanes)]
                        for c in range(SC_SUM_VREGS):
                            wd = rv[t * TOP_K + kk, pl.ds((cb * SC_SUM_VREGS + c) * lanes, lanes)]
                            lo = wk * plsc.bitcast(lax.shift_left(wd, sixteen), F32)
                            hi = wk * plsc.bitcast(wd & high_half, F32)
                            acc_lo[c] = lo if kk == 0 else acc_lo[c] + lo
                            acc_hi[c] = hi if kk == 0 else acc_hi[c] + hi
                    for c in range(SC_SUM_VREGS):
                        col = (cb * SC_SUM_VREGS + c) * lanes
                        ov[t, pl.ds(col, lanes)] = acc_lo[c]
                        ov[t, pl.ds(words + col, lanes)] = acc_hi[c]

        for cp in loads(0, 0):
            cp.start()

        @pl.loop(0, n_pairs)
        def _(p):
            c0 = 2 * p
            for cp in loads(c0 + 1, 1):
                cp.start()
            for cp in loads(c0, 0):
                cp.wait()
            reduce_window(0)
            write(c0, 0).start()
            for cp in loads(c0 + 1, 1):
                cp.wait()
            reduce_window(1)
            write(c0 + 1, 1).start()
            write(c0, 0).wait()

            @pl.when(p + 1 < n_pairs)
            def _():
                for cp in loads(c0 + 2, 0):
                    cp.start()

            write(c0 + 1, 1).wait()

    return k(rows, pos_tok, w_lanes)


def _gmm_kernel(tot_ref, ce_ref, row_ref, val_ref, ord_ref, nxt_ref, x_hbm, wg_hbm, wu_hbm, wd_hbm, y_hbm,
                wgb, wub, wdb, xbuf, xsem, ybuf, ysem, wgf, wuf, wdf, wsem):
    total = tot_ref[0]
    pieces = GMM_SUB // GMM_TAIL

    def w_copies(ex, slot):
        return [pltpu.make_async_copy(src.at[ex], dst.at[slot], wsem.at[slot, i])
                for i, (src, dst) in enumerate(((wg_hbm, wgf), (wu_hbm, wuf), (wd_hbm, wdf)))]

    def x_copy(g):
        slot = g % GMM_X_SLOTS
        rows = pl.ds(pl.multiple_of(row_ref[g], GMM_TAIL), GMM_SUB)
        return pltpu.make_async_copy(x_hbm.at[rows], xbuf.at[slot], xsem.at[slot])

    def y_piece(g, p):
        slot = g % 2
        rows = pl.ds(pl.multiple_of(row_ref[g] + p * GMM_TAIL, GMM_TAIL), GMM_TAIL)
        return pltpu.make_async_copy(ybuf.at[slot, pl.ds(p * GMM_TAIL, GMM_TAIL)], y_hbm.at[rows], ysem.at[slot])

    def for_y_pieces(g, action):
        for p in range(pieces):
            @pl.when(p * GMM_TAIL < val_ref[g])
            def _():
                action(y_piece(g, p))

    for ahead in range(GMM_X_SLOTS - 1):
        @pl.when(ahead < total)
        def _():
            x_copy(ahead).start()

    def chunk(g, carry):
        e = ce_ref[g]
        prev = ce_ref[jnp.maximum(g - 1, 0)]

        @pl.when((g == 0) | (e != prev))
        def _():
            slot = ord_ref[e] % 2

            @pl.when(g == 0)
            def _():
                for cp in w_copies(e, slot):
                    cp.start()

            for cp in w_copies(e, slot):
                cp.wait()
            wgb[...] = wgf[slot].astype(BF16)
            wub[...] = wuf[slot].astype(BF16)
            wdb[...] = wdf[slot].astype(BF16)
            nxt = nxt_ref[e]

            @pl.when(nxt >= 0)
            def _():
                for cp in w_copies(nxt, 1 - slot):
                    cp.start()

        x_copy(g).wait()

        @pl.when(g + GMM_X_SLOTS - 1 < total)
        def _():
            x_copy(g + GMM_X_SLOTS - 1).start()

        @pl.when(g >= 2)
        def _():
            for_y_pieces(g - 2, lambda cp: cp.wait())

        x_ref = xbuf.at[g % GMM_X_SLOTS]
        y_ref = ybuf.at[g % 2]

        def expert_rows(r0, n):
            rows = pl.ds(r0, n)
            lo, hi = _unpack_bf16_pairs(x_ref[rows, :])
            xb = jnp.concatenate([lo.astype(BF16), hi.astype(BF16)], axis=1)
            mid = (_silu(_dot(xb, wgb[...])) * _dot(xb, wub[...])).astype(BF16)
            y_ref[rows, :] = _pack_bf16_pairs(_dot(mid, wdb[...]))

        n_real = val_ref[g]

        @pl.when(n_real == GMM_SUB)
        def _():
            expert_rows(0, GMM_SUB)

        @pl.when(n_real < GMM_SUB)
        def _():
            @pl.loop(0, (n_real + GMM_TAIL - 1) // GMM_TAIL)
            def _(i):
                expert_rows(pl.multiple_of(i * GMM_TAIL, GMM_TAIL), GMM_TAIL)

        for_y_pieces(g, lambda cp: cp.start())
        return carry

    lax.fori_loop(0, total, chunk, 0)
    for back in (2, 1):
        @pl.when(total >= back)
        def _():
            for_y_pieces(total - back, lambda cp: cp.wait())


def _gmm(x_sorted, n_chunks, chunk_e, chunk_row, chunk_valid, e_ord, e_next, wg, wu, wd):
    r, half = x_sorted.shape
    d = 2 * half
    any_spec = pl.BlockSpec(memory_space=pl.ANY)
    return pl.pallas_call(
        _gmm_kernel,
        grid_spec=pltpu.PrefetchScalarGridSpec(
            num_scalar_prefetch=6,
            grid=(1,),
            in_specs=[any_spec, any_spec, any_spec, any_spec],
            out_specs=any_spec,
            scratch_shapes=[pltpu.VMEM((d, D_EXPERT), BF16), pltpu.VMEM((d, D_EXPERT), BF16),
                            pltpu.VMEM((D_EXPERT, d), BF16),
                            pltpu.VMEM((GMM_X_SLOTS, GMM_SUB, half), I32), pltpu.SemaphoreType.DMA((GMM_X_SLOTS,)),
                            pltpu.VMEM((2, GMM_SUB, half), I32), pltpu.SemaphoreType.DMA((2,)),
                            pltpu.VMEM((2, d, D_EXPERT), F32), pltpu.VMEM((2, d, D_EXPERT), F32),
                            pltpu.VMEM((2, D_EXPERT, d), F32), pltpu.SemaphoreType.DMA((2, 3))]),
        out_shape=jax.ShapeDtypeStruct((r, half), I32),
        compiler_params=pltpu.CompilerParams(dimension_semantics=("arbitrary",), vmem_limit_bytes=VMEM_LIMIT),
        name="expert_gmm",
    )(n_chunks, chunk_e, chunk_row, chunk_valid, e_ord, e_next, x_sorted, wg, wu, wd)


def _combine_kernel(base_ref, mod_ref, routed_ref, gain_ref, *rest):
    y_ref = rest[-1]
    d = base_ref.shape[-1]
    g2 = mod_ref[...][:, :, 2 * d:3 * d]
    out = base_ref[...] + g2 * routed_ref[...]
    y_ref[...] = _rms(out) * gain_ref[...]


def _combine(base, mod, routed, gain, first_chunk, n_chunks, out_chunks, out_first_chunk, out_buf=None):
    _, c, d = base.shape
    nc = COMB_TILE // c
    t0, o0 = first_chunk // nc, out_first_chunk // nc
    chunks_per_stream = out_chunks // mod.shape[0]
    if chunks_per_stream == 1:
        mod_spec = pl.BlockSpec((nc, 1, 3 * d), lambda i: (o0 + i, 0, 0))
    else:
        assert chunks_per_stream % nc == 0
        mod_spec = pl.BlockSpec((1, 1, 3 * d), lambda i: ((o0 + i) * nc // chunks_per_stream, 0, 0))
    blk3 = pl.BlockSpec((nc, c, d), lambda i: (t0 + i, 0, 0))
    in_specs = [blk3, mod_spec, blk3, pl.BlockSpec((1, 1, d), lambda i: (0, 0, 0))]
    args = [base, mod, routed, gain.reshape(1, 1, d)]
    aliases = {}
    if out_buf is not None:
        in_specs.append(pl.BlockSpec(memory_space=pl.ANY))
        args.append(out_buf)
        aliases = {len(args) - 1: 0}
    return pl.pallas_call(
        _combine_kernel,
        grid=(n_chunks // nc,),
        in_specs=in_specs,
        out_specs=pl.BlockSpec((nc, c, d), lambda i: (o0 + i, 0, 0)),
        out_shape=jax.ShapeDtypeStruct((out_chunks, c, d), F32),
        input_output_aliases=aliases,
        compiler_params=pltpu.CompilerParams(dimension_semantics=("arbitrary",), vmem_limit_bytes=VMEM_LIMIT),
        name="combine_norm",
    )(*args)


def _rope_tables(pos):
    half = HEAD_DIM // 2
    lane = jnp.arange(LANES, dtype=I32)
    inv_freq = ROPE_THETA ** (-(lane % half).astype(F32) / half)
    sign = jnp.where((lane % HEAD_DIM) < half, -1.0, 1.0).astype(F32)
    ang = pos.astype(F32)[:, None] * inv_freq[None, :]
    return jnp.cos(ang), jnp.sin(ang) * sign[None, :]


def _routed_ffn(h2, cw, w_gate, w_up, w_down):
    n, half = h2.shape
    rank, counts = _rank(cw)
    counts = counts[:, 0]
    padded = (counts + GMM_TAIL - 1) // GMM_TAIL * GMM_TAIL
    seg_start = (jnp.cumsum(padded) - padded).astype(I32)
    n_rows = n * TOP_K + N_EXPERTS * GMM_TAIL + GMM_SUB
    e_chunks = (counts + GMM_SUB - 1) // GMM_SUB
    chunk_end = jnp.cumsum(e_chunks)
    max_chunks = n * TOP_K // GMM_SUB + N_EXPERTS
    g = jnp.arange(max_chunks, dtype=I32)
    chunk_e = jnp.minimum(jnp.sum((chunk_end[None, :] <= g[:, None]).astype(I32), axis=1), N_EXPERTS - 1)
    eids = jnp.arange(N_EXPERTS, dtype=I32)
    own = chunk_e[:, None] == eids[None, :]
    pick = lambda table: jnp.sum(jnp.where(own, table[None, :], 0), axis=1)
    in_expert = (g - pick(chunk_end - e_chunks)) * GMM_SUB
    chunk_row = (pick(seg_start) + in_expert).astype(I32)
    chunk_valid = jnp.clip(pick(counts) - in_expert, 0, GMM_SUB).astype(I32)
    n_chunks = chunk_end[-1:].astype(I32)
    has_rows = counts > 0
    e_ord = (jnp.cumsum(has_rows.astype(I32)) - has_rows.astype(I32)).astype(I32)
    later = has_rows[None, :] & (eids[None, :] > eids[:, None])
    e_next = jnp.min(jnp.where(later, eids[None, :], N_EXPERTS), axis=1)
    e_next = jnp.where(e_next == N_EXPERTS, -1, e_next).astype(I32)
    pos, w_lanes, pos_tok = _slots(cw, rank, seg_start[:, None])
    x_sorted = _sc_dispatch(h2, pos, n_rows)
    y_sorted = _gmm(x_sorted, n_chunks, chunk_e, chunk_row, chunk_valid, e_ord, e_next, w_gate, w_up, w_down)
    return _sc_collect_sum(y_sorted, pos_tok.reshape(n * TOP_K), w_lanes)


def kernel(x_prompt, x_sample, cache_k, cache_v, state_conv, c_prompt, c_sample, w_ada, b_ada, w_in, w_conv,
           w_conv_out, w_attn_o, attn_sinks, w_mix_out, w_router, router_bias, w_exp_gate, w_exp_up, w_exp_down,
           w_sh_gate, w_sh_up, w_sh_down, final_gain):
    assert w_ada.shape[0] == 1, "one layer"
    bp, seq, d = x_prompt.shape
    bs, ts, _ = x_sample.shape
    assert ts == CHUNK and seq % MIX_TILE == 0 and bs % SAMPLE_BB == 0

    c_all = jnp.concatenate([c_prompt, c_sample], axis=0)
    pad = (-c_all.shape[0]) % SUBLANES
    mod = _ada(jnp.pad(c_all, ((0, pad), (0, 0))), w_ada[0], b_ada[0])[:bp + bs]
    mod_p, mod_s = mod[:bp, None, :], mod[bp:, None, :]

    head_axes = (N_KV_HEADS // 2, 2, GQA_GROUP, HEAD_DIM)
    w_in_l = w_in[0]
    w_q = w_in_l[:, OFF_Q:OFF_K].reshape((d,) + head_axes).transpose(0, 1, 3, 2, 4).reshape(d, Q_DIM)
    w_o = w_attn_o[0].reshape(head_axes + (d,)).transpose(0, 2, 1, 3, 4).reshape(Q_DIM, d)
    win = (w_in_l.astype(BF16), w_q.astype(BF16))
    wco, wao, wmo = (w.astype(BF16) for w in (w_conv_out[0], w_o, w_mix_out[0]))
    cos_p, sin_p = _rope_tables(jnp.arange(seq, dtype=I32))
    cos_s, sin_s = _rope_tables(PAST_LEN + jnp.arange(ts, dtype=I32))

    x1_p, conv_p, k_p, v_p = _mixer_prompt(x_prompt, mod_p, cos_p, sin_p, win, w_conv[0], wco, wao, attn_sinks[0], wmo)
    x1_s, conv_s, k_s, v_s = _mixer_sample(
        x_sample, mod_s, cos_s, sin_s, cache_k[0].reshape(bs, WINDOW, KV_DIM), cache_v[0].reshape(bs, WINDOW, KV_DIM),
        state_conv[0], win, w_conv[0], wco, wao, attn_sinks[0], wmo)

    n_p, n_s = bp * seq, bs * ts
    n = n_p + n_s
    mod2_p, mod2_s = mod[:bp, None, 3 * d:], mod[bp:, None, 3 * d:]
    x1_pc = x1_p.reshape(n_p // CHUNK, CHUNK, d)
    wsg, wsu, wsd = (w[0].astype(BF16) for w in (w_sh_gate, w_sh_up, w_sh_down))
    wr_t, rb = w_router[0].T, router_bias[0][:, None]
    wr_hi = wr_t.astype(BF16)
    wr_lo = (wr_t - wr_hi.astype(F32)).astype(BF16)

    ncp, ncs = n_p // CHUNK, n_s // CHUNK
    half = (ncp + ncs) * FFN_SET_A_SHARE[0] // FFN_SET_A_SHARE[1]
    tile_chunks = max(PRE_TILE, COMB_TILE) // CHUNK
    assert half <= ncp and half % tile_chunks == 0 and (ncp - half) % tile_chunks == 0 and ncs % tile_chunks == 0
    for set_tokens in (half * CHUNK, n - half * CHUNK):
        assert set_tokens % (SC_WORKERS * SC_WINDOW) == 0 and set_tokens % RANK_TILE == 0
        assert set_tokens % (SC_WORKERS * 2 * SC_SUM_TOKENS) == 0
    y_p = None
    for p0, np_c, ns_c in ((0, half, 0), (half, ncp - half, ncs)):
        h2, base, cw = _pre(x1_pc, p0, np_c, mod2_p, x1_s, ns_c, mod2_s, wsg, wsu, wsd, wr_hi, wr_lo, rb)
        routed = _routed_ffn(h2, cw, w_exp_gate[0], w_exp_up[0], w_exp_down[0]).reshape(base.shape)
        y_p = _combine(base, mod2_p, routed, final_gain, 0, np_c, ncp, p0, out_buf=y_p)
        if ns_c:
            y_s = _combine(base, mod2_s, routed, final_gain, np_c, ns_c, ncs, 0)

    kv = lambda a: a.reshape(1, a.shape[0], WINDOW, N_KV_HEADS, HEAD_DIM)
    return (y_p.reshape(bp, seq, d), y_s, conv_p[None], kv(k_p), kv(v_p), conv_s[None], kv(k_s), kv(v_s))
```

```python
import functools

import jax
import jax.numpy as jnp
from jax import lax
from jax.experimental import pallas as pl
from jax.experimental.pallas import tpu as pltpu
from jax.experimental.pallas import tpu_sc as plsc

F32 = jnp.float32
BF16 = jnp.bfloat16
I32 = jnp.int32

D_MODEL = 1024
CHUNK = 64
D_CONV = 1024
CONV_W = 3
N_HEADS = 16
N_KV_HEADS = 4
HEAD_DIM = 64
GQA_GROUP = N_HEADS // N_KV_HEADS
WINDOW = 128
ROPE_THETA = 10000.0
ATTN_SCALE = HEAD_DIM ** -0.5
N_EXPERTS = 64
TOP_K = 8
N_EXPERT_GROUPS = 8
GROUP_SIZE = N_EXPERTS // N_EXPERT_GROUPS
TOPK_GROUPS = 4
D_EXPERT = 256
D_SHARED = 256
ROUTED_SCALE = 2.5
EPS = 1e-6
PAST_LEN = 4096
Q_DIM = N_HEADS * HEAD_DIM
KV_DIM = N_KV_HEADS * HEAD_DIM
OFF_GB, OFF_GC, OFF_XC, OFF_Q, OFF_K, OFF_V, OFF_GCONV, OFF_GATTN, D_IN = (
    0, 1024, 2048, 3072, 4096, 4352, 4608, 5632, 6656)

LANES = 128
SUBLANES = 8
VMEM_LIMIT = 56 * 1024 * 1024

MIX_TILE = 512
ATT_Q = 128
MIX_SIDE_COLS = 256
SAMPLE_BB = 8
PRE_TILE = 512
RANK_TILE = 512
GMM_SUB = 512
GMM_MID = 256
GMM_TAIL = 128
GMM_X_SLOTS = 3
COMB_TILE = 512
FFN_SET_A_SHARE = (2, 3)
SC_WORKERS = 32
SC_WINDOW = 96
SC_LANES = 16
SC_SUM_TOKENS = 8
SC_SUM_VREGS = 16


def _const_spec(shape):
    nd = len(shape)
    return pl.BlockSpec(shape, lambda *_: (0,) * nd, pipeline_mode=pl.Buffered(1))


def _rms(x):
    return x * lax.rsqrt(jnp.mean(x * x, axis=-1, keepdims=True) + EPS)


def _sigmoid(x):
    return 1.0 / (1.0 + jnp.exp(-x))


def _silu(x):
    return x * _sigmoid(x)


def _dot(a, b):
    return jnp.dot(a, b, preferred_element_type=F32)


def _pack_bf16_pairs(x):
    half = x.shape[-1] // 2
    lo = lax.bitcast_convert_type(x[..., :half].astype(BF16).astype(F32), I32)
    hi = lax.bitcast_convert_type(x[..., half:].astype(BF16).astype(F32), I32)
    return lax.shift_right_logical(lo, 16) | hi


def _unpack_bf16_pairs(words):
    lo = lax.bitcast_convert_type(lax.shift_left(words, 16), F32)
    hi = lax.bitcast_convert_type(words & jnp.int32(-65536), F32)
    return lo, hi


def _ada_kernel(c_ref, w_ref, b_ref, o_ref):
    s = _silu(c_ref[...]).astype(BF16)
    o_ref[...] = _dot(s, w_ref[...].astype(BF16)) + b_ref[...]


def _ada(c_all, w_ada, b_ada):
    rows = c_all.shape[0]
    n_out = w_ada.shape[1]
    bn = 768
    return pl.pallas_call(
        _ada_kernel,
        grid=(n_out // bn,),
        in_specs=[pl.BlockSpec((rows, D_MODEL), lambda i: (0, 0)),
                  pl.BlockSpec((D_MODEL, bn), lambda i: (0, i)),
                  pl.BlockSpec((1, bn), lambda i: (0, i))],
        out_specs=pl.BlockSpec((rows, bn), lambda i: (0, i)),
        out_shape=jax.ShapeDtypeStruct((rows, n_out), F32),
        name="ada_mod",
    )(c_all, w_ada, b_ada.reshape(1, n_out))


def _rope(x, cos, sin_signed):
    lane = lax.broadcasted_iota(I32, (x.shape[0], LANES), 1)
    first_half = (lane % HEAD_DIM) < (HEAD_DIM // 2)
    outs = []
    for g in range(x.shape[1] // LANES):
        xg = x[:, g * LANES:(g + 1) * LANES]
        up = pltpu.roll(xg, LANES - HEAD_DIM // 2, axis=1)
        down = pltpu.roll(xg, HEAD_DIM // 2, axis=1)
        partner = jnp.where(first_half, up, down)
        outs.append(xg * cos + partner * sin_signed)
    return jnp.concatenate(outs, axis=1)


def _attention(blocks, sinks_ref, obuf, between):
    rq = GQA_GROUP * ATT_Q
    low = lax.broadcasted_iota(I32, (ATT_Q, LANES), 1) < HEAD_DIM
    head_of_lane = lax.broadcasted_iota(I32, (1, rq), 1) // ATT_Q
    units = [(b, pair, par) for b in range(len(blocks)) for pair in range(N_KV_HEADS // 2) for par in range(2)]
    loaded = {}

    def scores(u):
        b, pair, par = units[u]
        if b not in loaded:
            loaded.clear()
            loaded[b] = blocks[b]()
        q_blk, k_of_pair, _, mask, _ = loaded[b]
        keep = low if par == 0 else jnp.logical_not(low)
        cols = [q_blk[:, (GQA_GROUP * pair + i) * LANES:(GQA_GROUP * pair + i + 1) * LANES] for i in range(GQA_GROUP)]
        qg = jnp.concatenate([jnp.where(keep, c, jnp.zeros_like(c)) for c in cols], axis=0)
        st = lax.dot_general(k_of_pair(pair), qg, (((1,), (1,)), ((), ())), preferred_element_type=F32)
        vt = loaded[b][2](pair)[par * HEAD_DIM:(par + 1) * HEAD_DIM, :]
        return jnp.where(mask, st, -jnp.inf), vt, loaded[b][4]

    outs = []
    nxt = scores(0)
    for u, (b, pair, par) in enumerate(units):
        st, vt, row0 = nxt
        if u + 1 < len(units):
            nxt = scores(u + 1)
        if between:
            between.pop(0)()
        g = 2 * pair + par
        sink = jnp.full((1, rq), sinks_ref[g * GQA_GROUP + GQA_GROUP - 1], F32)
        for i in range(GQA_GROUP - 2, -1, -1):
            sink = jnp.where(head_of_lane == i, sinks_ref[g * GQA_GROUP + i], sink)
        m = jnp.maximum(jnp.max(st, axis=0, keepdims=True), sink)
        e = jnp.exp(st - m)
        z = jnp.sum(e, axis=0, keepdims=True) + jnp.exp(sink - m)
        outs.append(_dot(vt, e.astype(BF16)) / z)
        if par == 1:
            for i in range(GQA_GROUP):
                blk = jnp.concatenate([o[:, i * ATT_Q:(i + 1) * ATT_Q] for o in outs], axis=0)
                c0 = (GQA_GROUP * pair + i) * LANES
                obuf[row0:row0 + ATT_Q, c0:c0 + LANES] = blk.T
            outs = []
    for step in between:
        step()


def _in_proj(hb, win_refs, lo, hi):
    w_all, w_q = win_refs
    if (lo, hi) == (OFF_Q, OFF_K):
        return _dot(hb, w_q[...])
    assert hi <= OFF_Q or lo >= OFF_K
    return _dot(hb, w_all[:, lo:hi])


def _attention_free_steps(hb, conv, win_ref, wco_ref):
    out, parts = {}, {}
    n_parts = D_MODEL // MIX_SIDE_COLS

    def step(name, piece, compute):
        def run():
            parts.setdefault(name, []).append(compute(piece * MIX_SIDE_COLS, (piece + 1) * MIX_SIDE_COLS))
            if piece == n_parts - 1:
                out[name] = jnp.concatenate(parts.pop(name), axis=1)
        return run

    def conv_in():
        if "conv_in" not in out:
            out["conv_in"] = (out.pop("gate_b") * conv).astype(BF16)
        return out["conv_in"]

    computes = [("gate_b", lambda lo, hi: _in_proj(hb, win_ref, OFF_GB + lo, OFF_GB + hi)),
                ("g_conv", lambda lo, hi: _in_proj(hb, win_ref, OFF_GCONV + lo, OFF_GCONV + hi)),
                ("g_attn", lambda lo, hi: _in_proj(hb, win_ref, OFF_GATTN + lo, OFF_GATTN + hi)),
                ("y_conv", lambda lo, hi: _dot(conv_in(), wco_ref[:, lo:hi]))]
    return [step(name, p, fn) for name, fn in computes for p in range(n_parts)], out


def _mix_out(x, g1, side, y_attn_in, wao_ref, wmo_ref):
    y_attn = _dot(y_attn_in.astype(BF16), wao_ref[...])
    merged = _sigmoid(side["g_conv"]) * side["y_conv"] + _sigmoid(side["g_attn"]) * y_attn
    return x + g1 * _dot(merged.astype(BF16), wmo_ref[...])


def _mixer_prompt_kernel(x_ref, mod_ref, cos_ref, sin_ref, wall_ref, wq_ref, wconv_ref, wco_ref, wao_ref,
                         sinks_ref, wmo_ref, x1_ref, conv_ref, k_ref, v_ref, ubuf, kbuf, vtbuf, obuf):
    win_ref = (wall_ref, wq_ref)
    j = pl.program_id(1)
    t = x_ref.shape[1]

    @pl.when(j == 0)
    def _():
        ubuf[0:SUBLANES, :] = jnp.zeros((SUBLANES, D_CONV), F32)
        kbuf[0:WINDOW, :] = jnp.zeros((WINDOW, KV_DIM), BF16)
        vtbuf[:, 0:WINDOW] = jnp.zeros((KV_DIM, WINDOW), BF16)

    x = x_ref[0]
    mod = mod_ref[0]
    sh1, sc1, g1 = mod[:, 0:D_MODEL], mod[:, D_MODEL:2 * D_MODEL], mod[:, 2 * D_MODEL:3 * D_MODEL]
    hb = (_rms(x) * (1.0 + sc1) + sh1).astype(BF16)

    u = _in_proj(hb, win_ref, OFF_GC, OFF_XC) * _in_proj(hb, win_ref, OFF_XC, OFF_Q)
    ubuf[SUBLANES:SUBLANES + t, :] = u
    wc = wconv_ref[...]
    conv = wc[0:1] * ubuf[SUBLANES - 2:SUBLANES - 2 + t, :] + wc[1:2] * ubuf[SUBLANES - 1:SUBLANES - 1 + t, :] + wc[2:3] * u
    conv_ref[0] = u[t - (CONV_W - 1):t]
    ubuf[SUBLANES - 2:SUBLANES, :] = u[t - (CONV_W - 1):t]

    cos, sin = cos_ref[...], sin_ref[...]
    q = (_rope(_in_proj(hb, win_ref, OFF_Q, OFF_K), cos, sin) * ATTN_SCALE).astype(BF16)
    k = _rope(_in_proj(hb, win_ref, OFF_K, OFF_V), cos, sin)
    v = _in_proj(hb, win_ref, OFF_V, OFF_GCONV)
    kbuf[WINDOW:WINDOW + t, :] = k.astype(BF16)
    vtbuf[:, WINDOW:WINDOW + t] = v.T.astype(BF16)
    k_ref[0] = k[t - WINDOW:t]
    v_ref[0] = v[t - WINDOW:t]

    nkeys = ATT_Q + WINDOW
    rq = GQA_GROUP * ATT_Q
    ki = lax.broadcasted_iota(I32, (nkeys, rq), 0)
    qi = lax.broadcasted_iota(I32, (nkeys, rq), 1) % ATT_Q
    band = ki // CHUNK - qi // CHUNK
    band_ok = (band >= 0) & (band <= WINDOW // CHUNK)
    def block(s):
        def load():
            mask = band_ok & (ki + (j * t + s * ATT_Q - WINDOW) >= 0)
            k_of_pair = lambda pair: kbuf[s * ATT_Q:s * ATT_Q + nkeys, pair * LANES:(pair + 1) * LANES]
            vt_of_pair = lambda pair: vtbuf[pair * LANES:(pair + 1) * LANES, s * ATT_Q:s * ATT_Q + nkeys]
            return q[s * ATT_Q:(s + 1) * ATT_Q], k_of_pair, vt_of_pair, mask, s * ATT_Q
        return load

    steps, side = _attention_free_steps(hb, conv, win_ref, wco_ref)
    _attention([block(s) for s in range(t // ATT_Q)], sinks_ref, obuf, steps)
    kbuf[0:WINDOW, :] = kbuf[t:t + WINDOW, :]
    vtbuf[:, 0:WINDOW] = vtbuf[:, t:t + WINDOW]

    x1_ref[0] = _mix_out(x, g1, side, obuf[...], wao_ref, wmo_ref)


def _mixer_prompt(x, mod, cos, sin, win, wconv, wco, wao, sinks, wmo):
    b, seq, d = x.shape
    t = MIX_TILE
    return pl.pallas_call(
        _mixer_prompt_kernel,
        grid=(b, seq // t),
        in_specs=[pl.BlockSpec((1, t, d), lambda i, j: (i, j, 0)),
                  pl.BlockSpec((1, 1, 6 * d), lambda i, j: (i, 0, 0)),
                  pl.BlockSpec((t, LANES), lambda i, j: (j, 0)),
                  pl.BlockSpec((t, LANES), lambda i, j: (j, 0)),
                  *[_const_spec(w.shape) for w in win],
                  _const_spec(wconv.shape), _const_spec(wco.shape), _const_spec(wao.shape),
                  pl.BlockSpec(memory_space=pltpu.SMEM),
                  _const_spec(wmo.shape)],
        out_specs=[pl.BlockSpec((1, t, d), lambda i, j: (i, j, 0)),
                   pl.BlockSpec((1, CONV_W - 1, D_CONV), lambda i, j: (i, 0, 0)),
                   pl.BlockSpec((1, WINDOW, KV_DIM), lambda i, j: (i, 0, 0)),
                   pl.BlockSpec((1, WINDOW, KV_DIM), lambda i, j: (i, 0, 0))],
        out_shape=[jax.ShapeDtypeStruct((b, seq, d), F32),
                   jax.ShapeDtypeStruct((b, CONV_W - 1, D_CONV), F32),
                   jax.ShapeDtypeStruct((b, WINDOW, KV_DIM), F32),
                   jax.ShapeDtypeStruct((b, WINDOW, KV_DIM), F32)],
        scratch_shapes=[pltpu.VMEM((SUBLANES + t, D_CONV), F32),
                        pltpu.VMEM((WINDOW + t, KV_DIM), BF16),
                        pltpu.VMEM((KV_DIM, WINDOW + t), BF16),
                        pltpu.VMEM((t, Q_DIM), F32)],
        compiler_params=pltpu.CompilerParams(dimension_semantics=("arbitrary", "arbitrary"),
                                             vmem_limit_bytes=VMEM_LIMIT),
        name="mixer_prompt",
    )(x, mod, cos, sin, *win, wconv, wco, wao, sinks, wmo)


def _mixer_sample_kernel(x_ref, mod_ref, cos_ref, sin_ref, ck_ref, cv_ref, sconv_ref, wall_ref, wq_ref,
                         wconv_ref, wco_ref, wao_ref, sinks_ref, wmo_ref, x1_ref, conv_ref, k_ref, v_ref, ubuf, obuf):
    win_ref = (wall_ref, wq_ref)
    bb, t, d = x_ref.shape
    x3 = x_ref[...]
    mod = mod_ref[...]
    sh1, sc1, g1 = mod[:, :, 0:d], mod[:, :, d:2 * d], mod[:, :, 2 * d:3 * d]
    x = x3.reshape(bb * t, d)
    hb = (_rms(x3) * (1.0 + sc1) + sh1).astype(BF16).reshape(bb * t, d)

    u = _in_proj(hb, win_ref, OFF_GC, OFF_XC) * _in_proj(hb, win_ref, OFF_XC, OFF_Q)
    u3 = u.reshape(bb, t, D_CONV)
    ubuf[:, SUBLANES - 2:SUBLANES, :] = sconv_ref[...]
    ubuf[:, SUBLANES:SUBLANES + t, :] = u3
    wc = wconv_ref[...]
    conv = (wc[0:1] * ubuf[:, SUBLANES - 2:SUBLANES - 2 + t, :] + wc[1:2] * ubuf[:, SUBLANES - 1:SUBLANES - 1 + t, :]
            + wc[2:3] * u3).reshape(bb * t, D_CONV)
    conv_ref[...] = u3[:, t - (CONV_W - 1):t, :]

    cos = jnp.concatenate([cos_ref[...]] * bb, axis=0)
    sin = jnp.concatenate([sin_ref[...]] * bb, axis=0)
    q = (_rope(_in_proj(hb, win_ref, OFF_Q, OFF_K), cos, sin) * ATTN_SCALE).astype(BF16)
    k = _rope(_in_proj(hb, win_ref, OFF_K, OFF_V), cos, sin)
    v = _in_proj(hb, win_ref, OFF_V, OFF_GCONV)
    per = ATT_Q // t
    nkeys = per * (WINDOW + t)
    rq = GQA_GROUP * ATT_Q
    key_stream = lax.broadcasted_iota(I32, (nkeys, rq), 0) // (WINDOW + t)
    query_stream = (lax.broadcasted_iota(I32, (nkeys, rq), 1) % ATT_Q) // t
    mask = key_stream == query_stream
    def block(blk):
        def load():
            k_parts, v_parts = [], []
            for b in range(blk * per, (blk + 1) * per):
                kb, vb = k[b * t:(b + 1) * t], v[b * t:(b + 1) * t]
                ck, cv = ck_ref[b], cv_ref[b]
                k_ref[b] = jnp.concatenate([ck[t:WINDOW], kb], axis=0)
                v_ref[b] = jnp.concatenate([cv[t:WINDOW], vb], axis=0)
                k_parts += [ck, kb]
                v_parts += [cv, vb]
            k_all = jnp.concatenate(k_parts, axis=0).astype(BF16)
            vt_all = jnp.concatenate(v_parts, axis=0).T.astype(BF16)
            k_of_pair = lambda pair: k_all[:, pair * LANES:(pair + 1) * LANES]
            vt_of_pair = lambda pair: vt_all[pair * LANES:(pair + 1) * LANES, :]
            return q[blk * ATT_Q:(blk + 1) * ATT_Q], k_of_pair, vt_of_pair, mask, blk * ATT_Q
        return load

    steps, side = _attention_free_steps(hb, conv, win_ref, wco_ref)
    _attention([block(blk) for blk in range(bb // per)], sinks_ref, obuf, steps)

    g1f = jnp.broadcast_to(g1, (bb, t, d)).reshape(bb * t, d)
    x1_ref[...] = _mix_out(x, g1f, side, obuf[...], wao_ref, wmo_ref).reshape(bb, t, d)


def _mixer_sample(x, mod, cos, sin, ck, cv, sconv, win, wconv, wco, wao, sinks, wmo):
    b, t, d = x.shape
    bb = SAMPLE_BB
    blk = lambda *s: pl.BlockSpec((bb,) + s, lambda i: (i, 0, 0))
    return pl.pallas_call(
        _mixer_sample_kernel,
        grid=(b // bb,),
        in_specs=[blk(t, d), blk(1, 6 * d),
                  pl.BlockSpec((t, LANES), lambda i: (0, 0)), pl.BlockSpec((t, LANES), lambda i: (0, 0)),
                  blk(WINDOW, KV_DIM), blk(WINDOW, KV_DIM), blk(CONV_W - 1, D_CONV),
                  *[_const_spec(w.shape) for w in win],
                  _const_spec(wconv.shape), _const_spec(wco.shape), _const_spec(wao.shape),
                  pl.BlockSpec(memory_space=pltpu.SMEM),
                  _const_spec(wmo.shape)],
        out_specs=[blk(t, d), blk(CONV_W - 1, D_CONV), blk(WINDOW, KV_DIM), blk(WINDOW, KV_DIM)],
        out_shape=[jax.ShapeDtypeStruct((b, t, d), F32),
                   jax.ShapeDtypeStruct((b, CONV_W - 1, D_CONV), F32),
                   jax.ShapeDtypeStruct((b, WINDOW, KV_DIM), F32),
                   jax.ShapeDtypeStruct((b, WINDOW, KV_DIM), F32)],
        scratch_shapes=[pltpu.VMEM((bb, SUBLANES + t, D_CONV), F32),
                        pltpu.VMEM((bb * t, Q_DIM), F32)],
        compiler_params=pltpu.CompilerParams(dimension_semantics=("arbitrary",), vmem_limit_bytes=VMEM_LIMIT),
        name="mixer_sample",
    )(x, mod, cos, sin, ck, cv, sconv, *win, wconv, wco, wao, sinks, wmo)


def _pre_kernel(*refs, prompt_tiles, has_sample):
    if has_sample:
        xp_ref, mp_ref, xs_ref, ms_ref, wsg_ref, wsu_ref, wsd_ref, wrh_ref, wrl_ref, rb_ref, h2_ref, base_ref, cw_ref = refs
    else:
        xp_ref, mp_ref, wsg_ref, wsu_ref, wsd_ref, wrh_ref, wrl_ref, rb_ref, h2_ref, base_ref, cw_ref = refs
    nc, c, d = xp_ref.shape
    t = nc * c
    x3, mod = xp_ref[...], mp_ref[...]
    if has_sample:
        is_prompt = pl.program_id(0) < prompt_tiles
        x3 = jnp.where(is_prompt, x3, xs_ref[...])
        mod = jnp.where(is_prompt, mod, ms_ref[...])
    sh2, sc2, g2 = mod[:, :, 0:d], mod[:, :, d:2 * d], mod[:, :, 2 * d:3 * d]
    h3 = _rms(x3) * (1.0 + sc2) + sh2
    h2 = h3.reshape(t, d)
    hb = h2.astype(BF16)
    h2_ref[...] = _pack_bf16_pairs(h2)
    shared = _dot((_silu(_dot(hb, wsg_ref[...])) * _dot(hb, wsu_ref[...])).astype(BF16), wsd_ref[...])
    base_ref[...] = x3 + g2 * shared.reshape(nc, c, d)

    h_lo = (h2 - hb.astype(F32)).astype(BF16)
    nt = lambda a, b: lax.dot_general(a, b, (((1,), (1,)), ((), ())), preferred_element_type=F32)
    logits = nt(wrh_ref[...], hb) + (nt(wrh_ref[...], h_lo) + nt(wrl_ref[...], hb))
    scores = _sigmoid(logits)
    biased = scores + rb_ref[...]
    g3 = biased.reshape(N_EXPERT_GROUPS, GROUP_SIZE, t)
    member = lax.broadcasted_iota(I32, g3.shape, 1)
    m1 = jnp.max(g3, axis=1, keepdims=True)
    first = jnp.min(jnp.where(g3 == m1, member, GROUP_SIZE), axis=1, keepdims=True)
    m2 = jnp.max(jnp.where(member == first, -jnp.inf, g3), axis=1, keepdims=True)
    gs = m1 + m2
    gidx = lax.broadcasted_iota(I32, gs.shape, 0)
    grank = jnp.zeros(gs.shape, I32)
    for o in range(N_EXPERT_GROUPS):
        other = gs[o:o + 1]
        grank += ((other > gs) | ((other == gs) & (o < gidx))).astype(I32)
    group_ok = grank < TOPK_GROUPS
    slot = jnp.zeros((1, 1, t), I32)
    takes = []
    for gi in range(N_EXPERT_GROUPS):
        ok = group_ok[gi:gi + 1]
        takes.append([ok & (slot == s) for s in range(TOPK_GROUPS)])
        slot = slot + ok.astype(I32)
    packed = []
    for s in range(TOPK_GROUPS):
        vals = jnp.zeros((GROUP_SIZE, t), F32)
        for gi in range(N_EXPERT_GROUPS):
            vals = jnp.where(takes[gi][s][0], g3[gi], vals)
        packed.append(vals)
    cand = jnp.concatenate(packed, axis=0)
    cidx = lax.broadcasted_iota(I32, cand.shape, 0)
    crank = jnp.zeros(cand.shape, I32)
    for o in range(TOPK_GROUPS * GROUP_SIZE):
        other = cand[o:o + 1]
        crank += ((other > cand) | ((other == cand) & (o < cidx))).astype(I32)
    chosen = crank < TOP_K
    sel_groups = []
    for gi in range(N_EXPERT_GROUPS):
        hit = jnp.zeros((GROUP_SIZE, t), jnp.bool_)
        for s in range(TOPK_GROUPS):
            hit = hit | (takes[gi][s][0] & chosen[s * GROUP_SIZE:(s + 1) * GROUP_SIZE])
        sel_groups.append(hit)
    sel = jnp.concatenate(sel_groups, axis=0)
    ssum = jnp.sum(jnp.where(sel, scores, 0.0), axis=0, keepdims=True)
    cw_ref[...] = jnp.where(sel, scores / ssum * ROUTED_SCALE, -1.0)


def _pre(x1_p, p_chunk0, ncp, mod_p, x1_s, ncs, mod_s, wsg, wsu, wsd, wr_hi, wr_lo, rb):
    ncp_all, c, d = x1_p.shape
    nc = PRE_TILE // c
    nchunks = ncp + ncs
    n = nchunks * c
    pt, p0 = ncp // nc, p_chunk0 // nc
    tiles_per_stream = ncp_all // mod_p.shape[0] // nc
    blk3 = pl.BlockSpec((nc, c, d), lambda i: (i, 0, 0))
    p_tile = lambda i: p0 + jnp.minimum(i, pt - 1)
    s_tile = lambda i: jnp.maximum(i - pt, 0)
    s_args, s_specs = [], []
    if ncs:
        s_args = [x1_s, mod_s]
        s_specs = [pl.BlockSpec((nc, c, d), lambda i: (s_tile(i), 0, 0)),
                   pl.BlockSpec((nc, 1, 3 * d), lambda i: (s_tile(i), 0, 0))]
    return pl.pallas_call(
        functools.partial(_pre_kernel, prompt_tiles=pt, has_sample=bool(ncs)),
        grid=(nchunks // nc,),
        in_specs=[pl.BlockSpec((nc, c, d), lambda i: (p_tile(i), 0, 0)),
                  pl.BlockSpec((1, 1, 3 * d), lambda i: (p_tile(i) // tiles_per_stream, 0, 0))] + s_specs + [
                  _const_spec(wsg.shape), _const_spec(wsu.shape), _const_spec(wsd.shape),
                  _const_spec(wr_hi.shape), _const_spec(wr_lo.shape), _const_spec(rb.shape)],
        out_specs=[pl.BlockSpec((nc * c, d // 2), lambda i: (i, 0)), blk3,
                   pl.BlockSpec((N_EXPERTS, nc * c), lambda i: (0, i))],
        out_shape=[jax.ShapeDtypeStruct((n, d // 2), I32),
                   jax.ShapeDtypeStruct((nchunks, c, d), F32),
                   jax.ShapeDtypeStruct((N_EXPERTS, n), F32)],
        compiler_params=pltpu.CompilerParams(dimension_semantics=("arbitrary",), vmem_limit_bytes=VMEM_LIMIT),
        name="pre_ffn",
    )(x1_p, mod_p, *s_args, wsg, wsu, wsd, wr_hi, wr_lo, rb)


def _rank_kernel(cw_ref, rank_ref, cnt_ref, carry):
    i = pl.program_id(0)
    t = cw_ref.shape[1]

    @pl.when(i == 0)
    def _():
        carry[...] = jnp.zeros(carry.shape, F32)

    sel = (cw_ref[...] >= 0.0).astype(BF16)
    r = lax.broadcasted_iota(I32, (t, t), 0)
    c = lax.broadcasted_iota(I32, (t, t), 1)
    before = (r < c).astype(BF16)
    rank = carry[...] + _dot(sel, before)
    rank_ref[...] = rank.astype(I32)
    carry[...] = carry[...] + jnp.sum(sel.astype(F32), axis=1, keepdims=True)
    cnt_ref[...] = carry[...].astype(I32)


def _rank(cw):
    e, n = cw.shape
    t = RANK_TILE
    return pl.pallas_call(
        _rank_kernel,
        grid=(n // t,),
        in_specs=[pl.BlockSpec((e, t), lambda i: (0, i))],
        out_specs=[pl.BlockSpec((e, t), lambda i: (0, i)), pl.BlockSpec((e, 1), lambda i: (0, 0))],
        out_shape=[jax.ShapeDtypeStruct((e, n), I32), jax.ShapeDtypeStruct((e, 1), I32)],
        scratch_shapes=[pltpu.VMEM((e, 1), F32)],
        compiler_params=pltpu.CompilerParams(dimension_semantics=("arbitrary",)),
        name="expert_rank",
    )(cw)


def _slot_kernel(cw_ref, rank_ref, start_ref, pos_ref, w_ref, pos_tok_ref):
    cw = cw_ref[...]
    e, t = cw.shape
    sel = cw >= 0.0
    r = lax.broadcasted_iota(I32, (e, e), 0)
    c = lax.broadcasted_iota(I32, (e, e), 1)
    lower = (c < r).astype(BF16)
    kidx = _dot(lower, sel.astype(BF16))
    posf = start_ref[...].astype(F32) + rank_ref[...].astype(F32)
    pos_rows, w_rows = [], []
    for k in range(TOP_K):
        m = sel & (kidx == float(k))
        pos_rows.append(jnp.sum(jnp.where(m, posf, 0.0), axis=0, keepdims=True))
        w_rows.append(jnp.sum(jnp.where(m, cw, 0.0), axis=0, keepdims=True))
    pos_ref[...] = jnp.concatenate(pos_rows, axis=0).astype(I32)
    w_ref[...] = jnp.concatenate([jnp.broadcast_to(w, (SC_LANES, t)) for w in w_rows], axis=0).T
    pos_pad = jnp.concatenate(pos_rows + [jnp.zeros((LANES - TOP_K, t), F32)], axis=0)
    pos_tok_ref[...] = pos_pad.T[:, :TOP_K].astype(I32)


def _slots(cw, rank, seg_start):
    e, n = cw.shape
    t = RANK_TILE
    return pl.pallas_call(
        _slot_kernel,
        grid=(n // t,),
        in_specs=[pl.BlockSpec((e, t), lambda i: (0, i)), pl.BlockSpec((e, t), lambda i: (0, i)),
                  pl.BlockSpec((e, 1), lambda i: (0, 0))],
        out_specs=[pl.BlockSpec((TOP_K, t), lambda i: (0, i)), pl.BlockSpec((t, TOP_K * SC_LANES), lambda i: (i, 0)),
                   pl.BlockSpec((t, TOP_K), lambda i: (i, 0))],
        out_shape=[jax.ShapeDtypeStruct((TOP_K, n), I32), jax.ShapeDtypeStruct((n, TOP_K * SC_LANES), F32),
                   jax.ShapeDtypeStruct((n, TOP_K), I32)],
        compiler_params=pltpu.CompilerParams(dimension_semantics=("arbitrary",)),
        name="expert_slots",
    )(cw, rank, seg_start)


def _sc_mesh():
    return plsc.VectorSubcoreMesh(core_axis_name="c", subcore_axis_name="s")


def _sc_worker_id():
    return lax.axis_index("s") * (SC_WORKERS // 16) + lax.axis_index("c")


def _sc_dispatch(rows, pos, n_rows):
    n, d = rows.shape
    per_w = n // SC_WORKERS
    w = SC_WINDOW
    n_chunks = per_w // w

    @functools.partial(
        pl.kernel, mesh=_sc_mesh(),
        out_type=jax.ShapeDtypeStruct((n_rows, d), rows.dtype),
        scratch_types=[pltpu.VMEM((2, TOP_K, w), I32), pltpu.VMEM((2, w, d), rows.dtype),
                       pltpu.SemaphoreType.DMA((2,)), pltpu.SemaphoreType.DMA((2,)), pltpu.SemaphoreType.DMA((2,))],
        name="sc_dispatch")
    def k(rows_hbm, pos_hbm, o_hbm, idx_v, rows_v, row_sem, idx_sem, out_sem):
        wid = _sc_worker_id()
        base = wid * per_w

        def loads(c, slot):
            off = pl.multiple_of(base + c * w, SUBLANES)
            return (pltpu.make_async_copy(rows_hbm.at[pl.ds(off, w)], rows_v.at[slot], row_sem.at[slot]),
                    pltpu.make_async_copy(pos_hbm.at[wid * n_chunks + c], idx_v.at[slot], idx_sem.at[slot]))

        def scatters(slot):
            return [pltpu.make_async_copy(rows_v.at[slot], o_hbm.at[idx_v.at[slot, kk]], out_sem.at[slot])
                    for kk in range(TOP_K)]

        for cp in loads(0, 0):
            cp.start()
        for c in range(n_chunks):
            slot = c % 2
            for cp in loads(c, slot):
                cp.wait()
            for cp in scatters(slot):
                cp.start()
            if c >= 1:
                for cp in scatters(1 - slot):
                    cp.wait()
            if c + 1 < n_chunks:
                for cp in loads(c + 1, 1 - slot):
                    cp.start()
        for cp in scatters((n_chunks - 1) % 2):
            cp.wait()

    pos_chunks = pos.reshape(TOP_K, n // w, w).transpose(1, 0, 2)
    return k(rows, pos_chunks)


def _sc_collect_sum(rows, pos_tok, w_lanes):
    words = rows.shape[1]
    n = w_lanes.shape[0]
    lanes = SC_LANES
    per_w = n // SC_WORKERS
    tw = SC_SUM_TOKENS
    n_pairs = per_w // (2 * tw)
    col_blocks = words // lanes // SC_SUM_VREGS

    @functools.partial(
        pl.kernel, mesh=_sc_mesh(),
        out_type=jax.ShapeDtypeStruct((n, 2 * words), F32),
        scratch_types=[pltpu.VMEM((per_w * TOP_K,), I32), pltpu.VMEM((2, tw * TOP_K, words), I32),
                       pltpu.VMEM((2, tw, TOP_K * lanes), F32), pltpu.VMEM((2, tw, 2 * words), F32),
                       pltpu.SemaphoreType.DMA((2,)), pltpu.SemaphoreType.DMA((2,)), pltpu.SemaphoreType.DMA((2,))],
        compiler_params=pltpu.CompilerParams(needs_layout_passes=False),
        name="sc_collect_sum")
    def k(rows_hbm, pos_hbm, w_hbm, o_hbm, idx_v, rows_v, w_v, out_v, in_sem, w_sem, out_sem):
        base = pl.multiple_of(_sc_worker_id() * per_w, SUBLANES)
        pltpu.sync_copy(pos_hbm.at[pl.ds(pl.multiple_of(base * TOP_K, SUBLANES), per_w * TOP_K)], idx_v)

        def loads(c, slot):
            idx = idx_v.at[pl.ds(pl.multiple_of(c * tw * TOP_K, SUBLANES), tw * TOP_K)]
            tok0 = pl.multiple_of(base + c * tw, SUBLANES)
            return (pltpu.make_async_copy(rows_hbm.at[idx], rows_v.at[slot], in_sem.at[slot]),
                    pltpu.make_async_copy(w_hbm.at[pl.ds(tok0, tw)], w_v.at[slot], w_sem.at[slot]))

        def write(c, slot):
            tok0 = pl.multiple_of(base + c * tw, SUBLANES)
            return pltpu.make_async_copy(out_v.at[slot], o_hbm.at[pl.ds(tok0, tw)], out_sem.at[slot])

        high_half = jnp.full((lanes,), -65536, I32)
        sixteen = jnp.full((lanes,), 16, I32)

        def reduce_window(slot):
            rv, wv, ov = rows_v.at[slot], w_v.at[slot], out_v.at[slot]

            @pl.loop(0, tw)
            def _(t):
                for cb in range(col_blocks):
                    acc_lo, acc_hi = [None] * SC_SUM_VREGS, [None] * SC_SUM_VREGS
                    for kk in range(TOP_K):
                        wk = wv[t, pl.ds(kk * lanes, lanes)]
                        for c in range(SC_SUM_VREGS):
                            wd = rv[t * TOP_K + kk, pl.ds((cb * SC_SUM_VREGS + c) * lanes, lanes)]
                            lo = wk * plsc.bitcast(lax.shift_left(wd, sixteen), F32)
                            hi = wk * plsc.bitcast(wd & high_half, F32)
                            acc_lo[c] = lo if kk == 0 else acc_lo[c] + lo
                            acc_hi[c] = hi if kk == 0 else acc_hi[c] + hi
                    for c in range(SC_SUM_VREGS):
                        col = (cb * SC_SUM_VREGS + c) * lanes
                        ov[t, pl.ds(col, lanes)] = acc_lo[c]
                        ov[t, pl.ds(words + col, lanes)] = acc_hi[c]

        for cp in loads(0, 0):
            cp.start()

        @pl.loop(0, n_pairs)
        def _(p):
            c0 = 2 * p
            for cp in loads(c0 + 1, 1):
                cp.start()
            for cp in loads(c0, 0):
                cp.wait()
            reduce_window(0)
            write(c0, 0).start()
            for cp in loads(c0 + 1, 1):
                cp.wait()
            reduce_window(1)
            write(c0 + 1, 1).start()
            write(c0, 0).wait()

            @pl.when(p + 1 < n_pairs)
            def _():
                for cp in loads(c0 + 2, 0):
                    cp.start()

            write(c0 + 1, 1).wait()

    return k(rows, pos_tok, w_lanes)


def _gmm_kernel(tot_ref, ce_ref, row_ref, val_ref, ord_ref, nxt_ref, x_hbm, wg_hbm, wu_hbm, wd_hbm, y_hbm,
                wgb, wub, wdb, xbuf, xsem, ybuf, ysem, wgf, wuf, wdf, wsem):
    total = tot_ref[0]
    pieces = GMM_SUB // GMM_TAIL

    def w_copies(ex, slot):
        return [pltpu.make_async_copy(src.at[ex], dst.at[slot], wsem.at[slot, i])
                for i, (src, dst) in enumerate(((wg_hbm, wgf), (wu_hbm, wuf), (wd_hbm, wdf)))]

    def x_copy(g):
        slot = g % GMM_X_SLOTS
        rows = pl.ds(pl.multiple_of(row_ref[g], GMM_TAIL), GMM_SUB)
        return pltpu.make_async_copy(x_hbm.at[rows], xbuf.at[slot], xsem.at[slot])

    def y_piece(g, p):
        slot = g % 2
        rows = pl.ds(pl.multiple_of(row_ref[g] + p * GMM_TAIL, GMM_TAIL), GMM_TAIL)
        return pltpu.make_async_copy(ybuf.at[slot, pl.ds(p * GMM_TAIL, GMM_TAIL)], y_hbm.at[rows], ysem.at[slot])

    def for_y_pieces(g, action):
        for p in range(pieces):
            @pl.when(p * GMM_TAIL < val_ref[g])
            def _():
                action(y_piece(g, p))

    for ahead in range(GMM_X_SLOTS - 1):
        @pl.when(ahead < total)
        def _():
            x_copy(ahead).start()

    def chunk(g, carry):
        e = ce_ref[g]
        prev = ce_ref[jnp.maximum(g - 1, 0)]

        @pl.when((g == 0) | (e != prev))
        def _():
            slot = ord_ref[e] % 2

            @pl.when(g == 0)
            def _():
                for cp in w_copies(e, slot):
                    cp.start()

            for cp in w_copies(e, slot):
                cp.wait()
            wgb[...] = wgf[slot].astype(BF16)
            wub[...] = wuf[slot].astype(BF16)
            wdb[...] = wdf[slot].astype(BF16)
            nxt = nxt_ref[e]

            @pl.when(nxt >= 0)
            def _():
                for cp in w_copies(nxt, 1 - slot):
                    cp.start()

        x_copy(g).wait()

        @pl.when(g + GMM_X_SLOTS - 1 < total)
        def _():
            x_copy(g + GMM_X_SLOTS - 1).start()

        @pl.when(g >= 2)
        def _():
            for_y_pieces(g - 2, lambda cp: cp.wait())

        x_ref = xbuf.at[g % GMM_X_SLOTS]
        y_ref = ybuf.at[g % 2]

        def expert_rows(r0, n):
            rows = pl.ds(r0, n)
            lo, hi = _unpack_bf16_pairs(x_ref[rows, :])
            xb = jnp.concatenate([lo.astype(BF16), hi.astype(BF16)], axis=1)
            mid = (_silu(_dot(xb, wgb[...])) * _dot(xb, wub[...])).astype(BF16)
            y_ref[rows, :] = _pack_bf16_pairs(_dot(mid, wdb[...]))

        n_real = val_ref[g]

        @pl.when(n_real == GMM_SUB)
        def _():
            expert_rows(0, GMM_SUB)

        @pl.when((n_real < GMM_SUB) & (n_real >= GMM_MID))
        def _():
            expert_rows(0, GMM_MID)

        @pl.when(n_real < GMM_SUB)
        def _():
            done = jnp.where(n_real >= GMM_MID, GMM_MID, 0)

            @pl.loop(0, (n_real - done + GMM_TAIL - 1) // GMM_TAIL)
            def _(i):
                expert_rows(pl.multiple_of(done + i * GMM_TAIL, GMM_TAIL), GMM_TAIL)

        for_y_pieces(g, lambda cp: cp.start())
        return carry

    lax.fori_loop(0, total, chunk, 0)
    for back in (2, 1):
        @pl.when(total >= back)
        def _():
            for_y_pieces(total - back, lambda cp: cp.wait())


def _gmm(x_sorted, n_chunks, chunk_e, chunk_row, chunk_valid, e_ord, e_next, wg, wu, wd):
    r, half = x_sorted.shape
    d = 2 * half
    any_spec = pl.BlockSpec(memory_space=pl.ANY)
    return pl.pallas_call(
        _gmm_kernel,
        grid_spec=pltpu.PrefetchScalarGridSpec(
            num_scalar_prefetch=6,
            grid=(1,),
            in_specs=[any_spec, any_spec, any_spec, any_spec],
            out_specs=any_spec,
            scratch_shapes=[pltpu.VMEM((d, D_EXPERT), BF16), pltpu.VMEM((d, D_EXPERT), BF16),
                            pltpu.VMEM((D_EXPERT, d), BF16),
                            pltpu.VMEM((GMM_X_SLOTS, GMM_SUB, half), I32), pltpu.SemaphoreType.DMA((GMM_X_SLOTS,)),
                            pltpu.VMEM((2, GMM_SUB, half), I32), pltpu.SemaphoreType.DMA((2,)),
                            pltpu.VMEM((2, d, D_EXPERT), F32), pltpu.VMEM((2, d, D_EXPERT), F32),
                            pltpu.VMEM((2, D_EXPERT, d), F32), pltpu.SemaphoreType.DMA((2, 3))]),
        out_shape=jax.ShapeDtypeStruct((r, half), I32),
        compiler_params=pltpu.CompilerParams(dimension_semantics=("arbitrary",), vmem_limit_bytes=VMEM_LIMIT),
        name="expert_gmm",
    )(n_chunks, chunk_e, chunk_row, chunk_valid, e_ord, e_next, x_sorted, wg, wu, wd)


def _combine_kernel(base_ref, mod_ref, routed_ref, gain_ref, *rest):
    y_ref = rest[-1]
    d = base_ref.shape[-1]
    g2 = mod_ref[...][:, :, 2 * d:3 * d]
    out = base_ref[...] + g2 * routed_ref[...]
    y_ref[...] = _rms(out) * gain_ref[...]


def _combine(base, mod, routed, gain, first_chunk, n_chunks, out_chunks, out_first_chunk, out_buf=None):
    _, c, d = base.shape
    nc = COMB_TILE // c
    t0, o0 = first_chunk // nc, out_first_chunk // nc
    chunks_per_stream = out_chunks // mod.shape[0]
    if chunks_per_stream == 1:
        mod_spec = pl.BlockSpec((nc, 1, 3 * d), lambda i: (o0 + i, 0, 0))
    else:
        assert chunks_per_stream % nc == 0
        mod_spec = pl.BlockSpec((1, 1, 3 * d), lambda i: ((o0 + i) * nc // chunks_per_stream, 0, 0))
    blk3 = pl.BlockSpec((nc, c, d), lambda i: (t0 + i, 0, 0))
    in_specs = [blk3, mod_spec, blk3, pl.BlockSpec((1, 1, d), lambda i: (0, 0, 0))]
    args = [base, mod, routed, gain.reshape(1, 1, d)]
    aliases = {}
    if out_buf is not None:
        in_specs.append(pl.BlockSpec(memory_space=pl.ANY))
        args.append(out_buf)
        aliases = {len(args) - 1: 0}
    return pl.pallas_call(
        _combine_kernel,
        grid=(n_chunks // nc,),
        in_specs=in_specs,
        out_specs=pl.BlockSpec((nc, c, d), lambda i: (o0 + i, 0, 0)),
        out_shape=jax.ShapeDtypeStruct((out_chunks, c, d), F32),
        input_output_aliases=aliases,
        compiler_params=pltpu.CompilerParams(dimension_semantics=("arbitrary",), vmem_limit_bytes=VMEM_LIMIT),
        name="combine_norm",
    )(*args)


def _rope_tables(pos):
    half = HEAD_DIM // 2
    lane = jnp.arange(LANES, dtype=I32)
    inv_freq = ROPE_THETA ** (-(lane % half).astype(F32) / half)
    sign = jnp.where((lane % HEAD_DIM) < half, -1.0, 1.0).astype(F32)
    ang = pos.astype(F32)[:, None] * inv_freq[None, :]
    return jnp.cos(ang), jnp.sin(ang) * sign[None, :]


def _routed_ffn(h2, cw, w_gate, w_up, w_down):
    n, half = h2.shape
    rank, counts = _rank(cw)
    counts = counts[:, 0]
    padded = (counts + GMM_TAIL - 1) // GMM_TAIL * GMM_TAIL
    seg_start = (jnp.cumsum(padded) - padded).astype(I32)
    n_rows = n * TOP_K + N_EXPERTS * GMM_TAIL + GMM_SUB
    e_chunks = (counts + GMM_SUB - 1) // GMM_SUB
    chunk_end = jnp.cumsum(e_chunks)
    max_chunks = n * TOP_K // GMM_SUB + N_EXPERTS
    g = jnp.arange(max_chunks, dtype=I32)
    chunk_e = jnp.minimum(jnp.sum((chunk_end[None, :] <= g[:, None]).astype(I32), axis=1), N_EXPERTS - 1)
    eids = jnp.arange(N_EXPERTS, dtype=I32)
    own = chunk_e[:, None] == eids[None, :]
    pick = lambda table: jnp.sum(jnp.where(own, table[None, :], 0), axis=1)
    in_expert = (g - pick(chunk_end - e_chunks)) * GMM_SUB
    chunk_row = (pick(seg_start) + in_expert).astype(I32)
    chunk_valid = jnp.clip(pick(counts) - in_expert, 0, GMM_SUB).astype(I32)
    n_chunks = chunk_end[-1:].astype(I32)
    has_rows = counts > 0
    e_ord = (jnp.cumsum(has_rows.astype(I32)) - has_rows.astype(I32)).astype(I32)
    later = has_rows[None, :] & (eids[None, :] > eids[:, None])
    e_next = jnp.min(jnp.where(later, eids[None, :], N_EXPERTS), axis=1)
    e_next = jnp.where(e_next == N_EXPERTS, -1, e_next).astype(I32)
    pos, w_lanes, pos_tok = _slots(cw, rank, seg_start[:, None])
    x_sorted = _sc_dispatch(h2, pos, n_rows)
    y_sorted = _gmm(x_sorted, n_chunks, chunk_e, chunk_row, chunk_valid, e_ord, e_next, w_gate, w_up, w_down)
    return _sc_collect_sum(y_sorted, pos_tok.reshape(n * TOP_K), w_lanes)


def kernel(x_prompt, x_sample, cache_k, cache_v, state_conv, c_prompt, c_sample, w_ada, b_ada, w_in, w_conv,
           w_conv_out, w_attn_o, attn_sinks, w_mix_out, w_router, router_bias, w_exp_gate, w_exp_up, w_exp_down,
           w_sh_gate, w_sh_up, w_sh_down, final_gain):
    assert w_ada.shape[0] == 1, "one layer"
    bp, seq, d = x_prompt.shape
    bs, ts, _ = x_sample.shape
    assert ts == CHUNK and seq % MIX_TILE == 0 and bs % SAMPLE_BB == 0

    c_all = jnp.concatenate([c_prompt, c_sample], axis=0)
    pad = (-c_all.shape[0]) % SUBLANES
    mod = _ada(jnp.pad(c_all, ((0, pad), (0, 0))), w_ada[0], b_ada[0])[:bp + bs]
    mod_p, mod_s = mod[:bp, None, :], mod[bp:, None, :]

    head_axes = (N_KV_HEADS // 2, 2, GQA_GROUP, HEAD_DIM)
    w_in_l = w_in[0]
    w_q = w_in_l[:, OFF_Q:OFF_K].reshape((d,) + head_axes).transpose(0, 1, 3, 2, 4).reshape(d, Q_DIM)
    w_o = w_attn_o[0].reshape(head_axes + (d,)).transpose(0, 2, 1, 3, 4).reshape(Q_DIM, d)
    win = (w_in_l.astype(BF16), w_q.astype(BF16))
    wco, wao, wmo = (w.astype(BF16) for w in (w_conv_out[0], w_o, w_mix_out[0]))
    cos_p, sin_p = _rope_tables(jnp.arange(seq, dtype=I32))
    cos_s, sin_s = _rope_tables(PAST_LEN + jnp.arange(ts, dtype=I32))

    x1_p, conv_p, k_p, v_p = _mixer_prompt(x_prompt, mod_p, cos_p, sin_p, win, w_conv[0], wco, wao, attn_sinks[0], wmo)
    x1_s, conv_s, k_s, v_s = _mixer_sample(
        x_sample, mod_s, cos_s, sin_s, cache_k[0].reshape(bs, WINDOW, KV_DIM), cache_v[0].reshape(bs, WINDOW, KV_DIM),
        state_conv[0], win, w_conv[0], wco, wao, attn_sinks[0], wmo)

    n_p, n_s = bp * seq, bs * ts
    n = n_p + n_s
    mod2_p, mod2_s = mod[:bp, None, 3 * d:], mod[bp:, None, 3 * d:]
    x1_pc = x1_p.reshape(n_p // CHUNK, CHUNK, d)
    wsg, wsu, wsd = (w[0].astype(BF16) for w in (w_sh_gate, w_sh_up, w_sh_down))
    wr_t, rb = w_router[0].T, router_bias[0][:, None]
    wr_hi = wr_t.astype(BF16)
    wr_lo = (wr_t - wr_hi.astype(F32)).astype(BF16)

    ncp, ncs = n_p // CHUNK, n_s // CHUNK
    half = (ncp + ncs) * FFN_SET_A_SHARE[0] // FFN_SET_A_SHARE[1]
    tile_chunks = max(PRE_TILE, COMB_TILE) // CHUNK
    assert half <= ncp and half % tile_chunks == 0 and (ncp - half) % tile_chunks == 0 and ncs % tile_chunks == 0
    for set_tokens in (half * CHUNK, n - half * CHUNK):
        assert set_tokens % (SC_WORKERS * SC_WINDOW) == 0 and set_tokens % RANK_TILE == 0
        assert set_tokens % (SC_WORKERS * 2 * SC_SUM_TOKENS) == 0
    y_p = None
    for p0, np_c, ns_c in ((0, half, 0), (half, ncp - half, ncs)):
        h2, base, cw = _pre(x1_pc, p0, np_c, mod2_p, x1_s, ns_c, mod2_s, wsg, wsu, wsd, wr_hi, wr_lo, rb)
        routed = _routed_ffn(h2, cw, w_exp_gate[0], w_exp_up[0], w_exp_down[0]).reshape(base.shape)
        y_p = _combine(base, mod2_p, routed, final_gain, 0, np_c, ncp, p0, out_buf=y_p)
        if ns_c:
            y_s = _combine(base, mod2_s, routed, final_gain, np_c, ns_c, ncs, 0)

    kv = lambda a: a.reshape(1, a.shape[0], WINDOW, N_KV_HEADS, HEAD_DIM)
    return (y_p.reshape(bp, seq, d), y_s, conv_p[None], kv(k_p), kv(v_p), conv_s[None], kv(k_s), kv(v_s))
```

```python
import functools

import jax
import jax.numpy as jnp
from jax import lax
from jax.experimental import pallas as pl
from jax.experimental.pallas import tpu as pltpu
from jax.experimental.pallas import tpu_sc as plsc

F32 = jnp.float32
BF16 = jnp.bfloat16
I32 = jnp.int32

D_MODEL = 1024
CHUNK = 64
D_CONV = 1024
CONV_W = 3
N_HEADS = 16
N_KV_HEADS = 4
HEAD_DIM = 64
GQA_GROUP = N_HEADS // N_KV_HEADS
WINDOW = 128
ROPE_THETA = 10000.0
ATTN_SCALE = HEAD_DIM ** -0.5
N_EXPERTS = 64
TOP_K = 8
N_EXPERT_GROUPS = 8
GROUP_SIZE = N_EXPERTS // N_EXPERT_GROUPS
TOPK_GROUPS = 4
D_EXPERT = 256
D_SHARED = 256
ROUTED_SCALE = 2.5
EPS = 1e-6
PAST_LEN = 4096
Q_DIM = N_HEADS * HEAD_DIM
KV_DIM = N_KV_HEADS * HEAD_DIM
OFF_GB, OFF_GC, OFF_XC, OFF_Q, OFF_K, OFF_V, OFF_GCONV, OFF_GATTN, D_IN = (
    0, 1024, 2048, 3072, 4096, 4352, 4608, 5632, 6656)

LANES = 128
SUBLANES = 8
VMEM_LIMIT = 56 * 1024 * 1024

MIX_TILE = 512
ATT_Q = 128
MIX_SIDE_COLS = 256
SAMPLE_BB = 8
PRE_TILE = 512
RANK_TILE = 512
GMM_SUB = 512
GMM_MID = 256
GMM_TAIL = 128
GMM_X_SLOTS = 3
COMB_TILE = 512
FFN_SET_A_SHARE = (2, 3)
SC_WORKERS = 32
SC_WINDOW = 96
SC_LANES = 16
SC_SUM_TOKENS = 8
SC_SUM_VREGS = 16


def _const_spec(shape):
    nd = len(shape)
    return pl.BlockSpec(shape, lambda *_: (0,) * nd, pipeline_mode=pl.Buffered(1))


def _rms(x):
    return x * lax.rsqrt(jnp.mean(x * x, axis=-1, keepdims=True) + EPS)


def _sigmoid(x):
    return 1.0 / (1.0 + jnp.exp(-x))


def _silu(x):
    return x * _sigmoid(x)


def _dot(a, b):
    return jnp.dot(a, b, preferred_element_type=F32)


def _pack_bf16_pairs(x):
    half = x.shape[-1] // 2
    lo = lax.bitcast_convert_type(x[..., :half].astype(BF16).astype(F32), I32)
    hi = lax.bitcast_convert_type(x[..., half:].astype(BF16).astype(F32), I32)
    return lax.shift_right_logical(lo, 16) | hi


def _unpack_bf16_pairs(words):
    lo = lax.bitcast_convert_type(lax.shift_left(words, 16), F32)
    hi = lax.bitcast_convert_type(words & jnp.int32(-65536), F32)
    return lo, hi


def _ada_kernel(c_ref, w_ref, b_ref, o_ref):
    s = _silu(c_ref[...]).astype(BF16)
    o_ref[...] = _dot(s, w_ref[...].astype(BF16)) + b_ref[...]


def _ada(c_all, w_ada, b_ada):
    rows = c_all.shape[0]
    n_out = w_ada.shape[1]
    bn = 768
    return pl.pallas_call(
        _ada_kernel,
        grid=(n_out // bn,),
        in_specs=[pl.BlockSpec((rows, D_MODEL), lambda i: (0, 0)),
                  pl.BlockSpec((D_MODEL, bn), lambda i: (0, i)),
                  pl.BlockSpec((1, bn), lambda i: (0, i))],
        out_specs=pl.BlockSpec((rows, bn), lambda i: (0, i)),
        out_shape=jax.ShapeDtypeStruct((rows, n_out), F32),
        name="ada_mod",
    )(c_all, w_ada, b_ada.reshape(1, n_out))


def _rope(x, cos, sin_signed):
    lane = lax.broadcasted_iota(I32, (x.shape[0], LANES), 1)
    first_half = (lane % HEAD_DIM) < (HEAD_DIM // 2)
    outs = []
    for g in range(x.shape[1] // LANES):
        xg = x[:, g * LANES:(g + 1) * LANES]
        up = pltpu.roll(xg, LANES - HEAD_DIM // 2, axis=1)
        down = pltpu.roll(xg, HEAD_DIM // 2, axis=1)
        partner = jnp.where(first_half, up, down)
        outs.append(xg * cos + partner * sin_signed)
    return jnp.concatenate(outs, axis=1)


def _attention(blocks, sinks_ref, obuf, between):
    rq = GQA_GROUP * ATT_Q
    low = lax.broadcasted_iota(I32, (ATT_Q, LANES), 1) < HEAD_DIM
    head_of_lane = lax.broadcasted_iota(I32, (1, rq), 1) // ATT_Q
    units = [(b, pair, par) for b in range(len(blocks)) for pair in range(N_KV_HEADS // 2) for par in range(2)]
    loaded = {}

    def scores(u):
        b, pair, par = units[u]
        if b not in loaded:
            loaded.clear()
            loaded[b] = blocks[b]()
        q_blk, k_of_pair, _, mask, _ = loaded[b]
        keep = low if par == 0 else jnp.logical_not(low)
        cols = [q_blk[:, (GQA_GROUP * pair + i) * LANES:(GQA_GROUP * pair + i + 1) * LANES] for i in range(GQA_GROUP)]
        qg = jnp.concatenate([jnp.where(keep, c, jnp.zeros_like(c)) for c in cols], axis=0)
        st = lax.dot_general(k_of_pair(pair), qg, (((1,), (1,)), ((), ())), preferred_element_type=F32)
        vt = loaded[b][2](pair)[par * HEAD_DIM:(par + 1) * HEAD_DIM, :]
        return jnp.where(mask, st, -jnp.inf), vt, loaded[b][4]

    outs = []
    nxt = scores(0)
    for u, (b, pair, par) in enumerate(units):
        st, vt, row0 = nxt
        if u + 1 < len(units):
            nxt = scores(u + 1)
        if between:
            between.pop(0)()
        g = 2 * pair + par
        sink = jnp.full((1, rq), sinks_ref[g * GQA_GROUP + GQA_GROUP - 1], F32)
        for i in range(GQA_GROUP - 2, -1, -1):
            sink = jnp.where(head_of_lane == i, sinks_ref[g * GQA_GROUP + i], sink)
        m = jnp.maximum(jnp.max(st, axis=0, keepdims=True), sink)
        e = jnp.exp(st - m)
        z = jnp.sum(e, axis=0, keepdims=True) + jnp.exp(sink - m)
        outs.append(_dot(vt, e.astype(BF16)) / z)
        if par == 1:
            for i in range(GQA_GROUP):
                blk = jnp.concatenate([o[:, i * ATT_Q:(i + 1) * ATT_Q] for o in outs], axis=0)
                c0 = (GQA_GROUP * pair + i) * LANES
                obuf[row0:row0 + ATT_Q, c0:c0 + LANES] = blk.T
            outs = []
    for step in between:
        step()


def _in_proj(hb, win_refs, lo, hi):
    w_all, w_q = win_refs
    if (lo, hi) == (OFF_Q, OFF_K):
        return _dot(hb, w_q[...])
    assert hi <= OFF_Q or lo >= OFF_K
    return _dot(hb, w_all[:, lo:hi])


def _attention_free_steps(hb, conv, win_ref, wco_ref):
    out, parts = {}, {}
    n_parts = D_MODEL // MIX_SIDE_COLS

    def step(name, piece, compute):
        def run():
            parts.setdefault(name, []).append(compute(piece * MIX_SIDE_COLS, (piece + 1) * MIX_SIDE_COLS))
            if piece == n_parts - 1:
                out[name] = jnp.concatenate(parts.pop(name), axis=1)
        return run

    def conv_in():
        if "conv_in" not in out:
            out["conv_in"] = (out.pop("gate_b") * conv).astype(BF16)
        return out["conv_in"]

    computes = [("gate_b", lambda lo, hi: _in_proj(hb, win_ref, OFF_GB + lo, OFF_GB + hi)),
                ("g_conv", lambda lo, hi: _in_proj(hb, win_ref, OFF_GCONV + lo, OFF_GCONV + hi)),
                ("g_attn", lambda lo, hi: _in_proj(hb, win_ref, OFF_GATTN + lo, OFF_GATTN + hi)),
                ("y_conv", lambda lo, hi: _dot(conv_in(), wco_ref[:, lo:hi]))]
    return [step(name, p, fn) for name, fn in computes for p in range(n_parts)], out


def _mix_out(x, g1, side, y_attn_in, wao_ref, wmo_ref):
    y_attn = _dot(y_attn_in.astype(BF16), wao_ref[...])
    merged = _sigmoid(side["g_conv"]) * side["y_conv"] + _sigmoid(side["g_attn"]) * y_attn
    return x + g1 * _dot(merged.astype(BF16), wmo_ref[...])


def _mixer_prompt_kernel(x_ref, mod_ref, cos_ref, sin_ref, wall_ref, wq_ref, wconv_ref, wco_ref, wao_ref,
                         sinks_ref, wmo_ref, x1_ref, conv_ref, k_ref, v_ref, ubuf, kbuf, vtbuf, obuf):
    win_ref = (wall_ref, wq_ref)
    j = pl.program_id(1)
    t = x_ref.shape[1]

    @pl.when(j == 0)
    def _():
        ubuf[0:SUBLANES, :] = jnp.zeros((SUBLANES, D_CONV), F32)
        kbuf[0:WINDOW, :] = jnp.zeros((WINDOW, KV_DIM), BF16)
        vtbuf[:, 0:WINDOW] = jnp.zeros((KV_DIM, WINDOW), BF16)

    x = x_ref[0]
    mod = mod_ref[0]
    sh1, sc1, g1 = mod[:, 0:D_MODEL], mod[:, D_MODEL:2 * D_MODEL], mod[:, 2 * D_MODEL:3 * D_MODEL]
    hb = (_rms(x) * (1.0 + sc1) + sh1).astype(BF16)

    u = _in_proj(hb, win_ref, OFF_GC, OFF_XC) * _in_proj(hb, win_ref, OFF_XC, OFF_Q)
    ubuf[SUBLANES:SUBLANES + t, :] = u
    wc = wconv_ref[...]
    conv = wc[0:1] * ubuf[SUBLANES - 2:SUBLANES - 2 + t, :] + wc[1:2] * ubuf[SUBLANES - 1:SUBLANES - 1 + t, :] + wc[2:3] * u
    conv_ref[0] = u[t - (CONV_W - 1):t]
    ubuf[SUBLANES - 2:SUBLANES, :] = u[t - (CONV_W - 1):t]

    cos, sin = cos_ref[...], sin_ref[...]
    q = (_rope(_in_proj(hb, win_ref, OFF_Q, OFF_K), cos, sin) * ATTN_SCALE).astype(BF16)
    k = _rope(_in_proj(hb, win_ref, OFF_K, OFF_V), cos, sin)
    v = _in_proj(hb, win_ref, OFF_V, OFF_GCONV)
    kbuf[WINDOW:WINDOW + t, :] = k.astype(BF16)
    vtbuf[:, WINDOW:WINDOW + t] = v.T.astype(BF16)
    k_ref[0] = k[t - WINDOW:t]
    v_ref[0] = v[t - WINDOW:t]

    nkeys = ATT_Q + WINDOW
    rq = GQA_GROUP * ATT_Q
    ki = lax.broadcasted_iota(I32, (nkeys, rq), 0)
    qi = lax.broadcasted_iota(I32, (nkeys, rq), 1) % ATT_Q
    band = ki // CHUNK - qi // CHUNK
    band_ok = (band >= 0) & (band <= WINDOW // CHUNK)
    def block(s):
        def load():
            mask = band_ok & (ki + (j * t + s * ATT_Q - WINDOW) >= 0)
            k_of_pair = lambda pair: kbuf[s * ATT_Q:s * ATT_Q + nkeys, pair * LANES:(pair + 1) * LANES]
            vt_of_pair = lambda pair: vtbuf[pair * LANES:(pair + 1) * LANES, s * ATT_Q:s * ATT_Q + nkeys]
            return q[s * ATT_Q:(s + 1) * ATT_Q], k_of_pair, vt_of_pair, mask, s * ATT_Q
        return load

    steps, side = _attention_free_steps(hb, conv, win_ref, wco_ref)
    _attention([block(s) for s in range(t // ATT_Q)], sinks_ref, obuf, steps)
    kbuf[0:WINDOW, :] = kbuf[t:t + WINDOW, :]
    vtbuf[:, 0:WINDOW] = vtbuf[:, t:t + WINDOW]

    x1_ref[0] = _mix_out(x, g1, side, obuf[...], wao_ref, wmo_ref)


def _mixer_prompt(x, mod, cos, sin, win, wconv, wco, wao, sinks, wmo):
    b, seq, d = x.shape
    t = MIX_TILE
    return pl.pallas_call(
        _mixer_prompt_kernel,
        grid=(b, seq // t),
        in_specs=[pl.BlockSpec((1, t, d), lambda i, j: (i, j, 0)),
                  pl.BlockSpec((1, 1, 6 * d), lambda i, j: (i, 0, 0)),
                  pl.BlockSpec((t, LANES), lambda i, j: (j, 0)),
                  pl.BlockSpec((t, LANES), lambda i, j: (j, 0)),
                  *[_const_spec(w.shape) for w in win],
                  _const_spec(wconv.shape), _const_spec(wco.shape), _const_spec(wao.shape),
                  pl.BlockSpec(memory_space=pltpu.SMEM),
                  _const_spec(wmo.shape)],
        out_specs=[pl.BlockSpec((1, t, d), lambda i, j: (i, j, 0)),
                   pl.BlockSpec((1, CONV_W - 1, D_CONV), lambda i, j: (i, 0, 0)),
                   pl.BlockSpec((1, WINDOW, KV_DIM), lambda i, j: (i, 0, 0)),
                   pl.BlockSpec((1, WINDOW, KV_DIM), lambda i, j: (i, 0, 0))],
        out_shape=[jax.ShapeDtypeStruct((b, seq, d), F32),
                   jax.ShapeDtypeStruct((b, CONV_W - 1, D_CONV), F32),
                   jax.ShapeDtypeStruct((b, WINDOW, KV_DIM), F32),
                   jax.ShapeDtypeStruct((b, WINDOW, KV_DIM), F32)],
        scratch_shapes=[pltpu.VMEM((SUBLANES + t, D_CONV), F32),
                        pltpu.VMEM((WINDOW + t, KV_DIM), BF16),
                        pltpu.VMEM((KV_DIM, WINDOW + t), BF16),
                        pltpu.VMEM((t, Q_DIM), F32)],
        compiler_params=pltpu.CompilerParams(dimension_semantics=("arbitrary", "arbitrary"),
                                             vmem_limit_bytes=VMEM_LIMIT),
        name="mixer_prompt",
    )(x, mod, cos, sin, *win, wconv, wco, wao, sinks, wmo)


def _mixer_sample_kernel(x_ref, mod_ref, cos_ref, sin_ref, ck_ref, cv_ref, sconv_ref, wall_ref, wq_ref,
                         wconv_ref, wco_ref, wao_ref, sinks_ref, wmo_ref, x1_ref, conv_ref, k_ref, v_ref, ubuf, obuf):
    win_ref = (wall_ref, wq_ref)
    bb, t, d = x_ref.shape
    x3 = x_ref[...]
    mod = mod_ref[...]
    sh1, sc1, g1 = mod[:, :, 0:d], mod[:, :, d:2 * d], mod[:, :, 2 * d:3 * d]
    x = x3.reshape(bb * t, d)
    hb = (_rms(x3) * (1.0 + sc1) + sh1).astype(BF16).reshape(bb * t, d)

    u = _in_proj(hb, win_ref, OFF_GC, OFF_XC) * _in_proj(hb, win_ref, OFF_XC, OFF_Q)
    u3 = u.reshape(bb, t, D_CONV)
    ubuf[:, SUBLANES - 2:SUBLANES, :] = sconv_ref[...]
    ubuf[:, SUBLANES:SUBLANES + t, :] = u3
    wc = wconv_ref[...]
    conv = (wc[0:1] * ubuf[:, SUBLANES - 2:SUBLANES - 2 + t, :] + wc[1:2] * ubuf[:, SUBLANES - 1:SUBLANES - 1 + t, :]
            + wc[2:3] * u3).reshape(bb * t, D_CONV)
    conv_ref[...] = u3[:, t - (CONV_W - 1):t, :]

    cos = jnp.concatenate([cos_ref[...]] * bb, axis=0)
    sin = jnp.concatenate([sin_ref[...]] * bb, axis=0)
    q = (_rope(_in_proj(hb, win_ref, OFF_Q, OFF_K), cos, sin) * ATTN_SCALE).astype(BF16)
    k = _rope(_in_proj(hb, win_ref, OFF_K, OFF_V), cos, sin)
    v = _in_proj(hb, win_ref, OFF_V, OFF_GCONV)
    per = ATT_Q // t
    nkeys = per * (WINDOW + t)
    rq = GQA_GROUP * ATT_Q
    key_stream = lax.broadcasted_iota(I32, (nkeys, rq), 0) // (WINDOW + t)
    query_stream = (lax.broadcasted_iota(I32, (nkeys, rq), 1) % ATT_Q) // t
    mask = key_stream == query_stream
    def block(blk):
        def load():
            k_parts, v_parts = [], []
            for b in range(blk * per, (blk + 1) * per):
                kb, vb = k[b * t:(b + 1) * t], v[b * t:(b + 1) * t]
                ck, cv = ck_ref[b], cv_ref[b]
                k_ref[b] = jnp.concatenate([ck[t:WINDOW], kb], axis=0)
                v_ref[b] = jnp.concatenate([cv[t:WINDOW], vb], axis=0)
                k_parts += [ck, kb]
                v_parts += [cv, vb]
            k_all = jnp.concatenate(k_parts, axis=0).astype(BF16)
            vt_all = jnp.concatenate(v_parts, axis=0).T.astype(BF16)
            k_of_pair = lambda pair: k_all[:, pair * LANES:(pair + 1) * LANES]
            vt_of_pair = lambda pair: vt_all[pair * LANES:(pair + 1) * LANES, :]
            return q[blk * ATT_Q:(blk + 1) * ATT_Q], k_of_pair, vt_of_pair, mask, blk * ATT_Q
        return load

    steps, side = _attention_free_steps(hb, conv, win_ref, wco_ref)
    _attention([block(blk) for blk in range(bb // per)], sinks_ref, obuf, steps)

    g1f = jnp.broadcast_to(g1, (bb, t, d)).reshape(bb * t, d)
    x1_ref[...] = _mix_out(x, g1f, side, obuf[...], wao_ref, wmo_ref).reshape(bb, t, d)


def _mixer_sample(x, mod, cos, sin, ck, cv, sconv, win, wconv, wco, wao, sinks, wmo):
    b, t, d = x.shape
    bb = SAMPLE_BB
    blk = lambda *s: pl.BlockSpec((bb,) + s, lambda i: (i, 0, 0))
    return pl.pallas_call(
        _mixer_sample_kernel,
        grid=(b // bb,),
        in_specs=[blk(t, d), blk(1, 6 * d),
                  pl.BlockSpec((t, LANES), lambda i: (0, 0)), pl.BlockSpec((t, LANES), lambda i: (0, 0)),
                  blk(WINDOW, KV_DIM), blk(WINDOW, KV_DIM), blk(CONV_W - 1, D_CONV),
                  *[_const_spec(w.shape) for w in win],
                  _const_spec(wconv.shape), _const_spec(wco.shape), _const_spec(wao.shape),
                  pl.BlockSpec(memory_space=pltpu.SMEM),
                  _const_spec(wmo.shape)],
        out_specs=[blk(t, d), blk(CONV_W - 1, D_CONV), blk(WINDOW, KV_DIM), blk(WINDOW, KV_DIM)],
        out_shape=[jax.ShapeDtypeStruct((b, t, d), F32),
                   jax.ShapeDtypeStruct((b, CONV_W - 1, D_CONV), F32),
                   jax.ShapeDtypeStruct((b, WINDOW, KV_DIM), F32),
                   jax.ShapeDtypeStruct((b, WINDOW, KV_DIM), F32)],
        scratch_shapes=[pltpu.VMEM((bb, SUBLANES + t, D_CONV), F32),
                        pltpu.VMEM((bb * t, Q_DIM), F32)],
        compiler_params=pltpu.CompilerParams(dimension_semantics=("arbitrary",), vmem_limit_bytes=VMEM_LIMIT),
        name="mixer_sample",
    )(x, mod, cos, sin, ck, cv, sconv, *win, wconv, wco, wao, sinks, wmo)


def _pre_kernel(*refs, prompt_tiles, has_sample):
    if has_sample:
        xp_ref, mp_ref, xs_ref, ms_ref, *refs = refs
    else:
        xp_ref, mp_ref, *refs = refs
    wsg_ref, wsu_ref, wsd_ref, wrh_ref, wrl_ref, rb_ref, h2_ref, base_ref, cw_ref, rank_ref, cnt_ref, carry = refs
    nc, c, d = xp_ref.shape
    t = nc * c
    x3, mod = xp_ref[...], mp_ref[...]
    if has_sample:
        is_prompt = pl.program_id(0) < prompt_tiles
        x3 = jnp.where(is_prompt, x3, xs_ref[...])
        mod = jnp.where(is_prompt, mod, ms_ref[...])
    sh2, sc2, g2 = mod[:, :, 0:d], mod[:, :, d:2 * d], mod[:, :, 2 * d:3 * d]
    h3 = _rms(x3) * (1.0 + sc2) + sh2
    h2 = h3.reshape(t, d)
    hb = h2.astype(BF16)
    h2_ref[...] = _pack_bf16_pairs(h2)
    shared = _dot((_silu(_dot(hb, wsg_ref[...])) * _dot(hb, wsu_ref[...])).astype(BF16), wsd_ref[...])
    base_ref[...] = x3 + g2 * shared.reshape(nc, c, d)

    h_lo = (h2 - hb.astype(F32)).astype(BF16)
    nt = lambda a, b: lax.dot_general(a, b, (((1,), (1,)), ((), ())), preferred_element_type=F32)
    logits = nt(wrh_ref[...], hb) + (nt(wrh_ref[...], h_lo) + nt(wrl_ref[...], hb))
    scores = _sigmoid(logits)
    biased = scores + rb_ref[...]
    g3 = biased.reshape(N_EXPERT_GROUPS, GROUP_SIZE, t)
    member = lax.broadcasted_iota(I32, g3.shape, 1)
    m1 = jnp.max(g3, axis=1, keepdims=True)
    first = jnp.min(jnp.where(g3 == m1, member, GROUP_SIZE), axis=1, keepdims=True)
    m2 = jnp.max(jnp.where(member == first, -jnp.inf, g3), axis=1, keepdims=True)
    gs = m1 + m2
    gidx = lax.broadcasted_iota(I32, gs.shape, 0)
    grank = jnp.zeros(gs.shape, I32)
    for o in range(N_EXPERT_GROUPS):
        other = gs[o:o + 1]
        grank += ((other > gs) | ((other == gs) & (o < gidx))).astype(I32)
    group_ok = grank < TOPK_GROUPS
    slot = jnp.zeros((1, 1, t), I32)
    takes = []
    for gi in range(N_EXPERT_GROUPS):
        ok = group_ok[gi:gi + 1]
        takes.append([ok & (slot == s) for s in range(TOPK_GROUPS)])
        slot = slot + ok.astype(I32)
    packed = []
    for s in range(TOPK_GROUPS):
        vals = jnp.zeros((GROUP_SIZE, t), F32)
        for gi in range(N_EXPERT_GROUPS):
            vals = jnp.where(takes[gi][s][0], g3[gi], vals)
        packed.append(vals)
    cand = jnp.concatenate(packed, axis=0)
    cidx = lax.broadcasted_iota(I32, cand.shape, 0)
    crank = jnp.zeros(cand.shape, I32)
    for o in range(TOPK_GROUPS * GROUP_SIZE):
        other = cand[o:o + 1]
        crank += ((other > cand) | ((other == cand) & (o < cidx))).astype(I32)
    chosen = crank < TOP_K
    sel_groups = []
    for gi in range(N_EXPERT_GROUPS):
        hit = jnp.zeros((GROUP_SIZE, t), jnp.bool_)
        for s in range(TOPK_GROUPS):
            hit = hit | (takes[gi][s][0] & chosen[s * GROUP_SIZE:(s + 1) * GROUP_SIZE])
        sel_groups.append(hit)
    sel = jnp.concatenate(sel_groups, axis=0)
    ssum = jnp.sum(jnp.where(sel, scores, 0.0), axis=0, keepdims=True)
    cw_ref[...] = jnp.where(sel, scores / ssum * ROUTED_SCALE, -1.0)

    @pl.when(pl.program_id(0) == 0)
    def _():
        carry[...] = jnp.zeros(carry.shape, F32)

    picked = sel.astype(BF16)
    earlier = (lax.broadcasted_iota(I32, (t, t), 0) < lax.broadcasted_iota(I32, (t, t), 1)).astype(BF16)
    rank_ref[...] = (carry[...] + _dot(picked, earlier)).astype(I32)
    carry[...] = carry[...] + jnp.sum(picked.astype(F32), axis=1, keepdims=True)
    cnt_ref[...] = carry[...].astype(I32)


def _pre(x1_p, p_chunk0, ncp, mod_p, x1_s, ncs, mod_s, wsg, wsu, wsd, wr_hi, wr_lo, rb):
    ncp_all, c, d = x1_p.shape
    nc = PRE_TILE // c
    nchunks = ncp + ncs
    n = nchunks * c
    pt, p0 = ncp // nc, p_chunk0 // nc
    tiles_per_stream = ncp_all // mod_p.shape[0] // nc
    blk3 = pl.BlockSpec((nc, c, d), lambda i: (i, 0, 0))
    p_tile = lambda i: p0 + jnp.minimum(i, pt - 1)
    s_tile = lambda i: jnp.maximum(i - pt, 0)
    s_args, s_specs = [], []
    if ncs:
        s_args = [x1_s, mod_s]
        s_specs = [pl.BlockSpec((nc, c, d), lambda i: (s_tile(i), 0, 0)),
                   pl.BlockSpec((nc, 1, 3 * d), lambda i: (s_tile(i), 0, 0))]
    return pl.pallas_call(
        functools.partial(_pre_kernel, prompt_tiles=pt, has_sample=bool(ncs)),
        grid=(nchunks // nc,),
        in_specs=[pl.BlockSpec((nc, c, d), lambda i: (p_tile(i), 0, 0)),
                  pl.BlockSpec((1, 1, 3 * d), lambda i: (p_tile(i) // tiles_per_stream, 0, 0))] + s_specs + [
                  _const_spec(wsg.shape), _const_spec(wsu.shape), _const_spec(wsd.shape),
                  _const_spec(wr_hi.shape), _const_spec(wr_lo.shape), _const_spec(rb.shape)],
        out_specs=[pl.BlockSpec((nc * c, d // 2), lambda i: (i, 0)), blk3,
                   pl.BlockSpec((N_EXPERTS, nc * c), lambda i: (0, i)),
                   pl.BlockSpec((N_EXPERTS, nc * c), lambda i: (0, i)),
                   pl.BlockSpec((N_EXPERTS, 1), lambda i: (0, 0))],
        out_shape=[jax.ShapeDtypeStruct((n, d // 2), I32),
                   jax.ShapeDtypeStruct((nchunks, c, d), F32),
                   jax.ShapeDtypeStruct((N_EXPERTS, n), F32),
                   jax.ShapeDtypeStruct((N_EXPERTS, n), I32),
                   jax.ShapeDtypeStruct((N_EXPERTS, 1), I32)],
        scratch_shapes=[pltpu.VMEM((N_EXPERTS, 1), F32)],
        compiler_params=pltpu.CompilerParams(dimension_semantics=("arbitrary",), vmem_limit_bytes=VMEM_LIMIT),
        name="pre_ffn",
    )(x1_p, mod_p, *s_args, wsg, wsu, wsd, wr_hi, wr_lo, rb)


def _slot_kernel(cw_ref, rank_ref, start_ref, pos_ref, w_ref, pos_tok_ref):
    cw = cw_ref[...]
    e, t = cw.shape
    sel = cw >= 0.0
    r = lax.broadcasted_iota(I32, (e, e), 0)
    c = lax.broadcasted_iota(I32, (e, e), 1)
    lower = (c < r).astype(BF16)
    kidx = _dot(lower, sel.astype(BF16))
    posf = start_ref[...].astype(F32) + rank_ref[...].astype(F32)
    pos_rows, w_rows = [], []
    for k in range(TOP_K):
        m = sel & (kidx == float(k))
        pos_rows.append(jnp.sum(jnp.where(m, posf, 0.0), axis=0, keepdims=True))
        w_rows.append(jnp.sum(jnp.where(m, cw, 0.0), axis=0, keepdims=True))
    pos_ref[...] = jnp.concatenate(pos_rows, axis=0).astype(I32)
    w_ref[...] = jnp.concatenate([jnp.broadcast_to(w, (SC_LANES, t)) for w in w_rows], axis=0).T
    pos_pad = jnp.concatenate(pos_rows + [jnp.zeros((LANES - TOP_K, t), F32)], axis=0)
    pos_tok_ref[...] = pos_pad.T[:, :TOP_K].astype(I32)


def _slots(cw, rank, seg_start):
    e, n = cw.shape
    t = RANK_TILE
    return pl.pallas_call(
        _slot_kernel,
        grid=(n // t,),
        in_specs=[pl.BlockSpec((e, t), lambda i: (0, i)), pl.BlockSpec((e, t), lambda i: (0, i)),
                  pl.BlockSpec((e, 1), lambda i: (0, 0))],
        out_specs=[pl.BlockSpec((TOP_K, t), lambda i: (0, i)), pl.BlockSpec((t, TOP_K * SC_LANES), lambda i: (i, 0)),
                   pl.BlockSpec((t, TOP_K), lambda i: (i, 0))],
        out_shape=[jax.ShapeDtypeStruct((TOP_K, n), I32), jax.ShapeDtypeStruct((n, TOP_K * SC_LANES), F32),
                   jax.ShapeDtypeStruct((n, TOP_K), I32)],
        compiler_params=pltpu.CompilerParams(dimension_semantics=("arbitrary",)),
        name="expert_slots",
    )(cw, rank, seg_start)


def _sc_mesh():
    return plsc.VectorSubcoreMesh(core_axis_name="c", subcore_axis_name="s")


def _sc_worker_id():
    return lax.axis_index("s") * (SC_WORKERS // 16) + lax.axis_index("c")


def _sc_dispatch(rows, pos, n_rows):
    n, d = rows.shape
    per_w = n // SC_WORKERS
    w = SC_WINDOW
    n_chunks = per_w // w

    @functools.partial(
        pl.kernel, mesh=_sc_mesh(),
        out_type=jax.ShapeDtypeStruct((n_rows, d), rows.dtype),
        scratch_types=[pltpu.VMEM((2, TOP_K, w), I32), pltpu.VMEM((2, w, d), rows.dtype),
                       pltpu.SemaphoreType.DMA((2,)), pltpu.SemaphoreType.DMA((2,)), pltpu.SemaphoreType.DMA((2,))],
        name="sc_dispatch")
    def k(rows_hbm, pos_hbm, o_hbm, idx_v, rows_v, row_sem, idx_sem, out_sem):
        wid = _sc_worker_id()
        base = wid * per_w

        def loads(c, slot):
            off = pl.multiple_of(base + c * w, SUBLANES)
            return (pltpu.make_async_copy(rows_hbm.at[pl.ds(off, w)], rows_v.at[slot], row_sem.at[slot]),
                    pltpu.make_async_copy(pos_hbm.at[wid * n_chunks + c], idx_v.at[slot], idx_sem.at[slot]))

        def scatters(slot):
            return [pltpu.make_async_copy(rows_v.at[slot], o_hbm.at[idx_v.at[slot, kk]], out_sem.at[slot])
                    for kk in range(TOP_K)]

        for cp in loads(0, 0):
            cp.start()
        for c in range(n_chunks):
            slot = c % 2
            for cp in loads(c, slot):
                cp.wait()
            for cp in scatters(slot):
                cp.start()
            if c >= 1:
                for cp in scatters(1 - slot):
                    cp.wait()
            if c + 1 < n_chunks:
                for cp in loads(c + 1, 1 - slot):
                    cp.start()
        for cp in scatters((n_chunks - 1) % 2):
            cp.wait()

    pos_chunks = pos.reshape(TOP_K, n // w, w).transpose(1, 0, 2)
    return k(rows, pos_chunks)


def _sc_collect_sum(rows, pos_tok, w_lanes):
    words = rows.shape[1]
    n = w_lanes.shape[0]
    lanes = SC_LANES
    per_w = n // SC_WORKERS
    tw = SC_SUM_TOKENS
    n_pairs = per_w // (2 * tw)
    col_blocks = words // lanes // SC_SUM_VREGS

    @functools.partial(
        pl.kernel, mesh=_sc_mesh(),
        out_type=jax.ShapeDtypeStruct((n, 2 * words), F32),
        scratch_types=[pltpu.VMEM((per_w * TOP_K,), I32), pltpu.VMEM((2, tw * TOP_K, words), I32),
                       pltpu.VMEM((2, tw, TOP_K * lanes), F32), pltpu.VMEM((2, tw, 2 * words), F32),
                       pltpu.SemaphoreType.DMA((2,)), pltpu.SemaphoreType.DMA((2,)), pltpu.SemaphoreType.DMA((2,))],
        compiler_params=pltpu.CompilerParams(needs_layout_passes=False),
        name="sc_collect_sum")
    def k(rows_hbm, pos_hbm, w_hbm, o_hbm, idx_v, rows_v, w_v, out_v, in_sem, w_sem, out_sem):
        base = pl.multiple_of(_sc_worker_id() * per_w, SUBLANES)
        pltpu.sync_copy(pos_hbm.at[pl.ds(pl.multiple_of(base * TOP_K, SUBLANES), per_w * TOP_K)], idx_v)

        def loads(c, slot):
            idx = idx_v.at[pl.ds(pl.multiple_of(c * tw * TOP_K, SUBLANES), tw * TOP_K)]
            tok0 = pl.multiple_of(base + c * tw, SUBLANES)
            return (pltpu.make_async_copy(rows_hbm.at[idx], rows_v.at[slot], in_sem.at[slot]),
                    pltpu.make_async_copy(w_hbm.at[pl.ds(tok0, tw)], w_v.at[slot], w_sem.at[slot]))

        def write(c, slot):
            tok0 = pl.multiple_of(base + c * tw, SUBLANES)
            return pltpu.make_async_copy(out_v.at[slot], o_hbm.at[pl.ds(tok0, tw)], out_sem.at[slot])

        high_half = jnp.full((lanes,), -65536, I32)
        sixteen = jnp.full((lanes,), 16, I32)

        def reduce_window(slot):
            rv, wv, ov = rows_v.at[slot], w_v.at[slot], out_v.at[slot]

            @pl.loop(0, tw)
            def _(t):
                for cb in range(col_blocks):
                    acc_lo, acc_hi = [None] * SC_SUM_VREGS, [None] * SC_SUM_VREGS
                    for kk in range(TOP_K):
                        wk = wv[t, pl.ds(kk * lanes, lanes)]
                        for c in range(SC_SUM_VREGS):
                            wd = rv[t * TOP_K + kk, pl.ds((cb * SC_SUM_VREGS + c) * lanes, lanes)]
                            lo = wk * plsc.bitcast(lax.shift_left(wd, sixteen), F32)
                            hi = wk * plsc.bitcast(wd & high_half, F32)
                            acc_lo[c] = lo if kk == 0 else acc_lo[c] + lo
                            acc_hi[c] = hi if kk == 0 else acc_hi[c] + hi
                    for c in range(SC_SUM_VREGS):
                        col = (cb * SC_SUM_VREGS + c) * lanes
                        ov[t, pl.ds(col, lanes)] = acc_lo[c]
                        ov[t, pl.ds(words + col, lanes)] = acc_hi[c]

        for cp in loads(0, 0):
            cp.start()

        @pl.loop(0, n_pairs)
        def _(p):
            c0 = 2 * p
            for cp in loads(c0 + 1, 1):
                cp.start()
            for cp in loads(c0, 0):
                cp.wait()
            reduce_window(0)
            write(c0, 0).start()
            for cp in loads(c0 + 1, 1):
                cp.wait()
            reduce_window(1)
            write(c0 + 1, 1).start()
            write(c0, 0).wait()

            @pl.when(p + 1 < n_pairs)
            def _():
                for cp in loads(c0 + 2, 0):
                    cp.start()

            write(c0 + 1, 1).wait()

    return k(rows, pos_tok, w_lanes)


def _gmm_kernel(tot_ref, ce_ref, row_ref, val_ref, ord_ref, nxt_ref, x_hbm, wg_hbm, wu_hbm, wd_hbm, y_hbm,
                wgb, wub, wdb, xbuf, xsem, ybuf, ysem, wgf, wuf, wdf, wsem):
    total = tot_ref[0]
    pieces = GMM_SUB // GMM_TAIL

    def w_copies(ex, slot):
        return [pltpu.make_async_copy(src.at[ex], dst.at[slot], wsem.at[slot, i])
                for i, (src, dst) in enumerate(((wg_hbm, wgf), (wu_hbm, wuf), (wd_hbm, wdf)))]

    def x_copy(g):
        slot = g % GMM_X_SLOTS
        rows = pl.ds(pl.multiple_of(row_ref[g], GMM_TAIL), GMM_SUB)
        return pltpu.make_async_copy(x_hbm.at[rows], xbuf.at[slot], xsem.at[slot])

    def y_piece(g, p):
        slot = g % 2
        rows = pl.ds(pl.multiple_of(row_ref[g] + p * GMM_TAIL, GMM_TAIL), GMM_TAIL)
        return pltpu.make_async_copy(ybuf.at[slot, pl.ds(p * GMM_TAIL, GMM_TAIL)], y_hbm.at[rows], ysem.at[slot])

    def for_y_pieces(g, action):
        for p in range(pieces):
            @pl.when(p * GMM_TAIL < val_ref[g])
            def _():
                action(y_piece(g, p))

    for ahead in range(GMM_X_SLOTS - 1):
        @pl.when(ahead < total)
        def _():
            x_copy(ahead).start()

    def chunk(g, carry):
        e = ce_ref[g]
        prev = ce_ref[jnp.maximum(g - 1, 0)]

        @pl.when((g == 0) | (e != prev))
        def _():
            slot = ord_ref[e] % 2

            @pl.when(g == 0)
            def _():
                for cp in w_copies(e, slot):
                    cp.start()

            for cp in w_copies(e, slot):
                cp.wait()
            wgb[...] = wgf[slot].astype(BF16)
            wub[...] = wuf[slot].astype(BF16)
            wdb[...] = wdf[slot].astype(BF16)
            nxt = nxt_ref[e]

            @pl.when(nxt >= 0)
            def _():
                for cp in w_copies(nxt, 1 - slot):
                    cp.start()

        x_copy(g).wait()

        @pl.when(g + GMM_X_SLOTS - 1 < total)
        def _():
            x_copy(g + GMM_X_SLOTS - 1).start()

        @pl.when(g >= 2)
        def _():
            for_y_pieces(g - 2, lambda cp: cp.wait())

        x_ref = xbuf.at[g % GMM_X_SLOTS]
        y_ref = ybuf.at[g % 2]

        def expert_rows(r0, n):
            rows = pl.ds(r0, n)
            lo, hi = _unpack_bf16_pairs(x_ref[rows, :])
            xb = jnp.concatenate([lo.astype(BF16), hi.astype(BF16)], axis=1)
            mid = (_silu(_dot(xb, wgb[...])) * _dot(xb, wub[...])).astype(BF16)
            y_ref[rows, :] = _pack_bf16_pairs(_dot(mid, wdb[...]))

        n_real = val_ref[g]

        @pl.when(n_real == GMM_SUB)
        def _():
            expert_rows(0, GMM_SUB)

        @pl.when((n_real < GMM_SUB) & (n_real >= GMM_MID))
        def _():
            expert_rows(0, GMM_MID)

        @pl.when(n_real < GMM_SUB)
        def _():
            done = jnp.where(n_real >= GMM_MID, GMM_MID, 0)

            @pl.loop(0, (n_real - done + GMM_TAIL - 1) // GMM_TAIL)
            def _(i):
                expert_rows(pl.multiple_of(done + i * GMM_TAIL, GMM_TAIL), GMM_TAIL)

        for_y_pieces(g, lambda cp: cp.start())
        return carry

    lax.fori_loop(0, total, chunk, 0)
    for back in (2, 1):
        @pl.when(total >= back)
        def _():
            for_y_pieces(total - back, lambda cp: cp.wait())


def _gmm(x_sorted, n_chunks, chunk_e, chunk_row, chunk_valid, e_ord, e_next, wg, wu, wd):
    r, half = x_sorted.shape
    d = 2 * half
    any_spec = pl.BlockSpec(memory_space=pl.ANY)
    return pl.pallas_call(
        _gmm_kernel,
        grid_spec=pltpu.PrefetchScalarGridSpec(
            num_scalar_prefetch=6,
            grid=(1,),
            in_specs=[any_spec, any_spec, any_spec, any_spec],
            out_specs=any_spec,
            scratch_shapes=[pltpu.VMEM((d, D_EXPERT), BF16), pltpu.VMEM((d, D_EXPERT), BF16),
                            pltpu.VMEM((D_EXPERT, d), BF16),
                            pltpu.VMEM((GMM_X_SLOTS, GMM_SUB, half), I32), pltpu.SemaphoreType.DMA((GMM_X_SLOTS,)),
                            pltpu.VMEM((2, GMM_SUB, half), I32), pltpu.SemaphoreType.DMA((2,)),
                            pltpu.VMEM((2, d, D_EXPERT), F32), pltpu.VMEM((2, d, D_EXPERT), F32),
                            pltpu.VMEM((2, D_EXPERT, d), F32), pltpu.SemaphoreType.DMA((2, 3))]),
        out_shape=jax.ShapeDtypeStruct((r, half), I32),
        compiler_params=pltpu.CompilerParams(dimension_semantics=("arbitrary",), vmem_limit_bytes=VMEM_LIMIT),
        name="expert_gmm",
    )(n_chunks, chunk_e, chunk_row, chunk_valid, e_ord, e_next, x_sorted, wg, wu, wd)


def _combine_kernel(base_ref, mod_ref, routed_ref, gain_ref, *rest):
    y_ref = rest[-1]
    d = base_ref.shape[-1]
    g2 = mod_ref[...][:, :, 2 * d:3 * d]
    out = base_ref[...] + g2 * routed_ref[...]
    y_ref[...] = _rms(out) * gain_ref[...]


def _combine(base, mod, routed, gain, first_chunk, n_chunks, out_chunks, out_first_chunk, out_buf=None):
    _, c, d = base.shape
    nc = COMB_TILE // c
    t0, o0 = first_chunk // nc, out_first_chunk // nc
    chunks_per_stream = out_chunks // mod.shape[0]
    if chunks_per_stream == 1:
        mod_spec = pl.BlockSpec((nc, 1, 3 * d), lambda i: (o0 + i, 0, 0))
    else:
        assert chunks_per_stream % nc == 0
        mod_spec = pl.BlockSpec((1, 1, 3 * d), lambda i: ((o0 + i) * nc // chunks_per_stream, 0, 0))
    blk3 = pl.BlockSpec((nc, c, d), lambda i: (t0 + i, 0, 0))
    in_specs = [blk3, mod_spec, blk3, pl.BlockSpec((1, 1, d), lambda i: (0, 0, 0))]
    args = [base, mod, routed, gain.reshape(1, 1, d)]
    aliases = {}
    if out_buf is not None:
        in_specs.append(pl.BlockSpec(memory_space=pl.ANY))
        args.append(out_buf)
        aliases = {len(args) - 1: 0}
    return pl.pallas_call(
        _combine_kernel,
        grid=(n_chunks // nc,),
        in_specs=in_specs,
        out_specs=pl.BlockSpec((nc, c, d), lambda i: (o0 + i, 0, 0)),
        out_shape=jax.ShapeDtypeStruct((out_chunks, c, d), F32),
        input_output_aliases=aliases,
        compiler_params=pltpu.CompilerParams(dimension_semantics=("arbitrary",), vmem_limit_bytes=VMEM_LIMIT),
        name="combine_norm",
    )(*args)


def _rope_tables(pos):
    half = HEAD_DIM // 2
    lane = jnp.arange(LANES, dtype=I32)
    inv_freq = ROPE_THETA ** (-(lane % half).astype(F32) / half)
    sign = jnp.where((lane % HEAD_DIM) < half, -1.0, 1.0).astype(F32)
    ang = pos.astype(F32)[:, None] * inv_freq[None, :]
    return jnp.cos(ang), jnp.sin(ang) * sign[None, :]


def _routed_ffn(h2, cw, rank, counts, w_gate, w_up, w_down):
    n, half = h2.shape
    counts = counts[:, 0]
    padded = (counts + GMM_TAIL - 1) // GMM_TAIL * GMM_TAIL
    seg_start = (jnp.cumsum(padded) - padded).astype(I32)
    n_rows = n * TOP_K + N_EXPERTS * GMM_TAIL + GMM_SUB
    e_chunks = (counts + GMM_SUB - 1) // GMM_SUB
    chunk_end = jnp.cumsum(e_chunks)
    max_chunks = n * TOP_K // GMM_SUB + N_EXPERTS
    g = jnp.arange(max_chunks, dtype=I32)
    chunk_e = jnp.minimum(jnp.sum((chunk_end[None, :] <= g[:, None]).astype(I32), axis=1), N_EXPERTS - 1)
    eids = jnp.arange(N_EXPERTS, dtype=I32)
    own = chunk_e[:, None] == eids[None, :]
    pick = lambda table: jnp.sum(jnp.where(own, table[None, :], 0), axis=1)
    in_expert = (g - pick(chunk_end - e_chunks)) * GMM_SUB
    chunk_row = (pick(seg_start) + in_expert).astype(I32)
    chunk_valid = jnp.clip(pick(counts) - in_expert, 0, GMM_SUB).astype(I32)
    n_chunks = chunk_end[-1:].astype(I32)
    has_rows = counts > 0
    e_ord = (jnp.cumsum(has_rows.astype(I32)) - has_rows.astype(I32)).astype(I32)
    later = has_rows[None, :] & (eids[None, :] > eids[:, None])
    e_next = jnp.min(jnp.where(later, eids[None, :], N_EXPERTS), axis=1)
    e_next = jnp.where(e_next == N_EXPERTS, -1, e_next).astype(I32)
    pos, w_lanes, pos_tok = _slots(cw, rank, seg_start[:, None])
    x_sorted = _sc_dispatch(h2, pos, n_rows)
    y_sorted = _gmm(x_sorted, n_chunks, chunk_e, chunk_row, chunk_valid, e_ord, e_next, w_gate, w_up, w_down)
    return _sc_collect_sum(y_sorted, pos_tok.reshape(n * TOP_K), w_lanes)


def kernel(x_prompt, x_sample, cache_k, cache_v, state_conv, c_prompt, c_sample, w_ada, b_ada, w_in, w_conv,
           w_conv_out, w_attn_o, attn_sinks, w_mix_out, w_router, router_bias, w_exp_gate, w_exp_up, w_exp_down,
           w_sh_gate, w_sh_up, w_sh_down, final_gain):
    assert w_ada.shape[0] == 1, "one layer"
    bp, seq, d = x_prompt.shape
    bs, ts, _ = x_sample.shape
    assert ts == CHUNK and seq % MIX_TILE == 0 and bs % SAMPLE_BB == 0

    c_all = jnp.concatenate([c_prompt, c_sample], axis=0)
    pad = (-c_all.shape[0]) % SUBLANES
    mod = _ada(jnp.pad(c_all, ((0, pad), (0, 0))), w_ada[0], b_ada[0])[:bp + bs]
    mod_p, mod_s = mod[:bp, None, :], mod[bp:, None, :]

    head_axes = (N_KV_HEADS // 2, 2, GQA_GROUP, HEAD_DIM)
    w_in_l = w_in[0]
    w_q = w_in_l[:, OFF_Q:OFF_K].reshape((d,) + head_axes).transpose(0, 1, 3, 2, 4).reshape(d, Q_DIM)
    w_o = w_attn_o[0].reshape(head_axes + (d,)).transpose(0, 2, 1, 3, 4).reshape(Q_DIM, d)
    win = (w_in_l.astype(BF16), w_q.astype(BF16))
    wco, wao, wmo = (w.astype(BF16) for w in (w_conv_out[0], w_o, w_mix_out[0]))
    cos_p, sin_p = _rope_tables(jnp.arange(seq, dtype=I32))
    cos_s, sin_s = _rope_tables(PAST_LEN + jnp.arange(ts, dtype=I32))

    x1_p, conv_p, k_p, v_p = _mixer_prompt(x_prompt, mod_p, cos_p, sin_p, win, w_conv[0], wco, wao, attn_sinks[0], wmo)
    x1_s, conv_s, k_s, v_s = _mixer_sample(
        x_sample, mod_s, cos_s, sin_s, cache_k[0].reshape(bs, WINDOW, KV_DIM), cache_v[0].reshape(bs, WINDOW, KV_DIM),
        state_conv[0], win, w_conv[0], wco, wao, attn_sinks[0], wmo)

    n_p, n_s = bp * seq, bs * ts
    n = n_p + n_s
    mod2_p, mod2_s = mod[:bp, None, 3 * d:], mod[bp:, None, 3 * d:]
    x1_pc = x1_p.reshape(n_p // CHUNK, CHUNK, d)
    wsg, wsu, wsd = (w[0].astype(BF16) for w in (w_sh_gate, w_sh_up, w_sh_down))
    wr_t, rb = w_router[0].T, router_bias[0][:, None]
    wr_hi = wr_t.astype(BF16)
    wr_lo = (wr_t - wr_hi.astype(F32)).astype(BF16)

    ncp, ncs = n_p // CHUNK, n_s // CHUNK
    half = (ncp + ncs) * FFN_SET_A_SHARE[0] // FFN_SET_A_SHARE[1]
    tile_chunks = max(PRE_TILE, COMB_TILE) // CHUNK
    assert half <= ncp and half % tile_chunks == 0 and (ncp - half) % tile_chunks == 0 and ncs % tile_chunks == 0
    for set_tokens in (half * CHUNK, n - half * CHUNK):
        assert set_tokens % (SC_WORKERS * SC_WINDOW) == 0 and set_tokens % RANK_TILE == 0
        assert set_tokens % (SC_WORKERS * 2 * SC_SUM_TOKENS) == 0
    y_p = None
    for p0, np_c, ns_c in ((0, half, 0), (half, ncp - half, ncs)):
        h2, base, cw, rank, counts = _pre(x1_pc, p0, np_c, mod2_p, x1_s, ns_c, mod2_s, wsg, wsu, wsd, wr_hi, wr_lo, rb)
        routed = _routed_ffn(h2, cw, rank, counts, w_exp_gate[0], w_exp_up[0], w_exp_down[0]).reshape(base.shape)
        y_p = _combine(base, mod2_p, routed, final_gain, 0, np_c, ncp, p0, out_buf=y_p)
        if ns_c:
            y_s = _combine(base, mod2_s, routed, final_gain, np_c, ns_c, ncs, 0)

    kv = lambda a: a.reshape(1, a.shape[0], WINDOW, N_KV_HEADS, HEAD_DIM)
    return (y_p.reshape(bp, seq, d), y_s, conv_p[None], kv(k_p), kv(v_p), conv_s[None], kv(k_s), kv(v_s))
```

```python
import functools

import jax
import jax.numpy as jnp
from jax import lax
from jax.experimental import pallas as pl
from jax.experimental.pallas import tpu as pltpu
from jax.experimental.pallas import tpu_sc as plsc

F32 = jnp.float32
BF16 = jnp.bfloat16
I32 = jnp.int32

D_MODEL = 1024
CHUNK = 64
D_CONV = 1024
CONV_W = 3
N_HEADS = 16
N_KV_HEADS = 4
HEAD_DIM = 64
GQA_GROUP = N_HEADS // N_KV_HEADS
WINDOW = 128
ROPE_THETA = 10000.0
ATTN_SCALE = HEAD_DIM ** -0.5
N_EXPERTS = 64
TOP_K = 8
N_EXPERT_GROUPS = 8
GROUP_SIZE = N_EXPERTS // N_EXPERT_GROUPS
TOPK_GROUPS = 4
D_EXPERT = 256
D_SHARED = 256
ROUTED_SCALE = 2.5
EPS = 1e-6
PAST_LEN = 4096
Q_DIM = N_HEADS * HEAD_DIM
KV_DIM = N_KV_HEADS * HEAD_DIM
OFF_GB, OFF_GC, OFF_XC, OFF_Q, OFF_K, OFF_V, OFF_GCONV, OFF_GATTN, D_IN = (
    0, 1024, 2048, 3072, 4096, 4352, 4608, 5632, 6656)

LANES = 128
SUBLANES = 8
VMEM_LIMIT = 56 * 1024 * 1024

MIX_TILE = 512
ATT_Q = 128
MIX_SIDE_COLS = 256
SAMPLE_BB = 8
PRE_TILE = 512
RANK_TILE = 512
GMM_SUB = 512
GMM_MID = 256
GMM_TAIL = 128
GMM_X_SLOTS = 4
GMM_W_SLOTS = 3
COMB_TILE = 512
FFN_SET_A_SHARE = (2, 3)
SC_WORKERS = 32
SC_WINDOW = 96
SC_LANES = 16
SC_SUM_TOKENS = 8
SC_SUM_VREGS = 16


def _const_spec(shape):
    nd = len(shape)
    return pl.BlockSpec(shape, lambda *_: (0,) * nd, pipeline_mode=pl.Buffered(1))


def _rms(x):
    return x * lax.rsqrt(jnp.mean(x * x, axis=-1, keepdims=True) + EPS)


def _sigmoid(x):
    return 1.0 / (1.0 + jnp.exp(-x))


def _silu(x):
    return x * _sigmoid(x)


def _dot(a, b):
    return jnp.dot(a, b, preferred_element_type=F32)


def _pack_bf16_pairs(x):
    half = x.shape[-1] // 2
    lo = lax.bitcast_convert_type(x[..., :half].astype(BF16).astype(F32), I32)
    hi = lax.bitcast_convert_type(x[..., half:].astype(BF16).astype(F32), I32)
    return lax.shift_right_logical(lo, 16) | hi


def _unpack_bf16_pairs(words):
    lo = lax.bitcast_convert_type(lax.shift_left(words, 16), F32)
    hi = lax.bitcast_convert_type(words & jnp.int32(-65536), F32)
    return lo, hi


def _ada_kernel(c_ref, w_ref, b_ref, o_ref):
    s = _silu(c_ref[...]).astype(BF16)
    o_ref[...] = _dot(s, w_ref[...].astype(BF16)) + b_ref[...]


def _ada(c_all, w_ada, b_ada):
    rows = c_all.shape[0]
    n_out = w_ada.shape[1]
    bn = 768
    return pl.pallas_call(
        _ada_kernel,
        grid=(n_out // bn,),
        in_specs=[pl.BlockSpec((rows, D_MODEL), lambda i: (0, 0)),
                  pl.BlockSpec((D_MODEL, bn), lambda i: (0, i)),
                  pl.BlockSpec((1, bn), lambda i: (0, i))],
        out_specs=pl.BlockSpec((rows, bn), lambda i: (0, i)),
        out_shape=jax.ShapeDtypeStruct((rows, n_out), F32),
        name="ada_mod",
    )(c_all, w_ada, b_ada.reshape(1, n_out))


def _rope(x, cos, sin_signed):
    lane = lax.broadcasted_iota(I32, (x.shape[0], LANES), 1)
    first_half = (lane % HEAD_DIM) < (HEAD_DIM // 2)
    outs = []
    for g in range(x.shape[1] // LANES):
        xg = x[:, g * LANES:(g + 1) * LANES]
        up = pltpu.roll(xg, LANES - HEAD_DIM // 2, axis=1)
        down = pltpu.roll(xg, HEAD_DIM // 2, axis=1)
        partner = jnp.where(first_half, up, down)
        outs.append(xg * cos + partner * sin_signed)
    return jnp.concatenate(outs, axis=1)


def _attention(blocks, sinks_ref, obuf, between):
    rq = GQA_GROUP * ATT_Q
    low = lax.broadcasted_iota(I32, (ATT_Q, LANES), 1) < HEAD_DIM
    head_of_lane = lax.broadcasted_iota(I32, (1, rq), 1) // ATT_Q
    units = [(b, pair, par) for b in range(len(blocks)) for pair in range(N_KV_HEADS // 2) for par in range(2)]
    loaded = {}

    def scores(u):
        b, pair, par = units[u]
        if b not in loaded:
            loaded.clear()
            loaded[b] = blocks[b]()
        q_blk, k_of_pair, _, mask, _ = loaded[b]
        keep = low if par == 0 else jnp.logical_not(low)
        cols = [q_blk[:, (GQA_GROUP * pair + i) * LANES:(GQA_GROUP * pair + i + 1) * LANES] for i in range(GQA_GROUP)]
        qg = jnp.concatenate([jnp.where(keep, c, jnp.zeros_like(c)) for c in cols], axis=0)
        st = lax.dot_general(k_of_pair(pair), qg, (((1,), (1,)), ((), ())), preferred_element_type=F32)
        vt = loaded[b][2](pair)[par * HEAD_DIM:(par + 1) * HEAD_DIM, :]
        return jnp.where(mask, st, -jnp.inf), vt, loaded[b][4]

    outs = []
    nxt = scores(0)
    for u, (b, pair, par) in enumerate(units):
        st, vt, row0 = nxt
        if u + 1 < len(units):
            nxt = scores(u + 1)
        if between:
            between.pop(0)()
        g = 2 * pair + par
        sink = jnp.full((1, rq), sinks_ref[g * GQA_GROUP + GQA_GROUP - 1], F32)
        for i in range(GQA_GROUP - 2, -1, -1):
            sink = jnp.where(head_of_lane == i, sinks_ref[g * GQA_GROUP + i], sink)
        m = jnp.maximum(jnp.max(st, axis=0, keepdims=True), sink)
        e = jnp.exp(st - m)
        z = jnp.sum(e, axis=0, keepdims=True) + jnp.exp(sink - m)
        outs.append(_dot(vt, e.astype(BF16)) / z)
        if par == 1:
            for i in range(GQA_GROUP):
                blk = jnp.concatenate([o[:, i * ATT_Q:(i + 1) * ATT_Q] for o in outs], axis=0)
                c0 = (GQA_GROUP * pair + i) * LANES
                obuf[row0:row0 + ATT_Q, c0:c0 + LANES] = blk.T
            outs = []
    for step in between:
        step()


def _in_proj(hb, win_refs, lo, hi):
    w_all, w_q = win_refs
    if (lo, hi) == (OFF_Q, OFF_K):
        return _dot(hb, w_q[...])
    assert hi <= OFF_Q or lo >= OFF_K
    return _dot(hb, w_all[:, lo:hi])


def _attention_free_steps(hb, conv, win_ref, wco_ref):
    out, parts = {}, {}
    n_parts = D_MODEL // MIX_SIDE_COLS

    def step(name, piece, compute):
        def run():
            parts.setdefault(name, []).append(compute(piece * MIX_SIDE_COLS, (piece + 1) * MIX_SIDE_COLS))
            if piece == n_parts - 1:
                out[name] = jnp.concatenate(parts.pop(name), axis=1)
        return run

    def conv_in():
        if "conv_in" not in out:
            out["conv_in"] = (out.pop("gate_b") * conv).astype(BF16)
        return out["conv_in"]

    computes = [("gate_b", lambda lo, hi: _in_proj(hb, win_ref, OFF_GB + lo, OFF_GB + hi)),
                ("g_conv", lambda lo, hi: _in_proj(hb, win_ref, OFF_GCONV + lo, OFF_GCONV + hi)),
                ("g_attn", lambda lo, hi: _in_proj(hb, win_ref, OFF_GATTN + lo, OFF_GATTN + hi)),
                ("y_conv", lambda lo, hi: _dot(conv_in(), wco_ref[:, lo:hi]))]
    return [step(name, p, fn) for name, fn in computes for p in range(n_parts)], out


def _mix_out(x, g1, side, y_attn_in, wao_ref, wmo_ref):
    y_attn = _dot(y_attn_in.astype(BF16), wao_ref[...])
    merged = _sigmoid(side["g_conv"]) * side["y_conv"] + _sigmoid(side["g_attn"]) * y_attn
    return x + g1 * _dot(merged.astype(BF16), wmo_ref[...])


def _mixer_prompt_kernel(x_ref, mod_ref, cos_ref, sin_ref, wall_ref, wq_ref, wconv_ref, wco_ref, wao_ref,
                         sinks_ref, wmo_ref, x1_ref, conv_ref, k_ref, v_ref, ubuf, kbuf, vtbuf, obuf):
    win_ref = (wall_ref, wq_ref)
    j = pl.program_id(1)
    t = x_ref.shape[1]

    @pl.when(j == 0)
    def _():
        ubuf[0:SUBLANES, :] = jnp.zeros((SUBLANES, D_CONV), F32)
        kbuf[0:WINDOW, :] = jnp.zeros((WINDOW, KV_DIM), BF16)
        vtbuf[:, 0:WINDOW] = jnp.zeros((KV_DIM, WINDOW), BF16)

    x = x_ref[0]
    mod = mod_ref[0]
    sh1, sc1, g1 = mod[:, 0:D_MODEL], mod[:, D_MODEL:2 * D_MODEL], mod[:, 2 * D_MODEL:3 * D_MODEL]
    hb = (_rms(x) * (1.0 + sc1) + sh1).astype(BF16)

    u = _in_proj(hb, win_ref, OFF_GC, OFF_XC) * _in_proj(hb, win_ref, OFF_XC, OFF_Q)
    ubuf[SUBLANES:SUBLANES + t, :] = u
    wc = wconv_ref[...]
    conv = wc[0:1] * ubuf[SUBLANES - 2:SUBLANES - 2 + t, :] + wc[1:2] * ubuf[SUBLANES - 1:SUBLANES - 1 + t, :] + wc[2:3] * u
    conv_ref[0] = u[t - (CONV_W - 1):t]
    ubuf[SUBLANES - 2:SUBLANES, :] = u[t - (CONV_W - 1):t]

    cos, sin = cos_ref[...], sin_ref[...]
    q = (_rope(_in_proj(hb, win_ref, OFF_Q, OFF_K), cos, sin) * ATTN_SCALE).astype(BF16)
    k = _rope(_in_proj(hb, win_ref, OFF_K, OFF_V), cos, sin)
    v = _in_proj(hb, win_ref, OFF_V, OFF_GCONV)
    kbuf[WINDOW:WINDOW + t, :] = k.astype(BF16)
    vtbuf[:, WINDOW:WINDOW + t] = v.T.astype(BF16)
    k_ref[0] = k[t - WINDOW:t]
    v_ref[0] = v[t - WINDOW:t]

    nkeys = ATT_Q + WINDOW
    rq = GQA_GROUP * ATT_Q
    ki = lax.broadcasted_iota(I32, (nkeys, rq), 0)
    qi = lax.broadcasted_iota(I32, (nkeys, rq), 1) % ATT_Q
    band = ki // CHUNK - qi // CHUNK
    band_ok = (band >= 0) & (band <= WINDOW // CHUNK)
    def block(s):
        def load():
            mask = band_ok & (ki + (j * t + s * ATT_Q - WINDOW) >= 0)
            k_of_pair = lambda pair: kbuf[s * ATT_Q:s * ATT_Q + nkeys, pair * LANES:(pair + 1) * LANES]
            vt_of_pair = lambda pair: vtbuf[pair * LANES:(pair + 1) * LANES, s * ATT_Q:s * ATT_Q + nkeys]
            return q[s * ATT_Q:(s + 1) * ATT_Q], k_of_pair, vt_of_pair, mask, s * ATT_Q
        return load

    steps, side = _attention_free_steps(hb, conv, win_ref, wco_ref)
    _attention([block(s) for s in range(t // ATT_Q)], sinks_ref, obuf, steps)
    kbuf[0:WINDOW, :] = kbuf[t:t + WINDOW, :]
    vtbuf[:, 0:WINDOW] = vtbuf[:, t:t + WINDOW]

    x1_ref[0] = _mix_out(x, g1, side, obuf[...], wao_ref, wmo_ref)


def _mixer_prompt(x, mod, cos, sin, win, wconv, wco, wao, sinks, wmo):
    b, seq, d = x.shape
    t = MIX_TILE
    return pl.pallas_call(
        _mixer_prompt_kernel,
        grid=(b, seq // t),
        in_specs=[pl.BlockSpec((1, t, d), lambda i, j: (i, j, 0)),
                  pl.BlockSpec((1, 1, 6 * d), lambda i, j: (i, 0, 0)),
                  pl.BlockSpec((t, LANES), lambda i, j: (j, 0)),
                  pl.BlockSpec((t, LANES), lambda i, j: (j, 0)),
                  *[_const_spec(w.shape) for w in win],
                  _const_spec(wconv.shape), _const_spec(wco.shape), _const_spec(wao.shape),
                  pl.BlockSpec(memory_space=pltpu.SMEM),
                  _const_spec(wmo.shape)],
        out_specs=[pl.BlockSpec((1, t, d), lambda i, j: (i, j, 0)),
                   pl.BlockSpec((1, CONV_W - 1, D_CONV), lambda i, j: (i, 0, 0)),
                   pl.BlockSpec((1, WINDOW, KV_DIM), lambda i, j: (i, 0, 0)),
                   pl.BlockSpec((1, WINDOW, KV_DIM), lambda i, j: (i, 0, 0))],
        out_shape=[jax.ShapeDtypeStruct((b, seq, d), F32),
                   jax.ShapeDtypeStruct((b, CONV_W - 1, D_CONV), F32),
                   jax.ShapeDtypeStruct((b, WINDOW, KV_DIM), F32),
                   jax.ShapeDtypeStruct((b, WINDOW, KV_DIM), F32)],
        scratch_shapes=[pltpu.VMEM((SUBLANES + t, D_CONV), F32),
                        pltpu.VMEM((WINDOW + t, KV_DIM), BF16),
                        pltpu.VMEM((KV_DIM, WINDOW + t), BF16),
                        pltpu.VMEM((t, Q_DIM), F32)],
        compiler_params=pltpu.CompilerParams(dimension_semantics=("arbitrary", "arbitrary"),
                                             vmem_limit_bytes=VMEM_LIMIT),
        name="mixer_prompt",
    )(x, mod, cos, sin, *win, wconv, wco, wao, sinks, wmo)


def _mixer_sample_kernel(x_ref, mod_ref, cos_ref, sin_ref, ck_ref, cv_ref, sconv_ref, wall_ref, wq_ref,
                         wconv_ref, wco_ref, wao_ref, sinks_ref, wmo_ref, x1_ref, conv_ref, k_ref, v_ref, ubuf, obuf):
    win_ref = (wall_ref, wq_ref)
    bb, t, d = x_ref.shape
    x3 = x_ref[...]
    mod = mod_ref[...]
    sh1, sc1, g1 = mod[:, :, 0:d], mod[:, :, d:2 * d], mod[:, :, 2 * d:3 * d]
    x = x3.reshape(bb * t, d)
    hb = (_rms(x3) * (1.0 + sc1) + sh1).astype(BF16).reshape(bb * t, d)

    u = _in_proj(hb, win_ref, OFF_GC, OFF_XC) * _in_proj(hb, win_ref, OFF_XC, OFF_Q)
    u3 = u.reshape(bb, t, D_CONV)
    ubuf[:, SUBLANES - 2:SUBLANES, :] = sconv_ref[...]
    ubuf[:, SUBLANES:SUBLANES + t, :] = u3
    wc = wconv_ref[...]
    conv = (wc[0:1] * ubuf[:, SUBLANES - 2:SUBLANES - 2 + t, :] + wc[1:2] * ubuf[:, SUBLANES - 1:SUBLANES - 1 + t, :]
            + wc[2:3] * u3).reshape(bb * t, D_CONV)
    conv_ref[...] = u3[:, t - (CONV_W - 1):t, :]

    cos = jnp.concatenate([cos_ref[...]] * bb, axis=0)
    sin = jnp.concatenate([sin_ref[...]] * bb, axis=0)
    q = (_rope(_in_proj(hb, win_ref, OFF_Q, OFF_K), cos, sin) * ATTN_SCALE).astype(BF16)
    k = _rope(_in_proj(hb, win_ref, OFF_K, OFF_V), cos, sin)
    v = _in_proj(hb, win_ref, OFF_V, OFF_GCONV)
    per = ATT_Q // t
    nkeys = per * (WINDOW + t)
    rq = GQA_GROUP * ATT_Q
    key_stream = lax.broadcasted_iota(I32, (nkeys, rq), 0) // (WINDOW + t)
    query_stream = (lax.broadcasted_iota(I32, (nkeys, rq), 1) % ATT_Q) // t
    mask = key_stream == query_stream
    def block(blk):
        def load():
            k_parts, v_parts = [], []
            for b in range(blk * per, (blk + 1) * per):
                kb, vb = k[b * t:(b + 1) * t], v[b * t:(b + 1) * t]
                ck, cv = ck_ref[b], cv_ref[b]
                k_ref[b] = jnp.concatenate([ck[t:WINDOW], kb], axis=0)
                v_ref[b] = jnp.concatenate([cv[t:WINDOW], vb], axis=0)
                k_parts += [ck, kb]
                v_parts += [cv, vb]
            k_all = jnp.concatenate(k_parts, axis=0).astype(BF16)
            vt_all = jnp.concatenate(v_parts, axis=0).T.astype(BF16)
            k_of_pair = lambda pair: k_all[:, pair * LANES:(pair + 1) * LANES]
            vt_of_pair = lambda pair: vt_all[pair * LANES:(pair + 1) * LANES, :]
            return q[blk * ATT_Q:(blk + 1) * ATT_Q], k_of_pair, vt_of_pair, mask, blk * ATT_Q
        return load

    steps, side = _attention_free_steps(hb, conv, win_ref, wco_ref)
    _attention([block(blk) for blk in range(bb // per)], sinks_ref, obuf, steps)

    g1f = jnp.broadcast_to(g1, (bb, t, d)).reshape(bb * t, d)
    x1_ref[...] = _mix_out(x, g1f, side, obuf[...], wao_ref, wmo_ref).reshape(bb, t, d)


def _mixer_sample(x, mod, cos, sin, ck, cv, sconv, win, wconv, wco, wao, sinks, wmo):
    b, t, d = x.shape
    bb = SAMPLE_BB
    blk = lambda *s: pl.BlockSpec((bb,) + s, lambda i: (i, 0, 0))
    return pl.pallas_call(
        _mixer_sample_kernel,
        grid=(b // bb,),
        in_specs=[blk(t, d), blk(1, 6 * d),
                  pl.BlockSpec((t, LANES), lambda i: (0, 0)), pl.BlockSpec((t, LANES), lambda i: (0, 0)),
                  blk(WINDOW, KV_DIM), blk(WINDOW, KV_DIM), blk(CONV_W - 1, D_CONV),
                  *[_const_spec(w.shape) for w in win],
                  _const_spec(wconv.shape), _const_spec(wco.shape), _const_spec(wao.shape),
                  pl.BlockSpec(memory_space=pltpu.SMEM),
                  _const_spec(wmo.shape)],
        out_specs=[blk(t, d), blk(CONV_W - 1, D_CONV), blk(WINDOW, KV_DIM), blk(WINDOW, KV_DIM)],
        out_shape=[jax.ShapeDtypeStruct((b, t, d), F32),
                   jax.ShapeDtypeStruct((b, CONV_W - 1, D_CONV), F32),
                   jax.ShapeDtypeStruct((b, WINDOW, KV_DIM), F32),
                   jax.ShapeDtypeStruct((b, WINDOW, KV_DIM), F32)],
        scratch_shapes=[pltpu.VMEM((bb, SUBLANES + t, D_CONV), F32),
                        pltpu.VMEM((bb * t, Q_DIM), F32)],
        compiler_params=pltpu.CompilerParams(dimension_semantics=("arbitrary",), vmem_limit_bytes=VMEM_LIMIT),
        name="mixer_sample",
    )(x, mod, cos, sin, ck, cv, sconv, *win, wconv, wco, wao, sinks, wmo)


def _pre_kernel(*refs, prompt_tiles, has_sample):
    if has_sample:
        xp_ref, mp_ref, xs_ref, ms_ref, *refs = refs
    else:
        xp_ref, mp_ref, *refs = refs
    wsg_ref, wsu_ref, wsd_ref, wrh_ref, wrl_ref, rb_ref, h2_ref, base_ref, cw_ref, rank_ref, cnt_ref, carry = refs
    nc, c, d = xp_ref.shape
    t = nc * c
    x3, mod = xp_ref[...], mp_ref[...]
    if has_sample:
        is_prompt = pl.program_id(0) < prompt_tiles
        x3 = jnp.where(is_prompt, x3, xs_ref[...])
        mod = jnp.where(is_prompt, mod, ms_ref[...])
    sh2, sc2, g2 = mod[:, :, 0:d], mod[:, :, d:2 * d], mod[:, :, 2 * d:3 * d]
    h3 = _rms(x3) * (1.0 + sc2) + sh2
    h2 = h3.reshape(t, d)
    hb = h2.astype(BF16)
    h2_ref[...] = _pack_bf16_pairs(h2)
    shared = _dot((_silu(_dot(hb, wsg_ref[...])) * _dot(hb, wsu_ref[...])).astype(BF16), wsd_ref[...])
    base_ref[...] = x3 + g2 * shared.reshape(nc, c, d)

    h_lo = (h2 - hb.astype(F32)).astype(BF16)
    nt = lambda a, b: lax.dot_general(a, b, (((1,), (1,)), ((), ())), preferred_element_type=F32)
    logits = nt(wrh_ref[...], hb) + (nt(wrh_ref[...], h_lo) + nt(wrl_ref[...], hb))
    scores = _sigmoid(logits)
    biased = scores + rb_ref[...]
    g3 = biased.reshape(N_EXPERT_GROUPS, GROUP_SIZE, t)
    member = lax.broadcasted_iota(I32, g3.shape, 1)
    m1 = jnp.max(g3, axis=1, keepdims=True)
    first = jnp.min(jnp.where(g3 == m1, member, GROUP_SIZE), axis=1, keepdims=True)
    m2 = jnp.max(jnp.where(member == first, -jnp.inf, g3), axis=1, keepdims=True)
    gs = m1 + m2
    gidx = lax.broadcasted_iota(I32, gs.shape, 0)
    grank = jnp.zeros(gs.shape, I32)
    for o in range(N_EXPERT_GROUPS):
        other = gs[o:o + 1]
        grank += ((other > gs) | ((other == gs) & (o < gidx))).astype(I32)
    group_ok = grank < TOPK_GROUPS
    slot = jnp.zeros((1, 1, t), I32)
    takes = []
    for gi in range(N_EXPERT_GROUPS):
        ok = group_ok[gi:gi + 1]
        takes.append([ok & (slot == s) for s in range(TOPK_GROUPS)])
        slot = slot + ok.astype(I32)
    packed = []
    for s in range(TOPK_GROUPS):
        vals = jnp.zeros((GROUP_SIZE, t), F32)
        for gi in range(N_EXPERT_GROUPS):
            vals = jnp.where(takes[gi][s][0], g3[gi], vals)
        packed.append(vals)
    cand = jnp.concatenate(packed, axis=0)
    cidx = lax.broadcasted_iota(I32, cand.shape, 0)
    crank = jnp.zeros(cand.shape, I32)
    for o in range(TOPK_GROUPS * GROUP_SIZE):
        other = cand[o:o + 1]
        crank += ((other > cand) | ((other == cand) & (o < cidx))).astype(I32)
    chosen = crank < TOP_K
    sel_groups = []
    for gi in range(N_EXPERT_GROUPS):
        hit = jnp.zeros((GROUP_SIZE, t), jnp.bool_)
        for s in range(TOPK_GROUPS):
            hit = hit | (takes[gi][s][0] & chosen[s * GROUP_SIZE:(s + 1) * GROUP_SIZE])
        sel_groups.append(hit)
    sel = jnp.concatenate(sel_groups, axis=0)
    ssum = jnp.sum(jnp.where(sel, scores, 0.0), axis=0, keepdims=True)
    cw_ref[...] = jnp.where(sel, scores / ssum * ROUTED_SCALE, -1.0)

    @pl.when(pl.program_id(0) == 0)
    def _():
        carry[...] = jnp.zeros(carry.shape, F32)

    picked = sel.astype(BF16)
    earlier = (lax.broadcasted_iota(I32, (t, t), 0) < lax.broadcasted_iota(I32, (t, t), 1)).astype(BF16)
    rank_ref[...] = (carry[...] + _dot(picked, earlier)).astype(I32)
    carry[...] = carry[...] + jnp.sum(picked.astype(F32), axis=1, keepdims=True)
    cnt_ref[...] = carry[...].astype(I32)


def _pre(x1_p, p_chunk0, ncp, mod_p, x1_s, ncs, mod_s, wsg, wsu, wsd, wr_hi, wr_lo, rb):
    ncp_all, c, d = x1_p.shape
    nc = PRE_TILE // c
    nchunks = ncp + ncs
    n = nchunks * c
    pt, p0 = ncp // nc, p_chunk0 // nc
    tiles_per_stream = ncp_all // mod_p.shape[0] // nc
    blk3 = pl.BlockSpec((nc, c, d), lambda i: (i, 0, 0))
    p_tile = lambda i: p0 + jnp.minimum(i, pt - 1)
    s_tile = lambda i: jnp.maximum(i - pt, 0)
    s_args, s_specs = [], []
    if ncs:
        s_args = [x1_s, mod_s]
        s_specs = [pl.BlockSpec((nc, c, d), lambda i: (s_tile(i), 0, 0)),
                   pl.BlockSpec((nc, 1, 3 * d), lambda i: (s_tile(i), 0, 0))]
    return pl.pallas_call(
        functools.partial(_pre_kernel, prompt_tiles=pt, has_sample=bool(ncs)),
        grid=(nchunks // nc,),
        in_specs=[pl.BlockSpec((nc, c, d), lambda i: (p_tile(i), 0, 0)),
                  pl.BlockSpec((1, 1, 3 * d), lambda i: (p_tile(i) // tiles_per_stream, 0, 0))] + s_specs + [
                  _const_spec(wsg.shape), _const_spec(wsu.shape), _const_spec(wsd.shape),
                  _const_spec(wr_hi.shape), _const_spec(wr_lo.shape), _const_spec(rb.shape)],
        out_specs=[pl.BlockSpec((nc * c, d // 2), lambda i: (i, 0)), blk3,
                   pl.BlockSpec((N_EXPERTS, nc * c), lambda i: (0, i)),
                   pl.BlockSpec((N_EXPERTS, nc * c), lambda i: (0, i)),
                   pl.BlockSpec((N_EXPERTS, 1), lambda i: (0, 0))],
        out_shape=[jax.ShapeDtypeStruct((n, d // 2), I32),
                   jax.ShapeDtypeStruct((nchunks, c, d), F32),
                   jax.ShapeDtypeStruct((N_EXPERTS, n), F32),
                   jax.ShapeDtypeStruct((N_EXPERTS, n), I32),
                   jax.ShapeDtypeStruct((N_EXPERTS, 1), I32)],
        scratch_shapes=[pltpu.VMEM((N_EXPERTS, 1), F32)],
        compiler_params=pltpu.CompilerParams(dimension_semantics=("arbitrary",), vmem_limit_bytes=VMEM_LIMIT),
        name="pre_ffn",
    )(x1_p, mod_p, *s_args, wsg, wsu, wsd, wr_hi, wr_lo, rb)


def _slot_kernel(cw_ref, rank_ref, start_ref, pos_ref, w_ref, pos_tok_ref):
    cw = cw_ref[...]
    e, t = cw.shape
    sel = cw >= 0.0
    r = lax.broadcasted_iota(I32, (e, e), 0)
    c = lax.broadcasted_iota(I32, (e, e), 1)
    lower = (c < r).astype(BF16)
    kidx = _dot(lower, sel.astype(BF16))
    posf = start_ref[...].astype(F32) + rank_ref[...].astype(F32)
    pos_rows, w_rows = [], []
    for k in range(TOP_K):
        m = sel & (kidx == float(k))
        pos_rows.append(jnp.sum(jnp.where(m, posf, 0.0), axis=0, keepdims=True))
        w_rows.append(jnp.sum(jnp.where(m, cw, 0.0), axis=0, keepdims=True))
    pos_ref[...] = jnp.concatenate(pos_rows, axis=0).astype(I32)
    w_ref[...] = jnp.concatenate([jnp.broadcast_to(w, (SC_LANES, t)) for w in w_rows], axis=0).T
    pos_pad = jnp.concatenate(pos_rows + [jnp.zeros((LANES - TOP_K, t), F32)], axis=0)
    pos_tok_ref[...] = pos_pad.T[:, :TOP_K].astype(I32)


def _slots(cw, rank, seg_start):
    e, n = cw.shape
    t = RANK_TILE
    return pl.pallas_call(
        _slot_kernel,
        grid=(n // t,),
        in_specs=[pl.BlockSpec((e, t), lambda i: (0, i)), pl.BlockSpec((e, t), lambda i: (0, i)),
                  pl.BlockSpec((e, 1), lambda i: (0, 0))],
        out_specs=[pl.BlockSpec((TOP_K, t), lambda i: (0, i)), pl.BlockSpec((t, TOP_K * SC_LANES), lambda i: (i, 0)),
                   pl.BlockSpec((t, TOP_K), lambda i: (i, 0))],
        out_shape=[jax.ShapeDtypeStruct((TOP_K, n), I32), jax.ShapeDtypeStruct((n, TOP_K * SC_LANES), F32),
                   jax.ShapeDtypeStruct((n, TOP_K), I32)],
        compiler_params=pltpu.CompilerParams(dimension_semantics=("arbitrary",)),
        name="expert_slots",
    )(cw, rank, seg_start)


def _sc_mesh():
    return plsc.VectorSubcoreMesh(core_axis_name="c", subcore_axis_name="s")


def _sc_worker_id():
    return lax.axis_index("s") * (SC_WORKERS // 16) + lax.axis_index("c")


def _sc_dispatch(rows, pos, n_rows):
    n, d = rows.shape
    per_w = n // SC_WORKERS
    w = SC_WINDOW
    n_chunks = per_w // w

    @functools.partial(
        pl.kernel, mesh=_sc_mesh(),
        out_type=jax.ShapeDtypeStruct((n_rows, d), rows.dtype),
        scratch_types=[pltpu.VMEM((2, TOP_K, w), I32), pltpu.VMEM((2, w, d), rows.dtype),
                       pltpu.SemaphoreType.DMA((2,)), pltpu.SemaphoreType.DMA((2,)), pltpu.SemaphoreType.DMA((2,))],
        name="sc_dispatch")
    def k(rows_hbm, pos_hbm, o_hbm, idx_v, rows_v, row_sem, idx_sem, out_sem):
        wid = _sc_worker_id()
        base = wid * per_w

        def loads(c, slot):
            off = pl.multiple_of(base + c * w, SUBLANES)
            return (pltpu.make_async_copy(rows_hbm.at[pl.ds(off, w)], rows_v.at[slot], row_sem.at[slot]),
                    pltpu.make_async_copy(pos_hbm.at[wid * n_chunks + c], idx_v.at[slot], idx_sem.at[slot]))

        def scatters(slot):
            return [pltpu.make_async_copy(rows_v.at[slot], o_hbm.at[idx_v.at[slot, kk]], out_sem.at[slot])
                    for kk in range(TOP_K)]

        for cp in loads(0, 0):
            cp.start()
        for c in range(n_chunks):
            slot = c % 2
            for cp in loads(c, slot):
                cp.wait()
            for cp in scatters(slot):
                cp.start()
            if c >= 1:
                for cp in scatters(1 - slot):
                    cp.wait()
            if c + 1 < n_chunks:
                for cp in loads(c + 1, 1 - slot):
                    cp.start()
        for cp in scatters((n_chunks - 1) % 2):
            cp.wait()

    pos_chunks = pos.reshape(TOP_K, n // w, w).transpose(1, 0, 2)
    return k(rows, pos_chunks)


def _sc_collect_sum(rows, pos_tok, w_lanes):
    words = rows.shape[1]
    n = w_lanes.shape[0]
    lanes = SC_LANES
    per_w = n // SC_WORKERS
    tw = SC_SUM_TOKENS
    n_pairs = per_w // (2 * tw)
    col_blocks = words // lanes // SC_SUM_VREGS

    @functools.partial(
        pl.kernel, mesh=_sc_mesh(),
        out_type=jax.ShapeDtypeStruct((n, 2 * words), F32),
        scratch_types=[pltpu.VMEM((per_w * TOP_K,), I32), pltpu.VMEM((2, tw * TOP_K, words), I32),
                       pltpu.VMEM((2, tw, TOP_K * lanes), F32), pltpu.VMEM((2, tw, 2 * words), F32),
                       pltpu.SemaphoreType.DMA((2,)), pltpu.SemaphoreType.DMA((2,)), pltpu.SemaphoreType.DMA((2,))],
        compiler_params=pltpu.CompilerParams(needs_layout_passes=False),
        name="sc_collect_sum")
    def k(rows_hbm, pos_hbm, w_hbm, o_hbm, idx_v, rows_v, w_v, out_v, in_sem, w_sem, out_sem):
        base = pl.multiple_of(_sc_worker_id() * per_w, SUBLANES)
        pltpu.sync_copy(pos_hbm.at[pl.ds(pl.multiple_of(base * TOP_K, SUBLANES), per_w * TOP_K)], idx_v)

        def loads(c, slot):
            idx = idx_v.at[pl.ds(pl.multiple_of(c * tw * TOP_K, SUBLANES), tw * TOP_K)]
            tok0 = pl.multiple_of(base + c * tw, SUBLANES)
            return (pltpu.make_async_copy(rows_hbm.at[idx], rows_v.at[slot], in_sem.at[slot]),
                    pltpu.make_async_copy(w_hbm.at[pl.ds(tok0, tw)], w_v.at[slot], w_sem.at[slot]))

        def write(c, slot):
            tok0 = pl.multiple_of(base + c * tw, SUBLANES)
            return pltpu.make_async_copy(out_v.at[slot], o_hbm.at[pl.ds(tok0, tw)], out_sem.at[slot])

        high_half = jnp.full((lanes,), -65536, I32)
        sixteen = jnp.full((lanes,), 16, I32)

        def reduce_window(slot):
            rv, wv, ov = rows_v.at[slot], w_v.at[slot], out_v.at[slot]

            @pl.loop(0, tw)
            def _(t):
                for cb in range(col_blocks):
                    acc_lo, acc_hi = [None] * SC_SUM_VREGS, [None] * SC_SUM_VREGS
                    for kk in range(TOP_K):
                        wk = wv[t, pl.ds(kk * lanes, lanes)]
                        for c in range(SC_SUM_VREGS):
                            wd = rv[t * TOP_K + kk, pl.ds((cb * SC_SUM_VREGS + c) * lanes, lanes)]
                            lo = wk * plsc.bitcast(lax.shift_left(wd, sixteen), F32)
                            hi = wk * plsc.bitcast(wd & high_half, F32)
                            acc_lo[c] = lo if kk == 0 else acc_lo[c] + lo
                            acc_hi[c] = hi if kk == 0 else acc_hi[c] + hi
                    for c in range(SC_SUM_VREGS):
                        col = (cb * SC_SUM_VREGS + c) * lanes
                        ov[t, pl.ds(col, lanes)] = acc_lo[c]
                        ov[t, pl.ds(words + col, lanes)] = acc_hi[c]

        for cp in loads(0, 0):
            cp.start()

        @pl.loop(0, n_pairs)
        def _(p):
            c0 = 2 * p
            for cp in loads(c0 + 1, 1):
                cp.start()
            for cp in loads(c0, 0):
                cp.wait()
            reduce_window(0)
            write(c0, 0).start()
            for cp in loads(c0 + 1, 1):
                cp.wait()
            reduce_window(1)
            write(c0 + 1, 1).start()
            write(c0, 0).wait()

            @pl.when(p + 1 < n_pairs)
            def _():
                for cp in loads(c0 + 2, 0):
                    cp.start()

            write(c0 + 1, 1).wait()

    return k(rows, pos_tok, w_lanes)


def _gmm_kernel(tot_ref, ce_ref, row_ref, val_ref, ord_ref, nxt_ref, x_hbm, wg_hbm, wu_hbm, wd_hbm, y_hbm,
                wgb, wub, wdb, xbuf, xsem, ybuf, ysem, wgf, wuf, wdf, wsem):
    total = tot_ref[0]
    pieces = GMM_SUB // GMM_TAIL

    def w_copies(ex, slot):
        return [pltpu.make_async_copy(src.at[ex], dst.at[slot], wsem.at[slot, i])
                for i, (src, dst) in enumerate(((wg_hbm, wgf), (wu_hbm, wuf), (wd_hbm, wdf)))]

    def x_copy(g):
        slot = g % GMM_X_SLOTS
        rows = pl.ds(pl.multiple_of(row_ref[g], GMM_TAIL), GMM_SUB)
        return pltpu.make_async_copy(x_hbm.at[rows], xbuf.at[slot], xsem.at[slot])

    def y_piece(g, p):
        slot = g % 2
        rows = pl.ds(pl.multiple_of(row_ref[g] + p * GMM_TAIL, GMM_TAIL), GMM_TAIL)
        return pltpu.make_async_copy(ybuf.at[slot, pl.ds(p * GMM_TAIL, GMM_TAIL)], y_hbm.at[rows], ysem.at[slot])

    def for_y_pieces(g, action):
        for p in range(pieces):
            @pl.when(p * GMM_TAIL < val_ref[g])
            def _():
                action(y_piece(g, p))

    for ahead in range(GMM_X_SLOTS - 1):
        @pl.when(ahead < total)
        def _():
            x_copy(ahead).start()

    def chunk(g, carry):
        e = ce_ref[g]
        prev = ce_ref[jnp.maximum(g - 1, 0)]

        @pl.when((g == 0) | (e != prev))
        def _():
            slot = ord_ref[e] % GMM_W_SLOTS

            def start_ahead(ex, hops):
                for _ in range(hops):
                    ex = jnp.where(ex >= 0, nxt_ref[jnp.maximum(ex, 0)], -1)

                @pl.when(ex >= 0)
                def _():
                    for cp in w_copies(ex, ord_ref[ex] % GMM_W_SLOTS):
                        cp.start()

            @pl.when(g == 0)
            def _():
                for hops in range(GMM_W_SLOTS - 1):
                    start_ahead(e, hops)

            for cp in w_copies(e, slot):
                cp.wait()
            wgb[...] = wgf[slot].astype(BF16)
            wub[...] = wuf[slot].astype(BF16)
            wdb[...] = wdf[slot].astype(BF16)
            start_ahead(e, GMM_W_SLOTS - 1)

        x_copy(g).wait()

        @pl.when(g + GMM_X_SLOTS - 1 < total)
        def _():
            x_copy(g + GMM_X_SLOTS - 1).start()

        @pl.when(g >= 2)
        def _():
            for_y_pieces(g - 2, lambda cp: cp.wait())

        x_ref = xbuf.at[g % GMM_X_SLOTS]
        y_ref = ybuf.at[g % 2]

        def expert_rows(r0, n):
            rows = pl.ds(r0, n)
            lo, hi = _unpack_bf16_pairs(x_ref[rows, :])
            xb = jnp.concatenate([lo.astype(BF16), hi.astype(BF16)], axis=1)
            mid = (_silu(_dot(xb, wgb[...])) * _dot(xb, wub[...])).astype(BF16)
            y_ref[rows, :] = _pack_bf16_pairs(_dot(mid, wdb[...]))

        n_real = val_ref[g]

        @pl.when(n_real == GMM_SUB)
        def _():
            expert_rows(0, GMM_SUB)

        @pl.when((n_real < GMM_SUB) & (n_real >= GMM_MID))
        def _():
            expert_rows(0, GMM_MID)

        @pl.when(n_real < GMM_SUB)
        def _():
            done = jnp.where(n_real >= GMM_MID, GMM_MID, 0)

            @pl.loop(0, (n_real - done + GMM_TAIL - 1) // GMM_TAIL)
            def _(i):
                expert_rows(pl.multiple_of(done + i * GMM_TAIL, GMM_TAIL), GMM_TAIL)

        for_y_pieces(g, lambda cp: cp.start())
        return carry

    lax.fori_loop(0, total, chunk, 0)
    for back in (2, 1):
        @pl.when(total >= back)
        def _():
            for_y_pieces(total - back, lambda cp: cp.wait())


def _gmm(x_sorted, n_chunks, chunk_e, chunk_row, chunk_valid, e_ord, e_next, wg, wu, wd):
    r, half = x_sorted.shape
    d = 2 * half
    any_spec = pl.BlockSpec(memory_space=pl.ANY)
    return pl.pallas_call(
        _gmm_kernel,
        grid_spec=pltpu.PrefetchScalarGridSpec(
            num_scalar_prefetch=6,
            grid=(1,),
            in_specs=[any_spec, any_spec, any_spec, any_spec],
            out_specs=any_spec,
            scratch_shapes=[pltpu.VMEM((d, D_EXPERT), BF16), pltpu.VMEM((d, D_EXPERT), BF16),
                            pltpu.VMEM((D_EXPERT, d), BF16),
                            pltpu.VMEM((GMM_X_SLOTS, GMM_SUB, half), I32), pltpu.SemaphoreType.DMA((GMM_X_SLOTS,)),
                            pltpu.VMEM((2, GMM_SUB, half), I32), pltpu.SemaphoreType.DMA((2,)),
                            pltpu.VMEM((GMM_W_SLOTS, d, D_EXPERT), F32), pltpu.VMEM((GMM_W_SLOTS, d, D_EXPERT), F32),
                            pltpu.VMEM((GMM_W_SLOTS, D_EXPERT, d), F32), pltpu.SemaphoreType.DMA((GMM_W_SLOTS, 3))]),
        out_shape=jax.ShapeDtypeStruct((r, half), I32),
        compiler_params=pltpu.CompilerParams(dimension_semantics=("arbitrary",), vmem_limit_bytes=VMEM_LIMIT),
        name="expert_gmm",
    )(n_chunks, chunk_e, chunk_row, chunk_valid, e_ord, e_next, x_sorted, wg, wu, wd)


def _combine_kernel(base_ref, mod_ref, routed_ref, gain_ref, *rest):
    y_ref = rest[-1]
    d = base_ref.shape[-1]
    g2 = mod_ref[...][:, :, 2 * d:3 * d]
    out = base_ref[...] + g2 * routed_ref[...]
    y_ref[...] = _rms(out) * gain_ref[...]


def _combine(base, mod, routed, gain, first_chunk, n_chunks, out_chunks, out_first_chunk, out_buf=None):
    _, c, d = base.shape
    nc = COMB_TILE // c
    t0, o0 = first_chunk // nc, out_first_chunk // nc
    chunks_per_stream = out_chunks // mod.shape[0]
    if chunks_per_stream == 1:
        mod_spec = pl.BlockSpec((nc, 1, 3 * d), lambda i: (o0 + i, 0, 0))
    else:
        assert chunks_per_stream % nc == 0
        mod_spec = pl.BlockSpec((1, 1, 3 * d), lambda i: ((o0 + i) * nc // chunks_per_stream, 0, 0))
    blk3 = pl.BlockSpec((nc, c, d), lambda i: (t0 + i, 0, 0))
    in_specs = [blk3, mod_spec, blk3, pl.BlockSpec((1, 1, d), lambda i: (0, 0, 0))]
    args = [base, mod, routed, gain.reshape(1, 1, d)]
    aliases = {}
    if out_buf is not None:
        in_specs.append(pl.BlockSpec(memory_space=pl.ANY))
        args.append(out_buf)
        aliases = {len(args) - 1: 0}
    return pl.pallas_call(
        _combine_kernel,
        grid=(n_chunks // nc,),
        in_specs=in_specs,
        out_specs=pl.BlockSpec((nc, c, d), lambda i: (o0 + i, 0, 0)),
        out_shape=jax.ShapeDtypeStruct((out_chunks, c, d), F32),
        input_output_aliases=aliases,
        compiler_params=pltpu.CompilerParams(dimension_semantics=("arbitrary",), vmem_limit_bytes=VMEM_LIMIT),
        name="combine_norm",
    )(*args)


def _rope_tables(pos):
    half = HEAD_DIM // 2
    lane = jnp.arange(LANES, dtype=I32)
    inv_freq = ROPE_THETA ** (-(lane % half).astype(F32) / half)
    sign = jnp.where((lane % HEAD_DIM) < half, -1.0, 1.0).astype(F32)
    ang = pos.astype(F32)[:, None] * inv_freq[None, :]
    return jnp.cos(ang), jnp.sin(ang) * sign[None, :]


def _routed_ffn(h2, cw, rank, counts, w_gate, w_up, w_down):
    n, half = h2.shape
    counts = counts[:, 0]
    padded = (counts + GMM_TAIL - 1) // GMM_TAIL * GMM_TAIL
    seg_start = (jnp.cumsum(padded) - padded).astype(I32)
    n_rows = n * TOP_K + N_EXPERTS * GMM_TAIL + GMM_SUB
    e_chunks = (counts + GMM_SUB - 1) // GMM_SUB
    chunk_end = jnp.cumsum(e_chunks)
    max_chunks = n * TOP_K // GMM_SUB + N_EXPERTS
    g = jnp.arange(max_chunks, dtype=I32)
    chunk_e = jnp.minimum(jnp.sum((chunk_end[None, :] <= g[:, None]).astype(I32), axis=1), N_EXPERTS - 1)
    eids = jnp.arange(N_EXPERTS, dtype=I32)
    own = chunk_e[:, None] == eids[None, :]
    pick = lambda table: jnp.sum(jnp.where(own, table[None, :], 0), axis=1)
    in_expert = (g - pick(chunk_end - e_chunks)) * GMM_SUB
    chunk_row = (pick(seg_start) + in_expert).astype(I32)
    chunk_valid = jnp.clip(pick(counts) - in_expert, 0, GMM_SUB).astype(I32)
    n_chunks = chunk_end[-1:].astype(I32)
    has_rows = counts > 0
    e_ord = (jnp.cumsum(has_rows.astype(I32)) - has_rows.astype(I32)).astype(I32)
    later = has_rows[None, :] & (eids[None, :] > eids[:, None])
    e_next = jnp.min(jnp.where(later, eids[None, :], N_EXPERTS), axis=1)
    e_next = jnp.where(e_next == N_EXPERTS, -1, e_next).astype(I32)
    pos, w_lanes, pos_tok = _slots(cw, rank, seg_start[:, None])
    x_sorted = _sc_dispatch(h2, pos, n_rows)
    y_sorted = _gmm(x_sorted, n_chunks, chunk_e, chunk_row, chunk_valid, e_ord, e_next, w_gate, w_up, w_down)
    return _sc_collect_sum(y_sorted, pos_tok.reshape(n * TOP_K), w_lanes)


def kernel(x_prompt, x_sample, cache_k, cache_v, state_conv, c_prompt, c_sample, w_ada, b_ada, w_in, w_conv,
           w_conv_out, w_attn_o, attn_sinks, w_mix_out, w_router, router_bias, w_exp_gate, w_exp_up, w_exp_down,
           w_sh_gate, w_sh_up, w_sh_down, final_gain):
    assert w_ada.shape[0] == 1, "one layer"
    bp, seq, d = x_prompt.shape
    bs, ts, _ = x_sample.shape
    assert ts == CHUNK and seq % MIX_TILE == 0 and bs % SAMPLE_BB == 0

    c_all = jnp.concatenate([c_prompt, c_sample], axis=0)
    pad = (-c_all.shape[0]) % SUBLANES
    mod = _ada(jnp.pad(c_all, ((0, pad), (0, 0))), w_ada[0], b_ada[0])[:bp + bs]
    mod_p, mod_s = mod[:bp, None, :], mod[bp:, None, :]

    head_axes = (N_KV_HEADS // 2, 2, GQA_GROUP, HEAD_DIM)
    w_in_l = w_in[0]
    w_q = w_in_l[:, OFF_Q:OFF_K].reshape((d,) + head_axes).transpose(0, 1, 3, 2, 4).reshape(d, Q_DIM)
    w_o = w_attn_o[0].reshape(head_axes + (d,)).transpose(0, 2, 1, 3, 4).reshape(Q_DIM, d)
    win = (w_in_l.astype(BF16), w_q.astype(BF16))
    wco, wao, wmo = (w.astype(BF16) for w in (w_conv_out[0], w_o, w_mix_out[0]))
    cos_p, sin_p = _rope_tables(jnp.arange(seq, dtype=I32))
    cos_s, sin_s = _rope_tables(PAST_LEN + jnp.arange(ts, dtype=I32))

    x1_p, conv_p, k_p, v_p = _mixer_prompt(x_prompt, mod_p, cos_p, sin_p, win, w_conv[0], wco, wao, attn_sinks[0], wmo)
    x1_s, conv_s, k_s, v_s = _mixer_sample(
        x_sample, mod_s, cos_s, sin_s, cache_k[0].reshape(bs, WINDOW, KV_DIM), cache_v[0].reshape(bs, WINDOW, KV_DIM),
        state_conv[0], win, w_conv[0], wco, wao, attn_sinks[0], wmo)

    n_p, n_s = bp * seq, bs * ts
    n = n_p + n_s
    mod2_p, mod2_s = mod[:bp, None, 3 * d:], mod[bp:, None, 3 * d:]
    x1_pc = x1_p.reshape(n_p // CHUNK, CHUNK, d)
    wsg, wsu, wsd = (w[0].astype(BF16) for w in (w_sh_gate, w_sh_up, w_sh_down))
    wr_t, rb = w_router[0].T, router_bias[0][:, None]
    wr_hi = wr_t.astype(BF16)
    wr_lo = (wr_t - wr_hi.astype(F32)).astype(BF16)

    ncp, ncs = n_p // CHUNK, n_s // CHUNK
    half = (ncp + ncs) * FFN_SET_A_SHARE[0] // FFN_SET_A_SHARE[1]
    tile_chunks = max(PRE_TILE, COMB_TILE) // CHUNK
    assert half <= ncp and half % tile_chunks == 0 and (ncp - half) % tile_chunks == 0 and ncs % tile_chunks == 0
    for set_tokens in (half * CHUNK, n - half * CHUNK):
        assert set_tokens % (SC_WORKERS * SC_WINDOW) == 0 and set_tokens % RANK_TILE == 0
        assert set_tokens % (SC_WORKERS * 2 * SC_SUM_TOKENS) == 0
    y_p = None
    for p0, np_c, ns_c in ((0, half, 0), (half, ncp - half, ncs)):
        h2, base, cw, rank, counts = _pre(x1_pc, p0, np_c, mod2_p, x1_s, ns_c, mod2_s, wsg, wsu, wsd, wr_hi, wr_lo, rb)
        routed = _routed_ffn(h2, cw, rank, counts, w_exp_gate[0], w_exp_up[0], w_exp_down[0]).reshape(base.shape)
        y_p = _combine(base, mod2_p, routed, final_gain, 0, np_c, ncp, p0, out_buf=y_p)
        if ns_c:
            y_s = _combine(base, mod2_s, routed, final_gain, np_c, ns_c, ncs, 0)

    kv = lambda a: a.reshape(1, a.shape[0], WINDOW, N_KV_HEADS, HEAD_DIM)
    return (y_p.reshape(bp, seq, d), y_s, conv_p[None], kv(k_p), kv(v_p), conv_s[None], kv(k_s), kv(v_s))
```

```python
import functools
import itertools

import jax
import jax.numpy as jnp
from jax import lax
from jax.experimental import pallas as pl
from jax.experimental.pallas import tpu as pltpu
from jax.experimental.pallas import tpu_sc as plsc

F32 = jnp.float32
BF16 = jnp.bfloat16
I32 = jnp.int32

D_MODEL = 1024
CHUNK = 64
D_CONV = 1024
CONV_W = 3
N_HEADS = 16
N_KV_HEADS = 4
HEAD_DIM = 64
GQA_GROUP = N_HEADS // N_KV_HEADS
WINDOW = 128
ROPE_THETA = 10000.0
ATTN_SCALE = HEAD_DIM ** -0.5
N_EXPERTS = 64
TOP_K = 8
N_EXPERT_GROUPS = 8
GROUP_SIZE = N_EXPERTS // N_EXPERT_GROUPS
TOPK_GROUPS = 4
D_EXPERT = 256
D_SHARED = 256
ROUTED_SCALE = 2.5
EPS = 1e-6
PAST_LEN = 4096
Q_DIM = N_HEADS * HEAD_DIM
KV_DIM = N_KV_HEADS * HEAD_DIM
OFF_GB, OFF_GC, OFF_XC, OFF_Q, OFF_K, OFF_V, OFF_GCONV, OFF_GATTN, D_IN = itertools.accumulate(
    (0, D_CONV, D_CONV, D_CONV, Q_DIM, KV_DIM, KV_DIM, D_MODEL, D_MODEL))

LANES = 128
SUBLANES = 8
VMEM_LIMIT = 56 * 1024 * 1024

MIX_TILE = 512
ATT_Q = 128
MIX_SIDE_COLS = 256
SAMPLE_BB = 8
PRE_TILE = 512
RANK_TILE = 512
GMM_SUB = 512
GMM_MID = 256
GMM_TAIL = 128
GMM_X_SLOTS = 6
GMM_W_SLOTS = 4
COMB_TILE = 512
FFN_SET_A_SHARE = (2, 3)
SC_SUBCORES = 16
SC_WORKERS = 32
SC_WINDOW = 96
SC_LANES = 16
SC_SUM_TOKENS = 8
SC_SUM_VREGS = 16


def _const_spec(shape):
    nd = len(shape)
    return pl.BlockSpec(shape, lambda *_: (0,) * nd, pipeline_mode=pl.Buffered(1))


def _rms(x):
    return x * lax.rsqrt(jnp.mean(x * x, axis=-1, keepdims=True) + EPS)


def _sigmoid(x):
    return 1.0 / (1.0 + jnp.exp(-x))


def _silu(x):
    return x * _sigmoid(x)


def _dot(a, b):
    return jnp.dot(a, b, preferred_element_type=F32)


def _pack_bf16_pairs(x):
    half = x.shape[-1] // 2
    lo = lax.bitcast_convert_type(x[..., :half].astype(BF16).astype(F32), I32)
    hi = lax.bitcast_convert_type(x[..., half:].astype(BF16).astype(F32), I32)
    return lax.shift_right_logical(lo, 16) | hi


def _unpack_bf16_pairs(words):
    lo = lax.bitcast_convert_type(lax.shift_left(words, 16), F32)
    hi = lax.bitcast_convert_type(words & jnp.int32(-65536), F32)
    return lo, hi


def _ada_kernel(c_ref, w_ref, b_ref, o_ref):
    s = _silu(c_ref[...]).astype(BF16)
    o_ref[...] = _dot(s, w_ref[...].astype(BF16)) + b_ref[...]


def _ada(c_all, w_ada, b_ada):
    rows = c_all.shape[0]
    n_out = w_ada.shape[1]
    bn = 768
    return pl.pallas_call(
        _ada_kernel,
        grid=(n_out // bn,),
        in_specs=[pl.BlockSpec((rows, D_MODEL), lambda i: (0, 0)),
                  pl.BlockSpec((D_MODEL, bn), lambda i: (0, i)),
                  pl.BlockSpec((1, bn), lambda i: (0, i))],
        out_specs=pl.BlockSpec((rows, bn), lambda i: (0, i)),
        out_shape=jax.ShapeDtypeStruct((rows, n_out), F32),
        name="ada_mod",
    )(c_all, w_ada, b_ada.reshape(1, n_out))


def _rope(x, cos, sin_signed):
    lane = lax.broadcasted_iota(I32, (x.shape[0], LANES), 1)
    first_half = (lane % HEAD_DIM) < (HEAD_DIM // 2)
    outs = []
    for g in range(x.shape[1] // LANES):
        xg = x[:, g * LANES:(g + 1) * LANES]
        up = pltpu.roll(xg, LANES - HEAD_DIM // 2, axis=1)
        down = pltpu.roll(xg, HEAD_DIM // 2, axis=1)
        partner = jnp.where(first_half, up, down)
        outs.append(xg * cos + partner * sin_signed)
    return jnp.concatenate(outs, axis=1)


def _attention(blocks, sinks_ref, obuf, between):
    rq = GQA_GROUP * ATT_Q
    low = lax.broadcasted_iota(I32, (ATT_Q, LANES), 1) < HEAD_DIM
    head_of_lane = lax.broadcasted_iota(I32, (1, rq), 1) // ATT_Q
    units = [(b, pair, par) for b in range(len(blocks)) for pair in range(N_KV_HEADS // 2) for par in range(2)]
    loaded = {}

    def scores(u):
        b, pair, par = units[u]
        if b not in loaded:
            loaded.clear()
            loaded[b] = blocks[b]()
        q_blk, k_of_pair, _, mask, _ = loaded[b]
        keep = low if par == 0 else jnp.logical_not(low)
        cols = [q_blk[:, (GQA_GROUP * pair + i) * LANES:(GQA_GROUP * pair + i + 1) * LANES] for i in range(GQA_GROUP)]
        qg = jnp.concatenate([jnp.where(keep, c, jnp.zeros_like(c)) for c in cols], axis=0)
        st = lax.dot_general(k_of_pair(pair), qg, (((1,), (1,)), ((), ())), preferred_element_type=F32)
        vt = loaded[b][2](pair)[par * HEAD_DIM:(par + 1) * HEAD_DIM, :]
        return jnp.where(mask, st, -jnp.inf), vt, loaded[b][4]

    outs = []
    nxt = scores(0)
    for u, (b, pair, par) in enumerate(units):
        st, vt, row0 = nxt
        if u + 1 < len(units):
            nxt = scores(u + 1)
        if between:
            between.pop(0)()
        g = 2 * pair + par
        sink = jnp.full((1, rq), sinks_ref[g * GQA_GROUP + GQA_GROUP - 1], F32)
        for i in range(GQA_GROUP - 2, -1, -1):
            sink = jnp.where(head_of_lane == i, sinks_ref[g * GQA_GROUP + i], sink)
        m = jnp.maximum(jnp.max(st, axis=0, keepdims=True), sink)
        e = jnp.exp(st - m)
        z = jnp.sum(e, axis=0, keepdims=True) + jnp.exp(sink - m)
        outs.append(_dot(vt, e.astype(BF16)) / z)
        if par == 1:
            for i in range(GQA_GROUP):
                blk = jnp.concatenate([o[:, i * ATT_Q:(i + 1) * ATT_Q] for o in outs], axis=0)
                c0 = (GQA_GROUP * pair + i) * LANES
                obuf[row0:row0 + ATT_Q, c0:c0 + LANES] = blk.T
            outs = []
    for step in between:
        step()


def _in_proj(hb, win_refs, lo, hi):
    w_all, w_q = win_refs
    if (lo, hi) == (OFF_Q, OFF_K):
        return _dot(hb, w_q[...])
    assert hi <= OFF_Q or lo >= OFF_K
    return _dot(hb, w_all[:, lo:hi])


def _attention_free_steps(hb, conv, win_ref, wco_ref):
    out, parts = {}, {}
    n_parts = D_MODEL // MIX_SIDE_COLS

    def step(name, piece, compute):
        def run():
            parts.setdefault(name, []).append(compute(piece * MIX_SIDE_COLS, (piece + 1) * MIX_SIDE_COLS))
            if piece == n_parts - 1:
                out[name] = jnp.concatenate(parts.pop(name), axis=1)
        return run

    def conv_in():
        if "conv_in" not in out:
            out["conv_in"] = (out.pop("gate_b") * conv).astype(BF16)
        return out["conv_in"]

    computes = [("gate_b", lambda lo, hi: _in_proj(hb, win_ref, OFF_GB + lo, OFF_GB + hi)),
                ("g_conv", lambda lo, hi: _in_proj(hb, win_ref, OFF_GCONV + lo, OFF_GCONV + hi)),
                ("g_attn", lambda lo, hi: _in_proj(hb, win_ref, OFF_GATTN + lo, OFF_GATTN + hi)),
                ("y_conv", lambda lo, hi: _dot(conv_in(), wco_ref[:, lo:hi]))]
    return [step(name, p, fn) for name, fn in computes for p in range(n_parts)], out


def _mix_out(x, g1, side, y_attn_in, wao_ref, wmo_ref):
    y_attn = _dot(y_attn_in.astype(BF16), wao_ref[...])
    merged = _sigmoid(side["g_conv"]) * side["y_conv"] + _sigmoid(side["g_attn"]) * y_attn
    return x + g1 * _dot(merged.astype(BF16), wmo_ref[...])


def _mixer_prompt_kernel(x_ref, mod_ref, cos_ref, sin_ref, wall_ref, wq_ref, wconv_ref, wco_ref, wao_ref,
                         sinks_ref, wmo_ref, x1_ref, conv_ref, k_ref, v_ref, ubuf, kbuf, vtbuf, obuf):
    win_ref = (wall_ref, wq_ref)
    j = pl.program_id(1)
    t = x_ref.shape[1]

    @pl.when(j == 0)
    def _():
        ubuf[0:SUBLANES, :] = jnp.zeros((SUBLANES, D_CONV), F32)
        kbuf[0:WINDOW, :] = jnp.zeros((WINDOW, KV_DIM), BF16)
        vtbuf[:, 0:WINDOW] = jnp.zeros((KV_DIM, WINDOW), BF16)

    x = x_ref[0]
    mod = mod_ref[0]
    sh1, sc1, g1 = mod[:, 0:D_MODEL], mod[:, D_MODEL:2 * D_MODEL], mod[:, 2 * D_MODEL:3 * D_MODEL]
    hb = (_rms(x) * (1.0 + sc1) + sh1).astype(BF16)

    u = _in_proj(hb, win_ref, OFF_GC, OFF_XC) * _in_proj(hb, win_ref, OFF_XC, OFF_Q)
    ubuf[SUBLANES:SUBLANES + t, :] = u
    wc = wconv_ref[...]
    conv = wc[0:1] * ubuf[SUBLANES - 2:SUBLANES - 2 + t, :] + wc[1:2] * ubuf[SUBLANES - 1:SUBLANES - 1 + t, :] + wc[2:3] * u
    conv_ref[0] = u[t - (CONV_W - 1):t]
    ubuf[SUBLANES - 2:SUBLANES, :] = u[t - (CONV_W - 1):t]

    cos, sin = cos_ref[...], sin_ref[...]
    q = (_rope(_in_proj(hb, win_ref, OFF_Q, OFF_K), cos, sin) * ATTN_SCALE).astype(BF16)
    k = _rope(_in_proj(hb, win_ref, OFF_K, OFF_V), cos, sin)
    v = _in_proj(hb, win_ref, OFF_V, OFF_GCONV)
    kbuf[WINDOW:WINDOW + t, :] = k.astype(BF16)
    vtbuf[:, WINDOW:WINDOW + t] = v.T.astype(BF16)
    k_ref[0] = k[t - WINDOW:t]
    v_ref[0] = v[t - WINDOW:t]

    nkeys = ATT_Q + WINDOW
    rq = GQA_GROUP * ATT_Q
    ki = lax.broadcasted_iota(I32, (nkeys, rq), 0)
    qi = lax.broadcasted_iota(I32, (nkeys, rq), 1) % ATT_Q
    band = ki // CHUNK - qi // CHUNK
    band_ok = (band >= 0) & (band <= WINDOW // CHUNK)
    def block(s):
        def load():
            mask = band_ok & (ki + (j * t + s * ATT_Q - WINDOW) >= 0)
            k_of_pair = lambda pair: kbuf[s * ATT_Q:s * ATT_Q + nkeys, pair * LANES:(pair + 1) * LANES]
            vt_of_pair = lambda pair: vtbuf[pair * LANES:(pair + 1) * LANES, s * ATT_Q:s * ATT_Q + nkeys]
            return q[s * ATT_Q:(s + 1) * ATT_Q], k_of_pair, vt_of_pair, mask, s * ATT_Q
        return load

    steps, side = _attention_free_steps(hb, conv, win_ref, wco_ref)
    _attention([block(s) for s in range(t // ATT_Q)], sinks_ref, obuf, steps)
    kbuf[0:WINDOW, :] = kbuf[t:t + WINDOW, :]
    vtbuf[:, 0:WINDOW] = vtbuf[:, t:t + WINDOW]

    x1_ref[0] = _mix_out(x, g1, side, obuf[...], wao_ref, wmo_ref)


def _mixer_prompt(x, mod, cos, sin, win, wconv, wco, wao, sinks, wmo):
    b, seq, d = x.shape
    t = MIX_TILE
    return pl.pallas_call(
        _mixer_prompt_kernel,
        grid=(b, seq // t),
        in_specs=[pl.BlockSpec((1, t, d), lambda i, j: (i, j, 0)),
                  pl.BlockSpec((1, 1, 6 * d), lambda i, j: (i, 0, 0)),
                  pl.BlockSpec((t, LANES), lambda i, j: (j, 0)),
                  pl.BlockSpec((t, LANES), lambda i, j: (j, 0)),
                  *[_const_spec(w.shape) for w in win],
                  _const_spec(wconv.shape), _const_spec(wco.shape), _const_spec(wao.shape),
                  pl.BlockSpec(memory_space=pltpu.SMEM),
                  _const_spec(wmo.shape)],
        out_specs=[pl.BlockSpec((1, t, d), lambda i, j: (i, j, 0)),
                   pl.BlockSpec((1, CONV_W - 1, D_CONV), lambda i, j: (i, 0, 0)),
                   pl.BlockSpec((1, WINDOW, KV_DIM), lambda i, j: (i, 0, 0)),
                   pl.BlockSpec((1, WINDOW, KV_DIM), lambda i, j: (i, 0, 0))],
        out_shape=[jax.ShapeDtypeStruct((b, seq, d), F32),
                   jax.ShapeDtypeStruct((b, CONV_W - 1, D_CONV), F32),
                   jax.ShapeDtypeStruct((b, WINDOW, KV_DIM), F32),
                   jax.ShapeDtypeStruct((b, WINDOW, KV_DIM), F32)],
        scratch_shapes=[pltpu.VMEM((SUBLANES + t, D_CONV), F32),
                        pltpu.VMEM((WINDOW + t, KV_DIM), BF16),
                        pltpu.VMEM((KV_DIM, WINDOW + t), BF16),
                        pltpu.VMEM((t, Q_DIM), F32)],
        compiler_params=pltpu.CompilerParams(dimension_semantics=("arbitrary", "arbitrary"),
                                             vmem_limit_bytes=VMEM_LIMIT),
        name="mixer_prompt",
    )(x, mod, cos, sin, *win, wconv, wco, wao, sinks, wmo)


def _mixer_sample_kernel(x_ref, mod_ref, cos_ref, sin_ref, ck_ref, cv_ref, sconv_ref, wall_ref, wq_ref,
                         wconv_ref, wco_ref, wao_ref, sinks_ref, wmo_ref, x1_ref, conv_ref, k_ref, v_ref, ubuf, obuf):
    win_ref = (wall_ref, wq_ref)
    bb, t, d = x_ref.shape
    x3 = x_ref[...]
    mod = mod_ref[...]
    sh1, sc1, g1 = mod[:, :, 0:d], mod[:, :, d:2 * d], mod[:, :, 2 * d:3 * d]
    x = x3.reshape(bb * t, d)
    hb = (_rms(x3) * (1.0 + sc1) + sh1).astype(BF16).reshape(bb * t, d)

    u = _in_proj(hb, win_ref, OFF_GC, OFF_XC) * _in_proj(hb, win_ref, OFF_XC, OFF_Q)
    u3 = u.reshape(bb, t, D_CONV)
    ubuf[:, SUBLANES - 2:SUBLANES, :] = sconv_ref[...]
    ubuf[:, SUBLANES:SUBLANES + t, :] = u3
    wc = wconv_ref[...]
    conv = (wc[0:1] * ubuf[:, SUBLANES - 2:SUBLANES - 2 + t, :] + wc[1:2] * ubuf[:, SUBLANES - 1:SUBLANES - 1 + t, :]
            + wc[2:3] * u3).reshape(bb * t, D_CONV)
    conv_ref[...] = u3[:, t - (CONV_W - 1):t, :]

    cos = jnp.concatenate([cos_ref[...]] * bb, axis=0)
    sin = jnp.concatenate([sin_ref[...]] * bb, axis=0)
    q = (_rope(_in_proj(hb, win_ref, OFF_Q, OFF_K), cos, sin) * ATTN_SCALE).astype(BF16)
    k = _rope(_in_proj(hb, win_ref, OFF_K, OFF_V), cos, sin)
    v = _in_proj(hb, win_ref, OFF_V, OFF_GCONV)
    per = ATT_Q // t
    nkeys = per * (WINDOW + t)
    rq = GQA_GROUP * ATT_Q
    key_stream = lax.broadcasted_iota(I32, (nkeys, rq), 0) // (WINDOW + t)
    query_stream = (lax.broadcasted_iota(I32, (nkeys, rq), 1) % ATT_Q) // t
    mask = key_stream == query_stream
    def block(blk):
        def load():
            k_parts, v_parts = [], []
            for b in range(blk * per, (blk + 1) * per):
                kb, vb = k[b * t:(b + 1) * t], v[b * t:(b + 1) * t]
                ck, cv = ck_ref[b], cv_ref[b]
                k_ref[b] = jnp.concatenate([ck[t:WINDOW], kb], axis=0)
                v_ref[b] = jnp.concatenate([cv[t:WINDOW], vb], axis=0)
                k_parts += [ck, kb]
                v_parts += [cv, vb]
            k_all = jnp.concatenate(k_parts, axis=0).astype(BF16)
            vt_all = jnp.concatenate(v_parts, axis=0).T.astype(BF16)
            k_of_pair = lambda pair: k_all[:, pair * LANES:(pair + 1) * LANES]
            vt_of_pair = lambda pair: vt_all[pair * LANES:(pair + 1) * LANES, :]
            return q[blk * ATT_Q:(blk + 1) * ATT_Q], k_of_pair, vt_of_pair, mask, blk * ATT_Q
        return load

    steps, side = _attention_free_steps(hb, conv, win_ref, wco_ref)
    _attention([block(blk) for blk in range(bb // per)], sinks_ref, obuf, steps)

    g1f = jnp.broadcast_to(g1, (bb, t, d)).reshape(bb * t, d)
    x1_ref[...] = _mix_out(x, g1f, side, obuf[...], wao_ref, wmo_ref).reshape(bb, t, d)


def _mixer_sample(x, mod, cos, sin, ck, cv, sconv, win, wconv, wco, wao, sinks, wmo):
    b, t, d = x.shape
    bb = SAMPLE_BB
    blk = lambda *s: pl.BlockSpec((bb,) + s, lambda i: (i, 0, 0))
    return pl.pallas_call(
        _mixer_sample_kernel,
        grid=(b // bb,),
        in_specs=[blk(t, d), blk(1, 6 * d),
                  pl.BlockSpec((t, LANES), lambda i: (0, 0)), pl.BlockSpec((t, LANES), lambda i: (0, 0)),
                  blk(WINDOW, KV_DIM), blk(WINDOW, KV_DIM), blk(CONV_W - 1, D_CONV),
                  *[_const_spec(w.shape) for w in win],
                  _const_spec(wconv.shape), _const_spec(wco.shape), _const_spec(wao.shape),
                  pl.BlockSpec(memory_space=pltpu.SMEM),
                  _const_spec(wmo.shape)],
        out_specs=[blk(t, d), blk(CONV_W - 1, D_CONV), blk(WINDOW, KV_DIM), blk(WINDOW, KV_DIM)],
        out_shape=[jax.ShapeDtypeStruct((b, t, d), F32),
                   jax.ShapeDtypeStruct((b, CONV_W - 1, D_CONV), F32),
                   jax.ShapeDtypeStruct((b, WINDOW, KV_DIM), F32),
                   jax.ShapeDtypeStruct((b, WINDOW, KV_DIM), F32)],
        scratch_shapes=[pltpu.VMEM((bb, SUBLANES + t, D_CONV), F32),
                        pltpu.VMEM((bb * t, Q_DIM), F32)],
        compiler_params=pltpu.CompilerParams(dimension_semantics=("arbitrary",), vmem_limit_bytes=VMEM_LIMIT),
        name="mixer_sample",
    )(x, mod, cos, sin, ck, cv, sconv, *win, wconv, wco, wao, sinks, wmo)


def _pre_kernel(*refs, prompt_tiles, has_sample):
    if has_sample:
        xp_ref, mp_ref, xs_ref, ms_ref, *refs = refs
    else:
        xp_ref, mp_ref, *refs = refs
    wsg_ref, wsu_ref, wsd_ref, wrh_ref, wrl_ref, rb_ref, h2_ref, base_ref, cw_ref, rank_ref, cnt_ref, carry = refs
    nc, c, d = xp_ref.shape
    t = nc * c
    x3, mod = xp_ref[...], mp_ref[...]
    if has_sample:
        is_prompt = pl.program_id(0) < prompt_tiles
        x3 = jnp.where(is_prompt, x3, xs_ref[...])
        mod = jnp.where(is_prompt, mod, ms_ref[...])
    sh2, sc2, g2 = mod[:, :, 0:d], mod[:, :, d:2 * d], mod[:, :, 2 * d:3 * d]
    h3 = _rms(x3) * (1.0 + sc2) + sh2
    h2 = h3.reshape(t, d)
    hb = h2.astype(BF16)
    h2_ref[...] = _pack_bf16_pairs(h2)
    shared = _dot((_silu(_dot(hb, wsg_ref[...])) * _dot(hb, wsu_ref[...])).astype(BF16), wsd_ref[...])
    base_ref[...] = x3 + g2 * shared.reshape(nc, c, d)

    h_lo = (h2 - hb.astype(F32)).astype(BF16)
    nt = lambda a, b: lax.dot_general(a, b, (((1,), (1,)), ((), ())), preferred_element_type=F32)
    logits = nt(wrh_ref[...], hb) + (nt(wrh_ref[...], h_lo) + nt(wrl_ref[...], hb))
    scores = _sigmoid(logits)
    biased = scores + rb_ref[...]
    g3 = biased.reshape(N_EXPERT_GROUPS, GROUP_SIZE, t)
    member = lax.broadcasted_iota(I32, g3.shape, 1)
    m1 = jnp.max(g3, axis=1, keepdims=True)
    first = jnp.min(jnp.where(g3 == m1, member, GROUP_SIZE), axis=1, keepdims=True)
    m2 = jnp.max(jnp.where(member == first, -jnp.inf, g3), axis=1, keepdims=True)
    gs = m1 + m2
    gidx = lax.broadcasted_iota(I32, gs.shape, 0)
    grank = jnp.zeros(gs.shape, I32)
    for o in range(N_EXPERT_GROUPS):
        other = gs[o:o + 1]
        grank += ((other > gs) | ((other == gs) & (o < gidx))).astype(I32)
    group_ok = grank < TOPK_GROUPS
    slot = jnp.zeros((1, 1, t), I32)
    takes = []
    for gi in range(N_EXPERT_GROUPS):
        ok = group_ok[gi:gi + 1]
        takes.append([ok & (slot == s) for s in range(TOPK_GROUPS)])
        slot = slot + ok.astype(I32)
    packed = []
    for s in range(TOPK_GROUPS):
        vals = jnp.zeros((GROUP_SIZE, t), F32)
        for gi in range(N_EXPERT_GROUPS):
            vals = jnp.where(takes[gi][s][0], g3[gi], vals)
        packed.append(vals)
    cand = jnp.concatenate(packed, axis=0)
    cidx = lax.broadcasted_iota(I32, cand.shape, 0)
    crank = jnp.zeros(cand.shape, I32)
    for o in range(TOPK_GROUPS * GROUP_SIZE):
        other = cand[o:o + 1]
        crank += ((other > cand) | ((other == cand) & (o < cidx))).astype(I32)
    chosen = crank < TOP_K
    sel_groups = []
    for gi in range(N_EXPERT_GROUPS):
        hit = jnp.zeros((GROUP_SIZE, t), jnp.bool_)
        for s in range(TOPK_GROUPS):
            hit = hit | (takes[gi][s][0] & chosen[s * GROUP_SIZE:(s + 1) * GROUP_SIZE])
        sel_groups.append(hit)
    sel = jnp.concatenate(sel_groups, axis=0)
    ssum = jnp.sum(jnp.where(sel, scores, 0.0), axis=0, keepdims=True)
    cw_ref[...] = jnp.where(sel, scores / ssum * ROUTED_SCALE, -1.0)

    @pl.when(pl.program_id(0) == 0)
    def _():
        carry[...] = jnp.zeros(carry.shape, F32)

    picked = sel.astype(BF16)
    earlier = (lax.broadcasted_iota(I32, (t, t), 0) < lax.broadcasted_iota(I32, (t, t), 1)).astype(BF16)
    rank_ref[...] = (carry[...] + _dot(picked, earlier)).astype(I32)
    carry[...] = carry[...] + jnp.sum(picked.astype(F32), axis=1, keepdims=True)
    cnt_ref[...] = carry[...].astype(I32)


def _pre(x1_p, p_chunk0, ncp, mod_p, x1_s, ncs, mod_s, wsg, wsu, wsd, wr_hi, wr_lo, rb):
    ncp_all, c, d = x1_p.shape
    nc = PRE_TILE // c
    nchunks = ncp + ncs
    n = nchunks * c
    pt, p0 = ncp // nc, p_chunk0 // nc
    tiles_per_stream = ncp_all // mod_p.shape[0] // nc
    blk3 = pl.BlockSpec((nc, c, d), lambda i: (i, 0, 0))
    p_tile = lambda i: p0 + jnp.minimum(i, pt - 1)
    s_tile = lambda i: jnp.maximum(i - pt, 0)
    s_args, s_specs = [], []
    if ncs:
        s_args = [x1_s, mod_s]
        s_specs = [pl.BlockSpec((nc, c, d), lambda i: (s_tile(i), 0, 0)),
                   pl.BlockSpec((nc, 1, 3 * d), lambda i: (s_tile(i), 0, 0))]
    return pl.pallas_call(
        functools.partial(_pre_kernel, prompt_tiles=pt, has_sample=bool(ncs)),
        grid=(nchunks // nc,),
        in_specs=[pl.BlockSpec((nc, c, d), lambda i: (p_tile(i), 0, 0)),
                  pl.BlockSpec((1, 1, 3 * d), lambda i: (p_tile(i) // tiles_per_stream, 0, 0))] + s_specs + [
                  _const_spec(wsg.shape), _const_spec(wsu.shape), _const_spec(wsd.shape),
                  _const_spec(wr_hi.shape), _const_spec(wr_lo.shape), _const_spec(rb.shape)],
        out_specs=[pl.BlockSpec((nc * c, d // 2), lambda i: (i, 0)), blk3,
                   pl.BlockSpec((N_EXPERTS, nc * c), lambda i: (0, i)),
                   pl.BlockSpec((N_EXPERTS, nc * c), lambda i: (0, i)),
                   pl.BlockSpec((N_EXPERTS, 1), lambda i: (0, 0))],
        out_shape=[jax.ShapeDtypeStruct((n, d // 2), I32),
                   jax.ShapeDtypeStruct((nchunks, c, d), F32),
                   jax.ShapeDtypeStruct((N_EXPERTS, n), F32),
                   jax.ShapeDtypeStruct((N_EXPERTS, n), I32),
                   jax.ShapeDtypeStruct((N_EXPERTS, 1), I32)],
        scratch_shapes=[pltpu.VMEM((N_EXPERTS, 1), F32)],
        compiler_params=pltpu.CompilerParams(dimension_semantics=("arbitrary",), vmem_limit_bytes=VMEM_LIMIT),
        name="pre_ffn",
    )(x1_p, mod_p, *s_args, wsg, wsu, wsd, wr_hi, wr_lo, rb)


def _slot_kernel(cw_ref, rank_ref, start_ref, pos_ref, w_ref, pos_tok_ref):
    cw = cw_ref[...]
    e, t = cw.shape
    sel = cw >= 0.0
    r = lax.broadcasted_iota(I32, (e, e), 0)
    c = lax.broadcasted_iota(I32, (e, e), 1)
    lower = (c < r).astype(BF16)
    kidx = _dot(lower, sel.astype(BF16))
    posf = start_ref[...].astype(F32) + rank_ref[...].astype(F32)
    pos_rows, w_rows = [], []
    for k in range(TOP_K):
        m = sel & (kidx == float(k))
        pos_rows.append(jnp.sum(jnp.where(m, posf, 0.0), axis=0, keepdims=True))
        w_rows.append(jnp.sum(jnp.where(m, cw, 0.0), axis=0, keepdims=True))
    pos_ref[...] = jnp.concatenate(pos_rows, axis=0).astype(I32)
    w_ref[...] = jnp.concatenate([jnp.broadcast_to(w, (SC_LANES, t)) for w in w_rows], axis=0).T
    pos_pad = jnp.concatenate(pos_rows + [jnp.zeros((LANES - TOP_K, t), F32)], axis=0)
    pos_tok_ref[...] = pos_pad.T[:, :TOP_K].astype(I32)


def _slots(cw, rank, seg_start):
    e, n = cw.shape
    t = RANK_TILE
    return pl.pallas_call(
        _slot_kernel,
        grid=(n // t,),
        in_specs=[pl.BlockSpec((e, t), lambda i: (0, i)), pl.BlockSpec((e, t), lambda i: (0, i)),
                  pl.BlockSpec((e, 1), lambda i: (0, 0))],
        out_specs=[pl.BlockSpec((TOP_K, t), lambda i: (0, i)), pl.BlockSpec((t, TOP_K * SC_LANES), lambda i: (i, 0)),
                   pl.BlockSpec((t, TOP_K), lambda i: (i, 0))],
        out_shape=[jax.ShapeDtypeStruct((TOP_K, n), I32), jax.ShapeDtypeStruct((n, TOP_K * SC_LANES), F32),
                   jax.ShapeDtypeStruct((n, TOP_K), I32)],
        compiler_params=pltpu.CompilerParams(dimension_semantics=("arbitrary",)),
        name="expert_slots",
    )(cw, rank, seg_start)


def _sc_mesh():
    return plsc.VectorSubcoreMesh(core_axis_name="c", subcore_axis_name="s")


def _sc_worker_id():
    return lax.axis_index("s") * (SC_WORKERS // SC_SUBCORES) + lax.axis_index("c")


def _sc_dispatch(rows, pos, n_rows):
    n, d = rows.shape
    per_w = n // SC_WORKERS
    w = SC_WINDOW
    n_chunks = per_w // w

    @functools.partial(
        pl.kernel, mesh=_sc_mesh(),
        out_type=jax.ShapeDtypeStruct((n_rows, d), rows.dtype),
        scratch_types=[pltpu.VMEM((2, TOP_K, w), I32), pltpu.VMEM((2, w, d), rows.dtype),
                       pltpu.SemaphoreType.DMA((2,)), pltpu.SemaphoreType.DMA((2,)), pltpu.SemaphoreType.DMA((2,))],
        name="sc_dispatch")
    def k(rows_hbm, pos_hbm, o_hbm, idx_v, rows_v, row_sem, idx_sem, out_sem):
        wid = _sc_worker_id()
        base = wid * per_w

        def loads(c, slot):
            off = pl.multiple_of(base + c * w, SUBLANES)
            return (pltpu.make_async_copy(rows_hbm.at[pl.ds(off, w)], rows_v.at[slot], row_sem.at[slot]),
                    pltpu.make_async_copy(pos_hbm.at[wid * n_chunks + c], idx_v.at[slot], idx_sem.at[slot]))

        def scatters(slot):
            return [pltpu.make_async_copy(rows_v.at[slot], o_hbm.at[idx_v.at[slot, kk]], out_sem.at[slot])
                    for kk in range(TOP_K)]

        for cp in loads(0, 0):
            cp.start()
        for c in range(n_chunks):
            slot = c % 2
            for cp in loads(c, slot):
                cp.wait()
            for cp in scatters(slot):
                cp.start()
            if c >= 1:
                for cp in scatters(1 - slot):
                    cp.wait()
            if c + 1 < n_chunks:
                for cp in loads(c + 1, 1 - slot):
                    cp.start()
        for cp in scatters((n_chunks - 1) % 2):
            cp.wait()

    pos_chunks = pos.reshape(TOP_K, n // w, w).transpose(1, 0, 2)
    return k(rows, pos_chunks)


def _sc_collect_sum(rows, pos_tok, w_lanes):
    words = rows.shape[1]
    n = w_lanes.shape[0]
    lanes = SC_LANES
    per_w = n // SC_WORKERS
    tw = SC_SUM_TOKENS
    n_pairs = per_w // (2 * tw)
    col_blocks = words // lanes // SC_SUM_VREGS

    @functools.partial(
        pl.kernel, mesh=_sc_mesh(),
        out_type=jax.ShapeDtypeStruct((n, 2 * words), F32),
        scratch_types=[pltpu.VMEM((per_w * TOP_K,), I32), pltpu.VMEM((2, tw * TOP_K, words), I32),
                       pltpu.VMEM((2, tw, TOP_K * lanes), F32), pltpu.VMEM((2, tw, 2 * words), F32),
                       pltpu.SemaphoreType.DMA((2,)), pltpu.SemaphoreType.DMA((2,)), pltpu.SemaphoreType.DMA((2,))],
        compiler_params=pltpu.CompilerParams(needs_layout_passes=False),
        name="sc_collect_sum")
    def k(rows_hbm, pos_hbm, w_hbm, o_hbm, idx_v, rows_v, w_v, out_v, in_sem, w_sem, out_sem):
        base = pl.multiple_of(_sc_worker_id() * per_w, SUBLANES)
        pltpu.sync_copy(pos_hbm.at[pl.ds(pl.multiple_of(base * TOP_K, SUBLANES), per_w * TOP_K)], idx_v)

        def loads(c, slot):
            idx = idx_v.at[pl.ds(pl.multiple_of(c * tw * TOP_K, SUBLANES), tw * TOP_K)]
            tok0 = pl.multiple_of(base + c * tw, SUBLANES)
            return (pltpu.make_async_copy(rows_hbm.at[idx], rows_v.at[slot], in_sem.at[slot]),
                    pltpu.make_async_copy(w_hbm.at[pl.ds(tok0, tw)], w_v.at[slot], w_sem.at[slot]))

        def write(c, slot):
            tok0 = pl.multiple_of(base + c * tw, SUBLANES)
            return pltpu.make_async_copy(out_v.at[slot], o_hbm.at[pl.ds(tok0, tw)], out_sem.at[slot])

        high_half = jnp.full((lanes,), -65536, I32)
        sixteen = jnp.full((lanes,), 16, I32)

        def reduce_window(slot):
            rv, wv, ov = rows_v.at[slot], w_v.at[slot], out_v.at[slot]

            @pl.loop(0, tw)
            def _(t):
                for cb in range(col_blocks):
                    acc_lo, acc_hi = [None] * SC_SUM_VREGS, [None] * SC_SUM_VREGS
                    for kk in range(TOP_K):
                        wk = wv[t, pl.ds(kk * lanes, lanes)]
                        for c in range(SC_SUM_VREGS):
                            wd = rv[t * TOP_K + kk, pl.ds((cb * SC_SUM_VREGS + c) * lanes, lanes)]
                            lo = wk * plsc.bitcast(lax.shift_left(wd, sixteen), F32)
                            hi = wk * plsc.bitcast(wd & high_half, F32)
                            acc_lo[c] = lo if kk == 0 else acc_lo[c] + lo
                            acc_hi[c] = hi if kk == 0 else acc_hi[c] + hi
                    for c in range(SC_SUM_VREGS):
                        col = (cb * SC_SUM_VREGS + c) * lanes
                        ov[t, pl.ds(col, lanes)] = acc_lo[c]
                        ov[t, pl.ds(words + col, lanes)] = acc_hi[c]

        for cp in loads(0, 0):
            cp.start()

        @pl.loop(0, n_pairs)
        def _(p):
            c0 = 2 * p
            for cp in loads(c0 + 1, 1):
                cp.start()
            for cp in loads(c0, 0):
                cp.wait()
            reduce_window(0)
            write(c0, 0).start()
            for cp in loads(c0 + 1, 1):
                cp.wait()
            reduce_window(1)
            write(c0 + 1, 1).start()
            write(c0, 0).wait()

            @pl.when(p + 1 < n_pairs)
            def _():
                for cp in loads(c0 + 2, 0):
                    cp.start()

            write(c0 + 1, 1).wait()

    return k(rows, pos_tok, w_lanes)


def _gmm_kernel(tot_ref, ce_ref, row_ref, val_ref, ord_ref, nxt_ref, x_hbm, wg_hbm, wu_hbm, wd_hbm, y_hbm,
                wgb, wub, wdb, xbuf, xsem, ybuf, ysem, wgf, wuf, wdf, wsem):
    total = tot_ref[0]
    pieces = GMM_SUB // GMM_TAIL

    def w_copies(ex, slot):
        return [pltpu.make_async_copy(src.at[ex], dst.at[slot], wsem.at[slot, i])
                for i, (src, dst) in enumerate(((wg_hbm, wgf), (wu_hbm, wuf), (wd_hbm, wdf)))]

    def x_copy(g):
        slot = g % GMM_X_SLOTS
        rows = pl.ds(pl.multiple_of(row_ref[g], GMM_TAIL), GMM_SUB)
        return pltpu.make_async_copy(x_hbm.at[rows], xbuf.at[slot], xsem.at[slot])

    def y_piece(g, p):
        slot = g % 2
        rows = pl.ds(pl.multiple_of(row_ref[g] + p * GMM_TAIL, GMM_TAIL), GMM_TAIL)
        return pltpu.make_async_copy(ybuf.at[slot, pl.ds(p * GMM_TAIL, GMM_TAIL)], y_hbm.at[rows], ysem.at[slot])

    def for_y_pieces(g, action):
        for p in range(pieces):
            @pl.when(p * GMM_TAIL < val_ref[g])
            def _():
                action(y_piece(g, p))

    for ahead in range(GMM_X_SLOTS - 1):
        @pl.when(ahead < total)
        def _():
            x_copy(ahead).start()

    def chunk(g, carry):
        e = ce_ref[g]
        prev = ce_ref[jnp.maximum(g - 1, 0)]

        @pl.when((g == 0) | (e != prev))
        def _():
            slot = ord_ref[e] % GMM_W_SLOTS

            def start_ahead(ex, hops):
                for _ in range(hops):
                    ex = jnp.where(ex >= 0, nxt_ref[jnp.maximum(ex, 0)], -1)

                @pl.when(ex >= 0)
                def _():
                    for cp in w_copies(ex, ord_ref[ex] % GMM_W_SLOTS):
                        cp.start()

            @pl.when(g == 0)
            def _():
                for hops in range(GMM_W_SLOTS - 1):
                    start_ahead(e, hops)

            for cp in w_copies(e, slot):
                cp.wait()
            wgb[...] = wgf[slot].astype(BF16)
            wub[...] = wuf[slot].astype(BF16)
            wdb[...] = wdf[slot].astype(BF16)
            start_ahead(e, GMM_W_SLOTS - 1)

        x_copy(g).wait()

        @pl.when(g + GMM_X_SLOTS - 1 < total)
        def _():
            x_copy(g + GMM_X_SLOTS - 1).start()

        @pl.when(g >= 2)
        def _():
            for_y_pieces(g - 2, lambda cp: cp.wait())

        x_ref = xbuf.at[g % GMM_X_SLOTS]
        y_ref = ybuf.at[g % 2]

        def expert_rows(r0, n):
            rows = pl.ds(r0, n)
            lo, hi = _unpack_bf16_pairs(x_ref[rows, :])
            xb = jnp.concatenate([lo.astype(BF16), hi.astype(BF16)], axis=1)
            mid = (_silu(_dot(xb, wgb[...])) * _dot(xb, wub[...])).astype(BF16)
            y_ref[rows, :] = _pack_bf16_pairs(_dot(mid, wdb[...]))

        n_real = val_ref[g]

        @pl.when(n_real == GMM_SUB)
        def _():
            expert_rows(0, GMM_SUB)

        @pl.when((n_real < GMM_SUB) & (n_real >= GMM_MID))
        def _():
            expert_rows(0, GMM_MID)

        @pl.when(n_real < GMM_SUB)
        def _():
            done = jnp.where(n_real >= GMM_MID, GMM_MID, 0)

            @pl.loop(0, (n_real - done + GMM_TAIL - 1) // GMM_TAIL)
            def _(i):
                expert_rows(pl.multiple_of(done + i * GMM_TAIL, GMM_TAIL), GMM_TAIL)

        for_y_pieces(g, lambda cp: cp.start())
        return carry

    lax.fori_loop(0, total, chunk, 0)
    for back in (2, 1):
        @pl.when(total >= back)
        def _():
            for_y_pieces(total - back, lambda cp: cp.wait())


def _gmm(x_sorted, n_chunks, chunk_e, chunk_row, chunk_valid, e_ord, e_next, wg, wu, wd):
    r, half = x_sorted.shape
    d = 2 * half
    any_spec = pl.BlockSpec(memory_space=pl.ANY)
    return pl.pallas_call(
        _gmm_kernel,
        grid_spec=pltpu.PrefetchScalarGridSpec(
            num_scalar_prefetch=6,
            grid=(1,),
            in_specs=[any_spec, any_spec, any_spec, any_spec],
            out_specs=any_spec,
            scratch_shapes=[pltpu.VMEM((d, D_EXPERT), BF16), pltpu.VMEM((d, D_EXPERT), BF16),
                            pltpu.VMEM((D_EXPERT, d), BF16),
                            pltpu.VMEM((GMM_X_SLOTS, GMM_SUB, half), I32), pltpu.SemaphoreType.DMA((GMM_X_SLOTS,)),
                            pltpu.VMEM((2, GMM_SUB, half), I32), pltpu.SemaphoreType.DMA((2,)),
                            pltpu.VMEM((GMM_W_SLOTS, d, D_EXPERT), F32), pltpu.VMEM((GMM_W_SLOTS, d, D_EXPERT), F32),
                            pltpu.VMEM((GMM_W_SLOTS, D_EXPERT, d), F32), pltpu.SemaphoreType.DMA((GMM_W_SLOTS, 3))]),
        out_shape=jax.ShapeDtypeStruct((r, half), I32),
        compiler_params=pltpu.CompilerParams(dimension_semantics=("arbitrary",), vmem_limit_bytes=VMEM_LIMIT),
        name="expert_gmm",
    )(n_chunks, chunk_e, chunk_row, chunk_valid, e_ord, e_next, x_sorted, wg, wu, wd)


def _combine_kernel(base_ref, mod_ref, routed_ref, gain_ref, *rest):
    y_ref = rest[-1]
    d = base_ref.shape[-1]
    g2 = mod_ref[...][:, :, 2 * d:3 * d]
    out = base_ref[...] + g2 * routed_ref[...]
    y_ref[...] = _rms(out) * gain_ref[...]


def _combine(base, mod, routed, gain, first_chunk, n_chunks, out_chunks, out_first_chunk, out_buf=None):
    _, c, d = base.shape
    nc = COMB_TILE // c
    t0, o0 = first_chunk // nc, out_first_chunk // nc
    chunks_per_stream = out_chunks // mod.shape[0]
    if chunks_per_stream == 1:
        mod_spec = pl.BlockSpec((nc, 1, 3 * d), lambda i: (o0 + i, 0, 0))
    else:
        assert chunks_per_stream % nc == 0
        mod_spec = pl.BlockSpec((1, 1, 3 * d), lambda i: ((o0 + i) * nc // chunks_per_stream, 0, 0))
    blk3 = pl.BlockSpec((nc, c, d), lambda i: (t0 + i, 0, 0))
    in_specs = [blk3, mod_spec, blk3, pl.BlockSpec((1, 1, d), lambda i: (0, 0, 0))]
    args = [base, mod, routed, gain.reshape(1, 1, d)]
    aliases = {}
    if out_buf is not None:
        in_specs.append(pl.BlockSpec(memory_space=pl.ANY))
        args.append(out_buf)
        aliases = {len(args) - 1: 0}
    return pl.pallas_call(
        _combine_kernel,
        grid=(n_chunks // nc,),
        in_specs=in_specs,
        out_specs=pl.BlockSpec((nc, c, d), lambda i: (o0 + i, 0, 0)),
        out_shape=jax.ShapeDtypeStruct((out_chunks, c, d), F32),
        input_output_aliases=aliases,
        compiler_params=pltpu.CompilerParams(dimension_semantics=("arbitrary",), vmem_limit_bytes=VMEM_LIMIT),
        name="combine_norm",
    )(*args)


def _rope_tables(pos):
    half = HEAD_DIM // 2
    lane = jnp.arange(LANES, dtype=I32)
    inv_freq = ROPE_THETA ** (-(lane % half).astype(F32) / half)
    sign = jnp.where((lane % HEAD_DIM) < half, -1.0, 1.0).astype(F32)
    ang = pos.astype(F32)[:, None] * inv_freq[None, :]
    return jnp.cos(ang), jnp.sin(ang) * sign[None, :]


def _routed_ffn(h2, cw, rank, counts, w_gate, w_up, w_down):
    n, half = h2.shape
    counts = counts[:, 0]
    padded = (counts + GMM_TAIL - 1) // GMM_TAIL * GMM_TAIL
    seg_start = (jnp.cumsum(padded) - padded).astype(I32)
    n_rows = n * TOP_K + N_EXPERTS * GMM_TAIL + GMM_SUB
    e_chunks = (counts + GMM_SUB - 1) // GMM_SUB
    chunk_end = jnp.cumsum(e_chunks)
    max_chunks = n * TOP_K // GMM_SUB + N_EXPERTS
    g = jnp.arange(max_chunks, dtype=I32)
    chunk_e = jnp.minimum(jnp.sum((chunk_end[None, :] <= g[:, None]).astype(I32), axis=1), N_EXPERTS - 1)
    eids = jnp.arange(N_EXPERTS, dtype=I32)
    own = chunk_e[:, None] == eids[None, :]
    pick = lambda table: jnp.sum(jnp.where(own, table[None, :], 0), axis=1)
    in_expert = (g - pick(chunk_end - e_chunks)) * GMM_SUB
    chunk_row = (pick(seg_start) + in_expert).astype(I32)
    chunk_valid = jnp.clip(pick(counts) - in_expert, 0, GMM_SUB).astype(I32)
    n_chunks = chunk_end[-1:].astype(I32)
    has_rows = counts > 0
    e_ord = (jnp.cumsum(has_rows.astype(I32)) - has_rows.astype(I32)).astype(I32)
    later = has_rows[None, :] & (eids[None, :] > eids[:, None])
    e_next = jnp.min(jnp.where(later, eids[None, :], N_EXPERTS), axis=1)
    e_next = jnp.where(e_next == N_EXPERTS, -1, e_next).astype(I32)
    pos, w_lanes, pos_tok = _slots(cw, rank, seg_start[:, None])
    x_sorted = _sc_dispatch(h2, pos, n_rows)
    y_sorted = _gmm(x_sorted, n_chunks, chunk_e, chunk_row, chunk_valid, e_ord, e_next, w_gate, w_up, w_down)
    return _sc_collect_sum(y_sorted, pos_tok.reshape(n * TOP_K), w_lanes)


def kernel(x_prompt, x_sample, cache_k, cache_v, state_conv, c_prompt, c_sample, w_ada, b_ada, w_in, w_conv,
           w_conv_out, w_attn_o, attn_sinks, w_mix_out, w_router, router_bias, w_exp_gate, w_exp_up, w_exp_down,
           w_sh_gate, w_sh_up, w_sh_down, final_gain):
    assert w_ada.shape[0] == 1, "one layer"
    bp, seq, d = x_prompt.shape
    bs, ts, _ = x_sample.shape
    assert ts == CHUNK and seq % MIX_TILE == 0 and bs % SAMPLE_BB == 0

    c_all = jnp.concatenate([c_prompt, c_sample], axis=0)
    pad = (-c_all.shape[0]) % SUBLANES
    mod = _ada(jnp.pad(c_all, ((0, pad), (0, 0))), w_ada[0], b_ada[0])[:bp + bs]
    mod_p, mod_s = mod[:bp, None, :], mod[bp:, None, :]

    head_axes = (N_KV_HEADS // 2, 2, GQA_GROUP, HEAD_DIM)
    w_in_l = w_in[0]
    w_q = w_in_l[:, OFF_Q:OFF_K].reshape((d,) + head_axes).transpose(0, 1, 3, 2, 4).reshape(d, Q_DIM)
    w_o = w_attn_o[0].reshape(head_axes + (d,)).transpose(0, 2, 1, 3, 4).reshape(Q_DIM, d)
    win = (w_in_l.astype(BF16), w_q.astype(BF16))
    wco, wao, wmo = (w.astype(BF16) for w in (w_conv_out[0], w_o, w_mix_out[0]))
    cos_p, sin_p = _rope_tables(jnp.arange(seq, dtype=I32))
    cos_s, sin_s = _rope_tables(PAST_LEN + jnp.arange(ts, dtype=I32))

    x1_p, conv_p, k_p, v_p = _mixer_prompt(x_prompt, mod_p, cos_p, sin_p, win, w_conv[0], wco, wao, attn_sinks[0], wmo)
    x1_s, conv_s, k_s, v_s = _mixer_sample(
        x_sample, mod_s, cos_s, sin_s, cache_k[0].reshape(bs, WINDOW, KV_DIM), cache_v[0].reshape(bs, WINDOW, KV_DIM),
        state_conv[0], win, w_conv[0], wco, wao, attn_sinks[0], wmo)

    n_p, n_s = bp * seq, bs * ts
    n = n_p + n_s
    mod2_p, mod2_s = mod[:bp, None, 3 * d:], mod[bp:, None, 3 * d:]
    x1_pc = x1_p.reshape(n_p // CHUNK, CHUNK, d)
    wsg, wsu, wsd = (w[0].astype(BF16) for w in (w_sh_gate, w_sh_up, w_sh_down))
    wr_t, rb = w_router[0].T, router_bias[0][:, None]
    wr_hi = wr_t.astype(BF16)
    wr_lo = (wr_t - wr_hi.astype(F32)).astype(BF16)

    ncp, ncs = n_p // CHUNK, n_s // CHUNK
    half = (ncp + ncs) * FFN_SET_A_SHARE[0] // FFN_SET_A_SHARE[1]
    tile_chunks = max(PRE_TILE, COMB_TILE) // CHUNK
    assert half <= ncp and half % tile_chunks == 0 and (ncp - half) % tile_chunks == 0 and ncs % tile_chunks == 0
    for set_tokens in (half * CHUNK, n - half * CHUNK):
        assert set_tokens % (SC_WORKERS * SC_WINDOW) == 0 and set_tokens % RANK_TILE == 0
        assert set_tokens % (SC_WORKERS * 2 * SC_SUM_TOKENS) == 0
    y_p = None
    for p0, np_c, ns_c in ((0, half, 0), (half, ncp - half, ncs)):
        h2, base, cw, rank, counts = _pre(x1_pc, p0, np_c, mod2_p, x1_s, ns_c, mod2_s, wsg, wsu, wsd, wr_hi, wr_lo, rb)
        routed = _routed_ffn(h2, cw, rank, counts, w_exp_gate[0], w_exp_up[0], w_exp_down[0]).reshape(base.shape)
        y_p = _combine(base, mod2_p, routed, final_gain, 0, np_c, ncp, p0, out_buf=y_p)
        if ns_c:
            y_s = _combine(base, mod2_s, routed, final_gain, np_c, ns_c, ncs, 0)

    kv = lambda a: a.reshape(1, a.shape[0], WINDOW, N_KV_HEADS, HEAD_DIM)
    return (y_p.reshape(bp, seq, d), y_s, conv_p[None], kv(k_p), kv(v_p), conv_s[None], kv(k_s), kv(v_s))
```

```python
import functools
import itertools

import jax
import jax.numpy as jnp
from jax import lax
from jax.experimental import pallas as pl
from jax.experimental.pallas import tpu as pltpu
from jax.experimental.pallas import tpu_sc as plsc

F32 = jnp.float32
BF16 = jnp.bfloat16
I32 = jnp.int32

D_MODEL = 1024
CHUNK = 64
D_CONV = 1024
CONV_W = 3
N_HEADS = 16
N_KV_HEADS = 4
HEAD_DIM = 64
GQA_GROUP = N_HEADS // N_KV_HEADS
WINDOW = 128
ROPE_THETA = 10000.0
ATTN_SCALE = HEAD_DIM ** -0.5
N_EXPERTS = 64
TOP_K = 8
N_EXPERT_GROUPS = 8
GROUP_SIZE = N_EXPERTS // N_EXPERT_GROUPS
TOPK_GROUPS = 4
D_EXPERT = 256
D_SHARED = 256
ROUTED_SCALE = 2.5
EPS = 1e-6
PAST_LEN = 4096
Q_DIM = N_HEADS * HEAD_DIM
KV_DIM = N_KV_HEADS * HEAD_DIM
OFF_GB, OFF_GC, OFF_XC, OFF_Q, OFF_K, OFF_V, OFF_GCONV, OFF_GATTN, D_IN = itertools.accumulate(
    (0, D_CONV, D_CONV, D_CONV, Q_DIM, KV_DIM, KV_DIM, D_MODEL, D_MODEL))

LANES = 128
SUBLANES = 8
VMEM_LIMIT = 56 * 1024 * 1024

MIX_TILE = 512
ATT_Q = 128
MIX_SIDE_COLS = 256
SAMPLE_BB = 8
PRE_TILE = 512
RANK_TILE = 512
GMM_SUB = 512
GMM_MID = 256
GMM_TAIL = 128
GMM_X_SLOTS = 4
GMM_W_DMA_PRIORITY = 1
GMM_W_SLOTS = 3
COMB_TILE = 512
FFN_SET_A_SHARE = (2, 3)
SC_SUBCORES = 16
SC_WORKERS = 32
SC_WINDOW = 96
SC_LANES = 16
SC_SUM_TOKENS = 8
SC_SUM_VREGS = 16


def _const_spec(shape):
    nd = len(shape)
    return pl.BlockSpec(shape, lambda *_: (0,) * nd, pipeline_mode=pl.Buffered(1))


def _rms(x):
    return x * lax.rsqrt(jnp.mean(x * x, axis=-1, keepdims=True) + EPS)


def _sigmoid(x):
    return 1.0 / (1.0 + jnp.exp(-x))


def _silu(x):
    return x * _sigmoid(x)


def _dot(a, b):
    return jnp.dot(a, b, preferred_element_type=F32)


def _pack_bf16_pairs(x):
    half = x.shape[-1] // 2
    lo = lax.bitcast_convert_type(x[..., :half].astype(BF16).astype(F32), I32)
    hi = lax.bitcast_convert_type(x[..., half:].astype(BF16).astype(F32), I32)
    return lax.shift_right_logical(lo, 16) | hi


def _unpack_bf16_pairs(words):
    lo = lax.bitcast_convert_type(lax.shift_left(words, 16), F32)
    hi = lax.bitcast_convert_type(words & jnp.int32(-65536), F32)
    return lo, hi


def _ada_kernel(c_ref, w_ref, b_ref, o_ref):
    s = _silu(c_ref[...]).astype(BF16)
    o_ref[...] = _dot(s, w_ref[...].astype(BF16)) + b_ref[...]


def _ada(c_all, w_ada, b_ada):
    rows = c_all.shape[0]
    n_out = w_ada.shape[1]
    bn = 768
    return pl.pallas_call(
        _ada_kernel,
        grid=(n_out // bn,),
        in_specs=[pl.BlockSpec((rows, D_MODEL), lambda i: (0, 0)),
                  pl.BlockSpec((D_MODEL, bn), lambda i: (0, i)),
                  pl.BlockSpec((1, bn), lambda i: (0, i))],
        out_specs=pl.BlockSpec((rows, bn), lambda i: (0, i)),
        out_shape=jax.ShapeDtypeStruct((rows, n_out), F32),
        name="ada_mod",
    )(c_all, w_ada, b_ada.reshape(1, n_out))


def _rope(x, cos, sin_signed):
    lane = lax.broadcasted_iota(I32, (x.shape[0], LANES), 1)
    first_half = (lane % HEAD_DIM) < (HEAD_DIM // 2)
    outs = []
    for g in range(x.shape[1] // LANES):
        xg = x[:, g * LANES:(g + 1) * LANES]
        up = pltpu.roll(xg, LANES - HEAD_DIM // 2, axis=1)
        down = pltpu.roll(xg, HEAD_DIM // 2, axis=1)
        partner = jnp.where(first_half, up, down)
        outs.append(xg * cos + partner * sin_signed)
    return jnp.concatenate(outs, axis=1)


def _attention(blocks, sinks_ref, obuf, between):
    rq = GQA_GROUP * ATT_Q
    low = lax.broadcasted_iota(I32, (ATT_Q, LANES), 1) < HEAD_DIM
    head_of_lane = lax.broadcasted_iota(I32, (1, rq), 1) // ATT_Q
    units = [(b, pair, par) for b in range(len(blocks)) for pair in range(N_KV_HEADS // 2) for par in range(2)]
    loaded = {}

    def scores(u):
        b, pair, par = units[u]
        if b not in loaded:
            loaded.clear()
            loaded[b] = blocks[b]()
        q_blk, k_of_pair, _, mask, _ = loaded[b]
        keep = low if par == 0 else jnp.logical_not(low)
        cols = [q_blk[:, (GQA_GROUP * pair + i) * LANES:(GQA_GROUP * pair + i + 1) * LANES] for i in range(GQA_GROUP)]
        qg = jnp.concatenate([jnp.where(keep, c, jnp.zeros_like(c)) for c in cols], axis=0)
        st = lax.dot_general(k_of_pair(pair), qg, (((1,), (1,)), ((), ())), preferred_element_type=F32)
        vt = loaded[b][2](pair)[par * HEAD_DIM:(par + 1) * HEAD_DIM, :]
        return jnp.where(mask, st, -jnp.inf), vt, loaded[b][4]

    outs = []
    nxt = scores(0)
    for u, (b, pair, par) in enumerate(units):
        st, vt, row0 = nxt
        if u + 1 < len(units):
            nxt = scores(u + 1)
        if between:
            between.pop(0)()
        g = 2 * pair + par
        sink = jnp.full((1, rq), sinks_ref[g * GQA_GROUP + GQA_GROUP - 1], F32)
        for i in range(GQA_GROUP - 2, -1, -1):
            sink = jnp.where(head_of_lane == i, sinks_ref[g * GQA_GROUP + i], sink)
        m = jnp.maximum(jnp.max(st, axis=0, keepdims=True), sink)
        e = jnp.exp(st - m)
        z = jnp.sum(e, axis=0, keepdims=True) + jnp.exp(sink - m)
        outs.append(_dot(vt, e.astype(BF16)) / z)
        if par == 1:
            for i in range(GQA_GROUP):
                blk = jnp.concatenate([o[:, i * ATT_Q:(i + 1) * ATT_Q] for o in outs], axis=0)
                c0 = (GQA_GROUP * pair + i) * LANES
                obuf[row0:row0 + ATT_Q, c0:c0 + LANES] = blk.T
            outs = []
    for step in between:
        step()


def _in_proj(hb, win_refs, lo, hi):
    w_all, w_q = win_refs
    if (lo, hi) == (OFF_Q, OFF_K):
        return _dot(hb, w_q[...])
    assert hi <= OFF_Q or lo >= OFF_K
    return _dot(hb, w_all[:, lo:hi])


def _attention_free_steps(hb, conv, win_ref, wco_ref):
    out, parts = {}, {}
    n_parts = D_MODEL // MIX_SIDE_COLS

    def step(name, piece, compute):
        def run():
            parts.setdefault(name, []).append(compute(piece * MIX_SIDE_COLS, (piece + 1) * MIX_SIDE_COLS))
            if piece == n_parts - 1:
                out[name] = jnp.concatenate(parts.pop(name), axis=1)
        return run

    def conv_in():
        if "conv_in" not in out:
            out["conv_in"] = (out.pop("gate_b") * conv).astype(BF16)
        return out["conv_in"]

    computes = [("gate_b", lambda lo, hi: _in_proj(hb, win_ref, OFF_GB + lo, OFF_GB + hi)),
                ("g_conv", lambda lo, hi: _in_proj(hb, win_ref, OFF_GCONV + lo, OFF_GCONV + hi)),
                ("g_attn", lambda lo, hi: _in_proj(hb, win_ref, OFF_GATTN + lo, OFF_GATTN + hi)),
                ("y_conv", lambda lo, hi: _dot(conv_in(), wco_ref[:, lo:hi]))]
    return [step(name, p, fn) for name, fn in computes for p in range(n_parts)], out


def _mix_out(x, g1, side, y_attn_in, wao_ref, wmo_ref):
    y_attn = _dot(y_attn_in.astype(BF16), wao_ref[...])
    merged = _sigmoid(side["g_conv"]) * side["y_conv"] + _sigmoid(side["g_attn"]) * y_attn
    return x + g1 * _dot(merged.astype(BF16), wmo_ref[...])


def _mixer_prompt_kernel(x_ref, mod_ref, cos_ref, sin_ref, wall_ref, wq_ref, wconv_ref, wco_ref, wao_ref,
                         sinks_ref, wmo_ref, x1_ref, conv_ref, k_ref, v_ref, ubuf, kbuf, vtbuf, obuf):
    win_ref = (wall_ref, wq_ref)
    j = pl.program_id(1)
    t = x_ref.shape[1]

    @pl.when(j == 0)
    def _():
        ubuf[0:SUBLANES, :] = jnp.zeros((SUBLANES, D_CONV), F32)
        kbuf[0:WINDOW, :] = jnp.zeros((WINDOW, KV_DIM), BF16)
        vtbuf[:, 0:WINDOW] = jnp.zeros((KV_DIM, WINDOW), BF16)

    x = x_ref[0]
    mod = mod_ref[0]
    sh1, sc1, g1 = mod[:, 0:D_MODEL], mod[:, D_MODEL:2 * D_MODEL], mod[:, 2 * D_MODEL:3 * D_MODEL]
    hb = (_rms(x) * (1.0 + sc1) + sh1).astype(BF16)

    u = _in_proj(hb, win_ref, OFF_GC, OFF_XC) * _in_proj(hb, win_ref, OFF_XC, OFF_Q)
    ubuf[SUBLANES:SUBLANES + t, :] = u
    wc = wconv_ref[...]
    conv = wc[0:1] * ubuf[SUBLANES - 2:SUBLANES - 2 + t, :] + wc[1:2] * ubuf[SUBLANES - 1:SUBLANES - 1 + t, :] + wc[2:3] * u
    conv_ref[0] = u[t - (CONV_W - 1):t]
    ubuf[SUBLANES - 2:SUBLANES, :] = u[t - (CONV_W - 1):t]

    cos, sin = cos_ref[...], sin_ref[...]
    q = (_rope(_in_proj(hb, win_ref, OFF_Q, OFF_K), cos, sin) * ATTN_SCALE).astype(BF16)
    k = _rope(_in_proj(hb, win_ref, OFF_K, OFF_V), cos, sin)
    v = _in_proj(hb, win_ref, OFF_V, OFF_GCONV)
    kbuf[WINDOW:WINDOW + t, :] = k.astype(BF16)
    vtbuf[:, WINDOW:WINDOW + t] = v.T.astype(BF16)
    k_ref[0] = k[t - WINDOW:t]
    v_ref[0] = v[t - WINDOW:t]

    nkeys = ATT_Q + WINDOW
    rq = GQA_GROUP * ATT_Q
    ki = lax.broadcasted_iota(I32, (nkeys, rq), 0)
    qi = lax.broadcasted_iota(I32, (nkeys, rq), 1) % ATT_Q
    band = ki // CHUNK - qi // CHUNK
    band_ok = (band >= 0) & (band <= WINDOW // CHUNK)
    def block(s):
        def load():
            mask = band_ok & (ki + (j * t + s * ATT_Q - WINDOW) >= 0)
            k_of_pair = lambda pair: kbuf[s * ATT_Q:s * ATT_Q + nkeys, pair * LANES:(pair + 1) * LANES]
            vt_of_pair = lambda pair: vtbuf[pair * LANES:(pair + 1) * LANES, s * ATT_Q:s * ATT_Q + nkeys]
            return q[s * ATT_Q:(s + 1) * ATT_Q], k_of_pair, vt_of_pair, mask, s * ATT_Q
        return load

    steps, side = _attention_free_steps(hb, conv, win_ref, wco_ref)
    _attention([block(s) for s in range(t // ATT_Q)], sinks_ref, obuf, steps)
    kbuf[0:WINDOW, :] = kbuf[t:t + WINDOW, :]
    vtbuf[:, 0:WINDOW] = vtbuf[:, t:t + WINDOW]

    x1_ref[0] = _mix_out(x, g1, side, obuf[...], wao_ref, wmo_ref)


def _mixer_prompt(x, mod, cos, sin, win, wconv, wco, wao, sinks, wmo):
    b, seq, d = x.shape
    t = MIX_TILE
    return pl.pallas_call(
        _mixer_prompt_kernel,
        grid=(b, seq // t),
        in_specs=[pl.BlockSpec((1, t, d), lambda i, j: (i, j, 0)),
                  pl.BlockSpec((1, 1, 6 * d), lambda i, j: (i, 0, 0)),
                  pl.BlockSpec((t, LANES), lambda i, j: (j, 0)),
                  pl.BlockSpec((t, LANES), lambda i, j: (j, 0)),
                  *[_const_spec(w.shape) for w in win],
                  _const_spec(wconv.shape), _const_spec(wco.shape), _const_spec(wao.shape),
                  pl.BlockSpec(memory_space=pltpu.SMEM),
                  _const_spec(wmo.shape)],
        out_specs=[pl.BlockSpec((1, t, d), lambda i, j: (i, j, 0)),
                   pl.BlockSpec((1, CONV_W - 1, D_CONV), lambda i, j: (i, 0, 0)),
                   pl.BlockSpec((1, WINDOW, KV_DIM), lambda i, j: (i, 0, 0)),
                   pl.BlockSpec((1, WINDOW, KV_DIM), lambda i, j: (i, 0, 0))],
        out_shape=[jax.ShapeDtypeStruct((b, seq, d), F32),
                   jax.ShapeDtypeStruct((b, CONV_W - 1, D_CONV), F32),
                   jax.ShapeDtypeStruct((b, WINDOW, KV_DIM), F32),
                   jax.ShapeDtypeStruct((b, WINDOW, KV_DIM), F32)],
        scratch_shapes=[pltpu.VMEM((SUBLANES + t, D_CONV), F32),
                        pltpu.VMEM((WINDOW + t, KV_DIM), BF16),
                        pltpu.VMEM((KV_DIM, WINDOW + t), BF16),
                        pltpu.VMEM((t, Q_DIM), F32)],
        compiler_params=pltpu.CompilerParams(dimension_semantics=("arbitrary", "arbitrary"),
                                             vmem_limit_bytes=VMEM_LIMIT),
        name="mixer_prompt",
    )(x, mod, cos, sin, *win, wconv, wco, wao, sinks, wmo)


def _mixer_sample_kernel(x_ref, mod_ref, cos_ref, sin_ref, ck_ref, cv_ref, sconv_ref, wall_ref, wq_ref,
                         wconv_ref, wco_ref, wao_ref, sinks_ref, wmo_ref, x1_ref, conv_ref, k_ref, v_ref, ubuf, obuf):
    win_ref = (wall_ref, wq_ref)
    bb, t, d = x_ref.shape
    x3 = x_ref[...]
    mod = mod_ref[...]
    sh1, sc1, g1 = mod[:, :, 0:d], mod[:, :, d:2 * d], mod[:, :, 2 * d:3 * d]
    x = x3.reshape(bb * t, d)
    hb = (_rms(x3) * (1.0 + sc1) + sh1).astype(BF16).reshape(bb * t, d)

    u = _in_proj(hb, win_ref, OFF_GC, OFF_XC) * _in_proj(hb, win_ref, OFF_XC, OFF_Q)
    u3 = u.reshape(bb, t, D_CONV)
    ubuf[:, SUBLANES - 2:SUBLANES, :] = sconv_ref[...]
    ubuf[:, SUBLANES:SUBLANES + t, :] = u3
    wc = wconv_ref[...]
    conv = (wc[0:1] * ubuf[:, SUBLANES - 2:SUBLANES - 2 + t, :] + wc[1:2] * ubuf[:, SUBLANES - 1:SUBLANES - 1 + t, :]
            + wc[2:3] * u3).reshape(bb * t, D_CONV)
    conv_ref[...] = u3[:, t - (CONV_W - 1):t, :]

    cos = jnp.concatenate([cos_ref[...]] * bb, axis=0)
    sin = jnp.concatenate([sin_ref[...]] * bb, axis=0)
    q = (_rope(_in_proj(hb, win_ref, OFF_Q, OFF_K), cos, sin) * ATTN_SCALE).astype(BF16)
    k = _rope(_in_proj(hb, win_ref, OFF_K, OFF_V), cos, sin)
    v = _in_proj(hb, win_ref, OFF_V, OFF_GCONV)
    per = ATT_Q // t
    nkeys = per * (WINDOW + t)
    rq = GQA_GROUP * ATT_Q
    key_stream = lax.broadcasted_iota(I32, (nkeys, rq), 0) // (WINDOW + t)
    query_stream = (lax.broadcasted_iota(I32, (nkeys, rq), 1) % ATT_Q) // t
    mask = key_stream == query_stream
    def block(blk):
        def load():
            k_parts, v_parts = [], []
            for b in range(blk * per, (blk + 1) * per):
                kb, vb = k[b * t:(b + 1) * t], v[b * t:(b + 1) * t]
                ck, cv = ck_ref[b], cv_ref[b]
                k_ref[b] = jnp.concatenate([ck[t:WINDOW], kb], axis=0)
                v_ref[b] = jnp.concatenate([cv[t:WINDOW], vb], axis=0)
                k_parts += [ck, kb]
                v_parts += [cv, vb]
            k_all = jnp.concatenate(k_parts, axis=0).astype(BF16)
            vt_all = jnp.concatenate(v_parts, axis=0).T.astype(BF16)
            k_of_pair = lambda pair: k_all[:, pair * LANES:(pair + 1) * LANES]
            vt_of_pair = lambda pair: vt_all[pair * LANES:(pair + 1) * LANES, :]
            return q[blk * ATT_Q:(blk + 1) * ATT_Q], k_of_pair, vt_of_pair, mask, blk * ATT_Q
        return load

    steps, side = _attention_free_steps(hb, conv, win_ref, wco_ref)
    _attention([block(blk) for blk in range(bb // per)], sinks_ref, obuf, steps)

    g1f = jnp.broadcast_to(g1, (bb, t, d)).reshape(bb * t, d)
    x1_ref[...] = _mix_out(x, g1f, side, obuf[...], wao_ref, wmo_ref).reshape(bb, t, d)


def _mixer_sample(x, mod, cos, sin, ck, cv, sconv, win, wconv, wco, wao, sinks, wmo):
    b, t, d = x.shape
    bb = SAMPLE_BB
    blk = lambda *s: pl.BlockSpec((bb,) + s, lambda i: (i, 0, 0))
    return pl.pallas_call(
        _mixer_sample_kernel,
        grid=(b // bb,),
        in_specs=[blk(t, d), blk(1, 6 * d),
                  pl.BlockSpec((t, LANES), lambda i: (0, 0)), pl.BlockSpec((t, LANES), lambda i: (0, 0)),
                  blk(WINDOW, KV_DIM), blk(WINDOW, KV_DIM), blk(CONV_W - 1, D_CONV),
                  *[_const_spec(w.shape) for w in win],
                  _const_spec(wconv.shape), _const_spec(wco.shape), _const_spec(wao.shape),
                  pl.BlockSpec(memory_space=pltpu.SMEM),
                  _const_spec(wmo.shape)],
        out_specs=[blk(t, d), blk(CONV_W - 1, D_CONV), blk(WINDOW, KV_DIM), blk(WINDOW, KV_DIM)],
        out_shape=[jax.ShapeDtypeStruct((b, t, d), F32),
                   jax.ShapeDtypeStruct((b, CONV_W - 1, D_CONV), F32),
                   jax.ShapeDtypeStruct((b, WINDOW, KV_DIM), F32),
                   jax.ShapeDtypeStruct((b, WINDOW, KV_DIM), F32)],
        scratch_shapes=[pltpu.VMEM((bb, SUBLANES + t, D_CONV), F32),
                        pltpu.VMEM((bb * t, Q_DIM), F32)],
        compiler_params=pltpu.CompilerParams(dimension_semantics=("arbitrary",), vmem_limit_bytes=VMEM_LIMIT),
        name="mixer_sample",
    )(x, mod, cos, sin, ck, cv, sconv, *win, wconv, wco, wao, sinks, wmo)


def _pre_kernel(*refs, prompt_tiles, has_sample):
    if has_sample:
        xp_ref, mp_ref, xs_ref, ms_ref, *refs = refs
    else:
        xp_ref, mp_ref, *refs = refs
    wsg_ref, wsu_ref, wsd_ref, wrh_ref, wrl_ref, rb_ref, h2_ref, base_ref, cw_ref, rank_ref, cnt_ref, carry = refs
    nc, c, d = xp_ref.shape
    t = nc * c
    x3, mod = xp_ref[...], mp_ref[...]
    if has_sample:
        is_prompt = pl.program_id(0) < prompt_tiles
        x3 = jnp.where(is_prompt, x3, xs_ref[...])
        mod = jnp.where(is_prompt, mod, ms_ref[...])
    sh2, sc2, g2 = mod[:, :, 0:d], mod[:, :, d:2 * d], mod[:, :, 2 * d:3 * d]
    h3 = _rms(x3) * (1.0 + sc2) + sh2
    h2 = h3.reshape(t, d)
    hb = h2.astype(BF16)
    h2_ref[...] = _pack_bf16_pairs(h2)
    shared = _dot((_silu(_dot(hb, wsg_ref[...])) * _dot(hb, wsu_ref[...])).astype(BF16), wsd_ref[...])
    base_ref[...] = x3 + g2 * shared.reshape(nc, c, d)

    h_lo = (h2 - hb.astype(F32)).astype(BF16)
    nt = lambda a, b: lax.dot_general(a, b, (((1,), (1,)), ((), ())), preferred_element_type=F32)
    logits = nt(wrh_ref[...], hb) + (nt(wrh_ref[...], h_lo) + nt(wrl_ref[...], hb))
    scores = _sigmoid(logits)
    biased = scores + rb_ref[...]
    g3 = biased.reshape(N_EXPERT_GROUPS, GROUP_SIZE, t)
    member = lax.broadcasted_iota(I32, g3.shape, 1)
    m1 = jnp.max(g3, axis=1, keepdims=True)
    first = jnp.min(jnp.where(g3 == m1, member, GROUP_SIZE), axis=1, keepdims=True)
    m2 = jnp.max(jnp.where(member == first, -jnp.inf, g3), axis=1, keepdims=True)
    gs = m1 + m2
    gidx = lax.broadcasted_iota(I32, gs.shape, 0)
    grank = jnp.zeros(gs.shape, I32)
    for o in range(N_EXPERT_GROUPS):
        other = gs[o:o + 1]
        grank += ((other > gs) | ((other == gs) & (o < gidx))).astype(I32)
    group_ok = grank < TOPK_GROUPS
    slot = jnp.zeros((1, 1, t), I32)
    takes = []
    for gi in range(N_EXPERT_GROUPS):
        ok = group_ok[gi:gi + 1]
        takes.append([ok & (slot == s) for s in range(TOPK_GROUPS)])
        slot = slot + ok.astype(I32)
    packed = []
    for s in range(TOPK_GROUPS):
        vals = jnp.zeros((GROUP_SIZE, t), F32)
        for gi in range(N_EXPERT_GROUPS):
            vals = jnp.where(takes[gi][s][0], g3[gi], vals)
        packed.append(vals)
    cand = jnp.concatenate(packed, axis=0)
    cidx = lax.broadcasted_iota(I32, cand.shape, 0)
    crank = jnp.zeros(cand.shape, I32)
    for o in range(TOPK_GROUPS * GROUP_SIZE):
        other = cand[o:o + 1]
        crank += ((other > cand) | ((other == cand) & (o < cidx))).astype(I32)
    chosen = crank < TOP_K
    sel_groups = []
    for gi in range(N_EXPERT_GROUPS):
        hit = jnp.zeros((GROUP_SIZE, t), jnp.bool_)
        for s in range(TOPK_GROUPS):
            hit = hit | (takes[gi][s][0] & chosen[s * GROUP_SIZE:(s + 1) * GROUP_SIZE])
        sel_groups.append(hit)
    sel = jnp.concatenate(sel_groups, axis=0)
    ssum = jnp.sum(jnp.where(sel, scores, 0.0), axis=0, keepdims=True)
    cw_ref[...] = jnp.where(sel, scores / ssum * ROUTED_SCALE, -1.0)

    @pl.when(pl.program_id(0) == 0)
    def _():
        carry[...] = jnp.zeros(carry.shape, F32)

    picked = sel.astype(BF16)
    earlier = (lax.broadcasted_iota(I32, (t, t), 0) < lax.broadcasted_iota(I32, (t, t), 1)).astype(BF16)
    rank_ref[...] = (carry[...] + _dot(picked, earlier)).astype(I32)
    carry[...] = carry[...] + jnp.sum(picked.astype(F32), axis=1, keepdims=True)
    cnt_ref[...] = carry[...].astype(I32)


def _pre(x1_p, p_chunk0, ncp, mod_p, x1_s, ncs, mod_s, wsg, wsu, wsd, wr_hi, wr_lo, rb):
    ncp_all, c, d = x1_p.shape
    nc = PRE_TILE // c
    nchunks = ncp + ncs
    n = nchunks * c
    pt, p0 = ncp // nc, p_chunk0 // nc
    tiles_per_stream = ncp_all // mod_p.shape[0] // nc
    blk3 = pl.BlockSpec((nc, c, d), lambda i: (i, 0, 0))
    p_tile = lambda i: p0 + jnp.minimum(i, pt - 1)
    s_tile = lambda i: jnp.maximum(i - pt, 0)
    s_args, s_specs = [], []
    if ncs:
        s_args = [x1_s, mod_s]
        s_specs = [pl.BlockSpec((nc, c, d), lambda i: (s_tile(i), 0, 0)),
                   pl.BlockSpec((nc, 1, 3 * d), lambda i: (s_tile(i), 0, 0))]
    return pl.pallas_call(
        functools.partial(_pre_kernel, prompt_tiles=pt, has_sample=bool(ncs)),
        grid=(nchunks // nc,),
        in_specs=[pl.BlockSpec((nc, c, d), lambda i: (p_tile(i), 0, 0)),
                  pl.BlockSpec((1, 1, 3 * d), lambda i: (p_tile(i) // tiles_per_stream, 0, 0))] + s_specs + [
                  _const_spec(wsg.shape), _const_spec(wsu.shape), _const_spec(wsd.shape),
                  _const_spec(wr_hi.shape), _const_spec(wr_lo.shape), _const_spec(rb.shape)],
        out_specs=[pl.BlockSpec((nc * c, d // 2), lambda i: (i, 0)), blk3,
                   pl.BlockSpec((N_EXPERTS, nc * c), lambda i: (0, i)),
                   pl.BlockSpec((N_EXPERTS, nc * c), lambda i: (0, i)),
                   pl.BlockSpec((N_EXPERTS, 1), lambda i: (0, 0))],
        out_shape=[jax.ShapeDtypeStruct((n, d // 2), I32),
                   jax.ShapeDtypeStruct((nchunks, c, d), F32),
                   jax.ShapeDtypeStruct((N_EXPERTS, n), F32),
                   jax.ShapeDtypeStruct((N_EXPERTS, n), I32),
                   jax.ShapeDtypeStruct((N_EXPERTS, 1), I32)],
        scratch_shapes=[pltpu.VMEM((N_EXPERTS, 1), F32)],
        compiler_params=pltpu.CompilerParams(dimension_semantics=("arbitrary",), vmem_limit_bytes=VMEM_LIMIT),
        name="pre_ffn",
    )(x1_p, mod_p, *s_args, wsg, wsu, wsd, wr_hi, wr_lo, rb)


def _slot_kernel(cw_ref, rank_ref, start_ref, pos_ref, w_ref, pos_tok_ref):
    cw = cw_ref[...]
    e, t = cw.shape
    sel = cw >= 0.0
    r = lax.broadcasted_iota(I32, (e, e), 0)
    c = lax.broadcasted_iota(I32, (e, e), 1)
    lower = (c < r).astype(BF16)
    kidx = _dot(lower, sel.astype(BF16))
    posf = start_ref[...].astype(F32) + rank_ref[...].astype(F32)
    pos_rows, w_rows = [], []
    for k in range(TOP_K):
        m = sel & (kidx == float(k))
        pos_rows.append(jnp.sum(jnp.where(m, posf, 0.0), axis=0, keepdims=True))
        w_rows.append(jnp.sum(jnp.where(m, cw, 0.0), axis=0, keepdims=True))
    pos_ref[...] = jnp.concatenate(pos_rows, axis=0).astype(I32)
    w_ref[...] = jnp.concatenate([jnp.broadcast_to(w, (SC_LANES, t)) for w in w_rows], axis=0).T
    pos_pad = jnp.concatenate(pos_rows + [jnp.zeros((LANES - TOP_K, t), F32)], axis=0)
    pos_tok_ref[...] = pos_pad.T[:, :TOP_K].astype(I32)


def _slots(cw, rank, seg_start):
    e, n = cw.shape
    t = RANK_TILE
    return pl.pallas_call(
        _slot_kernel,
        grid=(n // t,),
        in_specs=[pl.BlockSpec((e, t), lambda i: (0, i)), pl.BlockSpec((e, t), lambda i: (0, i)),
                  pl.BlockSpec((e, 1), lambda i: (0, 0))],
        out_specs=[pl.BlockSpec((TOP_K, t), lambda i: (0, i)), pl.BlockSpec((t, TOP_K * SC_LANES), lambda i: (i, 0)),
                   pl.BlockSpec((t, TOP_K), lambda i: (i, 0))],
        out_shape=[jax.ShapeDtypeStruct((TOP_K, n), I32), jax.ShapeDtypeStruct((n, TOP_K * SC_LANES), F32),
                   jax.ShapeDtypeStruct((n, TOP_K), I32)],
        compiler_params=pltpu.CompilerParams(dimension_semantics=("arbitrary",)),
        name="expert_slots",
    )(cw, rank, seg_start)


def _sc_mesh():
    return plsc.VectorSubcoreMesh(core_axis_name="c", subcore_axis_name="s")


def _sc_worker_id():
    return lax.axis_index("s") * (SC_WORKERS // SC_SUBCORES) + lax.axis_index("c")


def _sc_dispatch(rows, pos, n_rows):
    n, d = rows.shape
    per_w = n // SC_WORKERS
    w = SC_WINDOW
    n_chunks = per_w // w

    @functools.partial(
        pl.kernel, mesh=_sc_mesh(),
        out_type=jax.ShapeDtypeStruct((n_rows, d), rows.dtype),
        scratch_types=[pltpu.VMEM((2, TOP_K, w), I32), pltpu.VMEM((2, w, d), rows.dtype),
                       pltpu.SemaphoreType.DMA((2,)), pltpu.SemaphoreType.DMA((2,)), pltpu.SemaphoreType.DMA((2,))],
        name="sc_dispatch")
    def k(rows_hbm, pos_hbm, o_hbm, idx_v, rows_v, row_sem, idx_sem, out_sem):
        wid = _sc_worker_id()
        base = wid * per_w

        def loads(c, slot):
            off = pl.multiple_of(base + c * w, SUBLANES)
            return (pltpu.make_async_copy(rows_hbm.at[pl.ds(off, w)], rows_v.at[slot], row_sem.at[slot]),
                    pltpu.make_async_copy(pos_hbm.at[wid * n_chunks + c], idx_v.at[slot], idx_sem.at[slot]))

        def scatters(slot):
            return [pltpu.make_async_copy(rows_v.at[slot], o_hbm.at[idx_v.at[slot, kk]], out_sem.at[slot])
                    for kk in range(TOP_K)]

        for cp in loads(0, 0):
            cp.start()
        for c in range(n_chunks):
            slot = c % 2
            for cp in loads(c, slot):
                cp.wait()
            for cp in scatters(slot):
                cp.start()
            if c >= 1:
                for cp in scatters(1 - slot):
                    cp.wait()
            if c + 1 < n_chunks:
                for cp in loads(c + 1, 1 - slot):
                    cp.start()
        for cp in scatters((n_chunks - 1) % 2):
            cp.wait()

    pos_chunks = pos.reshape(TOP_K, n // w, w).transpose(1, 0, 2)
    return k(rows, pos_chunks)


def _sc_collect_sum(rows, pos_tok, w_lanes):
    words = rows.shape[1]
    n = w_lanes.shape[0]
    lanes = SC_LANES
    per_w = n // SC_WORKERS
    tw = SC_SUM_TOKENS
    n_pairs = per_w // (2 * tw)
    col_blocks = words // lanes // SC_SUM_VREGS

    @functools.partial(
        pl.kernel, mesh=_sc_mesh(),
        out_type=jax.ShapeDtypeStruct((n, 2 * words), F32),
        scratch_types=[pltpu.VMEM((per_w * TOP_K,), I32), pltpu.VMEM((2, tw * TOP_K, words), I32),
                       pltpu.VMEM((2, tw, TOP_K * lanes), F32), pltpu.VMEM((2, tw, 2 * words), F32),
                       pltpu.SemaphoreType.DMA((2,)), pltpu.SemaphoreType.DMA((2,)), pltpu.SemaphoreType.DMA((2,))],
        compiler_params=pltpu.CompilerParams(needs_layout_passes=False),
        name="sc_collect_sum")
    def k(rows_hbm, pos_hbm, w_hbm, o_hbm, idx_v, rows_v, w_v, out_v, in_sem, w_sem, out_sem):
        base = pl.multiple_of(_sc_worker_id() * per_w, SUBLANES)
        pltpu.sync_copy(pos_hbm.at[pl.ds(pl.multiple_of(base * TOP_K, SUBLANES), per_w * TOP_K)], idx_v)

        def loads(c, slot):
            idx = idx_v.at[pl.ds(pl.multiple_of(c * tw * TOP_K, SUBLANES), tw * TOP_K)]
            tok0 = pl.multiple_of(base + c * tw, SUBLANES)
            return (pltpu.make_async_copy(rows_hbm.at[idx], rows_v.at[slot], in_sem.at[slot]),
                    pltpu.make_async_copy(w_hbm.at[pl.ds(tok0, tw)], w_v.at[slot], w_sem.at[slot]))

        def write(c, slot):
            tok0 = pl.multiple_of(base + c * tw, SUBLANES)
            return pltpu.make_async_copy(out_v.at[slot], o_hbm.at[pl.ds(tok0, tw)], out_sem.at[slot])

        high_half = jnp.full((lanes,), -65536, I32)
        sixteen = jnp.full((lanes,), 16, I32)

        def reduce_window(slot):
            rv, wv, ov = rows_v.at[slot], w_v.at[slot], out_v.at[slot]

            @pl.loop(0, tw)
            def _(t):
                for cb in range(col_blocks):
                    acc_lo, acc_hi = [None] * SC_SUM_VREGS, [None] * SC_SUM_VREGS
                    for kk in range(TOP_K):
                        wk = wv[t, pl.ds(kk * lanes, lanes)]
                        for c in range(SC_SUM_VREGS):
                            wd = rv[t * TOP_K + kk, pl.ds((cb * SC_SUM_VREGS + c) * lanes, lanes)]
                            lo = wk * plsc.bitcast(lax.shift_left(wd, sixteen), F32)
                            hi = wk * plsc.bitcast(wd & high_half, F32)
                            acc_lo[c] = lo if kk == 0 else acc_lo[c] + lo
                            acc_hi[c] = hi if kk == 0 else acc_hi[c] + hi
                    for c in range(SC_SUM_VREGS):
                        col = (cb * SC_SUM_VREGS + c) * lanes
                        ov[t, pl.ds(col, lanes)] = acc_lo[c]
                        ov[t, pl.ds(words + col, lanes)] = acc_hi[c]

        for cp in loads(0, 0):
            cp.start()

        @pl.loop(0, n_pairs)
        def _(p):
            c0 = 2 * p
            for cp in loads(c0 + 1, 1):
                cp.start()
            for cp in loads(c0, 0):
                cp.wait()
            reduce_window(0)
            write(c0, 0).start()
            for cp in loads(c0 + 1, 1):
                cp.wait()
            reduce_window(1)
            write(c0 + 1, 1).start()
            write(c0, 0).wait()

            @pl.when(p + 1 < n_pairs)
            def _():
                for cp in loads(c0 + 2, 0):
                    cp.start()

            write(c0 + 1, 1).wait()

    return k(rows, pos_tok, w_lanes)


def _gmm_kernel(tot_ref, ce_ref, row_ref, val_ref, ord_ref, nxt_ref, x_hbm, wg_hbm, wu_hbm, wd_hbm, y_hbm,
                wgb, wub, wdb, xbuf, xsem, ybuf, ysem, wgf, wuf, wdf, wsem):
    total = tot_ref[0]
    pieces = GMM_SUB // GMM_TAIL

    def w_copies(ex, slot):
        return [pltpu.make_async_copy(src.at[ex], dst.at[slot], wsem.at[slot, i])
                for i, (src, dst) in enumerate(((wg_hbm, wgf), (wu_hbm, wuf), (wd_hbm, wdf)))]

    def x_copy(g):
        slot = g % GMM_X_SLOTS
        rows = pl.ds(pl.multiple_of(row_ref[g], GMM_TAIL), GMM_SUB)
        return pltpu.make_async_copy(x_hbm.at[rows], xbuf.at[slot], xsem.at[slot])

    def y_piece(g, p):
        slot = g % 2
        rows = pl.ds(pl.multiple_of(row_ref[g] + p * GMM_TAIL, GMM_TAIL), GMM_TAIL)
        return pltpu.make_async_copy(ybuf.at[slot, pl.ds(p * GMM_TAIL, GMM_TAIL)], y_hbm.at[rows], ysem.at[slot])

    def for_y_pieces(g, action):
        for p in range(pieces):
            @pl.when(p * GMM_TAIL < val_ref[g])
            def _():
                action(y_piece(g, p))

    for ahead in range(GMM_X_SLOTS - 1):
        @pl.when(ahead < total)
        def _():
            x_copy(ahead).start()

    def chunk(g, carry):
        e = ce_ref[g]
        prev = ce_ref[jnp.maximum(g - 1, 0)]

        @pl.when((g == 0) | (e != prev))
        def _():
            slot = ord_ref[e] % GMM_W_SLOTS

            def start_ahead(ex, hops):
                for _ in range(hops):
                    ex = jnp.where(ex >= 0, nxt_ref[jnp.maximum(ex, 0)], -1)

                @pl.when(ex >= 0)
                def _():
                    for cp in w_copies(ex, ord_ref[ex] % GMM_W_SLOTS):
                        cp.start(priority=GMM_W_DMA_PRIORITY)

            @pl.when(g == 0)
            def _():
                for hops in range(GMM_W_SLOTS - 1):
                    start_ahead(e, hops)

            for cp in w_copies(e, slot):
                cp.wait()
            wgb[...] = wgf[slot].astype(BF16)
            wub[...] = wuf[slot].astype(BF16)
            wdb[...] = wdf[slot].astype(BF16)
            start_ahead(e, GMM_W_SLOTS - 1)

        x_copy(g).wait()

        @pl.when(g + GMM_X_SLOTS - 1 < total)
        def _():
            x_copy(g + GMM_X_SLOTS - 1).start()

        @pl.when(g >= 2)
        def _():
            for_y_pieces(g - 2, lambda cp: cp.wait())

        x_ref = xbuf.at[g % GMM_X_SLOTS]
        y_ref = ybuf.at[g % 2]

        def expert_rows(r0, n):
            rows = pl.ds(r0, n)
            lo, hi = _unpack_bf16_pairs(x_ref[rows, :])
            xb = jnp.concatenate([lo.astype(BF16), hi.astype(BF16)], axis=1)
            mid = (_silu(_dot(xb, wgb[...])) * _dot(xb, wub[...])).astype(BF16)
            y_ref[rows, :] = _pack_bf16_pairs(_dot(mid, wdb[...]))

        n_real = val_ref[g]

        @pl.when(n_real == GMM_SUB)
        def _():
            expert_rows(0, GMM_SUB)

        @pl.when((n_real < GMM_SUB) & (n_real >= GMM_MID))
        def _():
            expert_rows(0, GMM_MID)

        @pl.when(n_real < GMM_SUB)
        def _():
            done = jnp.where(n_real >= GMM_MID, GMM_MID, 0)

            @pl.loop(0, (n_real - done + GMM_TAIL - 1) // GMM_TAIL)
            def _(i):
                expert_rows(pl.multiple_of(done + i * GMM_TAIL, GMM_TAIL), GMM_TAIL)

        for_y_pieces(g, lambda cp: cp.start())
        return carry

    lax.fori_loop(0, total, chunk, 0)
    for back in (2, 1):
        @pl.when(total >= back)
        def _():
            for_y_pieces(total - back, lambda cp: cp.wait())


def _gmm(x_sorted, n_chunks, chunk_e, chunk_row, chunk_valid, e_ord, e_next, wg, wu, wd):
    r, half = x_sorted.shape
    d = 2 * half
    any_spec = pl.BlockSpec(memory_space=pl.ANY)
    return pl.pallas_call(
        _gmm_kernel,
        grid_spec=pltpu.PrefetchScalarGridSpec(
            num_scalar_prefetch=6,
            grid=(1,),
            in_specs=[any_spec, any_spec, any_spec, any_spec],
            out_specs=any_spec,
            scratch_shapes=[pltpu.VMEM((d, D_EXPERT), BF16), pltpu.VMEM((d, D_EXPERT), BF16),
                            pltpu.VMEM((D_EXPERT, d), BF16),
                            pltpu.VMEM((GMM_X_SLOTS, GMM_SUB, half), I32), pltpu.SemaphoreType.DMA((GMM_X_SLOTS,)),
                            pltpu.VMEM((2, GMM_SUB, half), I32), pltpu.SemaphoreType.DMA((2,)),
                            pltpu.VMEM((GMM_W_SLOTS, d, D_EXPERT), F32), pltpu.VMEM((GMM_W_SLOTS, d, D_EXPERT), F32),
                            pltpu.VMEM((GMM_W_SLOTS, D_EXPERT, d), F32), pltpu.SemaphoreType.DMA((GMM_W_SLOTS, 3))]),
        out_shape=jax.ShapeDtypeStruct((r, half), I32),
        compiler_params=pltpu.CompilerParams(dimension_semantics=("arbitrary",), vmem_limit_bytes=VMEM_LIMIT),
        name="expert_gmm",
    )(n_chunks, chunk_e, chunk_row, chunk_valid, e_ord, e_next, x_sorted, wg, wu, wd)


def _combine_kernel(base_ref, mod_ref, routed_ref, gain_ref, *rest):
    y_ref = rest[-1]
    d = base_ref.shape[-1]
    g2 = mod_ref[...][:, :, 2 * d:3 * d]
    out = base_ref[...] + g2 * routed_ref[...]
    y_ref[...] = _rms(out) * gain_ref[...]


def _combine(base, mod, routed, gain, first_chunk, n_chunks, out_chunks, out_first_chunk, out_buf=None):
    _, c, d = base.shape
    nc = COMB_TILE // c
    t0, o0 = first_chunk // nc, out_first_chunk // nc
    chunks_per_stream = out_chunks // mod.shape[0]
    if chunks_per_stream == 1:
        mod_spec = pl.BlockSpec((nc, 1, 3 * d), lambda i: (o0 + i, 0, 0))
    else:
        assert chunks_per_stream % nc == 0
        mod_spec = pl.BlockSpec((1, 1, 3 * d), lambda i: ((o0 + i) * nc // chunks_per_stream, 0, 0))
    blk3 = pl.BlockSpec((nc, c, d), lambda i: (t0 + i, 0, 0))
    in_specs = [blk3, mod_spec, blk3, pl.BlockSpec((1, 1, d), lambda i: (0, 0, 0))]
    args = [base, mod, routed, gain.reshape(1, 1, d)]
    aliases = {}
    if out_buf is not None:
        in_specs.append(pl.BlockSpec(memory_space=pl.ANY))
        args.append(out_buf)
        aliases = {len(args) - 1: 0}
    return pl.pallas_call(
        _combine_kernel,
        grid=(n_chunks // nc,),
        in_specs=in_specs,
        out_specs=pl.BlockSpec((nc, c, d), lambda i: (o0 + i, 0, 0)),
        out_shape=jax.ShapeDtypeStruct((out_chunks, c, d), F32),
        input_output_aliases=aliases,
        compiler_params=pltpu.CompilerParams(dimension_semantics=("arbitrary",), vmem_limit_bytes=VMEM_LIMIT),
        name="combine_norm",
    )(*args)


def _rope_tables(pos):
    half = HEAD_DIM // 2
    lane = jnp.arange(LANES, dtype=I32)
    inv_freq = ROPE_THETA ** (-(lane % half).astype(F32) / half)
    sign = jnp.where((lane % HEAD_DIM) < half, -1.0, 1.0).astype(F32)
    ang = pos.astype(F32)[:, None] * inv_freq[None, :]
    return jnp.cos(ang), jnp.sin(ang) * sign[None, :]


def _routed_ffn(h2, cw, rank, counts, w_gate, w_up, w_down):
    n, half = h2.shape
    counts = counts[:, 0]
    padded = (counts + GMM_TAIL - 1) // GMM_TAIL * GMM_TAIL
    seg_start = (jnp.cumsum(padded) - padded).astype(I32)
    n_rows = n * TOP_K + N_EXPERTS * GMM_TAIL + GMM_SUB
    e_chunks = (counts + GMM_SUB - 1) // GMM_SUB
    chunk_end = jnp.cumsum(e_chunks)
    max_chunks = n * TOP_K // GMM_SUB + N_EXPERTS
    g = jnp.arange(max_chunks, dtype=I32)
    chunk_e = jnp.minimum(jnp.sum((chunk_end[None, :] <= g[:, None]).astype(I32), axis=1), N_EXPERTS - 1)
    eids = jnp.arange(N_EXPERTS, dtype=I32)
    own = chunk_e[:, None] == eids[None, :]
    pick = lambda table: jnp.sum(jnp.where(own, table[None, :], 0), axis=1)
    in_expert = (g - pick(chunk_end - e_chunks)) * GMM_SUB
    chunk_row = (pick(seg_start) + in_expert).astype(I32)
    chunk_valid = jnp.clip(pick(counts) - in_expert, 0, GMM_SUB).astype(I32)
    n_chunks = chunk_end[-1:].astype(I32)
    has_rows = counts > 0
    e_ord = (jnp.cumsum(has_rows.astype(I32)) - has_rows.astype(I32)).astype(I32)
    later = has_rows[None, :] & (eids[None, :] > eids[:, None])
    e_next = jnp.min(jnp.where(later, eids[None, :], N_EXPERTS), axis=1)
    e_next = jnp.where(e_next == N_EXPERTS, -1, e_next).astype(I32)
    pos, w_lanes, pos_tok = _slots(cw, rank, seg_start[:, None])
    x_sorted = _sc_dispatch(h2, pos, n_rows)
    y_sorted = _gmm(x_sorted, n_chunks, chunk_e, chunk_row, chunk_valid, e_ord, e_next, w_gate, w_up, w_down)
    return _sc_collect_sum(y_sorted, pos_tok.reshape(n * TOP_K), w_lanes)


def kernel(x_prompt, x_sample, cache_k, cache_v, state_conv, c_prompt, c_sample, w_ada, b_ada, w_in, w_conv,
           w_conv_out, w_attn_o, attn_sinks, w_mix_out, w_router, router_bias, w_exp_gate, w_exp_up, w_exp_down,
           w_sh_gate, w_sh_up, w_sh_down, final_gain):
    assert w_ada.shape[0] == 1, "one layer"
    bp, seq, d = x_prompt.shape
    bs, ts, _ = x_sample.shape
    assert ts == CHUNK and seq % MIX_TILE == 0 and bs % SAMPLE_BB == 0

    c_all = jnp.concatenate([c_prompt, c_sample], axis=0)
    pad = (-c_all.shape[0]) % SUBLANES
    mod = _ada(jnp.pad(c_all, ((0, pad), (0, 0))), w_ada[0], b_ada[0])[:bp + bs]
    mod_p, mod_s = mod[:bp, None, :], mod[bp:, None, :]

    head_axes = (N_KV_HEADS // 2, 2, GQA_GROUP, HEAD_DIM)
    w_in_l = w_in[0]
    w_q = w_in_l[:, OFF_Q:OFF_K].reshape((d,) + head_axes).transpose(0, 1, 3, 2, 4).reshape(d, Q_DIM)
    w_o = w_attn_o[0].reshape(head_axes + (d,)).transpose(0, 2, 1, 3, 4).reshape(Q_DIM, d)
    win = (w_in_l.astype(BF16), w_q.astype(BF16))
    wco, wao, wmo = (w.astype(BF16) for w in (w_conv_out[0], w_o, w_mix_out[0]))
    cos_p, sin_p = _rope_tables(jnp.arange(seq, dtype=I32))
    cos_s, sin_s = _rope_tables(PAST_LEN + jnp.arange(ts, dtype=I32))

    x1_p, conv_p, k_p, v_p = _mixer_prompt(x_prompt, mod_p, cos_p, sin_p, win, w_conv[0], wco, wao, attn_sinks[0], wmo)
    x1_s, conv_s, k_s, v_s = _mixer_sample(
        x_sample, mod_s, cos_s, sin_s, cache_k[0].reshape(bs, WINDOW, KV_DIM), cache_v[0].reshape(bs, WINDOW, KV_DIM),
        state_conv[0], win, w_conv[0], wco, wao, attn_sinks[0], wmo)

    n_p, n_s = bp * seq, bs * ts
    n = n_p + n_s
    mod2_p, mod2_s = mod[:bp, None, 3 * d:], mod[bp:, None, 3 * d:]
    x1_pc = x1_p.reshape(n_p // CHUNK, CHUNK, d)
    wsg, wsu, wsd = (w[0].astype(BF16) for w in (w_sh_gate, w_sh_up, w_sh_down))
    wr_t, rb = w_router[0].T, router_bias[0][:, None]
    wr_hi = wr_t.astype(BF16)
    wr_lo = (wr_t - wr_hi.astype(F32)).astype(BF16)

    ncp, ncs = n_p // CHUNK, n_s // CHUNK
    half = (ncp + ncs) * FFN_SET_A_SHARE[0] // FFN_SET_A_SHARE[1]
    tile_chunks = max(PRE_TILE, COMB_TILE) // CHUNK
    assert half <= ncp and half % tile_chunks == 0 and (ncp - half) % tile_chunks == 0 and ncs % tile_chunks == 0
    for set_tokens in (half * CHUNK, n - half * CHUNK):
        assert set_tokens % (SC_WORKERS * SC_WINDOW) == 0 and set_tokens % RANK_TILE == 0
        assert set_tokens % (SC_WORKERS * 2 * SC_SUM_TOKENS) == 0
    y_p = None
    for p0, np_c, ns_c in ((0, half, 0), (half, ncp - half, ncs)):
        h2, base, cw, rank, counts = _pre(x1_pc, p0, np_c, mod2_p, x1_s, ns_c, mod2_s, wsg, wsu, wsd, wr_hi, wr_lo, rb)
        routed = _routed_ffn(h2, cw, rank, counts, w_exp_gate[0], w_exp_up[0], w_exp_down[0]).reshape(base.shape)
        y_p = _combine(base, mod2_p, routed, final_gain, 0, np_c, ncp, p0, out_buf=y_p)
        if ns_c:
            y_s = _combine(base, mod2_s, routed, final_gain, np_c, ns_c, ncs, 0)

    kv = lambda a: a.reshape(1, a.shape[0], WINDOW, N_KV_HEADS, HEAD_DIM)
    return (y_p.reshape(bp, seq, d), y_s, conv_p[None], kv(k_p), kv(v_p), conv_s[None], kv(k_s), kv(v_s))
```

```python
import functools
import itertools

import jax
import jax.numpy as jnp
from jax import lax
from jax.experimental import pallas as pl
from jax.experimental.pallas import tpu as pltpu
from jax.experimental.pallas import tpu_sc as plsc

F32 = jnp.float32
BF16 = jnp.bfloat16
I32 = jnp.int32

D_MODEL = 1024
CHUNK = 64
D_CONV = 1024
CONV_W = 3
N_HEADS = 16
N_KV_HEADS = 4
HEAD_DIM = 64
GQA_GROUP = N_HEADS // N_KV_HEADS
WINDOW = 128
ROPE_THETA = 10000.0
ATTN_SCALE = HEAD_DIM ** -0.5
N_EXPERTS = 64
TOP_K = 8
N_EXPERT_GROUPS = 8
GROUP_SIZE = N_EXPERTS // N_EXPERT_GROUPS
TOPK_GROUPS = 4
D_EXPERT = 256
D_SHARED = 256
ROUTED_SCALE = 2.5
EPS = 1e-6
PAST_LEN = 4096
Q_DIM = N_HEADS * HEAD_DIM
KV_DIM = N_KV_HEADS * HEAD_DIM
OFF_GB, OFF_GC, OFF_XC, OFF_Q, OFF_K, OFF_V, OFF_GCONV, OFF_GATTN, D_IN = itertools.accumulate(
    (0, D_CONV, D_CONV, D_CONV, Q_DIM, KV_DIM, KV_DIM, D_MODEL, D_MODEL))

LANES = 128
SUBLANES = 8
VMEM_LIMIT = 56 * 1024 * 1024

MIX_TILE = 512
ATT_Q = 128
MIX_SIDE_COLS = 256
SAMPLE_BB = 8
PRE_TILE = 512
RANK_TILE = 512
GMM_SUB = 512
GMM_MID = 256
GMM_TAIL = 128
GMM_X_SLOTS = 4
GMM_W_SLOTS = 3
COMB_TILE = 512
FFN_SET_A_SHARE = (2, 3)
SC_SUBCORES = 16
SC_WORKERS = 32
SC_WINDOW = 96
SC_LANES = 16
SC_SUM_TOKENS = 8
SC_SUM_VREGS = 16


def _const_spec(shape):
    nd = len(shape)
    return pl.BlockSpec(shape, lambda *_: (0,) * nd, pipeline_mode=pl.Buffered(1))


def _rms(x):
    return x * lax.rsqrt(jnp.mean(x * x, axis=-1, keepdims=True) + EPS)


def _sigmoid(x):
    return 1.0 / (1.0 + jnp.exp(-x))


def _silu(x):
    return x * _sigmoid(x)


def _dot(a, b):
    return jnp.dot(a, b, preferred_element_type=F32)


def _pack_bf16_pairs(x):
    half = x.shape[-1] // 2
    lo = lax.bitcast_convert_type(x[..., :half].astype(BF16).astype(F32), I32)
    hi = lax.bitcast_convert_type(x[..., half:].astype(BF16).astype(F32), I32)
    return lax.shift_right_logical(lo, 16) | hi


def _unpack_bf16_pairs(words):
    lo = lax.bitcast_convert_type(lax.shift_left(words, 16), F32)
    hi = lax.bitcast_convert_type(words & jnp.int32(-65536), F32)
    return lo, hi


def _ada_kernel(c_ref, w_ref, b_ref, o_ref):
    s = _silu(c_ref[...]).astype(BF16)
    o_ref[...] = _dot(s, w_ref[...].astype(BF16)) + b_ref[...]


def _ada(c_all, w_ada, b_ada):
    rows = c_all.shape[0]
    n_out = w_ada.shape[1]
    bn = 768
    return pl.pallas_call(
        _ada_kernel,
        grid=(n_out // bn,),
        in_specs=[pl.BlockSpec((rows, D_MODEL), lambda i: (0, 0)),
                  pl.BlockSpec((D_MODEL, bn), lambda i: (0, i)),
                  pl.BlockSpec((1, bn), lambda i: (0, i))],
        out_specs=pl.BlockSpec((rows, bn), lambda i: (0, i)),
        out_shape=jax.ShapeDtypeStruct((rows, n_out), F32),
        name="ada_mod",
    )(c_all, w_ada, b_ada.reshape(1, n_out))


def _rope(x, cos, sin_signed):
    lane = lax.broadcasted_iota(I32, (x.shape[0], LANES), 1)
    first_half = (lane % HEAD_DIM) < (HEAD_DIM // 2)
    outs = []
    for g in range(x.shape[1] // LANES):
        xg = x[:, g * LANES:(g + 1) * LANES]
        up = pltpu.roll(xg, LANES - HEAD_DIM // 2, axis=1)
        down = pltpu.roll(xg, HEAD_DIM // 2, axis=1)
        partner = jnp.where(first_half, up, down)
        outs.append(xg * cos + partner * sin_signed)
    return jnp.concatenate(outs, axis=1)


def _attention(blocks, sinks_ref, obuf, between):
    rq = GQA_GROUP * ATT_Q
    low = lax.broadcasted_iota(I32, (ATT_Q, LANES), 1) < HEAD_DIM
    head_of_lane = lax.broadcasted_iota(I32, (1, rq), 1) // ATT_Q
    units = [(b, pair, par) for b in range(len(blocks)) for pair in range(N_KV_HEADS // 2) for par in range(2)]
    loaded = {}

    def scores(u):
        b, pair, par = units[u]
        if b not in loaded:
            loaded.clear()
            loaded[b] = blocks[b]()
        q_blk, k_of_pair, _, mask, _ = loaded[b]
        keep = low if par == 0 else jnp.logical_not(low)
        cols = [q_blk[:, (GQA_GROUP * pair + i) * LANES:(GQA_GROUP * pair + i + 1) * LANES] for i in range(GQA_GROUP)]
        qg = jnp.concatenate([jnp.where(keep, c, jnp.zeros_like(c)) for c in cols], axis=0)
        st = lax.dot_general(k_of_pair(pair), qg, (((1,), (1,)), ((), ())), preferred_element_type=F32)
        vt = loaded[b][2](pair)[par * HEAD_DIM:(par + 1) * HEAD_DIM, :]
        return jnp.where(mask, st, -jnp.inf), vt, loaded[b][4]

    outs = []
    nxt = scores(0)
    for u, (b, pair, par) in enumerate(units):
        st, vt, row0 = nxt
        if u + 1 < len(units):
            nxt = scores(u + 1)
        if between:
            between.pop(0)()
        g = 2 * pair + par
        sink = jnp.full((1, rq), sinks_ref[g * GQA_GROUP + GQA_GROUP - 1], F32)
        for i in range(GQA_GROUP - 2, -1, -1):
            sink = jnp.where(head_of_lane == i, sinks_ref[g * GQA_GROUP + i], sink)
        m = jnp.maximum(jnp.max(st, axis=0, keepdims=True), sink)
        e = jnp.exp(st - m)
        z = jnp.sum(e, axis=0, keepdims=True) + jnp.exp(sink - m)
        outs.append(_dot(vt, e.astype(BF16)) / z)
        if par == 1:
            for i in range(GQA_GROUP):
                blk = jnp.concatenate([o[:, i * ATT_Q:(i + 1) * ATT_Q] for o in outs], axis=0)
                c0 = (GQA_GROUP * pair + i) * LANES
                obuf[row0:row0 + ATT_Q, c0:c0 + LANES] = blk.T
            outs = []
    for step in between:
        step()


def _in_proj(hb, win_refs, lo, hi):
    w_all, w_q = win_refs
    if (lo, hi) == (OFF_Q, OFF_K):
        return _dot(hb, w_q[...])
    assert hi <= OFF_Q or lo >= OFF_K
    return _dot(hb, w_all[:, lo:hi])


def _attention_free_steps(hb, conv, win_ref, wco_ref):
    out, parts = {}, {}
    n_parts = D_MODEL // MIX_SIDE_COLS

    def step(name, piece, compute):
        def run():
            parts.setdefault(name, []).append(compute(piece * MIX_SIDE_COLS, (piece + 1) * MIX_SIDE_COLS))
            if piece == n_parts - 1:
                out[name] = jnp.concatenate(parts.pop(name), axis=1)
        return run

    def conv_in():
        if "conv_in" not in out:
            out["conv_in"] = (out.pop("gate_b") * conv).astype(BF16)
        return out["conv_in"]

    computes = [("gate_b", lambda lo, hi: _in_proj(hb, win_ref, OFF_GB + lo, OFF_GB + hi)),
                ("g_conv", lambda lo, hi: _in_proj(hb, win_ref, OFF_GCONV + lo, OFF_GCONV + hi)),
                ("g_attn", lambda lo, hi: _in_proj(hb, win_ref, OFF_GATTN + lo, OFF_GATTN + hi)),
                ("y_conv", lambda lo, hi: _dot(conv_in(), wco_ref[:, lo:hi]))]
    return [step(name, p, fn) for name, fn in computes for p in range(n_parts)], out


def _mix_out(x, g1, side, y_attn_in, wao_ref, wmo_ref):
    y_attn = _dot(y_attn_in.astype(BF16), wao_ref[...])
    merged = _sigmoid(side["g_conv"]) * side["y_conv"] + _sigmoid(side["g_attn"]) * y_attn
    return x + g1 * _dot(merged.astype(BF16), wmo_ref[...])


def _mixer_prompt_kernel(x_ref, mod_ref, cos_ref, sin_ref, wall_ref, wq_ref, wconv_ref, wco_ref, wao_ref,
                         sinks_ref, wmo_ref, x1_ref, conv_ref, k_ref, v_ref, ubuf, kbuf, vtbuf, obuf):
    win_ref = (wall_ref, wq_ref)
    j = pl.program_id(1)
    t = x_ref.shape[1]

    @pl.when(j == 0)
    def _():
        ubuf[0:SUBLANES, :] = jnp.zeros((SUBLANES, D_CONV), F32)
        kbuf[0:WINDOW, :] = jnp.zeros((WINDOW, KV_DIM), BF16)
        vtbuf[:, 0:WINDOW] = jnp.zeros((KV_DIM, WINDOW), BF16)

    x = x_ref[0]
    mod = mod_ref[0]
    sh1, sc1, g1 = mod[:, 0:D_MODEL], mod[:, D_MODEL:2 * D_MODEL], mod[:, 2 * D_MODEL:3 * D_MODEL]
    hb = (_rms(x) * (1.0 + sc1) + sh1).astype(BF16)

    u = _in_proj(hb, win_ref, OFF_GC, OFF_XC) * _in_proj(hb, win_ref, OFF_XC, OFF_Q)
    ubuf[SUBLANES:SUBLANES + t, :] = u
    wc = wconv_ref[...]
    conv = wc[0:1] * ubuf[SUBLANES - 2:SUBLANES - 2 + t, :] + wc[1:2] * ubuf[SUBLANES - 1:SUBLANES - 1 + t, :] + wc[2:3] * u
    conv_ref[0] = u[t - (CONV_W - 1):t]
    ubuf[SUBLANES - 2:SUBLANES, :] = u[t - (CONV_W - 1):t]

    cos, sin = cos_ref[...], sin_ref[...]
    q = (_rope(_in_proj(hb, win_ref, OFF_Q, OFF_K), cos, sin) * ATTN_SCALE).astype(BF16)
    k = _rope(_in_proj(hb, win_ref, OFF_K, OFF_V), cos, sin)
    v = _in_proj(hb, win_ref, OFF_V, OFF_GCONV)
    kbuf[WINDOW:WINDOW + t, :] = k.astype(BF16)
    vtbuf[:, WINDOW:WINDOW + t] = v.T.astype(BF16)
    k_ref[0] = k[t - WINDOW:t]
    v_ref[0] = v[t - WINDOW:t]

    nkeys = ATT_Q + WINDOW
    rq = GQA_GROUP * ATT_Q
    ki = lax.broadcasted_iota(I32, (nkeys, rq), 0)
    qi = lax.broadcasted_iota(I32, (nkeys, rq), 1) % ATT_Q
    band = ki // CHUNK - qi // CHUNK
    band_ok = (band >= 0) & (band <= WINDOW // CHUNK)
    def block(s):
        def load():
            mask = band_ok & (ki + (j * t + s * ATT_Q - WINDOW) >= 0)
            k_of_pair = lambda pair: kbuf[s * ATT_Q:s * ATT_Q + nkeys, pair * LANES:(pair + 1) * LANES]
            vt_of_pair = lambda pair: vtbuf[pair * LANES:(pair + 1) * LANES, s * ATT_Q:s * ATT_Q + nkeys]
            return q[s * ATT_Q:(s + 1) * ATT_Q], k_of_pair, vt_of_pair, mask, s * ATT_Q
        return load

    steps, side = _attention_free_steps(hb, conv, win_ref, wco_ref)
    _attention([block(s) for s in range(t // ATT_Q)], sinks_ref, obuf, steps)
    kbuf[0:WINDOW, :] = kbuf[t:t + WINDOW, :]
    vtbuf[:, 0:WINDOW] = vtbuf[:, t:t + WINDOW]

    x1_ref[0] = _mix_out(x, g1, side, obuf[...], wao_ref, wmo_ref)


def _mixer_prompt(x, mod, cos, sin, win, wconv, wco, wao, sinks, wmo):
    b, seq, d = x.shape
    t = MIX_TILE
    return pl.pallas_call(
        _mixer_prompt_kernel,
        grid=(b, seq // t),
        in_specs=[pl.BlockSpec((1, t, d), lambda i, j: (i, j, 0)),
                  pl.BlockSpec((1, 1, 6 * d), lambda i, j: (i, 0, 0)),
                  pl.BlockSpec((t, LANES), lambda i, j: (j, 0)),
                  pl.BlockSpec((t, LANES), lambda i, j: (j, 0)),
                  *[_const_spec(w.shape) for w in win],
                  _const_spec(wconv.shape), _const_spec(wco.shape), _const_spec(wao.shape),
                  pl.BlockSpec(memory_space=pltpu.SMEM),
                  _const_spec(wmo.shape)],
        out_specs=[pl.BlockSpec((1, t, d), lambda i, j: (i, j, 0)),
                   pl.BlockSpec((1, CONV_W - 1, D_CONV), lambda i, j: (i, 0, 0)),
                   pl.BlockSpec((1, WINDOW, KV_DIM), lambda i, j: (i, 0, 0)),
                   pl.BlockSpec((1, WINDOW, KV_DIM), lambda i, j: (i, 0, 0))],
        out_shape=[jax.ShapeDtypeStruct((b, seq, d), F32),
                   jax.ShapeDtypeStruct((b, CONV_W - 1, D_CONV), F32),
                   jax.ShapeDtypeStruct((b, WINDOW, KV_DIM), F32),
                   jax.ShapeDtypeStruct((b, WINDOW, KV_DIM), F32)],
        scratch_shapes=[pltpu.VMEM((SUBLANES + t, D_CONV), F32),
                        pltpu.VMEM((WINDOW + t, KV_DIM), BF16),
                        pltpu.VMEM((KV_DIM, WINDOW + t), BF16),
                        pltpu.VMEM((t, Q_DIM), F32)],
        compiler_params=pltpu.CompilerParams(dimension_semantics=("arbitrary", "arbitrary"),
                                             vmem_limit_bytes=VMEM_LIMIT),
        name="mixer_prompt",
    )(x, mod, cos, sin, *win, wconv, wco, wao, sinks, wmo)


def _mixer_sample_kernel(x_ref, mod_ref, cos_ref, sin_ref, ck_ref, cv_ref, sconv_ref, wall_ref, wq_ref,
                         wconv_ref, wco_ref, wao_ref, sinks_ref, wmo_ref, x1_ref, conv_ref, k_ref, v_ref, ubuf, obuf):
    win_ref = (wall_ref, wq_ref)
    bb, t, d = x_ref.shape
    x3 = x_ref[...]
    mod = mod_ref[...]
    sh1, sc1, g1 = mod[:, :, 0:d], mod[:, :, d:2 * d], mod[:, :, 2 * d:3 * d]
    x = x3.reshape(bb * t, d)
    hb = (_rms(x3) * (1.0 + sc1) + sh1).astype(BF16).reshape(bb * t, d)

    u = _in_proj(hb, win_ref, OFF_GC, OFF_XC) * _in_proj(hb, win_ref, OFF_XC, OFF_Q)
    u3 = u.reshape(bb, t, D_CONV)
    ubuf[:, SUBLANES - 2:SUBLANES, :] = sconv_ref[...]
    ubuf[:, SUBLANES:SUBLANES + t, :] = u3
    wc = wconv_ref[...]
    conv = (wc[0:1] * ubuf[:, SUBLANES - 2:SUBLANES - 2 + t, :] + wc[1:2] * ubuf[:, SUBLANES - 1:SUBLANES - 1 + t, :]
            + wc[2:3] * u3).reshape(bb * t, D_CONV)
    conv_ref[...] = u3[:, t - (CONV_W - 1):t, :]

    cos = jnp.concatenate([cos_ref[...]] * bb, axis=0)
    sin = jnp.concatenate([sin_ref[...]] * bb, axis=0)
    q = (_rope(_in_proj(hb, win_ref, OFF_Q, OFF_K), cos, sin) * ATTN_SCALE).astype(BF16)
    k = _rope(_in_proj(hb, win_ref, OFF_K, OFF_V), cos, sin)
    v = _in_proj(hb, win_ref, OFF_V, OFF_GCONV)
    per = ATT_Q // t
    nkeys = per * (WINDOW + t)
    rq = GQA_GROUP * ATT_Q
    key_stream = lax.broadcasted_iota(I32, (nkeys, rq), 0) // (WINDOW + t)
    query_stream = (lax.broadcasted_iota(I32, (nkeys, rq), 1) % ATT_Q) // t
    mask = key_stream == query_stream
    def block(blk):
        def load():
            k_parts, v_parts = [], []
            for b in range(blk * per, (blk + 1) * per):
                kb, vb = k[b * t:(b + 1) * t], v[b * t:(b + 1) * t]
                ck, cv = ck_ref[b], cv_ref[b]
                k_ref[b] = jnp.concatenate([ck[t:WINDOW], kb], axis=0)
                v_ref[b] = jnp.concatenate([cv[t:WINDOW], vb], axis=0)
                k_parts += [ck, kb]
                v_parts += [cv, vb]
            k_all = jnp.concatenate(k_parts, axis=0).astype(BF16)
            vt_all = jnp.concatenate(v_parts, axis=0).T.astype(BF16)
            k_of_pair = lambda pair: k_all[:, pair * LANES:(pair + 1) * LANES]
            vt_of_pair = lambda pair: vt_all[pair * LANES:(pair + 1) * LANES, :]
            return q[blk * ATT_Q:(blk + 1) * ATT_Q], k_of_pair, vt_of_pair, mask, blk * ATT_Q
        return load

    steps, side = _attention_free_steps(hb, conv, win_ref, wco_ref)
    _attention([block(blk) for blk in range(bb // per)], sinks_ref, obuf, steps)

    g1f = jnp.broadcast_to(g1, (bb, t, d)).reshape(bb * t, d)
    x1_ref[...] = _mix_out(x, g1f, side, obuf[...], wao_ref, wmo_ref).reshape(bb, t, d)


def _mixer_sample(x, mod, cos, sin, ck, cv, sconv, win, wconv, wco, wao, sinks, wmo):
    b, t, d = x.shape
    bb = SAMPLE_BB
    blk = lambda *s: pl.BlockSpec((bb,) + s, lambda i: (i, 0, 0))
    return pl.pallas_call(
        _mixer_sample_kernel,
        grid=(b // bb,),
        in_specs=[blk(t, d), blk(1, 6 * d),
                  pl.BlockSpec((t, LANES), lambda i: (0, 0)), pl.BlockSpec((t, LANES), lambda i: (0, 0)),
                  blk(WINDOW, KV_DIM), blk(WINDOW, KV_DIM), blk(CONV_W - 1, D_CONV),
                  *[_const_spec(w.shape) for w in win],
                  _const_spec(wconv.shape), _const_spec(wco.shape), _const_spec(wao.shape),
                  pl.BlockSpec(memory_space=pltpu.SMEM),
                  _const_spec(wmo.shape)],
        out_specs=[blk(t, d), blk(CONV_W - 1, D_CONV), blk(WINDOW, KV_DIM), blk(WINDOW, KV_DIM)],
        out_shape=[jax.ShapeDtypeStruct((b, t, d), F32),
                   jax.ShapeDtypeStruct((b, CONV_W - 1, D_CONV), F32),
                   jax.ShapeDtypeStruct((b, WINDOW, KV_DIM), F32),
                   jax.ShapeDtypeStruct((b, WINDOW, KV_DIM), F32)],
        scratch_shapes=[pltpu.VMEM((bb, SUBLANES + t, D_CONV), F32),
                        pltpu.VMEM((bb * t, Q_DIM), F32)],
        compiler_params=pltpu.CompilerParams(dimension_semantics=("arbitrary",), vmem_limit_bytes=VMEM_LIMIT),
        name="mixer_sample",
    )(x, mod, cos, sin, ck, cv, sconv, *win, wconv, wco, wao, sinks, wmo)


def _pre_kernel(*refs, prompt_tiles, has_sample):
    if has_sample:
        xp_ref, mp_ref, xs_ref, ms_ref, *refs = refs
    else:
        xp_ref, mp_ref, *refs = refs
    wsg_ref, wsu_ref, wsd_ref, wrh_ref, wrl_ref, rb_ref, h2_ref, base_ref, cw_ref, rank_ref, cnt_ref, carry = refs
    nc, c, d = xp_ref.shape
    t = nc * c
    x3, mod = xp_ref[...], mp_ref[...]
    if has_sample:
        is_prompt = pl.program_id(0) < prompt_tiles
        x3 = jnp.where(is_prompt, x3, xs_ref[...])
        mod = jnp.where(is_prompt, mod, ms_ref[...])
    sh2, sc2, g2 = mod[:, :, 0:d], mod[:, :, d:2 * d], mod[:, :, 2 * d:3 * d]
    h3 = _rms(x3) * (1.0 + sc2) + sh2
    h2 = h3.reshape(t, d)
    hb = h2.astype(BF16)
    h2_ref[...] = _pack_bf16_pairs(h2)
    shared = _dot((_silu(_dot(hb, wsg_ref[...])) * _dot(hb, wsu_ref[...])).astype(BF16), wsd_ref[...])
    base_ref[...] = x3 + g2 * shared.reshape(nc, c, d)

    h_lo = (h2 - hb.astype(F32)).astype(BF16)
    nt = lambda a, b: lax.dot_general(a, b, (((1,), (1,)), ((), ())), preferred_element_type=F32)
    logits = nt(wrh_ref[...], hb) + (nt(wrh_ref[...], h_lo) + nt(wrl_ref[...], hb))
    scores = _sigmoid(logits)
    biased = scores + rb_ref[...]
    g3 = biased.reshape(N_EXPERT_GROUPS, GROUP_SIZE, t)
    member = lax.broadcasted_iota(I32, g3.shape, 1)
    m1 = jnp.max(g3, axis=1, keepdims=True)
    first = jnp.min(jnp.where(g3 == m1, member, GROUP_SIZE), axis=1, keepdims=True)
    m2 = jnp.max(jnp.where(member == first, -jnp.inf, g3), axis=1, keepdims=True)
    gs = m1 + m2
    gidx = lax.broadcasted_iota(I32, gs.shape, 0)
    grank = jnp.zeros(gs.shape, I32)
    for o in range(N_EXPERT_GROUPS):
        other = gs[o:o + 1]
        grank += ((other > gs) | ((other == gs) & (o < gidx))).astype(I32)
    group_ok = grank < TOPK_GROUPS
    slot = jnp.zeros((1, 1, t), I32)
    takes = []
    for gi in range(N_EXPERT_GROUPS):
        ok = group_ok[gi:gi + 1]
        takes.append([ok & (slot == s) for s in range(TOPK_GROUPS)])
        slot = slot + ok.astype(I32)
    packed = []
    for s in range(TOPK_GROUPS):
        vals = jnp.zeros((GROUP_SIZE, t), F32)
        for gi in range(N_EXPERT_GROUPS):
            vals = jnp.where(takes[gi][s][0], g3[gi], vals)
        packed.append(vals)
    cand = jnp.concatenate(packed, axis=0)
    cidx = lax.broadcasted_iota(I32, cand.shape, 0)
    crank = jnp.zeros(cand.shape, I32)
    for o in range(TOPK_GROUPS * GROUP_SIZE):
        other = cand[o:o + 1]
        crank += ((other > cand) | ((other == cand) & (o < cidx))).astype(I32)
    chosen = crank < TOP_K
    sel_groups = []
    for gi in range(N_EXPERT_GROUPS):
        hit = jnp.zeros((GROUP_SIZE, t), jnp.bool_)
        for s in range(TOPK_GROUPS):
            hit = hit | (takes[gi][s][0] & chosen[s * GROUP_SIZE:(s + 1) * GROUP_SIZE])
        sel_groups.append(hit)
    sel = jnp.concatenate(sel_groups, axis=0)
    ssum = jnp.sum(jnp.where(sel, scores, 0.0), axis=0, keepdims=True)
    cw_ref[...] = jnp.where(sel, scores / ssum * ROUTED_SCALE, -1.0)

    @pl.when(pl.program_id(0) == 0)
    def _():
        carry[...] = jnp.zeros(carry.shape, F32)

    picked = sel.astype(BF16)
    earlier = (lax.broadcasted_iota(I32, (t, t), 0) < lax.broadcasted_iota(I32, (t, t), 1)).astype(BF16)
    rank_ref[...] = (carry[...] + _dot(picked, earlier)).astype(I32)
    carry[...] = carry[...] + jnp.sum(picked.astype(F32), axis=1, keepdims=True)
    cnt_ref[...] = carry[...].astype(I32)


def _pre(x1_p, p_chunk0, ncp, mod_p, x1_s, ncs, mod_s, wsg, wsu, wsd, wr_hi, wr_lo, rb):
    ncp_all, c, d = x1_p.shape
    nc = PRE_TILE // c
    nchunks = ncp + ncs
    n = nchunks * c
    pt, p0 = ncp // nc, p_chunk0 // nc
    tiles_per_stream = ncp_all // mod_p.shape[0] // nc
    blk3 = pl.BlockSpec((nc, c, d), lambda i: (i, 0, 0))
    p_tile = lambda i: p0 + jnp.minimum(i, pt - 1)
    s_tile = lambda i: jnp.maximum(i - pt, 0)
    s_args, s_specs = [], []
    if ncs:
        s_args = [x1_s, mod_s]
        s_specs = [pl.BlockSpec((nc, c, d), lambda i: (s_tile(i), 0, 0)),
                   pl.BlockSpec((nc, 1, 3 * d), lambda i: (s_tile(i), 0, 0))]
    return pl.pallas_call(
        functools.partial(_pre_kernel, prompt_tiles=pt, has_sample=bool(ncs)),
        grid=(nchunks // nc,),
        in_specs=[pl.BlockSpec((nc, c, d), lambda i: (p_tile(i), 0, 0)),
                  pl.BlockSpec((1, 1, 3 * d), lambda i: (p_tile(i) // tiles_per_stream, 0, 0))] + s_specs + [
                  _const_spec(wsg.shape), _const_spec(wsu.shape), _const_spec(wsd.shape),
                  _const_spec(wr_hi.shape), _const_spec(wr_lo.shape), _const_spec(rb.shape)],
        out_specs=[pl.BlockSpec((nc * c, d // 2), lambda i: (i, 0)), blk3,
                   pl.BlockSpec((N_EXPERTS, nc * c), lambda i: (0, i)),
                   pl.BlockSpec((N_EXPERTS, nc * c), lambda i: (0, i)),
                   pl.BlockSpec((N_EXPERTS, 1), lambda i: (0, 0))],
        out_shape=[jax.ShapeDtypeStruct((n, d // 2), I32),
                   jax.ShapeDtypeStruct((nchunks, c, d), F32),
                   jax.ShapeDtypeStruct((N_EXPERTS, n), F32),
                   jax.ShapeDtypeStruct((N_EXPERTS, n), I32),
                   jax.ShapeDtypeStruct((N_EXPERTS, 1), I32)],
        scratch_shapes=[pltpu.VMEM((N_EXPERTS, 1), F32)],
        compiler_params=pltpu.CompilerParams(dimension_semantics=("arbitrary",), vmem_limit_bytes=VMEM_LIMIT),
        name="pre_ffn",
    )(x1_p, mod_p, *s_args, wsg, wsu, wsd, wr_hi, wr_lo, rb)


def _slot_kernel(cw_ref, rank_ref, start_ref, pos_ref, w_ref, pos_tok_ref):
    cw = cw_ref[...]
    e, t = cw.shape
    sel = cw >= 0.0
    r = lax.broadcasted_iota(I32, (e, e), 0)
    c = lax.broadcasted_iota(I32, (e, e), 1)
    lower = (c < r).astype(BF16)
    kidx = _dot(lower, sel.astype(BF16))
    posf = start_ref[...].astype(F32) + rank_ref[...].astype(F32)
    pos_rows, w_rows = [], []
    for k in range(TOP_K):
        m = sel & (kidx == float(k))
        pos_rows.append(jnp.sum(jnp.where(m, posf, 0.0), axis=0, keepdims=True))
        w_rows.append(jnp.sum(jnp.where(m, cw, 0.0), axis=0, keepdims=True))
    pos_ref[...] = jnp.concatenate(pos_rows, axis=0).astype(I32)
    w_ref[...] = jnp.concatenate([jnp.broadcast_to(w, (SC_LANES, t)) for w in w_rows], axis=0).T
    pos_pad = jnp.concatenate(pos_rows + [jnp.zeros((LANES - TOP_K, t), F32)], axis=0)
    pos_tok_ref[...] = pos_pad.T[:, :TOP_K].astype(I32)


def _slots(cw, rank, seg_start):
    e, n = cw.shape
    t = RANK_TILE
    return pl.pallas_call(
        _slot_kernel,
        grid=(n // t,),
        in_specs=[pl.BlockSpec((e, t), lambda i: (0, i)), pl.BlockSpec((e, t), lambda i: (0, i)),
                  pl.BlockSpec((e, 1), lambda i: (0, 0))],
        out_specs=[pl.BlockSpec((TOP_K, t), lambda i: (0, i)), pl.BlockSpec((t, TOP_K * SC_LANES), lambda i: (i, 0)),
                   pl.BlockSpec((t, TOP_K), lambda i: (i, 0))],
        out_shape=[jax.ShapeDtypeStruct((TOP_K, n), I32), jax.ShapeDtypeStruct((n, TOP_K * SC_LANES), F32),
                   jax.ShapeDtypeStruct((n, TOP_K), I32)],
        compiler_params=pltpu.CompilerParams(dimension_semantics=("arbitrary",)),
        name="expert_slots",
    )(cw, rank, seg_start)


def _sc_mesh():
    return plsc.VectorSubcoreMesh(core_axis_name="c", subcore_axis_name="s")


def _sc_worker_id():
    return lax.axis_index("s") * (SC_WORKERS // SC_SUBCORES) + lax.axis_index("c")


def _sc_dispatch(rows, pos, n_rows):
    n, d = rows.shape
    per_w = n // SC_WORKERS
    w = SC_WINDOW
    n_chunks = per_w // w

    @functools.partial(
        pl.kernel, mesh=_sc_mesh(),
        out_type=jax.ShapeDtypeStruct((n_rows, d), rows.dtype),
        scratch_types=[pltpu.VMEM((2, TOP_K, w), I32), pltpu.VMEM((2, w, d), rows.dtype),
                       pltpu.SemaphoreType.DMA((2,)), pltpu.SemaphoreType.DMA((2,)), pltpu.SemaphoreType.DMA((2,))],
        name="sc_dispatch")
    def k(rows_hbm, pos_hbm, o_hbm, idx_v, rows_v, row_sem, idx_sem, out_sem):
        wid = _sc_worker_id()
        base = wid * per_w

        def loads(c, slot):
            off = pl.multiple_of(base + c * w, SUBLANES)
            return (pltpu.make_async_copy(rows_hbm.at[pl.ds(off, w)], rows_v.at[slot], row_sem.at[slot]),
                    pltpu.make_async_copy(pos_hbm.at[wid * n_chunks + c], idx_v.at[slot], idx_sem.at[slot]))

        def scatters(slot):
            return [pltpu.make_async_copy(rows_v.at[slot], o_hbm.at[idx_v.at[slot, kk]], out_sem.at[slot])
                    for kk in range(TOP_K)]

        for cp in loads(0, 0):
            cp.start()
        for c in range(n_chunks):
            slot = c % 2
            for cp in loads(c, slot):
                cp.wait()
            for cp in scatters(slot):
                cp.start()
            if c >= 1:
                for cp in scatters(1 - slot):
                    cp.wait()
            if c + 1 < n_chunks:
                for cp in loads(c + 1, 1 - slot):
                    cp.start()
        for cp in scatters((n_chunks - 1) % 2):
            cp.wait()

    pos_chunks = pos.reshape(TOP_K, n // w, w).transpose(1, 0, 2)
    return k(rows, pos_chunks)


def _sc_collect_sum(rows, pos_tok, w_lanes):
    words = rows.shape[1]
    n = w_lanes.shape[0]
    lanes = SC_LANES
    per_w = n // SC_WORKERS
    tw = SC_SUM_TOKENS
    n_pairs = per_w // (2 * tw)
    col_blocks = words // lanes // SC_SUM_VREGS

    @functools.partial(
        pl.kernel, mesh=_sc_mesh(),
        out_type=jax.ShapeDtypeStruct((n, 2 * words), F32),
        scratch_types=[pltpu.VMEM((per_w * TOP_K,), I32), pltpu.VMEM((2, tw * TOP_K, words), I32),
                       pltpu.VMEM((2, tw, TOP_K * lanes), F32), pltpu.VMEM((2, tw, 2 * words), F32),
                       pltpu.SemaphoreType.DMA((2,)), pltpu.SemaphoreType.DMA((2,)), pltpu.SemaphoreType.DMA((2,))],
        compiler_params=pltpu.CompilerParams(needs_layout_passes=False),
        name="sc_collect_sum")
    def k(rows_hbm, pos_hbm, w_hbm, o_hbm, idx_v, rows_v, w_v, out_v, in_sem, w_sem, out_sem):
        base = pl.multiple_of(_sc_worker_id() * per_w, SUBLANES)
        pltpu.sync_copy(pos_hbm.at[pl.ds(pl.multiple_of(base * TOP_K, SUBLANES), per_w * TOP_K)], idx_v)

        def loads(c, slot):
            idx = idx_v.at[pl.ds(pl.multiple_of(c * tw * TOP_K, SUBLANES), tw * TOP_K)]
            tok0 = pl.multiple_of(base + c * tw, SUBLANES)
            return (pltpu.make_async_copy(rows_hbm.at[idx], rows_v.at[slot], in_sem.at[slot]),
                    pltpu.make_async_copy(w_hbm.at[pl.ds(tok0, tw)], w_v.at[slot], w_sem.at[slot]))

        def write(c, slot):
            tok0 = pl.multiple_of(base + c * tw, SUBLANES)
            return pltpu.make_async_copy(out_v.at[slot], o_hbm.at[pl.ds(tok0, tw)], out_sem.at[slot])

        high_half = jnp.full((lanes,), -65536, I32)
        sixteen = jnp.full((lanes,), 16, I32)

        def reduce_window(slot):
            rv, wv, ov = rows_v.at[slot], w_v.at[slot], out_v.at[slot]

            @pl.loop(0, tw)
            def _(t):
                for cb in range(col_blocks):
                    acc_lo, acc_hi = [None] * SC_SUM_VREGS, [None] * SC_SUM_VREGS
                    for kk in range(TOP_K):
                        wk = wv[t, pl.ds(kk * lanes, lanes)]
                        for c in range(SC_SUM_VREGS):
                            wd = rv[t * TOP_K + kk, pl.ds((cb * SC_SUM_VREGS + c) * lanes, lanes)]
                            lo = wk * plsc.bitcast(lax.shift_left(wd, sixteen), F32)
                            hi = wk * plsc.bitcast(wd & high_half, F32)
                            acc_lo[c] = lo if kk == 0 else acc_lo[c] + lo
                            acc_hi[c] = hi if kk == 0 else acc_hi[c] + hi
                    for c in range(SC_SUM_VREGS):
                        col = (cb * SC_SUM_VREGS + c) * lanes
                        ov[t, pl.ds(col, lanes)] = acc_lo[c]
                        ov[t, pl.ds(words + col, lanes)] = acc_hi[c]

        for cp in loads(0, 0):
            cp.start()

        @pl.loop(0, n_pairs)
        def _(p):
            c0 = 2 * p
            for cp in loads(c0 + 1, 1):
                cp.start()
            for cp in loads(c0, 0):
                cp.wait()
            reduce_window(0)
            write(c0, 0).start()
            for cp in loads(c0 + 1, 1):
                cp.wait()
            reduce_window(1)
            write(c0 + 1, 1).start()
            write(c0, 0).wait()

            @pl.when(p + 1 < n_pairs)
            def _():
                for cp in loads(c0 + 2, 0):
                    cp.start()

            write(c0 + 1, 1).wait()

    return k(rows, pos_tok, w_lanes)


def _gmm_kernel(tot_ref, ce_ref, row_ref, val_ref, ord_ref, nxt_ref, x_hbm, wg_hbm, wu_hbm, wd_hbm, y_hbm,
                wgb, wub, wdb, xbuf, xsem, ybuf, ysem, wgf, wuf, wdf, wsem):
    total = tot_ref[0]
    pieces = GMM_SUB // GMM_TAIL

    def w_copies(ex, slot):
        return [pltpu.make_async_copy(src.at[ex], dst.at[slot], wsem.at[slot, i])
                for i, (src, dst) in enumerate(((wg_hbm, wgf), (wu_hbm, wuf), (wd_hbm, wdf)))]

    def x_copy(g):
        slot = g % GMM_X_SLOTS
        rows = pl.ds(pl.multiple_of(row_ref[g], GMM_TAIL), GMM_SUB)
        return pltpu.make_async_copy(x_hbm.at[rows], xbuf.at[slot], xsem.at[slot])

    def y_piece(g, p):
        slot = g % 2
        rows = pl.ds(pl.multiple_of(row_ref[g] + p * GMM_TAIL, GMM_TAIL), GMM_TAIL)
        return pltpu.make_async_copy(ybuf.at[slot, pl.ds(p * GMM_TAIL, GMM_TAIL)], y_hbm.at[rows], ysem.at[slot])

    def for_y_pieces(g, action):
        for p in range(pieces):
            @pl.when(p * GMM_TAIL < val_ref[g])
            def _():
                action(y_piece(g, p), p)

    for ahead in range(GMM_X_SLOTS - 1):
        @pl.when(ahead < total)
        def _():
            x_copy(ahead).start()

    def chunk(g, carry):
        e = ce_ref[g]
        prev = ce_ref[jnp.maximum(g - 1, 0)]

        @pl.when((g == 0) | (e != prev))
        def _():
            slot = ord_ref[e] % GMM_W_SLOTS

            def start_ahead(ex, hops):
                for _ in range(hops):
                    ex = jnp.where(ex >= 0, nxt_ref[jnp.maximum(ex, 0)], -1)

                @pl.when(ex >= 0)
                def _():
                    for cp in w_copies(ex, ord_ref[ex] % GMM_W_SLOTS):
                        cp.start()

            @pl.when(g == 0)
            def _():
                for hops in range(GMM_W_SLOTS - 1):
                    start_ahead(e, hops)

            for cp in w_copies(e, slot):
                cp.wait()
            wgb[...] = wgf[slot].astype(BF16)
            wub[...] = wuf[slot].astype(BF16)
            wdb[...] = wdf[slot].astype(BF16)
            start_ahead(e, GMM_W_SLOTS - 1)

        x_copy(g).wait()

        @pl.when(g + GMM_X_SLOTS - 1 < total)
        def _():
            x_copy(g + GMM_X_SLOTS - 1).start()

        @pl.when(g >= 2)
        def _():
            for_y_pieces(g - 2, lambda cp, p: cp.wait())

        x_ref = xbuf.at[g % GMM_X_SLOTS]
        y_ref = ybuf.at[g % 2]

        def expert_rows(r0, n):
            rows = pl.ds(r0, n)
            lo, hi = _unpack_bf16_pairs(x_ref[rows, :])
            xb = jnp.concatenate([lo.astype(BF16), hi.astype(BF16)], axis=1)
            mid = (_silu(_dot(xb, wgb[...])) * _dot(xb, wub[...])).astype(BF16)
            y_ref[rows, :] = _pack_bf16_pairs(_dot(mid, wdb[...]))

        n_real = val_ref[g]

        @pl.when(n_real == GMM_SUB)
        def _():
            expert_rows(0, GMM_SUB)

        @pl.when((n_real < GMM_SUB) & (n_real >= GMM_MID))
        def _():
            expert_rows(0, GMM_MID)

        @pl.when(n_real < GMM_SUB)
        def _():
            done = jnp.where(n_real >= GMM_MID, GMM_MID, 0)

            @pl.loop(0, (n_real - done + GMM_TAIL - 1) // GMM_TAIL)
            def _(i):
                expert_rows(pl.multiple_of(done + i * GMM_TAIL, GMM_TAIL), GMM_TAIL)

        for_y_pieces(g, lambda cp, p: cp.start(priority=p % 2))
        return carry

    lax.fori_loop(0, total, chunk, 0)
    for back in (2, 1):
        @pl.when(total >= back)
        def _():
            for_y_pieces(total - back, lambda cp, p: cp.wait())


def _gmm(x_sorted, n_chunks, chunk_e, chunk_row, chunk_valid, e_ord, e_next, wg, wu, wd):
    r, half = x_sorted.shape
    d = 2 * half
    any_spec = pl.BlockSpec(memory_space=pl.ANY)
    return pl.pallas_call(
        _gmm_kernel,
        grid_spec=pltpu.PrefetchScalarGridSpec(
            num_scalar_prefetch=6,
            grid=(1,),
            in_specs=[any_spec, any_spec, any_spec, any_spec],
            out_specs=any_spec,
            scratch_shapes=[pltpu.VMEM((d, D_EXPERT), BF16), pltpu.VMEM((d, D_EXPERT), BF16),
                            pltpu.VMEM((D_EXPERT, d), BF16),
                            pltpu.VMEM((GMM_X_SLOTS, GMM_SUB, half), I32), pltpu.SemaphoreType.DMA((GMM_X_SLOTS,)),
                            pltpu.VMEM((2, GMM_SUB, half), I32), pltpu.SemaphoreType.DMA((2,)),
                            pltpu.VMEM((GMM_W_SLOTS, d, D_EXPERT), F32), pltpu.VMEM((GMM_W_SLOTS, d, D_EXPERT), F32),
                            pltpu.VMEM((GMM_W_SLOTS, D_EXPERT, d), F32), pltpu.SemaphoreType.DMA((GMM_W_SLOTS, 3))]),
        out_shape=jax.ShapeDtypeStruct((r, half), I32),
        compiler_params=pltpu.CompilerParams(dimension_semantics=("arbitrary",), vmem_limit_bytes=VMEM_LIMIT),
        name="expert_gmm",
    )(n_chunks, chunk_e, chunk_row, chunk_valid, e_ord, e_next, x_sorted, wg, wu, wd)


def _combine_kernel(base_ref, mod_ref, routed_ref, gain_ref, *rest):
    y_ref = rest[-1]
    d = base_ref.shape[-1]
    g2 = mod_ref[...][:, :, 2 * d:3 * d]
    out = base_ref[...] + g2 * routed_ref[...]
    y_ref[...] = _rms(out) * gain_ref[...]


def _combine(base, mod, routed, gain, first_chunk, n_chunks, out_chunks, out_first_chunk, out_buf=None):
    _, c, d = base.shape
    nc = COMB_TILE // c
    t0, o0 = first_chunk // nc, out_first_chunk // nc
    chunks_per_stream = out_chunks // mod.shape[0]
    if chunks_per_stream == 1:
        mod_spec = pl.BlockSpec((nc, 1, 3 * d), lambda i: (o0 + i, 0, 0))
    else:
        assert chunks_per_stream % nc == 0
        mod_spec = pl.BlockSpec((1, 1, 3 * d), lambda i: ((o0 + i) * nc // chunks_per_stream, 0, 0))
    blk3 = pl.BlockSpec((nc, c, d), lambda i: (t0 + i, 0, 0))
    in_specs = [blk3, mod_spec, blk3, pl.BlockSpec((1, 1, d), lambda i: (0, 0, 0))]
    args = [base, mod, routed, gain.reshape(1, 1, d)]
    aliases = {}
    if out_buf is not None:
        in_specs.append(pl.BlockSpec(memory_space=pl.ANY))
        args.append(out_buf)
        aliases = {len(args) - 1: 0}
    return pl.pallas_call(
        _combine_kernel,
        grid=(n_chunks // nc,),
        in_specs=in_specs,
        out_specs=pl.BlockSpec((nc, c, d), lambda i: (o0 + i, 0, 0)),
        out_shape=jax.ShapeDtypeStruct((out_chunks, c, d), F32),
        input_output_aliases=aliases,
        compiler_params=pltpu.CompilerParams(dimension_semantics=("arbitrary",), vmem_limit_bytes=VMEM_LIMIT),
        name="combine_norm",
    )(*args)


def _rope_tables(pos):
    half = HEAD_DIM // 2
    lane = jnp.arange(LANES, dtype=I32)
    inv_freq = ROPE_THETA ** (-(lane % half).astype(F32) / half)
    sign = jnp.where((lane % HEAD_DIM) < half, -1.0, 1.0).astype(F32)
    ang = pos.astype(F32)[:, None] * inv_freq[None, :]
    return jnp.cos(ang), jnp.sin(ang) * sign[None, :]


def _routed_ffn(h2, cw, rank, counts, w_gate, w_up, w_down):
    n, half = h2.shape
    counts = counts[:, 0]
    padded = (counts + GMM_TAIL - 1) // GMM_TAIL * GMM_TAIL
    seg_start = (jnp.cumsum(padded) - padded).astype(I32)
    n_rows = n * TOP_K + N_EXPERTS * GMM_TAIL + GMM_SUB
    e_chunks = (counts + GMM_SUB - 1) // GMM_SUB
    chunk_end = jnp.cumsum(e_chunks)
    max_chunks = n * TOP_K // GMM_SUB + N_EXPERTS
    g = jnp.arange(max_chunks, dtype=I32)
    chunk_e = jnp.minimum(jnp.sum((chunk_end[None, :] <= g[:, None]).astype(I32), axis=1), N_EXPERTS - 1)
    eids = jnp.arange(N_EXPERTS, dtype=I32)
    own = chunk_e[:, None] == eids[None, :]
    pick = lambda table: jnp.sum(jnp.where(own, table[None, :], 0), axis=1)
    in_expert = (g - pick(chunk_end - e_chunks)) * GMM_SUB
    chunk_row = (pick(seg_start) + in_expert).astype(I32)
    chunk_valid = jnp.clip(pick(counts) - in_expert, 0, GMM_SUB).astype(I32)
    n_chunks = chunk_end[-1:].astype(I32)
    has_rows = counts > 0
    e_ord = (jnp.cumsum(has_rows.astype(I32)) - has_rows.astype(I32)).astype(I32)
    later = has_rows[None, :] & (eids[None, :] > eids[:, None])
    e_next = jnp.min(jnp.where(later, eids[None, :], N_EXPERTS), axis=1)
    e_next = jnp.where(e_next == N_EXPERTS, -1, e_next).astype(I32)
    pos, w_lanes, pos_tok = _slots(cw, rank, seg_start[:, None])
    x_sorted = _sc_dispatch(h2, pos, n_rows)
    y_sorted = _gmm(x_sorted, n_chunks, chunk_e, chunk_row, chunk_valid, e_ord, e_next, w_gate, w_up, w_down)
    return _sc_collect_sum(y_sorted, pos_tok.reshape(n * TOP_K), w_lanes)


def kernel(x_prompt, x_sample, cache_k, cache_v, state_conv, c_prompt, c_sample, w_ada, b_ada, w_in, w_conv,
           w_conv_out, w_attn_o, attn_sinks, w_mix_out, w_router, router_bias, w_exp_gate, w_exp_up, w_exp_down,
           w_sh_gate, w_sh_up, w_sh_down, final_gain):
    assert w_ada.shape[0] == 1, "one layer"
    bp, seq, d = x_prompt.shape
    bs, ts, _ = x_sample.shape
    assert ts == CHUNK and seq % MIX_TILE == 0 and bs % SAMPLE_BB == 0

    c_all = jnp.concatenate([c_prompt, c_sample], axis=0)
    pad = (-c_all.shape[0]) % SUBLANES
    mod = _ada(jnp.pad(c_all, ((0, pad), (0, 0))), w_ada[0], b_ada[0])[:bp + bs]
    mod_p, mod_s = mod[:bp, None, :], mod[bp:, None, :]

    head_axes = (N_KV_HEADS // 2, 2, GQA_GROUP, HEAD_DIM)
    w_in_l = w_in[0]
    w_q = w_in_l[:, OFF_Q:OFF_K].reshape((d,) + head_axes).transpose(0, 1, 3, 2, 4).reshape(d, Q_DIM)
    w_o = w_attn_o[0].reshape(head_axes + (d,)).transpose(0, 2, 1, 3, 4).reshape(Q_DIM, d)
    win = (w_in_l.astype(BF16), w_q.astype(BF16))
    wco, wao, wmo = (w.astype(BF16) for w in (w_conv_out[0], w_o, w_mix_out[0]))
    cos_p, sin_p = _rope_tables(jnp.arange(seq, dtype=I32))
    cos_s, sin_s = _rope_tables(PAST_LEN + jnp.arange(ts, dtype=I32))

    x1_p, conv_p, k_p, v_p = _mixer_prompt(x_prompt, mod_p, cos_p, sin_p, win, w_conv[0], wco, wao, attn_sinks[0], wmo)
    x1_s, conv_s, k_s, v_s = _mixer_sample(
        x_sample, mod_s, cos_s, sin_s, cache_k[0].reshape(bs, WINDOW, KV_DIM), cache_v[0].reshape(bs, WINDOW, KV_DIM),
        state_conv[0], win, w_conv[0], wco, wao, attn_sinks[0], wmo)

    n_p, n_s = bp * seq, bs * ts
    n = n_p + n_s
    mod2_p, mod2_s = mod[:bp, None, 3 * d:], mod[bp:, None, 3 * d:]
    x1_pc = x1_p.reshape(n_p // CHUNK, CHUNK, d)
    wsg, wsu, wsd = (w[0].astype(BF16) for w in (w_sh_gate, w_sh_up, w_sh_down))
    wr_t, rb = w_router[0].T, router_bias[0][:, None]
    wr_hi = wr_t.astype(BF16)
    wr_lo = (wr_t - wr_hi.astype(F32)).astype(BF16)

    ncp, ncs = n_p // CHUNK, n_s // CHUNK
    half = (ncp + ncs) * FFN_SET_A_SHARE[0] // FFN_SET_A_SHARE[1]
    tile_chunks = max(PRE_TILE, COMB_TILE) // CHUNK
    assert half <= ncp and half % tile_chunks == 0 and (ncp - half) % tile_chunks == 0 and ncs % tile_chunks == 0
    for set_tokens in (half * CHUNK, n - half * CHUNK):
        assert set_tokens % (SC_WORKERS * SC_WINDOW) == 0 and set_tokens % RANK_TILE == 0
        assert set_tokens % (SC_WORKERS * 2 * SC_SUM_TOKENS) == 0
    y_p = None
    for p0, np_c, ns_c in ((0, half, 0), (half, ncp - half, ncs)):
        h2, base, cw, rank, counts = _pre(x1_pc, p0, np_c, mod2_p, x1_s, ns_c, mod2_s, wsg, wsu, wsd, wr_hi, wr_lo, rb)
        routed = _routed_ffn(h2, cw, rank, counts, w_exp_gate[0], w_exp_up[0], w_exp_down[0]).reshape(base.shape)
        y_p = _combine(base, mod2_p, routed, final_gain, 0, np_c, ncp, p0, out_buf=y_p)
        if ns_c:
            y_s = _combine(base, mod2_s, routed, final_gain, np_c, ns_c, ncs, 0)

    kv = lambda a: a.reshape(1, a.shape[0], WINDOW, N_KV_HEADS, HEAD_DIM)
    return (y_p.reshape(bp, seq, d), y_s, conv_p[None], kv(k_p), kv(v_p), conv_s[None], kv(k_s), kv(v_s))
```
